```python
import jax, jax.numpy as jnp
from jax import lax
import numpy as np

D_MODEL = 1024
BATCH = 8
SEQ = 8192
DEPTH = 2

D_FF = 2816
SSD_EXPAND = 2
SSD_D_INNER = SSD_EXPAND * D_MODEL
SSD_HEAD_DIM = 64
SSD_HEADS = SSD_D_INNER // SSD_HEAD_DIM
SSD_GROUPS = 4
SSD_HPG = SSD_HEADS // SSD_GROUPS
SSD_STATE = 128
SSD_CONV = 4
SSD_CHUNK = 128
SSD_CONV_DIM = SSD_D_INNER + 2 * SSD_GROUPS * SSD_STATE
MLA_HEADS = 8
MLA_Q_LORA = 512
MLA_KV_LORA = 256
MLA_NOPE = 128
MLA_ROPE = 64
MLA_V = 128
MLA_QK = MLA_NOPE + MLA_ROPE
ATTN_BLOCK = 128
ROPE_THETA = 10000.0
EPS = 1e-6
IN_SPLIT_SIZES = (SSD_D_INNER, SSD_CONV_DIM, SSD_HEADS, MLA_Q_LORA, MLA_KV_LORA, MLA_ROPE, 2 * D_MODEL)
D_IN_PROJ = SSD_D_INNER + SSD_CONV_DIM + SSD_HEADS + MLA_Q_LORA + MLA_KV_LORA + MLA_ROPE + 2 * D_MODEL

kernel_name = "macaron_gated_ssd_mla_hybrid"


def rms_norm(x, g):
    xf = x.astype(jnp.float32)
    y = xf * lax.rsqrt(jnp.mean(xf * xf, axis=-1, keepdims=True) + EPS)
    return (y * g.astype(jnp.float32)).astype(x.dtype)


def swiglu(x, w13, w2):
    gu = x @ w13
    gate, up = gu[..., :D_FF], gu[..., D_FF:]
    return (jax.nn.silu(gate) * up) @ w2


def split_cols(t, sizes):
    out, start = [], 0
    for n in sizes:
        out.append(t[..., start:start + n])
        start += n
    return out


def rope_tables(positions):
    inv = 1.0 / (ROPE_THETA ** (jnp.arange(0, MLA_ROPE, 2, dtype=jnp.float32) / MLA_ROPE))
    ang = positions.astype(jnp.float32)[..., None] * inv
    return jnp.cos(ang), jnp.sin(ang)


def apply_rope(t, cos, sin):
    half = MLA_ROPE // 2
    tf = t.astype(jnp.float32)
    t1, t2 = tf[..., :half], tf[..., half:]
    c, s = cos[:, :, None, :], sin[:, :, None, :]
    return jnp.concatenate([t1 * c - t2 * s, t2 * c + t1 * s], axis=-1).astype(t.dtype)


def causal_depthwise_conv(t, w, b):
    y = lax.conv_general_dilated(
        t, w[:, None, :].astype(t.dtype), window_strides=(1,), padding=[(SSD_CONV - 1, 0)],
        dimension_numbers=('NWC', 'WIO', 'NWC'), feature_group_count=t.shape[-1])
    return y + b.astype(t.dtype)


def ssd_chunked_scan(xh, dt, a, bmat, cmat):
    bsz, s = xh.shape[0], xh.shape[1]
    nc = s // SSD_CHUNK

    def chunks(t):
        t = t.reshape((bsz, nc, SSD_CHUNK) + t.shape[2:])
        return jnp.moveaxis(t, 1, 0)

    x_c = chunks(xh.astype(jnp.float32).reshape(bsz, s, SSD_GROUPS, SSD_HPG, SSD_HEAD_DIM))
    dt_c = chunks(dt.reshape(bsz, s, SSD_GROUPS, SSD_HPG))
    b_c = chunks(bmat.astype(jnp.float32))
    c_c = chunks(cmat.astype(jnp.float32))
    a_g = a.reshape(SSD_GROUPS, SSD_HPG)
    causal = jnp.tril(jnp.ones((SSD_CHUNK, SSD_CHUNK), dtype=bool))[None, :, :, None, None]

    def step(state, inp):
        xc, dtc, bc, cc = inp
        acum = jnp.cumsum(dtc * a_g, axis=1)
        seg = acum[:, :, None] - acum[:, None, :]
        decay = jnp.exp(jnp.where(causal, seg, -jnp.inf))
        cb = jnp.einsum('btgn,bsgn->btsg', cc, bc)
        xdt = xc * dtc[..., None]
        y_diag = jnp.einsum('btsg,btsgh,bsghp->btghp', cb, decay, xdt)
        y_off = jnp.einsum('btgn,bghpn->btghp', cc, state) * jnp.exp(acum)[..., None]
        last = acum[:, -1]
        w_state = jnp.exp(last[:, None] - acum)
        new_state = (state * jnp.exp(last)[..., None, None]
                     + jnp.einsum('bsgn,bsgh,bsghp->bghpn', bc, w_state, xdt))
        return new_state, y_diag + y_off

    state0 = jnp.zeros((bsz, SSD_GROUPS, SSD_HPG, SSD_HEAD_DIM, SSD_STATE), jnp.float32)
    _, y = lax.scan(step, state0, (x_c, dt_c, b_c, c_c))
    return jnp.moveaxis(y, 0, 1).reshape(bsz, s, SSD_HEADS, SSD_HEAD_DIM)


def ssd_branch(z, xbc, dt_raw, conv_w, conv_b, dt_bias, a_log, d_skip, norm_g, w_out):
    bsz, s, _ = z.shape
    xbc = jax.nn.silu(causal_depthwise_conv(xbc, conv_w, conv_b))
    xs, bm, cm = split_cols(xbc, (SSD_D_INNER, SSD_GROUPS * SSD_STATE, SSD_GROUPS * SSD_STATE))
    xh = xs.reshape(bsz, s, SSD_HEADS, SSD_HEAD_DIM)
    bm = bm.reshape(bsz, s, SSD_GROUPS, SSD_STATE)
    cm = cm.reshape(bsz, s, SSD_GROUPS, SSD_STATE)
    dt = jax.nn.softplus(dt_raw.astype(jnp.float32) + dt_bias.astype(jnp.float32))
    a = -jnp.exp(a_log.astype(jnp.float32))
    y = ssd_chunked_scan(xh, dt, a, bm, cm) + xh.astype(jnp.float32) * d_skip.astype(jnp.float32)[:, None]
    y = y.reshape(bsz, s, SSD_D_INNER).astype(z.dtype)
    y = rms_norm(y * jax.nn.silu(z), norm_g)
    return y @ w_out


def causal_block_attention(q, k, v):
    bsz, s, h, dq = q.shape
    nb = s // ATTN_BLOCK
    scale = dq ** -0.5
    qb = jnp.moveaxis(q.reshape(bsz, nb, ATTN_BLOCK, h, dq), 1, 0)
    kpos = jnp.arange(s)

    def one_block(args):
        i, qi = args
        sc = jnp.einsum('bqhd,bkhd->bhqk', qi, k, preferred_element_type=jnp.float32) * scale
        qpos = i * ATTN_BLOCK + jnp.arange(ATTN_BLOCK)
        sc = jnp.where(kpos[None, :] <= qpos[:, None], sc, -jnp.inf)
        p = jax.nn.softmax(sc, axis=-1)
        return jnp.einsum('bhqk,bkhd->bqhd', p.astype(v.dtype), v)

    o = lax.map(one_block, (jnp.arange(nb), qb))
    return jnp.moveaxis(o, 0, 1).reshape(bsz, s, h, v.shape[-1])


def mla_branch(cq, ckv, kr, cos, sin, q_lora_g, w_uq, kv_lora_g, w_ukv, q_norm_g, k_norm_g, w_out):
    bsz, s, _ = cq.shape
    q = (rms_norm(cq, q_lora_g) @ w_uq).reshape(bsz, s, MLA_HEADS, MLA_QK)
    kv = (rms_norm(ckv, kv_lora_g) @ w_ukv).reshape(bsz, s, MLA_HEADS, MLA_NOPE + MLA_V)
    k_nope, v = kv[..., :MLA_NOPE], kv[..., MLA_NOPE:]
    k_pe = jnp.broadcast_to(kr[:, :, None, :], (bsz, s, MLA_HEADS, MLA_ROPE))
    k = jnp.concatenate([k_nope, k_pe], axis=-1)
    q = rms_norm(q, q_norm_g)
    k = rms_norm(k, k_norm_g)
    q = jnp.concatenate([q[..., :MLA_NOPE], apply_rope(q[..., MLA_NOPE:], cos, sin)], axis=-1)
    k = jnp.concatenate([k[..., :MLA_NOPE], apply_rope(k[..., MLA_NOPE:], cos, sin)], axis=-1)
    o = causal_block_attention(q, k, v)
    return o.reshape(bsz, s, MLA_HEADS * MLA_V) @ w_out


def _fwd_setup_inputs(seed: int = 0) -> dict:
    key = jax.random.key(seed)
    ks = jax.random.split(key, 32)
    f32 = jnp.float32

    def nrm(k, shape, fan_in):
        return jax.random.normal(k, shape, f32) * (fan_in ** -0.5)

    def gain(k, n):
        return 1.0 + 0.02 * jax.random.normal(k, (DEPTH, n), f32)

    dt0 = jnp.exp(jax.random.uniform(ks[10], (DEPTH, SSD_HEADS), f32) * (np.log(0.1) - np.log(0.001)) + np.log(0.001))
    dt_bias = dt0 + jnp.log(-jnp.expm1(-dt0))
    a_log = jnp.log(jax.random.uniform(ks[11], (DEPTH, SSD_HEADS), f32, 1.0, 16.0))
    return {
        "x": jax.random.normal(ks[0], (BATCH, SEQ, D_MODEL), f32),
        "positions": jnp.broadcast_to(jnp.arange(SEQ, dtype=jnp.int32)[None, :], (BATCH, SEQ)),
        "ln_ffn1": gain(ks[1], D_MODEL),
        "ffn1_w13": nrm(ks[2], (DEPTH, D_MODEL, 2 * D_FF), D_MODEL),
        "ffn1_w2": nrm(ks[3], (DEPTH, D_FF, D_MODEL), D_FF),
        "ln_mix": gain(ks[4], D_MODEL),
        "w_in": nrm(ks[5], (DEPTH, D_MODEL, D_IN_PROJ), D_MODEL),
        "conv_w": nrm(ks[6], (DEPTH, SSD_CONV, SSD_CONV_DIM), SSD_CONV),
        "conv_b": 0.02 * jax.random.normal(ks[7], (DEPTH, SSD_CONV_DIM), f32),
        "dt_bias": dt_bias,
        "a_log": a_log,
        "d_skip": 1.0 + 0.1 * jax.random.normal(ks[12], (DEPTH, SSD_HEADS), f32),
        "ssd_norm": gain(ks[13], SSD_D_INNER),
        "w_ssd_out": nrm(ks[14], (DEPTH, SSD_D_INNER, D_MODEL), SSD_D_INNER),
        "q_lora_norm": gain(ks[15], MLA_Q_LORA),
        "w_uq": nrm(ks[16], (DEPTH, MLA_Q_LORA, MLA_HEADS * MLA_QK), MLA_Q_LORA),
        "kv_lora_norm": gain(ks[17], MLA_KV_LORA),
        "w_ukv": nrm(ks[18], (DEPTH, MLA_KV_LORA, MLA_HEADS * (MLA_NOPE + MLA_V)), MLA_KV_LORA),
        "q_norm": gain(ks[19], MLA_QK),
        "k_norm": gain(ks[20], MLA_QK),
        "w_mla_out": nrm(ks[21], (DEPTH, MLA_HEADS * MLA_V, D_MODEL), MLA_HEADS * MLA_V),
        "w_o": nrm(ks[22], (DEPTH, D_MODEL, D_MODEL), D_MODEL),
        "ln_ffn2": gain(ks[23], D_MODEL),
        "ffn2_w13": nrm(ks[24], (DEPTH, D_MODEL, 2 * D_FF), D_MODEL),
        "ffn2_w2": nrm(ks[25], (DEPTH, D_FF, D_MODEL), D_FF),
    }


def _fwd_reference(x, positions, ln_ffn1, ffn1_w13, ffn1_w2, ln_mix, w_in, conv_w, conv_b, dt_bias,
              a_log, d_skip, ssd_norm, w_ssd_out, q_lora_norm, w_uq, kv_lora_norm, w_ukv,
              q_norm, k_norm, w_mla_out, w_o, ln_ffn2, ffn2_w13, ffn2_w2):
    cos, sin = rope_tables(positions)
    h = x
    for l in range(DEPTH):
        h = h + 0.5 * swiglu(rms_norm(h, ln_ffn1[l]), ffn1_w13[l], ffn1_w2[l])
        u = rms_norm(h, ln_mix[l])
        z, xbc, dt_raw, cq, ckv, kr, gates = split_cols(u @ w_in[l], IN_SPLIT_SIZES)
        y_ssd = ssd_branch(z, xbc, dt_raw, conv_w[l], conv_b[l], dt_bias[l], a_log[l], d_skip[l],
                           ssd_norm[l], w_ssd_out[l])
        y_mla = mla_branch(cq, ckv, kr, cos, sin, q_lora_norm[l], w_uq[l], kv_lora_norm[l], w_ukv[l],
                           q_norm[l], k_norm[l], w_mla_out[l])
        g = jax.nn.sigmoid(gates.astype(jnp.float32)).astype(h.dtype)
        merged = g[..., :D_MODEL] * y_ssd + g[..., D_MODEL:] * y_mla
        h = h + merged @ w_o[l]
        h = h + 0.5 * swiglu(rms_norm(h, ln_ffn2[l]), ffn2_w13[l], ffn2_w2[l])
    return h


import jax as _jax
import jax.numpy as _jnp

TWIN_FORMAT = 'train_step'
FWD_PARAMS = ['x', 'positions', 'ln_ffn1', 'ffn1_w13', 'ffn1_w2', 'ln_mix', 'w_in', 'conv_w', 'conv_b', 'dt_bias', 'a_log', 'd_skip', 'ssd_norm', 'w_ssd_out', 'q_lora_norm', 'w_uq', 'kv_lora_norm', 'w_ukv', 'q_norm', 'k_norm', 'w_mla_out', 'w_o', 'ln_ffn2', 'ffn2_w13', 'ffn2_w2']
TWIN_WEIGHTS = ['ln_ffn1', 'ffn1_w13', 'ffn1_w2', 'ln_mix', 'w_in', 'conv_w', 'conv_b', 'dt_bias', 'a_log', 'd_skip', 'ssd_norm', 'w_ssd_out', 'q_lora_norm', 'w_uq', 'kv_lora_norm', 'w_ukv', 'q_norm', 'k_norm', 'w_mla_out', 'w_o', 'ln_ffn2', 'ffn2_w13', 'ffn2_w2']
TWIN_DIFF_INPUT = 'x'
TWIN_INPUTS = ['x', 'positions', 'ln_ffn1', 'ffn1_w13', 'ffn1_w2', 'ln_mix', 'w_in', 'conv_w', 'conv_b', 'dt_bias', 'a_log', 'd_skip', 'ssd_norm', 'w_ssd_out', 'q_lora_norm', 'w_uq', 'kv_lora_norm', 'w_ukv', 'q_norm', 'k_norm', 'w_mla_out', 'w_o', 'ln_ffn2', 'ffn2_w13', 'ffn2_w2', 'loss_target', 'm_ln_ffn1', 'm_ffn1_w13', 'm_ffn1_w2', 'm_ln_mix', 'm_w_in', 'm_conv_w', 'm_conv_b', 'm_dt_bias', 'm_a_log', 'm_d_skip', 'm_ssd_norm', 'm_w_ssd_out', 'm_q_lora_norm', 'm_w_uq', 'm_kv_lora_norm', 'm_w_ukv', 'm_q_norm', 'm_k_norm', 'm_w_mla_out', 'm_w_o', 'm_ln_ffn2', 'm_ffn2_w13', 'm_ffn2_w2', 'v_ln_ffn1', 'v_ffn1_w13', 'v_ffn1_w2', 'v_ln_mix', 'v_w_in', 'v_conv_w', 'v_conv_b', 'v_dt_bias', 'v_a_log', 'v_d_skip', 'v_ssd_norm', 'v_w_ssd_out', 'v_q_lora_norm', 'v_w_uq', 'v_kv_lora_norm', 'v_w_ukv', 'v_q_norm', 'v_k_norm', 'v_w_mla_out', 'v_w_o', 'v_ln_ffn2', 'v_ffn2_w13', 'v_ffn2_w2']
TWIN_OUTPUTS = ['loss', 'grad_x', 'grad_ln_ffn1', 'grad_ffn1_w13', 'grad_ffn1_w2', 'grad_ln_mix', 'grad_w_in', 'grad_conv_w', 'grad_conv_b', 'grad_dt_bias', 'grad_a_log', 'grad_d_skip', 'grad_ssd_norm', 'grad_w_ssd_out', 'grad_q_lora_norm', 'grad_w_uq', 'grad_kv_lora_norm', 'grad_w_ukv', 'grad_q_norm', 'grad_k_norm', 'grad_w_mla_out', 'grad_w_o', 'grad_ln_ffn2', 'grad_ffn2_w13', 'grad_ffn2_w2', 'delta_ln_ffn1', 'delta_ffn1_w13', 'delta_ffn1_w2', 'delta_ln_mix', 'delta_w_in', 'delta_conv_w', 'delta_conv_b', 'delta_dt_bias', 'delta_a_log', 'delta_d_skip', 'delta_ssd_norm', 'delta_w_ssd_out', 'delta_q_lora_norm', 'delta_w_uq', 'delta_kv_lora_norm', 'delta_w_ukv', 'delta_q_norm', 'delta_k_norm', 'delta_w_mla_out', 'delta_w_o', 'delta_ln_ffn2', 'delta_ffn2_w13', 'delta_ffn2_w2', 'new_m_ln_ffn1', 'new_m_ffn1_w13', 'new_m_ffn1_w2', 'new_m_ln_mix', 'new_m_w_in', 'new_m_conv_w', 'new_m_conv_b', 'new_m_dt_bias', 'new_m_a_log', 'new_m_d_skip', 'new_m_ssd_norm', 'new_m_w_ssd_out', 'new_m_q_lora_norm', 'new_m_w_uq', 'new_m_kv_lora_norm', 'new_m_w_ukv', 'new_m_q_norm', 'new_m_k_norm', 'new_m_w_mla_out', 'new_m_w_o', 'new_m_ln_ffn2', 'new_m_ffn2_w13', 'new_m_ffn2_w2', 'new_v_ln_ffn1', 'new_v_ffn1_w13', 'new_v_ffn1_w2', 'new_v_ln_mix', 'new_v_w_in', 'new_v_conv_w', 'new_v_conv_b', 'new_v_dt_bias', 'new_v_a_log', 'new_v_d_skip', 'new_v_ssd_norm', 'new_v_w_ssd_out', 'new_v_q_lora_norm', 'new_v_w_uq', 'new_v_kv_lora_norm', 'new_v_w_ukv', 'new_v_q_norm', 'new_v_k_norm', 'new_v_w_mla_out', 'new_v_w_o', 'new_v_ln_ffn2', 'new_v_ffn2_w13', 'new_v_ffn2_w2']
TWIN_LEAF_KINDS = {'loss': 'loss', 'grad_x': 'grad_x', 'grad_ln_ffn1': 'grad_w', 'grad_ffn1_w13': 'grad_w', 'grad_ffn1_w2': 'grad_w', 'grad_ln_mix': 'grad_w', 'grad_w_in': 'grad_w', 'grad_conv_w': 'grad_w', 'grad_conv_b': 'grad_w', 'grad_dt_bias': 'grad_w', 'grad_a_log': 'grad_w', 'grad_d_skip': 'grad_w', 'grad_ssd_norm': 'grad_w', 'grad_w_ssd_out': 'grad_w', 'grad_q_lora_norm': 'grad_w', 'grad_w_uq': 'grad_w', 'grad_kv_lora_norm': 'grad_w', 'grad_w_ukv': 'grad_w', 'grad_q_norm': 'grad_w', 'grad_k_norm': 'grad_w', 'grad_w_mla_out': 'grad_w', 'grad_w_o': 'grad_w', 'grad_ln_ffn2': 'grad_w', 'grad_ffn2_w13': 'grad_w', 'grad_ffn2_w2': 'grad_w', 'delta_ln_ffn1': 'delta_w', 'delta_ffn1_w13': 'delta_w', 'delta_ffn1_w2': 'delta_w', 'delta_ln_mix': 'delta_w', 'delta_w_in': 'delta_w', 'delta_conv_w': 'delta_w', 'delta_conv_b': 'delta_w', 'delta_dt_bias': 'delta_w', 'delta_a_log': 'delta_w', 'delta_d_skip': 'delta_w', 'delta_ssd_norm': 'delta_w', 'delta_w_ssd_out': 'delta_w', 'delta_q_lora_norm': 'delta_w', 'delta_w_uq': 'delta_w', 'delta_kv_lora_norm': 'delta_w', 'delta_w_ukv': 'delta_w', 'delta_q_norm': 'delta_w', 'delta_k_norm': 'delta_w', 'delta_w_mla_out': 'delta_w', 'delta_w_o': 'delta_w', 'delta_ln_ffn2': 'delta_w', 'delta_ffn2_w13': 'delta_w', 'delta_ffn2_w2': 'delta_w', 'new_m_ln_ffn1': 'new_m', 'new_m_ffn1_w13': 'new_m', 'new_m_ffn1_w2': 'new_m', 'new_m_ln_mix': 'new_m', 'new_m_w_in': 'new_m', 'new_m_conv_w': 'new_m', 'new_m_conv_b': 'new_m', 'new_m_dt_bias': 'new_m', 'new_m_a_log': 'new_m', 'new_m_d_skip': 'new_m', 'new_m_ssd_norm': 'new_m', 'new_m_w_ssd_out': 'new_m', 'new_m_q_lora_norm': 'new_m', 'new_m_w_uq': 'new_m', 'new_m_kv_lora_norm': 'new_m', 'new_m_w_ukv': 'new_m', 'new_m_q_norm': 'new_m', 'new_m_k_norm': 'new_m', 'new_m_w_mla_out': 'new_m', 'new_m_w_o': 'new_m', 'new_m_ln_ffn2': 'new_m', 'new_m_ffn2_w13': 'new_m', 'new_m_ffn2_w2': 'new_m', 'new_v_ln_ffn1': 'new_v', 'new_v_ffn1_w13': 'new_v', 'new_v_ffn1_w2': 'new_v', 'new_v_ln_mix': 'new_v', 'new_v_w_in': 'new_v', 'new_v_conv_w': 'new_v', 'new_v_conv_b': 'new_v', 'new_v_dt_bias': 'new_v', 'new_v_a_log': 'new_v', 'new_v_d_skip': 'new_v', 'new_v_ssd_norm': 'new_v', 'new_v_w_ssd_out': 'new_v', 'new_v_q_lora_norm': 'new_v', 'new_v_w_uq': 'new_v', 'new_v_kv_lora_norm': 'new_v', 'new_v_w_ukv': 'new_v', 'new_v_q_norm': 'new_v', 'new_v_k_norm': 'new_v', 'new_v_w_mla_out': 'new_v', 'new_v_w_o': 'new_v', 'new_v_ln_ffn2': 'new_v', 'new_v_ffn2_w13': 'new_v', 'new_v_ffn2_w2': 'new_v'}


def _forward(args):
    return _fwd_reference(*[args[k] for k in FWD_PARAMS])


def _output_shape():
    def fwd():
        inp = _fwd_setup_inputs(0)
        return _fwd_reference(*[inp[k] for k in FWD_PARAMS])
    out = _jax.eval_shape(fwd)
    return out.shape, out.dtype

N_MICROBATCH = 1
ADAM_LR = 0.001
ADAM_B1 = 0.9
ADAM_B2 = 0.999
ADAM_EPS = 1e-08
ADAM_WD = 0.01
ADAM_STEP = 10
PER_EXAMPLE_BATCH_AXIS = {'x': 0, 'positions': 0, 'loss_target': 0}
SHARED_INPUTS = []
_WEIGHT_DTYPES = {'ln_ffn1': _jnp.float32, 'ffn1_w13': _jnp.float32, 'ffn1_w2': _jnp.float32, 'ln_mix': _jnp.float32, 'w_in': _jnp.float32, 'conv_w': _jnp.float32, 'conv_b': _jnp.float32, 'dt_bias': _jnp.float32, 'a_log': _jnp.float32, 'd_skip': _jnp.float32, 'ssd_norm': _jnp.float32, 'w_ssd_out': _jnp.float32, 'q_lora_norm': _jnp.float32, 'w_uq': _jnp.float32, 'kv_lora_norm': _jnp.float32, 'w_ukv': _jnp.float32, 'q_norm': _jnp.float32, 'k_norm': _jnp.float32, 'w_mla_out': _jnp.float32, 'w_o': _jnp.float32, 'ln_ffn2': _jnp.float32, 'ffn2_w13': _jnp.float32, 'ffn2_w2': _jnp.float32}
MOMENT_SCALE = {'ln_ffn1': 1.226239e+01, 'ffn1_w13': 1.596016e-01, 'ffn1_w2': 2.744627e-01, 'ln_mix': 1.704804e+00, 'w_in': 2.445744e-01, 'conv_w': 5.719712e-01, 'conv_b': 1.992388e+00, 'dt_bias': 7.338357e-01, 'a_log': 4.378333e+00, 'd_skip': 4.388036e+00, 'ssd_norm': 1.451750e+01, 'w_ssd_out': 1.426154e+00, 'q_lora_norm': 8.690267e-02, 'w_uq': 4.994461e-02, 'kv_lora_norm': 7.724072e-01, 'w_ukv': 1.244431e-01, 'q_norm': 4.956031e-01, 'k_norm': 4.957677e-01, 'w_mla_out': 1.888455e-01, 'w_o': 1.212484e+00, 'ln_ffn2': 1.232481e+01, 'ffn2_w13': 1.556605e-01, 'ffn2_w2': 2.642645e-01}


def _to_microbatches(a, axis):
    t = _jnp.moveaxis(a, axis, 0)
    t = t.reshape((N_MICROBATCH, t.shape[0] // N_MICROBATCH) + t.shape[1:])
    return _jnp.moveaxis(t, 1, axis + 1)


def setup_inputs(seed: int = 0) -> dict:
    inp = _fwd_setup_inputs(seed)
    key = _jax.random.fold_in(_jax.random.key(seed), 7919)
    shape, _ = _output_shape()
    out = dict(inp)
    out["loss_target"] = _jax.random.normal(_jax.random.fold_in(key, 0), shape, _jnp.float32)
    for i, name in enumerate(TWIN_WEIGHTS):
        w = inp[name].astype(_jnp.float32)
        if MOMENT_SCALE is None:
            s = _jnp.sqrt(_jnp.mean(_jnp.square(w)) + 1e-30)
        else:
            s = MOMENT_SCALE[name]
        km, kv = _jax.random.split(_jax.random.fold_in(key, i + 1))
        out[name] = w
        out["m_" + name] = s * _jax.random.normal(km, w.shape, _jnp.float32)
        out["v_" + name] = (s * s) * _jax.random.uniform(kv, w.shape, _jnp.float32, 0.5, 1.5)
    if N_MICROBATCH > 1:
        for name, axis in PER_EXAMPLE_BATCH_AXIS.items():
            out[name] = _to_microbatches(out[name], axis)
    return {'x': out['x'], 'positions': out['positions'], 'ln_ffn1': out['ln_ffn1'], 'ffn1_w13': out['ffn1_w13'], 'ffn1_w2': out['ffn1_w2'], 'ln_mix': out['ln_mix'], 'w_in': out['w_in'], 'conv_w': out['conv_w'], 'conv_b': out['conv_b'], 'dt_bias': out['dt_bias'], 'a_log': out['a_log'], 'd_skip': out['d_skip'], 'ssd_norm': out['ssd_norm'], 'w_ssd_out': out['w_ssd_out'], 'q_lora_norm': out['q_lora_norm'], 'w_uq': out['w_uq'], 'kv_lora_norm': out['kv_lora_norm'], 'w_ukv': out['w_ukv'], 'q_norm': out['q_norm'], 'k_norm': out['k_norm'], 'w_mla_out': out['w_mla_out'], 'w_o': out['w_o'], 'ln_ffn2': out['ln_ffn2'], 'ffn2_w13': out['ffn2_w13'], 'ffn2_w2': out['ffn2_w2'], 'loss_target': out['loss_target'], 'm_ln_ffn1': out['m_ln_ffn1'], 'm_ffn1_w13': out['m_ffn1_w13'], 'm_ffn1_w2': out['m_ffn1_w2'], 'm_ln_mix': out['m_ln_mix'], 'm_w_in': out['m_w_in'], 'm_conv_w': out['m_conv_w'], 'm_conv_b': out['m_conv_b'], 'm_dt_bias': out['m_dt_bias'], 'm_a_log': out['m_a_log'], 'm_d_skip': out['m_d_skip'], 'm_ssd_norm': out['m_ssd_norm'], 'm_w_ssd_out': out['m_w_ssd_out'], 'm_q_lora_norm': out['m_q_lora_norm'], 'm_w_uq': out['m_w_uq'], 'm_kv_lora_norm': out['m_kv_lora_norm'], 'm_w_ukv': out['m_w_ukv'], 'm_q_norm': out['m_q_norm'], 'm_k_norm': out['m_k_norm'], 'm_w_mla_out': out['m_w_mla_out'], 'm_w_o': out['m_w_o'], 'm_ln_ffn2': out['m_ln_ffn2'], 'm_ffn2_w13': out['m_ffn2_w13'], 'm_ffn2_w2': out['m_ffn2_w2'], 'v_ln_ffn1': out['v_ln_ffn1'], 'v_ffn1_w13': out['v_ffn1_w13'], 'v_ffn1_w2': out['v_ffn1_w2'], 'v_ln_mix': out['v_ln_mix'], 'v_w_in': out['v_w_in'], 'v_conv_w': out['v_conv_w'], 'v_conv_b': out['v_conv_b'], 'v_dt_bias': out['v_dt_bias'], 'v_a_log': out['v_a_log'], 'v_d_skip': out['v_d_skip'], 'v_ssd_norm': out['v_ssd_norm'], 'v_w_ssd_out': out['v_w_ssd_out'], 'v_q_lora_norm': out['v_q_lora_norm'], 'v_w_uq': out['v_w_uq'], 'v_kv_lora_norm': out['v_kv_lora_norm'], 'v_w_ukv': out['v_w_ukv'], 'v_q_norm': out['v_q_norm'], 'v_k_norm': out['v_k_norm'], 'v_w_mla_out': out['v_w_mla_out'], 'v_w_o': out['v_w_o'], 'v_ln_ffn2': out['v_ln_ffn2'], 'v_ffn2_w13': out['v_ffn2_w13'], 'v_ffn2_w2': out['v_ffn2_w2']}


def _loss(weights, diff, rest, loss_target):
    with _jax.named_scope("forward"):
        args = {**rest, TWIN_DIFF_INPUT: diff, **{k: w.astype(_WEIGHT_DTYPES[k]) for k, w in weights.items()}}
        y = _forward(args)
    with _jax.named_scope("loss_head"):
        err = _jnp.square(y.astype(_jnp.float32) - loss_target)
        return 0.5 * _jnp.sum(_jnp.mean(err, axis=-1)) if err.ndim else 0.5 * err


def _adamw(w, g, m, v):
    m = ADAM_B1 * m + (1.0 - ADAM_B1) * g
    v = ADAM_B2 * v + (1.0 - ADAM_B2) * _jnp.square(g)
    m_hat = m / (1.0 - ADAM_B1 ** ADAM_STEP)
    v_hat = v / (1.0 - ADAM_B2 ** ADAM_STEP)
    delta = -ADAM_LR * (m_hat / (_jnp.sqrt(v_hat) + ADAM_EPS) + ADAM_WD * w)
    return delta, m, v


def reference(x, positions, ln_ffn1, ffn1_w13, ffn1_w2, ln_mix, w_in, conv_w, conv_b, dt_bias, a_log, d_skip, ssd_norm, w_ssd_out, q_lora_norm, w_uq, kv_lora_norm, w_ukv, q_norm, k_norm, w_mla_out, w_o, ln_ffn2, ffn2_w13, ffn2_w2, loss_target, m_ln_ffn1, m_ffn1_w13, m_ffn1_w2, m_ln_mix, m_w_in, m_conv_w, m_conv_b, m_dt_bias, m_a_log, m_d_skip, m_ssd_norm, m_w_ssd_out, m_q_lora_norm, m_w_uq, m_kv_lora_norm, m_w_ukv, m_q_norm, m_k_norm, m_w_mla_out, m_w_o, m_ln_ffn2, m_ffn2_w13, m_ffn2_w2, v_ln_ffn1, v_ffn1_w13, v_ffn1_w2, v_ln_mix, v_w_in, v_conv_w, v_conv_b, v_dt_bias, v_a_log, v_d_skip, v_ssd_norm, v_w_ssd_out, v_q_lora_norm, v_w_uq, v_kv_lora_norm, v_w_ukv, v_q_norm, v_k_norm, v_w_mla_out, v_w_o, v_ln_ffn2, v_ffn2_w13, v_ffn2_w2):
    given = dict(x=x, positions=positions, ln_ffn1=ln_ffn1, ffn1_w13=ffn1_w13, ffn1_w2=ffn1_w2, ln_mix=ln_mix, w_in=w_in, conv_w=conv_w, conv_b=conv_b, dt_bias=dt_bias, a_log=a_log, d_skip=d_skip, ssd_norm=ssd_norm, w_ssd_out=w_ssd_out, q_lora_norm=q_lora_norm, w_uq=w_uq, kv_lora_norm=kv_lora_norm, w_ukv=w_ukv, q_norm=q_norm, k_norm=k_norm, w_mla_out=w_mla_out, w_o=w_o, ln_ffn2=ln_ffn2, ffn2_w13=ffn2_w13, ffn2_w2=ffn2_w2, loss_target=loss_target, m_ln_ffn1=m_ln_ffn1, m_ffn1_w13=m_ffn1_w13, m_ffn1_w2=m_ffn1_w2, m_ln_mix=m_ln_mix, m_w_in=m_w_in, m_conv_w=m_conv_w, m_conv_b=m_conv_b, m_dt_bias=m_dt_bias, m_a_log=m_a_log, m_d_skip=m_d_skip, m_ssd_norm=m_ssd_norm, m_w_ssd_out=m_w_ssd_out, m_q_lora_norm=m_q_lora_norm, m_w_uq=m_w_uq, m_kv_lora_norm=m_kv_lora_norm, m_w_ukv=m_w_ukv, m_q_norm=m_q_norm, m_k_norm=m_k_norm, m_w_mla_out=m_w_mla_out, m_w_o=m_w_o, m_ln_ffn2=m_ln_ffn2, m_ffn2_w13=m_ffn2_w13, m_ffn2_w2=m_ffn2_w2, v_ln_ffn1=v_ln_ffn1, v_ffn1_w13=v_ffn1_w13, v_ffn1_w2=v_ffn1_w2, v_ln_mix=v_ln_mix, v_w_in=v_w_in, v_conv_w=v_conv_w, v_conv_b=v_conv_b, v_dt_bias=v_dt_bias, v_a_log=v_a_log, v_d_skip=v_d_skip, v_ssd_norm=v_ssd_norm, v_w_ssd_out=v_w_ssd_out, v_q_lora_norm=v_q_lora_norm, v_w_uq=v_w_uq, v_kv_lora_norm=v_kv_lora_norm, v_w_ukv=v_w_ukv, v_q_norm=v_q_norm, v_k_norm=v_k_norm, v_w_mla_out=v_w_mla_out, v_w_o=v_w_o, v_ln_ffn2=v_ln_ffn2, v_ffn2_w13=v_ffn2_w13, v_ffn2_w2=v_ffn2_w2)
    weights = {n: given[n] for n in TWIN_WEIGHTS}
    shared = {n: given[n] for n in SHARED_INPUTS}
    per_example = {n: given[n] for n in ['x', 'positions']}
    grad_fn = _jax.value_and_grad(_loss, argnums=(0, 1))

    def one_microbatch(ex, loss_target):
        ex = dict(ex)
        diff = ex.pop(TWIN_DIFF_INPUT)
        return grad_fn(weights, diff, {**shared, **ex}, loss_target)

    if N_MICROBATCH == 1:
        loss, (grad_w, grad_x) = one_microbatch(per_example, given["loss_target"])
    else:
        def body(carry, xs):
            loss_sum, grad_sum = carry
            l_k, (gw_k, gx_k) = one_microbatch(xs[0], xs[1])
            with _jax.named_scope("update"):
                return (loss_sum + l_k, _jax.tree.map(_jnp.add, grad_sum, gw_k)), gx_k

        init = (_jnp.zeros((), _jnp.float32), _jax.tree.map(_jnp.zeros_like, weights))
        (loss, grad_w), grad_x = _jax.lax.scan(body, init, (per_example, given["loss_target"]))
    with _jax.named_scope("update"):
        delta_w, new_m, new_v = {}, {}, {}
        for n in TWIN_WEIGHTS:
            delta_w[n], new_m[n], new_v[n] = _adamw(weights[n], grad_w[n], given["m_" + n], given["v_" + n])
    return (loss, grad_x, *[grad_w[n] for n in TWIN_WEIGHTS], *[delta_w[n] for n in TWIN_WEIGHTS],
            *[new_m[n] for n in TWIN_WEIGHTS], *[new_v[n] for n in TWIN_WEIGHTS])
```

```python
import functools

import jax
import jax.numpy as jnp
from jax import lax
from jax.experimental import pallas as pl
from jax.experimental.pallas import tpu as pltpu

F32 = jnp.float32
BF16 = jnp.bfloat16

D_MODEL = 1024
D_FF = 2816
DEPTH = 2
SSD_DI = 2048
SSD_P = 64
SSD_H = 32
SSD_G = 4
SSD_HPG = 8
SSD_N = 128
SSD_L = 128
CONV_K = 4
CONV_DIM = 3072
MLA_H = 8
Q_LORA = 512
KV_LORA = 256
NOPE = 128
ROPE = 64
VDIM = 128
QK = 192
ROPE_THETA = 10000.0
EPS = 1e-6
IN_SPLIT = (SSD_DI, CONV_DIM, SSD_H, Q_LORA, KV_LORA, ROPE, 2 * D_MODEL)
D_IN = sum(IN_SPLIT)
D_IN_PAD = 8064
N_DEV = 8
LANE = 128
PACK_COLS = 1024

ADAM_LR = 0.001
ADAM_B1 = 0.9
ADAM_B2 = 0.999
ADAM_EPS = 1e-08
ADAM_WD = 0.01
ADAM_STEP = 10

VMEM_LIMIT = 48 * 1024 * 1024
ROW_IO_BUDGET = 8 * 1024 * 1024
NEG = -1e30

MESH_AXES = ("x", "y", "c")


def _cparams(*sem):
    return pltpu.CompilerParams(dimension_semantics=sem, vmem_limit_bytes=VMEM_LIMIT)


def _pick_tile(n, target, align):
    if n <= target:
        return n
    best = None
    for t in range(align, target + 1, align):
        if n % t == 0:
            best = t
    assert best is not None, (n, target, align)
    return best


def _acc_store(ref, val, first):
    @pl.when(first)
    def _():
        ref[...] = val

    @pl.when(jnp.logical_not(first))
    def _():
        ref[...] += val


def _row_tile(rows, tiled_cols_bytes):
    per_row = sum(tiled_cols_bytes)
    if rows <= 16:
        return rows
    t = 1024
    while t > 16 and (t * per_row > ROW_IO_BUDGET or rows % t):
        t //= 2
    assert rows % t == 0, (rows, t)
    return t


def _rowwise_call(fn, tiled, params, outs, accs, name):
    rows = tiled[0].shape[0]
    tile = _row_tile(rows, [a.shape[1] * a.dtype.itemsize for a in tiled] + [c * jnp.dtype(d).itemsize for c, d in outs])
    n_in = len(tiled) + len(params)
    n_o = len(outs)

    def body(*refs):
        vals = [r[...] for r in refs[:n_in]]
        t_out, a_out = fn(*vals)
        for r, v in zip(refs[n_in:n_in + n_o], t_out):
            r[...] = v.astype(r.dtype)
        first = pl.program_id(0) == 0
        for r, v in zip(refs[n_in + n_o:], a_out):
            _acc_store(r, v.astype(F32), first)

    in_specs = [pl.BlockSpec((tile, a.shape[1]), lambda i: (i, 0)) for a in tiled]
    in_specs += [pl.BlockSpec(p.shape, lambda i: (0, 0)) for p in params]
    out_specs = [pl.BlockSpec((tile, c), lambda i: (i, 0)) for c, _ in outs]
    out_specs += [pl.BlockSpec(s, lambda i: (0, 0)) for s in accs]
    out_shape = [jax.ShapeDtypeStruct((rows, c), d) for c, d in outs]
    out_shape += [jax.ShapeDtypeStruct(s, F32) for s in accs]
    return pl.pallas_call(
        body, grid=(rows // tile,), in_specs=in_specs, out_specs=out_specs, out_shape=out_shape,
        compiler_params=_cparams("arbitrary"), name=name,
    )(*tiled, *params)


def _rowwise_op(f, n_t, n_p, out_dtypes, name, bwd=None):
    def to_f32(vals):
        return [v.astype(F32) for v in vals]

    def call_fwd(*args):
        shapes = jax.eval_shape(f, *[jax.ShapeDtypeStruct(a.shape, F32) for a in args])
        outs = [(s.shape[1], d) for s, d in zip(shapes, out_dtypes)]
        return tuple(_rowwise_call(lambda *v: (f(*to_f32(v)), ()), args[:n_t], args[n_t:], outs, [], name + "_fwd"))

    @jax.custom_vjp
    def op(*args):
        return call_fwd(*args)

    def op_fwd(*args):
        return call_fwd(*args), args

    def op_bwd(args, gs):
        n_g = len(gs)

        def bwd_fn(*vals):
            vals = to_f32(vals)
            prim = vals[:n_t] + vals[n_t + n_g:]
            g = tuple(vals[n_t:n_t + n_g])
            if bwd is not None:
                return bwd(*prim, *g)
            _, vjp = jax.vjp(f, *prim)
            cts = vjp(g)
            return tuple(cts[:n_t]), tuple(cts[n_t:])

        outs = [(a.shape[1], a.dtype) for a in args[:n_t]]
        accs = [p.shape for p in args[n_t:]]
        return tuple(_rowwise_call(bwd_fn, list(args[:n_t]) + list(gs), args[n_t:], outs, accs, name + "_bwd"))

    op.defvjp(op_fwd, op_bwd)
    return op


def _f_rmsnorm(x, g):
    return (x * lax.rsqrt(jnp.mean(x * x, axis=-1, keepdims=True) + EPS) * g,)


def _f_swiglu(gu):
    gate, up = gu[:, :D_FF], gu[:, D_FF:]
    return (gate * jax.nn.sigmoid(gate) * up,)


def _b_swiglu(gu, d):
    gate, up = gu[:, :D_FF], gu[:, D_FF:]
    s = jax.nn.sigmoid(gate)
    d_gate = d * up * s * (1.0 + gate * (1.0 - s))
    d_up = d * gate * s
    return (jnp.concatenate([d_gate, d_up], axis=1),), ()


def _f_gated_norm(ys, xs, z, dsk, g):
    t = (ys + xs * dsk) * (z * jax.nn.sigmoid(z))
    return (t * lax.rsqrt(jnp.mean(t * t, axis=-1, keepdims=True) + EPS) * g,)


def _f_merge(gates, ys, ym):
    s = jax.nn.sigmoid(gates)
    return (s[:, :D_MODEL] * ys + s[:, D_MODEL:] * ym,)


def _b_merge(gates, ys, ym, d):
    s = jax.nn.sigmoid(gates)
    s1, s2 = s[:, :D_MODEL], s[:, D_MODEL:]
    d_gates = jnp.concatenate([d * ys * s1 * (1.0 - s1), d * ym * s2 * (1.0 - s2)], axis=1)
    return (d_gates, d * s1, d * s2), ()


def _rmsnorm(x, g, out_dtype, name):
    return _rowwise_op(_f_rmsnorm, 1, 1, [out_dtype], name)(x, g.reshape(1, -1))[0]


def _swiglu(gu):
    return _rowwise_op(_f_swiglu, 1, 0, [BF16], "swiglu", bwd=_b_swiglu)(gu)[0]


def _gated_norm(ys, xs, z, dsk, g):
    return _rowwise_op(_f_gated_norm, 3, 2, [BF16], "gated_norm")(ys, xs, z, dsk.reshape(1, -1), g.reshape(1, -1))[0]


def _merge(gates, ys, ym):
    return _rowwise_op(_f_merge, 3, 0, [BF16], "merge", bwd=_b_merge)(gates, ys, ym)[0]


def _loss_and_grad(y, target):
    def fn(yv, tv):
        d = yv - tv
        return (d * (1.0 / D_MODEL),), (jnp.sum(d * d, axis=0, keepdims=True) * (0.5 / D_MODEL),)

    dy, part = _rowwise_call(fn, [y, target], [], [(D_MODEL, F32)], [(1, D_MODEL)], "loss")
    return jnp.sum(part), dy


def _adam(w, g, m, v):
    def fn(wv, gv, mv, vv):
        m2 = ADAM_B1 * mv + (1.0 - ADAM_B1) * gv
        v2 = ADAM_B2 * vv + (1.0 - ADAM_B2) * (gv * gv)
        m_hat = m2 / (1.0 - ADAM_B1 ** ADAM_STEP)
        v_hat = v2 / (1.0 - ADAM_B2 ** ADAM_STEP)
        delta = -ADAM_LR * (m_hat / (jnp.sqrt(v_hat) + ADAM_EPS) + ADAM_WD * wv)
        return (delta, m2, v2), ()

    c = w.shape[1]
    return _rowwise_call(fn, [w, g, m, v], [], [(c, F32)] * 3, [], "adamw")


def _sum_blocks(blocks):
    _, rows, c = blocks.shape
    tile = _row_tile(rows, [N_DEV * c * 4, c * 4])

    def body(b_ref, o_ref):
        acc = b_ref[0]
        for i in range(1, N_DEV):
            acc = acc + b_ref[i]
        o_ref[...] = acc

    return pl.pallas_call(
        body, grid=(rows // tile,), in_specs=[pl.BlockSpec((N_DEV, tile, c), lambda i: (0, i, 0))],
        out_specs=pl.BlockSpec((tile, c), lambda i: (i, 0)), out_shape=jax.ShapeDtypeStruct((rows, c), F32),
        compiler_params=_cparams("arbitrary"), name="sum_blocks",
    )(blocks)


def _mm_call(a, b, ta, tb, out_dtype):
    r_dim, p_dim = a.shape if ta else a.shape[::-1]
    r2, q_dim = b.shape[::-1] if tb else b.shape
    assert r_dim == r2, (a.shape, b.shape, ta, tb)
    tp = _pick_tile(p_dim, 512, LANE)
    tq = _pick_tile(q_dim, 1536, LANE)
    tr = _pick_tile(r_dim, 1536, LANE)
    nr = r_dim // tr
    dims = (((0 if ta else 1,), (1 if tb else 0,)), ((), ()))

    def body(a_ref, b_ref, o_ref, *scratch):
        part = lax.dot_general(a_ref[...], b_ref[...], dims, preferred_element_type=F32)
        if nr == 1:
            o_ref[...] = part.astype(o_ref.dtype)
        else:
            acc_ref = scratch[0]
            k = pl.program_id(2)
            _acc_store(acc_ref, part, k == 0)

            @pl.when(k == nr - 1)
            def _():
                o_ref[...] = acc_ref[...].astype(o_ref.dtype)

    a_spec = pl.BlockSpec((tr, tp), lambda j, i, k: (k, i)) if ta else pl.BlockSpec((tp, tr), lambda j, i, k: (i, k))
    b_spec = pl.BlockSpec((tq, tr), lambda j, i, k: (j, k)) if tb else pl.BlockSpec((tr, tq), lambda j, i, k: (k, j))
    return pl.pallas_call(
        body, grid=(q_dim // tq, p_dim // tp, nr), in_specs=[a_spec, b_spec],
        out_specs=pl.BlockSpec((tp, tq), lambda j, i, k: (i, j)),
        out_shape=jax.ShapeDtypeStruct((p_dim, q_dim), out_dtype),
        scratch_shapes=[pltpu.VMEM((tp, tq), F32)] if nr > 1 else [],
        compiler_params=_cparams("arbitrary", "arbitrary", "arbitrary"),
        name=f"mm_{'t' if ta else 'n'}{'t' if tb else 'n'}_{p_dim}x{r_dim}x{q_dim}",
    )(a, b)


@jax.custom_vjp
def _mm(a, w, w_f32_slot):
    del w_f32_slot
    return _mm_call(a, w, False, False, F32)


def _mm_fwd(a, w, w_f32_slot):
    del w_f32_slot
    return _mm_call(a, w, False, False, F32), (a, w)


def _mm_bwd(res, g):
    a, w = res
    gb = g.astype(BF16)
    da = _mm_call(gb, w, False, True, a.dtype)
    dw = _mm_call(a, gb, True, False, F32)
    return da, jnp.zeros_like(w), dw


_mm.defvjp(_mm_fwd, _mm_bwd)


ATTN_SCALE = QK ** -0.5


def _attn_tile(s):
    return min(512, s)


def _causal_mask(t, row_blk, col_blk):
    row = lax.broadcasted_iota(jnp.int32, (t, t), 0) + row_blk * t
    col = lax.broadcasted_iota(jnp.int32, (t, t), 1) + col_blk * t
    return row, col


def _nt(a, b):
    return lax.dot_general(a, b, (((1,), (1,)), ((), ())), preferred_element_type=F32)


def _attn_fwd_call(q, k, v):
    nh, s, _ = q.shape
    t = _attn_tile(s)
    nb = s // t

    def body(q_ref, k_ref, v_ref, o_ref, lse_ref, m_sc, l_sc, acc_sc):
        qi, ki = pl.program_id(1), pl.program_id(2)

        @pl.when(ki == 0)
        def _():
            m_sc[...] = jnp.full_like(m_sc, NEG)
            l_sc[...] = jnp.zeros_like(l_sc)
            acc_sc[...] = jnp.zeros_like(acc_sc)

        @pl.when(ki <= qi)
        def _():
            sc = _nt(q_ref[0], k_ref[0]) * ATTN_SCALE
            row, col = _causal_mask(t, qi, ki)
            sc = jnp.where(col <= row, sc, NEG)
            m_prev = m_sc[...]
            m_new = jnp.maximum(m_prev, jnp.max(sc, axis=-1, keepdims=True))
            p = jnp.exp(sc - m_new)
            alpha = jnp.exp(m_prev - m_new)
            l_sc[...] = alpha * l_sc[...] + jnp.sum(p, axis=-1, keepdims=True)
            acc_sc[...] = alpha * acc_sc[...] + jnp.dot(p.astype(BF16), v_ref[0], preferred_element_type=F32)
            m_sc[...] = m_new

        @pl.when(ki == qi)
        def _():
            o_ref[0] = (acc_sc[...] / l_sc[...]).astype(o_ref.dtype)
            lse_ref[0] = m_sc[...] + jnp.log(l_sc[...])

    qmap = lambda h, i, j: (h, i, 0)
    kmap = lambda h, i, j: (h, jnp.minimum(i, j), 0)
    return pl.pallas_call(
        body, grid=(nh, nb, nb),
        in_specs=[pl.BlockSpec((1, t, QK), qmap), pl.BlockSpec((1, t, QK), kmap), pl.BlockSpec((1, t, VDIM), kmap)],
        out_specs=[pl.BlockSpec((1, t, VDIM), qmap), pl.BlockSpec((1, t, 1), qmap)],
        out_shape=[jax.ShapeDtypeStruct((nh, s, VDIM), BF16), jax.ShapeDtypeStruct((nh, s, 1), F32)],
        scratch_shapes=[pltpu.VMEM((t, 1), F32), pltpu.VMEM((t, 1), F32), pltpu.VMEM((t, VDIM), F32)],
        compiler_params=_cparams("arbitrary", "arbitrary", "arbitrary"), name="attn_fwd",
    )(q, k, v)


def _attn_dq_call(q, k, v, o, do, lse):
    nh, s, _ = q.shape
    t = _attn_tile(s)
    nb = s // t

    def body(q_ref, k_ref, v_ref, o_ref, do_ref, lse_ref, dq_ref, delta_ref, acc_sc, delta_sc):
        qi, ki = pl.program_id(1), pl.program_id(2)

        @pl.when(ki == 0)
        def _():
            acc_sc[...] = jnp.zeros_like(acc_sc)
            delta_sc[...] = jnp.sum(do_ref[0].astype(F32) * o_ref[0].astype(F32), axis=-1, keepdims=True)

        @pl.when(ki <= qi)
        def _():
            sc = _nt(q_ref[0], k_ref[0]) * ATTN_SCALE
            row, col = _causal_mask(t, qi, ki)
            sc = jnp.where(col <= row, sc, NEG)
            p = jnp.exp(sc - lse_ref[0])
            dp = _nt(do_ref[0], v_ref[0])
            ds = p * (dp - delta_sc[...])
            acc_sc[...] += jnp.dot(ds.astype(BF16), k_ref[0], preferred_element_type=F32)

        @pl.when(ki == qi)
        def _():
            dq_ref[0] = (acc_sc[...] * ATTN_SCALE).astype(dq_ref.dtype)
            delta_ref[0] = delta_sc[...]

    qmap = lambda h, i, j: (h, i, 0)
    kmap = lambda h, i, j: (h, jnp.minimum(i, j), 0)
    return pl.pallas_call(
        body, grid=(nh, nb, nb),
        in_specs=[pl.BlockSpec((1, t, QK), qmap), pl.BlockSpec((1, t, QK), kmap), pl.BlockSpec((1, t, VDIM), kmap),
                  pl.BlockSpec((1, t, VDIM), qmap), pl.BlockSpec((1, t, VDIM), qmap), pl.BlockSpec((1, t, 1), qmap)],
        out_specs=[pl.BlockSpec((1, t, QK), qmap), pl.BlockSpec((1, t, 1), qmap)],
        out_shape=[jax.ShapeDtypeStruct((nh, s, QK), BF16), jax.ShapeDtypeStruct((nh, s, 1), F32)],
        scratch_shapes=[pltpu.VMEM((t, QK), F32), pltpu.VMEM((t, 1), F32)],
        compiler_params=_cparams("arbitrary", "arbitrary", "arbitrary"), name="attn_dq",
    )(q, k, v, o, do, lse)


def _attn_dkv_call(q, k, v, do, lse_t, delta_t):
    nh, s, _ = q.shape
    t = _attn_tile(s)
    nb = s // t

    def body(q_ref, k_ref, v_ref, do_ref, lse_ref, delta_ref, dk_ref, dv_ref, dk_sc, dv_sc):
        ki, qi = pl.program_id(1), pl.program_id(2)

        @pl.when(qi == 0)
        def _():
            dk_sc[...] = jnp.zeros_like(dk_sc)
            dv_sc[...] = jnp.zeros_like(dv_sc)

        @pl.when(qi >= ki)
        def _():
            sc = _nt(k_ref[0], q_ref[0]) * ATTN_SCALE
            krow, qcol = _causal_mask(t, ki, qi)
            sc = jnp.where(krow <= qcol, sc, NEG)
            p = jnp.exp(sc - lse_ref[0])
            dv_sc[...] += jnp.dot(p.astype(BF16), do_ref[0], preferred_element_type=F32)
            dp = _nt(v_ref[0], do_ref[0])
            ds = p * (dp - delta_ref[0])
            dk_sc[...] += jnp.dot(ds.astype(BF16), q_ref[0], preferred_element_type=F32)

        @pl.when(qi == nb - 1)
        def _():
            dk_ref[0] = (dk_sc[...] * ATTN_SCALE).astype(dk_ref.dtype)
            dv_ref[0] = dv_sc[...].astype(dv_ref.dtype)

    kmap = lambda h, j, i: (h, j, 0)
    qmap = lambda h, j, i: (h, jnp.maximum(i, j), 0)
    smap = lambda h, j, i: (h, 0, jnp.maximum(i, j))
    return pl.pallas_call(
        body, grid=(nh, nb, nb),
        in_specs=[pl.BlockSpec((1, t, QK), qmap), pl.BlockSpec((1, t, QK), kmap), pl.BlockSpec((1, t, VDIM), kmap),
                  pl.BlockSpec((1, t, VDIM), qmap), pl.BlockSpec((1, 1, t), smap), pl.BlockSpec((1, 1, t), smap)],
        out_specs=[pl.BlockSpec((1, t, QK), kmap), pl.BlockSpec((1, t, VDIM), kmap)],
        out_shape=[jax.ShapeDtypeStruct((nh, s, QK), BF16), jax.ShapeDtypeStruct((nh, s, VDIM), BF16)],
        scratch_shapes=[pltpu.VMEM((t, QK), F32), pltpu.VMEM((t, VDIM), F32)],
        compiler_params=_cparams("arbitrary", "arbitrary", "arbitrary"), name="attn_dkv",
    )(q, k, v, do, lse_t, delta_t)


@jax.custom_vjp
def _attention(q, k, v):
    return _attn_fwd_call(q, k, v)[0]


def _attention_fwd(q, k, v):
    o, lse = _attn_fwd_call(q, k, v)
    return o, (q, k, v, o, lse)


def _attention_bwd(res, do):
    q, k, v, o, lse = res
    dq, delta = _attn_dq_call(q, k, v, o, do, lse)
    dk, dv = _attn_dkv_call(q, k, v, do, jnp.swapaxes(lse, 1, 2), jnp.swapaxes(delta, 1, 2))
    return dq, dk, dv


_attention.defvjp(_attention_fwd, _attention_bwd)


CONV_TC = 512
HALO = 8


def _conv_tiles(s):
    return min(512, s)


def _conv_fwd_call(x, w, b):
    s, c = x.shape
    ts = _conv_tiles(s)
    hb = ts // HALO

    def body(x_ref, prev_ref, w_ref, b_ref, y_ref, buf):
        si = pl.program_id(1)
        buf[0:HALO, :] = jnp.where(si > 0, prev_ref[...], 0.0)
        buf[HALO:, :] = x_ref[...]
        acc = jnp.broadcast_to(b_ref[...], (ts, CONV_TC))
        for k in range(CONV_K):
            acc = acc + w_ref[k:k + 1, :] * buf[pl.ds(HALO - (CONV_K - 1) + k, ts), :]
        y_ref[...] = acc * jax.nn.sigmoid(acc)

    return pl.pallas_call(
        body, grid=(c // CONV_TC, s // ts),
        in_specs=[pl.BlockSpec((ts, CONV_TC), lambda ci, si: (si, ci)),
                  pl.BlockSpec((HALO, CONV_TC), lambda ci, si: (jnp.maximum(si * hb - 1, 0), ci)),
                  pl.BlockSpec((CONV_K, CONV_TC), lambda ci, si: (0, ci)),
                  pl.BlockSpec((1, CONV_TC), lambda ci, si: (0, ci))],
        out_specs=pl.BlockSpec((ts, CONV_TC), lambda ci, si: (si, ci)),
        out_shape=jax.ShapeDtypeStruct((s, c), F32),
        scratch_shapes=[pltpu.VMEM((ts + HALO, CONV_TC), F32)],
        compiler_params=_cparams("arbitrary", "arbitrary"), name="conv_fwd",
    )(x, x, w, b)


def _conv_bwd_call(x, w, b, dy):
    s, c = x.shape
    ts = _conv_tiles(s)
    hb = ts // HALO
    ns = s // ts
    last_halo = s // HALO - 1

    def body(x_ref, prev_ref, next_ref, dy_ref, dyn_ref, w_ref, b_ref, dx_ref, dw_ref, db_ref, xbuf, dbuf):
        si = pl.program_id(1)
        xbuf[0:HALO, :] = jnp.where(si > 0, prev_ref[...], 0.0)
        xbuf[HALO:HALO + ts, :] = x_ref[...]
        xbuf[HALO + ts:, :] = next_ref[...]
        pre = jnp.broadcast_to(b_ref[...], (ts + HALO, CONV_TC))
        for k in range(CONV_K):
            pre = pre + w_ref[k:k + 1, :] * xbuf[pl.ds(HALO - (CONV_K - 1) + k, ts + HALO), :]
        sg = jax.nn.sigmoid(pre)
        dsilu = sg * (1.0 + pre * (1.0 - sg))
        dbuf[0:ts, :] = dy_ref[...] * dsilu[0:ts]
        dbuf[ts:, :] = jnp.where(si < ns - 1, dyn_ref[...] * dsilu[ts:], 0.0)
        dx = jnp.zeros((ts, CONV_TC), F32)
        for k in range(CONV_K):
            dx = dx + w_ref[k:k + 1, :] * dbuf[pl.ds(CONV_K - 1 - k, ts), :]
        dx_ref[...] = dx
        dpre = dbuf[0:ts, :]
        first = si == 0
        _acc_store(db_ref, jnp.sum(dpre, axis=0, keepdims=True), first)
        for k in range(CONV_K):
            dw_k = jnp.sum(dpre * xbuf[pl.ds(HALO - (CONV_K - 1) + k, ts), :], axis=0, keepdims=True)
            _acc_store(dw_ref.at[pl.ds(k, 1), :], dw_k, first)

    main = lambda ci, si: (si, ci)
    return pl.pallas_call(
        body, grid=(c // CONV_TC, ns),
        in_specs=[pl.BlockSpec((ts, CONV_TC), main),
                  pl.BlockSpec((HALO, CONV_TC), lambda ci, si: (jnp.maximum(si * hb - 1, 0), ci)),
                  pl.BlockSpec((HALO, CONV_TC), lambda ci, si: (jnp.minimum(si * hb + hb, last_halo), ci)),
                  pl.BlockSpec((ts, CONV_TC), main),
                  pl.BlockSpec((HALO, CONV_TC), lambda ci, si: (jnp.minimum(si * hb + hb, last_halo), ci)),
                  pl.BlockSpec((CONV_K, CONV_TC), lambda ci, si: (0, ci)),
                  pl.BlockSpec((1, CONV_TC), lambda ci, si: (0, ci))],
        out_specs=[pl.BlockSpec((ts, CONV_TC), main),
                   pl.BlockSpec((CONV_K, CONV_TC), lambda ci, si: (0, ci)),
                   pl.BlockSpec((1, CONV_TC), lambda ci, si: (0, ci))],
        out_shape=[jax.ShapeDtypeStruct((s, c), F32), jax.ShapeDtypeStruct((CONV_K, c), F32), jax.ShapeDtypeStruct((1, c), F32)],
        scratch_shapes=[pltpu.VMEM((ts + 2 * HALO, CONV_TC), F32), pltpu.VMEM((ts + HALO, CONV_TC), F32)],
        compiler_params=_cparams("arbitrary", "arbitrary"), name="conv_bwd",
    )(x, x, x, dy, dy, w, b)


@jax.custom_vjp
def _conv_silu(x, w, b):
    return _conv_fwd_call(x, w, b)


def _conv_silu_fwd(x, w, b):
    return _conv_fwd_call(x, w, b), (x, w, b)


def _conv_silu_bwd(res, dy):
    x, w, b = res
    return tuple(_conv_bwd_call(x, w, b, dy))


_conv_silu.defvjp(_conv_silu_fwd, _conv_silu_bwd)


GW = SSD_HPG * SSD_P


def _ones_where(mask):
    return jnp.where(mask, 1.0, 0.0).astype(BF16)


def _split3(v):
    hi = v.astype(BF16)
    r1 = v - hi.astype(F32)
    mid = r1.astype(BF16)
    lo = (r1 - mid.astype(F32)).astype(BF16)
    return hi, mid, lo


def _dot_sel_r(v, sel):
    out = None
    for part in _split3(v):
        t = jnp.dot(part, sel, preferred_element_type=F32)
        out = t if out is None else out + t
    return out


def _dot_sel_l(sel, v):
    out = None
    for part in _split3(v):
        t = jnp.dot(sel, part, preferred_element_type=F32)
        out = t if out is None else out + t
    return out


def _ssd_consts():
    r = lax.broadcasted_iota(jnp.int32, (SSD_L, SSD_L), 0)
    c = lax.broadcasted_iota(jnp.int32, (SSD_L, SSD_L), 1)
    tril = r >= c
    triu = c >= r
    shift = SSD_P.bit_length() - 1
    eh = lax.broadcasted_iota(jnp.int32, (SSD_H, SSD_DI), 0)
    ej = lax.broadcasted_iota(jnp.int32, (SSD_H, SSD_DI), 1)
    expand = _ones_where(lax.shift_right_logical(ej, shift) == eh)
    rj = lax.broadcasted_iota(jnp.int32, (SSD_DI, SSD_H), 0)
    rh = lax.broadcasted_iota(jnp.int32, (SSD_DI, SSD_H), 1)
    reduce_ = _ones_where(lax.shift_right_logical(rj, shift) == rh)
    lane = lax.broadcasted_iota(jnp.int32, (SSD_L, LANE), 1)
    return tril, triu, expand, reduce_, lane < SSD_P


def _ssd_decays(dt, dt_t, a, a_t, tril, triu, expand):
    dta = dt * a
    acum = _dot_sel_l(_ones_where(tril), dta)
    acum_t = _dot_sel_r(dt_t * a_t, _ones_where(triu))
    dta_e = _dot_sel_r(dta, expand)
    acum_e = _dot_sel_r(acum, expand)
    last_e = jnp.sum(dta_e, axis=0, keepdims=True)
    return acum, acum_t, acum_e, last_e


def _head_decay(acum, acum_t, h, tril):
    seg = acum[:, h:h + 1] - acum_t[h:h + 1, :]
    return jnp.exp(jnp.where(tril, seg, NEG))


def _ssd_fwd_call(x, dt, a, bm, cm):
    s = x.shape[0]
    nc = s // SSD_L
    dt_t = dt.T
    a_t = a.T

    def body(x_ref, dt_ref, dtt_ref, a_ref, at_ref, b_ref, c_ref, y_ref, st_ref, s_sc):
        ci = pl.program_id(0)

        @pl.when(ci == 0)
        def _():
            s_sc[...] = jnp.zeros_like(s_sc)

        st_ref[0] = s_sc[...]
        tril, triu, expand, _, low_half = _ssd_consts()
        acum, acum_t, acum_e, last_e = _ssd_decays(dt_ref[...], dtt_ref[...], a_ref[...], at_ref[...], tril, triu, expand)
        dt_e = _dot_sel_r(dt_ref[...], expand)
        xdt = x_ref[...] * dt_e
        xdt_b = xdt.astype(BF16)
        xw_b = (xdt * jnp.exp(last_e - acum_e)).astype(BF16)
        ea_e = jnp.exp(acum_e)
        el_e = jnp.exp(last_e)
        for g in range(SSD_G):
            gs = slice(g * GW, (g + 1) * GW)
            bg = b_ref[:, g * SSD_N:(g + 1) * SSD_N]
            cg_b = c_ref[:, g * SSD_N:(g + 1) * SSD_N].astype(BF16)
            bg_b = bg.astype(BF16)
            cb = _nt(cg_b, bg_b)
            st = s_sc[:, gs]
            y_off = jnp.dot(cg_b, st.astype(BF16), preferred_element_type=F32) * ea_e[:, gs]
            for pr in range(SSD_HPG // 2):
                ls = slice(g * GW + pr * LANE, g * GW + (pr + 1) * LANE)
                xp = xdt_b[:, ls]
                yd = []
                for half in range(2):
                    h = g * SSD_HPG + pr * 2 + half
                    m = (cb * _head_decay(acum, acum_t, h, tril)).astype(BF16)
                    yd.append(jnp.dot(m, xp, preferred_element_type=F32))
                y_ref[:, ls] = jnp.where(low_half, yd[0], yd[1]) + y_off[:, pr * LANE:(pr + 1) * LANE]
            s_sc[:, gs] = st * el_e[:, gs] + jnp.dot(bg.T.astype(BF16), xw_b[:, gs], preferred_element_type=F32)

    row = lambda i: (i, 0)
    return pl.pallas_call(
        body, grid=(nc,),
        in_specs=[pl.BlockSpec((SSD_L, SSD_DI), row), pl.BlockSpec((SSD_L, SSD_H), row),
                  pl.BlockSpec((SSD_H, SSD_L), lambda i: (0, i)), pl.BlockSpec((1, SSD_H), lambda i: (0, 0)),
                  pl.BlockSpec((SSD_H, 1), lambda i: (0, 0)),
                  pl.BlockSpec((SSD_L, SSD_G * SSD_N), row), pl.BlockSpec((SSD_L, SSD_G * SSD_N), row)],
        out_specs=[pl.BlockSpec((SSD_L, SSD_DI), row), pl.BlockSpec((1, SSD_N, SSD_DI), lambda i: (i, 0, 0))],
        out_shape=[jax.ShapeDtypeStruct((s, SSD_DI), F32), jax.ShapeDtypeStruct((nc, SSD_N, SSD_DI), F32)],
        scratch_shapes=[pltpu.VMEM((SSD_N, SSD_DI), F32)],
        compiler_params=_cparams("arbitrary"), name="ssd_fwd",
    )(x, dt, dt_t, a, a_t, bm, cm)


def _ssd_bwd_call(x, dt, a, bm, cm, states, dy):
    s = x.shape[0]
    nc = s // SSD_L
    dt_t = dt.T
    a_t = a.T

    def body(x_ref, dt_ref, dtt_ref, a_ref, at_ref, b_ref, c_ref, st_ref, dy_ref,
             dx_ref, ddt_ref, db_ref, dc_ref, da_ref, ds_sc, yf_sc, dxd_sc, dxw_sc):
        i = pl.program_id(0)

        @pl.when(i == 0)
        def _():
            ds_sc[...] = jnp.zeros_like(ds_sc)

        tril, triu, expand, reduce_, low_half = _ssd_consts()
        dt = dt_ref[...]
        a_row = a_ref[...]
        acum, acum_t, acum_e, last_e = _ssd_decays(dt, dtt_ref[...], a_row, at_ref[...], tril, triu, expand)
        dt_e = _dot_sel_r(dt, expand)
        x = x_ref[...]
        xdt = x * dt_e
        xdt_b = xdt.astype(BF16)
        w_e = jnp.exp(last_e - acum_e)
        xw_b = (xdt * w_e).astype(BF16)
        ea_e = jnp.exp(acum_e)
        el_e = jnp.exp(last_e)
        dy = dy_ref[...]
        dy_b = dy.astype(BF16)
        s_prev = st_ref[0]
        ds_new = ds_sc[...]
        ds_new_b = ds_new.astype(BF16)
        triu_b = _ones_where(triu)
        strict_tril = jnp.logical_not(triu)
        head_ids = lax.broadcasted_iota(jnp.int32, (1, SSD_H), 1)
        d_dta_diag = jnp.zeros((SSD_L, SSD_H), F32)
        for g in range(SSD_G):
            gs = slice(g * GW, (g + 1) * GW)
            ns_ = slice(g * SSD_N, (g + 1) * SSD_N)
            bg = b_ref[:, ns_]
            cg = c_ref[:, ns_]
            bg_b, cg_b = bg.astype(BF16), cg.astype(BF16)
            st_b = s_prev[:, gs].astype(BF16)
            y_off = jnp.dot(cg_b, st_b, preferred_element_type=F32) * ea_e[:, gs]
            yf_sc[:, gs] = y_off
            dz_b = (dy[:, gs] * ea_e[:, gs]).astype(BF16)
            d_c = _nt(dz_b, st_b)
            ds_prev = ds_new[:, gs] * el_e[:, gs] + jnp.dot(cg.T.astype(BF16), dz_b, preferred_element_type=F32)
            dxw_sc[:, gs] = jnp.dot(bg_b, ds_new_b[:, gs], preferred_element_type=F32)
            d_b = _nt(xw_b[:, gs], ds_new_b[:, gs])
            cb = _nt(cg_b, bg_b)
            d_g = jnp.zeros((SSD_L, SSD_L), F32)
            for pr in range(SSD_HPG // 2):
                ls = slice(g * GW + pr * LANE, g * GW + (pr + 1) * LANE)
                xp = xdt_b[:, ls]
                dyp = dy[:, ls]
                dyp_b = dy_b[:, ls]
                dxd = []
                for half in range(2):
                    h = g * SSD_HPG + pr * 2 + half
                    dec = _head_decay(acum, acum_t, h, tril)
                    m = cb * dec
                    dxd.append(jnp.dot(m.T.astype(BF16), dyp_b, preferred_element_type=F32))
                    mine = low_half if half == 0 else jnp.logical_not(low_half)
                    d_m = _nt(jnp.where(mine, dyp, 0.0).astype(BF16), xp)
                    d_g = d_g + d_m * dec
                    below = jnp.dot(triu_b, (d_m * m).astype(BF16), preferred_element_type=F32)
                    col = jnp.sum(jnp.where(strict_tril, below, 0.0), axis=1, keepdims=True)
                    d_dta_diag = d_dta_diag + col * jnp.where(head_ids == h, 1.0, 0.0)
                dxd_sc[:, ls] = jnp.where(low_half, dxd[0], dxd[1])
            d_g_b = d_g.astype(BF16)
            dc_ref[:, ns_] = d_c + jnp.dot(d_g_b, bg_b, preferred_element_type=F32)
            db_ref[:, ns_] = d_b + jnp.dot(d_g.T.astype(BF16), cg_b, preferred_element_type=F32)
            ds_sc[:, gs] = ds_prev
        dxw = dxw_sc[...]
        dxd = dxd_sc[...]
        dw_e = xdt * dxw * w_e
        d_acum_e = dy * yf_sc[...] - dw_e
        d_last_e = jnp.sum(ds_new * s_prev, axis=0, keepdims=True) * el_e + jnp.sum(dw_e, axis=0, keepdims=True)
        suffix = _dot_sel_l(triu_b, d_acum_e)
        d_dta = _dot_sel_r(suffix + d_last_e, reduce_) + d_dta_diag
        dxdt = dxd + dxw * w_e
        dx_ref[...] = dxdt * dt_e
        ddt_ref[...] = d_dta * a_row + _dot_sel_r(dxdt * x, reduce_)
        _acc_store(da_ref, jnp.sum(d_dta * dt, axis=0, keepdims=True), i == 0)

    rev = lambda i: (nc - 1 - i, 0)
    return pl.pallas_call(
        body, grid=(nc,),
        in_specs=[pl.BlockSpec((SSD_L, SSD_DI), rev), pl.BlockSpec((SSD_L, SSD_H), rev),
                  pl.BlockSpec((SSD_H, SSD_L), lambda i: (0, nc - 1 - i)), pl.BlockSpec((1, SSD_H), lambda i: (0, 0)),
                  pl.BlockSpec((SSD_H, 1), lambda i: (0, 0)),
                  pl.BlockSpec((SSD_L, SSD_G * SSD_N), rev), pl.BlockSpec((SSD_L, SSD_G * SSD_N), rev),
                  pl.BlockSpec((1, SSD_N, SSD_DI), lambda i: (nc - 1 - i, 0, 0)),
                  pl.BlockSpec((SSD_L, SSD_DI), rev)],
        out_specs=[pl.BlockSpec((SSD_L, SSD_DI), rev), pl.BlockSpec((SSD_L, SSD_H), rev),
                   pl.BlockSpec((SSD_L, SSD_G * SSD_N), rev), pl.BlockSpec((SSD_L, SSD_G * SSD_N), rev),
                   pl.BlockSpec((1, SSD_H), lambda i: (0, 0))],
        out_shape=[jax.ShapeDtypeStruct((s, SSD_DI), F32), jax.ShapeDtypeStruct((s, SSD_H), F32),
                   jax.ShapeDtypeStruct((s, SSD_G * SSD_N), F32), jax.ShapeDtypeStruct((s, SSD_G * SSD_N), F32),
                   jax.ShapeDtypeStruct((1, SSD_H), F32)],
        scratch_shapes=[pltpu.VMEM((SSD_N, SSD_DI), F32), pltpu.VMEM((SSD_L, SSD_DI), F32),
                        pltpu.VMEM((SSD_L, SSD_DI), F32), pltpu.VMEM((SSD_L, SSD_DI), F32)],
        compiler_params=_cparams("arbitrary"), name="ssd_bwd",
    )(x, dt, dt_t, a, a_t, bm, cm, states, dy)


@jax.custom_vjp
def _ssd(x, dt, a, bm, cm):
    return _ssd_fwd_call(x, dt, a, bm, cm)[0]


def _ssd_vjp_fwd(x, dt, a, bm, cm):
    y, states = _ssd_fwd_call(x, dt, a, bm, cm)
    return y, (x, dt, a, bm, cm, states)


def _ssd_vjp_bwd(res, dy):
    x, dt, a, bm, cm, states = res
    dx, ddt, db, dc, da = _ssd_bwd_call(x, dt, a, bm, cm, states, dy)
    return dx, ddt, da, db, dc


_ssd.defvjp(_ssd_vjp_fwd, _ssd_vjp_bwd)


HBM_SPEC = pl.BlockSpec(memory_space=pltpu.HBM)
N_PEERS = N_DEV - 1


def _flip(v, f):
    return 1 - v if f else v


def _all_gather(shard):
    rows, c = shard.shape

    def body(x_ref, out_ref, send_sems, recv_sems, local_sem):
        x, y, cc = lax.axis_index("x"), lax.axis_index("y"), lax.axis_index("c")
        me, sibling = (x, y, cc), (x, y, 1 - cc)
        chips = [(1 - x, y), (x, 1 - y), (1 - x, 1 - y)]

        def slot(px, py, pc):
            return out_ref.at[4 * px + 2 * py + pc]

        def copy(k, block, to, src=None):
            return pltpu.make_async_remote_copy(
                src_ref=slot(*block) if src is None else src, dst_ref=slot(*block),
                send_sem=send_sems.at[k], recv_sem=recv_sems.at[k],
                device_id=to, device_id_type=pl.DeviceIdType.MESH)

        mine = pltpu.make_async_copy(x_ref, slot(*me), local_sem)
        mine.start()
        first = [copy(0, me, sibling, src=x_ref)]
        first += [copy(1 + j, me, (*chip, cc), src=x_ref) for j, chip in enumerate(chips)]
        for cp in first:
            cp.start()
        passed = [copy(4 + j, (*chip, cc), sibling) for j, chip in enumerate(chips)]
        for j, chip in enumerate(chips):
            copy(1 + j, (*chip, cc), me).wait_recv()
            passed[j].start()
        copy(0, sibling, me).wait_recv()
        for j, chip in enumerate(chips):
            copy(4 + j, (*chip, 1 - cc), me).wait_recv()
        for cp in first + passed:
            cp.wait_send()
        mine.wait()

    return pl.pallas_call(
        body, out_shape=jax.ShapeDtypeStruct((N_DEV, rows, c), shard.dtype),
        in_specs=[HBM_SPEC], out_specs=HBM_SPEC,
        scratch_shapes=[pltpu.SemaphoreType.DMA((N_PEERS,)), pltpu.SemaphoreType.DMA((N_PEERS,)), pltpu.SemaphoreType.DMA(())],
        name="all_gather",
    )(shard)


def _exchange_blocks(blocks):
    _, rows, c = blocks.shape

    def body(g_ref, out_ref, send_sems, recv_sems, local_sem):
        x, y, cc = lax.axis_index("x"), lax.axis_index("y"), lax.axis_index("c")
        me = 4 * x + 2 * y + cc
        mine = pltpu.make_async_copy(g_ref.at[me], out_ref.at[me], local_sem)
        mine.start()
        copies = []
        for k in range(1, N_DEV):
            px, py, pc = _flip(x, k & 4), _flip(y, k & 2), _flip(cc, k & 1)
            peer = 4 * px + 2 * py + pc
            copies.append((
                pltpu.make_async_remote_copy(
                    src_ref=g_ref.at[peer], dst_ref=out_ref.at[me], send_sem=send_sems.at[k - 1], recv_sem=recv_sems.at[k - 1],
                    device_id=(px, py, pc), device_id_type=pl.DeviceIdType.MESH),
                pltpu.make_async_remote_copy(
                    src_ref=g_ref.at[peer], dst_ref=out_ref.at[peer], send_sem=send_sems.at[k - 1], recv_sem=recv_sems.at[k - 1],
                    device_id=(px, py, pc), device_id_type=pl.DeviceIdType.MESH)))
        for send, _ in copies:
            send.start()
        for _, landed in copies:
            landed.wait_recv()
        for send, _ in copies:
            send.wait_send()
        mine.wait()

    return pl.pallas_call(
        body, out_shape=jax.ShapeDtypeStruct(blocks.shape, blocks.dtype),
        in_specs=[HBM_SPEC], out_specs=HBM_SPEC,
        scratch_shapes=[pltpu.SemaphoreType.DMA((N_PEERS,)), pltpu.SemaphoreType.DMA((N_PEERS,)), pltpu.SemaphoreType.DMA(())],
        name="exchange_blocks",
    )(blocks)


BIG = [
    ("ffn1_w13", (D_MODEL, 2 * D_FF), 1), ("ffn1_w2", (D_FF, D_MODEL), 0), ("w_in", (D_MODEL, D_IN), 1),
    ("w_ssd_out", (SSD_DI, D_MODEL), 0), ("w_uq", (Q_LORA, MLA_H * QK), 1), ("w_ukv", (KV_LORA, MLA_H * (NOPE + VDIM)), 1),
    ("w_mla_out", (MLA_H * VDIM, D_MODEL), 0), ("w_o", (D_MODEL, D_MODEL), 0),
    ("ffn2_w13", (D_MODEL, 2 * D_FF), 1), ("ffn2_w2", (D_FF, D_MODEL), 0),
]
SMALL = [
    ("ln_ffn1", D_MODEL), ("ln_mix", D_MODEL), ("conv_b", CONV_DIM), ("dt_bias", SSD_H), ("a_log", SSD_H), ("d_skip", SSD_H),
    ("ssd_norm", SSD_DI), ("q_lora_norm", Q_LORA), ("kv_lora_norm", KV_LORA), ("q_norm", QK), ("k_norm", QK), ("ln_ffn2", D_MODEL),
]


def _shard_shape(full, axis):
    k, n = full
    return (k // N_DEV, n) if axis == 0 else (k, n // N_DEV)


def _shard_rows(full):
    return full[0] * full[1] // N_DEV // PACK_COLS


LAYER_ROWS = sum(_shard_rows(f) for _, f, _ in BIG)
LAYER_ROWS_PAD = -(-LAYER_ROWS // 256) * 256


def _pack_shards(shards):
    parts = [shards[name].reshape(-1, PACK_COLS) for name, _, _ in BIG]
    pad = LAYER_ROWS_PAD - LAYER_ROWS
    if pad:
        parts.append(jnp.zeros((pad, PACK_COLS), parts[0].dtype))
    return jnp.concatenate(parts, axis=0)


def _unpack_shards(packed):
    out, r = {}, 0
    for name, full, axis in BIG:
        n = _shard_rows(full)
        out[name] = packed[r:r + n].reshape(_shard_shape(full, axis))
        r += n
    return out


def _unpack_gathered(gathered):
    out, r = {}, 0
    for name, full, axis in BIG:
        n = _shard_rows(full)
        blk = gathered[:, r:r + n].reshape((N_DEV,) + _shard_shape(full, axis))
        out[name] = blk.reshape(full) if axis == 0 else jnp.transpose(blk, (1, 0, 2)).reshape(full)
        r += n
    return out


def _pack_full_grads(grads):
    parts = []
    for name, full, axis in BIG:
        g = grads[name]
        k, n = _shard_shape(full, axis)
        blk = g.reshape(N_DEV, k, n) if axis == 0 else jnp.transpose(g.reshape(k, N_DEV, n), (1, 0, 2))
        parts.append(blk.reshape(N_DEV, -1, PACK_COLS))
    pad = LAYER_ROWS_PAD - LAYER_ROWS
    if pad:
        parts.append(jnp.zeros((N_DEV, pad, PACK_COLS), F32))
    return jnp.concatenate(parts, axis=1)


SMALL_COLS = sum(n for _, n in SMALL) + CONV_K * CONV_DIM
SMALL_ROWS = -(-(DEPTH * SMALL_COLS) // (8 * PACK_COLS)) * 8


def _pack_small(vals, conv_w):
    flat = jnp.concatenate([vals[name] for name, _ in SMALL] + [conv_w.reshape(DEPTH, -1)], axis=1).reshape(-1)
    flat = jnp.concatenate([flat, jnp.zeros((SMALL_ROWS * PACK_COLS - flat.shape[0],), F32)])
    return flat.reshape(SMALL_ROWS, PACK_COLS)


def _unpack_small(packed):
    flat = packed.reshape(-1)[:DEPTH * SMALL_COLS].reshape(DEPTH, SMALL_COLS)
    out, c = {}, 0
    for name, n in SMALL:
        out[name] = flat[:, c:c + n]
        c += n
    return out, flat[:, c:].reshape(DEPTH, CONV_K, CONV_DIM)


def _rope(t, cos, sin):
    half = ROPE // 2
    t1, t2 = t[..., :half], t[..., half:]
    c, s = cos[:, None, :], sin[:, None, :]
    return jnp.concatenate([t1 * c - t2 * s, t2 * c + t1 * s], axis=-1)


def _ffn(h, ln, w13, w2, slot13, slot2, name):
    n = _rmsnorm(h, ln, BF16, name)
    act = _swiglu(_mm(n, w13, slot13))
    return h + 0.5 * _mm(act, w2, slot2)


def _layer(h, big, slots, small, conv_w, cos, sin):
    s = h.shape[0]
    h = _ffn(h, small["ln_ffn1"], big["ffn1_w13"], big["ffn1_w2"], slots["ffn1_w13"], slots["ffn1_w2"], "ln_ffn1")
    u = _rmsnorm(h, small["ln_mix"], BF16, "ln_mix")
    proj = _mm(u, big["w_in"], slots["w_in"])
    parts, start = [], 0
    for n in IN_SPLIT:
        parts.append(proj[:, start:start + n])
        start += n
    z, xbc, dt_raw, cq, ckv, kr, gates = parts
    xbc = _conv_silu(xbc, conv_w, small["conv_b"].reshape(1, -1))
    xs, bm, cm = xbc[:, :SSD_DI], xbc[:, SSD_DI:SSD_DI + SSD_G * SSD_N], xbc[:, SSD_DI + SSD_G * SSD_N:]
    dt = jax.nn.softplus(dt_raw + small["dt_bias"][None, :])
    a = -jnp.exp(small["a_log"])[None, :]
    y_scan = _ssd(xs, dt, a, bm, cm)
    dsk = jnp.repeat(small["d_skip"], SSD_P)
    y_ssd = _mm(_gated_norm(y_scan, xs, z, dsk, small["ssd_norm"]), big["w_ssd_out"], slots["w_ssd_out"])
    q = _mm(_rmsnorm(cq, small["q_lora_norm"], BF16, "q_lora_norm"), big["w_uq"], slots["w_uq"])
    kv = _mm(_rmsnorm(ckv, small["kv_lora_norm"], BF16, "kv_lora_norm"), big["w_ukv"], slots["w_ukv"])
    kv = kv.reshape(s, MLA_H, NOPE + VDIM)
    k = jnp.concatenate([kv[..., :NOPE], jnp.broadcast_to(kr[:, None, :], (s, MLA_H, ROPE))], axis=-1)
    v = kv[..., NOPE:]
    q = _rmsnorm(q.reshape(s * MLA_H, QK), small["q_norm"], F32, "q_norm").reshape(s, MLA_H, QK)
    k = _rmsnorm(k.reshape(s * MLA_H, QK), small["k_norm"], F32, "k_norm").reshape(s, MLA_H, QK)
    q = jnp.concatenate([q[..., :NOPE], _rope(q[..., NOPE:], cos, sin)], axis=-1)
    k = jnp.concatenate([k[..., :NOPE], _rope(k[..., NOPE:], cos, sin)], axis=-1)
    heads_first = lambda t: jnp.transpose(t, (1, 0, 2)).astype(BF16)
    o = _attention(heads_first(q), heads_first(k), heads_first(v))
    o = jnp.transpose(o, (1, 0, 2)).reshape(s, MLA_H * VDIM)
    y_mla = _mm(o, big["w_mla_out"], slots["w_mla_out"])
    h = h + _mm(_merge(gates, y_ssd, y_mla), big["w_o"], slots["w_o"])
    return _ffn(h, small["ln_ffn2"], big["ffn2_w13"], big["ffn2_w2"], slots["ffn2_w13"], slots["ffn2_w2"], "ln_ffn2")


def _pad_w_in(w):
    return jnp.pad(w, ((0, 0), (0, D_IN_PAD - D_IN)))


def _local_step(x, positions, target, big, small, conv_w):
    inv = 1.0 / (ROPE_THETA ** (jnp.arange(0, ROPE, 2, dtype=F32) / ROPE))
    ang = positions.astype(F32)[:, None] * inv
    cos, sin = jnp.cos(ang), jnp.sin(ang)
    big = [dict(b, w_in=_pad_w_in(b["w_in"])) for b in big]
    slots = [{name: jnp.zeros(w.shape, F32) for name, w in b.items()} for b in big]

    def fwd(x_, slots_, small_, conv_w_):
        h = x_
        for l in range(DEPTH):
            h = _layer(h, big[l], slots_[l], {k: v[l] for k, v in small_.items()}, conv_w_[l], cos, sin)
        return h

    y, vjp = jax.vjp(fwd, x, slots, small, conv_w)
    loss, dy = _loss_and_grad(y, target)
    dx, d_big, d_small, d_conv_w = vjp(dy)
    d_big = [dict(d, w_in=d["w_in"][:, :D_IN]) for d in d_big]
    return loss, dx, d_big, d_small, d_conv_w


def _step(args):
    dev = 4 * lax.axis_index("x") + 2 * lax.axis_index("y") + lax.axis_index("c")
    x, positions, target = args["x"][0], args["positions"][0], args["loss_target"][0]

    big = []
    for l in range(DEPTH):
        packed = _pack_shards({name: args[name][l].astype(BF16) for name, _, _ in BIG})
        big.append(_unpack_gathered(_all_gather(packed)))
    cw = args["conv_w"]
    cw_cols = cw.shape[-1]
    cw_rows = -(-cw.size // (8 * PACK_COLS)) * 8
    cw_flat = jnp.concatenate([cw.reshape(-1), jnp.zeros((cw_rows * PACK_COLS - cw.size,), F32)]).reshape(cw_rows, PACK_COLS)
    cw_all = _all_gather(cw_flat).reshape(N_DEV, -1)[:, :cw.size].reshape(N_DEV, DEPTH, CONV_K, cw_cols)
    conv_w = jnp.transpose(cw_all, (1, 2, 0, 3)).reshape(DEPTH, CONV_K, CONV_DIM)
    small = {name: args[name] for name, _ in SMALL}

    loss, dx, d_big, d_small, d_conv_w = _local_step(x, positions, target, big, small, conv_w)
    loss = lax.psum(loss, MESH_AXES)

    out = {"loss": loss, "grad_x": dx[None]}

    grads = {name: [] for name, _, _ in BIG}
    for l in range(DEPTH):
        summed = _sum_blocks(_exchange_blocks(_pack_full_grads(d_big[l])))
        for name, g in _unpack_shards(summed).items():
            grads[name].append(g)
    for name, _, _ in BIG:
        g = jnp.stack(grads[name])
        w = args[name]
        flat = lambda t: t.reshape(-1, t.shape[-1])
        delta, m2, v2 = _adam(flat(w), flat(g), flat(args["m_" + name]), flat(args["v_" + name]))
        out["grad_" + name] = g
        out["delta_" + name] = delta.reshape(w.shape)
        out["new_m_" + name] = m2.reshape(w.shape)
        out["new_v_" + name] = v2.reshape(w.shape)

    total = _sum_blocks(_all_gather(_pack_small(d_small, d_conv_w)))
    g_small, g_conv_w = _unpack_small(total)
    zeros_cw = jnp.zeros((DEPTH, CONV_K, CONV_DIM), F32)
    delta, m2, v2 = _adam(_pack_small(small, zeros_cw), total,
                          _pack_small({name: args["m_" + name] for name, _ in SMALL}, zeros_cw),
                          _pack_small({name: args["v_" + name] for name, _ in SMALL}, zeros_cw))
    for kind, packed in (("grad_", total), ("delta_", delta), ("new_m_", m2), ("new_v_", v2)):
        for name, val in _unpack_small(packed)[0].items():
            out[kind + name] = val
    g_cw = lax.dynamic_slice_in_dim(g_conv_w, dev * cw_cols, cw_cols, axis=2)
    flat = lambda t: t.reshape(-1, t.shape[-1])
    delta, m2, v2 = _adam(flat(cw), flat(g_cw), flat(args["m_conv_w"]), flat(args["v_conv_w"]))
    out["grad_conv_w"] = g_cw
    out["delta_conv_w"] = delta.reshape(cw.shape)
    out["new_m_conv_w"] = m2.reshape(cw.shape)
    out["new_v_conv_w"] = v2.reshape(cw.shape)
    return out


WEIGHTS = ["ln_ffn1", "ffn1_w13", "ffn1_w2", "ln_mix", "w_in", "conv_w", "conv_b", "dt_bias", "a_log", "d_skip", "ssd_norm",
           "w_ssd_out", "q_lora_norm", "w_uq", "kv_lora_norm", "w_ukv", "q_norm", "k_norm", "w_mla_out", "w_o", "ln_ffn2",
           "ffn2_w13", "ffn2_w2"]
ARG_NAMES = (["x", "positions"] + WEIGHTS + ["loss_target"] + ["m_" + n for n in WEIGHTS] + ["v_" + n for n in WEIGHTS])


def kernel(x, positions, ln_ffn1, ffn1_w13, ffn1_w2, ln_mix, w_in, conv_w, conv_b, dt_bias, a_log, d_skip, ssd_norm, w_ssd_out, q_lora_norm, w_uq, kv_lora_norm, w_ukv, q_norm, k_norm, w_mla_out, w_o, ln_ffn2, ffn2_w13, ffn2_w2, loss_target, m_ln_ffn1, m_ffn1_w13, m_ffn1_w2, m_ln_mix, m_w_in, m_conv_w, m_conv_b, m_dt_bias, m_a_log, m_d_skip, m_ssd_norm, m_w_ssd_out, m_q_lora_norm, m_w_uq, m_kv_lora_norm, m_w_ukv, m_q_norm, m_k_norm, m_w_mla_out, m_w_o, m_ln_ffn2, m_ffn2_w13, m_ffn2_w2, v_ln_ffn1, v_ffn1_w13, v_ffn1_w2, v_ln_mix, v_w_in, v_conv_w, v_conv_b, v_dt_bias, v_a_log, v_d_skip, v_ssd_norm, v_w_ssd_out, v_q_lora_norm, v_w_uq, v_kv_lora_norm, v_w_ukv, v_q_norm, v_k_norm, v_w_mla_out, v_w_o, v_ln_ffn2, v_ffn2_w13, v_ffn2_w2):
    vals = (x, positions, ln_ffn1, ffn1_w13, ffn1_w2, ln_mix, w_in, conv_w, conv_b, dt_bias, a_log, d_skip, ssd_norm, w_ssd_out, q_lora_norm, w_uq, kv_lora_norm, w_ukv, q_norm, k_norm, w_mla_out, w_o, ln_ffn2, ffn2_w13, ffn2_w2, loss_target, m_ln_ffn1, m_ffn1_w13, m_ffn1_w2, m_ln_mix, m_w_in, m_conv_w, m_conv_b, m_dt_bias, m_a_log, m_d_skip, m_ssd_norm, m_w_ssd_out, m_q_lora_norm, m_w_uq, m_kv_lora_norm, m_w_ukv, m_q_norm, m_k_norm, m_w_mla_out, m_w_o, m_ln_ffn2, m_ffn2_w13, m_ffn2_w2, v_ln_ffn1, v_ffn1_w13, v_ffn1_w2, v_ln_mix, v_w_in, v_conv_w, v_conv_b, v_dt_bias, v_a_log, v_d_skip, v_ssd_norm, v_w_ssd_out, v_q_lora_norm, v_w_uq, v_kv_lora_norm, v_w_ukv, v_q_norm, v_k_norm, v_w_mla_out, v_w_o, v_ln_ffn2, v_ffn2_w13, v_ffn2_w2)
    out = _step(dict(zip(ARG_NAMES, vals)))
    order = ["loss", "grad_x"] + [k + n for k in ("grad_", "delta_", "new_m_", "new_v_") for n in WEIGHTS]
    return tuple(out[n] for n in order)
```

```python
import functools

import jax
import jax.numpy as jnp
from jax import lax
from jax.experimental import pallas as pl
from jax.experimental.pallas import tpu as pltpu

F32 = jnp.float32
BF16 = jnp.bfloat16

D_MODEL = 1024
D_FF = 2816
DEPTH = 2
SSD_DI = 2048
SSD_P = 64
SSD_H = 32
SSD_G = 4
SSD_HPG = 8
SSD_N = 128
SSD_L = 128
CONV_K = 4
CONV_DIM = 3072
MLA_H = 8
Q_LORA = 512
KV_LORA = 256
NOPE = 128
ROPE = 64
VDIM = 128
QK = 192
ROPE_THETA = 10000.0
EPS = 1e-6
IN_SPLIT = (SSD_DI, CONV_DIM, SSD_H, Q_LORA, KV_LORA, ROPE, 2 * D_MODEL)
D_IN = sum(IN_SPLIT)
D_IN_PAD = 8064
N_DEV = 8
LANE = 128
PACK_COLS = 1024

ADAM_LR = 0.001
ADAM_B1 = 0.9
ADAM_B2 = 0.999
ADAM_EPS = 1e-08
ADAM_WD = 0.01
ADAM_STEP = 10

VMEM_LIMIT = 48 * 1024 * 1024
ROW_IO_BUDGET = 8 * 1024 * 1024
NEG = -1e30

MESH_AXES = ("x", "y", "c")


def _cparams(*sem):
    return pltpu.CompilerParams(dimension_semantics=sem, vmem_limit_bytes=VMEM_LIMIT)


def _pick_tile(n, target, align):
    if n <= target:
        return n
    best = None
    for t in range(align, target + 1, align):
        if n % t == 0:
            best = t
    assert best is not None, (n, target, align)
    return best


def _acc_store(ref, val, first):
    @pl.when(first)
    def _():
        ref[...] = val

    @pl.when(jnp.logical_not(first))
    def _():
        ref[...] += val


def _row_tile(rows, tiled_cols_bytes):
    per_row = sum(tiled_cols_bytes)
    if rows <= 16:
        return rows
    t = 1024
    while t > 16 and (t * per_row > ROW_IO_BUDGET or rows % t):
        t //= 2
    assert rows % t == 0, (rows, t)
    return t


def _rowwise_call(fn, tiled, params, outs, accs, name):
    rows = tiled[0].shape[0]
    tile = _row_tile(rows, [a.shape[1] * a.dtype.itemsize for a in tiled] + [c * jnp.dtype(d).itemsize for c, d in outs])
    n_in = len(tiled) + len(params)
    n_o = len(outs)

    def body(*refs):
        vals = [r[...] for r in refs[:n_in]]
        t_out, a_out = fn(*vals)
        for r, v in zip(refs[n_in:n_in + n_o], t_out):
            r[...] = v.astype(r.dtype)
        first = pl.program_id(0) == 0
        for r, v in zip(refs[n_in + n_o:], a_out):
            _acc_store(r, v.astype(F32), first)

    in_specs = [pl.BlockSpec((tile, a.shape[1]), lambda i: (i, 0)) for a in tiled]
    in_specs += [pl.BlockSpec(p.shape, lambda i: (0, 0)) for p in params]
    out_specs = [pl.BlockSpec((tile, c), lambda i: (i, 0)) for c, _ in outs]
    out_specs += [pl.BlockSpec(s, lambda i: (0, 0)) for s in accs]
    out_shape = [jax.ShapeDtypeStruct((rows, c), d) for c, d in outs]
    out_shape += [jax.ShapeDtypeStruct(s, F32) for s in accs]
    return pl.pallas_call(
        body, grid=(rows // tile,), in_specs=in_specs, out_specs=out_specs, out_shape=out_shape,
        compiler_params=_cparams("arbitrary"), name=name,
    )(*tiled, *params)


def _rowwise_op(f, n_t, n_p, out_dtypes, name, bwd=None):
    def to_f32(vals):
        return [v.astype(F32) for v in vals]

    def call_fwd(*args):
        shapes = jax.eval_shape(f, *[jax.ShapeDtypeStruct(a.shape, F32) for a in args])
        outs = [(s.shape[1], d) for s, d in zip(shapes, out_dtypes)]
        return tuple(_rowwise_call(lambda *v: (f(*to_f32(v)), ()), args[:n_t], args[n_t:], outs, [], name + "_fwd"))

    @jax.custom_vjp
    def op(*args):
        return call_fwd(*args)

    def op_fwd(*args):
        return call_fwd(*args), args

    def op_bwd(args, gs):
        n_g = len(gs)

        def bwd_fn(*vals):
            vals = to_f32(vals)
            prim = vals[:n_t] + vals[n_t + n_g:]
            g = tuple(vals[n_t:n_t + n_g])
            if bwd is not None:
                return bwd(*prim, *g)
            _, vjp = jax.vjp(f, *prim)
            cts = vjp(g)
            return tuple(cts[:n_t]), tuple(cts[n_t:])

        outs = [(a.shape[1], a.dtype) for a in args[:n_t]]
        accs = [p.shape for p in args[n_t:]]
        return tuple(_rowwise_call(bwd_fn, list(args[:n_t]) + list(gs), args[n_t:], outs, accs, name + "_bwd"))

    op.defvjp(op_fwd, op_bwd)
    return op


def _f_rmsnorm(x, g):
    return (x * lax.rsqrt(jnp.mean(x * x, axis=-1, keepdims=True) + EPS) * g,)


def _f_swiglu(gu):
    gate, up = gu[:, :D_FF], gu[:, D_FF:]
    return (gate * jax.nn.sigmoid(gate) * up,)


def _b_swiglu(gu, d):
    gate, up = gu[:, :D_FF], gu[:, D_FF:]
    s = jax.nn.sigmoid(gate)
    d_gate = d * up * s * (1.0 + gate * (1.0 - s))
    d_up = d * gate * s
    return (jnp.concatenate([d_gate, d_up], axis=1),), ()


def _f_gated_norm(ys, xs, z, dsk, g):
    t = (ys + xs * dsk) * (z * jax.nn.sigmoid(z))
    return (t * lax.rsqrt(jnp.mean(t * t, axis=-1, keepdims=True) + EPS) * g,)


def _f_merge(gates, ys, ym):
    s = jax.nn.sigmoid(gates)
    return (s[:, :D_MODEL] * ys + s[:, D_MODEL:] * ym,)


def _b_merge(gates, ys, ym, d):
    s = jax.nn.sigmoid(gates)
    s1, s2 = s[:, :D_MODEL], s[:, D_MODEL:]
    d_gates = jnp.concatenate([d * ys * s1 * (1.0 - s1), d * ym * s2 * (1.0 - s2)], axis=1)
    return (d_gates, d * s1, d * s2), ()


def _rmsnorm(x, g, out_dtype, name):
    return _rowwise_op(_f_rmsnorm, 1, 1, [out_dtype], name)(x, g.reshape(1, -1))[0]


def _swiglu(gu):
    return _rowwise_op(_f_swiglu, 1, 0, [BF16], "swiglu", bwd=_b_swiglu)(gu)[0]


def _gated_norm(ys, xs, z, dsk, g):
    return _rowwise_op(_f_gated_norm, 3, 2, [BF16], "gated_norm")(ys, xs, z, dsk.reshape(1, -1), g.reshape(1, -1))[0]


def _merge(gates, ys, ym):
    return _rowwise_op(_f_merge, 3, 0, [BF16], "merge", bwd=_b_merge)(gates, ys, ym)[0]


def _loss_and_grad(y, target):
    def fn(yv, tv):
        d = yv - tv
        return (d * (1.0 / D_MODEL),), (jnp.sum(d * d, axis=0, keepdims=True) * (0.5 / D_MODEL),)

    dy, part = _rowwise_call(fn, [y, target], [], [(D_MODEL, F32)], [(1, D_MODEL)], "loss")
    return jnp.sum(part), dy


def _adam(w, g, m, v):
    def fn(wv, gv, mv, vv):
        m2 = ADAM_B1 * mv + (1.0 - ADAM_B1) * gv
        v2 = ADAM_B2 * vv + (1.0 - ADAM_B2) * (gv * gv)
        m_hat = m2 / (1.0 - ADAM_B1 ** ADAM_STEP)
        v_hat = v2 / (1.0 - ADAM_B2 ** ADAM_STEP)
        delta = -ADAM_LR * (m_hat / (jnp.sqrt(v_hat) + ADAM_EPS) + ADAM_WD * wv)
        return (delta, m2, v2), ()

    c = w.shape[1]
    return _rowwise_call(fn, [w, g, m, v], [], [(c, F32)] * 3, [], "adamw")


def _sum_blocks(blocks):
    _, rows, c = blocks.shape
    tile = _row_tile(rows, [N_DEV * c * blocks.dtype.itemsize, c * 4])

    def body(b_ref, o_ref):
        acc = b_ref[0].astype(F32)
        for i in range(1, N_DEV):
            acc = acc + b_ref[i].astype(F32)
        o_ref[...] = acc

    return pl.pallas_call(
        body, grid=(rows // tile,), in_specs=[pl.BlockSpec((N_DEV, tile, c), lambda i: (0, i, 0))],
        out_specs=pl.BlockSpec((tile, c), lambda i: (i, 0)), out_shape=jax.ShapeDtypeStruct((rows, c), F32),
        compiler_params=_cparams("arbitrary"), name="sum_blocks",
    )(blocks)


def _mm_call(a, b, ta, tb, out_dtype):
    r_dim, p_dim = a.shape if ta else a.shape[::-1]
    r2, q_dim = b.shape[::-1] if tb else b.shape
    assert r_dim == r2, (a.shape, b.shape, ta, tb)
    tp = _pick_tile(p_dim, 512, LANE)
    tq = _pick_tile(q_dim, 1536, LANE)
    tr = _pick_tile(r_dim, 1536, LANE)
    nr = r_dim // tr
    dims = (((0 if ta else 1,), (1 if tb else 0,)), ((), ()))

    def body(a_ref, b_ref, o_ref, *scratch):
        part = lax.dot_general(a_ref[...], b_ref[...], dims, preferred_element_type=F32)
        if nr == 1:
            o_ref[...] = part.astype(o_ref.dtype)
        else:
            acc_ref = scratch[0]
            k = pl.program_id(2)
            _acc_store(acc_ref, part, k == 0)

            @pl.when(k == nr - 1)
            def _():
                o_ref[...] = acc_ref[...].astype(o_ref.dtype)

    a_spec = pl.BlockSpec((tr, tp), lambda j, i, k: (k, i)) if ta else pl.BlockSpec((tp, tr), lambda j, i, k: (i, k))
    b_spec = pl.BlockSpec((tq, tr), lambda j, i, k: (j, k)) if tb else pl.BlockSpec((tr, tq), lambda j, i, k: (k, j))
    return pl.pallas_call(
        body, grid=(q_dim // tq, p_dim // tp, nr), in_specs=[a_spec, b_spec],
        out_specs=pl.BlockSpec((tp, tq), lambda j, i, k: (i, j)),
        out_shape=jax.ShapeDtypeStruct((p_dim, q_dim), out_dtype),
        scratch_shapes=[pltpu.VMEM((tp, tq), F32)] if nr > 1 else [],
        compiler_params=_cparams("arbitrary", "arbitrary", "arbitrary"),
        name=f"mm_{'t' if ta else 'n'}{'t' if tb else 'n'}_{p_dim}x{r_dim}x{q_dim}",
    )(a, b)


@jax.custom_vjp
def _mm(a, w, w_grad_slot):
    del w_grad_slot
    return _mm_call(a, w, False, False, F32)


def _mm_fwd(a, w, w_grad_slot):
    del w_grad_slot
    return _mm_call(a, w, False, False, F32), (a, w)


def _mm_bwd(res, g):
    a, w = res
    gb = g.astype(BF16)
    da = _mm_call(gb, w, False, True, a.dtype)
    dw = _mm_call(a, gb, True, False, BF16)
    return da, jnp.zeros_like(w), dw


_mm.defvjp(_mm_fwd, _mm_bwd)


ATTN_SCALE = QK ** -0.5
LOG2E = 1.4426950408889634
ATTN_C = ATTN_SCALE * LOG2E


def _attn_tile(s):
    return min(512, s)


def _causal_keep(t, keys_on_rows=False):
    row = lax.broadcasted_iota(jnp.int32, (t, t), 0)
    col = lax.broadcasted_iota(jnp.int32, (t, t), 1)
    return row <= col if keys_on_rows else col <= row


def _nt(a, b):
    return lax.dot_general(a, b, (((1,), (1,)), ((), ())), preferred_element_type=F32)


def _attn_fwd_call(q, k, v):
    nh, s, _ = q.shape
    t = _attn_tile(s)
    nb = s // t

    def body(q_ref, k_ref, v_ref, o_ref, lse_ref):
        qi = pl.program_id(1)
        q = q_ref[0]

        def block(kb, carry, diagonal):
            m_prev, l_prev, acc = carry
            start = pl.multiple_of(kb * t, t)
            sc = _nt(q, k_ref[0, pl.ds(start, t), :])
            if diagonal:
                sc = jnp.where(_causal_keep(t), sc, NEG)
            m_new = jnp.maximum(m_prev, jnp.max(sc, axis=-1, keepdims=True))
            p = jnp.exp2(sc * ATTN_C - m_new * ATTN_C)
            alpha = jnp.exp2((m_prev - m_new) * ATTN_C)
            l_new = alpha * l_prev + jnp.sum(p, axis=-1, keepdims=True)
            acc = alpha * acc + jnp.dot(p.astype(BF16), v_ref[0, pl.ds(start, t), :], preferred_element_type=F32)
            return m_new, l_new, acc

        init = (jnp.full((t, 1), NEG, F32), jnp.zeros((t, 1), F32), jnp.zeros((t, VDIM), F32))
        carry = lax.fori_loop(0, qi, lambda kb, c: block(kb, c, False), init)
        m, l, acc = block(qi, carry, True)
        o_ref[0] = (acc / l).astype(o_ref.dtype)
        lse_ref[0] = m * ATTN_SCALE + jnp.log(l)

    qmap = lambda h, i: (h, i, 0)
    whole = lambda h, i: (h, 0, 0)
    return pl.pallas_call(
        body, grid=(nh, nb),
        in_specs=[pl.BlockSpec((1, t, QK), qmap), pl.BlockSpec((1, s, QK), whole), pl.BlockSpec((1, s, VDIM), whole)],
        out_specs=[pl.BlockSpec((1, t, VDIM), qmap), pl.BlockSpec((1, t, 1), qmap)],
        out_shape=[jax.ShapeDtypeStruct((nh, s, VDIM), BF16), jax.ShapeDtypeStruct((nh, s, 1), F32)],
        compiler_params=_cparams("arbitrary", "arbitrary"), name="attn_fwd",
    )(q, k, v)


def _attn_delta_call(o, do):
    nh, s, d = o.shape

    def fn(ov, dv):
        return (jnp.sum(ov.astype(F32) * dv.astype(F32), axis=-1, keepdims=True),), ()

    return _rowwise_call(fn, [o.reshape(nh * s, d), do.reshape(nh * s, d)], [], [(1, F32)], [], "attn_delta")[0]


def _attn_bwd_call(q, k, v, do, lse_t, delta_t):
    nh, s, _ = q.shape
    t = _attn_tile(s)
    nb = s // t

    def body(q_ref, k_ref, v_ref, do_ref, lse_ref, delta_ref, dq_ref, dk_ref, dv_ref, dk_sc, dv_sc):
        kj = pl.program_id(1)

        @pl.when(kj == 0)
        def _():
            dq_ref[...] = jnp.zeros_like(dq_ref)

        dk_sc[...] = jnp.zeros_like(dk_sc)
        dv_sc[...] = jnp.zeros_like(dv_sc)
        kblk, vblk = k_ref[0], v_ref[0]

        def block(qb, diagonal):
            start = pl.multiple_of(qb * t, t)
            qblk = q_ref[0, pl.ds(start, t), :]
            doblk = do_ref[0, pl.ds(start, t), :]
            sc = _nt(kblk, qblk)
            if diagonal:
                sc = jnp.where(_causal_keep(t, keys_on_rows=True), sc, NEG)
            p = jnp.exp2(sc * ATTN_C - lse_ref[0, :, pl.ds(start, t)] * LOG2E)
            dv_sc[...] += jnp.dot(p.astype(BF16), doblk, preferred_element_type=F32)
            dp = _nt(vblk, doblk)
            ds = (p * (dp - delta_ref[0, :, pl.ds(start, t)])).astype(BF16)
            dk_sc[...] += jnp.dot(ds, qblk, preferred_element_type=F32)
            dq_ref[0, pl.ds(start, t), :] += lax.dot_general(ds, kblk, (((0,), (0,)), ((), ())), preferred_element_type=F32)

        block(kj, True)

        def rest(qb, carry):
            block(qb, False)
            return carry

        lax.fori_loop(kj + 1, nb, rest, 0)
        dk_ref[0] = (dk_sc[...] * ATTN_SCALE).astype(dk_ref.dtype)
        dv_ref[0] = dv_sc[...].astype(dv_ref.dtype)

        @pl.when(kj == nb - 1)
        def _():
            dq_ref[...] = dq_ref[...] * ATTN_SCALE

    kmap = lambda h, j: (h, j, 0)
    whole = lambda h, j: (h, 0, 0)
    return pl.pallas_call(
        body, grid=(nh, nb),
        in_specs=[pl.BlockSpec((1, s, QK), whole), pl.BlockSpec((1, t, QK), kmap), pl.BlockSpec((1, t, VDIM), kmap),
                  pl.BlockSpec((1, s, VDIM), whole), pl.BlockSpec((1, 1, s), whole), pl.BlockSpec((1, 1, s), whole)],
        out_specs=[pl.BlockSpec((1, s, QK), whole), pl.BlockSpec((1, t, QK), kmap), pl.BlockSpec((1, t, VDIM), kmap)],
        out_shape=[jax.ShapeDtypeStruct((nh, s, QK), F32), jax.ShapeDtypeStruct((nh, s, QK), F32), jax.ShapeDtypeStruct((nh, s, VDIM), F32)],
        scratch_shapes=[pltpu.VMEM((t, QK), F32), pltpu.VMEM((t, VDIM), F32)],
        compiler_params=_cparams("arbitrary", "arbitrary"), name="attn_bwd",
    )(q, k, v, do, lse_t, delta_t)


@jax.custom_vjp
def _attention(q, k, v):
    return _attn_fwd_call(q.astype(BF16), k.astype(BF16), v.astype(BF16))[0]


def _attention_fwd(q, k, v):
    q, k, v = q.astype(BF16), k.astype(BF16), v.astype(BF16)
    o, lse = _attn_fwd_call(q, k, v)
    return o, (q, k, v, o, lse)


def _attention_bwd(res, do):
    q, k, v, o, lse = res
    nh, s, _ = q.shape
    delta = _attn_delta_call(o, do)
    return tuple(_attn_bwd_call(q, k, v, do, lse.reshape(nh, 1, s), delta.reshape(nh, 1, s)))


_attention.defvjp(_attention_fwd, _attention_bwd)


CONV_TC = 512
HALO = 8


def _conv_tiles(s):
    return min(512, s)


def _conv_fwd_call(x, w, b):
    s, c = x.shape
    ts = _conv_tiles(s)
    hb = ts // HALO

    def body(x_ref, prev_ref, w_ref, b_ref, y_ref, buf):
        si = pl.program_id(1)
        buf[0:HALO, :] = jnp.where(si > 0, prev_ref[...], 0.0)
        buf[HALO:, :] = x_ref[...]
        acc = jnp.broadcast_to(b_ref[...], (ts, CONV_TC))
        for k in range(CONV_K):
            acc = acc + w_ref[k:k + 1, :] * buf[pl.ds(HALO - (CONV_K - 1) + k, ts), :]
        y_ref[...] = acc * jax.nn.sigmoid(acc)

    return pl.pallas_call(
        body, grid=(c // CONV_TC, s // ts),
        in_specs=[pl.BlockSpec((ts, CONV_TC), lambda ci, si: (si, ci)),
                  pl.BlockSpec((HALO, CONV_TC), lambda ci, si: (jnp.maximum(si * hb - 1, 0), ci)),
                  pl.BlockSpec((CONV_K, CONV_TC), lambda ci, si: (0, ci)),
                  pl.BlockSpec((1, CONV_TC), lambda ci, si: (0, ci))],
        out_specs=pl.BlockSpec((ts, CONV_TC), lambda ci, si: (si, ci)),
        out_shape=jax.ShapeDtypeStruct((s, c), F32),
        scratch_shapes=[pltpu.VMEM((ts + HALO, CONV_TC), F32)],
        compiler_params=_cparams("arbitrary", "arbitrary"), name="conv_fwd",
    )(x, x, w, b)


def _conv_bwd_call(x, w, b, dy):
    s, c = x.shape
    ts = _conv_tiles(s)
    hb = ts // HALO
    ns = s // ts
    last_halo = s // HALO - 1

    def body(x_ref, prev_ref, next_ref, dy_ref, dyn_ref, w_ref, b_ref, dx_ref, dw_ref, db_ref, xbuf, dbuf):
        si = pl.program_id(1)
        xbuf[0:HALO, :] = jnp.where(si > 0, prev_ref[...], 0.0)
        xbuf[HALO:HALO + ts, :] = x_ref[...]
        xbuf[HALO + ts:, :] = next_ref[...]
        pre = jnp.broadcast_to(b_ref[...], (ts + HALO, CONV_TC))
        for k in range(CONV_K):
            pre = pre + w_ref[k:k + 1, :] * xbuf[pl.ds(HALO - (CONV_K - 1) + k, ts + HALO), :]
        sg = jax.nn.sigmoid(pre)
        dsilu = sg * (1.0 + pre * (1.0 - sg))
        dbuf[0:ts, :] = dy_ref[...] * dsilu[0:ts]
        dbuf[ts:, :] = jnp.where(si < ns - 1, dyn_ref[...] * dsilu[ts:], 0.0)
        dx = jnp.zeros((ts, CONV_TC), F32)
        for k in range(CONV_K):
            dx = dx + w_ref[k:k + 1, :] * dbuf[pl.ds(CONV_K - 1 - k, ts), :]
        dx_ref[...] = dx
        dpre = dbuf[0:ts, :]
        first = si == 0
        _acc_store(db_ref, jnp.sum(dpre, axis=0, keepdims=True), first)
        for k in range(CONV_K):
            dw_k = jnp.sum(dpre * xbuf[pl.ds(HALO - (CONV_K - 1) + k, ts), :], axis=0, keepdims=True)
            _acc_store(dw_ref.at[pl.ds(k, 1), :], dw_k, first)

    main = lambda ci, si: (si, ci)
    return pl.pallas_call(
        body, grid=(c // CONV_TC, ns),
        in_specs=[pl.BlockSpec((ts, CONV_TC), main),
                  pl.BlockSpec((HALO, CONV_TC), lambda ci, si: (jnp.maximum(si * hb - 1, 0), ci)),
                  pl.BlockSpec((HALO, CONV_TC), lambda ci, si: (jnp.minimum(si * hb + hb, last_halo), ci)),
                  pl.BlockSpec((ts, CONV_TC), main),
                  pl.BlockSpec((HALO, CONV_TC), lambda ci, si: (jnp.minimum(si * hb + hb, last_halo), ci)),
                  pl.BlockSpec((CONV_K, CONV_TC), lambda ci, si: (0, ci)),
                  pl.BlockSpec((1, CONV_TC), lambda ci, si: (0, ci))],
        out_specs=[pl.BlockSpec((ts, CONV_TC), main),
                   pl.BlockSpec((CONV_K, CONV_TC), lambda ci, si: (0, ci)),
                   pl.BlockSpec((1, CONV_TC), lambda ci, si: (0, ci))],
        out_shape=[jax.ShapeDtypeStruct((s, c), F32), jax.ShapeDtypeStruct((CONV_K, c), F32), jax.ShapeDtypeStruct((1, c), F32)],
        scratch_shapes=[pltpu.VMEM((ts + 2 * HALO, CONV_TC), F32), pltpu.VMEM((ts + HALO, CONV_TC), F32)],
        compiler_params=_cparams("arbitrary", "arbitrary"), name="conv_bwd",
    )(x, x, x, dy, dy, w, b)


@jax.custom_vjp
def _conv_silu(x, w, b):
    return _conv_fwd_call(x, w, b)


def _conv_silu_fwd(x, w, b):
    return _conv_fwd_call(x, w, b), (x, w, b)


def _conv_silu_bwd(res, dy):
    x, w, b = res
    return tuple(_conv_bwd_call(x, w, b, dy))


_conv_silu.defvjp(_conv_silu_fwd, _conv_silu_bwd)


GW = SSD_HPG * SSD_P


def _ones_where(mask):
    return jnp.where(mask, 1.0, 0.0).astype(BF16)


def _split3(v):
    hi = v.astype(BF16)
    r1 = v - hi.astype(F32)
    mid = r1.astype(BF16)
    lo = (r1 - mid.astype(F32)).astype(BF16)
    return hi, mid, lo


def _dot_sel_r(v, sel):
    out = None
    for part in _split3(v):
        t = jnp.dot(part, sel, preferred_element_type=F32)
        out = t if out is None else out + t
    return out


def _dot_sel_l(sel, v):
    out = None
    for part in _split3(v):
        t = jnp.dot(sel, part, preferred_element_type=F32)
        out = t if out is None else out + t
    return out


def _ssd_consts():
    r = lax.broadcasted_iota(jnp.int32, (SSD_L, SSD_L), 0)
    c = lax.broadcasted_iota(jnp.int32, (SSD_L, SSD_L), 1)
    tril = r >= c
    triu = c >= r
    shift = SSD_P.bit_length() - 1
    eh = lax.broadcasted_iota(jnp.int32, (SSD_H, SSD_DI), 0)
    ej = lax.broadcasted_iota(jnp.int32, (SSD_H, SSD_DI), 1)
    expand = _ones_where(lax.shift_right_logical(ej, shift) == eh)
    rj = lax.broadcasted_iota(jnp.int32, (SSD_DI, SSD_H), 0)
    rh = lax.broadcasted_iota(jnp.int32, (SSD_DI, SSD_H), 1)
    reduce_ = _ones_where(lax.shift_right_logical(rj, shift) == rh)
    lane = lax.broadcasted_iota(jnp.int32, (SSD_L, LANE), 1)
    return tril, triu, expand, reduce_, lane < SSD_P


def _ssd_decays(dt, dt_t, a, a_t, tril, triu, expand):
    dta = dt * a
    acum = _dot_sel_l(_ones_where(tril), dta)
    acum_t = _dot_sel_r(dt_t * a_t, _ones_where(triu))
    dta_e = _dot_sel_r(dta, expand)
    acum_e = _dot_sel_r(acum, expand)
    last_e = jnp.sum(dta_e, axis=0, keepdims=True)
    return acum, acum_t, acum_e, last_e


def _head_decay(acum, acum_t, h, tril):
    seg = acum[:, h:h + 1] - acum_t[h:h + 1, :]
    return jnp.exp(jnp.where(tril, seg, NEG))


def _ssd_fwd_call(x, dt, a, bm, cm):
    s = x.shape[0]
    nc = s // SSD_L
    dt_t = dt.T
    a_t = a.T

    def body(x_ref, dt_ref, dtt_ref, a_ref, at_ref, b_ref, c_ref, y_ref, st_ref, s_sc):
        ci = pl.program_id(0)

        @pl.when(ci == 0)
        def _():
            s_sc[...] = jnp.zeros_like(s_sc)

        st_ref[0] = s_sc[...]
        tril, triu, expand, _, low_half = _ssd_consts()
        acum, acum_t, acum_e, last_e = _ssd_decays(dt_ref[...], dtt_ref[...], a_ref[...], at_ref[...], tril, triu, expand)
        dt_e = _dot_sel_r(dt_ref[...], expand)
        xdt = x_ref[...] * dt_e
        xdt_b = xdt.astype(BF16)
        xw_b = (xdt * jnp.exp(last_e - acum_e)).astype(BF16)
        ea_e = jnp.exp(acum_e)
        el_e = jnp.exp(last_e)
        for g in range(SSD_G):
            gs = slice(g * GW, (g + 1) * GW)
            bg = b_ref[:, g * SSD_N:(g + 1) * SSD_N]
            cg_b = c_ref[:, g * SSD_N:(g + 1) * SSD_N].astype(BF16)
            bg_b = bg.astype(BF16)
            cb = _nt(cg_b, bg_b)
            st = s_sc[:, gs]
            y_off = jnp.dot(cg_b, st.astype(BF16), preferred_element_type=F32) * ea_e[:, gs]
            for pr in range(SSD_HPG // 2):
                ls = slice(g * GW + pr * LANE, g * GW + (pr + 1) * LANE)
                xp = xdt_b[:, ls]
                yd = []
                for half in range(2):
                    h = g * SSD_HPG + pr * 2 + half
                    m = (cb * _head_decay(acum, acum_t, h, tril)).astype(BF16)
                    yd.append(jnp.dot(m, xp, preferred_element_type=F32))
                y_ref[:, ls] = jnp.where(low_half, yd[0], yd[1]) + y_off[:, pr * LANE:(pr + 1) * LANE]
            s_sc[:, gs] = st * el_e[:, gs] + jnp.dot(bg.T.astype(BF16), xw_b[:, gs], preferred_element_type=F32)

    row = lambda i: (i, 0)
    return pl.pallas_call(
        body, grid=(nc,),
        in_specs=[pl.BlockSpec((SSD_L, SSD_DI), row), pl.BlockSpec((SSD_L, SSD_H), row),
                  pl.BlockSpec((SSD_H, SSD_L), lambda i: (0, i)), pl.BlockSpec((1, SSD_H), lambda i: (0, 0)),
                  pl.BlockSpec((SSD_H, 1), lambda i: (0, 0)),
                  pl.BlockSpec((SSD_L, SSD_G * SSD_N), row), pl.BlockSpec((SSD_L, SSD_G * SSD_N), row)],
        out_specs=[pl.BlockSpec((SSD_L, SSD_DI), row), pl.BlockSpec((1, SSD_N, SSD_DI), lambda i: (i, 0, 0))],
        out_shape=[jax.ShapeDtypeStruct((s, SSD_DI), F32), jax.ShapeDtypeStruct((nc, SSD_N, SSD_DI), F32)],
        scratch_shapes=[pltpu.VMEM((SSD_N, SSD_DI), F32)],
        compiler_params=_cparams("arbitrary"), name="ssd_fwd",
    )(x, dt, dt_t, a, a_t, bm, cm)


def _ssd_bwd_call(x, dt, a, bm, cm, states, dy):
    s = x.shape[0]
    nc = s // SSD_L
    dt_t = dt.T
    a_t = a.T

    def body(x_ref, dt_ref, dtt_ref, a_ref, at_ref, b_ref, c_ref, st_ref, dy_ref,
             dx_ref, ddt_ref, db_ref, dc_ref, da_ref, ds_sc, yf_sc, dxd_sc, dxw_sc):
        i = pl.program_id(0)

        @pl.when(i == 0)
        def _():
            ds_sc[...] = jnp.zeros_like(ds_sc)

        tril, triu, expand, reduce_, low_half = _ssd_consts()
        dt = dt_ref[...]
        a_row = a_ref[...]
        acum, acum_t, acum_e, last_e = _ssd_decays(dt, dtt_ref[...], a_row, at_ref[...], tril, triu, expand)
        dt_e = _dot_sel_r(dt, expand)
        x = x_ref[...]
        xdt = x * dt_e
        xdt_b = xdt.astype(BF16)
        w_e = jnp.exp(last_e - acum_e)
        xw_b = (xdt * w_e).astype(BF16)
        ea_e = jnp.exp(acum_e)
        el_e = jnp.exp(last_e)
        dy = dy_ref[...]
        dy_b = dy.astype(BF16)
        s_prev = st_ref[0]
        ds_new = ds_sc[...]
        ds_new_b = ds_new.astype(BF16)
        triu_b = _ones_where(triu)
        strict_tril = jnp.logical_not(triu)
        head_ids = lax.broadcasted_iota(jnp.int32, (1, SSD_H), 1)
        d_dta_diag = jnp.zeros((SSD_L, SSD_H), F32)
        for g in range(SSD_G):
            gs = slice(g * GW, (g + 1) * GW)
            ns_ = slice(g * SSD_N, (g + 1) * SSD_N)
            bg = b_ref[:, ns_]
            cg = c_ref[:, ns_]
            bg_b, cg_b = bg.astype(BF16), cg.astype(BF16)
            st_b = s_prev[:, gs].astype(BF16)
            y_off = jnp.dot(cg_b, st_b, preferred_element_type=F32) * ea_e[:, gs]
            yf_sc[:, gs] = y_off
            dz_b = (dy[:, gs] * ea_e[:, gs]).astype(BF16)
            d_c = _nt(dz_b, st_b)
            ds_prev = ds_new[:, gs] * el_e[:, gs] + jnp.dot(cg.T.astype(BF16), dz_b, preferred_element_type=F32)
            dxw_sc[:, gs] = jnp.dot(bg_b, ds_new_b[:, gs], preferred_element_type=F32)
            d_b = _nt(xw_b[:, gs], ds_new_b[:, gs])
            cb = _nt(cg_b, bg_b)
            d_g = jnp.zeros((SSD_L, SSD_L), F32)
            for pr in range(SSD_HPG // 2):
                ls = slice(g * GW + pr * LANE, g * GW + (pr + 1) * LANE)
                xp = xdt_b[:, ls]
                dyp = dy[:, ls]
                dyp_b = dy_b[:, ls]
                dxd = []
                for half in range(2):
                    h = g * SSD_HPG + pr * 2 + half
                    dec = _head_decay(acum, acum_t, h, tril)
                    m = cb * dec
                    dxd.append(jnp.dot(m.T.astype(BF16), dyp_b, preferred_element_type=F32))
                    mine = low_half if half == 0 else jnp.logical_not(low_half)
                    d_m = _nt(jnp.where(mine, dyp, 0.0).astype(BF16), xp)
                    d_g = d_g + d_m * dec
                    below = jnp.dot(triu_b, (d_m * m).astype(BF16), preferred_element_type=F32)
                    col = jnp.sum(jnp.where(strict_tril, below, 0.0), axis=1, keepdims=True)
                    d_dta_diag = d_dta_diag + col * jnp.where(head_ids == h, 1.0, 0.0)
                dxd_sc[:, ls] = jnp.where(low_half, dxd[0], dxd[1])
            d_g_b = d_g.astype(BF16)
            dc_ref[:, ns_] = d_c + jnp.dot(d_g_b, bg_b, preferred_element_type=F32)
            db_ref[:, ns_] = d_b + jnp.dot(d_g.T.astype(BF16), cg_b, preferred_element_type=F32)
            ds_sc[:, gs] = ds_prev
        dxw = dxw_sc[...]
        dxd = dxd_sc[...]
        dw_e = xdt * dxw * w_e
        d_acum_e = dy * yf_sc[...] - dw_e
        d_last_e = jnp.sum(ds_new * s_prev, axis=0, keepdims=True) * el_e + jnp.sum(dw_e, axis=0, keepdims=True)
        suffix = _dot_sel_l(triu_b, d_acum_e)
        d_dta = _dot_sel_r(suffix + d_last_e, reduce_) + d_dta_diag
        dxdt = dxd + dxw * w_e
        dx_ref[...] = dxdt * dt_e
        ddt_ref[...] = d_dta * a_row + _dot_sel_r(dxdt * x, reduce_)
        _acc_store(da_ref, jnp.sum(d_dta * dt, axis=0, keepdims=True), i == 0)

    rev = lambda i: (nc - 1 - i, 0)
    return pl.pallas_call(
        body, grid=(nc,),
        in_specs=[pl.BlockSpec((SSD_L, SSD_DI), rev), pl.BlockSpec((SSD_L, SSD_H), rev),
                  pl.BlockSpec((SSD_H, SSD_L), lambda i: (0, nc - 1 - i)), pl.BlockSpec((1, SSD_H), lambda i: (0, 0)),
                  pl.BlockSpec((SSD_H, 1), lambda i: (0, 0)),
                  pl.BlockSpec((SSD_L, SSD_G * SSD_N), rev), pl.BlockSpec((SSD_L, SSD_G * SSD_N), rev),
                  pl.BlockSpec((1, SSD_N, SSD_DI), lambda i: (nc - 1 - i, 0, 0)),
                  pl.BlockSpec((SSD_L, SSD_DI), rev)],
        out_specs=[pl.BlockSpec((SSD_L, SSD_DI), rev), pl.BlockSpec((SSD_L, SSD_H), rev),
                   pl.BlockSpec((SSD_L, SSD_G * SSD_N), rev), pl.BlockSpec((SSD_L, SSD_G * SSD_N), rev),
                   pl.BlockSpec((1, SSD_H), lambda i: (0, 0))],
        out_shape=[jax.ShapeDtypeStruct((s, SSD_DI), F32), jax.ShapeDtypeStruct((s, SSD_H), F32),
                   jax.ShapeDtypeStruct((s, SSD_G * SSD_N), F32), jax.ShapeDtypeStruct((s, SSD_G * SSD_N), F32),
                   jax.ShapeDtypeStruct((1, SSD_H), F32)],
        scratch_shapes=[pltpu.VMEM((SSD_N, SSD_DI), F32), pltpu.VMEM((SSD_L, SSD_DI), F32),
                        pltpu.VMEM((SSD_L, SSD_DI), F32), pltpu.VMEM((SSD_L, SSD_DI), F32)],
        compiler_params=_cparams("arbitrary"), name="ssd_bwd",
    )(x, dt, dt_t, a, a_t, bm, cm, states, dy)


@jax.custom_vjp
def _ssd(x, dt, a, bm, cm):
    return _ssd_fwd_call(x, dt, a, bm, cm)[0]


def _ssd_vjp_fwd(x, dt, a, bm, cm):
    y, states = _ssd_fwd_call(x, dt, a, bm, cm)
    return y, (x, dt, a, bm, cm, states)


def _ssd_vjp_bwd(res, dy):
    x, dt, a, bm, cm, states = res
    dx, ddt, db, dc, da = _ssd_bwd_call(x, dt, a, bm, cm, states, dy)
    return dx, ddt, da, db, dc


_ssd.defvjp(_ssd_vjp_fwd, _ssd_vjp_bwd)


HBM_SPEC = pl.BlockSpec(memory_space=pltpu.HBM)
N_PEERS = N_DEV - 1


def _flip(v, f):
    return 1 - v if f else v


def _all_gather(shard):
    rows, c = shard.shape

    def body(x_ref, out_ref, send_sems, recv_sems, local_sem):
        x, y, cc = lax.axis_index("x"), lax.axis_index("y"), lax.axis_index("c")
        me, sibling = (x, y, cc), (x, y, 1 - cc)
        chips = [(1 - x, y), (x, 1 - y), (1 - x, 1 - y)]

        def slot(px, py, pc):
            return out_ref.at[4 * px + 2 * py + pc]

        def copy(k, block, to, src=None):
            return pltpu.make_async_remote_copy(
                src_ref=slot(*block) if src is None else src, dst_ref=slot(*block),
                send_sem=send_sems.at[k], recv_sem=recv_sems.at[k],
                device_id=to, device_id_type=pl.DeviceIdType.MESH)

        mine = pltpu.make_async_copy(x_ref, slot(*me), local_sem)
        mine.start()
        first = [copy(0, me, sibling, src=x_ref)]
        first += [copy(1 + j, me, (*chip, cc), src=x_ref) for j, chip in enumerate(chips)]
        for cp in first:
            cp.start()
        passed = [copy(4 + j, (*chip, cc), sibling) for j, chip in enumerate(chips)]
        for j, chip in enumerate(chips):
            copy(1 + j, (*chip, cc), me).wait_recv()
            passed[j].start()
        copy(0, sibling, me).wait_recv()
        for j, chip in enumerate(chips):
            copy(4 + j, (*chip, 1 - cc), me).wait_recv()
        for cp in first + passed:
            cp.wait_send()
        mine.wait()

    return pl.pallas_call(
        body, out_shape=jax.ShapeDtypeStruct((N_DEV, rows, c), shard.dtype),
        in_specs=[HBM_SPEC], out_specs=HBM_SPEC,
        scratch_shapes=[pltpu.SemaphoreType.DMA((N_PEERS,)), pltpu.SemaphoreType.DMA((N_PEERS,)), pltpu.SemaphoreType.DMA(())],
        name="all_gather",
    )(shard)


def _exchange_blocks(blocks):
    _, rows, c = blocks.shape

    def body(g_ref, out_ref, send_sems, recv_sems, local_sem):
        x, y, cc = lax.axis_index("x"), lax.axis_index("y"), lax.axis_index("c")
        me = 4 * x + 2 * y + cc
        mine = pltpu.make_async_copy(g_ref.at[me], out_ref.at[me], local_sem)
        mine.start()
        copies = []
        for k in range(1, N_DEV):
            px, py, pc = _flip(x, k & 4), _flip(y, k & 2), _flip(cc, k & 1)
            peer = 4 * px + 2 * py + pc
            copies.append((
                pltpu.make_async_remote_copy(
                    src_ref=g_ref.at[peer], dst_ref=out_ref.at[me], send_sem=send_sems.at[k - 1], recv_sem=recv_sems.at[k - 1],
                    device_id=(px, py, pc), device_id_type=pl.DeviceIdType.MESH),
                pltpu.make_async_remote_copy(
                    src_ref=g_ref.at[peer], dst_ref=out_ref.at[peer], send_sem=send_sems.at[k - 1], recv_sem=recv_sems.at[k - 1],
                    device_id=(px, py, pc), device_id_type=pl.DeviceIdType.MESH)))
        for send, _ in copies:
            send.start()
        for _, landed in copies:
            landed.wait_recv()
        for send, _ in copies:
            send.wait_send()
        mine.wait()

    return pl.pallas_call(
        body, out_shape=jax.ShapeDtypeStruct(blocks.shape, blocks.dtype),
        in_specs=[HBM_SPEC], out_specs=HBM_SPEC,
        scratch_shapes=[pltpu.SemaphoreType.DMA((N_PEERS,)), pltpu.SemaphoreType.DMA((N_PEERS,)), pltpu.SemaphoreType.DMA(())],
        name="exchange_blocks",
    )(blocks)


BIG = [
    ("ffn1_w13", (D_MODEL, 2 * D_FF), 1), ("ffn1_w2", (D_FF, D_MODEL), 0), ("w_in", (D_MODEL, D_IN), 1),
    ("w_ssd_out", (SSD_DI, D_MODEL), 0), ("w_uq", (Q_LORA, MLA_H * QK), 1), ("w_ukv", (KV_LORA, MLA_H * (NOPE + VDIM)), 1),
    ("w_mla_out", (MLA_H * VDIM, D_MODEL), 0), ("w_o", (D_MODEL, D_MODEL), 0),
    ("ffn2_w13", (D_MODEL, 2 * D_FF), 1), ("ffn2_w2", (D_FF, D_MODEL), 0),
]
SMALL = [
    ("ln_ffn1", D_MODEL), ("ln_mix", D_MODEL), ("conv_b", CONV_DIM), ("dt_bias", SSD_H), ("a_log", SSD_H), ("d_skip", SSD_H),
    ("ssd_norm", SSD_DI), ("q_lora_norm", Q_LORA), ("kv_lora_norm", KV_LORA), ("q_norm", QK), ("k_norm", QK), ("ln_ffn2", D_MODEL),
]


def _shard_shape(full, axis):
    k, n = full
    return (k // N_DEV, n) if axis == 0 else (k, n // N_DEV)


def _shard_rows(full):
    return full[0] * full[1] // N_DEV // PACK_COLS


LAYER_ROWS = sum(_shard_rows(f) for _, f, _ in BIG)
LAYER_ROWS_PAD = -(-LAYER_ROWS // 256) * 256


def _pack_shards(shards):
    parts = [shards[name].reshape(-1, PACK_COLS) for name, _, _ in BIG]
    pad = LAYER_ROWS_PAD - LAYER_ROWS
    if pad:
        parts.append(jnp.zeros((pad, PACK_COLS), parts[0].dtype))
    return jnp.concatenate(parts, axis=0)


def _unpack_shards(packed):
    out, r = {}, 0
    for name, full, axis in BIG:
        n = _shard_rows(full)
        out[name] = packed[r:r + n].reshape(_shard_shape(full, axis))
        r += n
    return out


def _unpack_gathered(gathered):
    out, r = {}, 0
    for name, full, axis in BIG:
        n = _shard_rows(full)
        blk = gathered[:, r:r + n].reshape((N_DEV,) + _shard_shape(full, axis))
        out[name] = blk.reshape(full) if axis == 0 else jnp.transpose(blk, (1, 0, 2)).reshape(full)
        r += n
    return out


def _pack_full_grads(grads):
    parts = []
    for name, full, axis in BIG:
        g = grads[name]
        k, n = _shard_shape(full, axis)
        blk = g.reshape(N_DEV, k, n) if axis == 0 else jnp.transpose(g.reshape(k, N_DEV, n), (1, 0, 2))
        parts.append(blk.reshape(N_DEV, -1, PACK_COLS))
    pad = LAYER_ROWS_PAD - LAYER_ROWS
    if pad:
        parts.append(jnp.zeros((N_DEV, pad, PACK_COLS), parts[0].dtype))
    return jnp.concatenate(parts, axis=1)


SMALL_COLS = sum(n for _, n in SMALL) + CONV_K * CONV_DIM
SMALL_ROWS = -(-(DEPTH * SMALL_COLS) // (8 * PACK_COLS)) * 8


def _pack_small(vals, conv_w):
    flat = jnp.concatenate([vals[name] for name, _ in SMALL] + [conv_w.reshape(DEPTH, -1)], axis=1).reshape(-1)
    flat = jnp.concatenate([flat, jnp.zeros((SMALL_ROWS * PACK_COLS - flat.shape[0],), F32)])
    return flat.reshape(SMALL_ROWS, PACK_COLS)


def _unpack_small(packed):
    flat = packed.reshape(-1)[:DEPTH * SMALL_COLS].reshape(DEPTH, SMALL_COLS)
    out, c = {}, 0
    for name, n in SMALL:
        out[name] = flat[:, c:c + n]
        c += n
    return out, flat[:, c:].reshape(DEPTH, CONV_K, CONV_DIM)


def _rope(t, cos, sin):
    half = ROPE // 2
    t1, t2 = t[..., :half], t[..., half:]
    c, s = cos[:, None, :], sin[:, None, :]
    return jnp.concatenate([t1 * c - t2 * s, t2 * c + t1 * s], axis=-1)


def _ffn(h, ln, w13, w2, slot13, slot2, name):
    n = _rmsnorm(h, ln, BF16, name)
    act = _swiglu(_mm(n, w13, slot13))
    return h + 0.5 * _mm(act, w2, slot2)


def _layer(h, big, slots, small, conv_w, cos, sin):
    s = h.shape[0]
    h = _ffn(h, small["ln_ffn1"], big["ffn1_w13"], big["ffn1_w2"], slots["ffn1_w13"], slots["ffn1_w2"], "ln_ffn1")
    u = _rmsnorm(h, small["ln_mix"], BF16, "ln_mix")
    proj = _mm(u, big["w_in"], slots["w_in"])
    parts, start = [], 0
    for n in IN_SPLIT:
        parts.append(proj[:, start:start + n])
        start += n
    z, xbc, dt_raw, cq, ckv, kr, gates = parts
    xbc = _conv_silu(xbc, conv_w, small["conv_b"].reshape(1, -1))
    xs, bm, cm = xbc[:, :SSD_DI], xbc[:, SSD_DI:SSD_DI + SSD_G * SSD_N], xbc[:, SSD_DI + SSD_G * SSD_N:]
    dt = jax.nn.softplus(dt_raw + small["dt_bias"][None, :])
    a = -jnp.exp(small["a_log"])[None, :]
    y_scan = _ssd(xs, dt, a, bm, cm)
    dsk = jnp.repeat(small["d_skip"], SSD_P)
    y_ssd = _mm(_gated_norm(y_scan, xs, z, dsk, small["ssd_norm"]), big["w_ssd_out"], slots["w_ssd_out"])
    q = _mm(_rmsnorm(cq, small["q_lora_norm"], BF16, "q_lora_norm"), big["w_uq"], slots["w_uq"])
    kv = _mm(_rmsnorm(ckv, small["kv_lora_norm"], BF16, "kv_lora_norm"), big["w_ukv"], slots["w_ukv"])
    kv = kv.reshape(s, MLA_H, NOPE + VDIM)
    k = jnp.concatenate([kv[..., :NOPE], jnp.broadcast_to(kr[:, None, :], (s, MLA_H, ROPE))], axis=-1)
    v = kv[..., NOPE:]
    q = _rmsnorm(q.reshape(s * MLA_H, QK), small["q_norm"], F32, "q_norm").reshape(s, MLA_H, QK)
    k = _rmsnorm(k.reshape(s * MLA_H, QK), small["k_norm"], F32, "k_norm").reshape(s, MLA_H, QK)
    q = jnp.concatenate([q[..., :NOPE], _rope(q[..., NOPE:], cos, sin)], axis=-1)
    k = jnp.concatenate([k[..., :NOPE], _rope(k[..., NOPE:], cos, sin)], axis=-1)
    heads_first = lambda t: jnp.transpose(t, (1, 0, 2))
    o = _attention(heads_first(q), heads_first(k), heads_first(v))
    o = jnp.transpose(o, (1, 0, 2)).reshape(s, MLA_H * VDIM)
    y_mla = _mm(o, big["w_mla_out"], slots["w_mla_out"])
    h = h + _mm(_merge(gates, y_ssd, y_mla), big["w_o"], slots["w_o"])
    return _ffn(h, small["ln_ffn2"], big["ffn2_w13"], big["ffn2_w2"], slots["ffn2_w13"], slots["ffn2_w2"], "ln_ffn2")


def _pad_w_in(w):
    return jnp.pad(w, ((0, 0), (0, D_IN_PAD - D_IN)))


def _local_step(x, positions, target, big, small, conv_w):
    inv = 1.0 / (ROPE_THETA ** (jnp.arange(0, ROPE, 2, dtype=F32) / ROPE))
    ang = positions.astype(F32)[:, None] * inv
    cos, sin = jnp.cos(ang), jnp.sin(ang)
    big = [dict(b, w_in=_pad_w_in(b["w_in"])) for b in big]
    slots = [{name: jnp.zeros(w.shape, BF16) for name, w in b.items()} for b in big]

    def fwd(x_, slots_, small_, conv_w_):
        h = x_
        for l in range(DEPTH):
            h = _layer(h, big[l], slots_[l], {k: v[l] for k, v in small_.items()}, conv_w_[l], cos, sin)
        return h

    y, vjp = jax.vjp(fwd, x, slots, small, conv_w)
    loss, dy = _loss_and_grad(y, target)
    dx, d_big, d_small, d_conv_w = vjp(dy)
    d_big = [dict(d, w_in=d["w_in"][:, :D_IN]) for d in d_big]
    return loss, dx, d_big, d_small, d_conv_w


def _step(args):
    dev = 4 * lax.axis_index("x") + 2 * lax.axis_index("y") + lax.axis_index("c")
    x, positions, target = args["x"][0], args["positions"][0], args["loss_target"][0]

    big = []
    for l in range(DEPTH):
        packed = _pack_shards({name: args[name][l].astype(BF16) for name, _, _ in BIG})
        big.append(_unpack_gathered(_all_gather(packed)))
    cw = args["conv_w"]
    cw_cols = cw.shape[-1]
    cw_rows = -(-cw.size // (8 * PACK_COLS)) * 8
    cw_flat = jnp.concatenate([cw.reshape(-1), jnp.zeros((cw_rows * PACK_COLS - cw.size,), F32)]).reshape(cw_rows, PACK_COLS)
    cw_all = _all_gather(cw_flat).reshape(N_DEV, -1)[:, :cw.size].reshape(N_DEV, DEPTH, CONV_K, cw_cols)
    conv_w = jnp.transpose(cw_all, (1, 2, 0, 3)).reshape(DEPTH, CONV_K, CONV_DIM)
    small = {name: args[name] for name, _ in SMALL}

    loss, dx, d_big, d_small, d_conv_w = _local_step(x, positions, target, big, small, conv_w)
    loss = lax.psum(loss, MESH_AXES)

    out = {"loss": loss, "grad_x": dx[None]}

    grads = {name: [] for name, _, _ in BIG}
    for l in range(DEPTH):
        summed = _sum_blocks(_exchange_blocks(_pack_full_grads(d_big[l])))
        for name, g in _unpack_shards(summed).items():
            grads[name].append(g)
    for name, _, _ in BIG:
        g = jnp.stack(grads[name])
        w = args[name]
        flat = lambda t: t.reshape(-1, t.shape[-1])
        delta, m2, v2 = _adam(flat(w), flat(g), flat(args["m_" + name]), flat(args["v_" + name]))
        out["grad_" + name] = g
        out["delta_" + name] = delta.reshape(w.shape)
        out["new_m_" + name] = m2.reshape(w.shape)
        out["new_v_" + name] = v2.reshape(w.shape)

    total = _sum_blocks(_all_gather(_pack_small(d_small, d_conv_w)))
    g_small, g_conv_w = _unpack_small(total)
    zeros_cw = jnp.zeros((DEPTH, CONV_K, CONV_DIM), F32)
    delta, m2, v2 = _adam(_pack_small(small, zeros_cw), total,
                          _pack_small({name: args["m_" + name] for name, _ in SMALL}, zeros_cw),
                          _pack_small({name: args["v_" + name] for name, _ in SMALL}, zeros_cw))
    for kind, packed in (("grad_", total), ("delta_", delta), ("new_m_", m2), ("new_v_", v2)):
        for name, val in _unpack_small(packed)[0].items():
            out[kind + name] = val
    g_cw = lax.dynamic_slice_in_dim(g_conv_w, dev * cw_cols, cw_cols, axis=2)
    flat = lambda t: t.reshape(-1, t.shape[-1])
    delta, m2, v2 = _adam(flat(cw), flat(g_cw), flat(args["m_conv_w"]), flat(args["v_conv_w"]))
    out["grad_conv_w"] = g_cw
    out["delta_conv_w"] = delta.reshape(cw.shape)
    out["new_m_conv_w"] = m2.reshape(cw.shape)
    out["new_v_conv_w"] = v2.reshape(cw.shape)
    return out


WEIGHTS = ["ln_ffn1", "ffn1_w13", "ffn1_w2", "ln_mix", "w_in", "conv_w", "conv_b", "dt_bias", "a_log", "d_skip", "ssd_norm",
           "w_ssd_out", "q_lora_norm", "w_uq", "kv_lora_norm", "w_ukv", "q_norm", "k_norm", "w_mla_out", "w_o", "ln_ffn2",
           "ffn2_w13", "ffn2_w2"]
ARG_NAMES = (["x", "positions"] + WEIGHTS + ["loss_target"] + ["m_" + n for n in WEIGHTS] + ["v_" + n for n in WEIGHTS])


def kernel(x, positions, ln_ffn1, ffn1_w13, ffn1_w2, ln_mix, w_in, conv_w, conv_b, dt_bias, a_log, d_skip, ssd_norm, w_ssd_out, q_lora_norm, w_uq, kv_lora_norm, w_ukv, q_norm, k_norm, w_mla_out, w_o, ln_ffn2, ffn2_w13, ffn2_w2, loss_target, m_ln_ffn1, m_ffn1_w13, m_ffn1_w2, m_ln_mix, m_w_in, m_conv_w, m_conv_b, m_dt_bias, m_a_log, m_d_skip, m_ssd_norm, m_w_ssd_out, m_q_lora_norm, m_w_uq, m_kv_lora_norm, m_w_ukv, m_q_norm, m_k_norm, m_w_mla_out, m_w_o, m_ln_ffn2, m_ffn2_w13, m_ffn2_w2, v_ln_ffn1, v_ffn1_w13, v_ffn1_w2, v_ln_mix, v_w_in, v_conv_w, v_conv_b, v_dt_bias, v_a_log, v_d_skip, v_ssd_norm, v_w_ssd_out, v_q_lora_norm, v_w_uq, v_kv_lora_norm, v_w_ukv, v_q_norm, v_k_norm, v_w_mla_out, v_w_o, v_ln_ffn2, v_ffn2_w13, v_ffn2_w2):
    vals = (x, positions, ln_ffn1, ffn1_w13, ffn1_w2, ln_mix, w_in, conv_w, conv_b, dt_bias, a_log, d_skip, ssd_norm, w_ssd_out, q_lora_norm, w_uq, kv_lora_norm, w_ukv, q_norm, k_norm, w_mla_out, w_o, ln_ffn2, ffn2_w13, ffn2_w2, loss_target, m_ln_ffn1, m_ffn1_w13, m_ffn1_w2, m_ln_mix, m_w_in, m_conv_w, m_conv_b, m_dt_bias, m_a_log, m_d_skip, m_ssd_norm, m_w_ssd_out, m_q_lora_norm, m_w_uq, m_kv_lora_norm, m_w_ukv, m_q_norm, m_k_norm, m_w_mla_out, m_w_o, m_ln_ffn2, m_ffn2_w13, m_ffn2_w2, v_ln_ffn1, v_ffn1_w13, v_ffn1_w2, v_ln_mix, v_w_in, v_conv_w, v_conv_b, v_dt_bias, v_a_log, v_d_skip, v_ssd_norm, v_w_ssd_out, v_q_lora_norm, v_w_uq, v_kv_lora_norm, v_w_ukv, v_q_norm, v_k_norm, v_w_mla_out, v_w_o, v_ln_ffn2, v_ffn2_w13, v_ffn2_w2)
    out = _step(dict(zip(ARG_NAMES, vals)))
    order = ["loss", "grad_x"] + [k + n for k in ("grad_", "delta_", "new_m_", "new_v_") for n in WEIGHTS]
    return tuple(out[n] for n in order)
```

```python
import jax
import jax.numpy as jnp
from jax import lax
from jax.experimental import pallas as pl
from jax.experimental.pallas import tpu as pltpu

F32 = jnp.float32
BF16 = jnp.bfloat16

D_MODEL = 1024
D_FF = 2816
DEPTH = 2
SSD_DI = 2048
SSD_P = 64
SSD_H = 32
SSD_G = 4
SSD_HPG = 8
SSD_N = 128
SSD_L = 128
CONV_K = 4
CONV_DIM = 3072
MLA_H = 8
Q_LORA = 512
KV_LORA = 256
NOPE = 128
ROPE = 64
VDIM = 128
QK = 192
ROPE_THETA = 10000.0
EPS = 1e-6
IN_SPLIT = (SSD_DI, CONV_DIM, SSD_H, Q_LORA, KV_LORA, ROPE, 2 * D_MODEL)
D_IN = sum(IN_SPLIT)
N_DEV = 8
LANE = 128
PACK_COLS = 1024

PROJ_Z = 0
PROJ_GATES = PROJ_Z + SSD_DI
PROJ_XBC = PROJ_GATES + 2 * D_MODEL
PROJ_CQ = PROJ_XBC + CONV_DIM
PROJ_CKV = PROJ_CQ + Q_LORA
PROJ_LAST = PROJ_CKV + KV_LORA
D_IN_PAD = PROJ_LAST + LANE

ADAM_LR = 0.001
ADAM_B1 = 0.9
ADAM_B2 = 0.999
ADAM_EPS = 1e-08
ADAM_WD = 0.01
ADAM_STEP = 10

VMEM_LIMIT = 48 * 1024 * 1024
ROW_IO_BUDGET = 8 * 1024 * 1024
NEG = -1e30

MESH_AXES = ("x", "y", "c")


def _cparams(*sem):
    return pltpu.CompilerParams(dimension_semantics=sem, vmem_limit_bytes=VMEM_LIMIT)


def _pick_tile(n, target, align):
    if n <= target:
        return n
    best = None
    for t in range(align, target + 1, align):
        if n % t == 0:
            best = t
    assert best is not None, (n, target, align)
    return best


def _acc_store(ref, val, first):
    @pl.when(first)
    def _():
        ref[...] = val

    @pl.when(jnp.logical_not(first))
    def _():
        ref[...] += val


def _win(arr, start, width):
    assert start % width == 0, (start, width)
    return (arr, start, width)


def _operand(entry):
    if isinstance(entry, tuple):
        arr, start, width = entry
        return arr, width, start // width
    return entry, entry.shape[1], 0


def _row_tile(rows, bytes_per_row):
    if rows <= 16:
        return rows
    t = 1024
    while t > 16 and (t * bytes_per_row > ROW_IO_BUDGET or rows % t):
        t //= 2
    assert rows % t == 0, (rows, t)
    return t


def _rowwise_call(fn, tiled, params, outs, accs, name):
    ops = [_operand(e) for e in tiled]
    rows = ops[0][0].shape[0]
    per_row = sum(w * a.dtype.itemsize for a, w, _ in ops) + sum(c * jnp.dtype(d).itemsize for c, d in outs)
    tile = _row_tile(rows, per_row)
    n_in = len(tiled) + len(params)
    n_o = len(outs)

    def body(*refs):
        vals = [r[...] for r in refs[:n_in]]
        t_out, a_out = fn(*vals)
        for r, v in zip(refs[n_in:n_in + n_o], t_out):
            r[...] = v.astype(r.dtype)
        first = pl.program_id(0) == 0
        for r, v in zip(refs[n_in + n_o:], a_out):
            _acc_store(r, v.astype(F32), first)

    def tiled_spec(width, blk):
        return pl.BlockSpec((tile, width), lambda i: (i, blk))

    in_specs = [tiled_spec(w, blk) for _, w, blk in ops]
    in_specs += [pl.BlockSpec(p.shape, lambda i: (0, 0)) for p in params]
    out_specs = [tiled_spec(c, 0) for c, _ in outs]
    out_specs += [pl.BlockSpec(s, lambda i: (0, 0)) for s in accs]
    out_shape = [jax.ShapeDtypeStruct((rows, c), d) for c, d in outs]
    out_shape += [jax.ShapeDtypeStruct(s, F32) for s in accs]
    return pl.pallas_call(
        body, grid=(rows // tile,), in_specs=in_specs, out_specs=out_specs, out_shape=out_shape,
        compiler_params=_cparams("arbitrary"), name=name,
    )(*[a for a, _, _ in ops], *params)


def _to_f32(vals):
    return [v.astype(F32) for v in vals]


def _row_fwd(f, tiled, params, out_dtypes, name):
    ops = [_operand(e) for e in tiled]
    rows = ops[0][0].shape[0]
    shapes = jax.eval_shape(f, *[jax.ShapeDtypeStruct((rows, w), F32) for _, w, _ in ops],
                            *[jax.ShapeDtypeStruct(p.shape, F32) for p in params])
    outs = [(s.shape[1], d) for s, d in zip(shapes, out_dtypes)]
    return _rowwise_call(lambda *v: (f(*_to_f32(v)), ()), tiled, params, outs, [], name)


def _row_bwd(f, tiled, params, gs, d_dtypes, name, bwd=None, add=None):
    n_t, n_g = len(tiled), len(gs)
    adds = sorted((add or {}).items())
    n_a = len(adds)

    def fn(*vals):
        vals = _to_f32(vals)
        prim = vals[:n_t] + vals[n_t + n_g + n_a:]
        g = tuple(vals[n_t:n_t + n_g])
        if bwd is not None:
            d_t, d_p = bwd(*prim, *g)
        else:
            _, vjp = jax.vjp(f, *prim)
            cts = vjp(g)
            d_t, d_p = cts[:n_t], cts[n_t:]
        d_t = list(d_t)
        for (idx, _), extra in zip(adds, vals[n_t + n_g:n_t + n_g + n_a]):
            d_t[idx] = d_t[idx] + extra
        return tuple(d_t), tuple(d_p)

    outs = [(_operand(e)[1], d) for e, d in zip(tiled, d_dtypes)]
    accs = [p.shape for p in params]
    res = _rowwise_call(fn, list(tiled) + list(gs) + [a for _, a in adds], params, outs, accs, name)
    return res[:n_t], res[n_t:]


def _rowwise_op(f, n_t, out_dtypes, name):
    @jax.custom_vjp
    def op(*args):
        return tuple(_row_fwd(f, args[:n_t], args[n_t:], out_dtypes, name + "_fwd"))

    def op_fwd(*args):
        return op(*args), args

    def op_bwd(args, gs):
        d_t, d_p = _row_bwd(f, args[:n_t], args[n_t:], gs, [a.dtype for a in args[:n_t]], name + "_bwd")
        return tuple(d_t) + tuple(d_p)

    op.defvjp(op_fwd, op_bwd)
    return op


def _f_rmsnorm(x, g):
    return (x * lax.rsqrt(jnp.mean(x * x, axis=-1, keepdims=True) + EPS) * g,)


def _f_swiglu(gu):
    gate, up = gu[:, :D_FF], gu[:, D_FF:]
    return (gate * jax.nn.sigmoid(gate) * up,)


def _b_swiglu(gu, d):
    gate, up = gu[:, :D_FF], gu[:, D_FF:]
    s = jax.nn.sigmoid(gate)
    d_gate = d * up * s * (1.0 + gate * (1.0 - s))
    d_up = d * gate * s
    return (jnp.concatenate([d_gate, d_up], axis=1),), ()


def _f_gated_norm(ys, xs, z, dsk, g):
    t = (ys + xs * dsk) * (z * jax.nn.sigmoid(z))
    return (t * lax.rsqrt(jnp.mean(t * t, axis=-1, keepdims=True) + EPS) * g,)


def _f_merge(gates, ys, ym):
    s = jax.nn.sigmoid(gates)
    return (s[:, :D_MODEL] * ys + s[:, D_MODEL:] * ym,)


def _b_merge(gates, ys, ym, d):
    s = jax.nn.sigmoid(gates)
    s1, s2 = s[:, :D_MODEL], s[:, D_MODEL:]
    d_gates = jnp.concatenate([d * ys * s1 * (1.0 - s1), d * ym * s2 * (1.0 - s2)], axis=1)
    return (d_gates, d * s1, d * s2), ()


def _head_rmsnorm(x, g, name):
    return _rowwise_op(_f_rmsnorm, 1, [F32], name)(x, g.reshape(1, -1))[0]


def _loss_and_grad(y, target):
    def fn(yv, tv):
        d = yv - tv
        return (d * (1.0 / D_MODEL),), (jnp.sum(d * d, axis=0, keepdims=True) * (0.5 / D_MODEL),)

    dy, part = _rowwise_call(fn, [y, target], [], [(D_MODEL, F32)], [(1, D_MODEL)], "loss")
    return jnp.sum(part), dy


def _adam(w, g, m, v):
    def fn(wv, gv, mv, vv):
        m2 = ADAM_B1 * mv + (1.0 - ADAM_B1) * gv
        v2 = ADAM_B2 * vv + (1.0 - ADAM_B2) * (gv * gv)
        m_hat = m2 / (1.0 - ADAM_B1 ** ADAM_STEP)
        v_hat = v2 / (1.0 - ADAM_B2 ** ADAM_STEP)
        delta = -ADAM_LR * (m_hat / (jnp.sqrt(v_hat) + ADAM_EPS) + ADAM_WD * wv)
        return (delta, m2, v2), ()

    c = w.shape[1]
    return _rowwise_call(fn, [w, g, m, v], [], [(c, F32)] * 3, [], "adamw")


def _sum_blocks(blocks):
    _, rows, c = blocks.shape
    tile = _row_tile(rows, N_DEV * c * blocks.dtype.itemsize + c * 4)

    def body(b_ref, o_ref):
        acc = b_ref[0].astype(F32)
        for i in range(1, N_DEV):
            acc = acc + b_ref[i].astype(F32)
        o_ref[...] = acc

    return pl.pallas_call(
        body, grid=(rows // tile,), in_specs=[pl.BlockSpec((N_DEV, tile, c), lambda i: (0, i, 0))],
        out_specs=pl.BlockSpec((tile, c), lambda i: (i, 0)), out_shape=jax.ShapeDtypeStruct((rows, c), F32),
        compiler_params=_cparams("arbitrary"), name="sum_blocks",
    )(blocks)


def _mm(a, b, ta=False, tb=False, out_dtype=F32, alpha=1.0, res=None):
    r_dim, p_dim = a.shape if ta else a.shape[::-1]
    r2, q_dim = b.shape[::-1] if tb else b.shape
    assert r_dim == r2, (a.shape, b.shape, ta, tb)
    tp = _pick_tile(p_dim, 512, LANE)
    if tp < 512 < p_dim:
        tp = _pick_tile(p_dim, 1536, LANE)
    tq = _pick_tile(q_dim, 1536, LANE)
    tr = _pick_tile(r_dim, 1536, LANE)
    nr = r_dim // tr
    dims = (((0 if ta else 1,), (1 if tb else 0,)), ((), ()))
    has_res = res is not None

    def body(*refs):
        a_ref, b_ref = refs[:2]
        res_ref = refs[2] if has_res else None
        o_ref = refs[2 + has_res]

        def finish(val):
            if alpha != 1.0:
                val = val * alpha
            if has_res:
                val = val + res_ref[...].astype(F32)
            o_ref[...] = val.astype(o_ref.dtype)

        part = lax.dot_general(a_ref[...].astype(BF16), b_ref[...].astype(BF16), dims, preferred_element_type=F32)
        if nr == 1:
            finish(part)
        else:
            acc_ref = refs[3 + has_res]
            k = pl.program_id(2)
            _acc_store(acc_ref, part, k == 0)

            @pl.when(k == nr - 1)
            def _():
                finish(acc_ref[...])

    a_spec = pl.BlockSpec((tr, tp), lambda j, i, k: (k, i)) if ta else pl.BlockSpec((tp, tr), lambda j, i, k: (i, k))
    b_spec = pl.BlockSpec((tq, tr), lambda j, i, k: (j, k)) if tb else pl.BlockSpec((tr, tq), lambda j, i, k: (k, j))
    o_spec = pl.BlockSpec((tp, tq), lambda j, i, k: (i, j))
    return pl.pallas_call(
        body, grid=(q_dim // tq, p_dim // tp, nr), in_specs=[a_spec, b_spec] + ([o_spec] if has_res else []),
        out_specs=o_spec, out_shape=jax.ShapeDtypeStruct((p_dim, q_dim), out_dtype),
        scratch_shapes=[pltpu.VMEM((tp, tq), F32)] if nr > 1 else [],
        compiler_params=_cparams("arbitrary", "arbitrary", "arbitrary"),
        name=f"mm_{'t' if ta else 'n'}{'t' if tb else 'n'}_{p_dim}x{r_dim}x{q_dim}",
    )(*([a, b] + ([res] if has_res else [])))


ATTN_SCALE = QK ** -0.5
LOG2E = 1.4426950408889634
ATTN_C = ATTN_SCALE * LOG2E


def _attn_tile(s):
    return min(512, s)


def _causal_keep(t, keys_on_rows=False):
    row = lax.broadcasted_iota(jnp.int32, (t, t), 0)
    col = lax.broadcasted_iota(jnp.int32, (t, t), 1)
    return row <= col if keys_on_rows else col <= row


def _nt(a, b):
    return lax.dot_general(a, b, (((1,), (1,)), ((), ())), preferred_element_type=F32)


def _attn_fwd_call(q, k, v):
    nh, s, _ = q.shape
    t = _attn_tile(s)
    nb = s // t

    def body(q_ref, k_ref, v_ref, o_ref, lse_ref):
        qi = pl.program_id(1)
        q = q_ref[0]

        def block(kb, carry, diagonal):
            m_prev, l_prev, acc = carry
            start = pl.multiple_of(kb * t, t)
            sc = _nt(q, k_ref[0, pl.ds(start, t), :])
            if diagonal:
                sc = jnp.where(_causal_keep(t), sc, NEG)
            m_new = jnp.maximum(m_prev, jnp.max(sc, axis=-1, keepdims=True))
            p = jnp.exp2(sc * ATTN_C - m_new * ATTN_C)
            alpha = jnp.exp2((m_prev - m_new) * ATTN_C)
            l_new = alpha * l_prev + jnp.sum(p, axis=-1, keepdims=True)
            acc = alpha * acc + jnp.dot(p.astype(BF16), v_ref[0, pl.ds(start, t), :], preferred_element_type=F32)
            return m_new, l_new, acc

        init = (jnp.full((t, 1), NEG, F32), jnp.zeros((t, 1), F32), jnp.zeros((t, VDIM), F32))
        carry = lax.fori_loop(0, qi, lambda kb, c: block(kb, c, False), init)
        m, l, acc = block(qi, carry, True)
        o_ref[0] = (acc / l).astype(o_ref.dtype)
        lse_ref[0] = m * ATTN_SCALE + jnp.log(l)

    qmap = lambda h, i: (h, i, 0)
    whole = lambda h, i: (h, 0, 0)
    return pl.pallas_call(
        body, grid=(nh, nb),
        in_specs=[pl.BlockSpec((1, t, QK), qmap), pl.BlockSpec((1, s, QK), whole), pl.BlockSpec((1, s, VDIM), whole)],
        out_specs=[pl.BlockSpec((1, t, VDIM), qmap), pl.BlockSpec((1, t, 1), qmap)],
        out_shape=[jax.ShapeDtypeStruct((nh, s, VDIM), BF16), jax.ShapeDtypeStruct((nh, s, 1), F32)],
        compiler_params=_cparams("arbitrary", "arbitrary"), name="attn_fwd",
    )(q, k, v)


def _attn_delta_call(o, do):
    nh, s, d = o.shape

    def fn(ov, dv):
        return (jnp.sum(ov.astype(F32) * dv.astype(F32), axis=-1, keepdims=True),), ()

    return _rowwise_call(fn, [o.reshape(nh * s, d), do.reshape(nh * s, d)], [], [(1, F32)], [], "attn_delta")[0]


def _attn_bwd_call(q, k, v, do, lse_t, delta_t):
    nh, s, _ = q.shape
    t = _attn_tile(s)
    nb = s // t

    def body(q_ref, k_ref, v_ref, do_ref, lse_ref, delta_ref, dq_ref, dk_ref, dv_ref, dk_sc, dv_sc):
        kj = pl.program_id(1)

        @pl.when(kj == 0)
        def _():
            dq_ref[...] = jnp.zeros_like(dq_ref)

        dk_sc[...] = jnp.zeros_like(dk_sc)
        dv_sc[...] = jnp.zeros_like(dv_sc)
        kblk, vblk = k_ref[0], v_ref[0]

        def block(qb, diagonal):
            start = pl.multiple_of(qb * t, t)
            qblk = q_ref[0, pl.ds(start, t), :]
            doblk = do_ref[0, pl.ds(start, t), :]
            sc = _nt(kblk, qblk)
            if diagonal:
                sc = jnp.where(_causal_keep(t, keys_on_rows=True), sc, NEG)
            p = jnp.exp2(sc * ATTN_C - lse_ref[0, :, pl.ds(start, t)] * LOG2E)
            dv_sc[...] += jnp.dot(p.astype(BF16), doblk, preferred_element_type=F32)
            dp = _nt(vblk, doblk)
            ds = (p * (dp - delta_ref[0, :, pl.ds(start, t)])).astype(BF16)
            dk_sc[...] += jnp.dot(ds, qblk, preferred_element_type=F32)
            dq_ref[0, pl.ds(start, t), :] += lax.dot_general(ds, kblk, (((0,), (0,)), ((), ())), preferred_element_type=F32)

        block(kj, True)

        def rest(qb, carry):
            block(qb, False)
            return carry

        lax.fori_loop(kj + 1, nb, rest, 0)
        dk_ref[0] = (dk_sc[...] * ATTN_SCALE).astype(dk_ref.dtype)
        dv_ref[0] = dv_sc[...].astype(dv_ref.dtype)

        @pl.when(kj == nb - 1)
        def _():
            dq_ref[...] = dq_ref[...] * ATTN_SCALE

    kmap = lambda h, j: (h, j, 0)
    whole = lambda h, j: (h, 0, 0)
    return pl.pallas_call(
        body, grid=(nh, nb),
        in_specs=[pl.BlockSpec((1, s, QK), whole), pl.BlockSpec((1, t, QK), kmap), pl.BlockSpec((1, t, VDIM), kmap),
                  pl.BlockSpec((1, s, VDIM), whole), pl.BlockSpec((1, 1, s), whole), pl.BlockSpec((1, 1, s), whole)],
        out_specs=[pl.BlockSpec((1, s, QK), whole), pl.BlockSpec((1, t, QK), kmap), pl.BlockSpec((1, t, VDIM), kmap)],
        out_shape=[jax.ShapeDtypeStruct((nh, s, QK), F32), jax.ShapeDtypeStruct((nh, s, QK), F32), jax.ShapeDtypeStruct((nh, s, VDIM), F32)],
        scratch_shapes=[pltpu.VMEM((t, QK), F32), pltpu.VMEM((t, VDIM), F32)],
        compiler_params=_cparams("arbitrary", "arbitrary"), name="attn_bwd",
    )(q, k, v, do, lse_t, delta_t)


CONV_TC = 512
HALO = 8


def _conv_tiles(s):
    return min(512, s)


def _conv_fwd_call(x, col0, w, b):
    s = x.shape[0]
    ts = _conv_tiles(s)
    hb = ts // HALO
    c0 = col0 // CONV_TC
    assert col0 % CONV_TC == 0

    def body(x_ref, prev_ref, w_ref, b_ref, y_ref, buf):
        si = pl.program_id(1)
        buf[0:HALO, :] = jnp.where(si > 0, prev_ref[...], 0.0)
        buf[HALO:, :] = x_ref[...]
        acc = jnp.broadcast_to(b_ref[...], (ts, CONV_TC))
        for k in range(CONV_K):
            acc = acc + w_ref[k:k + 1, :] * buf[pl.ds(HALO - (CONV_K - 1) + k, ts), :]
        y_ref[...] = acc * jax.nn.sigmoid(acc)

    return pl.pallas_call(
        body, grid=(CONV_DIM // CONV_TC, s // ts),
        in_specs=[pl.BlockSpec((ts, CONV_TC), lambda ci, si: (si, ci + c0)),
                  pl.BlockSpec((HALO, CONV_TC), lambda ci, si: (jnp.maximum(si * hb - 1, 0), ci + c0)),
                  pl.BlockSpec((CONV_K, CONV_TC), lambda ci, si: (0, ci)),
                  pl.BlockSpec((1, CONV_TC), lambda ci, si: (0, ci))],
        out_specs=pl.BlockSpec((ts, CONV_TC), lambda ci, si: (si, ci)),
        out_shape=jax.ShapeDtypeStruct((s, CONV_DIM), F32),
        scratch_shapes=[pltpu.VMEM((ts + HALO, CONV_TC), F32)],
        compiler_params=_cparams("arbitrary", "arbitrary"), name="conv_fwd",
    )(x, x, w, b)


def _conv_bwd_call(x, col0, w, b, dy):
    s = x.shape[0]
    ts = _conv_tiles(s)
    hb = ts // HALO
    ns = s // ts
    last_halo = s // HALO - 1
    c0 = col0 // CONV_TC

    def body(x_ref, prev_ref, next_ref, dy_ref, dyn_ref, w_ref, b_ref, dx_ref, dw_ref, db_ref, xbuf, dbuf):
        si = pl.program_id(1)
        xbuf[0:HALO, :] = jnp.where(si > 0, prev_ref[...], 0.0)
        xbuf[HALO:HALO + ts, :] = x_ref[...]
        xbuf[HALO + ts:, :] = next_ref[...]
        pre = jnp.broadcast_to(b_ref[...], (ts + HALO, CONV_TC))
        for k in range(CONV_K):
            pre = pre + w_ref[k:k + 1, :] * xbuf[pl.ds(HALO - (CONV_K - 1) + k, ts + HALO), :]
        sg = jax.nn.sigmoid(pre)
        dsilu = sg * (1.0 + pre * (1.0 - sg))
        dbuf[0:ts, :] = dy_ref[...] * dsilu[0:ts]
        dbuf[ts:, :] = jnp.where(si < ns - 1, dyn_ref[...] * dsilu[ts:], 0.0)
        dx = jnp.zeros((ts, CONV_TC), F32)
        for k in range(CONV_K):
            dx = dx + w_ref[k:k + 1, :] * dbuf[pl.ds(CONV_K - 1 - k, ts), :]
        dx_ref[...] = dx.astype(dx_ref.dtype)
        dpre = dbuf[0:ts, :]
        first = si == 0
        _acc_store(db_ref, jnp.sum(dpre, axis=0, keepdims=True), first)
        for k in range(CONV_K):
            dw_k = jnp.sum(dpre * xbuf[pl.ds(HALO - (CONV_K - 1) + k, ts), :], axis=0, keepdims=True)
            _acc_store(dw_ref.at[pl.ds(k, 1), :], dw_k, first)

    main = lambda ci, si: (si, ci)
    x_main = lambda ci, si: (si, ci + c0)
    x_prev = lambda ci, si: (jnp.maximum(si * hb - 1, 0), ci + c0)
    x_next = lambda ci, si: (jnp.minimum(si * hb + hb, last_halo), ci + c0)
    return pl.pallas_call(
        body, grid=(CONV_DIM // CONV_TC, ns),
        in_specs=[pl.BlockSpec((ts, CONV_TC), x_main), pl.BlockSpec((HALO, CONV_TC), x_prev), pl.BlockSpec((HALO, CONV_TC), x_next),
                  pl.BlockSpec((ts, CONV_TC), main),
                  pl.BlockSpec((HALO, CONV_TC), lambda ci, si: (jnp.minimum(si * hb + hb, last_halo), ci)),
                  pl.BlockSpec((CONV_K, CONV_TC), lambda ci, si: (0, ci)),
                  pl.BlockSpec((1, CONV_TC), lambda ci, si: (0, ci))],
        out_specs=[pl.BlockSpec((ts, CONV_TC), main),
                   pl.BlockSpec((CONV_K, CONV_TC), lambda ci, si: (0, ci)),
                   pl.BlockSpec((1, CONV_TC), lambda ci, si: (0, ci))],
        out_shape=[jax.ShapeDtypeStruct((s, CONV_DIM), BF16), jax.ShapeDtypeStruct((CONV_K, CONV_DIM), F32),
                   jax.ShapeDtypeStruct((1, CONV_DIM), F32)],
        scratch_shapes=[pltpu.VMEM((ts + 2 * HALO, CONV_TC), F32), pltpu.VMEM((ts + HALO, CONV_TC), F32)],
        compiler_params=_cparams("arbitrary", "arbitrary"), name="conv_bwd",
    )(x, x, x, dy, dy, w, b)


GW = SSD_HPG * SSD_P
B_COL = SSD_DI
C_COL = SSD_DI + SSD_G * SSD_N


def _ones_where(mask):
    return jnp.where(mask, 1.0, 0.0).astype(BF16)


def _split3(v):
    hi = v.astype(BF16)
    r1 = v - hi.astype(F32)
    mid = r1.astype(BF16)
    lo = (r1 - mid.astype(F32)).astype(BF16)
    return hi, mid, lo


def _dot_sel_r(v, sel):
    out = None
    for part in _split3(v):
        t = jnp.dot(part, sel, preferred_element_type=F32)
        out = t if out is None else out + t
    return out


def _dot_sel_l(sel, v):
    out = None
    for part in _split3(v):
        t = jnp.dot(sel, part, preferred_element_type=F32)
        out = t if out is None else out + t
    return out


def _ssd_consts():
    r = lax.broadcasted_iota(jnp.int32, (SSD_L, SSD_L), 0)
    c = lax.broadcasted_iota(jnp.int32, (SSD_L, SSD_L), 1)
    tril = r >= c
    triu = c >= r
    shift = SSD_P.bit_length() - 1
    eh = lax.broadcasted_iota(jnp.int32, (SSD_H, SSD_DI), 0)
    ej = lax.broadcasted_iota(jnp.int32, (SSD_H, SSD_DI), 1)
    expand = _ones_where(lax.shift_right_logical(ej, shift) == eh)
    rj = lax.broadcasted_iota(jnp.int32, (SSD_DI, SSD_H), 0)
    rh = lax.broadcasted_iota(jnp.int32, (SSD_DI, SSD_H), 1)
    reduce_ = _ones_where(lax.shift_right_logical(rj, shift) == rh)
    lane = lax.broadcasted_iota(jnp.int32, (SSD_L, LANE), 1)
    return tril, triu, expand, reduce_, lane < SSD_P


def _ssd_decays(dt, dt_t, a, a_t, tril, triu, expand):
    dta = dt * a
    acum = _dot_sel_l(_ones_where(tril), dta)
    acum_t = _dot_sel_r(dt_t * a_t, _ones_where(triu))
    dta_e = _dot_sel_r(dta, expand)
    acum_e = _dot_sel_r(acum, expand)
    last_e = jnp.sum(dta_e, axis=0, keepdims=True)
    return acum, acum_t, acum_e, last_e


def _head_decay(acum, acum_t, h, tril):
    seg = acum[:, h:h + 1] - acum_t[h:h + 1, :]
    return jnp.exp(jnp.where(tril, seg, NEG))


def _ssd_fwd_call(xbc, dt, a):
    s = xbc.shape[0]
    nc = s // SSD_L
    dt_t = dt.T
    a_t = a.T

    def body(xbc_ref, dt_ref, dtt_ref, a_ref, at_ref, y_ref, st_ref, s_sc):
        ci = pl.program_id(0)

        @pl.when(ci == 0)
        def _():
            s_sc[...] = jnp.zeros_like(s_sc)

        st_ref[0] = s_sc[...]
        tril, triu, expand, _, low_half = _ssd_consts()
        acum, acum_t, acum_e, last_e = _ssd_decays(dt_ref[...], dtt_ref[...], a_ref[...], at_ref[...], tril, triu, expand)
        dt_e = _dot_sel_r(dt_ref[...], expand)
        xdt = xbc_ref[:, :SSD_DI] * dt_e
        xdt_b = xdt.astype(BF16)
        xw_b = (xdt * jnp.exp(last_e - acum_e)).astype(BF16)
        ea_e = jnp.exp(acum_e)
        el_e = jnp.exp(last_e)
        for g in range(SSD_G):
            gs = slice(g * GW, (g + 1) * GW)
            bg = xbc_ref[:, B_COL + g * SSD_N:B_COL + (g + 1) * SSD_N]
            cg_b = xbc_ref[:, C_COL + g * SSD_N:C_COL + (g + 1) * SSD_N].astype(BF16)
            bg_b = bg.astype(BF16)
            cb = _nt(cg_b, bg_b)
            st = s_sc[:, gs]
            y_off = jnp.dot(cg_b, st.astype(BF16), preferred_element_type=F32) * ea_e[:, gs]
            for pr in range(SSD_HPG // 2):
                ls = slice(g * GW + pr * LANE, g * GW + (pr + 1) * LANE)
                xp = xdt_b[:, ls]
                yd = []
                for half in range(2):
                    h = g * SSD_HPG + pr * 2 + half
                    m = (cb * _head_decay(acum, acum_t, h, tril)).astype(BF16)
                    yd.append(jnp.dot(m, xp, preferred_element_type=F32))
                y_ref[:, ls] = jnp.where(low_half, yd[0], yd[1]) + y_off[:, pr * LANE:(pr + 1) * LANE]
            s_sc[:, gs] = st * el_e[:, gs] + jnp.dot(bg.T.astype(BF16), xw_b[:, gs], preferred_element_type=F32)

    row = lambda i: (i, 0)
    return pl.pallas_call(
        body, grid=(nc,),
        in_specs=[pl.BlockSpec((SSD_L, CONV_DIM), row), pl.BlockSpec((SSD_L, SSD_H), row),
                  pl.BlockSpec((SSD_H, SSD_L), lambda i: (0, i)), pl.BlockSpec((1, SSD_H), lambda i: (0, 0)),
                  pl.BlockSpec((SSD_H, 1), lambda i: (0, 0))],
        out_specs=[pl.BlockSpec((SSD_L, SSD_DI), row), pl.BlockSpec((1, SSD_N, SSD_DI), lambda i: (i, 0, 0))],
        out_shape=[jax.ShapeDtypeStruct((s, SSD_DI), F32), jax.ShapeDtypeStruct((nc, SSD_N, SSD_DI), F32)],
        scratch_shapes=[pltpu.VMEM((SSD_N, SSD_DI), F32)],
        compiler_params=_cparams("arbitrary"), name="ssd_fwd",
    )(xbc, dt, dt_t, a, a_t)


def _ssd_bwd_call(xbc, dt, a, states, dy, dx_extra):
    s = xbc.shape[0]
    nc = s // SSD_L
    dt_t = dt.T
    a_t = a.T

    def body(xbc_ref, dt_ref, dtt_ref, a_ref, at_ref, st_ref, dy_ref, dxe_ref,
             dxbc_ref, ddt_ref, da_ref, ds_sc, yf_sc, dxd_sc, dxw_sc):
        i = pl.program_id(0)

        @pl.when(i == 0)
        def _():
            ds_sc[...] = jnp.zeros_like(ds_sc)

        tril, triu, expand, reduce_, low_half = _ssd_consts()
        dt = dt_ref[...]
        a_row = a_ref[...]
        acum, acum_t, acum_e, last_e = _ssd_decays(dt, dtt_ref[...], a_row, at_ref[...], tril, triu, expand)
        dt_e = _dot_sel_r(dt, expand)
        x = xbc_ref[:, :SSD_DI]
        xdt = x * dt_e
        xdt_b = xdt.astype(BF16)
        w_e = jnp.exp(last_e - acum_e)
        xw_b = (xdt * w_e).astype(BF16)
        ea_e = jnp.exp(acum_e)
        el_e = jnp.exp(last_e)
        dy = dy_ref[...]
        dy_b = dy.astype(BF16)
        s_prev = st_ref[0]
        ds_new = ds_sc[...]
        ds_new_b = ds_new.astype(BF16)
        triu_b = _ones_where(triu)
        strict_tril = jnp.logical_not(triu)
        head_ids = lax.broadcasted_iota(jnp.int32, (1, SSD_H), 1)
        d_dta_diag = jnp.zeros((SSD_L, SSD_H), F32)
        for g in range(SSD_G):
            gs = slice(g * GW, (g + 1) * GW)
            bs_ = slice(B_COL + g * SSD_N, B_COL + (g + 1) * SSD_N)
            cs_ = slice(C_COL + g * SSD_N, C_COL + (g + 1) * SSD_N)
            bg = xbc_ref[:, bs_]
            cg = xbc_ref[:, cs_]
            bg_b, cg_b = bg.astype(BF16), cg.astype(BF16)
            st_b = s_prev[:, gs].astype(BF16)
            y_off = jnp.dot(cg_b, st_b, preferred_element_type=F32) * ea_e[:, gs]
            yf_sc[:, gs] = y_off
            dz_b = (dy[:, gs] * ea_e[:, gs]).astype(BF16)
            d_c = _nt(dz_b, st_b)
            ds_prev = ds_new[:, gs] * el_e[:, gs] + jnp.dot(cg.T.astype(BF16), dz_b, preferred_element_type=F32)
            dxw_sc[:, gs] = jnp.dot(bg_b, ds_new_b[:, gs], preferred_element_type=F32)
            d_b = _nt(xw_b[:, gs], ds_new_b[:, gs])
            cb = _nt(cg_b, bg_b)
            d_g = jnp.zeros((SSD_L, SSD_L), F32)
            for pr in range(SSD_HPG // 2):
                ls = slice(g * GW + pr * LANE, g * GW + (pr + 1) * LANE)
                xp = xdt_b[:, ls]
                dyp = dy[:, ls]
                dyp_b = dy_b[:, ls]
                dxd = []
                for half in range(2):
                    h = g * SSD_HPG + pr * 2 + half
                    dec = _head_decay(acum, acum_t, h, tril)
                    m = cb * dec
                    dxd.append(jnp.dot(m.T.astype(BF16), dyp_b, preferred_element_type=F32))
                    mine = low_half if half == 0 else jnp.logical_not(low_half)
                    d_m = _nt(jnp.where(mine, dyp, 0.0).astype(BF16), xp)
                    d_g = d_g + d_m * dec
                    below = jnp.dot(triu_b, (d_m * m).astype(BF16), preferred_element_type=F32)
                    col = jnp.sum(jnp.where(strict_tril, below, 0.0), axis=1, keepdims=True)
                    d_dta_diag = d_dta_diag + col * jnp.where(head_ids == h, 1.0, 0.0)
                dxd_sc[:, ls] = jnp.where(low_half, dxd[0], dxd[1])
            d_g_b = d_g.astype(BF16)
            dxbc_ref[:, cs_] = d_c + jnp.dot(d_g_b, bg_b, preferred_element_type=F32)
            dxbc_ref[:, bs_] = d_b + jnp.dot(d_g.T.astype(BF16), cg_b, preferred_element_type=F32)
            ds_sc[:, gs] = ds_prev
        dxw = dxw_sc[...]
        dxd = dxd_sc[...]
        dw_e = xdt * dxw * w_e
        d_acum_e = dy * yf_sc[...] - dw_e
        d_last_e = jnp.sum(ds_new * s_prev, axis=0, keepdims=True) * el_e + jnp.sum(dw_e, axis=0, keepdims=True)
        suffix = _dot_sel_l(triu_b, d_acum_e)
        d_dta = _dot_sel_r(suffix + d_last_e, reduce_) + d_dta_diag
        dxdt = dxd + dxw * w_e
        dxbc_ref[:, :SSD_DI] = dxdt * dt_e + dxe_ref[...]
        ddt_ref[...] = d_dta * a_row + _dot_sel_r(dxdt * x, reduce_)
        _acc_store(da_ref, jnp.sum(d_dta * dt, axis=0, keepdims=True), i == 0)

    rev = lambda i: (nc - 1 - i, 0)
    return pl.pallas_call(
        body, grid=(nc,),
        in_specs=[pl.BlockSpec((SSD_L, CONV_DIM), rev), pl.BlockSpec((SSD_L, SSD_H), rev),
                  pl.BlockSpec((SSD_H, SSD_L), lambda i: (0, nc - 1 - i)), pl.BlockSpec((1, SSD_H), lambda i: (0, 0)),
                  pl.BlockSpec((SSD_H, 1), lambda i: (0, 0)),
                  pl.BlockSpec((1, SSD_N, SSD_DI), lambda i: (nc - 1 - i, 0, 0)),
                  pl.BlockSpec((SSD_L, SSD_DI), rev), pl.BlockSpec((SSD_L, SSD_DI), rev)],
        out_specs=[pl.BlockSpec((SSD_L, CONV_DIM), rev), pl.BlockSpec((SSD_L, SSD_H), rev),
                   pl.BlockSpec((1, SSD_H), lambda i: (0, 0))],
        out_shape=[jax.ShapeDtypeStruct((s, CONV_DIM), F32), jax.ShapeDtypeStruct((s, SSD_H), F32),
                   jax.ShapeDtypeStruct((1, SSD_H), F32)],
        scratch_shapes=[pltpu.VMEM((SSD_N, SSD_DI), F32), pltpu.VMEM((SSD_L, SSD_DI), F32),
                        pltpu.VMEM((SSD_L, SSD_DI), F32), pltpu.VMEM((SSD_L, SSD_DI), F32)],
        compiler_params=_cparams("arbitrary"), name="ssd_bwd",
    )(xbc, dt, dt_t, a, a_t, states, dy, dx_extra)


HBM_SPEC = pl.BlockSpec(memory_space=pltpu.HBM)
N_PEERS = N_DEV - 1


def _flip(v, f):
    return 1 - v if f else v


def _all_gather(shard):
    rows, c = shard.shape

    def body(x_ref, out_ref, send_sems, recv_sems, local_sem):
        x, y, cc = lax.axis_index("x"), lax.axis_index("y"), lax.axis_index("c")
        me, sibling = (x, y, cc), (x, y, 1 - cc)
        chips = [(1 - x, y), (x, 1 - y), (1 - x, 1 - y)]

        def slot(px, py, pc):
            return out_ref.at[4 * px + 2 * py + pc]

        def copy(k, block, to, src=None):
            return pltpu.make_async_remote_copy(
                src_ref=slot(*block) if src is None else src, dst_ref=slot(*block),
                send_sem=send_sems.at[k], recv_sem=recv_sems.at[k],
                device_id=to, device_id_type=pl.DeviceIdType.MESH)

        mine = pltpu.make_async_copy(x_ref, slot(*me), local_sem)
        mine.start()
        first = [copy(0, me, sibling, src=x_ref)]
        first += [copy(1 + j, me, (*chip, cc), src=x_ref) for j, chip in enumerate(chips)]
        for cp in first:
            cp.start()
        passed = [copy(4 + j, (*chip, cc), sibling) for j, chip in enumerate(chips)]
        for j, chip in enumerate(chips):
            copy(1 + j, (*chip, cc), me).wait_recv()
            passed[j].start()
        copy(0, sibling, me).wait_recv()
        for j, chip in enumerate(chips):
            copy(4 + j, (*chip, 1 - cc), me).wait_recv()
        for cp in first + passed:
            cp.wait_send()
        mine.wait()

    return pl.pallas_call(
        body, out_shape=jax.ShapeDtypeStruct((N_DEV, rows, c), shard.dtype),
        in_specs=[HBM_SPEC], out_specs=HBM_SPEC,
        scratch_shapes=[pltpu.SemaphoreType.DMA((N_PEERS,)), pltpu.SemaphoreType.DMA((N_PEERS,)), pltpu.SemaphoreType.DMA(())],
        name="all_gather",
    )(shard)


def _exchange_blocks(blocks):
    _, rows, c = blocks.shape

    def body(g_ref, out_ref, send_sems, recv_sems, local_sem):
        x, y, cc = lax.axis_index("x"), lax.axis_index("y"), lax.axis_index("c")
        me = 4 * x + 2 * y + cc
        mine = pltpu.make_async_copy(g_ref.at[me], out_ref.at[me], local_sem)
        mine.start()
        copies = []
        for k in range(1, N_DEV):
            px, py, pc = _flip(x, k & 4), _flip(y, k & 2), _flip(cc, k & 1)
            peer = 4 * px + 2 * py + pc
            copies.append((
                pltpu.make_async_remote_copy(
                    src_ref=g_ref.at[peer], dst_ref=out_ref.at[me], send_sem=send_sems.at[k - 1], recv_sem=recv_sems.at[k - 1],
                    device_id=(px, py, pc), device_id_type=pl.DeviceIdType.MESH),
                pltpu.make_async_remote_copy(
                    src_ref=g_ref.at[peer], dst_ref=out_ref.at[peer], send_sem=send_sems.at[k - 1], recv_sem=recv_sems.at[k - 1],
                    device_id=(px, py, pc), device_id_type=pl.DeviceIdType.MESH)))
        for send, _ in copies:
            send.start()
        for _, landed in copies:
            landed.wait_recv()
        for send, _ in copies:
            send.wait_send()
        mine.wait()

    return pl.pallas_call(
        body, out_shape=jax.ShapeDtypeStruct(blocks.shape, blocks.dtype),
        in_specs=[HBM_SPEC], out_specs=HBM_SPEC,
        scratch_shapes=[pltpu.SemaphoreType.DMA((N_PEERS,)), pltpu.SemaphoreType.DMA((N_PEERS,)), pltpu.SemaphoreType.DMA(())],
        name="exchange_blocks",
    )(blocks)


BIG = [
    ("ffn1_w13", (D_MODEL, 2 * D_FF), 1), ("ffn1_w2", (D_FF, D_MODEL), 0), ("w_in", (D_MODEL, D_IN), 1),
    ("w_ssd_out", (SSD_DI, D_MODEL), 0), ("w_uq", (Q_LORA, MLA_H * QK), 1), ("w_ukv", (KV_LORA, MLA_H * (NOPE + VDIM)), 1),
    ("w_mla_out", (MLA_H * VDIM, D_MODEL), 0), ("w_o", (D_MODEL, D_MODEL), 0),
    ("ffn2_w13", (D_MODEL, 2 * D_FF), 1), ("ffn2_w2", (D_FF, D_MODEL), 0),
]
SMALL = [
    ("ln_ffn1", D_MODEL), ("ln_mix", D_MODEL), ("conv_b", CONV_DIM), ("dt_bias", SSD_H), ("a_log", SSD_H), ("d_skip", SSD_H),
    ("ssd_norm", SSD_DI), ("q_lora_norm", Q_LORA), ("kv_lora_norm", KV_LORA), ("q_norm", QK), ("k_norm", QK), ("ln_ffn2", D_MODEL),
]


def _shard_shape(full, axis):
    k, n = full
    return (k // N_DEV, n) if axis == 0 else (k, n // N_DEV)


def _shard_rows(full):
    return full[0] * full[1] // N_DEV // PACK_COLS


LAYER_ROWS = sum(_shard_rows(f) for _, f, _ in BIG)
LAYER_ROWS_PAD = -(-LAYER_ROWS // 256) * 256


def _pack_shards(shards):
    parts = [shards[name].reshape(-1, PACK_COLS) for name, _, _ in BIG]
    pad = LAYER_ROWS_PAD - LAYER_ROWS
    if pad:
        parts.append(jnp.zeros((pad, PACK_COLS), parts[0].dtype))
    return jnp.concatenate(parts, axis=0)


def _unpack_shards(packed):
    out, r = {}, 0
    for name, full, axis in BIG:
        n = _shard_rows(full)
        out[name] = packed[r:r + n].reshape(_shard_shape(full, axis))
        r += n
    return out


def _unpack_gathered(gathered):
    out, r = {}, 0
    for name, full, axis in BIG:
        n = _shard_rows(full)
        blk = gathered[:, r:r + n].reshape((N_DEV,) + _shard_shape(full, axis))
        out[name] = blk.reshape(full) if axis == 0 else jnp.transpose(blk, (1, 0, 2)).reshape(full)
        r += n
    return out


def _pack_full_grads(grads):
    parts = []
    for name, full, axis in BIG:
        g = grads[name]
        k, n = _shard_shape(full, axis)
        blk = g.reshape(N_DEV, k, n) if axis == 0 else jnp.transpose(g.reshape(k, N_DEV, n), (1, 0, 2))
        parts.append(blk.reshape(N_DEV, -1, PACK_COLS))
    pad = LAYER_ROWS_PAD - LAYER_ROWS
    if pad:
        parts.append(jnp.zeros((N_DEV, pad, PACK_COLS), parts[0].dtype))
    return jnp.concatenate(parts, axis=1)


SMALL_COLS = sum(n for _, n in SMALL) + CONV_K * CONV_DIM
SMALL_ROWS = -(-(DEPTH * SMALL_COLS) // (8 * PACK_COLS)) * 8


def _pack_small(vals, conv_w):
    flat = jnp.concatenate([vals[name] for name, _ in SMALL] + [conv_w.reshape(DEPTH, -1)], axis=1).reshape(-1)
    flat = jnp.concatenate([flat, jnp.zeros((SMALL_ROWS * PACK_COLS - flat.shape[0],), F32)])
    return flat.reshape(SMALL_ROWS, PACK_COLS)


def _unpack_small(packed):
    flat = packed.reshape(-1)[:DEPTH * SMALL_COLS].reshape(DEPTH, SMALL_COLS)
    out, c = {}, 0
    for name, n in SMALL:
        out[name] = flat[:, c:c + n]
        c += n
    return out, flat[:, c:].reshape(DEPTH, CONV_K, CONV_DIM)


_IN_OFFS = [sum(IN_SPLIT[:i]) for i in range(len(IN_SPLIT) + 1)]


def _arrange_w_in(w):
    z, xbc, dt, cq, ckv, kr, gates = [w[:, _IN_OFFS[i]:_IN_OFFS[i + 1]] for i in range(len(IN_SPLIT))]
    pad = jnp.zeros((w.shape[0], LANE - ROPE - SSD_H), w.dtype)
    return jnp.concatenate([z, gates, xbc, cq, ckv, kr, dt, pad], axis=1)


def _restore_w_in(g):
    z, gates, xbc = g[:, PROJ_Z:PROJ_GATES], g[:, PROJ_GATES:PROJ_XBC], g[:, PROJ_XBC:PROJ_CQ]
    cq, ckv = g[:, PROJ_CQ:PROJ_CKV], g[:, PROJ_CKV:PROJ_LAST]
    kr, dt = g[:, PROJ_LAST:PROJ_LAST + ROPE], g[:, PROJ_LAST + ROPE:PROJ_LAST + ROPE + SSD_H]
    return jnp.concatenate([z, xbc, dt, cq, ckv, kr, gates], axis=1)


def _rope(t, cos, sin):
    half = ROPE // 2
    t1, t2 = t[..., :half], t[..., half:]
    c, s = cos[:, None, :], sin[:, None, :]
    return jnp.concatenate([t1 * c - t2 * s, t2 * c + t1 * s], axis=-1)


def _mla_heads(q, kv, kr, q_gain, k_gain, cos, sin):
    s = q.shape[0]
    kv = kv.reshape(s, MLA_H, NOPE + VDIM)
    k = jnp.concatenate([kv[..., :NOPE], jnp.broadcast_to(kr[:, None, :], (s, MLA_H, ROPE))], axis=-1)
    v = kv[..., NOPE:]
    q = _head_rmsnorm(q.reshape(s * MLA_H, QK), q_gain, "q_norm").reshape(s, MLA_H, QK)
    k = _head_rmsnorm(k.reshape(s * MLA_H, QK), k_gain, "k_norm").reshape(s, MLA_H, QK)
    q = jnp.concatenate([q[..., :NOPE], _rope(q[..., NOPE:], cos, sin)], axis=-1)
    k = jnp.concatenate([k[..., :NOPE], _rope(k[..., NOPE:], cos, sin)], axis=-1)
    heads_first = lambda t: jnp.transpose(t, (1, 0, 2))
    return heads_first(q), heads_first(k), heads_first(v)


def _row(v):
    return v.reshape(1, -1)


def _ffn_fwd(h, ln, w13, w2, name):
    n = _row_fwd(_f_rmsnorm, [h], [_row(ln)], [BF16], name + "_fwd")[0]
    gu = _mm(n, w13)
    act = _row_fwd(_f_swiglu, [gu], [], [BF16], "swiglu_fwd")[0]
    return _mm(act, w2, alpha=0.5, res=h), (h, n, gu, act)


def _ffn_bwd(dh_out, saved, ln, w13, w2, name):
    h, n, gu, act = saved
    d_act = _mm(dh_out, w2, tb=True, out_dtype=BF16, alpha=0.5)
    d_w2 = _mm(act, dh_out, ta=True, out_dtype=BF16, alpha=0.5)
    d_gu = _row_bwd(_f_swiglu, [gu], [], [d_act], [BF16], "swiglu_bwd", bwd=_b_swiglu)[0][0]
    d_n = _mm(d_gu, w13, tb=True, out_dtype=BF16)
    d_w13 = _mm(n, d_gu, ta=True, out_dtype=BF16)
    (dh,), (d_ln,) = _row_bwd(_f_rmsnorm, [h], [_row(ln)], [d_n], [F32], name + "_bwd", add={0: dh_out})
    return dh, d_w13, d_w2, d_ln[0]


def _mixer_fwd(h, big, small, conv_w, cos, sin):
    s = h.shape[0]
    u = _row_fwd(_f_rmsnorm, [h], [_row(small["ln_mix"])], [BF16], "ln_mix_fwd")[0]
    proj = _mm(u, big["w_in"])
    xbc = _conv_fwd_call(proj, PROJ_XBC, conv_w, _row(small["conv_b"]))
    last = proj[:, PROJ_LAST:]
    kr, dt_in = last[:, :ROPE], last[:, ROPE:ROPE + SSD_H] + small["dt_bias"][None, :]
    dt = jax.nn.softplus(dt_in)
    a = -jnp.exp(small["a_log"])[None, :]
    y_scan, states = _ssd_fwd_call(xbc, dt, a)
    dsk = _row(jnp.repeat(small["d_skip"], SSD_P))
    gn_in = [y_scan, _win(xbc, 0, SSD_DI), _win(proj, PROJ_Z, SSD_DI)]
    yn = _row_fwd(_f_gated_norm, gn_in, [dsk, _row(small["ssd_norm"])], [BF16], "gated_norm_fwd")[0]
    y_ssd = _mm(yn, big["w_ssd_out"])
    qn = _row_fwd(_f_rmsnorm, [_win(proj, PROJ_CQ, Q_LORA)], [_row(small["q_lora_norm"])], [BF16], "q_lora_norm_fwd")[0]
    kvn = _row_fwd(_f_rmsnorm, [_win(proj, PROJ_CKV, KV_LORA)], [_row(small["kv_lora_norm"])], [BF16], "kv_lora_norm_fwd")[0]
    q = _mm(qn, big["w_uq"])
    kv = _mm(kvn, big["w_ukv"])
    heads = lambda q_, kv_, kr_, qg, kg: _mla_heads(q_, kv_, kr_, qg, kg, cos, sin)
    (qh, kh, vh), heads_vjp = jax.vjp(heads, q, kv, kr, small["q_norm"], small["k_norm"])
    qh, kh, vh = qh.astype(BF16), kh.astype(BF16), vh.astype(BF16)
    o, lse = _attn_fwd_call(qh, kh, vh)
    o_rows = jnp.transpose(o, (1, 0, 2)).reshape(s, MLA_H * VDIM)
    y_mla = _mm(o_rows, big["w_mla_out"])
    mg = _row_fwd(_f_merge, [_win(proj, PROJ_GATES, 2 * D_MODEL), y_ssd, y_mla], [], [BF16], "merge_fwd")[0]
    out = _mm(mg, big["w_o"], res=h)
    return out, (h, u, proj, xbc, dt_in, dt, a, y_scan, states, dsk, yn, y_ssd, qn, kvn, heads_vjp, qh, kh, vh, o, lse, o_rows, y_mla, mg)


def _mixer_bwd(dh_out, saved, big, small, conv_w):
    (h, u, proj, xbc, dt_in, dt, a, y_scan, states, dsk, yn, y_ssd, qn, kvn, heads_vjp, qh, kh, vh, o, lse, o_rows, y_mla, mg) = saved
    s = h.shape[0]
    d_big, d_small = {}, {}
    d_mg = _mm(dh_out, big["w_o"], tb=True, out_dtype=BF16)
    d_big["w_o"] = _mm(mg, dh_out, ta=True, out_dtype=BF16)
    merge_in = [_win(proj, PROJ_GATES, 2 * D_MODEL), y_ssd, y_mla]
    (d_gates, d_y_ssd, d_y_mla), _ = _row_bwd(_f_merge, merge_in, [], [d_mg], [BF16, BF16, BF16], "merge_bwd", bwd=_b_merge)
    d_o_rows = _mm(d_y_mla, big["w_mla_out"], tb=True, out_dtype=BF16)
    d_big["w_mla_out"] = _mm(o_rows, d_y_mla, ta=True, out_dtype=BF16)
    d_o = jnp.transpose(d_o_rows.reshape(s, MLA_H, VDIM), (1, 0, 2))
    delta = _attn_delta_call(o, d_o)
    d_heads = _attn_bwd_call(qh, kh, vh, d_o, lse.reshape(MLA_H, 1, s), delta.reshape(MLA_H, 1, s))
    d_q, d_kv, d_kr, d_small["q_norm"], d_small["k_norm"] = heads_vjp(tuple(d_heads))
    d_qn = _mm(d_q, big["w_uq"], tb=True, out_dtype=BF16)
    d_big["w_uq"] = _mm(qn, d_q, ta=True, out_dtype=BF16)
    d_kvn = _mm(d_kv, big["w_ukv"], tb=True, out_dtype=BF16)
    d_big["w_ukv"] = _mm(kvn, d_kv, ta=True, out_dtype=BF16)
    (d_cq,), (d_g,) = _row_bwd(_f_rmsnorm, [_win(proj, PROJ_CQ, Q_LORA)], [_row(small["q_lora_norm"])], [d_qn], [BF16], "q_lora_norm_bwd")
    d_small["q_lora_norm"] = d_g[0]
    (d_ckv,), (d_g,) = _row_bwd(_f_rmsnorm, [_win(proj, PROJ_CKV, KV_LORA)], [_row(small["kv_lora_norm"])], [d_kvn], [BF16], "kv_lora_norm_bwd")
    d_small["kv_lora_norm"] = d_g[0]
    d_yn = _mm(d_y_ssd, big["w_ssd_out"], tb=True, out_dtype=BF16)
    d_big["w_ssd_out"] = _mm(yn, d_y_ssd, ta=True, out_dtype=BF16)
    gn_in = [y_scan, _win(xbc, 0, SSD_DI), _win(proj, PROJ_Z, SSD_DI)]
    (d_y_scan, d_xs, d_z), (d_dsk, d_g) = _row_bwd(
        _f_gated_norm, gn_in, [dsk, _row(small["ssd_norm"])], [d_yn], [F32, F32, BF16], "gated_norm_bwd")
    d_small["ssd_norm"] = d_g[0]
    d_small["d_skip"] = jnp.sum(d_dsk.reshape(SSD_H, SSD_P), axis=1)
    d_xbc_act, d_dt, d_a = _ssd_bwd_call(xbc, dt, a, states, d_y_scan, d_xs)
    d_xbc, d_conv_w, d_conv_b = _conv_bwd_call(proj, PROJ_XBC, conv_w, _row(small["conv_b"]), d_xbc_act)
    d_small["conv_b"] = d_conv_b[0]
    d_dt_in = d_dt * jax.nn.sigmoid(dt_in)
    d_small["dt_bias"] = jnp.sum(d_dt_in, axis=0)
    d_small["a_log"] = d_a[0] * a[0]
    d_last = jnp.concatenate([d_kr, d_dt_in, jnp.zeros((s, LANE - ROPE - SSD_H), F32)], axis=1).astype(BF16)
    d_proj = jnp.concatenate([d_z, d_gates, d_xbc, d_cq, d_ckv, d_last], axis=1)
    d_u = _mm(d_proj, big["w_in"], tb=True, out_dtype=BF16)
    d_big["w_in"] = _mm(u, d_proj, ta=True, out_dtype=BF16)
    (dh,), (d_ln,) = _row_bwd(_f_rmsnorm, [h], [_row(small["ln_mix"])], [d_u], [F32], "ln_mix_bwd", add={0: dh_out})
    d_small["ln_mix"] = d_ln[0]
    return dh, d_big, d_small, d_conv_w


def _local_step(x, positions, target, big, small, conv_w):
    inv = 1.0 / (ROPE_THETA ** (jnp.arange(0, ROPE, 2, dtype=F32) / ROPE))
    ang = positions.astype(F32)[:, None] * inv
    cos, sin = jnp.cos(ang), jnp.sin(ang)
    big = [dict(b, w_in=_arrange_w_in(b["w_in"])) for b in big]
    layer_small = [{k: v[l] for k, v in small.items()} for l in range(DEPTH)]

    h, saved = x, []
    for l in range(DEPTH):
        b, sm = big[l], layer_small[l]
        h, s1 = _ffn_fwd(h, sm["ln_ffn1"], b["ffn1_w13"], b["ffn1_w2"], "ln_ffn1")
        h, s2 = _mixer_fwd(h, b, sm, conv_w[l], cos, sin)
        h, s3 = _ffn_fwd(h, sm["ln_ffn2"], b["ffn2_w13"], b["ffn2_w2"], "ln_ffn2")
        saved.append((s1, s2, s3))
    loss, dh = _loss_and_grad(h, target)

    d_big, d_small, d_conv_w = [None] * DEPTH, [None] * DEPTH, [None] * DEPTH
    for l in reversed(range(DEPTH)):
        b, sm = big[l], layer_small[l]
        s1, s2, s3 = saved[l]
        dh, d_w13_2, d_w2_2, d_ln2 = _ffn_bwd(dh, s3, sm["ln_ffn2"], b["ffn2_w13"], b["ffn2_w2"], "ln_ffn2")
        dh, db, ds, d_conv_w[l] = _mixer_bwd(dh, s2, b, sm, conv_w[l])
        dh, d_w13_1, d_w2_1, d_ln1 = _ffn_bwd(dh, s1, sm["ln_ffn1"], b["ffn1_w13"], b["ffn1_w2"], "ln_ffn1")
        db.update(ffn1_w13=d_w13_1, ffn1_w2=d_w2_1, ffn2_w13=d_w13_2, ffn2_w2=d_w2_2, w_in=_restore_w_in(db["w_in"]))
        ds.update(ln_ffn1=d_ln1, ln_ffn2=d_ln2)
        d_big[l], d_small[l] = db, ds
    d_small = {name: jnp.stack([d_small[l][name] for l in range(DEPTH)]) for name, _ in SMALL}
    return loss, dh, d_big, d_small, jnp.stack(d_conv_w)


def _step(args):
    dev = 4 * lax.axis_index("x") + 2 * lax.axis_index("y") + lax.axis_index("c")
    x, positions, target = args["x"][0], args["positions"][0], args["loss_target"][0]

    big = []
    for l in range(DEPTH):
        packed = _pack_shards({name: args[name][l].astype(BF16) for name, _, _ in BIG})
        big.append(_unpack_gathered(_all_gather(packed)))
    cw = args["conv_w"]
    cw_cols = cw.shape[-1]
    cw_rows = -(-cw.size // (8 * PACK_COLS)) * 8
    cw_flat = jnp.concatenate([cw.reshape(-1), jnp.zeros((cw_rows * PACK_COLS - cw.size,), F32)]).reshape(cw_rows, PACK_COLS)
    cw_all = _all_gather(cw_flat).reshape(N_DEV, -1)[:, :cw.size].reshape(N_DEV, DEPTH, CONV_K, cw_cols)
    conv_w = jnp.transpose(cw_all, (1, 2, 0, 3)).reshape(DEPTH, CONV_K, CONV_DIM)
    small = {name: args[name] for name, _ in SMALL}

    loss, dx, d_big, d_small, d_conv_w = _local_step(x, positions, target, big, small, conv_w)
    loss = lax.psum(loss, MESH_AXES)

    out = {"loss": loss, "grad_x": dx[None]}

    grads = {name: [] for name, _, _ in BIG}
    for l in range(DEPTH):
        summed = _sum_blocks(_exchange_blocks(_pack_full_grads(d_big[l])))
        for name, g in _unpack_shards(summed).items():
            grads[name].append(g)
    flat = lambda t: t.reshape(-1, t.shape[-1])
    for name, _, _ in BIG:
        g = jnp.stack(grads[name])
        w = args[name]
        delta, m2, v2 = _adam(flat(w), flat(g), flat(args["m_" + name]), flat(args["v_" + name]))
        out["grad_" + name] = g
        out["delta_" + name] = delta.reshape(w.shape)
        out["new_m_" + name] = m2.reshape(w.shape)
        out["new_v_" + name] = v2.reshape(w.shape)

    total = _sum_blocks(_all_gather(_pack_small(d_small, d_conv_w)))
    g_conv_w = _unpack_small(total)[1]
    zeros_cw = jnp.zeros((DEPTH, CONV_K, CONV_DIM), F32)
    delta, m2, v2 = _adam(_pack_small(small, zeros_cw), total,
                          _pack_small({name: args["m_" + name] for name, _ in SMALL}, zeros_cw),
                          _pack_small({name: args["v_" + name] for name, _ in SMALL}, zeros_cw))
    for kind, packed in (("grad_", total), ("delta_", delta), ("new_m_", m2), ("new_v_", v2)):
        for name, val in _unpack_small(packed)[0].items():
            out[kind + name] = val
    g_cw = lax.dynamic_slice_in_dim(g_conv_w, dev * cw_cols, cw_cols, axis=2)
    delta, m2, v2 = _adam(flat(cw), flat(g_cw), flat(args["m_conv_w"]), flat(args["v_conv_w"]))
    out["grad_conv_w"] = g_cw
    out["delta_conv_w"] = delta.reshape(cw.shape)
    out["new_m_conv_w"] = m2.reshape(cw.shape)
    out["new_v_conv_w"] = v2.reshape(cw.shape)
    return out


WEIGHTS = ["ln_ffn1", "ffn1_w13", "ffn1_w2", "ln_mix", "w_in", "conv_w", "conv_b", "dt_bias", "a_log", "d_skip", "ssd_norm",
           "w_ssd_out", "q_lora_norm", "w_uq", "kv_lora_norm", "w_ukv", "q_norm", "k_norm", "w_mla_out", "w_o", "ln_ffn2",
           "ffn2_w13", "ffn2_w2"]
ARG_NAMES = (["x", "positions"] + WEIGHTS + ["loss_target"] + ["m_" + n for n in WEIGHTS] + ["v_" + n for n in WEIGHTS])


def kernel(x, positions, ln_ffn1, ffn1_w13, ffn1_w2, ln_mix, w_in, conv_w, conv_b, dt_bias, a_log, d_skip, ssd_norm, w_ssd_out, q_lora_norm, w_uq, kv_lora_norm, w_ukv, q_norm, k_norm, w_mla_out, w_o, ln_ffn2, ffn2_w13, ffn2_w2, loss_target, m_ln_ffn1, m_ffn1_w13, m_ffn1_w2, m_ln_mix, m_w_in, m_conv_w, m_conv_b, m_dt_bias, m_a_log, m_d_skip, m_ssd_norm, m_w_ssd_out, m_q_lora_norm, m_w_uq, m_kv_lora_norm, m_w_ukv, m_q_norm, m_k_norm, m_w_mla_out, m_w_o, m_ln_ffn2, m_ffn2_w13, m_ffn2_w2, v_ln_ffn1, v_ffn1_w13, v_ffn1_w2, v_ln_mix, v_w_in, v_conv_w, v_conv_b, v_dt_bias, v_a_log, v_d_skip, v_ssd_norm, v_w_ssd_out, v_q_lora_norm, v_w_uq, v_kv_lora_norm, v_w_ukv, v_q_norm, v_k_norm, v_w_mla_out, v_w_o, v_ln_ffn2, v_ffn2_w13, v_ffn2_w2):
    vals = (x, positions, ln_ffn1, ffn1_w13, ffn1_w2, ln_mix, w_in, conv_w, conv_b, dt_bias, a_log, d_skip, ssd_norm, w_ssd_out, q_lora_norm, w_uq, kv_lora_norm, w_ukv, q_norm, k_norm, w_mla_out, w_o, ln_ffn2, ffn2_w13, ffn2_w2, loss_target, m_ln_ffn1, m_ffn1_w13, m_ffn1_w2, m_ln_mix, m_w_in, m_conv_w, m_conv_b, m_dt_bias, m_a_log, m_d_skip, m_ssd_norm, m_w_ssd_out, m_q_lora_norm, m_w_uq, m_kv_lora_norm, m_w_ukv, m_q_norm, m_k_norm, m_w_mla_out, m_w_o, m_ln_ffn2, m_ffn2_w13, m_ffn2_w2, v_ln_ffn1, v_ffn1_w13, v_ffn1_w2, v_ln_mix, v_w_in, v_conv_w, v_conv_b, v_dt_bias, v_a_log, v_d_skip, v_ssd_norm, v_w_ssd_out, v_q_lora_norm, v_w_uq, v_kv_lora_norm, v_w_ukv, v_q_norm, v_k_norm, v_w_mla_out, v_w_o, v_ln_ffn2, v_ffn2_w13, v_ffn2_w2)
    out = _step(dict(zip(ARG_NAMES, vals)))
    order = ["loss", "grad_x"] + [k + n for k in ("grad_", "delta_", "new_m_", "new_v_") for n in WEIGHTS]
    return tuple(out[n] for n in order)
```

```python
import jax
import jax.numpy as jnp
from jax import lax
from jax.experimental import pallas as pl
from jax.experimental.pallas import tpu as pltpu

F32 = jnp.float32
BF16 = jnp.bfloat16

D_MODEL = 1024
D_FF = 2816
DEPTH = 2
SSD_DI = 2048
SSD_P = 64
SSD_H = 32
SSD_G = 4
SSD_HPG = 8
SSD_N = 128
SSD_L = 128
CONV_K = 4
CONV_DIM = 3072
MLA_H = 8
Q_LORA = 512
KV_LORA = 256
NOPE = 128
ROPE = 64
VDIM = 128
QK = 192
ROPE_THETA = 10000.0
EPS = 1e-6
IN_SPLIT = (SSD_DI, CONV_DIM, SSD_H, Q_LORA, KV_LORA, ROPE, 2 * D_MODEL)
D_IN = sum(IN_SPLIT)
N_DEV = 8
LANE = 128
PACK_COLS = 1024

PROJ_Z = 0
PROJ_GATES = PROJ_Z + SSD_DI
PROJ_XBC = PROJ_GATES + 2 * D_MODEL
PROJ_CQ = PROJ_XBC + CONV_DIM
PROJ_CKV = PROJ_CQ + Q_LORA
PROJ_LAST = PROJ_CKV + KV_LORA
D_IN_PAD = PROJ_LAST + LANE

ADAM_LR = 0.001
ADAM_B1 = 0.9
ADAM_B2 = 0.999
ADAM_EPS = 1e-08
ADAM_WD = 0.01
ADAM_STEP = 10

VMEM_LIMIT = 48 * 1024 * 1024
ROW_IO_BUDGET = 8 * 1024 * 1024
NEG = -1e30

MESH_AXES = ("x", "y", "c")


def _cparams(*sem):
    return pltpu.CompilerParams(dimension_semantics=sem, vmem_limit_bytes=VMEM_LIMIT)


def _pick_tile(n, target, align):
    if n <= target:
        return n
    best = None
    for t in range(align, target + 1, align):
        if n % t == 0:
            best = t
    assert best is not None, (n, target, align)
    return best


def _acc_store(ref, val, first):
    @pl.when(first)
    def _():
        ref[...] = val

    @pl.when(jnp.logical_not(first))
    def _():
        ref[...] += val


def _win(arr, start, width):
    assert start % width == 0, (start, width)
    return (arr, start, width)


def _operand(entry):
    if isinstance(entry, tuple):
        arr, start, width = entry
        return arr, width, start // width
    return entry, entry.shape[1], 0


def _row_tile(rows, bytes_per_row):
    if rows <= 16:
        return rows
    t = 1024
    while t > 16 and (t * bytes_per_row > ROW_IO_BUDGET or rows % t):
        t //= 2
    assert rows % t == 0, (rows, t)
    return t


def _rowwise_call(fn, tiled, params, outs, accs, name):
    ops = [_operand(e) for e in tiled]
    rows = ops[0][0].shape[0]
    per_row = sum(w * a.dtype.itemsize for a, w, _ in ops) + sum(c * jnp.dtype(d).itemsize for c, d in outs)
    tile = _row_tile(rows, per_row)
    n_in = len(tiled) + len(params)
    n_o = len(outs)

    def body(*refs):
        vals = [r[...] for r in refs[:n_in]]
        t_out, a_out = fn(*vals)
        for r, v in zip(refs[n_in:n_in + n_o], t_out):
            r[...] = v.astype(r.dtype)
        first = pl.program_id(0) == 0
        for r, v in zip(refs[n_in + n_o:], a_out):
            _acc_store(r, v.astype(F32), first)

    def tiled_spec(width, blk):
        return pl.BlockSpec((tile, width), lambda i: (i, blk))

    in_specs = [tiled_spec(w, blk) for _, w, blk in ops]
    in_specs += [pl.BlockSpec(p.shape, lambda i: (0, 0)) for p in params]
    out_specs = [tiled_spec(c, 0) for c, _ in outs]
    out_specs += [pl.BlockSpec(s, lambda i: (0, 0)) for s in accs]
    out_shape = [jax.ShapeDtypeStruct((rows, c), d) for c, d in outs]
    out_shape += [jax.ShapeDtypeStruct(s, F32) for s in accs]
    return pl.pallas_call(
        body, grid=(rows // tile,), in_specs=in_specs, out_specs=out_specs, out_shape=out_shape,
        compiler_params=_cparams("arbitrary"), name=name,
    )(*[a for a, _, _ in ops], *params)


def _to_f32(vals):
    return [v.astype(F32) for v in vals]


def _row_fwd(f, tiled, params, out_dtypes, name):
    ops = [_operand(e) for e in tiled]
    rows = ops[0][0].shape[0]
    shapes = jax.eval_shape(f, *[jax.ShapeDtypeStruct((rows, w), F32) for _, w, _ in ops],
                            *[jax.ShapeDtypeStruct(p.shape, F32) for p in params])
    outs = [(s.shape[1], d) for s, d in zip(shapes, out_dtypes)]
    return _rowwise_call(lambda *v: (f(*_to_f32(v)), ()), tiled, params, outs, [], name)


def _row_bwd(f, tiled, params, gs, d_dtypes, name, bwd=None, add=None):
    n_t, n_g = len(tiled), len(gs)
    adds = sorted((add or {}).items())
    n_a = len(adds)

    def fn(*vals):
        vals = _to_f32(vals)
        prim = vals[:n_t] + vals[n_t + n_g + n_a:]
        g = tuple(vals[n_t:n_t + n_g])
        if bwd is not None:
            d_t, d_p = bwd(*prim, *g)
        else:
            _, vjp = jax.vjp(f, *prim)
            cts = vjp(g)
            d_t, d_p = cts[:n_t], cts[n_t:]
        d_t = list(d_t)
        for (idx, _), extra in zip(adds, vals[n_t + n_g:n_t + n_g + n_a]):
            d_t[idx] = d_t[idx] + extra
        return tuple(d_t), tuple(d_p)

    outs = [(_operand(e)[1], d) for e, d in zip(tiled, d_dtypes)]
    accs = [p.shape for p in params]
    res = _rowwise_call(fn, list(tiled) + list(gs) + [a for _, a in adds], params, outs, accs, name)
    return res[:n_t], res[n_t:]


def _f_rmsnorm(x, g):
    return (x * lax.rsqrt(jnp.mean(x * x, axis=-1, keepdims=True) + EPS) * g,)


def _f_swiglu(gu):
    gate, up = gu[:, :D_FF], gu[:, D_FF:]
    return (gate * jax.nn.sigmoid(gate) * up,)


def _b_swiglu(gu, d):
    gate, up = gu[:, :D_FF], gu[:, D_FF:]
    s = jax.nn.sigmoid(gate)
    d_gate = d * up * s * (1.0 + gate * (1.0 - s))
    d_up = d * gate * s
    return (jnp.concatenate([d_gate, d_up], axis=1),), ()


def _f_gated_norm(ys, xs, z, dsk, g):
    t = (ys + xs * dsk) * (z * jax.nn.sigmoid(z))
    return (t * lax.rsqrt(jnp.mean(t * t, axis=-1, keepdims=True) + EPS) * g,)


def _f_merge(gates, ys, ym):
    s = jax.nn.sigmoid(gates)
    return (s[:, :D_MODEL] * ys + s[:, D_MODEL:] * ym,)


def _b_merge(gates, ys, ym, d):
    s = jax.nn.sigmoid(gates)
    s1, s2 = s[:, :D_MODEL], s[:, D_MODEL:]
    d_gates = jnp.concatenate([d * ys * s1 * (1.0 - s1), d * ym * s2 * (1.0 - s2)], axis=1)
    return (d_gates, d * s1, d * s2), ()


def _loss_and_grad(y, target):
    def fn(yv, tv):
        d = yv - tv
        return (d * (1.0 / D_MODEL),), (jnp.sum(d * d, axis=0, keepdims=True) * (0.5 / D_MODEL),)

    dy, part = _rowwise_call(fn, [y, target], [], [(D_MODEL, F32)], [(1, D_MODEL)], "loss")
    return jnp.sum(part), dy


def _adam(w, g, m, v):
    def fn(wv, gv, mv, vv):
        m2 = ADAM_B1 * mv + (1.0 - ADAM_B1) * gv
        v2 = ADAM_B2 * vv + (1.0 - ADAM_B2) * (gv * gv)
        m_hat = m2 / (1.0 - ADAM_B1 ** ADAM_STEP)
        v_hat = v2 / (1.0 - ADAM_B2 ** ADAM_STEP)
        delta = -ADAM_LR * (m_hat / (jnp.sqrt(v_hat) + ADAM_EPS) + ADAM_WD * wv)
        return (delta, m2, v2), ()

    c = w.shape[1]
    return _rowwise_call(fn, [w, g, m, v], [], [(c, F32)] * 3, [], "adamw")


def _sum_blocks(blocks):
    _, rows, c = blocks.shape
    tile = _row_tile(rows, N_DEV * c * blocks.dtype.itemsize + c * 4)

    def body(b_ref, o_ref):
        acc = b_ref[0].astype(F32)
        for i in range(1, N_DEV):
            acc = acc + b_ref[i].astype(F32)
        o_ref[...] = acc

    return pl.pallas_call(
        body, grid=(rows // tile,), in_specs=[pl.BlockSpec((N_DEV, tile, c), lambda i: (0, i, 0))],
        out_specs=pl.BlockSpec((tile, c), lambda i: (i, 0)), out_shape=jax.ShapeDtypeStruct((rows, c), F32),
        compiler_params=_cparams("arbitrary"), name="sum_blocks",
    )(blocks)


def _mm(a, b, ta=False, tb=False, out_dtype=F32, alpha=1.0, res=None):
    r_dim, p_dim = a.shape if ta else a.shape[::-1]
    r2, q_dim = b.shape[::-1] if tb else b.shape
    assert r_dim == r2, (a.shape, b.shape, ta, tb)
    tp = _pick_tile(p_dim, 512, LANE)
    if tp < 512 < p_dim:
        tp = _pick_tile(p_dim, 1536, LANE)
    tq = _pick_tile(q_dim, 1536, LANE)
    tr = _pick_tile(r_dim, 1536, LANE)
    nr = r_dim // tr
    dims = (((0 if ta else 1,), (1 if tb else 0,)), ((), ()))
    has_res = res is not None

    def body(*refs):
        a_ref, b_ref = refs[:2]
        res_ref = refs[2] if has_res else None
        o_ref = refs[2 + has_res]

        def finish(val):
            if alpha != 1.0:
                val = val * alpha
            if has_res:
                val = val + res_ref[...].astype(F32)
            o_ref[...] = val.astype(o_ref.dtype)

        part = lax.dot_general(a_ref[...].astype(BF16), b_ref[...].astype(BF16), dims, preferred_element_type=F32)
        if nr == 1:
            finish(part)
        else:
            acc_ref = refs[3 + has_res]
            k = pl.program_id(2)
            _acc_store(acc_ref, part, k == 0)

            @pl.when(k == nr - 1)
            def _():
                finish(acc_ref[...])

    a_spec = pl.BlockSpec((tr, tp), lambda j, i, k: (k, i)) if ta else pl.BlockSpec((tp, tr), lambda j, i, k: (i, k))
    b_spec = pl.BlockSpec((tq, tr), lambda j, i, k: (j, k)) if tb else pl.BlockSpec((tr, tq), lambda j, i, k: (k, j))
    o_spec = pl.BlockSpec((tp, tq), lambda j, i, k: (i, j))
    return pl.pallas_call(
        body, grid=(q_dim // tq, p_dim // tp, nr), in_specs=[a_spec, b_spec] + ([o_spec] if has_res else []),
        out_specs=o_spec, out_shape=jax.ShapeDtypeStruct((p_dim, q_dim), out_dtype),
        scratch_shapes=[pltpu.VMEM((tp, tq), F32)] if nr > 1 else [],
        compiler_params=_cparams("arbitrary", "arbitrary", "arbitrary"),
        name=f"mm_{'t' if ta else 'n'}{'t' if tb else 'n'}_{p_dim}x{r_dim}x{q_dim}",
    )(*([a, b] + ([res] if has_res else [])))


ATTN_SCALE = QK ** -0.5
LOG2E = 1.4426950408889634
ATTN_C = ATTN_SCALE * LOG2E


def _attn_tile(s):
    return min(512, s)


def _causal_keep(t, keys_on_rows=False):
    row = lax.broadcasted_iota(jnp.int32, (t, t), 0)
    col = lax.broadcasted_iota(jnp.int32, (t, t), 1)
    return row <= col if keys_on_rows else col <= row


def _nt(a, b):
    return lax.dot_general(a, b, (((1,), (1,)), ((), ())), preferred_element_type=F32)


def _attn_fwd_call(q, k, v):
    nh, s, _ = q.shape
    t = _attn_tile(s)
    nb = s // t

    def body(q_ref, k_ref, v_ref, o_ref, lse_ref):
        qi = pl.program_id(1)
        q = q_ref[0]

        def block(kb, carry, diagonal):
            m_prev, l_prev, acc = carry
            start = pl.multiple_of(kb * t, t)
            sc = _nt(q, k_ref[0, pl.ds(start, t), :])
            if diagonal:
                sc = jnp.where(_causal_keep(t), sc, NEG)
            m_new = jnp.maximum(m_prev, jnp.max(sc, axis=-1, keepdims=True))
            p = jnp.exp2(sc * ATTN_C - m_new * ATTN_C)
            alpha = jnp.exp2((m_prev - m_new) * ATTN_C)
            l_new = alpha * l_prev + jnp.sum(p, axis=-1, keepdims=True)
            acc = alpha * acc + jnp.dot(p.astype(BF16), v_ref[0, pl.ds(start, t), :], preferred_element_type=F32)
            return m_new, l_new, acc

        init = (jnp.full((t, 1), NEG, F32), jnp.zeros((t, 1), F32), jnp.zeros((t, VDIM), F32))
        carry = lax.fori_loop(0, qi, lambda kb, c: block(kb, c, False), init)
        m, l, acc = block(qi, carry, True)
        o_ref[0] = (acc / l).astype(o_ref.dtype)
        lse_ref[0] = m * ATTN_SCALE + jnp.log(l)

    qmap = lambda h, i: (h, i, 0)
    whole = lambda h, i: (h, 0, 0)
    return pl.pallas_call(
        body, grid=(nh, nb),
        in_specs=[pl.BlockSpec((1, t, QK), qmap), pl.BlockSpec((1, s, QK), whole), pl.BlockSpec((1, s, VDIM), whole)],
        out_specs=[pl.BlockSpec((1, t, VDIM), qmap), pl.BlockSpec((1, t, 1), qmap)],
        out_shape=[jax.ShapeDtypeStruct((nh, s, VDIM), BF16), jax.ShapeDtypeStruct((nh, s, 1), F32)],
        compiler_params=_cparams("arbitrary", "arbitrary"), name="attn_fwd",
    )(q, k, v)


def _attn_delta_call(o, do):
    nh, s, d = o.shape

    def fn(ov, dv):
        return (jnp.sum(ov.astype(F32) * dv.astype(F32), axis=-1, keepdims=True),), ()

    return _rowwise_call(fn, [o.reshape(nh * s, d), do.reshape(nh * s, d)], [], [(1, F32)], [], "attn_delta")[0]


def _attn_bwd_call(q, k, v, do, lse_t, delta_t):
    nh, s, _ = q.shape
    t = _attn_tile(s)
    nb = s // t

    def body(q_ref, k_ref, v_ref, do_ref, lse_ref, delta_ref, dq_ref, dk_ref, dv_ref, dk_sc, dv_sc):
        kj = pl.program_id(1)

        @pl.when(kj == 0)
        def _():
            dq_ref[...] = jnp.zeros_like(dq_ref)

        dk_sc[...] = jnp.zeros_like(dk_sc)
        dv_sc[...] = jnp.zeros_like(dv_sc)
        kblk, vblk = k_ref[0], v_ref[0]

        def block(qb, diagonal):
            start = pl.multiple_of(qb * t, t)
            qblk = q_ref[0, pl.ds(start, t), :]
            doblk = do_ref[0, pl.ds(start, t), :]
            sc = _nt(kblk, qblk)
            if diagonal:
                sc = jnp.where(_causal_keep(t, keys_on_rows=True), sc, NEG)
            p = jnp.exp2(sc * ATTN_C - lse_ref[0, :, pl.ds(start, t)] * LOG2E)
            dv_sc[...] += jnp.dot(p.astype(BF16), doblk, preferred_element_type=F32)
            dp = _nt(vblk, doblk)
            ds = (p * (dp - delta_ref[0, :, pl.ds(start, t)])).astype(BF16)
            dk_sc[...] += jnp.dot(ds, qblk, preferred_element_type=F32)
            dq_ref[0, pl.ds(start, t), :] += lax.dot_general(ds, kblk, (((0,), (0,)), ((), ())), preferred_element_type=F32)

        block(kj, True)

        def rest(qb, carry):
            block(qb, False)
            return carry

        lax.fori_loop(kj + 1, nb, rest, 0)
        dk_ref[0] = (dk_sc[...] * ATTN_SCALE).astype(dk_ref.dtype)
        dv_ref[0] = dv_sc[...].astype(dv_ref.dtype)

        @pl.when(kj == nb - 1)
        def _():
            dq_ref[...] = dq_ref[...] * ATTN_SCALE

    kmap = lambda h, j: (h, j, 0)
    whole = lambda h, j: (h, 0, 0)
    return pl.pallas_call(
        body, grid=(nh, nb),
        in_specs=[pl.BlockSpec((1, s, QK), whole), pl.BlockSpec((1, t, QK), kmap), pl.BlockSpec((1, t, VDIM), kmap),
                  pl.BlockSpec((1, s, VDIM), whole), pl.BlockSpec((1, 1, s), whole), pl.BlockSpec((1, 1, s), whole)],
        out_specs=[pl.BlockSpec((1, s, QK), whole), pl.BlockSpec((1, t, QK), kmap), pl.BlockSpec((1, t, VDIM), kmap)],
        out_shape=[jax.ShapeDtypeStruct((nh, s, QK), F32), jax.ShapeDtypeStruct((nh, s, QK), F32), jax.ShapeDtypeStruct((nh, s, VDIM), F32)],
        scratch_shapes=[pltpu.VMEM((t, QK), F32), pltpu.VMEM((t, VDIM), F32)],
        compiler_params=_cparams("arbitrary", "arbitrary"), name="attn_bwd",
    )(q, k, v, do, lse_t, delta_t)


HEAD_COLS = NOPE + VDIM
HEADS_TILE = 256


def _swap_rope_halves(t, lane):
    half = ROPE // 2
    return jnp.where(lane < half, pltpu.roll(t, LANE - half, 1), pltpu.roll(t, half, 1))


def _head_fwd(n, p, gain, cs, sn, lane):
    r = lax.rsqrt((jnp.sum(n * n, axis=-1, keepdims=True) + jnp.sum(p * p, axis=-1, keepdims=True)) * (1.0 / QK) + EPS)
    yp = p * r * gain[:, NOPE:]
    return n * r * gain[:, :NOPE], yp * cs + _swap_rope_halves(yp, lane) * sn


def _head_bwd(n, p, gain, cs, sn, lane, dzn, dzp):
    r = lax.rsqrt((jnp.sum(n * n, axis=-1, keepdims=True) + jnp.sum(p * p, axis=-1, keepdims=True)) * (1.0 / QK) + EPS)
    dyp = dzp * cs + _swap_rope_halves(dzp * sn, lane)
    gyn, gyp = dzn * gain[:, :NOPE], dyp * gain[:, NOPE:]
    dot = jnp.sum(gyn * n, axis=-1, keepdims=True) + jnp.sum(gyp * p, axis=-1, keepdims=True)
    coef = dot * (r * r * r) * (1.0 / QK)
    d_gn = jnp.sum(dzn * n * r, axis=0, keepdims=True)
    d_gp = jnp.sum(dyp * p * r, axis=0, keepdims=True)
    return gyn * r - n * coef, gyp * r - p * coef, d_gn, d_gp


def _rope_key(last_ref, lane):
    return jnp.where(lane < ROPE, last_ref[...], 0.0)


def _heads_fwd_call(q, kv, proj, cs, sn, q_gain, k_gain):
    s = q.shape[0]
    t = min(HEADS_TILE, s)

    def body(q_ref, kv_ref, last_ref, cs_ref, sn_ref, qg_ref, kg_ref, qh_ref, kh_ref, vh_ref):
        lane = lax.broadcasted_iota(jnp.int32, (t, LANE), 1)
        cs_, sn_ = cs_ref[...], sn_ref[...]
        kp = _rope_key(last_ref, lane)
        for h in range(MLA_H):
            c0 = h * HEAD_COLS
            zn, zp = _head_fwd(q_ref[:, c0:c0 + NOPE], q_ref[:, c0 + NOPE:c0 + HEAD_COLS], qg_ref[...], cs_, sn_, lane)
            qh_ref[h, :, :NOPE] = zn.astype(BF16)
            qh_ref[h, :, NOPE:] = zp[:, :ROPE].astype(BF16)
            zn, zp = _head_fwd(kv_ref[:, c0:c0 + NOPE], kp, kg_ref[...], cs_, sn_, lane)
            kh_ref[h, :, :NOPE] = zn.astype(BF16)
            kh_ref[h, :, NOPE:] = zp[:, :ROPE].astype(BF16)
            vh_ref[h] = kv_ref[:, c0 + NOPE:c0 + HEAD_COLS].astype(BF16)

    rows = lambda i: (i, 0)
    whole = lambda i: (0, 0)
    heads = lambda i: (0, i, 0)
    wide = MLA_H * HEAD_COLS
    return pl.pallas_call(
        body, grid=(s // t,),
        in_specs=[pl.BlockSpec((t, wide), rows), pl.BlockSpec((t, wide), rows),
                  pl.BlockSpec((t, LANE), lambda i: (i, PROJ_LAST // LANE)),
                  pl.BlockSpec((t, LANE), rows), pl.BlockSpec((t, LANE), rows),
                  pl.BlockSpec((1, HEAD_COLS), whole), pl.BlockSpec((1, HEAD_COLS), whole)],
        out_specs=[pl.BlockSpec((MLA_H, t, QK), heads), pl.BlockSpec((MLA_H, t, QK), heads), pl.BlockSpec((MLA_H, t, VDIM), heads)],
        out_shape=[jax.ShapeDtypeStruct((MLA_H, s, QK), BF16), jax.ShapeDtypeStruct((MLA_H, s, QK), BF16),
                   jax.ShapeDtypeStruct((MLA_H, s, VDIM), BF16)],
        compiler_params=_cparams("arbitrary"), name="mla_heads_fwd",
    )(q, kv, proj, cs, sn, q_gain, k_gain)


def _heads_bwd_call(q, kv, proj, cs, sn, q_gain, k_gain, dqh, dkh, dvh):
    s = q.shape[0]
    t = min(HEADS_TILE, s)

    def body(q_ref, kv_ref, last_ref, cs_ref, sn_ref, qg_ref, kg_ref, dqh_ref, dkh_ref, dvh_ref,
             dq_ref, dkv_ref, dkr_ref, dqg_ref, dkg_ref):
        lane = lax.broadcasted_iota(jnp.int32, (t, LANE), 1)
        cs_, sn_ = cs_ref[...], sn_ref[...]
        kp = _rope_key(last_ref, lane)
        no_lanes = jnp.zeros((t, LANE - ROPE), F32)
        d_kp = jnp.zeros((t, LANE), F32)
        d_qg = [jnp.zeros((1, NOPE), F32), jnp.zeros((1, LANE), F32)]
        d_kg = [jnp.zeros((1, NOPE), F32), jnp.zeros((1, LANE), F32)]
        for h in range(MLA_H):
            c0 = h * HEAD_COLS
            dz = dqh_ref[h]
            dzp = jnp.concatenate([dz[:, NOPE:], no_lanes], axis=1)
            d_n, d_p, g_n, g_p = _head_bwd(q_ref[:, c0:c0 + NOPE], q_ref[:, c0 + NOPE:c0 + HEAD_COLS], qg_ref[...],
                                           cs_, sn_, lane, dz[:, :NOPE], dzp)
            dq_ref[:, c0:c0 + NOPE] = d_n.astype(dq_ref.dtype)
            dq_ref[:, c0 + NOPE:c0 + HEAD_COLS] = d_p.astype(dq_ref.dtype)
            d_qg = [d_qg[0] + g_n, d_qg[1] + g_p]
            dz = dkh_ref[h]
            dzp = jnp.concatenate([dz[:, NOPE:], no_lanes], axis=1)
            d_n, d_p, g_n, g_p = _head_bwd(kv_ref[:, c0:c0 + NOPE], kp, kg_ref[...], cs_, sn_, lane, dz[:, :NOPE], dzp)
            dkv_ref[:, c0:c0 + NOPE] = d_n.astype(dkv_ref.dtype)
            dkv_ref[:, c0 + NOPE:c0 + HEAD_COLS] = dvh_ref[h].astype(dkv_ref.dtype)
            d_kp = d_kp + d_p
            d_kg = [d_kg[0] + g_n, d_kg[1] + g_p]
        dkr_ref[...] = d_kp
        first = pl.program_id(0) == 0
        _acc_store(dqg_ref.at[:, pl.ds(0, NOPE)], d_qg[0], first)
        _acc_store(dqg_ref.at[:, pl.ds(NOPE, LANE)], d_qg[1], first)
        _acc_store(dkg_ref.at[:, pl.ds(0, NOPE)], d_kg[0], first)
        _acc_store(dkg_ref.at[:, pl.ds(NOPE, LANE)], d_kg[1], first)

    rows = lambda i: (i, 0)
    whole = lambda i: (0, 0)
    heads = lambda i: (0, i, 0)
    wide = MLA_H * HEAD_COLS
    return pl.pallas_call(
        body, grid=(s // t,),
        in_specs=[pl.BlockSpec((t, wide), rows), pl.BlockSpec((t, wide), rows),
                  pl.BlockSpec((t, LANE), lambda i: (i, PROJ_LAST // LANE)),
                  pl.BlockSpec((t, LANE), rows), pl.BlockSpec((t, LANE), rows),
                  pl.BlockSpec((1, HEAD_COLS), whole), pl.BlockSpec((1, HEAD_COLS), whole),
                  pl.BlockSpec((MLA_H, t, QK), heads), pl.BlockSpec((MLA_H, t, QK), heads), pl.BlockSpec((MLA_H, t, VDIM), heads)],
        out_specs=[pl.BlockSpec((t, wide), rows), pl.BlockSpec((t, wide), rows), pl.BlockSpec((t, LANE), rows),
                   pl.BlockSpec((1, HEAD_COLS), whole), pl.BlockSpec((1, HEAD_COLS), whole)],
        out_shape=[jax.ShapeDtypeStruct((s, wide), BF16), jax.ShapeDtypeStruct((s, wide), BF16), jax.ShapeDtypeStruct((s, LANE), F32),
                   jax.ShapeDtypeStruct((1, HEAD_COLS), F32), jax.ShapeDtypeStruct((1, HEAD_COLS), F32)],
        compiler_params=_cparams("arbitrary"), name="mla_heads_bwd",
    )(q, kv, proj, cs, sn, q_gain, k_gain, dqh, dkh, dvh)


CONV_TC = 512
HALO = 8


def _conv_tiles(s):
    return min(512, s)


def _conv_fwd_call(x, col0, w, b):
    s = x.shape[0]
    ts = _conv_tiles(s)
    hb = ts // HALO
    c0 = col0 // CONV_TC
    assert col0 % CONV_TC == 0

    def body(x_ref, prev_ref, w_ref, b_ref, y_ref, buf):
        si = pl.program_id(1)
        buf[0:HALO, :] = jnp.where(si > 0, prev_ref[...], 0.0)
        buf[HALO:, :] = x_ref[...]
        acc = jnp.broadcast_to(b_ref[...], (ts, CONV_TC))
        for k in range(CONV_K):
            acc = acc + w_ref[k:k + 1, :] * buf[pl.ds(HALO - (CONV_K - 1) + k, ts), :]
        y_ref[...] = acc * jax.nn.sigmoid(acc)

    return pl.pallas_call(
        body, grid=(CONV_DIM // CONV_TC, s // ts),
        in_specs=[pl.BlockSpec((ts, CONV_TC), lambda ci, si: (si, ci + c0)),
                  pl.BlockSpec((HALO, CONV_TC), lambda ci, si: (jnp.maximum(si * hb - 1, 0), ci + c0)),
                  pl.BlockSpec((CONV_K, CONV_TC), lambda ci, si: (0, ci)),
                  pl.BlockSpec((1, CONV_TC), lambda ci, si: (0, ci))],
        out_specs=pl.BlockSpec((ts, CONV_TC), lambda ci, si: (si, ci)),
        out_shape=jax.ShapeDtypeStruct((s, CONV_DIM), F32),
        scratch_shapes=[pltpu.VMEM((ts + HALO, CONV_TC), F32)],
        compiler_params=_cparams("arbitrary", "arbitrary"), name="conv_fwd",
    )(x, x, w, b)


def _conv_bwd_call(x, col0, w, b, dy):
    s = x.shape[0]
    ts = _conv_tiles(s)
    hb = ts // HALO
    ns = s // ts
    last_halo = s // HALO - 1
    c0 = col0 // CONV_TC

    def body(x_ref, prev_ref, next_ref, dy_ref, dyn_ref, w_ref, b_ref, dx_ref, dw_ref, db_ref, xbuf, dbuf):
        si = pl.program_id(1)
        xbuf[0:HALO, :] = jnp.where(si > 0, prev_ref[...], 0.0)
        xbuf[HALO:HALO + ts, :] = x_ref[...]
        xbuf[HALO + ts:, :] = next_ref[...]
        pre = jnp.broadcast_to(b_ref[...], (ts + HALO, CONV_TC))
        for k in range(CONV_K):
            pre = pre + w_ref[k:k + 1, :] * xbuf[pl.ds(HALO - (CONV_K - 1) + k, ts + HALO), :]
        sg = jax.nn.sigmoid(pre)
        dsilu = sg * (1.0 + pre * (1.0 - sg))
        dbuf[0:ts, :] = dy_ref[...] * dsilu[0:ts]
        dbuf[ts:, :] = jnp.where(si < ns - 1, dyn_ref[...] * dsilu[ts:], 0.0)
        dx = jnp.zeros((ts, CONV_TC), F32)
        for k in range(CONV_K):
            dx = dx + w_ref[k:k + 1, :] * dbuf[pl.ds(CONV_K - 1 - k, ts), :]
        dx_ref[...] = dx.astype(dx_ref.dtype)
        dpre = dbuf[0:ts, :]
        first = si == 0
        _acc_store(db_ref, jnp.sum(dpre, axis=0, keepdims=True), first)
        for k in range(CONV_K):
            dw_k = jnp.sum(dpre * xbuf[pl.ds(HALO - (CONV_K - 1) + k, ts), :], axis=0, keepdims=True)
            _acc_store(dw_ref.at[pl.ds(k, 1), :], dw_k, first)

    main = lambda ci, si: (si, ci)
    x_main = lambda ci, si: (si, ci + c0)
    x_prev = lambda ci, si: (jnp.maximum(si * hb - 1, 0), ci + c0)
    x_next = lambda ci, si: (jnp.minimum(si * hb + hb, last_halo), ci + c0)
    return pl.pallas_call(
        body, grid=(CONV_DIM // CONV_TC, ns),
        in_specs=[pl.BlockSpec((ts, CONV_TC), x_main), pl.BlockSpec((HALO, CONV_TC), x_prev), pl.BlockSpec((HALO, CONV_TC), x_next),
                  pl.BlockSpec((ts, CONV_TC), main),
                  pl.BlockSpec((HALO, CONV_TC), lambda ci, si: (jnp.minimum(si * hb + hb, last_halo), ci)),
                  pl.BlockSpec((CONV_K, CONV_TC), lambda ci, si: (0, ci)),
                  pl.BlockSpec((1, CONV_TC), lambda ci, si: (0, ci))],
        out_specs=[pl.BlockSpec((ts, CONV_TC), main),
                   pl.BlockSpec((CONV_K, CONV_TC), lambda ci, si: (0, ci)),
                   pl.BlockSpec((1, CONV_TC), lambda ci, si: (0, ci))],
        out_shape=[jax.ShapeDtypeStruct((s, CONV_DIM), BF16), jax.ShapeDtypeStruct((CONV_K, CONV_DIM), F32),
                   jax.ShapeDtypeStruct((1, CONV_DIM), F32)],
        scratch_shapes=[pltpu.VMEM((ts + 2 * HALO, CONV_TC), F32), pltpu.VMEM((ts + HALO, CONV_TC), F32)],
        compiler_params=_cparams("arbitrary", "arbitrary"), name="conv_bwd",
    )(x, x, x, dy, dy, w, b)


GW = SSD_HPG * SSD_P
B_COL = SSD_DI
C_COL = SSD_DI + SSD_G * SSD_N


def _ones_where(mask):
    return jnp.where(mask, 1.0, 0.0).astype(BF16)


def _split3(v):
    hi = v.astype(BF16)
    r1 = v - hi.astype(F32)
    mid = r1.astype(BF16)
    lo = (r1 - mid.astype(F32)).astype(BF16)
    return hi, mid, lo


def _dot_sel_r(v, sel):
    out = None
    for part in _split3(v):
        t = jnp.dot(part, sel, preferred_element_type=F32)
        out = t if out is None else out + t
    return out


def _dot_sel_l(sel, v):
    out = None
    for part in _split3(v):
        t = jnp.dot(sel, part, preferred_element_type=F32)
        out = t if out is None else out + t
    return out


def _ssd_consts():
    r = lax.broadcasted_iota(jnp.int32, (SSD_L, SSD_L), 0)
    c = lax.broadcasted_iota(jnp.int32, (SSD_L, SSD_L), 1)
    tril = r >= c
    triu = c >= r
    shift = SSD_P.bit_length() - 1
    eh = lax.broadcasted_iota(jnp.int32, (SSD_H, SSD_DI), 0)
    ej = lax.broadcasted_iota(jnp.int32, (SSD_H, SSD_DI), 1)
    expand = _ones_where(lax.shift_right_logical(ej, shift) == eh)
    rj = lax.broadcasted_iota(jnp.int32, (SSD_DI, SSD_H), 0)
    rh = lax.broadcasted_iota(jnp.int32, (SSD_DI, SSD_H), 1)
    reduce_ = _ones_where(lax.shift_right_logical(rj, shift) == rh)
    lane = lax.broadcasted_iota(jnp.int32, (SSD_L, LANE), 1)
    return tril, triu, expand, reduce_, lane < SSD_P


def _ssd_decays(dt, dt_t, a, a_t, tril, triu, expand):
    dta = dt * a
    acum = _dot_sel_l(_ones_where(tril), dta)
    acum_t = _dot_sel_r(dt_t * a_t, _ones_where(triu))
    dta_e = _dot_sel_r(dta, expand)
    acum_e = _dot_sel_r(acum, expand)
    last_e = jnp.sum(dta_e, axis=0, keepdims=True)
    return acum, acum_t, acum_e, last_e


def _head_decay(acum, acum_t, h, tril):
    seg = acum[:, h:h + 1] - acum_t[h:h + 1, :]
    return jnp.exp(jnp.where(tril, seg, NEG))


def _ssd_fwd_call(xbc, dt, a):
    s = xbc.shape[0]
    nc = s // SSD_L
    dt_t = dt.T
    a_t = a.T

    def body(xbc_ref, dt_ref, dtt_ref, a_ref, at_ref, y_ref, st_ref, s_sc):
        ci = pl.program_id(0)

        @pl.when(ci == 0)
        def _():
            s_sc[...] = jnp.zeros_like(s_sc)

        st_ref[0] = s_sc[...]
        tril, triu, expand, _, low_half = _ssd_consts()
        acum, acum_t, acum_e, last_e = _ssd_decays(dt_ref[...], dtt_ref[...], a_ref[...], at_ref[...], tril, triu, expand)
        dt_e = _dot_sel_r(dt_ref[...], expand)
        xdt = xbc_ref[:, :SSD_DI] * dt_e
        xdt_b = xdt.astype(BF16)
        xw_b = (xdt * jnp.exp(last_e - acum_e)).astype(BF16)
        ea_e = jnp.exp(acum_e)
        el_e = jnp.exp(last_e)
        for g in range(SSD_G):
            gs = slice(g * GW, (g + 1) * GW)
            bg = xbc_ref[:, B_COL + g * SSD_N:B_COL + (g + 1) * SSD_N]
            cg_b = xbc_ref[:, C_COL + g * SSD_N:C_COL + (g + 1) * SSD_N].astype(BF16)
            bg_b = bg.astype(BF16)
            cb = _nt(cg_b, bg_b)
            st = s_sc[:, gs]
            y_off = jnp.dot(cg_b, st.astype(BF16), preferred_element_type=F32) * ea_e[:, gs]
            for pr in range(SSD_HPG // 2):
                ls = slice(g * GW + pr * LANE, g * GW + (pr + 1) * LANE)
                xp = xdt_b[:, ls]
                yd = []
                for half in range(2):
                    h = g * SSD_HPG + pr * 2 + half
                    m = (cb * _head_decay(acum, acum_t, h, tril)).astype(BF16)
                    yd.append(jnp.dot(m, xp, preferred_element_type=F32))
                y_ref[:, ls] = jnp.where(low_half, yd[0], yd[1]) + y_off[:, pr * LANE:(pr + 1) * LANE]
            s_sc[:, gs] = st * el_e[:, gs] + jnp.dot(bg.T.astype(BF16), xw_b[:, gs], preferred_element_type=F32)

    row = lambda i: (i, 0)
    return pl.pallas_call(
        body, grid=(nc,),
        in_specs=[pl.BlockSpec((SSD_L, CONV_DIM), row), pl.BlockSpec((SSD_L, SSD_H), row),
                  pl.BlockSpec((SSD_H, SSD_L), lambda i: (0, i)), pl.BlockSpec((1, SSD_H), lambda i: (0, 0)),
                  pl.BlockSpec((SSD_H, 1), lambda i: (0, 0))],
        out_specs=[pl.BlockSpec((SSD_L, SSD_DI), row), pl.BlockSpec((1, SSD_N, SSD_DI), lambda i: (i, 0, 0))],
        out_shape=[jax.ShapeDtypeStruct((s, SSD_DI), F32), jax.ShapeDtypeStruct((nc, SSD_N, SSD_DI), F32)],
        scratch_shapes=[pltpu.VMEM((SSD_N, SSD_DI), F32)],
        compiler_params=_cparams("arbitrary"), name="ssd_fwd",
    )(xbc, dt, dt_t, a, a_t)


def _ssd_bwd_call(xbc, dt, a, states, dy, dx_extra):
    s = xbc.shape[0]
    nc = s // SSD_L
    dt_t = dt.T
    a_t = a.T

    def body(xbc_ref, dt_ref, dtt_ref, a_ref, at_ref, st_ref, dy_ref, dxe_ref,
             dxbc_ref, ddt_ref, da_ref, ds_sc, yf_sc, dxd_sc, dxw_sc):
        i = pl.program_id(0)

        @pl.when(i == 0)
        def _():
            ds_sc[...] = jnp.zeros_like(ds_sc)

        tril, triu, expand, reduce_, low_half = _ssd_consts()
        dt = dt_ref[...]
        a_row = a_ref[...]
        acum, acum_t, acum_e, last_e = _ssd_decays(dt, dtt_ref[...], a_row, at_ref[...], tril, triu, expand)
        dt_e = _dot_sel_r(dt, expand)
        x = xbc_ref[:, :SSD_DI]
        xdt = x * dt_e
        xdt_b = xdt.astype(BF16)
        w_e = jnp.exp(last_e - acum_e)
        xw_b = (xdt * w_e).astype(BF16)
        ea_e = jnp.exp(acum_e)
        el_e = jnp.exp(last_e)
        dy = dy_ref[...]
        dy_b = dy.astype(BF16)
        s_prev = st_ref[0]
        ds_new = ds_sc[...]
        ds_new_b = ds_new.astype(BF16)
        triu_b = _ones_where(triu)
        strict_tril = jnp.logical_not(triu)
        head_ids = lax.broadcasted_iota(jnp.int32, (1, SSD_H), 1)
        d_dta_diag = jnp.zeros((SSD_L, SSD_H), F32)
        for g in range(SSD_G):
            gs = slice(g * GW, (g + 1) * GW)
            bs_ = slice(B_COL + g * SSD_N, B_COL + (g + 1) * SSD_N)
            cs_ = slice(C_COL + g * SSD_N, C_COL + (g + 1) * SSD_N)
            bg = xbc_ref[:, bs_]
            cg = xbc_ref[:, cs_]
            bg_b, cg_b = bg.astype(BF16), cg.astype(BF16)
            st_b = s_prev[:, gs].astype(BF16)
            y_off = jnp.dot(cg_b, st_b, preferred_element_type=F32) * ea_e[:, gs]
            yf_sc[:, gs] = y_off
            dz_b = (dy[:, gs] * ea_e[:, gs]).astype(BF16)
            d_c = _nt(dz_b, st_b)
            ds_prev = ds_new[:, gs] * el_e[:, gs] + jnp.dot(cg.T.astype(BF16), dz_b, preferred_element_type=F32)
            dxw_sc[:, gs] = jnp.dot(bg_b, ds_new_b[:, gs], preferred_element_type=F32)
            d_b = _nt(xw_b[:, gs], ds_new_b[:, gs])
            cb = _nt(cg_b, bg_b)
            d_g = jnp.zeros((SSD_L, SSD_L), F32)
            for pr in range(SSD_HPG // 2):
                ls = slice(g * GW + pr * LANE, g * GW + (pr + 1) * LANE)
                xp = xdt_b[:, ls]
                dyp = dy[:, ls]
                dyp_b = dy_b[:, ls]
                dxd = []
                for half in range(2):
                    h = g * SSD_HPG + pr * 2 + half
                    dec = _head_decay(acum, acum_t, h, tril)
                    m = cb * dec
                    dxd.append(jnp.dot(m.T.astype(BF16), dyp_b, preferred_element_type=F32))
                    mine = low_half if half == 0 else jnp.logical_not(low_half)
                    d_m = _nt(jnp.where(mine, dyp, 0.0).astype(BF16), xp)
                    d_g = d_g + d_m * dec
                    below = jnp.dot(triu_b, (d_m * m).astype(BF16), preferred_element_type=F32)
                    col = jnp.sum(jnp.where(strict_tril, below, 0.0), axis=1, keepdims=True)
                    d_dta_diag = d_dta_diag + col * jnp.where(head_ids == h, 1.0, 0.0)
                dxd_sc[:, ls] = jnp.where(low_half, dxd[0], dxd[1])
            d_g_b = d_g.astype(BF16)
            dxbc_ref[:, cs_] = d_c + jnp.dot(d_g_b, bg_b, preferred_element_type=F32)
            dxbc_ref[:, bs_] = d_b + jnp.dot(d_g.T.astype(BF16), cg_b, preferred_element_type=F32)
            ds_sc[:, gs] = ds_prev
        dxw = dxw_sc[...]
        dxd = dxd_sc[...]
        dw_e = xdt * dxw * w_e
        d_acum_e = dy * yf_sc[...] - dw_e
        d_last_e = jnp.sum(ds_new * s_prev, axis=0, keepdims=True) * el_e + jnp.sum(dw_e, axis=0, keepdims=True)
        suffix = _dot_sel_l(triu_b, d_acum_e)
        d_dta = _dot_sel_r(suffix + d_last_e, reduce_) + d_dta_diag
        dxdt = dxd + dxw * w_e
        dxbc_ref[:, :SSD_DI] = dxdt * dt_e + dxe_ref[...]
        ddt_ref[...] = d_dta * a_row + _dot_sel_r(dxdt * x, reduce_)
        _acc_store(da_ref, jnp.sum(d_dta * dt, axis=0, keepdims=True), i == 0)

    rev = lambda i: (nc - 1 - i, 0)
    return pl.pallas_call(
        body, grid=(nc,),
        in_specs=[pl.BlockSpec((SSD_L, CONV_DIM), rev), pl.BlockSpec((SSD_L, SSD_H), rev),
                  pl.BlockSpec((SSD_H, SSD_L), lambda i: (0, nc - 1 - i)), pl.BlockSpec((1, SSD_H), lambda i: (0, 0)),
                  pl.BlockSpec((SSD_H, 1), lambda i: (0, 0)),
                  pl.BlockSpec((1, SSD_N, SSD_DI), lambda i: (nc - 1 - i, 0, 0)),
                  pl.BlockSpec((SSD_L, SSD_DI), rev), pl.BlockSpec((SSD_L, SSD_DI), rev)],
        out_specs=[pl.BlockSpec((SSD_L, CONV_DIM), rev), pl.BlockSpec((SSD_L, SSD_H), rev),
                   pl.BlockSpec((1, SSD_H), lambda i: (0, 0))],
        out_shape=[jax.ShapeDtypeStruct((s, CONV_DIM), F32), jax.ShapeDtypeStruct((s, SSD_H), F32),
                   jax.ShapeDtypeStruct((1, SSD_H), F32)],
        scratch_shapes=[pltpu.VMEM((SSD_N, SSD_DI), F32), pltpu.VMEM((SSD_L, SSD_DI), F32),
                        pltpu.VMEM((SSD_L, SSD_DI), F32), pltpu.VMEM((SSD_L, SSD_DI), F32)],
        compiler_params=_cparams("arbitrary"), name="ssd_bwd",
    )(xbc, dt, dt_t, a, a_t, states, dy, dx_extra)


HBM_SPEC = pl.BlockSpec(memory_space=pltpu.HBM)
N_PEERS = N_DEV - 1


def _flip(v, f):
    return 1 - v if f else v


def _all_gather(shard):
    rows, c = shard.shape

    def body(x_ref, out_ref, send_sems, recv_sems, local_sem):
        x, y, cc = lax.axis_index("x"), lax.axis_index("y"), lax.axis_index("c")
        me, sibling = (x, y, cc), (x, y, 1 - cc)
        chips = [(1 - x, y), (x, 1 - y), (1 - x, 1 - y)]

        def slot(px, py, pc):
            return out_ref.at[4 * px + 2 * py + pc]

        def copy(k, block, to, src=None):
            return pltpu.make_async_remote_copy(
                src_ref=slot(*block) if src is None else src, dst_ref=slot(*block),
                send_sem=send_sems.at[k], recv_sem=recv_sems.at[k],
                device_id=to, device_id_type=pl.DeviceIdType.MESH)

        mine = pltpu.make_async_copy(x_ref, slot(*me), local_sem)
        mine.start()
        first = [copy(0, me, sibling, src=x_ref)]
        first += [copy(1 + j, me, (*chip, cc), src=x_ref) for j, chip in enumerate(chips)]
        for cp in first:
            cp.start()
        passed = [copy(4 + j, (*chip, cc), sibling) for j, chip in enumerate(chips)]
        for j, chip in enumerate(chips):
            copy(1 + j, (*chip, cc), me).wait_recv()
            passed[j].start()
        copy(0, sibling, me).wait_recv()
        for j, chip in enumerate(chips):
            copy(4 + j, (*chip, 1 - cc), me).wait_recv()
        for cp in first + passed:
            cp.wait_send()
        mine.wait()

    return pl.pallas_call(
        body, out_shape=jax.ShapeDtypeStruct((N_DEV, rows, c), shard.dtype),
        in_specs=[HBM_SPEC], out_specs=HBM_SPEC,
        scratch_shapes=[pltpu.SemaphoreType.DMA((N_PEERS,)), pltpu.SemaphoreType.DMA((N_PEERS,)), pltpu.SemaphoreType.DMA(())],
        name="all_gather",
    )(shard)


def _exchange_blocks(blocks):
    _, rows, c = blocks.shape

    def body(g_ref, out_ref, send_sems, recv_sems, local_sem):
        x, y, cc = lax.axis_index("x"), lax.axis_index("y"), lax.axis_index("c")
        me = 4 * x + 2 * y + cc
        mine = pltpu.make_async_copy(g_ref.at[me], out_ref.at[me], local_sem)
        mine.start()
        copies = []
        for k in range(1, N_DEV):
            px, py, pc = _flip(x, k & 4), _flip(y, k & 2), _flip(cc, k & 1)
            peer = 4 * px + 2 * py + pc
            copies.append((
                pltpu.make_async_remote_copy(
                    src_ref=g_ref.at[peer], dst_ref=out_ref.at[me], send_sem=send_sems.at[k - 1], recv_sem=recv_sems.at[k - 1],
                    device_id=(px, py, pc), device_id_type=pl.DeviceIdType.MESH),
                pltpu.make_async_remote_copy(
                    src_ref=g_ref.at[peer], dst_ref=out_ref.at[peer], send_sem=send_sems.at[k - 1], recv_sem=recv_sems.at[k - 1],
                    device_id=(px, py, pc), device_id_type=pl.DeviceIdType.MESH)))
        for send, _ in copies:
            send.start()
        for _, landed in copies:
            landed.wait_recv()
        for send, _ in copies:
            send.wait_send()
        mine.wait()

    return pl.pallas_call(
        body, out_shape=jax.ShapeDtypeStruct(blocks.shape, blocks.dtype),
        in_specs=[HBM_SPEC], out_specs=HBM_SPEC,
        scratch_shapes=[pltpu.SemaphoreType.DMA((N_PEERS,)), pltpu.SemaphoreType.DMA((N_PEERS,)), pltpu.SemaphoreType.DMA(())],
        name="exchange_blocks",
    )(blocks)


BIG = [
    ("ffn1_w13", (D_MODEL, 2 * D_FF), 1), ("ffn1_w2", (D_FF, D_MODEL), 0), ("w_in", (D_MODEL, D_IN), 1),
    ("w_ssd_out", (SSD_DI, D_MODEL), 0), ("w_uq", (Q_LORA, MLA_H * QK), 1), ("w_ukv", (KV_LORA, MLA_H * (NOPE + VDIM)), 1),
    ("w_mla_out", (MLA_H * VDIM, D_MODEL), 0), ("w_o", (D_MODEL, D_MODEL), 0),
    ("ffn2_w13", (D_MODEL, 2 * D_FF), 1), ("ffn2_w2", (D_FF, D_MODEL), 0),
]
SMALL = [
    ("ln_ffn1", D_MODEL), ("ln_mix", D_MODEL), ("conv_b", CONV_DIM), ("dt_bias", SSD_H), ("a_log", SSD_H), ("d_skip", SSD_H),
    ("ssd_norm", SSD_DI), ("q_lora_norm", Q_LORA), ("kv_lora_norm", KV_LORA), ("q_norm", QK), ("k_norm", QK), ("ln_ffn2", D_MODEL),
]


def _shard_shape(full, axis):
    k, n = full
    return (k // N_DEV, n) if axis == 0 else (k, n // N_DEV)


def _shard_rows(full):
    return full[0] * full[1] // N_DEV // PACK_COLS


LAYER_ROWS = sum(_shard_rows(f) for _, f, _ in BIG)
LAYER_ROWS_PAD = -(-LAYER_ROWS // 256) * 256


def _pack_shards(shards):
    parts = [(shards[name] if axis == 0 else shards[name].T).reshape(-1, PACK_COLS) for name, _, axis in BIG]
    pad = LAYER_ROWS_PAD - LAYER_ROWS
    if pad:
        parts.append(jnp.zeros((pad, PACK_COLS), parts[0].dtype))
    return jnp.concatenate(parts, axis=0)


def _unpack_shards(packed):
    out, r = {}, 0
    for name, full, axis in BIG:
        n = _shard_rows(full)
        k, c = _shard_shape(full, axis)
        blk = packed[r:r + n]
        out[name] = blk.reshape(k, c) if axis == 0 else blk.reshape(c, k).T
        r += n
    return out


def _working_shape(full, axis):
    return full if axis == 0 else full[::-1]


def _unpack_gathered(gathered):
    out, r = {}, 0
    for name, full, axis in BIG:
        n = _shard_rows(full)
        out[name] = gathered[:, r:r + n].reshape(_working_shape(full, axis))
        r += n
    return out


def _pack_full_grads(grads):
    parts = [grads[name].reshape(N_DEV, -1, PACK_COLS) for name, _, _ in BIG]
    pad = LAYER_ROWS_PAD - LAYER_ROWS
    if pad:
        parts.append(jnp.zeros((N_DEV, pad, PACK_COLS), parts[0].dtype))
    return jnp.concatenate(parts, axis=1)


SMALL_COLS = sum(n for _, n in SMALL) + CONV_K * CONV_DIM
SMALL_ROWS = -(-(DEPTH * SMALL_COLS) // (8 * PACK_COLS)) * 8


def _pack_small(vals, conv_w):
    flat = jnp.concatenate([vals[name] for name, _ in SMALL] + [conv_w.reshape(DEPTH, -1)], axis=1).reshape(-1)
    flat = jnp.concatenate([flat, jnp.zeros((SMALL_ROWS * PACK_COLS - flat.shape[0],), F32)])
    return flat.reshape(SMALL_ROWS, PACK_COLS)


def _unpack_small(packed):
    flat = packed.reshape(-1)[:DEPTH * SMALL_COLS].reshape(DEPTH, SMALL_COLS)
    out, c = {}, 0
    for name, n in SMALL:
        out[name] = flat[:, c:c + n]
        c += n
    return out, flat[:, c:].reshape(DEPTH, CONV_K, CONV_DIM)


_IN_OFFS = [sum(IN_SPLIT[:i]) for i in range(len(IN_SPLIT) + 1)]


def _arrange_w_in(w_t):
    z, xbc, dt, cq, ckv, kr, gates = [w_t[_IN_OFFS[i]:_IN_OFFS[i + 1]] for i in range(len(IN_SPLIT))]
    pad = jnp.zeros((LANE - ROPE - SSD_H, w_t.shape[1]), w_t.dtype)
    return jnp.concatenate([z, gates, xbc, cq, ckv, kr, dt, pad], axis=0)


def _restore_w_in(g):
    z, gates, xbc = g[PROJ_Z:PROJ_GATES], g[PROJ_GATES:PROJ_XBC], g[PROJ_XBC:PROJ_CQ]
    cq, ckv = g[PROJ_CQ:PROJ_CKV], g[PROJ_CKV:PROJ_LAST]
    kr, dt = g[PROJ_LAST:PROJ_LAST + ROPE], g[PROJ_LAST + ROPE:PROJ_LAST + ROPE + SSD_H]
    return jnp.concatenate([z, xbc, dt, cq, ckv, kr, gates], axis=0)


def _pad_heads(w_t):
    k = w_t.shape[1]
    return jnp.pad(w_t.reshape(MLA_H, QK, k), ((0, 0), (0, HEAD_COLS - QK), (0, 0))).reshape(MLA_H * HEAD_COLS, k)


def _unpad_heads(g):
    k = g.shape[1]
    return g.reshape(MLA_H, HEAD_COLS, k)[:, :QK].reshape(MLA_H * QK, k)


def _row(v):
    return v.reshape(1, -1)


def _head_gain(g):
    return jnp.pad(g, (0, HEAD_COLS - QK)).reshape(1, HEAD_COLS)


def _ffn_fwd(h, ln, w13_t, w2, name):
    n = _row_fwd(_f_rmsnorm, [h], [_row(ln)], [BF16], name + "_fwd")[0]
    gu = _mm(n, w13_t, tb=True)
    act = _row_fwd(_f_swiglu, [gu], [], [BF16], "swiglu_fwd")[0]
    return _mm(act, w2, alpha=0.5, res=h), (h, n, gu, act)


def _ffn_bwd(dh_out, saved, ln, w13_t, w2, name):
    h, n, gu, act = saved
    d_act = _mm(dh_out, w2, tb=True, out_dtype=BF16, alpha=0.5)
    d_w2 = _mm(act, dh_out, ta=True, out_dtype=BF16, alpha=0.5)
    d_gu = _row_bwd(_f_swiglu, [gu], [], [d_act], [BF16], "swiglu_bwd", bwd=_b_swiglu)[0][0]
    d_n = _mm(d_gu, w13_t, out_dtype=BF16)
    d_w13_t = _mm(d_gu, n, ta=True, out_dtype=BF16)
    (dh,), (d_ln,) = _row_bwd(_f_rmsnorm, [h], [_row(ln)], [d_n], [F32], name + "_bwd", add={0: dh_out})
    return dh, d_w13_t, d_w2, d_ln[0]


def _mixer_fwd(h, big, small, conv_w, cs, sn):
    s = h.shape[0]
    u = _row_fwd(_f_rmsnorm, [h], [_row(small["ln_mix"])], [BF16], "ln_mix_fwd")[0]
    proj = _mm(u, big["w_in"], tb=True)
    xbc = _conv_fwd_call(proj, PROJ_XBC, conv_w, _row(small["conv_b"]))
    dt_in = proj[:, PROJ_LAST + ROPE:PROJ_LAST + ROPE + SSD_H] + small["dt_bias"][None, :]
    dt = jax.nn.softplus(dt_in)
    a = -jnp.exp(small["a_log"])[None, :]
    y_scan, states = _ssd_fwd_call(xbc, dt, a)
    dsk = _row(jnp.repeat(small["d_skip"], SSD_P))
    gn_in = [y_scan, _win(xbc, 0, SSD_DI), _win(proj, PROJ_Z, SSD_DI)]
    yn = _row_fwd(_f_gated_norm, gn_in, [dsk, _row(small["ssd_norm"])], [BF16], "gated_norm_fwd")[0]
    y_ssd = _mm(yn, big["w_ssd_out"])
    qn = _row_fwd(_f_rmsnorm, [_win(proj, PROJ_CQ, Q_LORA)], [_row(small["q_lora_norm"])], [BF16], "q_lora_norm_fwd")[0]
    kvn = _row_fwd(_f_rmsnorm, [_win(proj, PROJ_CKV, KV_LORA)], [_row(small["kv_lora_norm"])], [BF16], "kv_lora_norm_fwd")[0]
    q = _mm(qn, big["w_uq"], tb=True)
    kv = _mm(kvn, big["w_ukv"], tb=True)
    qh, kh, vh = _heads_fwd_call(q, kv, proj, cs, sn, _head_gain(small["q_norm"]), _head_gain(small["k_norm"]))
    o, lse = _attn_fwd_call(qh, kh, vh)
    o_rows = jnp.transpose(o, (1, 0, 2)).reshape(s, MLA_H * VDIM)
    y_mla = _mm(o_rows, big["w_mla_out"])
    mg = _row_fwd(_f_merge, [_win(proj, PROJ_GATES, 2 * D_MODEL), y_ssd, y_mla], [], [BF16], "merge_fwd")[0]
    out = _mm(mg, big["w_o"], res=h)
    return out, (h, u, proj, xbc, dt_in, dt, a, y_scan, states, dsk, yn, y_ssd, qn, kvn, q, kv, qh, kh, vh, o, lse, o_rows, y_mla, mg)


def _mixer_bwd(dh_out, saved, big, small, conv_w, cs, sn):
    (h, u, proj, xbc, dt_in, dt, a, y_scan, states, dsk, yn, y_ssd, qn, kvn, q, kv, qh, kh, vh, o, lse, o_rows, y_mla, mg) = saved
    s = h.shape[0]
    d_big, d_small = {}, {}
    d_mg = _mm(dh_out, big["w_o"], tb=True, out_dtype=BF16)
    d_big["w_o"] = _mm(mg, dh_out, ta=True, out_dtype=BF16)
    merge_in = [_win(proj, PROJ_GATES, 2 * D_MODEL), y_ssd, y_mla]
    (d_gates, d_y_ssd, d_y_mla), _ = _row_bwd(_f_merge, merge_in, [], [d_mg], [BF16, BF16, BF16], "merge_bwd", bwd=_b_merge)
    d_o_rows = _mm(d_y_mla, big["w_mla_out"], tb=True, out_dtype=BF16)
    d_big["w_mla_out"] = _mm(o_rows, d_y_mla, ta=True, out_dtype=BF16)
    d_o = jnp.transpose(d_o_rows.reshape(s, MLA_H, VDIM), (1, 0, 2))
    delta = _attn_delta_call(o, d_o)
    d_heads = _attn_bwd_call(qh, kh, vh, d_o, lse.reshape(MLA_H, 1, s), delta.reshape(MLA_H, 1, s))
    d_q, d_kv, d_kr, d_qg, d_kg = _heads_bwd_call(
        q, kv, proj, cs, sn, _head_gain(small["q_norm"]), _head_gain(small["k_norm"]), *d_heads)
    d_small["q_norm"], d_small["k_norm"] = d_qg[0, :QK], d_kg[0, :QK]
    d_qn = _mm(d_q, big["w_uq"], out_dtype=BF16)
    d_big["w_uq"] = _mm(d_q, qn, ta=True, out_dtype=BF16)
    d_kvn = _mm(d_kv, big["w_ukv"], out_dtype=BF16)
    d_big["w_ukv"] = _mm(d_kv, kvn, ta=True, out_dtype=BF16)
    (d_cq,), (d_g,) = _row_bwd(_f_rmsnorm, [_win(proj, PROJ_CQ, Q_LORA)], [_row(small["q_lora_norm"])], [d_qn], [BF16], "q_lora_norm_bwd")
    d_small["q_lora_norm"] = d_g[0]
    (d_ckv,), (d_g,) = _row_bwd(_f_rmsnorm, [_win(proj, PROJ_CKV, KV_LORA)], [_row(small["kv_lora_norm"])], [d_kvn], [BF16], "kv_lora_norm_bwd")
    d_small["kv_lora_norm"] = d_g[0]
    d_yn = _mm(d_y_ssd, big["w_ssd_out"], tb=True, out_dtype=BF16)
    d_big["w_ssd_out"] = _mm(yn, d_y_ssd, ta=True, out_dtype=BF16)
    gn_in = [y_scan, _win(xbc, 0, SSD_DI), _win(proj, PROJ_Z, SSD_DI)]
    (d_y_scan, d_xs, d_z), (d_dsk, d_g) = _row_bwd(
        _f_gated_norm, gn_in, [dsk, _row(small["ssd_norm"])], [d_yn], [F32, F32, BF16], "gated_norm_bwd")
    d_small["ssd_norm"] = d_g[0]
    d_small["d_skip"] = jnp.sum(d_dsk.reshape(SSD_H, SSD_P), axis=1)
    d_xbc_act, d_dt, d_a = _ssd_bwd_call(xbc, dt, a, states, d_y_scan, d_xs)
    d_xbc, d_conv_w, d_conv_b = _conv_bwd_call(proj, PROJ_XBC, conv_w, _row(small["conv_b"]), d_xbc_act)
    d_small["conv_b"] = d_conv_b[0]
    d_dt_in = d_dt * jax.nn.sigmoid(dt_in)
    d_small["dt_bias"] = jnp.sum(d_dt_in, axis=0)
    d_small["a_log"] = d_a[0] * a[0]
    d_last = (d_kr + jnp.pad(d_dt_in, ((0, 0), (ROPE, LANE - ROPE - SSD_H)))).astype(BF16)
    d_proj = jnp.concatenate([d_z, d_gates, d_xbc, d_cq, d_ckv, d_last], axis=1)
    d_u = _mm(d_proj, big["w_in"], out_dtype=BF16)
    d_big["w_in"] = _mm(d_proj, u, ta=True, out_dtype=BF16)
    (dh,), (d_ln,) = _row_bwd(_f_rmsnorm, [h], [_row(small["ln_mix"])], [d_u], [F32], "ln_mix_bwd", add={0: dh_out})
    d_small["ln_mix"] = d_ln[0]
    return dh, d_big, d_small, d_conv_w


def _local_step(x, positions, target, big, small, conv_w):
    inv = 1.0 / (ROPE_THETA ** (jnp.arange(0, ROPE, 2, dtype=F32) / ROPE))
    ang = positions.astype(F32)[:, None] * inv
    cos, sin = jnp.cos(ang), jnp.sin(ang)
    no_lanes = jnp.zeros((x.shape[0], LANE - ROPE), F32)
    cs = jnp.concatenate([cos, cos, no_lanes], axis=1)
    sn = jnp.concatenate([-sin, sin, no_lanes], axis=1)
    big = [dict(b, w_in=_arrange_w_in(b["w_in"]), w_uq=_pad_heads(b["w_uq"])) for b in big]
    layer_small = [{k: v[l] for k, v in small.items()} for l in range(DEPTH)]

    h, saved = x, []
    for l in range(DEPTH):
        b, sm = big[l], layer_small[l]
        h, s1 = _ffn_fwd(h, sm["ln_ffn1"], b["ffn1_w13"], b["ffn1_w2"], "ln_ffn1")
        h, s2 = _mixer_fwd(h, b, sm, conv_w[l], cs, sn)
        h, s3 = _ffn_fwd(h, sm["ln_ffn2"], b["ffn2_w13"], b["ffn2_w2"], "ln_ffn2")
        saved.append((s1, s2, s3))
    loss, dh = _loss_and_grad(h, target)

    d_big, d_small, d_conv_w = [None] * DEPTH, [None] * DEPTH, [None] * DEPTH
    for l in reversed(range(DEPTH)):
        b, sm = big[l], layer_small[l]
        s1, s2, s3 = saved[l]
        dh, d_w13_2, d_w2_2, d_ln2 = _ffn_bwd(dh, s3, sm["ln_ffn2"], b["ffn2_w13"], b["ffn2_w2"], "ln_ffn2")
        dh, db, ds, d_conv_w[l] = _mixer_bwd(dh, s2, b, sm, conv_w[l], cs, sn)
        dh, d_w13_1, d_w2_1, d_ln1 = _ffn_bwd(dh, s1, sm["ln_ffn1"], b["ffn1_w13"], b["ffn1_w2"], "ln_ffn1")
        db.update(ffn1_w13=d_w13_1, ffn1_w2=d_w2_1, ffn2_w13=d_w13_2, ffn2_w2=d_w2_2,
                  w_in=_restore_w_in(db["w_in"]), w_uq=_unpad_heads(db["w_uq"]))
        ds.update(ln_ffn1=d_ln1, ln_ffn2=d_ln2)
        d_big[l], d_small[l] = db, ds
    d_small = {name: jnp.stack([d_small[l][name] for l in range(DEPTH)]) for name, _ in SMALL}
    return loss, dh, d_big, d_small, jnp.stack(d_conv_w)


def _step(args):
    dev = 4 * lax.axis_index("x") + 2 * lax.axis_index("y") + lax.axis_index("c")
    x, positions, target = args["x"][0], args["positions"][0], args["loss_target"][0]

    big = []
    for l in range(DEPTH):
        packed = _pack_shards({name: args[name][l].astype(BF16) for name, _, _ in BIG})
        big.append(_unpack_gathered(_all_gather(packed)))
    cw = args["conv_w"]
    cw_cols = cw.shape[-1]
    cw_rows = -(-cw.size // (8 * PACK_COLS)) * 8
    cw_flat = jnp.concatenate([cw.reshape(-1), jnp.zeros((cw_rows * PACK_COLS - cw.size,), F32)]).reshape(cw_rows, PACK_COLS)
    cw_all = _all_gather(cw_flat).reshape(N_DEV, -1)[:, :cw.size].reshape(N_DEV, DEPTH, CONV_K, cw_cols)
    conv_w = jnp.transpose(cw_all, (1, 2, 0, 3)).reshape(DEPTH, CONV_K, CONV_DIM)
    small = {name: args[name] for name, _ in SMALL}

    loss, dx, d_big, d_small, d_conv_w = _local_step(x, positions, target, big, small, conv_w)
    loss = lax.psum(loss, MESH_AXES)

    out = {"loss": loss, "grad_x": dx[None]}

    grads = {name: [] for name, _, _ in BIG}
    for l in range(DEPTH):
        summed = _sum_blocks(_exchange_blocks(_pack_full_grads(d_big[l])))
        for name, g in _unpack_shards(summed).items():
            grads[name].append(g)
    flat = lambda t: t.reshape(-1, t.shape[-1])
    for name, _, _ in BIG:
        g = jnp.stack(grads[name])
        w = args[name]
        delta, m2, v2 = _adam(flat(w), flat(g), flat(args["m_" + name]), flat(args["v_" + name]))
        out["grad_" + name] = g
        out["delta_" + name] = delta.reshape(w.shape)
        out["new_m_" + name] = m2.reshape(w.shape)
        out["new_v_" + name] = v2.reshape(w.shape)

    total = _sum_blocks(_all_gather(_pack_small(d_small, d_conv_w)))
    g_conv_w = _unpack_small(total)[1]
    zeros_cw = jnp.zeros((DEPTH, CONV_K, CONV_DIM), F32)
    delta, m2, v2 = _adam(_pack_small(small, zeros_cw), total,
                          _pack_small({name: args["m_" + name] for name, _ in SMALL}, zeros_cw),
                          _pack_small({name: args["v_" + name] for name, _ in SMALL}, zeros_cw))
    for kind, packed in (("grad_", total), ("delta_", delta), ("new_m_", m2), ("new_v_", v2)):
        for name, val in _unpack_small(packed)[0].items():
            out[kind + name] = val
    g_cw = lax.dynamic_slice_in_dim(g_conv_w, dev * cw_cols, cw_cols, axis=2)
    delta, m2, v2 = _adam(flat(cw), flat(g_cw), flat(args["m_conv_w"]), flat(args["v_conv_w"]))
    out["grad_conv_w"] = g_cw
    out["delta_conv_w"] = delta.reshape(cw.shape)
    out["new_m_conv_w"] = m2.reshape(cw.shape)
    out["new_v_conv_w"] = v2.reshape(cw.shape)
    return out


WEIGHTS = ["ln_ffn1", "ffn1_w13", "ffn1_w2", "ln_mix", "w_in", "conv_w", "conv_b", "dt_bias", "a_log", "d_skip", "ssd_norm",
           "w_ssd_out", "q_lora_norm", "w_uq", "kv_lora_norm", "w_ukv", "q_norm", "k_norm", "w_mla_out", "w_o", "ln_ffn2",
           "ffn2_w13", "ffn2_w2"]
ARG_NAMES = (["x", "positions"] + WEIGHTS + ["loss_target"] + ["m_" + n for n in WEIGHTS] + ["v_" + n for n in WEIGHTS])


def kernel(x, positions, ln_ffn1, ffn1_w13, ffn1_w2, ln_mix, w_in, conv_w, conv_b, dt_bias, a_log, d_skip, ssd_norm, w_ssd_out, q_lora_norm, w_uq, kv_lora_norm, w_ukv, q_norm, k_norm, w_mla_out, w_o, ln_ffn2, ffn2_w13, ffn2_w2, loss_target, m_ln_ffn1, m_ffn1_w13, m_ffn1_w2, m_ln_mix, m_w_in, m_conv_w, m_conv_b, m_dt_bias, m_a_log, m_d_skip, m_ssd_norm, m_w_ssd_out, m_q_lora_norm, m_w_uq, m_kv_lora_norm, m_w_ukv, m_q_norm, m_k_norm, m_w_mla_out, m_w_o, m_ln_ffn2, m_ffn2_w13, m_ffn2_w2, v_ln_ffn1, v_ffn1_w13, v_ffn1_w2, v_ln_mix, v_w_in, v_conv_w, v_conv_b, v_dt_bias, v_a_log, v_d_skip, v_ssd_norm, v_w_ssd_out, v_q_lora_norm, v_w_uq, v_kv_lora_norm, v_w_ukv, v_q_norm, v_k_norm, v_w_mla_out, v_w_o, v_ln_ffn2, v_ffn2_w13, v_ffn2_w2):
    vals = (x, positions, ln_ffn1, ffn1_w13, ffn1_w2, ln_mix, w_in, conv_w, conv_b, dt_bias, a_log, d_skip, ssd_norm, w_ssd_out, q_lora_norm, w_uq, kv_lora_norm, w_ukv, q_norm, k_norm, w_mla_out, w_o, ln_ffn2, ffn2_w13, ffn2_w2, loss_target, m_ln_ffn1, m_ffn1_w13, m_ffn1_w2, m_ln_mix, m_w_in, m_conv_w, m_conv_b, m_dt_bias, m_a_log, m_d_skip, m_ssd_norm, m_w_ssd_out, m_q_lora_norm, m_w_uq, m_kv_lora_norm, m_w_ukv, m_q_norm, m_k_norm, m_w_mla_out, m_w_o, m_ln_ffn2, m_ffn2_w13, m_ffn2_w2, v_ln_ffn1, v_ffn1_w13, v_ffn1_w2, v_ln_mix, v_w_in, v_conv_w, v_conv_b, v_dt_bias, v_a_log, v_d_skip, v_ssd_norm, v_w_ssd_out, v_q_lora_norm, v_w_uq, v_kv_lora_norm, v_w_ukv, v_q_norm, v_k_norm, v_w_mla_out, v_w_o, v_ln_ffn2, v_ffn2_w13, v_ffn2_w2)
    out = _step(dict(zip(ARG_NAMES, vals)))
    order = ["loss", "grad_x"] + [k + n for k in ("grad_", "delta_", "new_m_", "new_v_") for n in WEIGHTS]
    return tuple(out[n] for n in order)
```

```python
import jax
import jax.numpy as jnp
from jax import lax
from jax.experimental import pallas as pl
from jax.experimental.pallas import tpu as pltpu

F32 = jnp.float32
BF16 = jnp.bfloat16

D_MODEL = 1024
D_FF = 2816
DEPTH = 2
SSD_DI = 2048
SSD_P = 64
SSD_H = 32
SSD_G = 4
SSD_HPG = 8
SSD_N = 128
SSD_L = 128
CONV_K = 4
CONV_DIM = 3072
MLA_H = 8
Q_LORA = 512
KV_LORA = 256
NOPE = 128
ROPE = 64
VDIM = 128
QK = 192
ROPE_THETA = 10000.0
EPS = 1e-6
IN_SPLIT = (SSD_DI, CONV_DIM, SSD_H, Q_LORA, KV_LORA, ROPE, 2 * D_MODEL)
D_IN = sum(IN_SPLIT)
N_DEV = 8
LANE = 128
PACK_COLS = 1024

PROJ_Z = 0
PROJ_GATES = PROJ_Z + SSD_DI
PROJ_XBC = PROJ_GATES + 2 * D_MODEL
PROJ_CQ = PROJ_XBC + CONV_DIM
PROJ_CKV = PROJ_CQ + Q_LORA
PROJ_LAST = PROJ_CKV + KV_LORA
D_IN_PAD = PROJ_LAST + LANE

ADAM_LR = 0.001
ADAM_B1 = 0.9
ADAM_B2 = 0.999
ADAM_EPS = 1e-08
ADAM_WD = 0.01
ADAM_STEP = 10

VMEM_LIMIT = 48 * 1024 * 1024
ROW_IO_BUDGET = 8 * 1024 * 1024
NEG = -1e30

MESH_AXES = ("x", "y", "c")


def _cparams(*sem):
    return pltpu.CompilerParams(dimension_semantics=sem, vmem_limit_bytes=VMEM_LIMIT)


def _pick_tile(n, target, align):
    if n <= target:
        return n
    best = None
    for t in range(align, target + 1, align):
        if n % t == 0:
            best = t
    assert best is not None, (n, target, align)
    return best


def _acc_store(ref, val, first):
    @pl.when(first)
    def _():
        ref[...] = val

    @pl.when(jnp.logical_not(first))
    def _():
        ref[...] += val


def _win(arr, start, width):
    assert start % width == 0, (start, width)
    return (arr, start, width)


def _operand(entry):
    if isinstance(entry, tuple):
        arr, start, width = entry
        return arr, width, start // width
    return entry, entry.shape[1], 0


def _row_tile(rows, bytes_per_row):
    if rows <= 16:
        return rows
    t = 1024
    while t > 16 and (t * bytes_per_row > ROW_IO_BUDGET or rows % t):
        t //= 2
    assert rows % t == 0, (rows, t)
    return t


def _rowwise_call(fn, tiled, params, outs, accs, name):
    ops = [_operand(e) for e in tiled]
    rows = ops[0][0].shape[0]
    per_row = sum(w * a.dtype.itemsize for a, w, _ in ops) + sum(c * jnp.dtype(d).itemsize for c, d in outs)
    tile = _row_tile(rows, per_row)
    n_in = len(tiled) + len(params)
    n_o = len(outs)

    def body(*refs):
        vals = [r[...] for r in refs[:n_in]]
        t_out, a_out = fn(*vals)
        for r, v in zip(refs[n_in:n_in + n_o], t_out):
            r[...] = v.astype(r.dtype)
        first = pl.program_id(0) == 0
        for r, v in zip(refs[n_in + n_o:], a_out):
            _acc_store(r, v.astype(F32), first)

    def tiled_spec(width, blk):
        return pl.BlockSpec((tile, width), lambda i: (i, blk))

    in_specs = [tiled_spec(w, blk) for _, w, blk in ops]
    in_specs += [pl.BlockSpec(p.shape, lambda i: (0, 0)) for p in params]
    out_specs = [tiled_spec(c, 0) for c, _ in outs]
    out_specs += [pl.BlockSpec(s, lambda i: (0, 0)) for s in accs]
    out_shape = [jax.ShapeDtypeStruct((rows, c), d) for c, d in outs]
    out_shape += [jax.ShapeDtypeStruct(s, F32) for s in accs]
    return pl.pallas_call(
        body, grid=(rows // tile,), in_specs=in_specs, out_specs=out_specs, out_shape=out_shape,
        compiler_params=_cparams("arbitrary"), name=name,
    )(*[a for a, _, _ in ops], *params)


def _to_f32(vals):
    return [v.astype(F32) for v in vals]


def _row_fwd(f, tiled, params, out_dtypes, name):
    ops = [_operand(e) for e in tiled]
    rows = ops[0][0].shape[0]
    shapes = jax.eval_shape(f, *[jax.ShapeDtypeStruct((rows, w), F32) for _, w, _ in ops],
                            *[jax.ShapeDtypeStruct(p.shape, F32) for p in params])
    outs = [(s.shape[1], d) for s, d in zip(shapes, out_dtypes)]
    return _rowwise_call(lambda *v: (f(*_to_f32(v)), ()), tiled, params, outs, [], name)


def _row_bwd(f, tiled, params, gs, d_dtypes, name, bwd=None, add=None):
    n_t, n_g = len(tiled), len(gs)
    adds = sorted((add or {}).items())
    n_a = len(adds)

    def fn(*vals):
        vals = _to_f32(vals)
        prim = vals[:n_t] + vals[n_t + n_g + n_a:]
        g = tuple(vals[n_t:n_t + n_g])
        if bwd is not None:
            d_t, d_p = bwd(*prim, *g)
        else:
            _, vjp = jax.vjp(f, *prim)
            cts = vjp(g)
            d_t, d_p = cts[:n_t], cts[n_t:]
        d_t = list(d_t)
        for (idx, _), extra in zip(adds, vals[n_t + n_g:n_t + n_g + n_a]):
            d_t[idx] = d_t[idx] + extra
        return tuple(d_t), tuple(d_p)

    outs = [(_operand(e)[1], d) for e, d in zip(tiled, d_dtypes)]
    accs = [p.shape for p in params]
    res = _rowwise_call(fn, list(tiled) + list(gs) + [a for _, a in adds], params, outs, accs, name)
    return res[:n_t], res[n_t:]


def _f_rmsnorm(x, g):
    return (x * lax.rsqrt(jnp.mean(x * x, axis=-1, keepdims=True) + EPS) * g,)


def _f_swiglu(gu):
    gate, up = gu[:, :D_FF], gu[:, D_FF:]
    return (gate * jax.nn.sigmoid(gate) * up,)


def _b_swiglu(gu, d):
    gate, up = gu[:, :D_FF], gu[:, D_FF:]
    s = jax.nn.sigmoid(gate)
    d_gate = d * up * s * (1.0 + gate * (1.0 - s))
    d_up = d * gate * s
    return (jnp.concatenate([d_gate, d_up], axis=1),), ()


def _f_gated_norm(ys, xs, z, dsk, g):
    t = (ys + xs * dsk) * (z * jax.nn.sigmoid(z))
    return (t * lax.rsqrt(jnp.mean(t * t, axis=-1, keepdims=True) + EPS) * g,)


def _f_merge(gates, ys, ym):
    s = jax.nn.sigmoid(gates)
    return (s[:, :D_MODEL] * ys + s[:, D_MODEL:] * ym,)


def _b_merge(gates, ys, ym, d):
    s = jax.nn.sigmoid(gates)
    s1, s2 = s[:, :D_MODEL], s[:, D_MODEL:]
    d_gates = jnp.concatenate([d * ys * s1 * (1.0 - s1), d * ym * s2 * (1.0 - s2)], axis=1)
    return (d_gates, d * s1, d * s2), ()


def _loss_and_grad(y, target):
    def fn(yv, tv):
        d = yv - tv
        return (d * (1.0 / D_MODEL),), (jnp.sum(d * d, axis=0, keepdims=True) * (0.5 / D_MODEL),)

    dy, part = _rowwise_call(fn, [y, target], [], [(D_MODEL, F32)], [(1, D_MODEL)], "loss")
    return jnp.sum(part), dy


def _adam(w, g, m, v):
    def fn(wv, gv, mv, vv):
        m2 = ADAM_B1 * mv + (1.0 - ADAM_B1) * gv
        v2 = ADAM_B2 * vv + (1.0 - ADAM_B2) * (gv * gv)
        m_hat = m2 / (1.0 - ADAM_B1 ** ADAM_STEP)
        v_hat = v2 / (1.0 - ADAM_B2 ** ADAM_STEP)
        delta = -ADAM_LR * (m_hat / (jnp.sqrt(v_hat) + ADAM_EPS) + ADAM_WD * wv)
        return (delta, m2, v2), ()

    c = w.shape[1]
    return _rowwise_call(fn, [w, g, m, v], [], [(c, F32)] * 3, [], "adamw")


def _sum_blocks(blocks):
    _, rows, c = blocks.shape
    tile = _row_tile(rows, N_DEV * c * blocks.dtype.itemsize + c * 4)

    def body(b_ref, o_ref):
        acc = b_ref[0].astype(F32)
        for i in range(1, N_DEV):
            acc = acc + b_ref[i].astype(F32)
        o_ref[...] = acc

    return pl.pallas_call(
        body, grid=(rows // tile,), in_specs=[pl.BlockSpec((N_DEV, tile, c), lambda i: (0, i, 0))],
        out_specs=pl.BlockSpec((tile, c), lambda i: (i, 0)), out_shape=jax.ShapeDtypeStruct((rows, c), F32),
        compiler_params=_cparams("arbitrary"), name="sum_blocks",
    )(blocks)


def _mm(a, b, ta=False, tb=False, out_dtype=F32, alpha=1.0, res=None):
    r_dim, p_dim = a.shape if ta else a.shape[::-1]
    r2, q_dim = b.shape[::-1] if tb else b.shape
    assert r_dim == r2, (a.shape, b.shape, ta, tb)
    tp = _pick_tile(p_dim, 512, LANE)
    if tp < 512 < p_dim:
        tp = _pick_tile(p_dim, 1536, LANE)
    tq = _pick_tile(q_dim, 1536, LANE)
    tr = _pick_tile(r_dim, 1536, LANE)
    nr = r_dim // tr
    dims = (((0 if ta else 1,), (1 if tb else 0,)), ((), ()))
    has_res = res is not None

    def body(*refs):
        a_ref, b_ref = refs[:2]
        res_ref = refs[2] if has_res else None
        o_ref = refs[2 + has_res]

        def finish(val):
            if alpha != 1.0:
                val = val * alpha
            if has_res:
                val = val + res_ref[...].astype(F32)
            o_ref[...] = val.astype(o_ref.dtype)

        part = lax.dot_general(a_ref[...].astype(BF16), b_ref[...].astype(BF16), dims, preferred_element_type=F32)
        if nr == 1:
            finish(part)
        else:
            acc_ref = refs[3 + has_res]
            k = pl.program_id(2)
            _acc_store(acc_ref, part, k == 0)

            @pl.when(k == nr - 1)
            def _():
                finish(acc_ref[...])

    a_spec = pl.BlockSpec((tr, tp), lambda j, i, k: (k, i)) if ta else pl.BlockSpec((tp, tr), lambda j, i, k: (i, k))
    b_spec = pl.BlockSpec((tq, tr), lambda j, i, k: (j, k)) if tb else pl.BlockSpec((tr, tq), lambda j, i, k: (k, j))
    o_spec = pl.BlockSpec((tp, tq), lambda j, i, k: (i, j))
    return pl.pallas_call(
        body, grid=(q_dim // tq, p_dim // tp, nr), in_specs=[a_spec, b_spec] + ([o_spec] if has_res else []),
        out_specs=o_spec, out_shape=jax.ShapeDtypeStruct((p_dim, q_dim), out_dtype),
        scratch_shapes=[pltpu.VMEM((tp, tq), F32)] if nr > 1 else [],
        compiler_params=_cparams("arbitrary", "arbitrary", "arbitrary"),
        name=f"mm_{'t' if ta else 'n'}{'t' if tb else 'n'}_{p_dim}x{r_dim}x{q_dim}",
    )(*([a, b] + ([res] if has_res else [])))


ATTN_SCALE = QK ** -0.5
LOG2E = 1.4426950408889634
ATTN_C = ATTN_SCALE * LOG2E


def _attn_tile(s):
    return min(512, s)


def _causal_keep(t, keys_on_rows=False):
    row = lax.broadcasted_iota(jnp.int32, (t, t), 0)
    col = lax.broadcasted_iota(jnp.int32, (t, t), 1)
    return row <= col if keys_on_rows else col <= row


def _nt(a, b):
    return lax.dot_general(a, b, (((1,), (1,)), ((), ())), preferred_element_type=F32)


def _rider_phases(rider, src_ref, out_ref, sems, first, last):
    @pl.when(first)
    def _():
        _peer_copies(src_ref, out_ref, sems, rider["gather"], "start")

    def finish():
        @pl.when(last)
        def _():
            _peer_copies(src_ref, out_ref, sems, rider["gather"], "finish")

    return finish


def _attn_fwd_call(q, k, v, rider=None):
    nh, s, _ = q.shape
    t = _attn_tile(s)
    nb = s // t
    n_r = 0 if rider is None else 1

    def body(*refs):
        q_ref, k_ref, v_ref = refs[:3]
        o_ref, lse_ref = refs[3 + n_r:5 + n_r]
        qi = pl.program_id(1)
        finish = None
        if rider is not None:
            h = pl.program_id(0)
            finish = _rider_phases(rider, refs[3], refs[5 + n_r], refs[6 + n_r:],
                                   jnp.logical_and(h == 0, qi == 0), jnp.logical_and(h == nh - 1, qi == nb - 1))
        q = q_ref[0]

        def block(kb, carry, diagonal, width=1):
            m_prev, l_prev, acc = carry
            start = pl.multiple_of(kb * t, t)
            sc = _nt(q, k_ref[0, pl.ds(start, width * t), :])
            if diagonal:
                sc = jnp.where(_causal_keep(t), sc, NEG)
            m_new = jnp.maximum(m_prev, jnp.max(sc, axis=-1, keepdims=True))
            p = jnp.exp2(sc * ATTN_C - m_new * ATTN_C)
            alpha = jnp.exp2((m_prev - m_new) * ATTN_C)
            l_new = alpha * l_prev + jnp.sum(p, axis=-1, keepdims=True)
            acc = alpha * acc + jnp.dot(p.astype(BF16), v_ref[0, pl.ds(start, width * t), :], preferred_element_type=F32)
            return m_new, l_new, acc

        init = (jnp.full((t, 1), NEG, F32), jnp.zeros((t, 1), F32), jnp.zeros((t, VDIM), F32))
        carry = lax.fori_loop(0, qi // 2, lambda j, c: block(2 * j, c, False, width=2), init)
        carry = lax.cond(qi % 2 == 1, lambda c: block(qi - 1, c, False), lambda c: c, carry)
        m, l, acc = block(qi, carry, True)
        o_ref[0] = (acc / l).astype(o_ref.dtype)
        lse_ref[0] = m * ATTN_SCALE + jnp.log(l)
        if finish is not None:
            finish()

    qmap = lambda h, i: (h, i, 0)
    whole = lambda h, i: (h, 0, 0)
    return pl.pallas_call(
        body, grid=(nh, nb),
        in_specs=[pl.BlockSpec((1, t, QK), qmap), pl.BlockSpec((1, s, QK), whole), pl.BlockSpec((1, s, VDIM), whole)] + [HBM_SPEC] * n_r,
        out_specs=[pl.BlockSpec((1, t, VDIM), qmap), pl.BlockSpec((1, t, 1), qmap)] + [HBM_SPEC] * n_r,
        out_shape=[jax.ShapeDtypeStruct((nh, s, VDIM), BF16), jax.ShapeDtypeStruct((nh, s, 1), F32)] + ([rider["out"]] if n_r else []),
        scratch_shapes=_comm_scratch() if n_r else [],
        compiler_params=_cparams("arbitrary", "arbitrary"), name="attn_fwd_gather" if n_r else "attn_fwd",
    )(*([q, k, v] + ([rider["src"]] if n_r else [])))


def _attn_delta_call(o, do):
    nh, s, d = o.shape

    def fn(ov, dv):
        return (jnp.sum(ov.astype(F32) * dv.astype(F32), axis=-1, keepdims=True),), ()

    return _rowwise_call(fn, [o.reshape(nh * s, d), do.reshape(nh * s, d)], [], [(1, F32)], [], "attn_delta")[0]


def _attn_bwd_call(q, k, v, do, lse_t, delta_t, rider=None):
    nh, s, _ = q.shape
    t = _attn_tile(s)
    nb = s // t
    n_r = 0 if rider is None else 1

    def body(*refs):
        q_ref, k_ref, v_ref, do_ref, lse_ref, delta_ref = refs[:6]
        dq_ref, dk_ref, dv_ref = refs[6 + n_r:9 + n_r]
        dk_sc, dv_sc = refs[9 + 2 * n_r:11 + 2 * n_r]
        kj = pl.program_id(1)
        finish = None
        if rider is not None:
            h = pl.program_id(0)
            finish = _rider_phases(rider, refs[6], refs[9 + n_r], refs[11 + 2 * n_r:],
                                   jnp.logical_and(h == 0, kj == 0), jnp.logical_and(h == nh - 1, kj == nb - 1))

        @pl.when(kj == 0)
        def _():
            dq_ref[...] = jnp.zeros_like(dq_ref)

        dk_sc[...] = jnp.zeros_like(dk_sc)
        dv_sc[...] = jnp.zeros_like(dv_sc)
        kblk, vblk = k_ref[0], v_ref[0]

        def block(qb, diagonal):
            start = pl.multiple_of(qb * t, t)
            qblk = q_ref[0, pl.ds(start, t), :]
            doblk = do_ref[0, pl.ds(start, t), :]
            sc = _nt(kblk, qblk)
            if diagonal:
                sc = jnp.where(_causal_keep(t, keys_on_rows=True), sc, NEG)
            p = jnp.exp2(sc * ATTN_C - lse_ref[0, :, pl.ds(start, t)] * LOG2E)
            dv_sc[...] += jnp.dot(p.astype(BF16), doblk, preferred_element_type=F32)
            dp = _nt(vblk, doblk)
            ds = (p * (dp - delta_ref[0, :, pl.ds(start, t)])).astype(BF16)
            dk_sc[...] += jnp.dot(ds, qblk, preferred_element_type=F32)
            dq_ref[0, pl.ds(start, t), :] += lax.dot_general(ds, kblk, (((0,), (0,)), ((), ())), preferred_element_type=F32)

        block(kj, True)

        def rest(qb, carry):
            block(qb, False)
            return carry

        lax.fori_loop(kj + 1, nb, rest, 0)
        dk_ref[0] = (dk_sc[...] * ATTN_SCALE).astype(dk_ref.dtype)
        dv_ref[0] = dv_sc[...].astype(dv_ref.dtype)

        @pl.when(kj == nb - 1)
        def _():
            dq_ref[...] = dq_ref[...] * ATTN_SCALE

        if finish is not None:
            finish()

    kmap = lambda h, j: (h, j, 0)
    whole = lambda h, j: (h, 0, 0)
    return pl.pallas_call(
        body, grid=(nh, nb),
        in_specs=[pl.BlockSpec((1, s, QK), whole), pl.BlockSpec((1, t, QK), kmap), pl.BlockSpec((1, t, VDIM), kmap),
                  pl.BlockSpec((1, s, VDIM), whole), pl.BlockSpec((1, 1, s), whole), pl.BlockSpec((1, 1, s), whole)] + [HBM_SPEC] * n_r,
        out_specs=[pl.BlockSpec((1, s, QK), whole), pl.BlockSpec((1, t, QK), kmap), pl.BlockSpec((1, t, VDIM), kmap)] + [HBM_SPEC] * n_r,
        out_shape=[jax.ShapeDtypeStruct((nh, s, QK), F32), jax.ShapeDtypeStruct((nh, s, QK), F32),
                   jax.ShapeDtypeStruct((nh, s, VDIM), F32)] + ([rider["out"]] if n_r else []),
        scratch_shapes=[pltpu.VMEM((t, QK), F32), pltpu.VMEM((t, VDIM), F32)] + (_comm_scratch() if n_r else []),
        compiler_params=_cparams("arbitrary", "arbitrary"), name="attn_bwd_exchange" if n_r else "attn_bwd",
    )(*([q, k, v, do, lse_t, delta_t] + ([rider["src"]] if n_r else [])))


HEAD_COLS = NOPE + VDIM
HEADS_TILE = 256


def _swap_rope_halves(t, lane):
    half = ROPE // 2
    return jnp.where(lane < half, pltpu.roll(t, LANE - half, 1), pltpu.roll(t, half, 1))


def _head_fwd(n, p, gain, cs, sn, lane):
    r = lax.rsqrt((jnp.sum(n * n, axis=-1, keepdims=True) + jnp.sum(p * p, axis=-1, keepdims=True)) * (1.0 / QK) + EPS)
    yp = p * r * gain[:, NOPE:]
    return n * r * gain[:, :NOPE], yp * cs + _swap_rope_halves(yp, lane) * sn


def _head_bwd(n, p, gain, cs, sn, lane, dzn, dzp):
    r = lax.rsqrt((jnp.sum(n * n, axis=-1, keepdims=True) + jnp.sum(p * p, axis=-1, keepdims=True)) * (1.0 / QK) + EPS)
    dyp = dzp * cs + _swap_rope_halves(dzp * sn, lane)
    gyn, gyp = dzn * gain[:, :NOPE], dyp * gain[:, NOPE:]
    dot = jnp.sum(gyn * n, axis=-1, keepdims=True) + jnp.sum(gyp * p, axis=-1, keepdims=True)
    coef = dot * (r * r * r) * (1.0 / QK)
    d_gn = jnp.sum(dzn * n * r, axis=0, keepdims=True)
    d_gp = jnp.sum(dyp * p * r, axis=0, keepdims=True)
    return gyn * r - n * coef, gyp * r - p * coef, d_gn, d_gp


def _rope_key(last_ref, lane):
    return jnp.where(lane < ROPE, last_ref[...], 0.0)


def _heads_fwd_call(q, kv, proj, cs, sn, q_gain, k_gain):
    s = q.shape[0]
    t = min(HEADS_TILE, s)

    def body(q_ref, kv_ref, last_ref, cs_ref, sn_ref, qg_ref, kg_ref, qh_ref, kh_ref, vh_ref):
        lane = lax.broadcasted_iota(jnp.int32, (t, LANE), 1)
        cs_, sn_ = cs_ref[...], sn_ref[...]
        kp = _rope_key(last_ref, lane)
        for h in range(MLA_H):
            c0 = h * HEAD_COLS
            zn, zp = _head_fwd(q_ref[:, c0:c0 + NOPE], q_ref[:, c0 + NOPE:c0 + HEAD_COLS], qg_ref[...], cs_, sn_, lane)
            qh_ref[h, :, :NOPE] = zn.astype(BF16)
            qh_ref[h, :, NOPE:] = zp[:, :ROPE].astype(BF16)
            zn, zp = _head_fwd(kv_ref[:, c0:c0 + NOPE], kp, kg_ref[...], cs_, sn_, lane)
            kh_ref[h, :, :NOPE] = zn.astype(BF16)
            kh_ref[h, :, NOPE:] = zp[:, :ROPE].astype(BF16)
            vh_ref[h] = kv_ref[:, c0 + NOPE:c0 + HEAD_COLS].astype(BF16)

    rows = lambda i: (i, 0)
    whole = lambda i: (0, 0)
    heads = lambda i: (0, i, 0)
    wide = MLA_H * HEAD_COLS
    return pl.pallas_call(
        body, grid=(s // t,),
        in_specs=[pl.BlockSpec((t, wide), rows), pl.BlockSpec((t, wide), rows),
                  pl.BlockSpec((t, LANE), lambda i: (i, PROJ_LAST // LANE)),
                  pl.BlockSpec((t, LANE), rows), pl.BlockSpec((t, LANE), rows),
                  pl.BlockSpec((1, HEAD_COLS), whole), pl.BlockSpec((1, HEAD_COLS), whole)],
        out_specs=[pl.BlockSpec((MLA_H, t, QK), heads), pl.BlockSpec((MLA_H, t, QK), heads), pl.BlockSpec((MLA_H, t, VDIM), heads)],
        out_shape=[jax.ShapeDtypeStruct((MLA_H, s, QK), BF16), jax.ShapeDtypeStruct((MLA_H, s, QK), BF16),
                   jax.ShapeDtypeStruct((MLA_H, s, VDIM), BF16)],
        compiler_params=_cparams("arbitrary"), name="mla_heads_fwd",
    )(q, kv, proj, cs, sn, q_gain, k_gain)


def _heads_bwd_call(q, kv, proj, cs, sn, q_gain, k_gain, dqh, dkh, dvh):
    s = q.shape[0]
    t = min(HEADS_TILE, s)

    def body(q_ref, kv_ref, last_ref, cs_ref, sn_ref, qg_ref, kg_ref, dqh_ref, dkh_ref, dvh_ref,
             dq_ref, dkv_ref, dkr_ref, dqg_ref, dkg_ref):
        lane = lax.broadcasted_iota(jnp.int32, (t, LANE), 1)
        cs_, sn_ = cs_ref[...], sn_ref[...]
        kp = _rope_key(last_ref, lane)
        no_lanes = jnp.zeros((t, LANE - ROPE), F32)
        d_kp = jnp.zeros((t, LANE), F32)
        d_qg = [jnp.zeros((1, NOPE), F32), jnp.zeros((1, LANE), F32)]
        d_kg = [jnp.zeros((1, NOPE), F32), jnp.zeros((1, LANE), F32)]
        for h in range(MLA_H):
            c0 = h * HEAD_COLS
            dz = dqh_ref[h]
            dzp = jnp.concatenate([dz[:, NOPE:], no_lanes], axis=1)
            d_n, d_p, g_n, g_p = _head_bwd(q_ref[:, c0:c0 + NOPE], q_ref[:, c0 + NOPE:c0 + HEAD_COLS], qg_ref[...],
                                           cs_, sn_, lane, dz[:, :NOPE], dzp)
            dq_ref[:, c0:c0 + NOPE] = d_n.astype(dq_ref.dtype)
            dq_ref[:, c0 + NOPE:c0 + HEAD_COLS] = d_p.astype(dq_ref.dtype)
            d_qg = [d_qg[0] + g_n, d_qg[1] + g_p]
            dz = dkh_ref[h]
            dzp = jnp.concatenate([dz[:, NOPE:], no_lanes], axis=1)
            d_n, d_p, g_n, g_p = _head_bwd(kv_ref[:, c0:c0 + NOPE], kp, kg_ref[...], cs_, sn_, lane, dz[:, :NOPE], dzp)
            dkv_ref[:, c0:c0 + NOPE] = d_n.astype(dkv_ref.dtype)
            dkv_ref[:, c0 + NOPE:c0 + HEAD_COLS] = dvh_ref[h].astype(dkv_ref.dtype)
            d_kp = d_kp + d_p
            d_kg = [d_kg[0] + g_n, d_kg[1] + g_p]
        dkr_ref[...] = d_kp
        first = pl.program_id(0) == 0
        _acc_store(dqg_ref.at[:, pl.ds(0, NOPE)], d_qg[0], first)
        _acc_store(dqg_ref.at[:, pl.ds(NOPE, LANE)], d_qg[1], first)
        _acc_store(dkg_ref.at[:, pl.ds(0, NOPE)], d_kg[0], first)
        _acc_store(dkg_ref.at[:, pl.ds(NOPE, LANE)], d_kg[1], first)

    rows = lambda i: (i, 0)
    whole = lambda i: (0, 0)
    heads = lambda i: (0, i, 0)
    wide = MLA_H * HEAD_COLS
    return pl.pallas_call(
        body, grid=(s // t,),
        in_specs=[pl.BlockSpec((t, wide), rows), pl.BlockSpec((t, wide), rows),
                  pl.BlockSpec((t, LANE), lambda i: (i, PROJ_LAST // LANE)),
                  pl.BlockSpec((t, LANE), rows), pl.BlockSpec((t, LANE), rows),
                  pl.BlockSpec((1, HEAD_COLS), whole), pl.BlockSpec((1, HEAD_COLS), whole),
                  pl.BlockSpec((MLA_H, t, QK), heads), pl.BlockSpec((MLA_H, t, QK), heads), pl.BlockSpec((MLA_H, t, VDIM), heads)],
        out_specs=[pl.BlockSpec((t, wide), rows), pl.BlockSpec((t, wide), rows), pl.BlockSpec((t, LANE), rows),
                   pl.BlockSpec((1, HEAD_COLS), whole), pl.BlockSpec((1, HEAD_COLS), whole)],
        out_shape=[jax.ShapeDtypeStruct((s, wide), BF16), jax.ShapeDtypeStruct((s, wide), BF16), jax.ShapeDtypeStruct((s, LANE), F32),
                   jax.ShapeDtypeStruct((1, HEAD_COLS), F32), jax.ShapeDtypeStruct((1, HEAD_COLS), F32)],
        compiler_params=_cparams("arbitrary"), name="mla_heads_bwd",
    )(q, kv, proj, cs, sn, q_gain, k_gain, dqh, dkh, dvh)


CONV_TC = 512
HALO = 8


def _conv_tiles(s):
    return min(512, s)


def _conv_fwd_call(x, col0, w, b):
    s = x.shape[0]
    ts = _conv_tiles(s)
    hb = ts // HALO
    c0 = col0 // CONV_TC
    assert col0 % CONV_TC == 0

    def body(x_ref, prev_ref, w_ref, b_ref, y_ref, buf):
        si = pl.program_id(1)
        buf[0:HALO, :] = jnp.where(si > 0, prev_ref[...], 0.0)
        buf[HALO:, :] = x_ref[...]
        acc = jnp.broadcast_to(b_ref[...], (ts, CONV_TC))
        for k in range(CONV_K):
            acc = acc + w_ref[k:k + 1, :] * buf[pl.ds(HALO - (CONV_K - 1) + k, ts), :]
        y_ref[...] = acc * jax.nn.sigmoid(acc)

    return pl.pallas_call(
        body, grid=(CONV_DIM // CONV_TC, s // ts),
        in_specs=[pl.BlockSpec((ts, CONV_TC), lambda ci, si: (si, ci + c0)),
                  pl.BlockSpec((HALO, CONV_TC), lambda ci, si: (jnp.maximum(si * hb - 1, 0), ci + c0)),
                  pl.BlockSpec((CONV_K, CONV_TC), lambda ci, si: (0, ci)),
                  pl.BlockSpec((1, CONV_TC), lambda ci, si: (0, ci))],
        out_specs=pl.BlockSpec((ts, CONV_TC), lambda ci, si: (si, ci)),
        out_shape=jax.ShapeDtypeStruct((s, CONV_DIM), F32),
        scratch_shapes=[pltpu.VMEM((ts + HALO, CONV_TC), F32)],
        compiler_params=_cparams("arbitrary", "arbitrary"), name="conv_fwd",
    )(x, x, w, b)


def _conv_bwd_call(x, col0, w, b, dy):
    s = x.shape[0]
    ts = _conv_tiles(s)
    hb = ts // HALO
    ns = s // ts
    last_halo = s // HALO - 1
    c0 = col0 // CONV_TC

    def body(x_ref, prev_ref, next_ref, dy_ref, dyn_ref, w_ref, b_ref, dx_ref, dw_ref, db_ref, xbuf, dbuf):
        si = pl.program_id(1)
        xbuf[0:HALO, :] = jnp.where(si > 0, prev_ref[...], 0.0)
        xbuf[HALO:HALO + ts, :] = x_ref[...]
        xbuf[HALO + ts:, :] = next_ref[...]
        pre = jnp.broadcast_to(b_ref[...], (ts + HALO, CONV_TC))
        for k in range(CONV_K):
            pre = pre + w_ref[k:k + 1, :] * xbuf[pl.ds(HALO - (CONV_K - 1) + k, ts + HALO), :]
        sg = jax.nn.sigmoid(pre)
        dsilu = sg * (1.0 + pre * (1.0 - sg))
        dbuf[0:ts, :] = dy_ref[...] * dsilu[0:ts]
        dbuf[ts:, :] = jnp.where(si < ns - 1, dyn_ref[...] * dsilu[ts:], 0.0)
        dx = jnp.zeros((ts, CONV_TC), F32)
        for k in range(CONV_K):
            dx = dx + w_ref[k:k + 1, :] * dbuf[pl.ds(CONV_K - 1 - k, ts), :]
        dx_ref[...] = dx.astype(dx_ref.dtype)
        dpre = dbuf[0:ts, :]
        first = si == 0
        _acc_store(db_ref, jnp.sum(dpre, axis=0, keepdims=True), first)
        for k in range(CONV_K):
            dw_k = jnp.sum(dpre * xbuf[pl.ds(HALO - (CONV_K - 1) + k, ts), :], axis=0, keepdims=True)
            _acc_store(dw_ref.at[pl.ds(k, 1), :], dw_k, first)

    main = lambda ci, si: (si, ci)
    x_main = lambda ci, si: (si, ci + c0)
    x_prev = lambda ci, si: (jnp.maximum(si * hb - 1, 0), ci + c0)
    x_next = lambda ci, si: (jnp.minimum(si * hb + hb, last_halo), ci + c0)
    return pl.pallas_call(
        body, grid=(CONV_DIM // CONV_TC, ns),
        in_specs=[pl.BlockSpec((ts, CONV_TC), x_main), pl.BlockSpec((HALO, CONV_TC), x_prev), pl.BlockSpec((HALO, CONV_TC), x_next),
                  pl.BlockSpec((ts, CONV_TC), main),
                  pl.BlockSpec((HALO, CONV_TC), lambda ci, si: (jnp.minimum(si * hb + hb, last_halo), ci)),
                  pl.BlockSpec((CONV_K, CONV_TC), lambda ci, si: (0, ci)),
                  pl.BlockSpec((1, CONV_TC), lambda ci, si: (0, ci))],
        out_specs=[pl.BlockSpec((ts, CONV_TC), main),
                   pl.BlockSpec((CONV_K, CONV_TC), lambda ci, si: (0, ci)),
                   pl.BlockSpec((1, CONV_TC), lambda ci, si: (0, ci))],
        out_shape=[jax.ShapeDtypeStruct((s, CONV_DIM), BF16), jax.ShapeDtypeStruct((CONV_K, CONV_DIM), F32),
                   jax.ShapeDtypeStruct((1, CONV_DIM), F32)],
        scratch_shapes=[pltpu.VMEM((ts + 2 * HALO, CONV_TC), F32), pltpu.VMEM((ts + HALO, CONV_TC), F32)],
        compiler_params=_cparams("arbitrary", "arbitrary"), name="conv_bwd",
    )(x, x, x, dy, dy, w, b)


GW = SSD_HPG * SSD_P
B_COL = SSD_DI
C_COL = SSD_DI + SSD_G * SSD_N


def _ones_where(mask):
    return jnp.where(mask, 1.0, 0.0).astype(BF16)


def _split3(v):
    hi = v.astype(BF16)
    r1 = v - hi.astype(F32)
    mid = r1.astype(BF16)
    lo = (r1 - mid.astype(F32)).astype(BF16)
    return hi, mid, lo


def _dot_sel_r(v, sel):
    out = None
    for part in _split3(v):
        t = jnp.dot(part, sel, preferred_element_type=F32)
        out = t if out is None else out + t
    return out


def _dot_sel_l(sel, v):
    out = None
    for part in _split3(v):
        t = jnp.dot(sel, part, preferred_element_type=F32)
        out = t if out is None else out + t
    return out


def _ssd_consts():
    r = lax.broadcasted_iota(jnp.int32, (SSD_L, SSD_L), 0)
    c = lax.broadcasted_iota(jnp.int32, (SSD_L, SSD_L), 1)
    tril = r >= c
    triu = c >= r
    shift = SSD_P.bit_length() - 1
    eh = lax.broadcasted_iota(jnp.int32, (SSD_H, SSD_DI), 0)
    ej = lax.broadcasted_iota(jnp.int32, (SSD_H, SSD_DI), 1)
    expand = _ones_where(lax.shift_right_logical(ej, shift) == eh)
    rj = lax.broadcasted_iota(jnp.int32, (SSD_DI, SSD_H), 0)
    rh = lax.broadcasted_iota(jnp.int32, (SSD_DI, SSD_H), 1)
    reduce_ = _ones_where(lax.shift_right_logical(rj, shift) == rh)
    lane = lax.broadcasted_iota(jnp.int32, (SSD_L, LANE), 1)
    return tril, triu, expand, reduce_, lane < SSD_P


def _ssd_decays(dt, dt_t, a, a_t, tril, triu, expand):
    dta = dt * a
    acum = _dot_sel_l(_ones_where(tril), dta)
    acum_t = _dot_sel_r(dt_t * a_t, _ones_where(triu))
    dta_e = _dot_sel_r(dta, expand)
    acum_e = _dot_sel_r(acum, expand)
    last_e = jnp.sum(dta_e, axis=0, keepdims=True)
    return acum, acum_t, acum_e, last_e


def _head_decay(acum, acum_t, h, tril):
    seg = acum[:, h:h + 1] - acum_t[h:h + 1, :]
    return jnp.exp(jnp.where(tril, seg, NEG))


def _ssd_fwd_call(xbc, dt, a):
    s = xbc.shape[0]
    nc = s // SSD_L
    dt_t = dt.T
    a_t = a.T

    def body(xbc_ref, dt_ref, dtt_ref, a_ref, at_ref, y_ref, st_ref, s_sc):
        ci = pl.program_id(0)

        @pl.when(ci == 0)
        def _():
            s_sc[...] = jnp.zeros_like(s_sc)

        st_ref[0] = s_sc[...]
        tril, triu, expand, _, low_half = _ssd_consts()
        acum, acum_t, acum_e, last_e = _ssd_decays(dt_ref[...], dtt_ref[...], a_ref[...], at_ref[...], tril, triu, expand)
        dt_e = _dot_sel_r(dt_ref[...], expand)
        xdt = xbc_ref[:, :SSD_DI] * dt_e
        xdt_b = xdt.astype(BF16)
        xw_b = (xdt * jnp.exp(last_e - acum_e)).astype(BF16)
        ea_e = jnp.exp(acum_e)
        el_e = jnp.exp(last_e)
        for g in range(SSD_G):
            gs = slice(g * GW, (g + 1) * GW)
            bg = xbc_ref[:, B_COL + g * SSD_N:B_COL + (g + 1) * SSD_N]
            cg_b = xbc_ref[:, C_COL + g * SSD_N:C_COL + (g + 1) * SSD_N].astype(BF16)
            bg_b = bg.astype(BF16)
            cb = _nt(cg_b, bg_b)
            st = s_sc[:, gs]
            y_off = jnp.dot(cg_b, st.astype(BF16), preferred_element_type=F32) * ea_e[:, gs]
            for pr in range(SSD_HPG // 2):
                ls = slice(g * GW + pr * LANE, g * GW + (pr + 1) * LANE)
                xp = xdt_b[:, ls]
                yd = []
                for half in range(2):
                    h = g * SSD_HPG + pr * 2 + half
                    m = (cb * _head_decay(acum, acum_t, h, tril)).astype(BF16)
                    yd.append(jnp.dot(m, xp, preferred_element_type=F32))
                y_ref[:, ls] = jnp.where(low_half, yd[0], yd[1]) + y_off[:, pr * LANE:(pr + 1) * LANE]
            s_sc[:, gs] = st * el_e[:, gs] + jnp.dot(bg.T.astype(BF16), xw_b[:, gs], preferred_element_type=F32)

    row = lambda i: (i, 0)
    return pl.pallas_call(
        body, grid=(nc,),
        in_specs=[pl.BlockSpec((SSD_L, CONV_DIM), row), pl.BlockSpec((SSD_L, SSD_H), row),
                  pl.BlockSpec((SSD_H, SSD_L), lambda i: (0, i)), pl.BlockSpec((1, SSD_H), lambda i: (0, 0)),
                  pl.BlockSpec((SSD_H, 1), lambda i: (0, 0))],
        out_specs=[pl.BlockSpec((SSD_L, SSD_DI), row), pl.BlockSpec((1, SSD_N, SSD_DI), lambda i: (i, 0, 0))],
        out_shape=[jax.ShapeDtypeStruct((s, SSD_DI), F32), jax.ShapeDtypeStruct((nc, SSD_N, SSD_DI), F32)],
        scratch_shapes=[pltpu.VMEM((SSD_N, SSD_DI), F32)],
        compiler_params=_cparams("arbitrary"), name="ssd_fwd",
    )(xbc, dt, dt_t, a, a_t)


def _ssd_bwd_call(xbc, dt, a, states, dy, dx_extra):
    s = xbc.shape[0]
    nc = s // SSD_L
    dt_t = dt.T
    a_t = a.T

    def body(xbc_ref, dt_ref, dtt_ref, a_ref, at_ref, st_ref, dy_ref, dxe_ref,
             dxbc_ref, ddt_ref, da_ref, ds_sc, yf_sc, dxd_sc, dxw_sc):
        i = pl.program_id(0)

        @pl.when(i == 0)
        def _():
            ds_sc[...] = jnp.zeros_like(ds_sc)

        tril, triu, expand, reduce_, low_half = _ssd_consts()
        dt = dt_ref[...]
        a_row = a_ref[...]
        acum, acum_t, acum_e, last_e = _ssd_decays(dt, dtt_ref[...], a_row, at_ref[...], tril, triu, expand)
        dt_e = _dot_sel_r(dt, expand)
        x = xbc_ref[:, :SSD_DI]
        xdt = x * dt_e
        xdt_b = xdt.astype(BF16)
        w_e = jnp.exp(last_e - acum_e)
        xw_b = (xdt * w_e).astype(BF16)
        ea_e = jnp.exp(acum_e)
        el_e = jnp.exp(last_e)
        dy = dy_ref[...]
        dy_b = dy.astype(BF16)
        s_prev = st_ref[0]
        ds_new = ds_sc[...]
        ds_new_b = ds_new.astype(BF16)
        triu_b = _ones_where(triu)
        strict_tril = jnp.logical_not(triu)
        head_ids = lax.broadcasted_iota(jnp.int32, (1, SSD_H), 1)
        d_dta_diag = jnp.zeros((SSD_L, SSD_H), F32)
        for g in range(SSD_G):
            gs = slice(g * GW, (g + 1) * GW)
            bs_ = slice(B_COL + g * SSD_N, B_COL + (g + 1) * SSD_N)
            cs_ = slice(C_COL + g * SSD_N, C_COL + (g + 1) * SSD_N)
            bg = xbc_ref[:, bs_]
            cg = xbc_ref[:, cs_]
            bg_b, cg_b = bg.astype(BF16), cg.astype(BF16)
            st_b = s_prev[:, gs].astype(BF16)
            y_off = jnp.dot(cg_b, st_b, preferred_element_type=F32) * ea_e[:, gs]
            yf_sc[:, gs] = y_off
            dz_b = (dy[:, gs] * ea_e[:, gs]).astype(BF16)
            d_c = _nt(dz_b, st_b)
            ds_prev = ds_new[:, gs] * el_e[:, gs] + jnp.dot(cg.T.astype(BF16), dz_b, preferred_element_type=F32)
            dxw_sc[:, gs] = jnp.dot(bg_b, ds_new_b[:, gs], preferred_element_type=F32)
            d_b = _nt(xw_b[:, gs], ds_new_b[:, gs])
            cb = _nt(cg_b, bg_b)
            d_g = jnp.zeros((SSD_L, SSD_L), F32)
            for pr in range(SSD_HPG // 2):
                ls = slice(g * GW + pr * LANE, g * GW + (pr + 1) * LANE)
                xp = xdt_b[:, ls]
                dyp = dy[:, ls]
                dyp_b = dy_b[:, ls]
                dxd = []
                for half in range(2):
                    h = g * SSD_HPG + pr * 2 + half
                    dec = _head_decay(acum, acum_t, h, tril)
                    m = cb * dec
                    dxd.append(jnp.dot(m.T.astype(BF16), dyp_b, preferred_element_type=F32))
                    mine = low_half if half == 0 else jnp.logical_not(low_half)
                    d_m = _nt(jnp.where(mine, dyp, 0.0).astype(BF16), xp)
                    d_g = d_g + d_m * dec
                    below = jnp.dot(triu_b, (d_m * m).astype(BF16), preferred_element_type=F32)
                    col = jnp.sum(jnp.where(strict_tril, below, 0.0), axis=1, keepdims=True)
                    d_dta_diag = d_dta_diag + col * jnp.where(head_ids == h, 1.0, 0.0)
                dxd_sc[:, ls] = jnp.where(low_half, dxd[0], dxd[1])
            d_g_b = d_g.astype(BF16)
            dxbc_ref[:, cs_] = d_c + jnp.dot(d_g_b, bg_b, preferred_element_type=F32)
            dxbc_ref[:, bs_] = d_b + jnp.dot(d_g.T.astype(BF16), cg_b, preferred_element_type=F32)
            ds_sc[:, gs] = ds_prev
        dxw = dxw_sc[...]
        dxd = dxd_sc[...]
        dw_e = xdt * dxw * w_e
        d_acum_e = dy * yf_sc[...] - dw_e
        d_last_e = jnp.sum(ds_new * s_prev, axis=0, keepdims=True) * el_e + jnp.sum(dw_e, axis=0, keepdims=True)
        suffix = _dot_sel_l(triu_b, d_acum_e)
        d_dta = _dot_sel_r(suffix + d_last_e, reduce_) + d_dta_diag
        dxdt = dxd + dxw * w_e
        dxbc_ref[:, :SSD_DI] = dxdt * dt_e + dxe_ref[...]
        ddt_ref[...] = d_dta * a_row + _dot_sel_r(dxdt * x, reduce_)
        _acc_store(da_ref, jnp.sum(d_dta * dt, axis=0, keepdims=True), i == 0)

    rev = lambda i: (nc - 1 - i, 0)
    return pl.pallas_call(
        body, grid=(nc,),
        in_specs=[pl.BlockSpec((SSD_L, CONV_DIM), rev), pl.BlockSpec((SSD_L, SSD_H), rev),
                  pl.BlockSpec((SSD_H, SSD_L), lambda i: (0, nc - 1 - i)), pl.BlockSpec((1, SSD_H), lambda i: (0, 0)),
                  pl.BlockSpec((SSD_H, 1), lambda i: (0, 0)),
                  pl.BlockSpec((1, SSD_N, SSD_DI), lambda i: (nc - 1 - i, 0, 0)),
                  pl.BlockSpec((SSD_L, SSD_DI), rev), pl.BlockSpec((SSD_L, SSD_DI), rev)],
        out_specs=[pl.BlockSpec((SSD_L, CONV_DIM), rev), pl.BlockSpec((SSD_L, SSD_H), rev),
                   pl.BlockSpec((1, SSD_H), lambda i: (0, 0))],
        out_shape=[jax.ShapeDtypeStruct((s, CONV_DIM), F32), jax.ShapeDtypeStruct((s, SSD_H), F32),
                   jax.ShapeDtypeStruct((1, SSD_H), F32)],
        scratch_shapes=[pltpu.VMEM((SSD_N, SSD_DI), F32), pltpu.VMEM((SSD_L, SSD_DI), F32),
                        pltpu.VMEM((SSD_L, SSD_DI), F32), pltpu.VMEM((SSD_L, SSD_DI), F32)],
        compiler_params=_cparams("arbitrary"), name="ssd_bwd",
    )(xbc, dt, dt_t, a, a_t, states, dy, dx_extra)


HBM_SPEC = pl.BlockSpec(memory_space=pltpu.HBM)
N_PEERS = N_DEV - 1


def _flip(v, f):
    return 1 - v if f else v


def _all_gather(shard):
    rows, c = shard.shape

    def body(x_ref, out_ref, send_sems, recv_sems, local_sem):
        x, y, cc = lax.axis_index("x"), lax.axis_index("y"), lax.axis_index("c")
        me, sibling = (x, y, cc), (x, y, 1 - cc)
        chips = [(1 - x, y), (x, 1 - y), (1 - x, 1 - y)]

        def slot(px, py, pc):
            return out_ref.at[4 * px + 2 * py + pc]

        def copy(k, block, to, src=None):
            return pltpu.make_async_remote_copy(
                src_ref=slot(*block) if src is None else src, dst_ref=slot(*block),
                send_sem=send_sems.at[k], recv_sem=recv_sems.at[k],
                device_id=to, device_id_type=pl.DeviceIdType.MESH)

        mine = pltpu.make_async_copy(x_ref, slot(*me), local_sem)
        mine.start()
        first = [copy(0, me, sibling, src=x_ref)]
        first += [copy(1 + j, me, (*chip, cc), src=x_ref) for j, chip in enumerate(chips)]
        for cp in first:
            cp.start()
        passed = [copy(4 + j, (*chip, cc), sibling) for j, chip in enumerate(chips)]
        for j, chip in enumerate(chips):
            copy(1 + j, (*chip, cc), me).wait_recv()
            passed[j].start()
        copy(0, sibling, me).wait_recv()
        for j, chip in enumerate(chips):
            copy(4 + j, (*chip, 1 - cc), me).wait_recv()
        for cp in first + passed:
            cp.wait_send()
        mine.wait()

    return pl.pallas_call(
        body, out_shape=jax.ShapeDtypeStruct((N_DEV, rows, c), shard.dtype),
        in_specs=[HBM_SPEC], out_specs=HBM_SPEC,
        scratch_shapes=[pltpu.SemaphoreType.DMA((N_PEERS,)), pltpu.SemaphoreType.DMA((N_PEERS,)), pltpu.SemaphoreType.DMA(())],
        name="all_gather",
    )(shard)


def _peer_copies(src_ref, out_ref, sems, gather, phase):
    send_sems, recv_sems, local_sem = sems
    x, y, cc = lax.axis_index("x"), lax.axis_index("y"), lax.axis_index("c")
    me = 4 * x + 2 * y + cc
    mine = pltpu.make_async_copy(src_ref if gather else src_ref.at[me], out_ref.at[me], local_sem)
    copies = []
    for k in range(1, N_DEV):
        px, py, pc = _flip(x, k & 4), _flip(y, k & 2), _flip(cc, k & 1)
        peer = 4 * px + 2 * py + pc
        src = src_ref if gather else src_ref.at[peer]
        copies.append((
            pltpu.make_async_remote_copy(
                src_ref=src, dst_ref=out_ref.at[me], send_sem=send_sems.at[k - 1], recv_sem=recv_sems.at[k - 1],
                device_id=(px, py, pc), device_id_type=pl.DeviceIdType.MESH),
            pltpu.make_async_remote_copy(
                src_ref=src, dst_ref=out_ref.at[peer], send_sem=send_sems.at[k - 1], recv_sem=recv_sems.at[k - 1],
                device_id=(px, py, pc), device_id_type=pl.DeviceIdType.MESH)))
    if phase == "start":
        mine.start()
        for send, _ in copies:
            send.start()
    else:
        for _, landed in copies:
            landed.wait_recv()
        for send, _ in copies:
            send.wait_send()
        mine.wait()


def _comm_scratch():
    return [pltpu.SemaphoreType.DMA((N_PEERS,)), pltpu.SemaphoreType.DMA((N_PEERS,)), pltpu.SemaphoreType.DMA(())]


def _gather_rider(shard):
    return dict(src=shard, out=jax.ShapeDtypeStruct((N_DEV,) + shard.shape, shard.dtype), gather=True)


def _exchange_rider(blocks):
    return dict(src=blocks, out=jax.ShapeDtypeStruct(blocks.shape, blocks.dtype), gather=False)


def _exchange_blocks(blocks):
    def body(g_ref, out_ref, *sems):
        _peer_copies(g_ref, out_ref, sems, False, "start")
        _peer_copies(g_ref, out_ref, sems, False, "finish")

    return pl.pallas_call(
        body, out_shape=jax.ShapeDtypeStruct(blocks.shape, blocks.dtype),
        in_specs=[HBM_SPEC], out_specs=HBM_SPEC, scratch_shapes=_comm_scratch(), name="exchange_blocks",
    )(blocks)


BIG = [
    ("ffn1_w13", (D_MODEL, 2 * D_FF), 1), ("ffn1_w2", (D_FF, D_MODEL), 0), ("w_in", (D_MODEL, D_IN), 1),
    ("w_ssd_out", (SSD_DI, D_MODEL), 0), ("w_uq", (Q_LORA, MLA_H * QK), 1), ("w_ukv", (KV_LORA, MLA_H * (NOPE + VDIM)), 1),
    ("w_mla_out", (MLA_H * VDIM, D_MODEL), 0), ("w_o", (D_MODEL, D_MODEL), 0),
    ("ffn2_w13", (D_MODEL, 2 * D_FF), 1), ("ffn2_w2", (D_FF, D_MODEL), 0),
]
SMALL = [
    ("ln_ffn1", D_MODEL), ("ln_mix", D_MODEL), ("conv_b", CONV_DIM), ("dt_bias", SSD_H), ("a_log", SSD_H), ("d_skip", SSD_H),
    ("ssd_norm", SSD_DI), ("q_lora_norm", Q_LORA), ("kv_lora_norm", KV_LORA), ("q_norm", QK), ("k_norm", QK), ("ln_ffn2", D_MODEL),
]


def _shard_shape(full, axis):
    k, n = full
    return (k // N_DEV, n) if axis == 0 else (k, n // N_DEV)


def _shard_rows(full):
    return full[0] * full[1] // N_DEV // PACK_COLS


LAYER_ROWS = sum(_shard_rows(f) for _, f, _ in BIG)
LAYER_ROWS_PAD = -(-LAYER_ROWS // 256) * 256


def _pack_shards(shards):
    parts = [(shards[name] if axis == 0 else shards[name].T).reshape(-1, PACK_COLS) for name, _, axis in BIG]
    pad = LAYER_ROWS_PAD - LAYER_ROWS
    if pad:
        parts.append(jnp.zeros((pad, PACK_COLS), parts[0].dtype))
    return jnp.concatenate(parts, axis=0)


def _unpack_shards(packed):
    out, r = {}, 0
    for name, full, axis in BIG:
        n = _shard_rows(full)
        k, c = _shard_shape(full, axis)
        blk = packed[r:r + n]
        out[name] = blk.reshape(k, c) if axis == 0 else blk.reshape(c, k).T
        r += n
    return out


def _working_shape(full, axis):
    return full if axis == 0 else full[::-1]


def _unpack_gathered(gathered):
    out, r = {}, 0
    for name, full, axis in BIG:
        n = _shard_rows(full)
        out[name] = gathered[:, r:r + n].reshape(_working_shape(full, axis))
        r += n
    return out


def _pack_full_grads(grads):
    parts = [grads[name].reshape(N_DEV, -1, PACK_COLS) for name, _, _ in BIG]
    pad = LAYER_ROWS_PAD - LAYER_ROWS
    if pad:
        parts.append(jnp.zeros((N_DEV, pad, PACK_COLS), parts[0].dtype))
    return jnp.concatenate(parts, axis=1)


SMALL_COLS = sum(n for _, n in SMALL) + CONV_K * CONV_DIM
SMALL_ROWS = -(-(DEPTH * SMALL_COLS) // (8 * PACK_COLS)) * 8


def _pack_small(vals, conv_w):
    flat = jnp.concatenate([vals[name] for name, _ in SMALL] + [conv_w.reshape(DEPTH, -1)], axis=1).reshape(-1)
    flat = jnp.concatenate([flat, jnp.zeros((SMALL_ROWS * PACK_COLS - flat.shape[0],), F32)])
    return flat.reshape(SMALL_ROWS, PACK_COLS)


def _unpack_small(packed):
    flat = packed.reshape(-1)[:DEPTH * SMALL_COLS].reshape(DEPTH, SMALL_COLS)
    out, c = {}, 0
    for name, n in SMALL:
        out[name] = flat[:, c:c + n]
        c += n
    return out, flat[:, c:].reshape(DEPTH, CONV_K, CONV_DIM)


_IN_OFFS = [sum(IN_SPLIT[:i]) for i in range(len(IN_SPLIT) + 1)]


def _arrange_w_in(w_t):
    z, xbc, dt, cq, ckv, kr, gates = [w_t[_IN_OFFS[i]:_IN_OFFS[i + 1]] for i in range(len(IN_SPLIT))]
    pad = jnp.zeros((LANE - ROPE - SSD_H, w_t.shape[1]), w_t.dtype)
    return jnp.concatenate([z, gates, xbc, cq, ckv, kr, dt, pad], axis=0)


def _restore_w_in(g):
    z, gates, xbc = g[PROJ_Z:PROJ_GATES], g[PROJ_GATES:PROJ_XBC], g[PROJ_XBC:PROJ_CQ]
    cq, ckv = g[PROJ_CQ:PROJ_CKV], g[PROJ_CKV:PROJ_LAST]
    kr, dt = g[PROJ_LAST:PROJ_LAST + ROPE], g[PROJ_LAST + ROPE:PROJ_LAST + ROPE + SSD_H]
    return jnp.concatenate([z, xbc, dt, cq, ckv, kr, gates], axis=0)


def _pad_heads(w_t):
    k = w_t.shape[1]
    return jnp.pad(w_t.reshape(MLA_H, QK, k), ((0, 0), (0, HEAD_COLS - QK), (0, 0))).reshape(MLA_H * HEAD_COLS, k)


def _unpad_heads(g):
    k = g.shape[1]
    return g.reshape(MLA_H, HEAD_COLS, k)[:, :QK].reshape(MLA_H * QK, k)


def _row(v):
    return v.reshape(1, -1)


def _head_gain(g):
    return jnp.pad(g, (0, HEAD_COLS - QK)).reshape(1, HEAD_COLS)


def _ffn_fwd(h, ln, w13_t, w2, name):
    n = _row_fwd(_f_rmsnorm, [h], [_row(ln)], [BF16], name + "_fwd")[0]
    gu = _mm(n, w13_t, tb=True)
    act = _row_fwd(_f_swiglu, [gu], [], [BF16], "swiglu_fwd")[0]
    return _mm(act, w2, alpha=0.5, res=h), (h, n, gu, act)


def _ffn_bwd(dh_out, saved, ln, w13_t, w2, name):
    h, n, gu, act = saved
    d_act = _mm(dh_out, w2, tb=True, out_dtype=BF16, alpha=0.5)
    d_w2 = _mm(act, dh_out, ta=True, out_dtype=BF16, alpha=0.5)
    d_gu = _row_bwd(_f_swiglu, [gu], [], [d_act], [BF16], "swiglu_bwd", bwd=_b_swiglu)[0][0]
    d_n = _mm(d_gu, w13_t, out_dtype=BF16)
    d_w13_t = _mm(d_gu, n, ta=True, out_dtype=BF16)
    (dh,), (d_ln,) = _row_bwd(_f_rmsnorm, [h], [_row(ln)], [d_n], [F32], name + "_bwd", add={0: dh_out})
    return dh, d_w13_t, d_w2, d_ln[0]


def _mixer_fwd(h, big, small, conv_w, cs, sn, rider=None):
    s = h.shape[0]
    u = _row_fwd(_f_rmsnorm, [h], [_row(small["ln_mix"])], [BF16], "ln_mix_fwd")[0]
    proj = _mm(u, big["w_in"], tb=True)
    xbc = _conv_fwd_call(proj, PROJ_XBC, conv_w, _row(small["conv_b"]))
    dt_in = proj[:, PROJ_LAST + ROPE:PROJ_LAST + ROPE + SSD_H] + small["dt_bias"][None, :]
    dt = jax.nn.softplus(dt_in)
    a = -jnp.exp(small["a_log"])[None, :]
    y_scan, states = _ssd_fwd_call(xbc, dt, a)
    dsk = _row(jnp.repeat(small["d_skip"], SSD_P))
    gn_in = [y_scan, _win(xbc, 0, SSD_DI), _win(proj, PROJ_Z, SSD_DI)]
    yn = _row_fwd(_f_gated_norm, gn_in, [dsk, _row(small["ssd_norm"])], [BF16], "gated_norm_fwd")[0]
    y_ssd = _mm(yn, big["w_ssd_out"])
    qn = _row_fwd(_f_rmsnorm, [_win(proj, PROJ_CQ, Q_LORA)], [_row(small["q_lora_norm"])], [BF16], "q_lora_norm_fwd")[0]
    kvn = _row_fwd(_f_rmsnorm, [_win(proj, PROJ_CKV, KV_LORA)], [_row(small["kv_lora_norm"])], [BF16], "kv_lora_norm_fwd")[0]
    q = _mm(qn, big["w_uq"], tb=True)
    kv = _mm(kvn, big["w_ukv"], tb=True)
    qh, kh, vh = _heads_fwd_call(q, kv, proj, cs, sn, _head_gain(small["q_norm"]), _head_gain(small["k_norm"]))
    o, lse, *carried = _attn_fwd_call(qh, kh, vh, rider)
    o_rows = jnp.transpose(o, (1, 0, 2)).reshape(s, MLA_H * VDIM)
    y_mla = _mm(o_rows, big["w_mla_out"])
    mg = _row_fwd(_f_merge, [_win(proj, PROJ_GATES, 2 * D_MODEL), y_ssd, y_mla], [], [BF16], "merge_fwd")[0]
    out = _mm(mg, big["w_o"], res=h)
    saved = (h, u, proj, xbc, dt_in, dt, a, y_scan, states, dsk, yn, y_ssd, qn, kvn, q, kv, qh, kh, vh, o, lse, o_rows, y_mla, mg)
    return out, saved, (carried[0] if carried else None)


def _mixer_bwd(dh_out, saved, big, small, conv_w, cs, sn, rider=None):
    (h, u, proj, xbc, dt_in, dt, a, y_scan, states, dsk, yn, y_ssd, qn, kvn, q, kv, qh, kh, vh, o, lse, o_rows, y_mla, mg) = saved
    s = h.shape[0]
    d_big, d_small = {}, {}
    d_mg = _mm(dh_out, big["w_o"], tb=True, out_dtype=BF16)
    d_big["w_o"] = _mm(mg, dh_out, ta=True, out_dtype=BF16)
    merge_in = [_win(proj, PROJ_GATES, 2 * D_MODEL), y_ssd, y_mla]
    (d_gates, d_y_ssd, d_y_mla), _ = _row_bwd(_f_merge, merge_in, [], [d_mg], [BF16, BF16, BF16], "merge_bwd", bwd=_b_merge)
    d_o_rows = _mm(d_y_mla, big["w_mla_out"], tb=True, out_dtype=BF16)
    d_big["w_mla_out"] = _mm(o_rows, d_y_mla, ta=True, out_dtype=BF16)
    d_o = jnp.transpose(d_o_rows.reshape(s, MLA_H, VDIM), (1, 0, 2))
    delta = _attn_delta_call(o, d_o)
    *d_heads, carried = list(_attn_bwd_call(qh, kh, vh, d_o, lse.reshape(MLA_H, 1, s), delta.reshape(MLA_H, 1, s), rider)) + ([None] if rider is None else [])
    d_q, d_kv, d_kr, d_qg, d_kg = _heads_bwd_call(
        q, kv, proj, cs, sn, _head_gain(small["q_norm"]), _head_gain(small["k_norm"]), *d_heads)
    d_small["q_norm"], d_small["k_norm"] = d_qg[0, :QK], d_kg[0, :QK]
    d_qn = _mm(d_q, big["w_uq"], out_dtype=BF16)
    d_big["w_uq"] = _mm(d_q, qn, ta=True, out_dtype=BF16)
    d_kvn = _mm(d_kv, big["w_ukv"], out_dtype=BF16)
    d_big["w_ukv"] = _mm(d_kv, kvn, ta=True, out_dtype=BF16)
    (d_cq,), (d_g,) = _row_bwd(_f_rmsnorm, [_win(proj, PROJ_CQ, Q_LORA)], [_row(small["q_lora_norm"])], [d_qn], [BF16], "q_lora_norm_bwd")
    d_small["q_lora_norm"] = d_g[0]
    (d_ckv,), (d_g,) = _row_bwd(_f_rmsnorm, [_win(proj, PROJ_CKV, KV_LORA)], [_row(small["kv_lora_norm"])], [d_kvn], [BF16], "kv_lora_norm_bwd")
    d_small["kv_lora_norm"] = d_g[0]
    d_yn = _mm(d_y_ssd, big["w_ssd_out"], tb=True, out_dtype=BF16)
    d_big["w_ssd_out"] = _mm(yn, d_y_ssd, ta=True, out_dtype=BF16)
    gn_in = [y_scan, _win(xbc, 0, SSD_DI), _win(proj, PROJ_Z, SSD_DI)]
    (d_y_scan, d_xs, d_z), (d_dsk, d_g) = _row_bwd(
        _f_gated_norm, gn_in, [dsk, _row(small["ssd_norm"])], [d_yn], [F32, F32, BF16], "gated_norm_bwd")
    d_small["ssd_norm"] = d_g[0]
    d_small["d_skip"] = jnp.sum(d_dsk.reshape(SSD_H, SSD_P), axis=1)
    d_xbc_act, d_dt, d_a = _ssd_bwd_call(xbc, dt, a, states, d_y_scan, d_xs)
    d_xbc, d_conv_w, d_conv_b = _conv_bwd_call(proj, PROJ_XBC, conv_w, _row(small["conv_b"]), d_xbc_act)
    d_small["conv_b"] = d_conv_b[0]
    d_dt_in = d_dt * jax.nn.sigmoid(dt_in)
    d_small["dt_bias"] = jnp.sum(d_dt_in, axis=0)
    d_small["a_log"] = d_a[0] * a[0]
    d_last = (d_kr + jnp.pad(d_dt_in, ((0, 0), (ROPE, LANE - ROPE - SSD_H)))).astype(BF16)
    d_proj = jnp.concatenate([d_z, d_gates, d_xbc, d_cq, d_ckv, d_last], axis=1)
    d_u = _mm(d_proj, big["w_in"], out_dtype=BF16)
    d_big["w_in"] = _mm(d_proj, u, ta=True, out_dtype=BF16)
    (dh,), (d_ln,) = _row_bwd(_f_rmsnorm, [h], [_row(small["ln_mix"])], [d_u], [F32], "ln_mix_bwd", add={0: dh_out})
    d_small["ln_mix"] = d_ln[0]
    return dh, d_big, d_small, d_conv_w, carried


def _prepare_big(b):
    return dict(b, w_in=_arrange_w_in(b["w_in"]), w_uq=_pad_heads(b["w_uq"]))


def _local_step(x, positions, target, big, small, conv_w, packed_last=None):
    inv = 1.0 / (ROPE_THETA ** (jnp.arange(0, ROPE, 2, dtype=F32) / ROPE))
    ang = positions.astype(F32)[:, None] * inv
    cos, sin = jnp.cos(ang), jnp.sin(ang)
    no_lanes = jnp.zeros((x.shape[0], LANE - ROPE), F32)
    cs = jnp.concatenate([cos, cos, no_lanes], axis=1)
    sn = jnp.concatenate([-sin, sin, no_lanes], axis=1)
    carrier = DEPTH - 2 if packed_last is not None else None
    big = [None if b is None else _prepare_big(b) for b in big]
    layer_small = [{k: v[l] for k, v in small.items()} for l in range(DEPTH)]

    h, saved = x, []
    for l in range(DEPTH):
        b, sm = big[l], layer_small[l]
        h, s1 = _ffn_fwd(h, sm["ln_ffn1"], b["ffn1_w13"], b["ffn1_w2"], "ln_ffn1")
        h, s2, gathered = _mixer_fwd(h, b, sm, conv_w[l], cs, sn, _gather_rider(packed_last) if l == carrier else None)
        if gathered is not None:
            big[l + 1] = _prepare_big(_unpack_gathered(gathered))
        h, s3 = _ffn_fwd(h, sm["ln_ffn2"], b["ffn2_w13"], b["ffn2_w2"], "ln_ffn2")
        saved.append((s1, s2, s3))
    loss, dh = _loss_and_grad(h, target)

    d_big, d_small, d_conv_w = [None] * DEPTH, [None] * DEPTH, [None] * DEPTH
    for l in reversed(range(DEPTH)):
        b, sm = big[l], layer_small[l]
        s1, s2, s3 = saved[l]
        dh, d_w13_2, d_w2_2, d_ln2 = _ffn_bwd(dh, s3, sm["ln_ffn2"], b["ffn2_w13"], b["ffn2_w2"], "ln_ffn2")
        rider = _exchange_rider(_pack_full_grads(d_big[l + 1])) if l == carrier else None
        dh, db, ds, d_conv_w[l], received = _mixer_bwd(dh, s2, b, sm, conv_w[l], cs, sn, rider)
        if received is not None:
            d_big[l + 1] = received
        dh, d_w13_1, d_w2_1, d_ln1 = _ffn_bwd(dh, s1, sm["ln_ffn1"], b["ffn1_w13"], b["ffn1_w2"], "ln_ffn1")
        db.update(ffn1_w13=d_w13_1, ffn1_w2=d_w2_1, ffn2_w13=d_w13_2, ffn2_w2=d_w2_2,
                  w_in=_restore_w_in(db["w_in"]), w_uq=_unpad_heads(db["w_uq"]))
        ds.update(ln_ffn1=d_ln1, ln_ffn2=d_ln2)
        d_big[l], d_small[l] = db, ds
    d_small = {name: jnp.stack([d_small[l][name] for l in range(DEPTH)]) for name, _ in SMALL}
    return loss, dh, d_big, d_small, jnp.stack(d_conv_w)


def _step(args):
    dev = 4 * lax.axis_index("x") + 2 * lax.axis_index("y") + lax.axis_index("c")
    x, positions, target = args["x"][0], args["positions"][0], args["loss_target"][0]

    packed = [_pack_shards({name: args[name][l].astype(BF16) for name, _, _ in BIG}) for l in range(DEPTH)]
    big = [_unpack_gathered(_all_gather(packed[l])) for l in range(DEPTH - 1)] + [None]
    cw = args["conv_w"]
    cw_cols = cw.shape[-1]
    cw_rows = -(-cw.size // (8 * PACK_COLS)) * 8
    cw_flat = jnp.concatenate([cw.reshape(-1), jnp.zeros((cw_rows * PACK_COLS - cw.size,), F32)]).reshape(cw_rows, PACK_COLS)
    cw_all = _all_gather(cw_flat).reshape(N_DEV, -1)[:, :cw.size].reshape(N_DEV, DEPTH, CONV_K, cw_cols)
    conv_w = jnp.transpose(cw_all, (1, 2, 0, 3)).reshape(DEPTH, CONV_K, CONV_DIM)
    small = {name: args[name] for name, _ in SMALL}

    loss, dx, d_big, d_small, d_conv_w = _local_step(x, positions, target, big, small, conv_w, packed_last=packed[-1])
    loss = lax.psum(loss, MESH_AXES)

    out = {"loss": loss, "grad_x": dx[None]}

    grads = {name: [] for name, _, _ in BIG}
    for l in range(DEPTH):
        received = d_big[l] if l == DEPTH - 1 else _exchange_blocks(_pack_full_grads(d_big[l]))
        summed = _sum_blocks(received)
        for name, g in _unpack_shards(summed).items():
            grads[name].append(g)
    flat = lambda t: t.reshape(-1, t.shape[-1])
    for name, _, _ in BIG:
        g = jnp.stack(grads[name])
        w = args[name]
        delta, m2, v2 = _adam(flat(w), flat(g), flat(args["m_" + name]), flat(args["v_" + name]))
        out["grad_" + name] = g
        out["delta_" + name] = delta.reshape(w.shape)
        out["new_m_" + name] = m2.reshape(w.shape)
        out["new_v_" + name] = v2.reshape(w.shape)

    total = _sum_blocks(_all_gather(_pack_small(d_small, d_conv_w)))
    g_conv_w = _unpack_small(total)[1]
    zeros_cw = jnp.zeros((DEPTH, CONV_K, CONV_DIM), F32)
    delta, m2, v2 = _adam(_pack_small(small, zeros_cw), total,
                          _pack_small({name: args["m_" + name] for name, _ in SMALL}, zeros_cw),
                          _pack_small({name: args["v_" + name] for name, _ in SMALL}, zeros_cw))
    for kind, packed in (("grad_", total), ("delta_", delta), ("new_m_", m2), ("new_v_", v2)):
        for name, val in _unpack_small(packed)[0].items():
            out[kind + name] = val
    g_cw = lax.dynamic_slice_in_dim(g_conv_w, dev * cw_cols, cw_cols, axis=2)
    delta, m2, v2 = _adam(flat(cw), flat(g_cw), flat(args["m_conv_w"]), flat(args["v_conv_w"]))
    out["grad_conv_w"] = g_cw
    out["delta_conv_w"] = delta.reshape(cw.shape)
    out["new_m_conv_w"] = m2.reshape(cw.shape)
    out["new_v_conv_w"] = v2.reshape(cw.shape)
    return out


WEIGHTS = ["ln_ffn1", "ffn1_w13", "ffn1_w2", "ln_mix", "w_in", "conv_w", "conv_b", "dt_bias", "a_log", "d_skip", "ssd_norm",
           "w_ssd_out", "q_lora_norm", "w_uq", "kv_lora_norm", "w_ukv", "q_norm", "k_norm", "w_mla_out", "w_o", "ln_ffn2",
           "ffn2_w13", "ffn2_w2"]
ARG_NAMES = (["x", "positions"] + WEIGHTS + ["loss_target"] + ["m_" + n for n in WEIGHTS] + ["v_" + n for n in WEIGHTS])


def kernel(x, positions, ln_ffn1, ffn1_w13, ffn1_w2, ln_mix, w_in, conv_w, conv_b, dt_bias, a_log, d_skip, ssd_norm, w_ssd_out, q_lora_norm, w_uq, kv_lora_norm, w_ukv, q_norm, k_norm, w_mla_out, w_o, ln_ffn2, ffn2_w13, ffn2_w2, loss_target, m_ln_ffn1, m_ffn1_w13, m_ffn1_w2, m_ln_mix, m_w_in, m_conv_w, m_conv_b, m_dt_bias, m_a_log, m_d_skip, m_ssd_norm, m_w_ssd_out, m_q_lora_norm, m_w_uq, m_kv_lora_norm, m_w_ukv, m_q_norm, m_k_norm, m_w_mla_out, m_w_o, m_ln_ffn2, m_ffn2_w13, m_ffn2_w2, v_ln_ffn1, v_ffn1_w13, v_ffn1_w2, v_ln_mix, v_w_in, v_conv_w, v_conv_b, v_dt_bias, v_a_log, v_d_skip, v_ssd_norm, v_w_ssd_out, v_q_lora_norm, v_w_uq, v_kv_lora_norm, v_w_ukv, v_q_norm, v_k_norm, v_w_mla_out, v_w_o, v_ln_ffn2, v_ffn2_w13, v_ffn2_w2):
    vals = (x, positions, ln_ffn1, ffn1_w13, ffn1_w2, ln_mix, w_in, conv_w, conv_b, dt_bias, a_log, d_skip, ssd_norm, w_ssd_out, q_lora_norm, w_uq, kv_lora_norm, w_ukv, q_norm, k_norm, w_mla_out, w_o, ln_ffn2, ffn2_w13, ffn2_w2, loss_target, m_ln_ffn1, m_ffn1_w13, m_ffn1_w2, m_ln_mix, m_w_in, m_conv_w, m_conv_b, m_dt_bias, m_a_log, m_d_skip, m_ssd_norm, m_w_ssd_out, m_q_lora_norm, m_w_uq, m_kv_lora_norm, m_w_ukv, m_q_norm, m_k_norm, m_w_mla_out, m_w_o, m_ln_ffn2, m_ffn2_w13, m_ffn2_w2, v_ln_ffn1, v_ffn1_w13, v_ffn1_w2, v_ln_mix, v_w_in, v_conv_w, v_conv_b, v_dt_bias, v_a_log, v_d_skip, v_ssd_norm, v_w_ssd_out, v_q_lora_norm, v_w_uq, v_kv_lora_norm, v_w_ukv, v_q_norm, v_k_norm, v_w_mla_out, v_w_o, v_ln_ffn2, v_ffn2_w13, v_ffn2_w2)
    out = _step(dict(zip(ARG_NAMES, vals)))
    order = ["loss", "grad_x"] + [k + n for k in ("grad_", "delta_", "new_m_", "new_v_") for n in WEIGHTS]
    return tuple(out[n] for n in order)
```

```python
import jax
import jax.numpy as jnp
from jax import lax
from jax.experimental import pallas as pl
from jax.experimental.pallas import tpu as pltpu

F32 = jnp.float32
BF16 = jnp.bfloat16

D_MODEL = 1024
D_FF = 2816
DEPTH = 2
SSD_DI = 2048
SSD_P = 64
SSD_H = 32
SSD_G = 4
SSD_HPG = 8
SSD_N = 128
SSD_L = 128
CONV_K = 4
CONV_DIM = 3072
MLA_H = 8
Q_LORA = 512
KV_LORA = 256
NOPE = 128
ROPE = 64
VDIM = 128
QK = 192
ROPE_THETA = 10000.0
EPS = 1e-6
IN_SPLIT = (SSD_DI, CONV_DIM, SSD_H, Q_LORA, KV_LORA, ROPE, 2 * D_MODEL)
D_IN = sum(IN_SPLIT)
N_DEV = 8
LANE = 128
PACK_COLS = 1024

PROJ_Z = 0
PROJ_GATES = PROJ_Z + SSD_DI
PROJ_XBC = PROJ_GATES + 2 * D_MODEL
PROJ_CQ = PROJ_XBC + CONV_DIM
PROJ_CKV = PROJ_CQ + Q_LORA
PROJ_LAST = PROJ_CKV + KV_LORA
D_IN_PAD = PROJ_LAST + LANE

ADAM_LR = 0.001
ADAM_B1 = 0.9
ADAM_B2 = 0.999
ADAM_EPS = 1e-08
ADAM_WD = 0.01
ADAM_STEP = 10

VMEM_LIMIT = 48 * 1024 * 1024
ROW_IO_BUDGET = 8 * 1024 * 1024
NEG = -1e30

MESH_AXES = ("x", "y", "c")


def _cparams(*sem):
    return pltpu.CompilerParams(dimension_semantics=sem, vmem_limit_bytes=VMEM_LIMIT)


def _pick_tile(n, target, align):
    if n <= target:
        return n
    best = None
    for t in range(align, target + 1, align):
        if n % t == 0:
            best = t
    assert best is not None, (n, target, align)
    return best


def _acc_store(ref, val, first):
    @pl.when(first)
    def _():
        ref[...] = val

    @pl.when(jnp.logical_not(first))
    def _():
        ref[...] += val


def _win(arr, start, width):
    assert start % width == 0, (start, width)
    return (arr, start, width)


def _operand(entry):
    if isinstance(entry, tuple):
        arr, start, width = entry
        return arr, width, start // width
    return entry, entry.shape[1], 0


def _row_tile(rows, bytes_per_row):
    if rows <= 16:
        return rows
    t = 1024
    while t > 16 and (t * bytes_per_row > ROW_IO_BUDGET or rows % t):
        t //= 2
    assert rows % t == 0, (rows, t)
    return t


def _rowwise_call(fn, tiled, params, outs, accs, name):
    ops = [_operand(e) for e in tiled]
    rows = ops[0][0].shape[0]
    per_row = sum(w * a.dtype.itemsize for a, w, _ in ops) + sum(c * jnp.dtype(d).itemsize for c, d in outs)
    tile = _row_tile(rows, per_row)
    n_in = len(tiled) + len(params)
    n_o = len(outs)

    def body(*refs):
        vals = [r[...] for r in refs[:n_in]]
        t_out, a_out = fn(*vals)
        for r, v in zip(refs[n_in:n_in + n_o], t_out):
            r[...] = v.astype(r.dtype)
        first = pl.program_id(0) == 0
        for r, v in zip(refs[n_in + n_o:], a_out):
            _acc_store(r, v.astype(F32), first)

    def tiled_spec(width, blk):
        return pl.BlockSpec((tile, width), lambda i: (i, blk))

    in_specs = [tiled_spec(w, blk) for _, w, blk in ops]
    in_specs += [pl.BlockSpec(p.shape, lambda i: (0, 0)) for p in params]
    out_specs = [tiled_spec(c, 0) for c, _ in outs]
    out_specs += [pl.BlockSpec(s, lambda i: (0, 0)) for s in accs]
    out_shape = [jax.ShapeDtypeStruct((rows, c), d) for c, d in outs]
    out_shape += [jax.ShapeDtypeStruct(s, F32) for s in accs]
    return pl.pallas_call(
        body, grid=(rows // tile,), in_specs=in_specs, out_specs=out_specs, out_shape=out_shape,
        compiler_params=_cparams("arbitrary"), name=name,
    )(*[a for a, _, _ in ops], *params)


def _to_f32(vals):
    return [v.astype(F32) for v in vals]


def _row_fwd(f, tiled, params, out_dtypes, name):
    ops = [_operand(e) for e in tiled]
    rows = ops[0][0].shape[0]
    shapes = jax.eval_shape(f, *[jax.ShapeDtypeStruct((rows, w), F32) for _, w, _ in ops],
                            *[jax.ShapeDtypeStruct(p.shape, F32) for p in params])
    outs = [(s.shape[1], d) for s, d in zip(shapes, out_dtypes)]
    return _rowwise_call(lambda *v: (f(*_to_f32(v)), ()), tiled, params, outs, [], name)


def _row_bwd(f, tiled, params, gs, d_dtypes, name, bwd=None, add=None):
    n_t, n_g = len(tiled), len(gs)
    adds = sorted((add or {}).items())
    n_a = len(adds)

    def fn(*vals):
        vals = _to_f32(vals)
        prim = vals[:n_t] + vals[n_t + n_g + n_a:]
        g = tuple(vals[n_t:n_t + n_g])
        if bwd is not None:
            d_t, d_p = bwd(*prim, *g)
        else:
            _, vjp = jax.vjp(f, *prim)
            cts = vjp(g)
            d_t, d_p = cts[:n_t], cts[n_t:]
        d_t = list(d_t)
        for (idx, _), extra in zip(adds, vals[n_t + n_g:n_t + n_g + n_a]):
            d_t[idx] = d_t[idx] + extra
        return tuple(d_t), tuple(d_p)

    outs = [(_operand(e)[1], d) for e, d in zip(tiled, d_dtypes)]
    accs = [p.shape for p in params]
    res = _rowwise_call(fn, list(tiled) + list(gs) + [a for _, a in adds], params, outs, accs, name)
    return res[:n_t], res[n_t:]


def _f_rmsnorm(x, g):
    return (x * lax.rsqrt(jnp.mean(x * x, axis=-1, keepdims=True) + EPS) * g,)


def _f_swiglu(gu):
    gate, up = gu[:, :D_FF], gu[:, D_FF:]
    return (gate * jax.nn.sigmoid(gate) * up,)


def _b_swiglu(gu, d):
    gate, up = gu[:, :D_FF], gu[:, D_FF:]
    s = jax.nn.sigmoid(gate)
    d_gate = d * up * s * (1.0 + gate * (1.0 - s))
    d_up = d * gate * s
    return (jnp.concatenate([d_gate, d_up], axis=1),), ()


def _f_gated_norm(ys, xs, z, dsk, g):
    t = (ys + xs * dsk) * (z * jax.nn.sigmoid(z))
    return (t * lax.rsqrt(jnp.mean(t * t, axis=-1, keepdims=True) + EPS) * g,)


def _f_merge(gates, ys, ym):
    s = jax.nn.sigmoid(gates)
    return (s[:, :D_MODEL] * ys + s[:, D_MODEL:] * ym,)


def _b_merge(gates, ys, ym, d):
    s = jax.nn.sigmoid(gates)
    s1, s2 = s[:, :D_MODEL], s[:, D_MODEL:]
    d_gates = jnp.concatenate([d * ys * s1 * (1.0 - s1), d * ym * s2 * (1.0 - s2)], axis=1)
    return (d_gates, d * s1, d * s2), ()


def _loss_and_grad(y, target):
    def fn(yv, tv):
        d = yv - tv
        return (d * (1.0 / D_MODEL),), (jnp.sum(d * d, axis=0, keepdims=True) * (0.5 / D_MODEL),)

    dy, part = _rowwise_call(fn, [y, target], [], [(D_MODEL, F32)], [(1, D_MODEL)], "loss")
    return jnp.sum(part), dy


def _adam(w, g, m, v):
    def fn(wv, gv, mv, vv):
        m2 = ADAM_B1 * mv + (1.0 - ADAM_B1) * gv
        v2 = ADAM_B2 * vv + (1.0 - ADAM_B2) * (gv * gv)
        m_hat = m2 / (1.0 - ADAM_B1 ** ADAM_STEP)
        v_hat = v2 / (1.0 - ADAM_B2 ** ADAM_STEP)
        delta = -ADAM_LR * (m_hat / (jnp.sqrt(v_hat) + ADAM_EPS) + ADAM_WD * wv)
        return (delta, m2, v2), ()

    c = w.shape[1]
    return _rowwise_call(fn, [w, g, m, v], [], [(c, F32)] * 3, [], "adamw")


def _sum_blocks(blocks):
    _, rows, c = blocks.shape
    tile = _row_tile(rows, N_DEV * c * blocks.dtype.itemsize + c * 4)

    def body(b_ref, o_ref):
        acc = b_ref[0].astype(F32)
        for i in range(1, N_DEV):
            acc = acc + b_ref[i].astype(F32)
        o_ref[...] = acc

    return pl.pallas_call(
        body, grid=(rows // tile,), in_specs=[pl.BlockSpec((N_DEV, tile, c), lambda i: (0, i, 0))],
        out_specs=pl.BlockSpec((tile, c), lambda i: (i, 0)), out_shape=jax.ShapeDtypeStruct((rows, c), F32),
        compiler_params=_cparams("arbitrary"), name="sum_blocks",
    )(blocks)


def _mm(a, b, ta=False, tb=False, out_dtype=F32, alpha=1.0, res=None):
    r_dim, p_dim = a.shape if ta else a.shape[::-1]
    r2, q_dim = b.shape[::-1] if tb else b.shape
    assert r_dim == r2, (a.shape, b.shape, ta, tb)
    tp = _pick_tile(p_dim, 512, LANE)
    if tp < 512 < p_dim:
        tp = _pick_tile(p_dim, 1536, LANE)
    tq = _pick_tile(q_dim, 1536, LANE)
    tr = _pick_tile(r_dim, 1536, LANE)
    nr = r_dim // tr
    dims = (((0 if ta else 1,), (1 if tb else 0,)), ((), ()))
    has_res = res is not None

    def body(*refs):
        a_ref, b_ref = refs[:2]
        res_ref = refs[2] if has_res else None
        o_ref = refs[2 + has_res]

        def finish(val):
            if alpha != 1.0:
                val = val * alpha
            if has_res:
                val = val + res_ref[...].astype(F32)
            o_ref[...] = val.astype(o_ref.dtype)

        part = lax.dot_general(a_ref[...].astype(BF16), b_ref[...].astype(BF16), dims, preferred_element_type=F32)
        if nr == 1:
            finish(part)
        else:
            acc_ref = refs[3 + has_res]
            k = pl.program_id(2)
            _acc_store(acc_ref, part, k == 0)

            @pl.when(k == nr - 1)
            def _():
                finish(acc_ref[...])

    a_spec = pl.BlockSpec((tr, tp), lambda j, i, k: (k, i)) if ta else pl.BlockSpec((tp, tr), lambda j, i, k: (i, k))
    b_spec = pl.BlockSpec((tq, tr), lambda j, i, k: (j, k)) if tb else pl.BlockSpec((tr, tq), lambda j, i, k: (k, j))
    o_spec = pl.BlockSpec((tp, tq), lambda j, i, k: (i, j))
    return pl.pallas_call(
        body, grid=(q_dim // tq, p_dim // tp, nr), in_specs=[a_spec, b_spec] + ([o_spec] if has_res else []),
        out_specs=o_spec, out_shape=jax.ShapeDtypeStruct((p_dim, q_dim), out_dtype),
        scratch_shapes=[pltpu.VMEM((tp, tq), F32)] if nr > 1 else [],
        compiler_params=_cparams("arbitrary", "arbitrary", "arbitrary"),
        name=f"mm_{'t' if ta else 'n'}{'t' if tb else 'n'}_{p_dim}x{r_dim}x{q_dim}",
    )(*([a, b] + ([res] if has_res else [])))


ATTN_SCALE = QK ** -0.5
LOG2E = 1.4426950408889634
ATTN_C = ATTN_SCALE * LOG2E


def _attn_tile(s):
    return min(512, s)


def _causal_keep(t, keys_on_rows=False):
    row = lax.broadcasted_iota(jnp.int32, (t, t), 0)
    col = lax.broadcasted_iota(jnp.int32, (t, t), 1)
    return row <= col if keys_on_rows else col <= row


def _nt(a, b):
    return lax.dot_general(a, b, (((1,), (1,)), ((), ())), preferred_element_type=F32)


def _rider_phases(rider, src_ref, out_ref, sems, first, last):
    @pl.when(first)
    def _():
        _peer_copies(src_ref, out_ref, sems, rider["gather"], "start")

    def finish():
        @pl.when(last)
        def _():
            _peer_copies(src_ref, out_ref, sems, rider["gather"], "finish")

    return finish


def _attn_fwd_call(q, k, v, rider=None):
    nh, s, _ = q.shape
    t = _attn_tile(s)
    nb = s // t
    n_r = 0 if rider is None else 1

    def body(*refs):
        q_ref, k_ref, v_ref = refs[:3]
        o_ref, lse_ref = refs[3 + n_r:5 + n_r]
        qi = pl.program_id(1)
        finish = None
        if rider is not None:
            h = pl.program_id(0)
            finish = _rider_phases(rider, refs[3], refs[5 + n_r], refs[6 + n_r:],
                                   jnp.logical_and(h == 0, qi == 0), jnp.logical_and(h == nh - 1, qi == nb - 1))
        q = q_ref[0]

        def block(kb, carry, diagonal, width=1):
            m_prev, l_prev, acc = carry
            start = pl.multiple_of(kb * t, t)
            sc = _nt(q, k_ref[0, pl.ds(start, width * t), :])
            if diagonal:
                sc = jnp.where(_causal_keep(t), sc, NEG)
            m_new = jnp.maximum(m_prev, jnp.max(sc, axis=-1, keepdims=True))
            p = jnp.exp2(sc * ATTN_C - m_new * ATTN_C)
            alpha = jnp.exp2((m_prev - m_new) * ATTN_C)
            l_new = alpha * l_prev + jnp.sum(p, axis=-1, keepdims=True)
            acc = alpha * acc + jnp.dot(p.astype(BF16), v_ref[0, pl.ds(start, width * t), :], preferred_element_type=F32)
            return m_new, l_new, acc

        init = (jnp.full((t, 1), NEG, F32), jnp.zeros((t, 1), F32), jnp.zeros((t, VDIM), F32))
        carry = lax.fori_loop(0, qi // 2, lambda j, c: block(2 * j, c, False, width=2), init)
        carry = lax.cond(qi % 2 == 1, lambda c: block(qi - 1, c, False), lambda c: c, carry)
        m, l, acc = block(qi, carry, True)
        o_ref[0] = (acc / l).astype(o_ref.dtype)
        lse_ref[0] = m * ATTN_SCALE + jnp.log(l)
        if finish is not None:
            finish()

    qmap = lambda h, i: (h, i, 0)
    whole = lambda h, i: (h, 0, 0)
    return pl.pallas_call(
        body, grid=(nh, nb),
        in_specs=[pl.BlockSpec((1, t, QK), qmap), pl.BlockSpec((1, s, QK), whole), pl.BlockSpec((1, s, VDIM), whole)] + [HBM_SPEC] * n_r,
        out_specs=[pl.BlockSpec((1, t, VDIM), qmap), pl.BlockSpec((1, t, 1), qmap)] + [HBM_SPEC] * n_r,
        out_shape=[jax.ShapeDtypeStruct((nh, s, VDIM), BF16), jax.ShapeDtypeStruct((nh, s, 1), F32)] + ([rider["out"]] if n_r else []),
        scratch_shapes=_comm_scratch() if n_r else [],
        compiler_params=_cparams("arbitrary", "arbitrary"), name="attn_fwd_gather" if n_r else "attn_fwd",
    )(*([q, k, v] + ([rider["src"]] if n_r else [])))


def _attn_delta_call(o, do):
    nh, s, d = o.shape

    def fn(ov, dv):
        return (jnp.sum(ov.astype(F32) * dv.astype(F32), axis=-1, keepdims=True),), ()

    return _rowwise_call(fn, [o.reshape(nh * s, d), do.reshape(nh * s, d)], [], [(1, F32)], [], "attn_delta")[0]


def _attn_bwd_call(q, k, v, do, lse_t, delta_t, rider=None):
    nh, s, _ = q.shape
    t = _attn_tile(s)
    nb = s // t
    n_r = 0 if rider is None else 1

    def body(*refs):
        q_ref, k_ref, v_ref, do_ref, lse_ref, delta_ref = refs[:6]
        dq_ref, dk_ref, dv_ref = refs[6 + n_r:9 + n_r]
        dk_sc, dv_sc = refs[9 + 2 * n_r:11 + 2 * n_r]
        kj = pl.program_id(1)
        finish = None
        if rider is not None:
            h = pl.program_id(0)
            finish = _rider_phases(rider, refs[6], refs[9 + n_r], refs[11 + 2 * n_r:],
                                   jnp.logical_and(h == 0, kj == 0), jnp.logical_and(h == nh - 1, kj == nb - 1))

        @pl.when(kj == 0)
        def _():
            dq_ref[...] = jnp.zeros_like(dq_ref)

        dk_sc[...] = jnp.zeros_like(dk_sc)
        dv_sc[...] = jnp.zeros_like(dv_sc)
        kblk, vblk = k_ref[0], v_ref[0]

        def block(qb, diagonal):
            start = pl.multiple_of(qb * t, t)
            qblk = q_ref[0, pl.ds(start, t), :]
            doblk = do_ref[0, pl.ds(start, t), :]
            sc = _nt(kblk, qblk)
            if diagonal:
                sc = jnp.where(_causal_keep(t, keys_on_rows=True), sc, NEG)
            p = jnp.exp2(sc * ATTN_C - lse_ref[0, :, pl.ds(start, t)] * LOG2E)
            dv_sc[...] += jnp.dot(p.astype(BF16), doblk, preferred_element_type=F32)
            dp = _nt(vblk, doblk)
            ds = (p * (dp - delta_ref[0, :, pl.ds(start, t)])).astype(BF16)
            dk_sc[...] += jnp.dot(ds, qblk, preferred_element_type=F32)
            dq_ref[0, pl.ds(start, t), :] += lax.dot_general(ds, kblk, (((0,), (0,)), ((), ())), preferred_element_type=F32)

        block(kj, True)

        def rest(qb, carry):
            block(qb, False)
            return carry

        lax.fori_loop(kj + 1, nb, rest, 0)
        dk_ref[0] = (dk_sc[...] * ATTN_SCALE).astype(dk_ref.dtype)
        dv_ref[0] = dv_sc[...].astype(dv_ref.dtype)

        @pl.when(kj == nb - 1)
        def _():
            dq_ref[...] = dq_ref[...] * ATTN_SCALE

        if finish is not None:
            finish()

    kmap = lambda h, j: (h, j, 0)
    whole = lambda h, j: (h, 0, 0)
    return pl.pallas_call(
        body, grid=(nh, nb),
        in_specs=[pl.BlockSpec((1, s, QK), whole), pl.BlockSpec((1, t, QK), kmap), pl.BlockSpec((1, t, VDIM), kmap),
                  pl.BlockSpec((1, s, VDIM), whole), pl.BlockSpec((1, 1, s), whole), pl.BlockSpec((1, 1, s), whole)] + [HBM_SPEC] * n_r,
        out_specs=[pl.BlockSpec((1, s, QK), whole), pl.BlockSpec((1, t, QK), kmap), pl.BlockSpec((1, t, VDIM), kmap)] + [HBM_SPEC] * n_r,
        out_shape=[jax.ShapeDtypeStruct((nh, s, QK), F32), jax.ShapeDtypeStruct((nh, s, QK), F32),
                   jax.ShapeDtypeStruct((nh, s, VDIM), F32)] + ([rider["out"]] if n_r else []),
        scratch_shapes=[pltpu.VMEM((t, QK), F32), pltpu.VMEM((t, VDIM), F32)] + (_comm_scratch() if n_r else []),
        compiler_params=_cparams("arbitrary", "arbitrary"), name="attn_bwd_exchange" if n_r else "attn_bwd",
    )(*([q, k, v, do, lse_t, delta_t] + ([rider["src"]] if n_r else [])))


HEAD_COLS = NOPE + VDIM
HEADS_TILE = 256


def _swap_rope_halves(t, lane):
    half = ROPE // 2
    return jnp.where(lane < half, pltpu.roll(t, LANE - half, 1), pltpu.roll(t, half, 1))


def _head_fwd(n, p, gain, cs, sn, lane):
    r = lax.rsqrt((jnp.sum(n * n, axis=-1, keepdims=True) + jnp.sum(p * p, axis=-1, keepdims=True)) * (1.0 / QK) + EPS)
    yp = p * r * gain[:, NOPE:]
    return n * r * gain[:, :NOPE], yp * cs + _swap_rope_halves(yp, lane) * sn


def _head_bwd(n, p, gain, cs, sn, lane, dzn, dzp):
    r = lax.rsqrt((jnp.sum(n * n, axis=-1, keepdims=True) + jnp.sum(p * p, axis=-1, keepdims=True)) * (1.0 / QK) + EPS)
    dyp = dzp * cs + _swap_rope_halves(dzp * sn, lane)
    gyn, gyp = dzn * gain[:, :NOPE], dyp * gain[:, NOPE:]
    dot = jnp.sum(gyn * n, axis=-1, keepdims=True) + jnp.sum(gyp * p, axis=-1, keepdims=True)
    coef = dot * (r * r * r) * (1.0 / QK)
    d_gn = jnp.sum(dzn * n * r, axis=0, keepdims=True)
    d_gp = jnp.sum(dyp * p * r, axis=0, keepdims=True)
    return gyn * r - n * coef, gyp * r - p * coef, d_gn, d_gp


def _rope_key(last_ref, lane):
    return jnp.where(lane < ROPE, last_ref[...], 0.0)


def _heads_fwd_call(q, kv, proj, cs, sn, q_gain, k_gain):
    s = q.shape[0]
    t = min(HEADS_TILE, s)

    def body(q_ref, kv_ref, last_ref, cs_ref, sn_ref, qg_ref, kg_ref, qh_ref, kh_ref, vh_ref):
        lane = lax.broadcasted_iota(jnp.int32, (t, LANE), 1)
        cs_, sn_ = cs_ref[...], sn_ref[...]
        kp = _rope_key(last_ref, lane)
        for h in range(MLA_H):
            c0 = h * HEAD_COLS
            zn, zp = _head_fwd(q_ref[:, c0:c0 + NOPE], q_ref[:, c0 + NOPE:c0 + HEAD_COLS], qg_ref[...], cs_, sn_, lane)
            qh_ref[h, :, :NOPE] = zn.astype(BF16)
            qh_ref[h, :, NOPE:] = zp[:, :ROPE].astype(BF16)
            zn, zp = _head_fwd(kv_ref[:, c0:c0 + NOPE], kp, kg_ref[...], cs_, sn_, lane)
            kh_ref[h, :, :NOPE] = zn.astype(BF16)
            kh_ref[h, :, NOPE:] = zp[:, :ROPE].astype(BF16)
            vh_ref[h] = kv_ref[:, c0 + NOPE:c0 + HEAD_COLS].astype(BF16)

    rows = lambda i: (i, 0)
    whole = lambda i: (0, 0)
    heads = lambda i: (0, i, 0)
    wide = MLA_H * HEAD_COLS
    return pl.pallas_call(
        body, grid=(s // t,),
        in_specs=[pl.BlockSpec((t, wide), rows), pl.BlockSpec((t, wide), rows),
                  pl.BlockSpec((t, LANE), lambda i: (i, PROJ_LAST // LANE)),
                  pl.BlockSpec((t, LANE), rows), pl.BlockSpec((t, LANE), rows),
                  pl.BlockSpec((1, HEAD_COLS), whole), pl.BlockSpec((1, HEAD_COLS), whole)],
        out_specs=[pl.BlockSpec((MLA_H, t, QK), heads), pl.BlockSpec((MLA_H, t, QK), heads), pl.BlockSpec((MLA_H, t, VDIM), heads)],
        out_shape=[jax.ShapeDtypeStruct((MLA_H, s, QK), BF16), jax.ShapeDtypeStruct((MLA_H, s, QK), BF16),
                   jax.ShapeDtypeStruct((MLA_H, s, VDIM), BF16)],
        compiler_params=_cparams("arbitrary"), name="mla_heads_fwd",
    )(q, kv, proj, cs, sn, q_gain, k_gain)


def _heads_bwd_call(q, kv, proj, cs, sn, q_gain, k_gain, dqh, dkh, dvh):
    s = q.shape[0]
    t = min(HEADS_TILE, s)

    def body(q_ref, kv_ref, last_ref, cs_ref, sn_ref, qg_ref, kg_ref, dqh_ref, dkh_ref, dvh_ref,
             dq_ref, dkv_ref, dkr_ref, dqg_ref, dkg_ref):
        lane = lax.broadcasted_iota(jnp.int32, (t, LANE), 1)
        cs_, sn_ = cs_ref[...], sn_ref[...]
        kp = _rope_key(last_ref, lane)
        no_lanes = jnp.zeros((t, LANE - ROPE), F32)
        d_kp = jnp.zeros((t, LANE), F32)
        d_qg = [jnp.zeros((1, NOPE), F32), jnp.zeros((1, LANE), F32)]
        d_kg = [jnp.zeros((1, NOPE), F32), jnp.zeros((1, LANE), F32)]
        for h in range(MLA_H):
            c0 = h * HEAD_COLS
            dz = dqh_ref[h]
            dzp = jnp.concatenate([dz[:, NOPE:], no_lanes], axis=1)
            d_n, d_p, g_n, g_p = _head_bwd(q_ref[:, c0:c0 + NOPE], q_ref[:, c0 + NOPE:c0 + HEAD_COLS], qg_ref[...],
                                           cs_, sn_, lane, dz[:, :NOPE], dzp)
            dq_ref[:, c0:c0 + NOPE] = d_n.astype(dq_ref.dtype)
            dq_ref[:, c0 + NOPE:c0 + HEAD_COLS] = d_p.astype(dq_ref.dtype)
            d_qg = [d_qg[0] + g_n, d_qg[1] + g_p]
            dz = dkh_ref[h]
            dzp = jnp.concatenate([dz[:, NOPE:], no_lanes], axis=1)
            d_n, d_p, g_n, g_p = _head_bwd(kv_ref[:, c0:c0 + NOPE], kp, kg_ref[...], cs_, sn_, lane, dz[:, :NOPE], dzp)
            dkv_ref[:, c0:c0 + NOPE] = d_n.astype(dkv_ref.dtype)
            dkv_ref[:, c0 + NOPE:c0 + HEAD_COLS] = dvh_ref[h].astype(dkv_ref.dtype)
            d_kp = d_kp + d_p
            d_kg = [d_kg[0] + g_n, d_kg[1] + g_p]
        dkr_ref[...] = d_kp
        first = pl.program_id(0) == 0
        _acc_store(dqg_ref.at[:, pl.ds(0, NOPE)], d_qg[0], first)
        _acc_store(dqg_ref.at[:, pl.ds(NOPE, LANE)], d_qg[1], first)
        _acc_store(dkg_ref.at[:, pl.ds(0, NOPE)], d_kg[0], first)
        _acc_store(dkg_ref.at[:, pl.ds(NOPE, LANE)], d_kg[1], first)

    rows = lambda i: (i, 0)
    whole = lambda i: (0, 0)
    heads = lambda i: (0, i, 0)
    wide = MLA_H * HEAD_COLS
    return pl.pallas_call(
        body, grid=(s // t,),
        in_specs=[pl.BlockSpec((t, wide), rows), pl.BlockSpec((t, wide), rows),
                  pl.BlockSpec((t, LANE), lambda i: (i, PROJ_LAST // LANE)),
                  pl.BlockSpec((t, LANE), rows), pl.BlockSpec((t, LANE), rows),
                  pl.BlockSpec((1, HEAD_COLS), whole), pl.BlockSpec((1, HEAD_COLS), whole),
                  pl.BlockSpec((MLA_H, t, QK), heads), pl.BlockSpec((MLA_H, t, QK), heads), pl.BlockSpec((MLA_H, t, VDIM), heads)],
        out_specs=[pl.BlockSpec((t, wide), rows), pl.BlockSpec((t, wide), rows), pl.BlockSpec((t, LANE), rows),
                   pl.BlockSpec((1, HEAD_COLS), whole), pl.BlockSpec((1, HEAD_COLS), whole)],
        out_shape=[jax.ShapeDtypeStruct((s, wide), BF16), jax.ShapeDtypeStruct((s, wide), BF16), jax.ShapeDtypeStruct((s, LANE), F32),
                   jax.ShapeDtypeStruct((1, HEAD_COLS), F32), jax.ShapeDtypeStruct((1, HEAD_COLS), F32)],
        compiler_params=_cparams("arbitrary"), name="mla_heads_bwd",
    )(q, kv, proj, cs, sn, q_gain, k_gain, dqh, dkh, dvh)


CONV_TC = 512
HALO = 8


def _conv_tiles(s):
    return min(512, s)


def _conv_fwd_call(x, col0, w, b):
    s = x.shape[0]
    ts = _conv_tiles(s)
    hb = ts // HALO
    c0 = col0 // CONV_TC
    assert col0 % CONV_TC == 0

    def body(x_ref, prev_ref, w_ref, b_ref, y_ref, buf):
        si = pl.program_id(1)
        buf[0:HALO, :] = jnp.where(si > 0, prev_ref[...], 0.0)
        buf[HALO:, :] = x_ref[...]
        acc = jnp.broadcast_to(b_ref[...], (ts, CONV_TC))
        for k in range(CONV_K):
            acc = acc + w_ref[k:k + 1, :] * buf[pl.ds(HALO - (CONV_K - 1) + k, ts), :]
        y_ref[...] = acc * jax.nn.sigmoid(acc)

    return pl.pallas_call(
        body, grid=(CONV_DIM // CONV_TC, s // ts),
        in_specs=[pl.BlockSpec((ts, CONV_TC), lambda ci, si: (si, ci + c0)),
                  pl.BlockSpec((HALO, CONV_TC), lambda ci, si: (jnp.maximum(si * hb - 1, 0), ci + c0)),
                  pl.BlockSpec((CONV_K, CONV_TC), lambda ci, si: (0, ci)),
                  pl.BlockSpec((1, CONV_TC), lambda ci, si: (0, ci))],
        out_specs=pl.BlockSpec((ts, CONV_TC), lambda ci, si: (si, ci)),
        out_shape=jax.ShapeDtypeStruct((s, CONV_DIM), F32),
        scratch_shapes=[pltpu.VMEM((ts + HALO, CONV_TC), F32)],
        compiler_params=_cparams("arbitrary", "arbitrary"), name="conv_fwd",
    )(x, x, w, b)


def _conv_bwd_call(x, col0, w, b, dy):
    s = x.shape[0]
    ts = _conv_tiles(s)
    hb = ts // HALO
    ns = s // ts
    last_halo = s // HALO - 1
    c0 = col0 // CONV_TC

    def body(x_ref, prev_ref, next_ref, dy_ref, dyn_ref, w_ref, b_ref, dx_ref, dw_ref, db_ref, xbuf, dbuf):
        si = pl.program_id(1)
        xbuf[0:HALO, :] = jnp.where(si > 0, prev_ref[...], 0.0)
        xbuf[HALO:HALO + ts, :] = x_ref[...]
        xbuf[HALO + ts:, :] = next_ref[...]
        pre = jnp.broadcast_to(b_ref[...], (ts + HALO, CONV_TC))
        for k in range(CONV_K):
            pre = pre + w_ref[k:k + 1, :] * xbuf[pl.ds(HALO - (CONV_K - 1) + k, ts + HALO), :]
        sg = jax.nn.sigmoid(pre)
        dsilu = sg * (1.0 + pre * (1.0 - sg))
        dbuf[0:ts, :] = dy_ref[...] * dsilu[0:ts]
        dbuf[ts:, :] = jnp.where(si < ns - 1, dyn_ref[...] * dsilu[ts:], 0.0)
        dx = jnp.zeros((ts, CONV_TC), F32)
        for k in range(CONV_K):
            dx = dx + w_ref[k:k + 1, :] * dbuf[pl.ds(CONV_K - 1 - k, ts), :]
        dx_ref[...] = dx.astype(dx_ref.dtype)
        dpre = dbuf[0:ts, :]
        first = si == 0
        _acc_store(db_ref, jnp.sum(dpre, axis=0, keepdims=True), first)
        for k in range(CONV_K):
            dw_k = jnp.sum(dpre * xbuf[pl.ds(HALO - (CONV_K - 1) + k, ts), :], axis=0, keepdims=True)
            _acc_store(dw_ref.at[pl.ds(k, 1), :], dw_k, first)

    main = lambda ci, si: (si, ci)
    x_main = lambda ci, si: (si, ci + c0)
    x_prev = lambda ci, si: (jnp.maximum(si * hb - 1, 0), ci + c0)
    x_next = lambda ci, si: (jnp.minimum(si * hb + hb, last_halo), ci + c0)
    return pl.pallas_call(
        body, grid=(CONV_DIM // CONV_TC, ns),
        in_specs=[pl.BlockSpec((ts, CONV_TC), x_main), pl.BlockSpec((HALO, CONV_TC), x_prev), pl.BlockSpec((HALO, CONV_TC), x_next),
                  pl.BlockSpec((ts, CONV_TC), main),
                  pl.BlockSpec((HALO, CONV_TC), lambda ci, si: (jnp.minimum(si * hb + hb, last_halo), ci)),
                  pl.BlockSpec((CONV_K, CONV_TC), lambda ci, si: (0, ci)),
                  pl.BlockSpec((1, CONV_TC), lambda ci, si: (0, ci))],
        out_specs=[pl.BlockSpec((ts, CONV_TC), main),
                   pl.BlockSpec((CONV_K, CONV_TC), lambda ci, si: (0, ci)),
                   pl.BlockSpec((1, CONV_TC), lambda ci, si: (0, ci))],
        out_shape=[jax.ShapeDtypeStruct((s, CONV_DIM), BF16), jax.ShapeDtypeStruct((CONV_K, CONV_DIM), F32),
                   jax.ShapeDtypeStruct((1, CONV_DIM), F32)],
        scratch_shapes=[pltpu.VMEM((ts + 2 * HALO, CONV_TC), F32), pltpu.VMEM((ts + HALO, CONV_TC), F32)],
        compiler_params=_cparams("arbitrary", "arbitrary"), name="conv_bwd",
    )(x, x, x, dy, dy, w, b)


GW = SSD_HPG * SSD_P
B_COL = SSD_DI
C_COL = SSD_DI + SSD_G * SSD_N


def _ones_where(mask):
    return jnp.where(mask, 1.0, 0.0).astype(BF16)


def _split(v, passes):
    parts, rest = [], v
    for i in range(passes):
        part = rest.astype(BF16)
        parts.append(part)
        if i + 1 < passes:
            rest = rest - part.astype(F32)
    return parts


def _dot_sel_r(v, sel, passes=3):
    out = None
    for part in _split(v, passes):
        t = jnp.dot(part, sel, preferred_element_type=F32)
        out = t if out is None else out + t
    return out


def _dot_sel_l(sel, v, passes=3):
    out = None
    for part in _split(v, passes):
        t = jnp.dot(sel, part, preferred_element_type=F32)
        out = t if out is None else out + t
    return out


def _ssd_consts():
    r = lax.broadcasted_iota(jnp.int32, (SSD_L, SSD_L), 0)
    c = lax.broadcasted_iota(jnp.int32, (SSD_L, SSD_L), 1)
    tril = r >= c
    triu = c >= r
    shift = SSD_P.bit_length() - 1
    eh = lax.broadcasted_iota(jnp.int32, (SSD_H, SSD_DI), 0)
    ej = lax.broadcasted_iota(jnp.int32, (SSD_H, SSD_DI), 1)
    expand = _ones_where(lax.shift_right_logical(ej, shift) == eh)
    rj = lax.broadcasted_iota(jnp.int32, (SSD_DI, SSD_H), 0)
    rh = lax.broadcasted_iota(jnp.int32, (SSD_DI, SSD_H), 1)
    reduce_ = _ones_where(lax.shift_right_logical(rj, shift) == rh)
    lane = lax.broadcasted_iota(jnp.int32, (SSD_L, LANE), 1)
    return tril, triu, expand, reduce_, lane < SSD_P


def _ssd_decays(dt, dt_t, a, a_t, tril, triu, expand):
    dta = dt * a
    acum = _dot_sel_l(_ones_where(tril), dta)
    acum_t = _dot_sel_r(dt_t * a_t, _ones_where(triu))
    dta_e = _dot_sel_r(dta, expand)
    acum_e = _dot_sel_r(acum, expand)
    last_e = jnp.sum(dta_e, axis=0, keepdims=True)
    return acum, acum_t, acum_e, last_e


def _head_decay(acum, acum_t, h, tril):
    seg = acum[:, h:h + 1] - acum_t[h:h + 1, :]
    return jnp.exp(jnp.where(tril, seg, NEG))


def _ssd_fwd_call(xbc, dt, a):
    s = xbc.shape[0]
    nc = s // SSD_L
    dt_t = dt.T
    a_t = a.T

    def body(xbc_ref, dt_ref, dtt_ref, a_ref, at_ref, y_ref, st_ref, s_sc):
        ci = pl.program_id(0)

        @pl.when(ci == 0)
        def _():
            s_sc[...] = jnp.zeros_like(s_sc)

        st_ref[0] = s_sc[...]
        tril, triu, expand, _, low_half = _ssd_consts()
        acum, acum_t, acum_e, last_e = _ssd_decays(dt_ref[...], dtt_ref[...], a_ref[...], at_ref[...], tril, triu, expand)
        dt_e = _dot_sel_r(dt_ref[...], expand, passes=2)
        xdt = xbc_ref[:, :SSD_DI] * dt_e
        xdt_b = xdt.astype(BF16)
        xw_b = (xdt * jnp.exp(last_e - acum_e)).astype(BF16)
        ea_e = jnp.exp(acum_e)
        el_e = jnp.exp(last_e)
        for g in range(SSD_G):
            gs = slice(g * GW, (g + 1) * GW)
            bg = xbc_ref[:, B_COL + g * SSD_N:B_COL + (g + 1) * SSD_N]
            cg_b = xbc_ref[:, C_COL + g * SSD_N:C_COL + (g + 1) * SSD_N].astype(BF16)
            bg_b = bg.astype(BF16)
            cb = _nt(cg_b, bg_b)
            st = s_sc[:, gs]
            y_off = jnp.dot(cg_b, st.astype(BF16), preferred_element_type=F32) * ea_e[:, gs]
            for pr in range(SSD_HPG // 2):
                ls = slice(g * GW + pr * LANE, g * GW + (pr + 1) * LANE)
                xp = xdt_b[:, ls]
                yd = []
                for half in range(2):
                    h = g * SSD_HPG + pr * 2 + half
                    m = (cb * _head_decay(acum, acum_t, h, tril)).astype(BF16)
                    yd.append(jnp.dot(m, xp, preferred_element_type=F32))
                y_ref[:, ls] = jnp.where(low_half, yd[0], yd[1]) + y_off[:, pr * LANE:(pr + 1) * LANE]
            s_sc[:, gs] = st * el_e[:, gs] + jnp.dot(bg.T.astype(BF16), xw_b[:, gs], preferred_element_type=F32)

    row = lambda i: (i, 0)
    return pl.pallas_call(
        body, grid=(nc,),
        in_specs=[pl.BlockSpec((SSD_L, CONV_DIM), row), pl.BlockSpec((SSD_L, SSD_H), row),
                  pl.BlockSpec((SSD_H, SSD_L), lambda i: (0, i)), pl.BlockSpec((1, SSD_H), lambda i: (0, 0)),
                  pl.BlockSpec((SSD_H, 1), lambda i: (0, 0))],
        out_specs=[pl.BlockSpec((SSD_L, SSD_DI), row), pl.BlockSpec((1, SSD_N, SSD_DI), lambda i: (i, 0, 0))],
        out_shape=[jax.ShapeDtypeStruct((s, SSD_DI), F32), jax.ShapeDtypeStruct((nc, SSD_N, SSD_DI), F32)],
        scratch_shapes=[pltpu.VMEM((SSD_N, SSD_DI), F32)],
        compiler_params=_cparams("arbitrary"), name="ssd_fwd",
    )(xbc, dt, dt_t, a, a_t)


def _ssd_bwd_call(xbc, dt, a, states, dy, dx_extra):
    s = xbc.shape[0]
    nc = s // SSD_L
    dt_t = dt.T
    a_t = a.T

    def body(xbc_ref, dt_ref, dtt_ref, a_ref, at_ref, st_ref, dy_ref, dxe_ref,
             dxbc_ref, ddt_ref, da_ref, ds_sc, yf_sc, dxd_sc, dxw_sc):
        i = pl.program_id(0)

        @pl.when(i == 0)
        def _():
            ds_sc[...] = jnp.zeros_like(ds_sc)

        tril, triu, expand, reduce_, low_half = _ssd_consts()
        dt = dt_ref[...]
        a_row = a_ref[...]
        acum, acum_t, acum_e, last_e = _ssd_decays(dt, dtt_ref[...], a_row, at_ref[...], tril, triu, expand)
        dt_e = _dot_sel_r(dt, expand, passes=2)
        x = xbc_ref[:, :SSD_DI]
        xdt = x * dt_e
        xdt_b = xdt.astype(BF16)
        w_e = jnp.exp(last_e - acum_e)
        xw_b = (xdt * w_e).astype(BF16)
        ea_e = jnp.exp(acum_e)
        el_e = jnp.exp(last_e)
        dy = dy_ref[...]
        dy_b = dy.astype(BF16)
        s_prev = st_ref[0]
        ds_new = ds_sc[...]
        ds_new_b = ds_new.astype(BF16)
        triu_b = _ones_where(triu)
        strict_tril = jnp.logical_not(triu)
        head_ids = lax.broadcasted_iota(jnp.int32, (1, SSD_H), 1)
        d_dta_diag = jnp.zeros((SSD_L, SSD_H), F32)
        for g in range(SSD_G):
            gs = slice(g * GW, (g + 1) * GW)
            bs_ = slice(B_COL + g * SSD_N, B_COL + (g + 1) * SSD_N)
            cs_ = slice(C_COL + g * SSD_N, C_COL + (g + 1) * SSD_N)
            bg = xbc_ref[:, bs_]
            cg = xbc_ref[:, cs_]
            bg_b, cg_b = bg.astype(BF16), cg.astype(BF16)
            st_b = s_prev[:, gs].astype(BF16)
            y_off = jnp.dot(cg_b, st_b, preferred_element_type=F32) * ea_e[:, gs]
            yf_sc[:, gs] = y_off
            dz_b = (dy[:, gs] * ea_e[:, gs]).astype(BF16)
            d_c = _nt(dz_b, st_b)
            ds_prev = ds_new[:, gs] * el_e[:, gs] + jnp.dot(cg.T.astype(BF16), dz_b, preferred_element_type=F32)
            dxw_sc[:, gs] = jnp.dot(bg_b, ds_new_b[:, gs], preferred_element_type=F32)
            d_b = _nt(xw_b[:, gs], ds_new_b[:, gs])
            cb = _nt(cg_b, bg_b)
            d_g = jnp.zeros((SSD_L, SSD_L), F32)
            for pr in range(SSD_HPG // 2):
                ls = slice(g * GW + pr * LANE, g * GW + (pr + 1) * LANE)
                xp = xdt_b[:, ls]
                dyp = dy[:, ls]
                dyp_b = dy_b[:, ls]
                dxd = []
                for half in range(2):
                    h = g * SSD_HPG + pr * 2 + half
                    dec = _head_decay(acum, acum_t, h, tril)
                    m = cb * dec
                    dxd.append(jnp.dot(m.T.astype(BF16), dyp_b, preferred_element_type=F32))
                    mine = low_half if half == 0 else jnp.logical_not(low_half)
                    d_m = _nt(jnp.where(mine, dyp, 0.0).astype(BF16), xp)
                    d_g = d_g + d_m * dec
                    below = jnp.dot(triu_b, (d_m * m).astype(BF16), preferred_element_type=F32)
                    col = jnp.sum(jnp.where(strict_tril, below, 0.0), axis=1, keepdims=True)
                    d_dta_diag = d_dta_diag + col * jnp.where(head_ids == h, 1.0, 0.0)
                dxd_sc[:, ls] = jnp.where(low_half, dxd[0], dxd[1])
            d_g_b = d_g.astype(BF16)
            dxbc_ref[:, cs_] = d_c + jnp.dot(d_g_b, bg_b, preferred_element_type=F32)
            dxbc_ref[:, bs_] = d_b + jnp.dot(d_g.T.astype(BF16), cg_b, preferred_element_type=F32)
            ds_sc[:, gs] = ds_prev
        dxw = dxw_sc[...]
        dxd = dxd_sc[...]
        dw_e = xdt * dxw * w_e
        d_out = _dot_sel_r(dy * yf_sc[...], reduce_, passes=2)
        d_upd = _dot_sel_r(dw_e, reduce_, passes=2)
        d_tot_e = jnp.sum(ds_new * s_prev, axis=0, keepdims=True) * el_e
        d_tot = _dot_sel_r(jnp.broadcast_to(d_tot_e, (8, SSD_DI)), reduce_, passes=2)[0:1]
        d_dta = _dot_sel_l(triu_b, d_out) + _dot_sel_l(_ones_where(strict_tril), d_upd) + d_tot + d_dta_diag
        dxdt = dxd + dxw * w_e
        dxbc_ref[:, :SSD_DI] = dxdt * dt_e + dxe_ref[...]
        ddt_ref[...] = d_dta * a_row + _dot_sel_r(dxdt * x, reduce_, passes=2)
        _acc_store(da_ref, jnp.sum(d_dta * dt, axis=0, keepdims=True), i == 0)

    rev = lambda i: (nc - 1 - i, 0)
    return pl.pallas_call(
        body, grid=(nc,),
        in_specs=[pl.BlockSpec((SSD_L, CONV_DIM), rev), pl.BlockSpec((SSD_L, SSD_H), rev),
                  pl.BlockSpec((SSD_H, SSD_L), lambda i: (0, nc - 1 - i)), pl.BlockSpec((1, SSD_H), lambda i: (0, 0)),
                  pl.BlockSpec((SSD_H, 1), lambda i: (0, 0)),
                  pl.BlockSpec((1, SSD_N, SSD_DI), lambda i: (nc - 1 - i, 0, 0)),
                  pl.BlockSpec((SSD_L, SSD_DI), rev), pl.BlockSpec((SSD_L, SSD_DI), rev)],
        out_specs=[pl.BlockSpec((SSD_L, CONV_DIM), rev), pl.BlockSpec((SSD_L, SSD_H), rev),
                   pl.BlockSpec((1, SSD_H), lambda i: (0, 0))],
        out_shape=[jax.ShapeDtypeStruct((s, CONV_DIM), F32), jax.ShapeDtypeStruct((s, SSD_H), F32),
                   jax.ShapeDtypeStruct((1, SSD_H), F32)],
        scratch_shapes=[pltpu.VMEM((SSD_N, SSD_DI), F32), pltpu.VMEM((SSD_L, SSD_DI), F32),
                        pltpu.VMEM((SSD_L, SSD_DI), F32), pltpu.VMEM((SSD_L, SSD_DI), F32)],
        compiler_params=_cparams("arbitrary"), name="ssd_bwd",
    )(xbc, dt, dt_t, a, a_t, states, dy, dx_extra)


HBM_SPEC = pl.BlockSpec(memory_space=pltpu.HBM)
N_PEERS = N_DEV - 1


def _flip(v, f):
    return 1 - v if f else v


def _all_gather(shard):
    rows, c = shard.shape

    def body(x_ref, out_ref, send_sems, recv_sems, local_sem):
        x, y, cc = lax.axis_index("x"), lax.axis_index("y"), lax.axis_index("c")
        me, sibling = (x, y, cc), (x, y, 1 - cc)
        chips = [(1 - x, y), (x, 1 - y), (1 - x, 1 - y)]

        def slot(px, py, pc):
            return out_ref.at[4 * px + 2 * py + pc]

        def copy(k, block, to, src=None):
            return pltpu.make_async_remote_copy(
                src_ref=slot(*block) if src is None else src, dst_ref=slot(*block),
                send_sem=send_sems.at[k], recv_sem=recv_sems.at[k],
                device_id=to, device_id_type=pl.DeviceIdType.MESH)

        mine = pltpu.make_async_copy(x_ref, slot(*me), local_sem)
        mine.start()
        first = [copy(0, me, sibling, src=x_ref)]
        first += [copy(1 + j, me, (*chip, cc), src=x_ref) for j, chip in enumerate(chips)]
        for cp in first:
            cp.start()
        passed = [copy(4 + j, (*chip, cc), sibling) for j, chip in enumerate(chips)]
        for j, chip in enumerate(chips):
            copy(1 + j, (*chip, cc), me).wait_recv()
            passed[j].start()
        copy(0, sibling, me).wait_recv()
        for j, chip in enumerate(chips):
            copy(4 + j, (*chip, 1 - cc), me).wait_recv()
        for cp in first + passed:
            cp.wait_send()
        mine.wait()

    return pl.pallas_call(
        body, out_shape=jax.ShapeDtypeStruct((N_DEV, rows, c), shard.dtype),
        in_specs=[HBM_SPEC], out_specs=HBM_SPEC,
        scratch_shapes=[pltpu.SemaphoreType.DMA((N_PEERS,)), pltpu.SemaphoreType.DMA((N_PEERS,)), pltpu.SemaphoreType.DMA(())],
        name="all_gather",
    )(shard)


def _peer_copies(src_ref, out_ref, sems, gather, phase):
    send_sems, recv_sems, local_sem = sems
    x, y, cc = lax.axis_index("x"), lax.axis_index("y"), lax.axis_index("c")
    me = 4 * x + 2 * y + cc
    mine = pltpu.make_async_copy(src_ref if gather else src_ref.at[me], out_ref.at[me], local_sem)
    copies = []
    for k in range(1, N_DEV):
        px, py, pc = _flip(x, k & 4), _flip(y, k & 2), _flip(cc, k & 1)
        peer = 4 * px + 2 * py + pc
        src = src_ref if gather else src_ref.at[peer]
        copies.append((
            pltpu.make_async_remote_copy(
                src_ref=src, dst_ref=out_ref.at[me], send_sem=send_sems.at[k - 1], recv_sem=recv_sems.at[k - 1],
                device_id=(px, py, pc), device_id_type=pl.DeviceIdType.MESH),
            pltpu.make_async_remote_copy(
                src_ref=src, dst_ref=out_ref.at[peer], send_sem=send_sems.at[k - 1], recv_sem=recv_sems.at[k - 1],
                device_id=(px, py, pc), device_id_type=pl.DeviceIdType.MESH)))
    if phase == "start":
        mine.start()
        for send, _ in copies:
            send.start()
    else:
        for _, landed in copies:
            landed.wait_recv()
        for send, _ in copies:
            send.wait_send()
        mine.wait()


def _comm_scratch():
    return [pltpu.SemaphoreType.DMA((N_PEERS,)), pltpu.SemaphoreType.DMA((N_PEERS,)), pltpu.SemaphoreType.DMA(())]


def _gather_rider(shard):
    return dict(src=shard, out=jax.ShapeDtypeStruct((N_DEV,) + shard.shape, shard.dtype), gather=True)


def _exchange_rider(blocks):
    return dict(src=blocks, out=jax.ShapeDtypeStruct(blocks.shape, blocks.dtype), gather=False)


def _exchange_blocks(blocks):
    def body(g_ref, out_ref, *sems):
        _peer_copies(g_ref, out_ref, sems, False, "start")
        _peer_copies(g_ref, out_ref, sems, False, "finish")

    return pl.pallas_call(
        body, out_shape=jax.ShapeDtypeStruct(blocks.shape, blocks.dtype),
        in_specs=[HBM_SPEC], out_specs=HBM_SPEC, scratch_shapes=_comm_scratch(), name="exchange_blocks",
    )(blocks)


BIG = [
    ("ffn1_w13", (D_MODEL, 2 * D_FF), 1), ("ffn1_w2", (D_FF, D_MODEL), 0),
    ("w_ssd_out", (SSD_DI, D_MODEL), 0), ("w_uq", (Q_LORA, MLA_H * QK), 1), ("w_ukv", (KV_LORA, MLA_H * (NOPE + VDIM)), 1),
    ("w_mla_out", (MLA_H * VDIM, D_MODEL), 0), ("w_o", (D_MODEL, D_MODEL), 0),
    ("ffn2_w13", (D_MODEL, 2 * D_FF), 1), ("ffn2_w2", (D_FF, D_MODEL), 0), ("w_in", (D_MODEL, D_IN), 1),
]
assert all(_r % 16 == 0 for _r in [_f[0] * _f[1] // N_DEV // PACK_COLS for _, _f, _ in BIG[:-1]])
SMALL = [
    ("ln_ffn1", D_MODEL), ("ln_mix", D_MODEL), ("conv_b", CONV_DIM), ("dt_bias", SSD_H), ("a_log", SSD_H), ("d_skip", SSD_H),
    ("ssd_norm", SSD_DI), ("q_lora_norm", Q_LORA), ("kv_lora_norm", KV_LORA), ("q_norm", QK), ("k_norm", QK), ("ln_ffn2", D_MODEL),
]


def _shard_shape(full, axis):
    k, n = full
    return (k // N_DEV, n) if axis == 0 else (k, n // N_DEV)


def _shard_rows(full):
    return full[0] * full[1] // N_DEV // PACK_COLS


LAYER_ROWS = sum(_shard_rows(f) for _, f, _ in BIG)
LAYER_ROWS_PAD = -(-LAYER_ROWS // 256) * 256


def _pack_shards(shards):
    parts = [(shards[name] if axis == 0 else shards[name].T).reshape(-1, PACK_COLS) for name, _, axis in BIG]
    pad = LAYER_ROWS_PAD - LAYER_ROWS
    if pad:
        parts.append(jnp.zeros((pad, PACK_COLS), parts[0].dtype))
    return jnp.concatenate(parts, axis=0)


def _unpack_shards(packed):
    out, r = {}, 0
    for name, full, axis in BIG:
        n = _shard_rows(full)
        k, c = _shard_shape(full, axis)
        blk = packed[r:r + n]
        out[name] = blk.reshape(k, c) if axis == 0 else blk.reshape(c, k).T
        r += n
    return out


def _working_shape(full, axis):
    return full if axis == 0 else full[::-1]


def _unpack_gathered(gathered):
    out, r = {}, 0
    for name, full, axis in BIG:
        n = _shard_rows(full)
        out[name] = gathered[:, r:r + n].reshape(_working_shape(full, axis))
        r += n
    return out


def _pack_full_grads(grads):
    parts = [grads[name].reshape(N_DEV, -1, PACK_COLS) for name, _, _ in BIG]
    pad = LAYER_ROWS_PAD - LAYER_ROWS
    if pad:
        parts.append(jnp.zeros((N_DEV, pad, PACK_COLS), parts[0].dtype))
    return jnp.concatenate(parts, axis=1)


SMALL_COLS = sum(n for _, n in SMALL) + CONV_K * CONV_DIM
SMALL_ROWS = -(-(DEPTH * SMALL_COLS) // (8 * PACK_COLS)) * 8


def _pack_small(vals, conv_w):
    flat = jnp.concatenate([vals[name] for name, _ in SMALL] + [conv_w.reshape(DEPTH, -1)], axis=1).reshape(-1)
    flat = jnp.concatenate([flat, jnp.zeros((SMALL_ROWS * PACK_COLS - flat.shape[0],), F32)])
    return flat.reshape(SMALL_ROWS, PACK_COLS)


def _unpack_small(packed):
    flat = packed.reshape(-1)[:DEPTH * SMALL_COLS].reshape(DEPTH, SMALL_COLS)
    out, c = {}, 0
    for name, n in SMALL:
        out[name] = flat[:, c:c + n]
        c += n
    return out, flat[:, c:].reshape(DEPTH, CONV_K, CONV_DIM)


_IN_OFFS = [sum(IN_SPLIT[:i]) for i in range(len(IN_SPLIT) + 1)]


def _arrange_w_in(w_t):
    z, xbc, dt, cq, ckv, kr, gates = [w_t[_IN_OFFS[i]:_IN_OFFS[i + 1]] for i in range(len(IN_SPLIT))]
    pad = jnp.zeros((LANE - ROPE - SSD_H, w_t.shape[1]), w_t.dtype)
    return jnp.concatenate([z, gates, xbc, cq, ckv, kr, dt, pad], axis=0)


def _restore_w_in(g):
    z, gates, xbc = g[PROJ_Z:PROJ_GATES], g[PROJ_GATES:PROJ_XBC], g[PROJ_XBC:PROJ_CQ]
    cq, ckv = g[PROJ_CQ:PROJ_CKV], g[PROJ_CKV:PROJ_LAST]
    kr, dt = g[PROJ_LAST:PROJ_LAST + ROPE], g[PROJ_LAST + ROPE:PROJ_LAST + ROPE + SSD_H]
    return jnp.concatenate([z, xbc, dt, cq, ckv, kr, gates], axis=0)


def _pad_heads(w_t):
    k = w_t.shape[1]
    return jnp.pad(w_t.reshape(MLA_H, QK, k), ((0, 0), (0, HEAD_COLS - QK), (0, 0))).reshape(MLA_H * HEAD_COLS, k)


def _unpad_heads(g):
    k = g.shape[1]
    return g.reshape(MLA_H, HEAD_COLS, k)[:, :QK].reshape(MLA_H * QK, k)


def _row(v):
    return v.reshape(1, -1)


def _head_gain(g):
    return jnp.pad(g, (0, HEAD_COLS - QK)).reshape(1, HEAD_COLS)


def _ffn_fwd(h, ln, w13_t, w2, name):
    n = _row_fwd(_f_rmsnorm, [h], [_row(ln)], [BF16], name + "_fwd")[0]
    gu = _mm(n, w13_t, tb=True, out_dtype=BF16)
    act = _row_fwd(_f_swiglu, [gu], [], [BF16], "swiglu_fwd")[0]
    return _mm(act, w2, alpha=0.5, res=h), (h, n, gu, act)


def _ffn_bwd(dh_out, saved, ln, w13_t, w2, name):
    h, n, gu, act = saved
    d_act = _mm(dh_out, w2, tb=True, out_dtype=BF16, alpha=0.5)
    d_w2 = _mm(act, dh_out, ta=True, out_dtype=BF16, alpha=0.5)
    d_gu = _row_bwd(_f_swiglu, [gu], [], [d_act], [BF16], "swiglu_bwd", bwd=_b_swiglu)[0][0]
    d_n = _mm(d_gu, w13_t, out_dtype=BF16)
    d_w13_t = _mm(d_gu, n, ta=True, out_dtype=BF16)
    (dh,), (d_ln,) = _row_bwd(_f_rmsnorm, [h], [_row(ln)], [d_n], [F32], name + "_bwd", add={0: dh_out})
    return dh, d_w13_t, d_w2, d_ln[0]


def _mixer_fwd(h, big, small, conv_w, cs, sn, rider=None):
    s = h.shape[0]
    u = _row_fwd(_f_rmsnorm, [h], [_row(small["ln_mix"])], [BF16], "ln_mix_fwd")[0]
    proj = _mm(u, big["w_in"], tb=True)
    xbc = _conv_fwd_call(proj, PROJ_XBC, conv_w, _row(small["conv_b"]))
    dt_in = proj[:, PROJ_LAST + ROPE:PROJ_LAST + ROPE + SSD_H] + small["dt_bias"][None, :]
    dt = jax.nn.softplus(dt_in)
    a = -jnp.exp(small["a_log"])[None, :]
    y_scan, states = _ssd_fwd_call(xbc, dt, a)
    dsk = _row(jnp.repeat(small["d_skip"], SSD_P))
    gn_in = [y_scan, _win(xbc, 0, SSD_DI), _win(proj, PROJ_Z, SSD_DI)]
    yn = _row_fwd(_f_gated_norm, gn_in, [dsk, _row(small["ssd_norm"])], [BF16], "gated_norm_fwd")[0]
    y_ssd = _mm(yn, big["w_ssd_out"])
    qn = _row_fwd(_f_rmsnorm, [_win(proj, PROJ_CQ, Q_LORA)], [_row(small["q_lora_norm"])], [BF16], "q_lora_norm_fwd")[0]
    kvn = _row_fwd(_f_rmsnorm, [_win(proj, PROJ_CKV, KV_LORA)], [_row(small["kv_lora_norm"])], [BF16], "kv_lora_norm_fwd")[0]
    q = _mm(qn, big["w_uq"], tb=True)
    kv = _mm(kvn, big["w_ukv"], tb=True)
    qh, kh, vh = _heads_fwd_call(q, kv, proj, cs, sn, _head_gain(small["q_norm"]), _head_gain(small["k_norm"]))
    o, lse, *carried = _attn_fwd_call(qh, kh, vh, rider)
    o_rows = jnp.transpose(o, (1, 0, 2)).reshape(s, MLA_H * VDIM)
    y_mla = _mm(o_rows, big["w_mla_out"])
    mg = _row_fwd(_f_merge, [_win(proj, PROJ_GATES, 2 * D_MODEL), y_ssd, y_mla], [], [BF16], "merge_fwd")[0]
    out = _mm(mg, big["w_o"], res=h)
    saved = (h, u, proj, xbc, dt_in, dt, a, y_scan, states, dsk, yn, y_ssd, qn, kvn, q, kv, qh, kh, vh, o, lse, o_rows, y_mla, mg)
    return out, saved, (carried[0] if carried else None)


def _mixer_bwd(dh_out, saved, big, small, conv_w, cs, sn, rider=None):
    (h, u, proj, xbc, dt_in, dt, a, y_scan, states, dsk, yn, y_ssd, qn, kvn, q, kv, qh, kh, vh, o, lse, o_rows, y_mla, mg) = saved
    s = h.shape[0]
    d_big, d_small = {}, {}
    d_mg = _mm(dh_out, big["w_o"], tb=True, out_dtype=BF16)
    d_big["w_o"] = _mm(mg, dh_out, ta=True, out_dtype=BF16)
    merge_in = [_win(proj, PROJ_GATES, 2 * D_MODEL), y_ssd, y_mla]
    (d_gates, d_y_ssd, d_y_mla), _ = _row_bwd(_f_merge, merge_in, [], [d_mg], [BF16, BF16, BF16], "merge_bwd", bwd=_b_merge)
    d_o_rows = _mm(d_y_mla, big["w_mla_out"], tb=True, out_dtype=BF16)
    d_big["w_mla_out"] = _mm(o_rows, d_y_mla, ta=True, out_dtype=BF16)
    d_o = jnp.transpose(d_o_rows.reshape(s, MLA_H, VDIM), (1, 0, 2))
    delta = _attn_delta_call(o, d_o)
    *d_heads, carried = list(_attn_bwd_call(qh, kh, vh, d_o, lse.reshape(MLA_H, 1, s), delta.reshape(MLA_H, 1, s), rider)) + ([None] if rider is None else [])
    d_q, d_kv, d_kr, d_qg, d_kg = _heads_bwd_call(
        q, kv, proj, cs, sn, _head_gain(small["q_norm"]), _head_gain(small["k_norm"]), *d_heads)
    d_small["q_norm"], d_small["k_norm"] = d_qg[0, :QK], d_kg[0, :QK]
    d_qn = _mm(d_q, big["w_uq"], out_dtype=BF16)
    d_big["w_uq"] = _mm(d_q, qn, ta=True, out_dtype=BF16)
    d_kvn = _mm(d_kv, big["w_ukv"], out_dtype=BF16)
    d_big["w_ukv"] = _mm(d_kv, kvn, ta=True, out_dtype=BF16)
    (d_cq,), (d_g,) = _row_bwd(_f_rmsnorm, [_win(proj, PROJ_CQ, Q_LORA)], [_row(small["q_lora_norm"])], [d_qn], [BF16], "q_lora_norm_bwd")
    d_small["q_lora_norm"] = d_g[0]
    (d_ckv,), (d_g,) = _row_bwd(_f_rmsnorm, [_win(proj, PROJ_CKV, KV_LORA)], [_row(small["kv_lora_norm"])], [d_kvn], [BF16], "kv_lora_norm_bwd")
    d_small["kv_lora_norm"] = d_g[0]
    d_yn = _mm(d_y_ssd, big["w_ssd_out"], tb=True, out_dtype=BF16)
    d_big["w_ssd_out"] = _mm(yn, d_y_ssd, ta=True, out_dtype=BF16)
    gn_in = [y_scan, _win(xbc, 0, SSD_DI), _win(proj, PROJ_Z, SSD_DI)]
    (d_y_scan, d_xs, d_z), (d_dsk, d_g) = _row_bwd(
        _f_gated_norm, gn_in, [dsk, _row(small["ssd_norm"])], [d_yn], [F32, F32, BF16], "gated_norm_bwd")
    d_small["ssd_norm"] = d_g[0]
    d_small["d_skip"] = jnp.sum(d_dsk.reshape(SSD_H, SSD_P), axis=1)
    d_xbc_act, d_dt, d_a = _ssd_bwd_call(xbc, dt, a, states, d_y_scan, d_xs)
    d_xbc, d_conv_w, d_conv_b = _conv_bwd_call(proj, PROJ_XBC, conv_w, _row(small["conv_b"]), d_xbc_act)
    d_small["conv_b"] = d_conv_b[0]
    d_dt_in = d_dt * jax.nn.sigmoid(dt_in)
    d_small["dt_bias"] = jnp.sum(d_dt_in, axis=0)
    d_small["a_log"] = d_a[0] * a[0]
    d_last = (d_kr + jnp.pad(d_dt_in, ((0, 0), (ROPE, LANE - ROPE - SSD_H)))).astype(BF16)
    d_proj = jnp.concatenate([d_z, d_gates, d_xbc, d_cq, d_ckv, d_last], axis=1)
    d_u = _mm(d_proj, big["w_in"], out_dtype=BF16)
    d_big["w_in"] = _mm(d_proj, u, ta=True, out_dtype=BF16)
    (dh,), (d_ln,) = _row_bwd(_f_rmsnorm, [h], [_row(small["ln_mix"])], [d_u], [F32], "ln_mix_bwd", add={0: dh_out})
    d_small["ln_mix"] = d_ln[0]
    return dh, d_big, d_small, d_conv_w, carried


def _prepare_big(b):
    return dict(b, w_in=_arrange_w_in(b["w_in"]), w_uq=_pad_heads(b["w_uq"]))


def _local_step(x, positions, target, big, small, conv_w, packed_last=None):
    inv = 1.0 / (ROPE_THETA ** (jnp.arange(0, ROPE, 2, dtype=F32) / ROPE))
    ang = positions.astype(F32)[:, None] * inv
    cos, sin = jnp.cos(ang), jnp.sin(ang)
    no_lanes = jnp.zeros((x.shape[0], LANE - ROPE), F32)
    cs = jnp.concatenate([cos, cos, no_lanes], axis=1)
    sn = jnp.concatenate([-sin, sin, no_lanes], axis=1)
    carrier = DEPTH - 2 if packed_last is not None else None
    big = [None if b is None else _prepare_big(b) for b in big]
    layer_small = [{k: v[l] for k, v in small.items()} for l in range(DEPTH)]

    h, saved = x, []
    for l in range(DEPTH):
        b, sm = big[l], layer_small[l]
        h, s1 = _ffn_fwd(h, sm["ln_ffn1"], b["ffn1_w13"], b["ffn1_w2"], "ln_ffn1")
        h, s2, gathered = _mixer_fwd(h, b, sm, conv_w[l], cs, sn, _gather_rider(packed_last) if l == carrier else None)
        if gathered is not None:
            big[l + 1] = _prepare_big(_unpack_gathered(gathered))
        h, s3 = _ffn_fwd(h, sm["ln_ffn2"], b["ffn2_w13"], b["ffn2_w2"], "ln_ffn2")
        saved.append((s1, s2, s3))
    loss, dh = _loss_and_grad(h, target)

    d_big, d_small, d_conv_w = [None] * DEPTH, [None] * DEPTH, [None] * DEPTH
    for l in reversed(range(DEPTH)):
        b, sm = big[l], layer_small[l]
        s1, s2, s3 = saved[l]
        dh, d_w13_2, d_w2_2, d_ln2 = _ffn_bwd(dh, s3, sm["ln_ffn2"], b["ffn2_w13"], b["ffn2_w2"], "ln_ffn2")
        rider = _exchange_rider(_pack_full_grads(d_big[l + 1])) if l == carrier else None
        dh, db, ds, d_conv_w[l], received = _mixer_bwd(dh, s2, b, sm, conv_w[l], cs, sn, rider)
        if received is not None:
            d_big[l + 1] = received
        dh, d_w13_1, d_w2_1, d_ln1 = _ffn_bwd(dh, s1, sm["ln_ffn1"], b["ffn1_w13"], b["ffn1_w2"], "ln_ffn1")
        db.update(ffn1_w13=d_w13_1, ffn1_w2=d_w2_1, ffn2_w13=d_w13_2, ffn2_w2=d_w2_2,
                  w_in=_restore_w_in(db["w_in"]), w_uq=_unpad_heads(db["w_uq"]))
        ds.update(ln_ffn1=d_ln1, ln_ffn2=d_ln2)
        d_big[l], d_small[l] = db, ds
    d_small = {name: jnp.stack([d_small[l][name] for l in range(DEPTH)]) for name, _ in SMALL}
    return loss, dh, d_big, d_small, jnp.stack(d_conv_w)


def _step(args):
    dev = 4 * lax.axis_index("x") + 2 * lax.axis_index("y") + lax.axis_index("c")
    x, positions, target = args["x"][0], args["positions"][0], args["loss_target"][0]

    packed = [_pack_shards({name: args[name][l].astype(BF16) for name, _, _ in BIG}) for l in range(DEPTH)]
    big = [_unpack_gathered(_all_gather(packed[l])) for l in range(DEPTH - 1)] + [None]
    cw = args["conv_w"]
    cw_cols = cw.shape[-1]
    cw_rows = -(-cw.size // (8 * PACK_COLS)) * 8
    cw_flat = jnp.concatenate([cw.reshape(-1), jnp.zeros((cw_rows * PACK_COLS - cw.size,), F32)]).reshape(cw_rows, PACK_COLS)
    cw_all = _all_gather(cw_flat).reshape(N_DEV, -1)[:, :cw.size].reshape(N_DEV, DEPTH, CONV_K, cw_cols)
    conv_w = jnp.transpose(cw_all, (1, 2, 0, 3)).reshape(DEPTH, CONV_K, CONV_DIM)
    small = {name: args[name] for name, _ in SMALL}

    loss, dx, d_big, d_small, d_conv_w = _local_step(x, positions, target, big, small, conv_w, packed_last=packed[-1])
    loss = lax.psum(loss, MESH_AXES)

    out = {"loss": loss, "grad_x": dx[None]}

    grads = {name: [] for name, _, _ in BIG}
    for l in range(DEPTH):
        received = d_big[l] if l == DEPTH - 1 else _exchange_blocks(_pack_full_grads(d_big[l]))
        summed = _sum_blocks(received)
        for name, g in _unpack_shards(summed).items():
            grads[name].append(g)
    flat = lambda t: t.reshape(-1, t.shape[-1])
    for name, _, _ in BIG:
        g = jnp.stack(grads[name])
        w = args[name]
        delta, m2, v2 = _adam(flat(w), flat(g), flat(args["m_" + name]), flat(args["v_" + name]))
        out["grad_" + name] = g
        out["delta_" + name] = delta.reshape(w.shape)
        out["new_m_" + name] = m2.reshape(w.shape)
        out["new_v_" + name] = v2.reshape(w.shape)

    total = _sum_blocks(_all_gather(_pack_small(d_small, d_conv_w)))
    g_conv_w = _unpack_small(total)[1]
    zeros_cw = jnp.zeros((DEPTH, CONV_K, CONV_DIM), F32)
    delta, m2, v2 = _adam(_pack_small(small, zeros_cw), total,
                          _pack_small({name: args["m_" + name] for name, _ in SMALL}, zeros_cw),
                          _pack_small({name: args["v_" + name] for name, _ in SMALL}, zeros_cw))
    for kind, packed in (("grad_", total), ("delta_", delta), ("new_m_", m2), ("new_v_", v2)):
        for name, val in _unpack_small(packed)[0].items():
            out[kind + name] = val
    g_cw = lax.dynamic_slice_in_dim(g_conv_w, dev * cw_cols, cw_cols, axis=2)
    delta, m2, v2 = _adam(flat(cw), flat(g_cw), flat(args["m_conv_w"]), flat(args["v_conv_w"]))
    out["grad_conv_w"] = g_cw
    out["delta_conv_w"] = delta.reshape(cw.shape)
    out["new_m_conv_w"] = m2.reshape(cw.shape)
    out["new_v_conv_w"] = v2.reshape(cw.shape)
    return out


WEIGHTS = ["ln_ffn1", "ffn1_w13", "ffn1_w2", "ln_mix", "w_in", "conv_w", "conv_b", "dt_bias", "a_log", "d_skip", "ssd_norm",
           "w_ssd_out", "q_lora_norm", "w_uq", "kv_lora_norm", "w_ukv", "q_norm", "k_norm", "w_mla_out", "w_o", "ln_ffn2",
           "ffn2_w13", "ffn2_w2"]
ARG_NAMES = (["x", "positions"] + WEIGHTS + ["loss_target"] + ["m_" + n for n in WEIGHTS] + ["v_" + n for n in WEIGHTS])


def kernel(x, positions, ln_ffn1, ffn1_w13, ffn1_w2, ln_mix, w_in, conv_w, conv_b, dt_bias, a_log, d_skip, ssd_norm, w_ssd_out, q_lora_norm, w_uq, kv_lora_norm, w_ukv, q_norm, k_norm, w_mla_out, w_o, ln_ffn2, ffn2_w13, ffn2_w2, loss_target, m_ln_ffn1, m_ffn1_w13, m_ffn1_w2, m_ln_mix, m_w_in, m_conv_w, m_conv_b, m_dt_bias, m_a_log, m_d_skip, m_ssd_norm, m_w_ssd_out, m_q_lora_norm, m_w_uq, m_kv_lora_norm, m_w_ukv, m_q_norm, m_k_norm, m_w_mla_out, m_w_o, m_ln_ffn2, m_ffn2_w13, m_ffn2_w2, v_ln_ffn1, v_ffn1_w13, v_ffn1_w2, v_ln_mix, v_w_in, v_conv_w, v_conv_b, v_dt_bias, v_a_log, v_d_skip, v_ssd_norm, v_w_ssd_out, v_q_lora_norm, v_w_uq, v_kv_lora_norm, v_w_ukv, v_q_norm, v_k_norm, v_w_mla_out, v_w_o, v_ln_ffn2, v_ffn2_w13, v_ffn2_w2):
    vals = (x, positions, ln_ffn1, ffn1_w13, ffn1_w2, ln_mix, w_in, conv_w, conv_b, dt_bias, a_log, d_skip, ssd_norm, w_ssd_out, q_lora_norm, w_uq, kv_lora_norm, w_ukv, q_norm, k_norm, w_mla_out, w_o, ln_ffn2, ffn2_w13, ffn2_w2, loss_target, m_ln_ffn1, m_ffn1_w13, m_ffn1_w2, m_ln_mix, m_w_in, m_conv_w, m_conv_b, m_dt_bias, m_a_log, m_d_skip, m_ssd_norm, m_w_ssd_out, m_q_lora_norm, m_w_uq, m_kv_lora_norm, m_w_ukv, m_q_norm, m_k_norm, m_w_mla_out, m_w_o, m_ln_ffn2, m_ffn2_w13, m_ffn2_w2, v_ln_ffn1, v_ffn1_w13, v_ffn1_w2, v_ln_mix, v_w_in, v_conv_w, v_conv_b, v_dt_bias, v_a_log, v_d_skip, v_ssd_norm, v_w_ssd_out, v_q_lora_norm, v_w_uq, v_kv_lora_norm, v_w_ukv, v_q_norm, v_k_norm, v_w_mla_out, v_w_o, v_ln_ffn2, v_ffn2_w13, v_ffn2_w2)
    out = _step(dict(zip(ARG_NAMES, vals)))
    order = ["loss", "grad_x"] + [k + n for k in ("grad_", "delta_", "new_m_", "new_v_") for n in WEIGHTS]
    return tuple(out[n] for n in order)
```

```python
import jax
import jax.numpy as jnp
from jax import lax
from jax.experimental import pallas as pl
from jax.experimental.pallas import tpu as pltpu

F32 = jnp.float32
BF16 = jnp.bfloat16

D_MODEL = 1024
D_FF = 2816
DEPTH = 2
SSD_DI = 2048
SSD_P = 64
SSD_H = 32
SSD_G = 4
SSD_HPG = 8
SSD_N = 128
SSD_L = 128
CONV_K = 4
CONV_DIM = 3072
MLA_H = 8
Q_LORA = 512
KV_LORA = 256
NOPE = 128
ROPE = 64
VDIM = 128
QK = 192
ROPE_THETA = 10000.0
EPS = 1e-6
IN_SPLIT = (SSD_DI, CONV_DIM, SSD_H, Q_LORA, KV_LORA, ROPE, 2 * D_MODEL)
D_IN = sum(IN_SPLIT)
N_DEV = 8
LANE = 128
PACK_COLS = 1024

PROJ_Z = 0
PROJ_GATES = PROJ_Z + SSD_DI
PROJ_XBC = PROJ_GATES + 2 * D_MODEL
PROJ_CQ = PROJ_XBC + CONV_DIM
PROJ_CKV = PROJ_CQ + Q_LORA
PROJ_LAST = PROJ_CKV + KV_LORA
D_IN_PAD = PROJ_LAST + LANE

ADAM_LR = 0.001
ADAM_B1 = 0.9
ADAM_B2 = 0.999
ADAM_EPS = 1e-08
ADAM_WD = 0.01
ADAM_STEP = 10

VMEM_LIMIT = 48 * 1024 * 1024
ROW_IO_BUDGET = 8 * 1024 * 1024
ROW_STRIP_ALIGN = 16
ROW_STRIP_ELEMS = 16 * 1024
NEG = -1e30

MESH_AXES = ("x", "y", "c")


def _cparams(*sem):
    return pltpu.CompilerParams(dimension_semantics=sem, vmem_limit_bytes=VMEM_LIMIT)


def _pick_tile(n, target, align):
    if n <= target:
        return n
    best = None
    for t in range(align, target + 1, align):
        if n % t == 0:
            best = t
    assert best is not None, (n, target, align)
    return best


def _acc_store(ref, val, first):
    @pl.when(first)
    def _():
        ref[...] = val

    @pl.when(jnp.logical_not(first))
    def _():
        ref[...] += val


def _win(arr, start, width):
    assert start % width == 0, (start, width)
    return (arr, start, width)


def _operand(entry):
    if isinstance(entry, tuple):
        arr, start, width = entry
        return arr, width, start // width
    return entry, entry.shape[1], 0


def _row_tile(rows, bytes_per_row):
    if rows <= 16:
        return rows
    t = 1024
    while t > 16 and (t * bytes_per_row > ROW_IO_BUDGET or rows % t):
        t //= 2
    assert rows % t == 0, (rows, t)
    return t


def _rowwise_call(fn, tiled, params, outs, accs, name):
    ops = [_operand(e) for e in tiled]
    rows = ops[0][0].shape[0]
    per_row = sum(w * a.dtype.itemsize for a, w, _ in ops) + sum(c * jnp.dtype(d).itemsize for c, d in outs)
    tile = _row_tile(rows, per_row)
    n_t = len(tiled)
    n_in = n_t + len(params)
    n_o = len(outs)
    widest = max([w for _, w, _ in ops] + [c for c, _ in outs])
    strip = max(ROW_STRIP_ALIGN, ROW_STRIP_ELEMS // widest // ROW_STRIP_ALIGN * ROW_STRIP_ALIGN)
    strip = strip if tile % strip == 0 else tile

    def body(*refs):
        out_refs = refs[n_in:n_in + n_o]

        def step(i, sums):
            r0 = pl.multiple_of(i * strip, strip)
            vals = [r[pl.ds(r0, strip), :] for r in refs[:n_t]] + [r[...] for r in refs[n_t:n_in]]
            t_out, a_out = fn(*vals)
            for r, v in zip(out_refs, t_out):
                r[pl.ds(r0, strip), :] = v.astype(r.dtype)
            return tuple(s_ + v.astype(F32) for s_, v in zip(sums, a_out))

        sums = lax.fori_loop(0, tile // strip, step, tuple(jnp.zeros(s_, F32) for s_ in accs))
        first = pl.program_id(0) == 0
        for r, v in zip(refs[n_in + n_o:], sums):
            _acc_store(r, v, first)

    def tiled_spec(width, blk):
        return pl.BlockSpec((tile, width), lambda i: (i, blk))

    in_specs = [tiled_spec(w, blk) for _, w, blk in ops]
    in_specs += [pl.BlockSpec(p.shape, lambda i: (0, 0)) for p in params]
    out_specs = [tiled_spec(c, 0) for c, _ in outs]
    out_specs += [pl.BlockSpec(s, lambda i: (0, 0)) for s in accs]
    out_shape = [jax.ShapeDtypeStruct((rows, c), d) for c, d in outs]
    out_shape += [jax.ShapeDtypeStruct(s, F32) for s in accs]
    return pl.pallas_call(
        body, grid=(rows // tile,), in_specs=in_specs, out_specs=out_specs, out_shape=out_shape,
        compiler_params=_cparams("arbitrary"), name=name,
    )(*[a for a, _, _ in ops], *params)


def _to_f32(vals):
    return [v.astype(F32) for v in vals]


def _row_fwd(f, tiled, params, out_dtypes, name):
    ops = [_operand(e) for e in tiled]
    rows = ops[0][0].shape[0]
    shapes = jax.eval_shape(f, *[jax.ShapeDtypeStruct((rows, w), F32) for _, w, _ in ops],
                            *[jax.ShapeDtypeStruct(p.shape, F32) for p in params])
    outs = [(s.shape[1], d) for s, d in zip(shapes, out_dtypes)]
    return _rowwise_call(lambda *v: (f(*_to_f32(v)), ()), tiled, params, outs, [], name)


def _row_bwd(f, tiled, params, gs, d_dtypes, name, bwd=None, add=None):
    n_t, n_g = len(tiled), len(gs)
    adds = sorted((add or {}).items())
    n_a = len(adds)

    def fn(*vals):
        vals = _to_f32(vals)
        prim = vals[:n_t] + vals[n_t + n_g + n_a:]
        g = tuple(vals[n_t:n_t + n_g])
        if bwd is not None:
            d_t, d_p = bwd(*prim, *g)
        else:
            _, vjp = jax.vjp(f, *prim)
            cts = vjp(g)
            d_t, d_p = cts[:n_t], cts[n_t:]
        d_t = list(d_t)
        for (idx, _), extra in zip(adds, vals[n_t + n_g:n_t + n_g + n_a]):
            d_t[idx] = d_t[idx] + extra
        return tuple(d_t), tuple(d_p)

    outs = [(_operand(e)[1], d) for e, d in zip(tiled, d_dtypes)]
    accs = [p.shape for p in params]
    res = _rowwise_call(fn, list(tiled) + list(gs) + [a for _, a in adds], params, outs, accs, name)
    return res[:n_t], res[n_t:]


def _f_rmsnorm(x, g):
    return (x * lax.rsqrt(jnp.mean(x * x, axis=-1, keepdims=True) + EPS) * g,)


def _f_swiglu(gu):
    gate, up = gu[:, :D_FF], gu[:, D_FF:]
    return (gate * jax.nn.sigmoid(gate) * up,)


def _b_swiglu(gu, d):
    gate, up = gu[:, :D_FF], gu[:, D_FF:]
    s = jax.nn.sigmoid(gate)
    d_gate = d * up * s * (1.0 + gate * (1.0 - s))
    d_up = d * gate * s
    return (jnp.concatenate([d_gate, d_up], axis=1),), ()


def _f_gated_norm(ys, xs, z, dsk, g):
    t = (ys + xs * dsk) * (z * jax.nn.sigmoid(z))
    return (t * lax.rsqrt(jnp.mean(t * t, axis=-1, keepdims=True) + EPS) * g,)


def _f_merge(gates, ys, ym):
    s = jax.nn.sigmoid(gates)
    return (s[:, :D_MODEL] * ys + s[:, D_MODEL:] * ym,)


def _b_merge(gates, ys, ym, d):
    s = jax.nn.sigmoid(gates)
    s1, s2 = s[:, :D_MODEL], s[:, D_MODEL:]
    d_gates = jnp.concatenate([d * ys * s1 * (1.0 - s1), d * ym * s2 * (1.0 - s2)], axis=1)
    return (d_gates, d * s1, d * s2), ()


def _loss_and_grad(y, target):
    def fn(yv, tv):
        d = yv - tv
        return (d * (1.0 / D_MODEL),), (jnp.sum(d * d, axis=0, keepdims=True) * (0.5 / D_MODEL),)

    dy, part = _rowwise_call(fn, [y, target], [], [(D_MODEL, F32)], [(1, D_MODEL)], "loss")
    return jnp.sum(part), dy


def _adam(w, g, m, v):
    def fn(wv, gv, mv, vv):
        m2 = ADAM_B1 * mv + (1.0 - ADAM_B1) * gv
        v2 = ADAM_B2 * vv + (1.0 - ADAM_B2) * (gv * gv)
        m_hat = m2 / (1.0 - ADAM_B1 ** ADAM_STEP)
        v_hat = v2 / (1.0 - ADAM_B2 ** ADAM_STEP)
        delta = -ADAM_LR * (m_hat / (jnp.sqrt(v_hat) + ADAM_EPS) + ADAM_WD * wv)
        return (delta, m2, v2), ()

    c = w.shape[1]
    return _rowwise_call(fn, [w, g, m, v], [], [(c, F32)] * 3, [], "adamw")


def _sum_blocks(blocks):
    _, rows, c = blocks.shape
    tile = _row_tile(rows, N_DEV * c * blocks.dtype.itemsize + c * 4)

    def body(b_ref, o_ref):
        acc = b_ref[0].astype(F32)
        for i in range(1, N_DEV):
            acc = acc + b_ref[i].astype(F32)
        o_ref[...] = acc

    return pl.pallas_call(
        body, grid=(rows // tile,), in_specs=[pl.BlockSpec((N_DEV, tile, c), lambda i: (0, i, 0))],
        out_specs=pl.BlockSpec((tile, c), lambda i: (i, 0)), out_shape=jax.ShapeDtypeStruct((rows, c), F32),
        compiler_params=_cparams("arbitrary"), name="sum_blocks",
    )(blocks)


def _mm(a, b, ta=False, tb=False, out_dtype=F32, alpha=1.0, res=None):
    r_dim, p_dim = a.shape if ta else a.shape[::-1]
    r2, q_dim = b.shape[::-1] if tb else b.shape
    assert r_dim == r2, (a.shape, b.shape, ta, tb)
    tp = _pick_tile(p_dim, 512, LANE)
    if tp < 512 < p_dim:
        tp = _pick_tile(p_dim, 1536, LANE)
    tq = _pick_tile(q_dim, 1536, LANE)
    tr = _pick_tile(r_dim, 1536, LANE)
    nr = r_dim // tr
    dims = (((0 if ta else 1,), (1 if tb else 0,)), ((), ()))
    has_res = res is not None

    def body(*refs):
        a_ref, b_ref = refs[:2]
        res_ref = refs[2] if has_res else None
        o_ref = refs[2 + has_res]

        def finish(val):
            if alpha != 1.0:
                val = val * alpha
            if has_res:
                val = val + res_ref[...].astype(F32)
            o_ref[...] = val.astype(o_ref.dtype)

        part = lax.dot_general(a_ref[...].astype(BF16), b_ref[...].astype(BF16), dims, preferred_element_type=F32)
        if nr == 1:
            finish(part)
        else:
            acc_ref = refs[3 + has_res]
            k = pl.program_id(2)
            _acc_store(acc_ref, part, k == 0)

            @pl.when(k == nr - 1)
            def _():
                finish(acc_ref[...])

    a_spec = pl.BlockSpec((tr, tp), lambda j, i, k: (k, i)) if ta else pl.BlockSpec((tp, tr), lambda j, i, k: (i, k))
    b_spec = pl.BlockSpec((tq, tr), lambda j, i, k: (j, k)) if tb else pl.BlockSpec((tr, tq), lambda j, i, k: (k, j))
    o_spec = pl.BlockSpec((tp, tq), lambda j, i, k: (i, j))
    return pl.pallas_call(
        body, grid=(q_dim // tq, p_dim // tp, nr), in_specs=[a_spec, b_spec] + ([o_spec] if has_res else []),
        out_specs=o_spec, out_shape=jax.ShapeDtypeStruct((p_dim, q_dim), out_dtype),
        scratch_shapes=[pltpu.VMEM((tp, tq), F32)] if nr > 1 else [],
        compiler_params=_cparams("arbitrary", "arbitrary", "arbitrary"),
        name=f"mm_{'t' if ta else 'n'}{'t' if tb else 'n'}_{p_dim}x{r_dim}x{q_dim}",
    )(*([a, b] + ([res] if has_res else [])))


ATTN_SCALE = QK ** -0.5
LOG2E = 1.4426950408889634
ATTN_C = ATTN_SCALE * LOG2E


ATTN_HEADS = 2


def _attn_tile(s):
    return min(512, s)


def _causal_keep(t, keys_on_rows=False):
    row = lax.broadcasted_iota(jnp.int32, (t, t), 0)
    col = lax.broadcasted_iota(jnp.int32, (t, t), 1)
    return row <= col if keys_on_rows else col <= row


def _nt(a, b):
    return lax.dot_general(a, b, (((1,), (1,)), ((), ())), preferred_element_type=F32)


def _rider_phases(rider, src_ref, out_ref, sems, first, last):
    @pl.when(first)
    def _():
        _peer_copies(src_ref, out_ref, sems, rider["gather"], "start")

    def finish():
        @pl.when(last)
        def _():
            _peer_copies(src_ref, out_ref, sems, rider["gather"], "finish")

    return finish


def _attn_fwd_call(q, k, v, rider=None):
    nh, s, _ = q.shape
    t = _attn_tile(s)
    nb = s // t
    hp = ATTN_HEADS
    n_r = 0 if rider is None else 1

    def body(*refs):
        q_ref, k_ref, v_ref = refs[:3]
        o_ref, lse_ref = refs[3 + n_r:5 + n_r]
        qi = pl.program_id(1)
        finish = None
        if rider is not None:
            h = pl.program_id(0)
            finish = _rider_phases(rider, refs[3], refs[5 + n_r], refs[6 + n_r:],
                                   jnp.logical_and(h == 0, qi == 0), jnp.logical_and(h == nh // hp - 1, qi == nb - 1))
        qs = [q_ref[i] for i in range(hp)]

        def block(kb, carries, diagonal, width=1):
            start = pl.multiple_of(kb * t, t)
            out = []
            for i, (m_prev, l_prev, acc) in enumerate(carries):
                sc = _nt(qs[i], k_ref[i, pl.ds(start, width * t), :])
                if diagonal:
                    sc = jnp.where(_causal_keep(t), sc, NEG)
                m_new = jnp.maximum(m_prev, jnp.max(sc, axis=-1, keepdims=True))
                p = jnp.exp2(sc * ATTN_C - m_new * ATTN_C)
                alpha = jnp.exp2((m_prev - m_new) * ATTN_C)
                l_new = alpha * l_prev + jnp.sum(p, axis=-1, keepdims=True)
                pv = jnp.dot(p.astype(BF16), v_ref[i, pl.ds(start, width * t), :], preferred_element_type=F32)
                out.append((m_new, l_new, alpha * acc + pv))
            return tuple(out)

        init = tuple((jnp.full((t, 1), NEG, F32), jnp.zeros((t, 1), F32), jnp.zeros((t, VDIM), F32)) for _ in range(hp))
        carries = lax.fori_loop(0, qi // 2, lambda j, c: block(2 * j, c, False, width=2), init)
        carries = lax.cond(qi % 2 == 1, lambda c: block(qi - 1, c, False), lambda c: c, carries)
        for i, (m, l, acc) in enumerate(block(qi, carries, True)):
            o_ref[i] = (acc / l).astype(o_ref.dtype)
            lse_ref[i] = m * ATTN_SCALE + jnp.log(l)
        if finish is not None:
            finish()

    qmap = lambda h, i: (h, i, 0)
    whole = lambda h, i: (h, 0, 0)
    return pl.pallas_call(
        body, grid=(nh // hp, nb),
        in_specs=[pl.BlockSpec((hp, t, QK), qmap), pl.BlockSpec((hp, s, QK), whole), pl.BlockSpec((hp, s, VDIM), whole)] + [HBM_SPEC] * n_r,
        out_specs=[pl.BlockSpec((hp, t, VDIM), qmap), pl.BlockSpec((hp, t, 1), qmap)] + [HBM_SPEC] * n_r,
        out_shape=[jax.ShapeDtypeStruct((nh, s, VDIM), BF16), jax.ShapeDtypeStruct((nh, s, 1), F32)] + ([rider["out"]] if n_r else []),
        scratch_shapes=_comm_scratch() if n_r else [],
        compiler_params=_cparams("arbitrary", "arbitrary"), name="attn_fwd_gather" if n_r else "attn_fwd",
    )(*([q, k, v] + ([rider["src"]] if n_r else [])))


def _attn_delta_call(o, do):
    nh, s, d = o.shape

    def fn(ov, dv):
        return (jnp.sum(ov.astype(F32) * dv.astype(F32), axis=-1, keepdims=True),), ()

    return _rowwise_call(fn, [o.reshape(nh * s, d), do.reshape(nh * s, d)], [], [(1, F32)], [], "attn_delta")[0]


def _attn_bwd_call(q, k, v, do, lse_t, delta_t, rider=None):
    nh, s, _ = q.shape
    t = _attn_tile(s)
    nb = s // t
    n_r = 0 if rider is None else 1

    def body(*refs):
        q_ref, k_ref, v_ref, do_ref, lse_ref, delta_ref = refs[:6]
        dq_ref, dk_ref, dv_ref = refs[6 + n_r:9 + n_r]
        dk_sc, dv_sc = refs[9 + 2 * n_r:11 + 2 * n_r]
        kj = pl.program_id(1)
        finish = None
        if rider is not None:
            h = pl.program_id(0)
            finish = _rider_phases(rider, refs[6], refs[9 + n_r], refs[11 + 2 * n_r:],
                                   jnp.logical_and(h == 0, kj == 0), jnp.logical_and(h == nh - 1, kj == nb - 1))

        @pl.when(kj == 0)
        def _():
            dq_ref[...] = jnp.zeros_like(dq_ref)

        dk_sc[...] = jnp.zeros_like(dk_sc)
        dv_sc[...] = jnp.zeros_like(dv_sc)
        kblk, vblk = k_ref[0], v_ref[0]

        def block(qb, diagonal):
            start = pl.multiple_of(qb * t, t)
            qblk = q_ref[0, pl.ds(start, t), :]
            doblk = do_ref[0, pl.ds(start, t), :]
            sc = _nt(kblk, qblk)
            if diagonal:
                sc = jnp.where(_causal_keep(t, keys_on_rows=True), sc, NEG)
            p = jnp.exp2(sc * ATTN_C - lse_ref[0, :, pl.ds(start, t)] * LOG2E)
            dv_sc[...] += jnp.dot(p.astype(BF16), doblk, preferred_element_type=F32)
            dp = _nt(vblk, doblk)
            ds = (p * (dp - delta_ref[0, :, pl.ds(start, t)])).astype(BF16)
            dk_sc[...] += jnp.dot(ds, qblk, preferred_element_type=F32)
            dq_ref[0, pl.ds(start, t), :] += lax.dot_general(ds, kblk, (((0,), (0,)), ((), ())), preferred_element_type=F32)

        block(kj, True)

        def rest(qb, carry):
            block(qb, False)
            return carry

        lax.fori_loop(kj + 1, nb, rest, 0)
        dk_ref[0] = (dk_sc[...] * ATTN_SCALE).astype(dk_ref.dtype)
        dv_ref[0] = dv_sc[...].astype(dv_ref.dtype)

        @pl.when(kj == nb - 1)
        def _():
            dq_ref[...] = dq_ref[...] * ATTN_SCALE

        if finish is not None:
            finish()

    kmap = lambda h, j: (h, j, 0)
    whole = lambda h, j: (h, 0, 0)
    return pl.pallas_call(
        body, grid=(nh, nb),
        in_specs=[pl.BlockSpec((1, s, QK), whole), pl.BlockSpec((1, t, QK), kmap), pl.BlockSpec((1, t, VDIM), kmap),
                  pl.BlockSpec((1, s, VDIM), whole), pl.BlockSpec((1, 1, s), whole), pl.BlockSpec((1, 1, s), whole)] + [HBM_SPEC] * n_r,
        out_specs=[pl.BlockSpec((1, s, QK), whole), pl.BlockSpec((1, t, QK), kmap), pl.BlockSpec((1, t, VDIM), kmap)] + [HBM_SPEC] * n_r,
        out_shape=[jax.ShapeDtypeStruct((nh, s, QK), F32), jax.ShapeDtypeStruct((nh, s, QK), F32),
                   jax.ShapeDtypeStruct((nh, s, VDIM), F32)] + ([rider["out"]] if n_r else []),
        scratch_shapes=[pltpu.VMEM((t, QK), F32), pltpu.VMEM((t, VDIM), F32)] + (_comm_scratch() if n_r else []),
        compiler_params=_cparams("arbitrary", "arbitrary"), name="attn_bwd_exchange" if n_r else "attn_bwd",
    )(*([q, k, v, do, lse_t, delta_t] + ([rider["src"]] if n_r else [])))


HEAD_COLS = NOPE + VDIM
HEADS_TILE = 256
HEADS_STRIP = 32


def _swap_rope_halves(t, lane):
    half = ROPE // 2
    return jnp.where(lane < half, pltpu.roll(t, LANE - half, 1), pltpu.roll(t, half, 1))


def _head_fwd(n, p, gain, cs, sn, lane):
    r = lax.rsqrt((jnp.sum(n * n, axis=-1, keepdims=True) + jnp.sum(p * p, axis=-1, keepdims=True)) * (1.0 / QK) + EPS)
    yp = p * r * gain[:, NOPE:]
    return n * r * gain[:, :NOPE], yp * cs + _swap_rope_halves(yp, lane) * sn


def _head_bwd(n, p, gain, cs, sn, lane, dzn, dzp):
    r = lax.rsqrt((jnp.sum(n * n, axis=-1, keepdims=True) + jnp.sum(p * p, axis=-1, keepdims=True)) * (1.0 / QK) + EPS)
    dyp = dzp * cs + _swap_rope_halves(dzp * sn, lane)
    gyn, gyp = dzn * gain[:, :NOPE], dyp * gain[:, NOPE:]
    dot = jnp.sum(gyn * n, axis=-1, keepdims=True) + jnp.sum(gyp * p, axis=-1, keepdims=True)
    coef = dot * (r * r * r) * (1.0 / QK)
    d_gn = jnp.sum(dzn * n * r, axis=0, keepdims=True)
    d_gp = jnp.sum(dyp * p * r, axis=0, keepdims=True)
    return gyn * r - n * coef, gyp * r - p * coef, d_gn, d_gp


def _heads_fwd_call(q, kv, proj, cs, sn, q_gain, k_gain):
    s = q.shape[0]
    t = min(HEADS_TILE, s)

    def body(q_ref, kv_ref, last_ref, cs_ref, sn_ref, qg_ref, kg_ref, qh_ref, kh_ref, vh_ref):
        hs = min(HEADS_STRIP, t)
        lane = lax.broadcasted_iota(jnp.int32, (hs, LANE), 1)

        def strip(i, carry):
            rows = pl.ds(pl.multiple_of(i * hs, hs), hs)
            cs_, sn_ = cs_ref[rows, :], sn_ref[rows, :]
            kp = jnp.where(lane < ROPE, last_ref[rows, :], 0.0)
            for h in range(MLA_H):
                c0 = h * HEAD_COLS
                zn, zp = _head_fwd(q_ref[rows, c0:c0 + NOPE], q_ref[rows, c0 + NOPE:c0 + HEAD_COLS], qg_ref[...], cs_, sn_, lane)
                qh_ref[h, rows, :NOPE] = zn.astype(BF16)
                qh_ref[h, rows, NOPE:] = zp[:, :ROPE].astype(BF16)
                zn, zp = _head_fwd(kv_ref[rows, c0:c0 + NOPE], kp, kg_ref[...], cs_, sn_, lane)
                kh_ref[h, rows, :NOPE] = zn.astype(BF16)
                kh_ref[h, rows, NOPE:] = zp[:, :ROPE].astype(BF16)
                vh_ref[h, rows, :] = kv_ref[rows, c0 + NOPE:c0 + HEAD_COLS].astype(BF16)
            return carry

        lax.fori_loop(0, t // hs, strip, 0)

    rows = lambda i: (i, 0)
    whole = lambda i: (0, 0)
    heads = lambda i: (0, i, 0)
    wide = MLA_H * HEAD_COLS
    return pl.pallas_call(
        body, grid=(s // t,),
        in_specs=[pl.BlockSpec((t, wide), rows), pl.BlockSpec((t, wide), rows),
                  pl.BlockSpec((t, LANE), lambda i: (i, PROJ_LAST // LANE)),
                  pl.BlockSpec((t, LANE), rows), pl.BlockSpec((t, LANE), rows),
                  pl.BlockSpec((1, HEAD_COLS), whole), pl.BlockSpec((1, HEAD_COLS), whole)],
        out_specs=[pl.BlockSpec((MLA_H, t, QK), heads), pl.BlockSpec((MLA_H, t, QK), heads), pl.BlockSpec((MLA_H, t, VDIM), heads)],
        out_shape=[jax.ShapeDtypeStruct((MLA_H, s, QK), BF16), jax.ShapeDtypeStruct((MLA_H, s, QK), BF16),
                   jax.ShapeDtypeStruct((MLA_H, s, VDIM), BF16)],
        compiler_params=_cparams("arbitrary"), name="mla_heads_fwd",
    )(q, kv, proj, cs, sn, q_gain, k_gain)


def _heads_bwd_call(q, kv, proj, cs, sn, q_gain, k_gain, dqh, dkh, dvh):
    s = q.shape[0]
    t = min(HEADS_TILE, s)

    def body(q_ref, kv_ref, last_ref, cs_ref, sn_ref, qg_ref, kg_ref, dqh_ref, dkh_ref, dvh_ref,
             dq_ref, dkv_ref, dkr_ref, dqg_ref, dkg_ref):
        hs = min(HEADS_STRIP, t)
        lane = lax.broadcasted_iota(jnp.int32, (hs, LANE), 1)
        no_lanes = jnp.zeros((hs, LANE - ROPE), F32)

        def strip(i, gains):
            rows = pl.ds(pl.multiple_of(i * hs, hs), hs)
            cs_, sn_ = cs_ref[rows, :], sn_ref[rows, :]
            kp = jnp.where(lane < ROPE, last_ref[rows, :], 0.0)
            d_kp = jnp.zeros((hs, LANE), F32)
            d_qg_n, d_qg_p, d_kg_n, d_kg_p = gains
            for h in range(MLA_H):
                c0 = h * HEAD_COLS
                dz = dqh_ref[h, rows, :]
                dzp = jnp.concatenate([dz[:, NOPE:], no_lanes], axis=1)
                d_n, d_p, g_n, g_p = _head_bwd(q_ref[rows, c0:c0 + NOPE], q_ref[rows, c0 + NOPE:c0 + HEAD_COLS], qg_ref[...],
                                               cs_, sn_, lane, dz[:, :NOPE], dzp)
                dq_ref[rows, c0:c0 + NOPE] = d_n.astype(dq_ref.dtype)
                dq_ref[rows, c0 + NOPE:c0 + HEAD_COLS] = d_p.astype(dq_ref.dtype)
                d_qg_n, d_qg_p = d_qg_n + g_n, d_qg_p + g_p
                dz = dkh_ref[h, rows, :]
                dzp = jnp.concatenate([dz[:, NOPE:], no_lanes], axis=1)
                d_n, d_p, g_n, g_p = _head_bwd(kv_ref[rows, c0:c0 + NOPE], kp, kg_ref[...], cs_, sn_, lane, dz[:, :NOPE], dzp)
                dkv_ref[rows, c0:c0 + NOPE] = d_n.astype(dkv_ref.dtype)
                dkv_ref[rows, c0 + NOPE:c0 + HEAD_COLS] = dvh_ref[h, rows, :].astype(dkv_ref.dtype)
                d_kp = d_kp + d_p
                d_kg_n, d_kg_p = d_kg_n + g_n, d_kg_p + g_p
            dkr_ref[rows, :] = d_kp
            return d_qg_n, d_qg_p, d_kg_n, d_kg_p

        zero_gains = (jnp.zeros((1, NOPE), F32), jnp.zeros((1, LANE), F32), jnp.zeros((1, NOPE), F32), jnp.zeros((1, LANE), F32))
        d_qg_n, d_qg_p, d_kg_n, d_kg_p = lax.fori_loop(0, t // hs, strip, zero_gains)
        first = pl.program_id(0) == 0
        _acc_store(dqg_ref.at[:, pl.ds(0, NOPE)], d_qg_n, first)
        _acc_store(dqg_ref.at[:, pl.ds(NOPE, LANE)], d_qg_p, first)
        _acc_store(dkg_ref.at[:, pl.ds(0, NOPE)], d_kg_n, first)
        _acc_store(dkg_ref.at[:, pl.ds(NOPE, LANE)], d_kg_p, first)

    rows = lambda i: (i, 0)
    whole = lambda i: (0, 0)
    heads = lambda i: (0, i, 0)
    wide = MLA_H * HEAD_COLS
    return pl.pallas_call(
        body, grid=(s // t,),
        in_specs=[pl.BlockSpec((t, wide), rows), pl.BlockSpec((t, wide), rows),
                  pl.BlockSpec((t, LANE), lambda i: (i, PROJ_LAST // LANE)),
                  pl.BlockSpec((t, LANE), rows), pl.BlockSpec((t, LANE), rows),
                  pl.BlockSpec((1, HEAD_COLS), whole), pl.BlockSpec((1, HEAD_COLS), whole),
                  pl.BlockSpec((MLA_H, t, QK), heads), pl.BlockSpec((MLA_H, t, QK), heads), pl.BlockSpec((MLA_H, t, VDIM), heads)],
        out_specs=[pl.BlockSpec((t, wide), rows), pl.BlockSpec((t, wide), rows), pl.BlockSpec((t, LANE), rows),
                   pl.BlockSpec((1, HEAD_COLS), whole), pl.BlockSpec((1, HEAD_COLS), whole)],
        out_shape=[jax.ShapeDtypeStruct((s, wide), BF16), jax.ShapeDtypeStruct((s, wide), BF16), jax.ShapeDtypeStruct((s, LANE), F32),
                   jax.ShapeDtypeStruct((1, HEAD_COLS), F32), jax.ShapeDtypeStruct((1, HEAD_COLS), F32)],
        compiler_params=_cparams("arbitrary"), name="mla_heads_bwd",
    )(q, kv, proj, cs, sn, q_gain, k_gain, dqh, dkh, dvh)


CONV_TC = 512
HALO = 8


CONV_STRIP = 32


def _conv_tiles(s):
    return min(512, s)


def _silu(v):
    return v * jax.nn.sigmoid(v)


def _conv_taps(win, w_ref, b_ref, n):
    acc = jnp.broadcast_to(b_ref[...], (n, win.shape[1]))
    for k in range(CONV_K):
        lo = HALO - (CONV_K - 1) + k
        acc = acc + w_ref[k:k + 1, :] * win[lo:lo + n]
    return acc


def _conv_fwd_call(x, col0, w, b):
    s = x.shape[0]
    ts = _conv_tiles(s)
    hb = ts // HALO
    c0 = col0 // CONV_TC
    assert col0 % CONV_TC == 0

    def body(x_ref, prev_ref, w_ref, b_ref, y_ref, buf):
        si = pl.program_id(1)
        buf[0:HALO, :] = jnp.where(si > 0, prev_ref[...], 0.0)
        buf[HALO:, :] = x_ref[...]
        cs = min(CONV_STRIP, ts)

        def strip(i, carry):
            r0 = pl.multiple_of(i * cs, cs)
            win = buf[pl.ds(r0, cs + HALO), :]
            y_ref[pl.ds(r0, cs), :] = _silu(_conv_taps(win, w_ref, b_ref, cs))
            return carry

        lax.fori_loop(0, ts // cs, strip, 0)

    return pl.pallas_call(
        body, grid=(CONV_DIM // CONV_TC, s // ts),
        in_specs=[pl.BlockSpec((ts, CONV_TC), lambda ci, si: (si, ci + c0)),
                  pl.BlockSpec((HALO, CONV_TC), lambda ci, si: (jnp.maximum(si * hb - 1, 0), ci + c0)),
                  pl.BlockSpec((CONV_K, CONV_TC), lambda ci, si: (0, ci)),
                  pl.BlockSpec((1, CONV_TC), lambda ci, si: (0, ci))],
        out_specs=pl.BlockSpec((ts, CONV_TC), lambda ci, si: (si, ci)),
        out_shape=jax.ShapeDtypeStruct((s, CONV_DIM), F32),
        scratch_shapes=[pltpu.VMEM((ts + HALO, CONV_TC), F32)],
        compiler_params=_cparams("arbitrary", "arbitrary"), name="conv_fwd",
    )(x, x, w, b)


def _conv_bwd_call(x, col0, w, b, dy):
    s = x.shape[0]
    ts = _conv_tiles(s)
    hb = ts // HALO
    ns = s // ts
    last_halo = s // HALO - 1
    c0 = col0 // CONV_TC

    def body(x_ref, prev_ref, next_ref, dy_ref, dyn_ref, w_ref, b_ref, dx_ref, dw_ref, db_ref, xbuf, dbuf):
        si = pl.program_id(1)
        xbuf[0:HALO, :] = jnp.where(si > 0, prev_ref[...], 0.0)
        xbuf[HALO:HALO + ts, :] = x_ref[...]
        xbuf[HALO + ts:, :] = next_ref[...]
        cs = min(CONV_STRIP, ts)

        def d_silu(pre):
            sg = jax.nn.sigmoid(pre)
            return sg * (1.0 + pre * (1.0 - sg))

        def strip_dpre(i, carry):
            r0 = pl.multiple_of(i * cs, cs)
            pre = _conv_taps(xbuf[pl.ds(r0, cs + HALO), :], w_ref, b_ref, cs)
            dbuf[pl.ds(r0, cs), :] = dy_ref[pl.ds(r0, cs), :] * d_silu(pre)
            return carry

        lax.fori_loop(0, ts // cs, strip_dpre, 0)
        pre = _conv_taps(xbuf[ts:ts + 2 * HALO, :], w_ref, b_ref, HALO)
        dbuf[ts:, :] = jnp.where(si < ns - 1, dyn_ref[...] * d_silu(pre), 0.0)

        def strip_grads(i, sums):
            r0 = pl.multiple_of(i * cs, cs)
            dwin = dbuf[pl.ds(r0, cs + HALO), :]
            xwin = xbuf[pl.ds(r0, cs + HALO), :]
            dx = jnp.zeros((cs, CONV_TC), F32)
            for k in range(CONV_K):
                dx = dx + w_ref[k:k + 1, :] * dwin[CONV_K - 1 - k:CONV_K - 1 - k + cs]
            dx_ref[pl.ds(r0, cs), :] = dx.astype(dx_ref.dtype)
            dpre = dwin[0:cs]
            d_b = sums[0] + jnp.sum(dpre, axis=0, keepdims=True)
            d_w = [sums[1 + k] + jnp.sum(dpre * xwin[HALO - (CONV_K - 1) + k:HALO - (CONV_K - 1) + k + cs], axis=0, keepdims=True)
                   for k in range(CONV_K)]
            return (d_b, *d_w)

        sums = lax.fori_loop(0, ts // cs, strip_grads, tuple(jnp.zeros((1, CONV_TC), F32) for _ in range(1 + CONV_K)))
        first = si == 0
        _acc_store(db_ref, sums[0], first)
        for k in range(CONV_K):
            _acc_store(dw_ref.at[pl.ds(k, 1), :], sums[1 + k], first)

    main = lambda ci, si: (si, ci)
    x_main = lambda ci, si: (si, ci + c0)
    x_prev = lambda ci, si: (jnp.maximum(si * hb - 1, 0), ci + c0)
    x_next = lambda ci, si: (jnp.minimum(si * hb + hb, last_halo), ci + c0)
    return pl.pallas_call(
        body, grid=(CONV_DIM // CONV_TC, ns),
        in_specs=[pl.BlockSpec((ts, CONV_TC), x_main), pl.BlockSpec((HALO, CONV_TC), x_prev), pl.BlockSpec((HALO, CONV_TC), x_next),
                  pl.BlockSpec((ts, CONV_TC), main),
                  pl.BlockSpec((HALO, CONV_TC), lambda ci, si: (jnp.minimum(si * hb + hb, last_halo), ci)),
                  pl.BlockSpec((CONV_K, CONV_TC), lambda ci, si: (0, ci)),
                  pl.BlockSpec((1, CONV_TC), lambda ci, si: (0, ci))],
        out_specs=[pl.BlockSpec((ts, CONV_TC), main),
                   pl.BlockSpec((CONV_K, CONV_TC), lambda ci, si: (0, ci)),
                   pl.BlockSpec((1, CONV_TC), lambda ci, si: (0, ci))],
        out_shape=[jax.ShapeDtypeStruct((s, CONV_DIM), BF16), jax.ShapeDtypeStruct((CONV_K, CONV_DIM), F32),
                   jax.ShapeDtypeStruct((1, CONV_DIM), F32)],
        scratch_shapes=[pltpu.VMEM((ts + 2 * HALO, CONV_TC), F32), pltpu.VMEM((ts + HALO, CONV_TC), F32)],
        compiler_params=_cparams("arbitrary", "arbitrary"), name="conv_bwd",
    )(x, x, x, dy, dy, w, b)


GW = SSD_HPG * SSD_P
B_COL = SSD_DI
C_COL = SSD_DI + SSD_G * SSD_N


def _ones_where(mask):
    return jnp.where(mask, 1.0, 0.0).astype(BF16)


def _split(v, passes):
    parts, rest = [], v
    for i in range(passes):
        part = rest.astype(BF16)
        parts.append(part)
        if i + 1 < passes:
            rest = rest - part.astype(F32)
    return parts


def _dot_sel_r(v, sel, passes=3):
    out = None
    for part in _split(v, passes):
        t = jnp.dot(part, sel, preferred_element_type=F32)
        out = t if out is None else out + t
    return out


def _dot_sel_l(sel, v, passes=3):
    out = None
    for part in _split(v, passes):
        t = jnp.dot(sel, part, preferred_element_type=F32)
        out = t if out is None else out + t
    return out


def _ssd_consts():
    r = lax.broadcasted_iota(jnp.int32, (SSD_L, SSD_L), 0)
    c = lax.broadcasted_iota(jnp.int32, (SSD_L, SSD_L), 1)
    tril = r >= c
    triu = c >= r
    shift = SSD_P.bit_length() - 1
    eh = lax.broadcasted_iota(jnp.int32, (SSD_H, SSD_DI), 0)
    ej = lax.broadcasted_iota(jnp.int32, (SSD_H, SSD_DI), 1)
    expand = _ones_where(lax.shift_right_logical(ej, shift) == eh)
    rj = lax.broadcasted_iota(jnp.int32, (SSD_DI, SSD_H), 0)
    rh = lax.broadcasted_iota(jnp.int32, (SSD_DI, SSD_H), 1)
    reduce_ = _ones_where(lax.shift_right_logical(rj, shift) == rh)
    lane = lax.broadcasted_iota(jnp.int32, (SSD_L, LANE), 1)
    return tril, triu, expand, reduce_, lane < SSD_P


def _ssd_decays(dt, dt_t, a, a_t, tril, triu, expand):
    dta = dt * a
    acum = _dot_sel_l(_ones_where(tril), dta)
    acum_t = _dot_sel_r(dt_t * a_t, _ones_where(triu))
    dta_e = _dot_sel_r(dta, expand)
    acum_e = _dot_sel_r(acum, expand)
    last_e = jnp.sum(dta_e, axis=0, keepdims=True)
    return acum, acum_t, acum_e, last_e


def _head_decay(acum, acum_t, h, tril):
    seg = acum[:, h:h + 1] - acum_t[h:h + 1, :]
    return jnp.exp(jnp.where(tril, seg, NEG))


def _ssd_fwd_call(xbc, dt, a):
    s = xbc.shape[0]
    nc = s // SSD_L
    dt_t = dt.T
    a_t = a.T

    def body(xbc_ref, dt_ref, dtt_ref, a_ref, at_ref, y_ref, st_ref, s_sc):
        ci = pl.program_id(0)

        @pl.when(ci == 0)
        def _():
            s_sc[...] = jnp.zeros_like(s_sc)

        st_ref[0] = s_sc[...]
        tril, triu, expand, _, low_half = _ssd_consts()
        acum, acum_t, acum_e, last_e = _ssd_decays(dt_ref[...], dtt_ref[...], a_ref[...], at_ref[...], tril, triu, expand)
        dt_e = _dot_sel_r(dt_ref[...], expand, passes=2)
        xdt = xbc_ref[:, :SSD_DI] * dt_e
        xdt_b = xdt.astype(BF16)
        xw_b = (xdt * jnp.exp(last_e - acum_e)).astype(BF16)
        ea_e = jnp.exp(acum_e)
        el_e = jnp.exp(last_e)
        for g in range(SSD_G):
            gs = slice(g * GW, (g + 1) * GW)
            bg = xbc_ref[:, B_COL + g * SSD_N:B_COL + (g + 1) * SSD_N]
            cg_b = xbc_ref[:, C_COL + g * SSD_N:C_COL + (g + 1) * SSD_N].astype(BF16)
            bg_b = bg.astype(BF16)
            cb = _nt(cg_b, bg_b)
            st = s_sc[:, gs]
            y_off = jnp.dot(cg_b, st.astype(BF16), preferred_element_type=F32) * ea_e[:, gs]
            for pr in range(SSD_HPG // 2):
                ls = slice(g * GW + pr * LANE, g * GW + (pr + 1) * LANE)
                xp = xdt_b[:, ls]
                yd = []
                for half in range(2):
                    h = g * SSD_HPG + pr * 2 + half
                    m = (cb * _head_decay(acum, acum_t, h, tril)).astype(BF16)
                    yd.append(jnp.dot(m, xp, preferred_element_type=F32))
                y_ref[:, ls] = jnp.where(low_half, yd[0], yd[1]) + y_off[:, pr * LANE:(pr + 1) * LANE]
            s_sc[:, gs] = st * el_e[:, gs] + jnp.dot(bg.T.astype(BF16), xw_b[:, gs], preferred_element_type=F32)

    row = lambda i: (i, 0)
    return pl.pallas_call(
        body, grid=(nc,),
        in_specs=[pl.BlockSpec((SSD_L, CONV_DIM), row), pl.BlockSpec((SSD_L, SSD_H), row),
                  pl.BlockSpec((SSD_H, SSD_L), lambda i: (0, i)), pl.BlockSpec((1, SSD_H), lambda i: (0, 0)),
                  pl.BlockSpec((SSD_H, 1), lambda i: (0, 0))],
        out_specs=[pl.BlockSpec((SSD_L, SSD_DI), row), pl.BlockSpec((1, SSD_N, SSD_DI), lambda i: (i, 0, 0))],
        out_shape=[jax.ShapeDtypeStruct((s, SSD_DI), F32), jax.ShapeDtypeStruct((nc, SSD_N, SSD_DI), F32)],
        scratch_shapes=[pltpu.VMEM((SSD_N, SSD_DI), F32)],
        compiler_params=_cparams("arbitrary"), name="ssd_fwd",
    )(xbc, dt, dt_t, a, a_t)


def _ssd_bwd_call(xbc, dt, a, states, dy, dx_extra):
    s = xbc.shape[0]
    nc = s // SSD_L
    dt_t = dt.T
    a_t = a.T

    def body(xbc_ref, dt_ref, dtt_ref, a_ref, at_ref, st_ref, dy_ref, dxe_ref,
             dxbc_ref, ddt_ref, da_ref, ds_sc, yf_sc, dxd_sc, dxw_sc):
        i = pl.program_id(0)

        @pl.when(i == 0)
        def _():
            ds_sc[...] = jnp.zeros_like(ds_sc)

        tril, triu, expand, reduce_, low_half = _ssd_consts()
        dt = dt_ref[...]
        a_row = a_ref[...]
        acum, acum_t, acum_e, last_e = _ssd_decays(dt, dtt_ref[...], a_row, at_ref[...], tril, triu, expand)
        dt_e = _dot_sel_r(dt, expand, passes=2)
        x = xbc_ref[:, :SSD_DI]
        xdt = x * dt_e
        xdt_b = xdt.astype(BF16)
        w_e = jnp.exp(last_e - acum_e)
        xw_b = (xdt * w_e).astype(BF16)
        ea_e = jnp.exp(acum_e)
        el_e = jnp.exp(last_e)
        dy = dy_ref[...]
        dy_b = dy.astype(BF16)
        s_prev = st_ref[0]
        ds_new = ds_sc[...]
        ds_new_b = ds_new.astype(BF16)
        triu_b = _ones_where(triu)
        strict_tril = jnp.logical_not(triu)
        head_ids = lax.broadcasted_iota(jnp.int32, (1, SSD_H), 1)
        d_dta_diag = jnp.zeros((SSD_L, SSD_H), F32)
        for g in range(SSD_G):
            gs = slice(g * GW, (g + 1) * GW)
            bs_ = slice(B_COL + g * SSD_N, B_COL + (g + 1) * SSD_N)
            cs_ = slice(C_COL + g * SSD_N, C_COL + (g + 1) * SSD_N)
            bg = xbc_ref[:, bs_]
            cg = xbc_ref[:, cs_]
            bg_b, cg_b = bg.astype(BF16), cg.astype(BF16)
            st_b = s_prev[:, gs].astype(BF16)
            y_off = jnp.dot(cg_b, st_b, preferred_element_type=F32) * ea_e[:, gs]
            yf_sc[:, gs] = y_off
            dz_b = (dy[:, gs] * ea_e[:, gs]).astype(BF16)
            d_c = _nt(dz_b, st_b)
            ds_prev = ds_new[:, gs] * el_e[:, gs] + jnp.dot(cg.T.astype(BF16), dz_b, preferred_element_type=F32)
            dxw_sc[:, gs] = jnp.dot(bg_b, ds_new_b[:, gs], preferred_element_type=F32)
            d_b = _nt(xw_b[:, gs], ds_new_b[:, gs])
            cb = _nt(cg_b, bg_b)
            d_g = jnp.zeros((SSD_L, SSD_L), F32)
            for pr in range(SSD_HPG // 2):
                ls = slice(g * GW + pr * LANE, g * GW + (pr + 1) * LANE)
                xp = xdt_b[:, ls]
                dyp = dy[:, ls]
                dyp_b = dy_b[:, ls]
                dxd = []
                for half in range(2):
                    h = g * SSD_HPG + pr * 2 + half
                    dec = _head_decay(acum, acum_t, h, tril)
                    m = cb * dec
                    dxd.append(jnp.dot(m.T.astype(BF16), dyp_b, preferred_element_type=F32))
                    mine = low_half if half == 0 else jnp.logical_not(low_half)
                    d_m = _nt(jnp.where(mine, dyp, 0.0).astype(BF16), xp)
                    d_g = d_g + d_m * dec
                    below = jnp.dot(triu_b, (d_m * m).astype(BF16), preferred_element_type=F32)
                    col = jnp.sum(jnp.where(strict_tril, below, 0.0), axis=1, keepdims=True)
                    d_dta_diag = d_dta_diag + col * jnp.where(head_ids == h, 1.0, 0.0)
                dxd_sc[:, ls] = jnp.where(low_half, dxd[0], dxd[1])
            d_g_b = d_g.astype(BF16)
            dxbc_ref[:, cs_] = d_c + jnp.dot(d_g_b, bg_b, preferred_element_type=F32)
            dxbc_ref[:, bs_] = d_b + jnp.dot(d_g.T.astype(BF16), cg_b, preferred_element_type=F32)
            ds_sc[:, gs] = ds_prev
        dxw = dxw_sc[...]
        dxd = dxd_sc[...]
        dw_e = xdt * dxw * w_e
        d_out = _dot_sel_r(dy * yf_sc[...], reduce_, passes=2)
        d_upd = _dot_sel_r(dw_e, reduce_, passes=2)
        d_tot_e = jnp.sum(ds_new * s_prev, axis=0, keepdims=True) * el_e
        d_tot = _dot_sel_r(jnp.broadcast_to(d_tot_e, (8, SSD_DI)), reduce_, passes=2)[0:1]
        d_dta = _dot_sel_l(triu_b, d_out) + _dot_sel_l(_ones_where(strict_tril), d_upd) + d_tot + d_dta_diag
        dxdt = dxd + dxw * w_e
        dxbc_ref[:, :SSD_DI] = dxdt * dt_e + dxe_ref[...]
        ddt_ref[...] = d_dta * a_row + _dot_sel_r(dxdt * x, reduce_, passes=2)
        _acc_store(da_ref, jnp.sum(d_dta * dt, axis=0, keepdims=True), i == 0)

    rev = lambda i: (nc - 1 - i, 0)
    return pl.pallas_call(
        body, grid=(nc,),
        in_specs=[pl.BlockSpec((SSD_L, CONV_DIM), rev), pl.BlockSpec((SSD_L, SSD_H), rev),
                  pl.BlockSpec((SSD_H, SSD_L), lambda i: (0, nc - 1 - i)), pl.BlockSpec((1, SSD_H), lambda i: (0, 0)),
                  pl.BlockSpec((SSD_H, 1), lambda i: (0, 0)),
                  pl.BlockSpec((1, SSD_N, SSD_DI), lambda i: (nc - 1 - i, 0, 0)),
                  pl.BlockSpec((SSD_L, SSD_DI), rev), pl.BlockSpec((SSD_L, SSD_DI), rev)],
        out_specs=[pl.BlockSpec((SSD_L, CONV_DIM), rev), pl.BlockSpec((SSD_L, SSD_H), rev),
                   pl.BlockSpec((1, SSD_H), lambda i: (0, 0))],
        out_shape=[jax.ShapeDtypeStruct((s, CONV_DIM), F32), jax.ShapeDtypeStruct((s, SSD_H), F32),
                   jax.ShapeDtypeStruct((1, SSD_H), F32)],
        scratch_shapes=[pltpu.VMEM((SSD_N, SSD_DI), F32), pltpu.VMEM((SSD_L, SSD_DI), F32),
                        pltpu.VMEM((SSD_L, SSD_DI), F32), pltpu.VMEM((SSD_L, SSD_DI), F32)],
        compiler_params=_cparams("arbitrary"), name="ssd_bwd",
    )(xbc, dt, dt_t, a, a_t, states, dy, dx_extra)


HBM_SPEC = pl.BlockSpec(memory_space=pltpu.HBM)
N_PEERS = N_DEV - 1


def _flip(v, f):
    return 1 - v if f else v


def _all_gather(shard):
    rows, c = shard.shape

    def body(x_ref, out_ref, send_sems, recv_sems, local_sem):
        x, y, cc = lax.axis_index("x"), lax.axis_index("y"), lax.axis_index("c")
        me, sibling = (x, y, cc), (x, y, 1 - cc)
        chips = [(1 - x, y), (x, 1 - y), (1 - x, 1 - y)]

        def slot(px, py, pc):
            return out_ref.at[4 * px + 2 * py + pc]

        def copy(k, block, to, src=None):
            return pltpu.make_async_remote_copy(
                src_ref=slot(*block) if src is None else src, dst_ref=slot(*block),
                send_sem=send_sems.at[k], recv_sem=recv_sems.at[k],
                device_id=to, device_id_type=pl.DeviceIdType.MESH)

        mine = pltpu.make_async_copy(x_ref, slot(*me), local_sem)
        mine.start()
        first = [copy(0, me, sibling, src=x_ref)]
        first += [copy(1 + j, me, (*chip, cc), src=x_ref) for j, chip in enumerate(chips)]
        for cp in first:
            cp.start()
        passed = [copy(4 + j, (*chip, cc), sibling) for j, chip in enumerate(chips)]
        for j, chip in enumerate(chips):
            copy(1 + j, (*chip, cc), me).wait_recv()
            passed[j].start()
        copy(0, sibling, me).wait_recv()
        for j, chip in enumerate(chips):
            copy(4 + j, (*chip, 1 - cc), me).wait_recv()
        for cp in first + passed:
            cp.wait_send()
        mine.wait()

    return pl.pallas_call(
        body, out_shape=jax.ShapeDtypeStruct((N_DEV, rows, c), shard.dtype),
        in_specs=[HBM_SPEC], out_specs=HBM_SPEC,
        scratch_shapes=[pltpu.SemaphoreType.DMA((N_PEERS,)), pltpu.SemaphoreType.DMA((N_PEERS,)), pltpu.SemaphoreType.DMA(())],
        name="all_gather",
    )(shard)


def _peer_copies(src_ref, out_ref, sems, gather, phase):
    send_sems, recv_sems, local_sem = sems
    x, y, cc = lax.axis_index("x"), lax.axis_index("y"), lax.axis_index("c")
    me = 4 * x + 2 * y + cc
    mine = pltpu.make_async_copy(src_ref if gather else src_ref.at[me], out_ref.at[me], local_sem)
    copies = []
    for k in range(1, N_DEV):
        px, py, pc = _flip(x, k & 4), _flip(y, k & 2), _flip(cc, k & 1)
        peer = 4 * px + 2 * py + pc
        src = src_ref if gather else src_ref.at[peer]
        copies.append((
            pltpu.make_async_remote_copy(
                src_ref=src, dst_ref=out_ref.at[me], send_sem=send_sems.at[k - 1], recv_sem=recv_sems.at[k - 1],
                device_id=(px, py, pc), device_id_type=pl.DeviceIdType.MESH),
            pltpu.make_async_remote_copy(
                src_ref=src, dst_ref=out_ref.at[peer], send_sem=send_sems.at[k - 1], recv_sem=recv_sems.at[k - 1],
                device_id=(px, py, pc), device_id_type=pl.DeviceIdType.MESH)))
    if phase == "start":
        mine.start()
        for send, _ in copies:
            send.start()
    else:
        for _, landed in copies:
            landed.wait_recv()
        for send, _ in copies:
            send.wait_send()
        mine.wait()


def _comm_scratch():
    return [pltpu.SemaphoreType.DMA((N_PEERS,)), pltpu.SemaphoreType.DMA((N_PEERS,)), pltpu.SemaphoreType.DMA(())]


def _gather_rider(shard):
    return dict(src=shard, out=jax.ShapeDtypeStruct((N_DEV,) + shard.shape, shard.dtype), gather=True)


def _exchange_rider(blocks):
    return dict(src=blocks, out=jax.ShapeDtypeStruct(blocks.shape, blocks.dtype), gather=False)


def _exchange_blocks(blocks):
    def body(g_ref, out_ref, *sems):
        _peer_copies(g_ref, out_ref, sems, False, "start")
        _peer_copies(g_ref, out_ref, sems, False, "finish")

    return pl.pallas_call(
        body, out_shape=jax.ShapeDtypeStruct(blocks.shape, blocks.dtype),
        in_specs=[HBM_SPEC], out_specs=HBM_SPEC, scratch_shapes=_comm_scratch(), name="exchange_blocks",
    )(blocks)


BIG = [
    ("ffn1_w13", (D_MODEL, 2 * D_FF), 1), ("ffn1_w2", (D_FF, D_MODEL), 0),
    ("w_ssd_out", (SSD_DI, D_MODEL), 0), ("w_uq", (Q_LORA, MLA_H * QK), 1), ("w_ukv", (KV_LORA, MLA_H * (NOPE + VDIM)), 1),
    ("w_mla_out", (MLA_H * VDIM, D_MODEL), 0), ("w_o", (D_MODEL, D_MODEL), 0),
    ("ffn2_w13", (D_MODEL, 2 * D_FF), 1), ("ffn2_w2", (D_FF, D_MODEL), 0), ("w_in", (D_MODEL, D_IN), 1),
]
assert all(_r % 16 == 0 for _r in [_f[0] * _f[1] // N_DEV // PACK_COLS for _, _f, _ in BIG[:-1]])
SMALL = [
    ("ln_ffn1", D_MODEL), ("ln_mix", D_MODEL), ("conv_b", CONV_DIM), ("dt_bias", SSD_H), ("a_log", SSD_H), ("d_skip", SSD_H),
    ("ssd_norm", SSD_DI), ("q_lora_norm", Q_LORA), ("kv_lora_norm", KV_LORA), ("q_norm", QK), ("k_norm", QK), ("ln_ffn2", D_MODEL),
]


def _shard_shape(full, axis):
    k, n = full
    return (k // N_DEV, n) if axis == 0 else (k, n // N_DEV)


def _shard_rows(full):
    return full[0] * full[1] // N_DEV // PACK_COLS


LAYER_ROWS = sum(_shard_rows(f) for _, f, _ in BIG)
LAYER_ROWS_PAD = -(-LAYER_ROWS // 256) * 256


def _pack_shards(shards):
    parts = [(shards[name] if axis == 0 else shards[name].T).reshape(-1, PACK_COLS) for name, _, axis in BIG]
    pad = LAYER_ROWS_PAD - LAYER_ROWS
    if pad:
        parts.append(jnp.zeros((pad, PACK_COLS), parts[0].dtype))
    return jnp.concatenate(parts, axis=0)


def _unpack_shards(packed):
    out, r = {}, 0
    for name, full, axis in BIG:
        n = _shard_rows(full)
        k, c = _shard_shape(full, axis)
        blk = packed[r:r + n]
        out[name] = blk.reshape(k, c) if axis == 0 else blk.reshape(c, k).T
        r += n
    return out


def _working_shape(full, axis):
    return full if axis == 0 else full[::-1]


def _unpack_gathered(gathered):
    out, r = {}, 0
    for name, full, axis in BIG:
        n = _shard_rows(full)
        out[name] = gathered[:, r:r + n].reshape(_working_shape(full, axis))
        r += n
    return out


def _pack_full_grads(grads):
    parts = [grads[name].reshape(N_DEV, -1, PACK_COLS) for name, _, _ in BIG]
    pad = LAYER_ROWS_PAD - LAYER_ROWS
    if pad:
        parts.append(jnp.zeros((N_DEV, pad, PACK_COLS), parts[0].dtype))
    return jnp.concatenate(parts, axis=1)


SMALL_COLS = sum(n for _, n in SMALL) + CONV_K * CONV_DIM
SMALL_ROWS = -(-(DEPTH * SMALL_COLS) // (8 * PACK_COLS)) * 8


def _pack_small(vals, conv_w):
    flat = jnp.concatenate([vals[name] for name, _ in SMALL] + [conv_w.reshape(DEPTH, -1)], axis=1).reshape(-1)
    flat = jnp.concatenate([flat, jnp.zeros((SMALL_ROWS * PACK_COLS - flat.shape[0],), F32)])
    return flat.reshape(SMALL_ROWS, PACK_COLS)


def _unpack_small(packed):
    flat = packed.reshape(-1)[:DEPTH * SMALL_COLS].reshape(DEPTH, SMALL_COLS)
    out, c = {}, 0
    for name, n in SMALL:
        out[name] = flat[:, c:c + n]
        c += n
    return out, flat[:, c:].reshape(DEPTH, CONV_K, CONV_DIM)


_IN_OFFS = [sum(IN_SPLIT[:i]) for i in range(len(IN_SPLIT) + 1)]


def _arrange_w_in(w_t):
    z, xbc, dt, cq, ckv, kr, gates = [w_t[_IN_OFFS[i]:_IN_OFFS[i + 1]] for i in range(len(IN_SPLIT))]
    pad = jnp.zeros((LANE - ROPE - SSD_H, w_t.shape[1]), w_t.dtype)
    return jnp.concatenate([z, gates, xbc, cq, ckv, kr, dt, pad], axis=0)


def _restore_w_in(g):
    z, gates, xbc = g[PROJ_Z:PROJ_GATES], g[PROJ_GATES:PROJ_XBC], g[PROJ_XBC:PROJ_CQ]
    cq, ckv = g[PROJ_CQ:PROJ_CKV], g[PROJ_CKV:PROJ_LAST]
    kr, dt = g[PROJ_LAST:PROJ_LAST + ROPE], g[PROJ_LAST + ROPE:PROJ_LAST + ROPE + SSD_H]
    return jnp.concatenate([z, xbc, dt, cq, ckv, kr, gates], axis=0)


def _pad_heads(w_t):
    k = w_t.shape[1]
    return jnp.pad(w_t.reshape(MLA_H, QK, k), ((0, 0), (0, HEAD_COLS - QK), (0, 0))).reshape(MLA_H * HEAD_COLS, k)


def _unpad_heads(g):
    k = g.shape[1]
    return g.reshape(MLA_H, HEAD_COLS, k)[:, :QK].reshape(MLA_H * QK, k)


def _row(v):
    return v.reshape(1, -1)


def _head_gain(g):
    return jnp.pad(g, (0, HEAD_COLS - QK)).reshape(1, HEAD_COLS)


def _ffn_fwd(h, ln, w13_t, w2, name):
    n = _row_fwd(_f_rmsnorm, [h], [_row(ln)], [BF16], name + "_fwd")[0]
    gu = _mm(n, w13_t, tb=True, out_dtype=BF16)
    act = _row_fwd(_f_swiglu, [gu], [], [BF16], "swiglu_fwd")[0]
    return _mm(act, w2, alpha=0.5, res=h), (h, n, gu, act)


def _ffn_bwd(dh_out, saved, ln, w13_t, w2, name):
    h, n, gu, act = saved
    d_act = _mm(dh_out, w2, tb=True, out_dtype=BF16, alpha=0.5)
    d_w2 = _mm(act, dh_out, ta=True, out_dtype=BF16, alpha=0.5)
    d_gu = _row_bwd(_f_swiglu, [gu], [], [d_act], [BF16], "swiglu_bwd", bwd=_b_swiglu)[0][0]
    d_n = _mm(d_gu, w13_t, out_dtype=BF16)
    d_w13_t = _mm(d_gu, n, ta=True, out_dtype=BF16)
    (dh,), (d_ln,) = _row_bwd(_f_rmsnorm, [h], [_row(ln)], [d_n], [F32], name + "_bwd", add={0: dh_out})
    return dh, d_w13_t, d_w2, d_ln[0]


def _mixer_fwd(h, big, small, conv_w, cs, sn, rider=None):
    s = h.shape[0]
    u = _row_fwd(_f_rmsnorm, [h], [_row(small["ln_mix"])], [BF16], "ln_mix_fwd")[0]
    proj = _mm(u, big["w_in"], tb=True)
    xbc = _conv_fwd_call(proj, PROJ_XBC, conv_w, _row(small["conv_b"]))
    dt_in = proj[:, PROJ_LAST + ROPE:PROJ_LAST + ROPE + SSD_H] + small["dt_bias"][None, :]
    dt = jax.nn.softplus(dt_in)
    a = -jnp.exp(small["a_log"])[None, :]
    y_scan, states = _ssd_fwd_call(xbc, dt, a)
    dsk = _row(jnp.repeat(small["d_skip"], SSD_P))
    gn_in = [y_scan, _win(xbc, 0, SSD_DI), _win(proj, PROJ_Z, SSD_DI)]
    yn = _row_fwd(_f_gated_norm, gn_in, [dsk, _row(small["ssd_norm"])], [BF16], "gated_norm_fwd")[0]
    y_ssd = _mm(yn, big["w_ssd_out"])
    qn = _row_fwd(_f_rmsnorm, [_win(proj, PROJ_CQ, Q_LORA)], [_row(small["q_lora_norm"])], [BF16], "q_lora_norm_fwd")[0]
    kvn = _row_fwd(_f_rmsnorm, [_win(proj, PROJ_CKV, KV_LORA)], [_row(small["kv_lora_norm"])], [BF16], "kv_lora_norm_fwd")[0]
    q = _mm(qn, big["w_uq"], tb=True)
    kv = _mm(kvn, big["w_ukv"], tb=True)
    qh, kh, vh = _heads_fwd_call(q, kv, proj, cs, sn, _head_gain(small["q_norm"]), _head_gain(small["k_norm"]))
    o, lse, *carried = _attn_fwd_call(qh, kh, vh, rider)
    o_rows = jnp.transpose(o, (1, 0, 2)).reshape(s, MLA_H * VDIM)
    y_mla = _mm(o_rows, big["w_mla_out"])
    mg = _row_fwd(_f_merge, [_win(proj, PROJ_GATES, 2 * D_MODEL), y_ssd, y_mla], [], [BF16], "merge_fwd")[0]
    out = _mm(mg, big["w_o"], res=h)
    saved = (h, u, proj, xbc, dt_in, dt, a, y_scan, states, dsk, yn, y_ssd, qn, kvn, q, kv, qh, kh, vh, o, lse, o_rows, y_mla, mg)
    return out, saved, (carried[0] if carried else None)


def _mixer_bwd(dh_out, saved, big, small, conv_w, cs, sn, rider=None):
    (h, u, proj, xbc, dt_in, dt, a, y_scan, states, dsk, yn, y_ssd, qn, kvn, q, kv, qh, kh, vh, o, lse, o_rows, y_mla, mg) = saved
    s = h.shape[0]
    d_big, d_small = {}, {}
    d_mg = _mm(dh_out, big["w_o"], tb=True, out_dtype=BF16)
    d_big["w_o"] = _mm(mg, dh_out, ta=True, out_dtype=BF16)
    merge_in = [_win(proj, PROJ_GATES, 2 * D_MODEL), y_ssd, y_mla]
    (d_gates, d_y_ssd, d_y_mla), _ = _row_bwd(_f_merge, merge_in, [], [d_mg], [BF16, BF16, BF16], "merge_bwd", bwd=_b_merge)
    d_o_rows = _mm(d_y_mla, big["w_mla_out"], tb=True, out_dtype=BF16)
    d_big["w_mla_out"] = _mm(o_rows, d_y_mla, ta=True, out_dtype=BF16)
    d_o = jnp.transpose(d_o_rows.reshape(s, MLA_H, VDIM), (1, 0, 2))
    delta = _attn_delta_call(o, d_o)
    *d_heads, carried = list(_attn_bwd_call(qh, kh, vh, d_o, lse.reshape(MLA_H, 1, s), delta.reshape(MLA_H, 1, s), rider)) + ([None] if rider is None else [])
    d_q, d_kv, d_kr, d_qg, d_kg = _heads_bwd_call(
        q, kv, proj, cs, sn, _head_gain(small["q_norm"]), _head_gain(small["k_norm"]), *d_heads)
    d_small["q_norm"], d_small["k_norm"] = d_qg[0, :QK], d_kg[0, :QK]
    d_qn = _mm(d_q, big["w_uq"], out_dtype=BF16)
    d_big["w_uq"] = _mm(d_q, qn, ta=True, out_dtype=BF16)
    d_kvn = _mm(d_kv, big["w_ukv"], out_dtype=BF16)
    d_big["w_ukv"] = _mm(d_kv, kvn, ta=True, out_dtype=BF16)
    (d_cq,), (d_g,) = _row_bwd(_f_rmsnorm, [_win(proj, PROJ_CQ, Q_LORA)], [_row(small["q_lora_norm"])], [d_qn], [BF16], "q_lora_norm_bwd")
    d_small["q_lora_norm"] = d_g[0]
    (d_ckv,), (d_g,) = _row_bwd(_f_rmsnorm, [_win(proj, PROJ_CKV, KV_LORA)], [_row(small["kv_lora_norm"])], [d_kvn], [BF16], "kv_lora_norm_bwd")
    d_small["kv_lora_norm"] = d_g[0]
    d_yn = _mm(d_y_ssd, big["w_ssd_out"], tb=True, out_dtype=BF16)
    d_big["w_ssd_out"] = _mm(yn, d_y_ssd, ta=True, out_dtype=BF16)
    gn_in = [y_scan, _win(xbc, 0, SSD_DI), _win(proj, PROJ_Z, SSD_DI)]
    (d_y_scan, d_xs, d_z), (d_dsk, d_g) = _row_bwd(
        _f_gated_norm, gn_in, [dsk, _row(small["ssd_norm"])], [d_yn], [F32, F32, BF16], "gated_norm_bwd")
    d_small["ssd_norm"] = d_g[0]
    d_small["d_skip"] = jnp.sum(d_dsk.reshape(SSD_H, SSD_P), axis=1)
    d_xbc_act, d_dt, d_a = _ssd_bwd_call(xbc, dt, a, states, d_y_scan, d_xs)
    d_xbc, d_conv_w, d_conv_b = _conv_bwd_call(proj, PROJ_XBC, conv_w, _row(small["conv_b"]), d_xbc_act)
    d_small["conv_b"] = d_conv_b[0]
    d_dt_in = d_dt * jax.nn.sigmoid(dt_in)
    d_small["dt_bias"] = jnp.sum(d_dt_in, axis=0)
    d_small["a_log"] = d_a[0] * a[0]
    d_last = (d_kr + jnp.pad(d_dt_in, ((0, 0), (ROPE, LANE - ROPE - SSD_H)))).astype(BF16)
    d_proj = jnp.concatenate([d_z, d_gates, d_xbc, d_cq, d_ckv, d_last], axis=1)
    d_u = _mm(d_proj, big["w_in"], out_dtype=BF16)
    d_big["w_in"] = _mm(d_proj, u, ta=True, out_dtype=BF16)
    (dh,), (d_ln,) = _row_bwd(_f_rmsnorm, [h], [_row(small["ln_mix"])], [d_u], [F32], "ln_mix_bwd", add={0: dh_out})
    d_small["ln_mix"] = d_ln[0]
    return dh, d_big, d_small, d_conv_w, carried


def _prepare_big(b):
    return dict(b, w_in=_arrange_w_in(b["w_in"]), w_uq=_pad_heads(b["w_uq"]))


def _local_step(x, positions, target, big, small, conv_w, packed_last=None):
    inv = 1.0 / (ROPE_THETA ** (jnp.arange(0, ROPE, 2, dtype=F32) / ROPE))
    ang = positions.astype(F32)[:, None] * inv
    cos, sin = jnp.cos(ang), jnp.sin(ang)
    no_lanes = jnp.zeros((x.shape[0], LANE - ROPE), F32)
    cs = jnp.concatenate([cos, cos, no_lanes], axis=1)
    sn = jnp.concatenate([-sin, sin, no_lanes], axis=1)
    carrier = DEPTH - 2 if packed_last is not None else None
    big = [None if b is None else _prepare_big(b) for b in big]
    layer_small = [{k: v[l] for k, v in small.items()} for l in range(DEPTH)]

    h, saved = x, []
    for l in range(DEPTH):
        b, sm = big[l], layer_small[l]
        h, s1 = _ffn_fwd(h, sm["ln_ffn1"], b["ffn1_w13"], b["ffn1_w2"], "ln_ffn1")
        h, s2, gathered = _mixer_fwd(h, b, sm, conv_w[l], cs, sn, _gather_rider(packed_last) if l == carrier else None)
        if gathered is not None:
            big[l + 1] = _prepare_big(_unpack_gathered(gathered))
        h, s3 = _ffn_fwd(h, sm["ln_ffn2"], b["ffn2_w13"], b["ffn2_w2"], "ln_ffn2")
        saved.append((s1, s2, s3))
    loss, dh = _loss_and_grad(h, target)

    d_big, d_small, d_conv_w = [None] * DEPTH, [None] * DEPTH, [None] * DEPTH
    for l in reversed(range(DEPTH)):
        b, sm = big[l], layer_small[l]
        s1, s2, s3 = saved[l]
        dh, d_w13_2, d_w2_2, d_ln2 = _ffn_bwd(dh, s3, sm["ln_ffn2"], b["ffn2_w13"], b["ffn2_w2"], "ln_ffn2")
        rider = _exchange_rider(_pack_full_grads(d_big[l + 1])) if l == carrier else None
        dh, db, ds, d_conv_w[l], received = _mixer_bwd(dh, s2, b, sm, conv_w[l], cs, sn, rider)
        if received is not None:
            d_big[l + 1] = received
        dh, d_w13_1, d_w2_1, d_ln1 = _ffn_bwd(dh, s1, sm["ln_ffn1"], b["ffn1_w13"], b["ffn1_w2"], "ln_ffn1")
        db.update(ffn1_w13=d_w13_1, ffn1_w2=d_w2_1, ffn2_w13=d_w13_2, ffn2_w2=d_w2_2,
                  w_in=_restore_w_in(db["w_in"]), w_uq=_unpad_heads(db["w_uq"]))
        ds.update(ln_ffn1=d_ln1, ln_ffn2=d_ln2)
        d_big[l], d_small[l] = db, ds
    d_small = {name: jnp.stack([d_small[l][name] for l in range(DEPTH)]) for name, _ in SMALL}
    return loss, dh, d_big, d_small, jnp.stack(d_conv_w)


def _step(args):
    dev = 4 * lax.axis_index("x") + 2 * lax.axis_index("y") + lax.axis_index("c")
    x, positions, target = args["x"][0], args["positions"][0], args["loss_target"][0]

    packed = [_pack_shards({name: args[name][l].astype(BF16) for name, _, _ in BIG}) for l in range(DEPTH)]
    big = [_unpack_gathered(_all_gather(packed[l])) for l in range(DEPTH - 1)] + [None]
    cw = args["conv_w"]
    cw_cols = cw.shape[-1]
    cw_rows = -(-cw.size // (8 * PACK_COLS)) * 8
    cw_flat = jnp.concatenate([cw.reshape(-1), jnp.zeros((cw_rows * PACK_COLS - cw.size,), F32)]).reshape(cw_rows, PACK_COLS)
    cw_all = _all_gather(cw_flat).reshape(N_DEV, -1)[:, :cw.size].reshape(N_DEV, DEPTH, CONV_K, cw_cols)
    conv_w = jnp.transpose(cw_all, (1, 2, 0, 3)).reshape(DEPTH, CONV_K, CONV_DIM)
    small = {name: args[name] for name, _ in SMALL}

    loss, dx, d_big, d_small, d_conv_w = _local_step(x, positions, target, big, small, conv_w, packed_last=packed[-1])
    loss = lax.psum(loss, MESH_AXES)

    out = {"loss": loss, "grad_x": dx[None]}

    grads = {name: [] for name, _, _ in BIG}
    for l in range(DEPTH):
        received = d_big[l] if l == DEPTH - 1 else _exchange_blocks(_pack_full_grads(d_big[l]))
        summed = _sum_blocks(received)
        for name, g in _unpack_shards(summed).items():
            grads[name].append(g)
    flat = lambda t: t.reshape(-1, t.shape[-1])
    for name, _, _ in BIG:
        g = jnp.stack(grads[name])
        w = args[name]
        delta, m2, v2 = _adam(flat(w), flat(g), flat(args["m_" + name]), flat(args["v_" + name]))
        out["grad_" + name] = g
        out["delta_" + name] = delta.reshape(w.shape)
        out["new_m_" + name] = m2.reshape(w.shape)
        out["new_v_" + name] = v2.reshape(w.shape)

    total = _sum_blocks(_all_gather(_pack_small(d_small, d_conv_w)))
    g_conv_w = _unpack_small(total)[1]
    zeros_cw = jnp.zeros((DEPTH, CONV_K, CONV_DIM), F32)
    delta, m2, v2 = _adam(_pack_small(small, zeros_cw), total,
                          _pack_small({name: args["m_" + name] for name, _ in SMALL}, zeros_cw),
                          _pack_small({name: args["v_" + name] for name, _ in SMALL}, zeros_cw))
    for kind, packed in (("grad_", total), ("delta_", delta), ("new_m_", m2), ("new_v_", v2)):
        for name, val in _unpack_small(packed)[0].items():
            out[kind + name] = val
    g_cw = lax.dynamic_slice_in_dim(g_conv_w, dev * cw_cols, cw_cols, axis=2)
    delta, m2, v2 = _adam(flat(cw), flat(g_cw), flat(args["m_conv_w"]), flat(args["v_conv_w"]))
    out["grad_conv_w"] = g_cw
    out["delta_conv_w"] = delta.reshape(cw.shape)
    out["new_m_conv_w"] = m2.reshape(cw.shape)
    out["new_v_conv_w"] = v2.reshape(cw.shape)
    return out


WEIGHTS = ["ln_ffn1", "ffn1_w13", "ffn1_w2", "ln_mix", "w_in", "conv_w", "conv_b", "dt_bias", "a_log", "d_skip", "ssd_norm",
           "w_ssd_out", "q_lora_norm", "w_uq", "kv_lora_norm", "w_ukv", "q_norm", "k_norm", "w_mla_out", "w_o", "ln_ffn2",
           "ffn2_w13", "ffn2_w2"]
ARG_NAMES = (["x", "positions"] + WEIGHTS + ["loss_target"] + ["m_" + n for n in WEIGHTS] + ["v_" + n for n in WEIGHTS])


def kernel(x, positions, ln_ffn1, ffn1_w13, ffn1_w2, ln_mix, w_in, conv_w, conv_b, dt_bias, a_log, d_skip, ssd_norm, w_ssd_out, q_lora_norm, w_uq, kv_lora_norm, w_ukv, q_norm, k_norm, w_mla_out, w_o, ln_ffn2, ffn2_w13, ffn2_w2, loss_target, m_ln_ffn1, m_ffn1_w13, m_ffn1_w2, m_ln_mix, m_w_in, m_conv_w, m_conv_b, m_dt_bias, m_a_log, m_d_skip, m_ssd_norm, m_w_ssd_out, m_q_lora_norm, m_w_uq, m_kv_lora_norm, m_w_ukv, m_q_norm, m_k_norm, m_w_mla_out, m_w_o, m_ln_ffn2, m_ffn2_w13, m_ffn2_w2, v_ln_ffn1, v_ffn1_w13, v_ffn1_w2, v_ln_mix, v_w_in, v_conv_w, v_conv_b, v_dt_bias, v_a_log, v_d_skip, v_ssd_norm, v_w_ssd_out, v_q_lora_norm, v_w_uq, v_kv_lora_norm, v_w_ukv, v_q_norm, v_k_norm, v_w_mla_out, v_w_o, v_ln_ffn2, v_ffn2_w13, v_ffn2_w2):
    vals = (x, positions, ln_ffn1, ffn1_w13, ffn1_w2, ln_mix, w_in, conv_w, conv_b, dt_bias, a_log, d_skip, ssd_norm, w_ssd_out, q_lora_norm, w_uq, kv_lora_norm, w_ukv, q_norm, k_norm, w_mla_out, w_o, ln_ffn2, ffn2_w13, ffn2_w2, loss_target, m_ln_ffn1, m_ffn1_w13, m_ffn1_w2, m_ln_mix, m_w_in, m_conv_w, m_conv_b, m_dt_bias, m_a_log, m_d_skip, m_ssd_norm, m_w_ssd_out, m_q_lora_norm, m_w_uq, m_kv_lora_norm, m_w_ukv, m_q_norm, m_k_norm, m_w_mla_out, m_w_o, m_ln_ffn2, m_ffn2_w13, m_ffn2_w2, v_ln_ffn1, v_ffn1_w13, v_ffn1_w2, v_ln_mix, v_w_in, v_conv_w, v_conv_b, v_dt_bias, v_a_log, v_d_skip, v_ssd_norm, v_w_ssd_out, v_q_lora_norm, v_w_uq, v_kv_lora_norm, v_w_ukv, v_q_norm, v_k_norm, v_w_mla_out, v_w_o, v_ln_ffn2, v_ffn2_w13, v_ffn2_w2)
    out = _step(dict(zip(ARG_NAMES, vals)))
    order = ["loss", "grad_x"] + [k + n for k in ("grad_", "delta_", "new_m_", "new_v_") for n in WEIGHTS]
    return tuple(out[n] for n in order)
```

```python
import jax
import jax.numpy as jnp
from jax import lax
from jax.experimental import pallas as pl
from jax.experimental.pallas import tpu as pltpu

F32 = jnp.float32
BF16 = jnp.bfloat16

D_MODEL = 1024
D_FF = 2816
DEPTH = 2
SSD_DI = 2048
SSD_P = 64
SSD_H = 32
SSD_G = 4
SSD_HPG = 8
SSD_N = 128
SSD_L = 128
CONV_K = 4
CONV_DIM = 3072
MLA_H = 8
Q_LORA = 512
KV_LORA = 256
NOPE = 128
ROPE = 64
VDIM = 128
QK = 192
ROPE_THETA = 10000.0
EPS = 1e-6
IN_SPLIT = (SSD_DI, CONV_DIM, SSD_H, Q_LORA, KV_LORA, ROPE, 2 * D_MODEL)
D_IN = sum(IN_SPLIT)
N_DEV = 8
LANE = 128
PACK_COLS = 1024

PROJ_Z = 0
PROJ_GATES = PROJ_Z + SSD_DI
PROJ_XBC = PROJ_GATES + 2 * D_MODEL
PROJ_CQ = PROJ_XBC + CONV_DIM
PROJ_CKV = PROJ_CQ + Q_LORA
PROJ_LAST = PROJ_CKV + KV_LORA
D_IN_PAD = PROJ_LAST + LANE

ADAM_LR = 0.001
ADAM_B1 = 0.9
ADAM_B2 = 0.999
ADAM_EPS = 1e-08
ADAM_WD = 0.01
ADAM_STEP = 10

VMEM_LIMIT = 48 * 1024 * 1024
ROW_IO_BUDGET = 8 * 1024 * 1024
NEG = -1e30

MESH_AXES = ("x", "y", "c")


def _cparams(*sem):
    return pltpu.CompilerParams(dimension_semantics=sem, vmem_limit_bytes=VMEM_LIMIT)


def _pick_tile(n, target, align):
    if n <= target:
        return n
    best = None
    for t in range(align, target + 1, align):
        if n % t == 0:
            best = t
    assert best is not None, (n, target, align)
    return best


def _acc_store(ref, val, first):
    @pl.when(first)
    def _():
        ref[...] = val

    @pl.when(jnp.logical_not(first))
    def _():
        ref[...] += val


def _win(arr, start, width):
    assert start % width == 0, (start, width)
    return (arr, start, width)


def _operand(entry):
    if isinstance(entry, tuple):
        arr, start, width = entry
        return arr, width, start // width
    return entry, entry.shape[1], 0


def _row_tile(rows, bytes_per_row):
    if rows <= 16:
        return rows
    t = 1024
    while t > 16 and (t * bytes_per_row > ROW_IO_BUDGET or rows % t):
        t //= 2
    assert rows % t == 0, (rows, t)
    return t


def _rowwise_call(fn, tiled, params, outs, accs, name):
    ops = [_operand(e) for e in tiled]
    rows = ops[0][0].shape[0]
    per_row = sum(w * a.dtype.itemsize for a, w, _ in ops) + sum(c * jnp.dtype(d).itemsize for c, d in outs)
    tile = _row_tile(rows, per_row)
    n_in = len(tiled) + len(params)
    n_o = len(outs)

    def body(*refs):
        vals = [r[...] for r in refs[:n_in]]
        t_out, a_out = fn(*vals)
        for r, v in zip(refs[n_in:n_in + n_o], t_out):
            r[...] = v.astype(r.dtype)
        first = pl.program_id(0) == 0
        for r, v in zip(refs[n_in + n_o:], a_out):
            _acc_store(r, v.astype(F32), first)

    def tiled_spec(width, blk):
        return pl.BlockSpec((tile, width), lambda i: (i, blk))

    in_specs = [tiled_spec(w, blk) for _, w, blk in ops]
    in_specs += [pl.BlockSpec(p.shape, lambda i: (0, 0)) for p in params]
    out_specs = [tiled_spec(c, 0) for c, _ in outs]
    out_specs += [pl.BlockSpec(s, lambda i: (0, 0)) for s in accs]
    out_shape = [jax.ShapeDtypeStruct((rows, c), d) for c, d in outs]
    out_shape += [jax.ShapeDtypeStruct(s, F32) for s in accs]
    return pl.pallas_call(
        body, grid=(rows // tile,), in_specs=in_specs, out_specs=out_specs, out_shape=out_shape,
        compiler_params=_cparams("arbitrary"), name=name,
    )(*[a for a, _, _ in ops], *params)


def _to_f32(vals):
    return [v.astype(F32) for v in vals]


def _row_fwd(f, tiled, params, out_dtypes, name):
    ops = [_operand(e) for e in tiled]
    rows = ops[0][0].shape[0]
    shapes = jax.eval_shape(f, *[jax.ShapeDtypeStruct((rows, w), F32) for _, w, _ in ops],
                            *[jax.ShapeDtypeStruct(p.shape, F32) for p in params])
    outs = [(s.shape[1], d) for s, d in zip(shapes, out_dtypes)]
    return _rowwise_call(lambda *v: (f(*_to_f32(v)), ()), tiled, params, outs, [], name)


def _row_bwd(f, tiled, params, gs, d_dtypes, name, bwd=None, add=None):
    n_t, n_g = len(tiled), len(gs)
    adds = sorted((add or {}).items())
    n_a = len(adds)

    def fn(*vals):
        vals = _to_f32(vals)
        prim = vals[:n_t] + vals[n_t + n_g + n_a:]
        g = tuple(vals[n_t:n_t + n_g])
        if bwd is not None:
            d_t, d_p = bwd(*prim, *g)
        else:
            _, vjp = jax.vjp(f, *prim)
            cts = vjp(g)
            d_t, d_p = cts[:n_t], cts[n_t:]
        d_t = list(d_t)
        for (idx, _), extra in zip(adds, vals[n_t + n_g:n_t + n_g + n_a]):
            d_t[idx] = d_t[idx] + extra
        return tuple(d_t), tuple(d_p)

    outs = [(_operand(e)[1], d) for e, d in zip(tiled, d_dtypes)]
    accs = [p.shape for p in params]
    res = _rowwise_call(fn, list(tiled) + list(gs) + [a for _, a in adds], params, outs, accs, name)
    return res[:n_t], res[n_t:]


def _f_rmsnorm(x, g):
    return (x * lax.rsqrt(jnp.mean(x * x, axis=-1, keepdims=True) + EPS) * g,)


def _f_gated_norm(ys, xs, z, dsk, g):
    t = (ys + xs * dsk) * (z * jax.nn.sigmoid(z))
    return (t * lax.rsqrt(jnp.mean(t * t, axis=-1, keepdims=True) + EPS) * g,)


def _f_merge(gates, ys, ym):
    s = jax.nn.sigmoid(gates)
    return (s[:, :D_MODEL] * ys + s[:, D_MODEL:] * ym,)


def _b_merge(gates, ys, ym, d):
    s = jax.nn.sigmoid(gates)
    s1, s2 = s[:, :D_MODEL], s[:, D_MODEL:]
    d_gates = jnp.concatenate([d * ys * s1 * (1.0 - s1), d * ym * s2 * (1.0 - s2)], axis=1)
    return (d_gates, d * s1, d * s2), ()


def _loss_and_grad(y, target):
    def fn(yv, tv):
        d = yv - tv
        return (d * (1.0 / D_MODEL),), (jnp.sum(d * d, axis=0, keepdims=True) * (0.5 / D_MODEL),)

    dy, part = _rowwise_call(fn, [y, target], [], [(D_MODEL, F32)], [(1, D_MODEL)], "loss")
    return jnp.sum(part), dy


def _adam(w, g, m, v):
    def fn(wv, gv, mv, vv):
        m2 = ADAM_B1 * mv + (1.0 - ADAM_B1) * gv
        v2 = ADAM_B2 * vv + (1.0 - ADAM_B2) * (gv * gv)
        m_hat = m2 / (1.0 - ADAM_B1 ** ADAM_STEP)
        v_hat = v2 / (1.0 - ADAM_B2 ** ADAM_STEP)
        delta = -ADAM_LR * (m_hat / (jnp.sqrt(v_hat) + ADAM_EPS) + ADAM_WD * wv)
        return (delta, m2, v2), ()

    c = w.shape[1]
    return _rowwise_call(fn, [w, g, m, v], [], [(c, F32)] * 3, [], "adamw")


def _sum_blocks(blocks):
    _, rows, c = blocks.shape
    tile = _row_tile(rows, N_DEV * c * blocks.dtype.itemsize + c * 4)

    def body(b_ref, o_ref):
        acc = b_ref[0].astype(F32)
        for i in range(1, N_DEV):
            acc = acc + b_ref[i].astype(F32)
        o_ref[...] = acc

    return pl.pallas_call(
        body, grid=(rows // tile,), in_specs=[pl.BlockSpec((N_DEV, tile, c), lambda i: (0, i, 0))],
        out_specs=pl.BlockSpec((tile, c), lambda i: (i, 0)), out_shape=jax.ShapeDtypeStruct((rows, c), F32),
        compiler_params=_cparams("arbitrary"), name="sum_blocks",
    )(blocks)


def _mm(a, b, ta=False, tb=False, out_dtype=F32, alpha=1.0, res=None, b_rows=None):
    r_dim, p_dim = a.shape if ta else a.shape[::-1]
    b_row0, b_nrows = (0, b.shape[0]) if b_rows is None else b_rows
    r2, q_dim = (b.shape[1], b_nrows) if tb else (b_nrows, b.shape[1])
    assert r_dim == r2, (a.shape, b.shape, ta, tb)
    tp = _pick_tile(p_dim, 512, LANE)
    if tp < 512 < p_dim:
        tp = _pick_tile(p_dim, 1536, LANE)
    tq = _pick_tile(q_dim, 1536, LANE)
    tr = _pick_tile(r_dim, 1536, LANE)
    nr = r_dim // tr
    dims = (((0 if ta else 1,), (1 if tb else 0,)), ((), ()))
    has_res = res is not None

    def body(*refs):
        a_ref, b_ref = refs[:2]
        res_ref = refs[2] if has_res else None
        o_ref = refs[2 + has_res]

        def finish(val):
            if alpha != 1.0:
                val = val * alpha
            if has_res:
                val = val + res_ref[...].astype(F32)
            o_ref[...] = val.astype(o_ref.dtype)

        part = lax.dot_general(a_ref[...].astype(BF16), b_ref[...].astype(BF16), dims, preferred_element_type=F32)
        if nr == 1:
            finish(part)
        else:
            acc_ref = refs[3 + has_res]
            k = pl.program_id(2)
            _acc_store(acc_ref, part, k == 0)

            @pl.when(k == nr - 1)
            def _():
                finish(acc_ref[...])

    a_spec = pl.BlockSpec((tr, tp), lambda j, i, k: (k, i)) if ta else pl.BlockSpec((tp, tr), lambda j, i, k: (i, k))
    assert b_row0 % (tq if tb else tr) == 0
    b0 = b_row0 // (tq if tb else tr)
    b_spec = pl.BlockSpec((tq, tr), lambda j, i, k: (j + b0, k)) if tb else pl.BlockSpec((tr, tq), lambda j, i, k: (k + b0, j))
    o_spec = pl.BlockSpec((tp, tq), lambda j, i, k: (i, j))
    return pl.pallas_call(
        body, grid=(q_dim // tq, p_dim // tp, nr), in_specs=[a_spec, b_spec] + ([o_spec] if has_res else []),
        out_specs=o_spec, out_shape=jax.ShapeDtypeStruct((p_dim, q_dim), out_dtype),
        scratch_shapes=[pltpu.VMEM((tp, tq), F32)] if nr > 1 else [],
        compiler_params=_cparams("arbitrary", "arbitrary", "arbitrary"),
        name=f"mm_{'t' if ta else 'n'}{'t' if tb else 'n'}_{p_dim}x{r_dim}x{q_dim}",
    )(*([a, b] + ([res] if has_res else [])))


FFN_TP = 512
FFN_TQ = 1408


def _ffn_up_call(n, w13_t):
    s, d = n.shape
    tp = min(FFN_TP, s)
    up0 = D_FF // FFN_TQ

    def body(n_ref, wg_ref, wu_ref, act_ref, gate_ref, up_ref):
        a = n_ref[...]
        g = _nt(a, wg_ref[...])
        u = _nt(a, wu_ref[...])
        act_ref[...] = (g * jax.nn.sigmoid(g) * u).astype(BF16)
        gate_ref[...] = g.astype(BF16)
        up_ref[...] = u.astype(BF16)

    o_spec = pl.BlockSpec((tp, FFN_TQ), lambda j, i: (i, j))
    return pl.pallas_call(
        body, grid=(D_FF // FFN_TQ, s // tp),
        in_specs=[pl.BlockSpec((tp, d), lambda j, i: (i, 0)), pl.BlockSpec((FFN_TQ, d), lambda j, i: (j, 0)),
                  pl.BlockSpec((FFN_TQ, d), lambda j, i: (j + up0, 0))],
        out_specs=[o_spec] * 3, out_shape=[jax.ShapeDtypeStruct((s, D_FF), BF16)] * 3,
        compiler_params=_cparams("arbitrary", "arbitrary"), name="ffn_up_swiglu",
    )(n, w13_t, w13_t)


def _ffn_down_bwd_call(dh, w2, gate, up):
    s, d = dh.shape
    tp = min(FFN_TP, s)

    def body(dh_ref, w2_ref, gate_ref, up_ref, dg_ref, du_ref):
        d_act = 0.5 * _nt(dh_ref[...].astype(BF16), w2_ref[...])
        g, u = gate_ref[...].astype(F32), up_ref[...].astype(F32)
        sg = jax.nn.sigmoid(g)
        dg_ref[...] = (d_act * u * sg * (1.0 + g * (1.0 - sg))).astype(BF16)
        du_ref[...] = (d_act * g * sg).astype(BF16)

    o_spec = pl.BlockSpec((tp, FFN_TQ), lambda j, i: (i, j))
    return pl.pallas_call(
        body, grid=(D_FF // FFN_TQ, s // tp),
        in_specs=[pl.BlockSpec((tp, d), lambda j, i: (i, 0)), pl.BlockSpec((FFN_TQ, d), lambda j, i: (j, 0)), o_spec, o_spec],
        out_specs=[o_spec] * 2, out_shape=[jax.ShapeDtypeStruct((s, D_FF), BF16)] * 2,
        compiler_params=_cparams("arbitrary", "arbitrary"), name="ffn_down_bwd_swiglu",
    )(dh, w2, gate, up)


ATTN_SCALE = QK ** -0.5
LOG2E = 1.4426950408889634
ATTN_C = ATTN_SCALE * LOG2E


ATTN_HEADS = 2


def _attn_tile(s):
    return min(512, s)


def _causal_keep(t, keys_on_rows=False):
    row = lax.broadcasted_iota(jnp.int32, (t, t), 0)
    col = lax.broadcasted_iota(jnp.int32, (t, t), 1)
    return row <= col if keys_on_rows else col <= row


def _nt(a, b):
    return lax.dot_general(a, b, (((1,), (1,)), ((), ())), preferred_element_type=F32)


def _rider_phases(rider, src_ref, out_ref, sems, first, last):
    @pl.when(first)
    def _():
        _peer_copies(src_ref, out_ref, sems, rider["gather"], "start")

    def finish():
        @pl.when(last)
        def _():
            _peer_copies(src_ref, out_ref, sems, rider["gather"], "finish")

    return finish


def _attn_fwd_call(q, k, v, rider=None):
    nh, s, _ = q.shape
    t = _attn_tile(s)
    nb = s // t
    hp = ATTN_HEADS
    n_r = 0 if rider is None else 1

    def body(*refs):
        q_ref, k_ref, v_ref = refs[:3]
        o_ref, lse_ref = refs[3 + n_r:5 + n_r]
        qi = pl.program_id(1)
        finish = None
        if rider is not None:
            h = pl.program_id(0)
            finish = _rider_phases(rider, refs[3], refs[5 + n_r], refs[6 + n_r:],
                                   jnp.logical_and(h == 0, qi == 0), jnp.logical_and(h == nh // hp - 1, qi == nb - 1))
        qs = [q_ref[i] for i in range(hp)]

        def block(kb, carries, diagonal, width=1):
            start = pl.multiple_of(kb * t, t)
            out = []
            for i, (m_prev, l_prev, acc) in enumerate(carries):
                sc = _nt(qs[i], k_ref[i, pl.ds(start, width * t), :])
                if diagonal:
                    sc = jnp.where(_causal_keep(t), sc, NEG)
                m_new = jnp.maximum(m_prev, jnp.max(sc, axis=-1, keepdims=True))
                p = jnp.exp2(sc * ATTN_C - m_new * ATTN_C)
                alpha = jnp.exp2((m_prev - m_new) * ATTN_C)
                l_new = alpha * l_prev + jnp.sum(p, axis=-1, keepdims=True)
                pv = jnp.dot(p.astype(BF16), v_ref[i, pl.ds(start, width * t), :], preferred_element_type=F32)
                out.append((m_new, l_new, alpha * acc + pv))
            return tuple(out)

        init = tuple((jnp.full((t, 1), NEG, F32), jnp.zeros((t, 1), F32), jnp.zeros((t, VDIM), F32)) for _ in range(hp))
        carries = lax.fori_loop(0, qi // 2, lambda j, c: block(2 * j, c, False, width=2), init)
        carries = lax.cond(qi % 2 == 1, lambda c: block(qi - 1, c, False), lambda c: c, carries)
        for i, (m, l, acc) in enumerate(block(qi, carries, True)):
            o_ref[i] = (acc / l).astype(o_ref.dtype)
            lse_ref[i] = m * ATTN_SCALE + jnp.log(l)
        if finish is not None:
            finish()

    qmap = lambda h, i: (h, i, 0)
    whole = lambda h, i: (h, 0, 0)
    return pl.pallas_call(
        body, grid=(nh // hp, nb),
        in_specs=[pl.BlockSpec((hp, t, QK), qmap), pl.BlockSpec((hp, s, QK), whole), pl.BlockSpec((hp, s, VDIM), whole)] + [HBM_SPEC] * n_r,
        out_specs=[pl.BlockSpec((hp, t, VDIM), qmap), pl.BlockSpec((hp, t, 1), qmap)] + [HBM_SPEC] * n_r,
        out_shape=[jax.ShapeDtypeStruct((nh, s, VDIM), BF16), jax.ShapeDtypeStruct((nh, s, 1), F32)] + ([rider["out"]] if n_r else []),
        scratch_shapes=_comm_scratch() if n_r else [],
        compiler_params=_cparams("arbitrary", "arbitrary"), name="attn_fwd_gather" if n_r else "attn_fwd",
    )(*([q, k, v] + ([rider["src"]] if n_r else [])))


def _attn_delta_call(o, do):
    nh, s, d = o.shape

    def fn(ov, dv):
        return (jnp.sum(ov.astype(F32) * dv.astype(F32), axis=-1, keepdims=True),), ()

    return _rowwise_call(fn, [o.reshape(nh * s, d), do.reshape(nh * s, d)], [], [(1, F32)], [], "attn_delta")[0]


def _attn_bwd_call(q, k, v, do, lse_t, delta_t, rider=None):
    nh, s, _ = q.shape
    t = _attn_tile(s)
    nb = s // t
    n_r = 0 if rider is None else 1

    def body(*refs):
        q_ref, k_ref, v_ref, do_ref, lse_ref, delta_ref = refs[:6]
        dq_ref, dk_ref, dv_ref = refs[6 + n_r:9 + n_r]
        dk_sc, dv_sc = refs[9 + 2 * n_r:11 + 2 * n_r]
        kj = pl.program_id(1)
        finish = None
        if rider is not None:
            h = pl.program_id(0)
            finish = _rider_phases(rider, refs[6], refs[9 + n_r], refs[11 + 2 * n_r:],
                                   jnp.logical_and(h == 0, kj == 0), jnp.logical_and(h == nh - 1, kj == nb - 1))

        @pl.when(kj == 0)
        def _():
            dq_ref[...] = jnp.zeros_like(dq_ref)

        dk_sc[...] = jnp.zeros_like(dk_sc)
        dv_sc[...] = jnp.zeros_like(dv_sc)
        kblk, vblk = k_ref[0], v_ref[0]

        def block(qb, diagonal):
            start = pl.multiple_of(qb * t, t)
            qblk = q_ref[0, pl.ds(start, t), :]
            doblk = do_ref[0, pl.ds(start, t), :]
            sc = _nt(kblk, qblk)
            if diagonal:
                sc = jnp.where(_causal_keep(t, keys_on_rows=True), sc, NEG)
            p = jnp.exp2(sc * ATTN_C - lse_ref[0, :, pl.ds(start, t)] * LOG2E)
            dv_sc[...] += jnp.dot(p.astype(BF16), doblk, preferred_element_type=F32)
            dp = _nt(vblk, doblk)
            ds = (p * (dp - delta_ref[0, :, pl.ds(start, t)])).astype(BF16)
            dk_sc[...] += jnp.dot(ds, qblk, preferred_element_type=F32)
            dq_ref[0, pl.ds(start, t), :] += lax.dot_general(ds, kblk, (((0,), (0,)), ((), ())), preferred_element_type=F32)

        block(kj, True)

        def rest(qb, carry):
            block(qb, False)
            return carry

        lax.fori_loop(kj + 1, nb, rest, 0)
        dk_ref[0] = (dk_sc[...] * ATTN_SCALE).astype(dk_ref.dtype)
        dv_ref[0] = dv_sc[...].astype(dv_ref.dtype)

        @pl.when(kj == nb - 1)
        def _():
            dq_ref[...] = dq_ref[...] * ATTN_SCALE

        if finish is not None:
            finish()

    kmap = lambda h, j: (h, j, 0)
    whole = lambda h, j: (h, 0, 0)
    return pl.pallas_call(
        body, grid=(nh, nb),
        in_specs=[pl.BlockSpec((1, s, QK), whole), pl.BlockSpec((1, t, QK), kmap), pl.BlockSpec((1, t, VDIM), kmap),
                  pl.BlockSpec((1, s, VDIM), whole), pl.BlockSpec((1, 1, s), whole), pl.BlockSpec((1, 1, s), whole)] + [HBM_SPEC] * n_r,
        out_specs=[pl.BlockSpec((1, s, QK), whole), pl.BlockSpec((1, t, QK), kmap), pl.BlockSpec((1, t, VDIM), kmap)] + [HBM_SPEC] * n_r,
        out_shape=[jax.ShapeDtypeStruct((nh, s, QK), F32), jax.ShapeDtypeStruct((nh, s, QK), F32),
                   jax.ShapeDtypeStruct((nh, s, VDIM), F32)] + ([rider["out"]] if n_r else []),
        scratch_shapes=[pltpu.VMEM((t, QK), F32), pltpu.VMEM((t, VDIM), F32)] + (_comm_scratch() if n_r else []),
        compiler_params=_cparams("arbitrary", "arbitrary"), name="attn_bwd_exchange" if n_r else "attn_bwd",
    )(*([q, k, v, do, lse_t, delta_t] + ([rider["src"]] if n_r else [])))


HEAD_COLS = NOPE + VDIM
HEADS_TILE = 256


def _swap_rope_halves(t, lane):
    half = ROPE // 2
    return jnp.where(lane < half, pltpu.roll(t, LANE - half, 1), pltpu.roll(t, half, 1))


def _head_fwd(n, p, gain, cs, sn, lane):
    r = lax.rsqrt((jnp.sum(n * n, axis=-1, keepdims=True) + jnp.sum(p * p, axis=-1, keepdims=True)) * (1.0 / QK) + EPS)
    yp = p * r * gain[:, NOPE:]
    return n * r * gain[:, :NOPE], yp * cs + _swap_rope_halves(yp, lane) * sn


def _head_bwd(n, p, gain, cs, sn, lane, dzn, dzp):
    r = lax.rsqrt((jnp.sum(n * n, axis=-1, keepdims=True) + jnp.sum(p * p, axis=-1, keepdims=True)) * (1.0 / QK) + EPS)
    dyp = dzp * cs + _swap_rope_halves(dzp * sn, lane)
    gyn, gyp = dzn * gain[:, :NOPE], dyp * gain[:, NOPE:]
    dot = jnp.sum(gyn * n, axis=-1, keepdims=True) + jnp.sum(gyp * p, axis=-1, keepdims=True)
    coef = dot * (r * r * r) * (1.0 / QK)
    d_gn = jnp.sum(dzn * n * r, axis=0, keepdims=True)
    d_gp = jnp.sum(dyp * p * r, axis=0, keepdims=True)
    return gyn * r - n * coef, gyp * r - p * coef, d_gn, d_gp


def _heads_fwd_call(q, kv, proj, cs, sn, q_gain, k_gain):
    s = q.shape[0]
    t = min(HEADS_TILE, s)

    def body(q_ref, kv_ref, last_ref, cs_ref, sn_ref, qg_ref, kg_ref, qh_ref, kh_ref, vh_ref):
        lane = lax.broadcasted_iota(jnp.int32, (t, LANE), 1)
        cs_, sn_ = cs_ref[...], sn_ref[...]
        kp = jnp.where(lane < ROPE, last_ref[...], 0.0)
        for h in range(MLA_H):
            c0 = h * HEAD_COLS
            zn, zp = _head_fwd(q_ref[:, c0:c0 + NOPE], q_ref[:, c0 + NOPE:c0 + HEAD_COLS], qg_ref[...], cs_, sn_, lane)
            qh_ref[h, :, :NOPE] = zn.astype(BF16)
            qh_ref[h, :, NOPE:] = zp[:, :ROPE].astype(BF16)
            zn, zp = _head_fwd(kv_ref[:, c0:c0 + NOPE], kp, kg_ref[...], cs_, sn_, lane)
            kh_ref[h, :, :NOPE] = zn.astype(BF16)
            kh_ref[h, :, NOPE:] = zp[:, :ROPE].astype(BF16)
            vh_ref[h] = kv_ref[:, c0 + NOPE:c0 + HEAD_COLS].astype(BF16)

    rows = lambda i: (i, 0)
    whole = lambda i: (0, 0)
    heads = lambda i: (0, i, 0)
    wide = MLA_H * HEAD_COLS
    return pl.pallas_call(
        body, grid=(s // t,),
        in_specs=[pl.BlockSpec((t, wide), rows), pl.BlockSpec((t, wide), rows),
                  pl.BlockSpec((t, LANE), lambda i: (i, PROJ_LAST // LANE)),
                  pl.BlockSpec((t, LANE), rows), pl.BlockSpec((t, LANE), rows),
                  pl.BlockSpec((1, HEAD_COLS), whole), pl.BlockSpec((1, HEAD_COLS), whole)],
        out_specs=[pl.BlockSpec((MLA_H, t, QK), heads), pl.BlockSpec((MLA_H, t, QK), heads), pl.BlockSpec((MLA_H, t, VDIM), heads)],
        out_shape=[jax.ShapeDtypeStruct((MLA_H, s, QK), BF16), jax.ShapeDtypeStruct((MLA_H, s, QK), BF16),
                   jax.ShapeDtypeStruct((MLA_H, s, VDIM), BF16)],
        compiler_params=_cparams("arbitrary"), name="mla_heads_fwd",
    )(q, kv, proj, cs, sn, q_gain, k_gain)


def _heads_bwd_call(q, kv, proj, cs, sn, q_gain, k_gain, dqh, dkh, dvh):
    s = q.shape[0]
    t = min(HEADS_TILE, s)

    def body(q_ref, kv_ref, last_ref, cs_ref, sn_ref, qg_ref, kg_ref, dqh_ref, dkh_ref, dvh_ref,
             dq_ref, dkv_ref, dkr_ref, dqg_ref, dkg_ref):
        lane = lax.broadcasted_iota(jnp.int32, (t, LANE), 1)
        cs_, sn_ = cs_ref[...], sn_ref[...]
        kp = jnp.where(lane < ROPE, last_ref[...], 0.0)
        no_lanes = jnp.zeros((t, LANE - ROPE), F32)
        d_kp = jnp.zeros((t, LANE), F32)
        d_qg = [jnp.zeros((1, NOPE), F32), jnp.zeros((1, LANE), F32)]
        d_kg = [jnp.zeros((1, NOPE), F32), jnp.zeros((1, LANE), F32)]
        for h in range(MLA_H):
            c0 = h * HEAD_COLS
            dz = dqh_ref[h]
            dzp = jnp.concatenate([dz[:, NOPE:], no_lanes], axis=1)
            d_n, d_p, g_n, g_p = _head_bwd(q_ref[:, c0:c0 + NOPE], q_ref[:, c0 + NOPE:c0 + HEAD_COLS], qg_ref[...],
                                           cs_, sn_, lane, dz[:, :NOPE], dzp)
            dq_ref[:, c0:c0 + NOPE] = d_n.astype(dq_ref.dtype)
            dq_ref[:, c0 + NOPE:c0 + HEAD_COLS] = d_p.astype(dq_ref.dtype)
            d_qg = [d_qg[0] + g_n, d_qg[1] + g_p]
            dz = dkh_ref[h]
            dzp = jnp.concatenate([dz[:, NOPE:], no_lanes], axis=1)
            d_n, d_p, g_n, g_p = _head_bwd(kv_ref[:, c0:c0 + NOPE], kp, kg_ref[...], cs_, sn_, lane, dz[:, :NOPE], dzp)
            dkv_ref[:, c0:c0 + NOPE] = d_n.astype(dkv_ref.dtype)
            dkv_ref[:, c0 + NOPE:c0 + HEAD_COLS] = dvh_ref[h].astype(dkv_ref.dtype)
            d_kp = d_kp + d_p
            d_kg = [d_kg[0] + g_n, d_kg[1] + g_p]
        dkr_ref[...] = d_kp
        first = pl.program_id(0) == 0
        _acc_store(dqg_ref.at[:, pl.ds(0, NOPE)], d_qg[0], first)
        _acc_store(dqg_ref.at[:, pl.ds(NOPE, LANE)], d_qg[1], first)
        _acc_store(dkg_ref.at[:, pl.ds(0, NOPE)], d_kg[0], first)
        _acc_store(dkg_ref.at[:, pl.ds(NOPE, LANE)], d_kg[1], first)

    rows = lambda i: (i, 0)
    whole = lambda i: (0, 0)
    heads = lambda i: (0, i, 0)
    wide = MLA_H * HEAD_COLS
    return pl.pallas_call(
        body, grid=(s // t,),
        in_specs=[pl.BlockSpec((t, wide), rows), pl.BlockSpec((t, wide), rows),
                  pl.BlockSpec((t, LANE), lambda i: (i, PROJ_LAST // LANE)),
                  pl.BlockSpec((t, LANE), rows), pl.BlockSpec((t, LANE), rows),
                  pl.BlockSpec((1, HEAD_COLS), whole), pl.BlockSpec((1, HEAD_COLS), whole),
                  pl.BlockSpec((MLA_H, t, QK), heads), pl.BlockSpec((MLA_H, t, QK), heads), pl.BlockSpec((MLA_H, t, VDIM), heads)],
        out_specs=[pl.BlockSpec((t, wide), rows), pl.BlockSpec((t, wide), rows), pl.BlockSpec((t, LANE), rows),
                   pl.BlockSpec((1, HEAD_COLS), whole), pl.BlockSpec((1, HEAD_COLS), whole)],
        out_shape=[jax.ShapeDtypeStruct((s, wide), BF16), jax.ShapeDtypeStruct((s, wide), BF16), jax.ShapeDtypeStruct((s, LANE), F32),
                   jax.ShapeDtypeStruct((1, HEAD_COLS), F32), jax.ShapeDtypeStruct((1, HEAD_COLS), F32)],
        compiler_params=_cparams("arbitrary"), name="mla_heads_bwd",
    )(q, kv, proj, cs, sn, q_gain, k_gain, dqh, dkh, dvh)


CONV_TC = 512
HALO = 8


def _conv_tiles(s):
    return min(512, s)


def _conv_fwd_call(x, col0, w, b):
    s = x.shape[0]
    ts = _conv_tiles(s)
    hb = ts // HALO
    c0 = col0 // CONV_TC
    assert col0 % CONV_TC == 0

    def body(x_ref, prev_ref, w_ref, b_ref, y_ref, buf):
        si = pl.program_id(1)
        buf[0:HALO, :] = jnp.where(si > 0, prev_ref[...], 0.0)
        buf[HALO:, :] = x_ref[...]
        acc = jnp.broadcast_to(b_ref[...], (ts, CONV_TC))
        for k in range(CONV_K):
            acc = acc + w_ref[k:k + 1, :] * buf[pl.ds(HALO - (CONV_K - 1) + k, ts), :]
        y_ref[...] = acc * jax.nn.sigmoid(acc)

    return pl.pallas_call(
        body, grid=(CONV_DIM // CONV_TC, s // ts),
        in_specs=[pl.BlockSpec((ts, CONV_TC), lambda ci, si: (si, ci + c0)),
                  pl.BlockSpec((HALO, CONV_TC), lambda ci, si: (jnp.maximum(si * hb - 1, 0), ci + c0)),
                  pl.BlockSpec((CONV_K, CONV_TC), lambda ci, si: (0, ci)),
                  pl.BlockSpec((1, CONV_TC), lambda ci, si: (0, ci))],
        out_specs=pl.BlockSpec((ts, CONV_TC), lambda ci, si: (si, ci)),
        out_shape=jax.ShapeDtypeStruct((s, CONV_DIM), F32),
        scratch_shapes=[pltpu.VMEM((ts + HALO, CONV_TC), F32)],
        compiler_params=_cparams("arbitrary", "arbitrary"), name="conv_fwd",
    )(x, x, w, b)


def _conv_bwd_call(x, col0, w, b, dy):
    s = x.shape[0]
    ts = _conv_tiles(s)
    hb = ts // HALO
    ns = s // ts
    last_halo = s // HALO - 1
    c0 = col0 // CONV_TC

    def body(x_ref, prev_ref, next_ref, dy_ref, dyn_ref, w_ref, b_ref, dx_ref, dw_ref, db_ref, xbuf, dbuf):
        si = pl.program_id(1)
        xbuf[0:HALO, :] = jnp.where(si > 0, prev_ref[...], 0.0)
        xbuf[HALO:HALO + ts, :] = x_ref[...]
        xbuf[HALO + ts:, :] = next_ref[...]
        pre = jnp.broadcast_to(b_ref[...], (ts + HALO, CONV_TC))
        for k in range(CONV_K):
            pre = pre + w_ref[k:k + 1, :] * xbuf[pl.ds(HALO - (CONV_K - 1) + k, ts + HALO), :]
        sg = jax.nn.sigmoid(pre)
        dsilu = sg * (1.0 + pre * (1.0 - sg))
        dbuf[0:ts, :] = dy_ref[...] * dsilu[0:ts]
        dbuf[ts:, :] = jnp.where(si < ns - 1, dyn_ref[...] * dsilu[ts:], 0.0)
        dx = jnp.zeros((ts, CONV_TC), F32)
        for k in range(CONV_K):
            dx = dx + w_ref[k:k + 1, :] * dbuf[pl.ds(CONV_K - 1 - k, ts), :]
        dx_ref[...] = dx.astype(dx_ref.dtype)
        dpre = dbuf[0:ts, :]
        first = si == 0
        _acc_store(db_ref, jnp.sum(dpre, axis=0, keepdims=True), first)
        for k in range(CONV_K):
            dw_k = jnp.sum(dpre * xbuf[pl.ds(HALO - (CONV_K - 1) + k, ts), :], axis=0, keepdims=True)
            _acc_store(dw_ref.at[pl.ds(k, 1), :], dw_k, first)

    main = lambda ci, si: (si, ci)
    x_main = lambda ci, si: (si, ci + c0)
    x_prev = lambda ci, si: (jnp.maximum(si * hb - 1, 0), ci + c0)
    x_next = lambda ci, si: (jnp.minimum(si * hb + hb, last_halo), ci + c0)
    return pl.pallas_call(
        body, grid=(CONV_DIM // CONV_TC, ns),
        in_specs=[pl.BlockSpec((ts, CONV_TC), x_main), pl.BlockSpec((HALO, CONV_TC), x_prev), pl.BlockSpec((HALO, CONV_TC), x_next),
                  pl.BlockSpec((ts, CONV_TC), main),
                  pl.BlockSpec((HALO, CONV_TC), lambda ci, si: (jnp.minimum(si * hb + hb, last_halo), ci)),
                  pl.BlockSpec((CONV_K, CONV_TC), lambda ci, si: (0, ci)),
                  pl.BlockSpec((1, CONV_TC), lambda ci, si: (0, ci))],
        out_specs=[pl.BlockSpec((ts, CONV_TC), main),
                   pl.BlockSpec((CONV_K, CONV_TC), lambda ci, si: (0, ci)),
                   pl.BlockSpec((1, CONV_TC), lambda ci, si: (0, ci))],
        out_shape=[jax.ShapeDtypeStruct((s, CONV_DIM), BF16), jax.ShapeDtypeStruct((CONV_K, CONV_DIM), F32),
                   jax.ShapeDtypeStruct((1, CONV_DIM), F32)],
        scratch_shapes=[pltpu.VMEM((ts + 2 * HALO, CONV_TC), F32), pltpu.VMEM((ts + HALO, CONV_TC), F32)],
        compiler_params=_cparams("arbitrary", "arbitrary"), name="conv_bwd",
    )(x, x, x, dy, dy, w, b)


GW = SSD_HPG * SSD_P
B_COL = SSD_DI
C_COL = SSD_DI + SSD_G * SSD_N


def _ones_where(mask):
    return jnp.where(mask, 1.0, 0.0).astype(BF16)


def _split(v, passes):
    parts, rest = [], v
    for i in range(passes):
        part = rest.astype(BF16)
        parts.append(part)
        if i + 1 < passes:
            rest = rest - part.astype(F32)
    return parts


def _dot_sel_r(v, sel, passes=3):
    out = None
    for part in _split(v, passes):
        t = jnp.dot(part, sel, preferred_element_type=F32)
        out = t if out is None else out + t
    return out


def _dot_sel_l(sel, v, passes=3):
    out = None
    for part in _split(v, passes):
        t = jnp.dot(sel, part, preferred_element_type=F32)
        out = t if out is None else out + t
    return out


def _ssd_consts():
    r = lax.broadcasted_iota(jnp.int32, (SSD_L, SSD_L), 0)
    c = lax.broadcasted_iota(jnp.int32, (SSD_L, SSD_L), 1)
    tril = r >= c
    triu = c >= r
    shift = SSD_P.bit_length() - 1
    eh = lax.broadcasted_iota(jnp.int32, (SSD_H, SSD_DI), 0)
    ej = lax.broadcasted_iota(jnp.int32, (SSD_H, SSD_DI), 1)
    expand = _ones_where(lax.shift_right_logical(ej, shift) == eh)
    rj = lax.broadcasted_iota(jnp.int32, (SSD_DI, SSD_H), 0)
    rh = lax.broadcasted_iota(jnp.int32, (SSD_DI, SSD_H), 1)
    reduce_ = _ones_where(lax.shift_right_logical(rj, shift) == rh)
    lane = lax.broadcasted_iota(jnp.int32, (SSD_L, LANE), 1)
    return tril, triu, expand, reduce_, lane < SSD_P


def _ssd_decays(dt, dt_t, a, a_t, tril, triu, expand):
    dta = dt * a
    acum = _dot_sel_l(_ones_where(tril), dta)
    acum_t = _dot_sel_r(dt_t * a_t, _ones_where(triu))
    dta_e = _dot_sel_r(dta, expand)
    acum_e = _dot_sel_r(acum, expand)
    last_e = jnp.sum(dta_e, axis=0, keepdims=True)
    return acum, acum_t, acum_e, last_e


def _head_decay(acum, acum_t, h, tril):
    seg = acum[:, h:h + 1] - acum_t[h:h + 1, :]
    return jnp.exp(jnp.where(tril, seg, NEG))


def _ssd_fwd_call(xbc, dt, a):
    s = xbc.shape[0]
    nc = s // SSD_L
    dt_t = dt.T
    a_t = a.T

    def body(xbc_ref, dt_ref, dtt_ref, a_ref, at_ref, y_ref, st_ref, s_sc):
        ci = pl.program_id(0)

        @pl.when(ci == 0)
        def _():
            s_sc[...] = jnp.zeros_like(s_sc)

        st_ref[0] = s_sc[...]
        tril, triu, expand, _, low_half = _ssd_consts()
        acum, acum_t, acum_e, last_e = _ssd_decays(dt_ref[...], dtt_ref[...], a_ref[...], at_ref[...], tril, triu, expand)
        dt_e = _dot_sel_r(dt_ref[...], expand, passes=2)
        xdt = xbc_ref[:, :SSD_DI] * dt_e
        xdt_b = xdt.astype(BF16)
        xw_b = (xdt * jnp.exp(last_e - acum_e)).astype(BF16)
        ea_e = jnp.exp(acum_e)
        el_e = jnp.exp(last_e)
        for g in range(SSD_G):
            gs = slice(g * GW, (g + 1) * GW)
            bg = xbc_ref[:, B_COL + g * SSD_N:B_COL + (g + 1) * SSD_N]
            cg_b = xbc_ref[:, C_COL + g * SSD_N:C_COL + (g + 1) * SSD_N].astype(BF16)
            bg_b = bg.astype(BF16)
            cb = _nt(cg_b, bg_b)
            st = s_sc[:, gs]
            y_off = jnp.dot(cg_b, st.astype(BF16), preferred_element_type=F32) * ea_e[:, gs]
            for pr in range(SSD_HPG // 2):
                ls = slice(g * GW + pr * LANE, g * GW + (pr + 1) * LANE)
                xp = xdt_b[:, ls]
                yd = []
                for half in range(2):
                    h = g * SSD_HPG + pr * 2 + half
                    m = (cb * _head_decay(acum, acum_t, h, tril)).astype(BF16)
                    yd.append(jnp.dot(m, xp, preferred_element_type=F32))
                y_ref[:, ls] = jnp.where(low_half, yd[0], yd[1]) + y_off[:, pr * LANE:(pr + 1) * LANE]
            s_sc[:, gs] = st * el_e[:, gs] + jnp.dot(bg.T.astype(BF16), xw_b[:, gs], preferred_element_type=F32)

    row = lambda i: (i, 0)
    return pl.pallas_call(
        body, grid=(nc,),
        in_specs=[pl.BlockSpec((SSD_L, CONV_DIM), row), pl.BlockSpec((SSD_L, SSD_H), row),
                  pl.BlockSpec((SSD_H, SSD_L), lambda i: (0, i)), pl.BlockSpec((1, SSD_H), lambda i: (0, 0)),
                  pl.BlockSpec((SSD_H, 1), lambda i: (0, 0))],
        out_specs=[pl.BlockSpec((SSD_L, SSD_DI), row), pl.BlockSpec((1, SSD_N, SSD_DI), lambda i: (i, 0, 0))],
        out_shape=[jax.ShapeDtypeStruct((s, SSD_DI), F32), jax.ShapeDtypeStruct((nc, SSD_N, SSD_DI), F32)],
        scratch_shapes=[pltpu.VMEM((SSD_N, SSD_DI), F32)],
        compiler_params=_cparams("arbitrary"), name="ssd_fwd",
    )(xbc, dt, dt_t, a, a_t)


def _ssd_bwd_call(xbc, dt, a, states, dy, dx_extra):
    s = xbc.shape[0]
    nc = s // SSD_L
    dt_t = dt.T
    a_t = a.T

    def body(xbc_ref, dt_ref, dtt_ref, a_ref, at_ref, st_ref, dy_ref, dxe_ref,
             dxbc_ref, ddt_ref, da_ref, ds_sc, yf_sc, dxd_sc, dxw_sc):
        i = pl.program_id(0)

        @pl.when(i == 0)
        def _():
            ds_sc[...] = jnp.zeros_like(ds_sc)

        tril, triu, expand, reduce_, low_half = _ssd_consts()
        dt = dt_ref[...]
        a_row = a_ref[...]
        acum, acum_t, acum_e, last_e = _ssd_decays(dt, dtt_ref[...], a_row, at_ref[...], tril, triu, expand)
        dt_e = _dot_sel_r(dt, expand, passes=2)
        x = xbc_ref[:, :SSD_DI]
        xdt = x * dt_e
        xdt_b = xdt.astype(BF16)
        w_e = jnp.exp(last_e - acum_e)
        xw_b = (xdt * w_e).astype(BF16)
        ea_e = jnp.exp(acum_e)
        el_e = jnp.exp(last_e)
        dy = dy_ref[...]
        dy_b = dy.astype(BF16)
        s_prev = st_ref[0]
        ds_new = ds_sc[...]
        ds_new_b = ds_new.astype(BF16)
        triu_b = _ones_where(triu)
        strict_tril = jnp.logical_not(triu)
        head_ids = lax.broadcasted_iota(jnp.int32, (1, SSD_H), 1)
        d_dta_diag = jnp.zeros((SSD_L, SSD_H), F32)
        for g in range(SSD_G):
            gs = slice(g * GW, (g + 1) * GW)
            bs_ = slice(B_COL + g * SSD_N, B_COL + (g + 1) * SSD_N)
            cs_ = slice(C_COL + g * SSD_N, C_COL + (g + 1) * SSD_N)
            bg = xbc_ref[:, bs_]
            cg = xbc_ref[:, cs_]
            bg_b, cg_b = bg.astype(BF16), cg.astype(BF16)
            st_b = s_prev[:, gs].astype(BF16)
            y_off = jnp.dot(cg_b, st_b, preferred_element_type=F32) * ea_e[:, gs]
            yf_sc[:, gs] = y_off
            dz_b = (dy[:, gs] * ea_e[:, gs]).astype(BF16)
            d_c = _nt(dz_b, st_b)
            ds_prev = ds_new[:, gs] * el_e[:, gs] + jnp.dot(cg.T.astype(BF16), dz_b, preferred_element_type=F32)
            dxw_sc[:, gs] = jnp.dot(bg_b, ds_new_b[:, gs], preferred_element_type=F32)
            d_b = _nt(xw_b[:, gs], ds_new_b[:, gs])
            cb = _nt(cg_b, bg_b)
            d_g = jnp.zeros((SSD_L, SSD_L), F32)
            for pr in range(SSD_HPG // 2):
                ls = slice(g * GW + pr * LANE, g * GW + (pr + 1) * LANE)
                xp = xdt_b[:, ls]
                dyp = dy[:, ls]
                dyp_b = dy_b[:, ls]
                dxd = []
                for half in range(2):
                    h = g * SSD_HPG + pr * 2 + half
                    dec = _head_decay(acum, acum_t, h, tril)
                    m = cb * dec
                    dxd.append(jnp.dot(m.T.astype(BF16), dyp_b, preferred_element_type=F32))
                    mine = low_half if half == 0 else jnp.logical_not(low_half)
                    d_m = _nt(jnp.where(mine, dyp, 0.0).astype(BF16), xp)
                    d_g = d_g + d_m * dec
                    below = jnp.dot(triu_b, (d_m * m).astype(BF16), preferred_element_type=F32)
                    col = jnp.sum(jnp.where(strict_tril, below, 0.0), axis=1, keepdims=True)
                    d_dta_diag = d_dta_diag + col * jnp.where(head_ids == h, 1.0, 0.0)
                dxd_sc[:, ls] = jnp.where(low_half, dxd[0], dxd[1])
            d_g_b = d_g.astype(BF16)
            dxbc_ref[:, cs_] = d_c + jnp.dot(d_g_b, bg_b, preferred_element_type=F32)
            dxbc_ref[:, bs_] = d_b + jnp.dot(d_g.T.astype(BF16), cg_b, preferred_element_type=F32)
            ds_sc[:, gs] = ds_prev
        dxw = dxw_sc[...]
        dxd = dxd_sc[...]
        dw_e = xdt * dxw * w_e
        d_out = _dot_sel_r(dy * yf_sc[...], reduce_, passes=2)
        d_upd = _dot_sel_r(dw_e, reduce_, passes=2)
        d_tot_e = jnp.sum(ds_new * s_prev, axis=0, keepdims=True) * el_e
        d_tot = _dot_sel_r(jnp.broadcast_to(d_tot_e, (8, SSD_DI)), reduce_, passes=2)[0:1]
        d_dta = _dot_sel_l(triu_b, d_out) + _dot_sel_l(_ones_where(strict_tril), d_upd) + d_tot + d_dta_diag
        dxdt = dxd + dxw * w_e
        dxbc_ref[:, :SSD_DI] = dxdt * dt_e + dxe_ref[...]
        ddt_ref[...] = d_dta * a_row + _dot_sel_r(dxdt * x, reduce_, passes=2)
        _acc_store(da_ref, jnp.sum(d_dta * dt, axis=0, keepdims=True), i == 0)

    rev = lambda i: (nc - 1 - i, 0)
    return pl.pallas_call(
        body, grid=(nc,),
        in_specs=[pl.BlockSpec((SSD_L, CONV_DIM), rev), pl.BlockSpec((SSD_L, SSD_H), rev),
                  pl.BlockSpec((SSD_H, SSD_L), lambda i: (0, nc - 1 - i)), pl.BlockSpec((1, SSD_H), lambda i: (0, 0)),
                  pl.BlockSpec((SSD_H, 1), lambda i: (0, 0)),
                  pl.BlockSpec((1, SSD_N, SSD_DI), lambda i: (nc - 1 - i, 0, 0)),
                  pl.BlockSpec((SSD_L, SSD_DI), rev), pl.BlockSpec((SSD_L, SSD_DI), rev)],
        out_specs=[pl.BlockSpec((SSD_L, CONV_DIM), rev), pl.BlockSpec((SSD_L, SSD_H), rev),
                   pl.BlockSpec((1, SSD_H), lambda i: (0, 0))],
        out_shape=[jax.ShapeDtypeStruct((s, CONV_DIM), F32), jax.ShapeDtypeStruct((s, SSD_H), F32),
                   jax.ShapeDtypeStruct((1, SSD_H), F32)],
        scratch_shapes=[pltpu.VMEM((SSD_N, SSD_DI), F32), pltpu.VMEM((SSD_L, SSD_DI), F32),
                        pltpu.VMEM((SSD_L, SSD_DI), F32), pltpu.VMEM((SSD_L, SSD_DI), F32)],
        compiler_params=_cparams("arbitrary"), name="ssd_bwd",
    )(xbc, dt, dt_t, a, a_t, states, dy, dx_extra)


HBM_SPEC = pl.BlockSpec(memory_space=pltpu.HBM)
N_PEERS = N_DEV - 1


def _flip(v, f):
    return 1 - v if f else v


def _all_gather(shard):
    rows, c = shard.shape

    def body(x_ref, out_ref, send_sems, recv_sems, local_sem):
        x, y, cc = lax.axis_index("x"), lax.axis_index("y"), lax.axis_index("c")
        me, sibling = (x, y, cc), (x, y, 1 - cc)
        chips = [(1 - x, y), (x, 1 - y), (1 - x, 1 - y)]

        def slot(px, py, pc):
            return out_ref.at[4 * px + 2 * py + pc]

        def copy(k, block, to, src=None):
            return pltpu.make_async_remote_copy(
                src_ref=slot(*block) if src is None else src, dst_ref=slot(*block),
                send_sem=send_sems.at[k], recv_sem=recv_sems.at[k],
                device_id=to, device_id_type=pl.DeviceIdType.MESH)

        mine = pltpu.make_async_copy(x_ref, slot(*me), local_sem)
        mine.start()
        first = [copy(0, me, sibling, src=x_ref)]
        first += [copy(1 + j, me, (*chip, cc), src=x_ref) for j, chip in enumerate(chips)]
        for cp in first:
            cp.start()
        passed = [copy(4 + j, (*chip, cc), sibling) for j, chip in enumerate(chips)]
        for j, chip in enumerate(chips):
            copy(1 + j, (*chip, cc), me).wait_recv()
            passed[j].start()
        copy(0, sibling, me).wait_recv()
        for j, chip in enumerate(chips):
            copy(4 + j, (*chip, 1 - cc), me).wait_recv()
        for cp in first + passed:
            cp.wait_send()
        mine.wait()

    return pl.pallas_call(
        body, out_shape=jax.ShapeDtypeStruct((N_DEV, rows, c), shard.dtype),
        in_specs=[HBM_SPEC], out_specs=HBM_SPEC,
        scratch_shapes=[pltpu.SemaphoreType.DMA((N_PEERS,)), pltpu.SemaphoreType.DMA((N_PEERS,)), pltpu.SemaphoreType.DMA(())],
        name="all_gather",
    )(shard)


def _peer_copies(src_ref, out_ref, sems, gather, phase):
    send_sems, recv_sems, local_sem = sems
    x, y, cc = lax.axis_index("x"), lax.axis_index("y"), lax.axis_index("c")
    me = 4 * x + 2 * y + cc
    mine = pltpu.make_async_copy(src_ref if gather else src_ref.at[me], out_ref.at[me], local_sem)
    copies = []
    for k in range(1, N_DEV):
        px, py, pc = _flip(x, k & 4), _flip(y, k & 2), _flip(cc, k & 1)
        peer = 4 * px + 2 * py + pc
        src = src_ref if gather else src_ref.at[peer]
        copies.append((
            pltpu.make_async_remote_copy(
                src_ref=src, dst_ref=out_ref.at[me], send_sem=send_sems.at[k - 1], recv_sem=recv_sems.at[k - 1],
                device_id=(px, py, pc), device_id_type=pl.DeviceIdType.MESH),
            pltpu.make_async_remote_copy(
                src_ref=src, dst_ref=out_ref.at[peer], send_sem=send_sems.at[k - 1], recv_sem=recv_sems.at[k - 1],
                device_id=(px, py, pc), device_id_type=pl.DeviceIdType.MESH)))
    if phase == "start":
        mine.start()
        for send, _ in copies:
            send.start()
    else:
        for _, landed in copies:
            landed.wait_recv()
        for send, _ in copies:
            send.wait_send()
        mine.wait()


def _comm_scratch():
    return [pltpu.SemaphoreType.DMA((N_PEERS,)), pltpu.SemaphoreType.DMA((N_PEERS,)), pltpu.SemaphoreType.DMA(())]


def _gather_rider(shard):
    return dict(src=shard, out=jax.ShapeDtypeStruct((N_DEV,) + shard.shape, shard.dtype), gather=True)


def _exchange_rider(blocks):
    return dict(src=blocks, out=jax.ShapeDtypeStruct(blocks.shape, blocks.dtype), gather=False)


def _exchange_blocks(blocks):
    def body(g_ref, out_ref, *sems):
        _peer_copies(g_ref, out_ref, sems, False, "start")
        _peer_copies(g_ref, out_ref, sems, False, "finish")

    return pl.pallas_call(
        body, out_shape=jax.ShapeDtypeStruct(blocks.shape, blocks.dtype),
        in_specs=[HBM_SPEC], out_specs=HBM_SPEC, scratch_shapes=_comm_scratch(), name="exchange_blocks",
    )(blocks)


BIG = [
    ("ffn1_w13", (D_MODEL, 2 * D_FF), 1), ("ffn1_w2", (D_FF, D_MODEL), 0),
    ("w_ssd_out", (SSD_DI, D_MODEL), 0), ("w_uq", (Q_LORA, MLA_H * QK), 1), ("w_ukv", (KV_LORA, MLA_H * (NOPE + VDIM)), 1),
    ("w_mla_out", (MLA_H * VDIM, D_MODEL), 0), ("w_o", (D_MODEL, D_MODEL), 0),
    ("ffn2_w13", (D_MODEL, 2 * D_FF), 1), ("ffn2_w2", (D_FF, D_MODEL), 0), ("w_in", (D_MODEL, D_IN), 1),
]
assert all(_r % 16 == 0 for _r in [_f[0] * _f[1] // N_DEV // PACK_COLS for _, _f, _ in BIG[:-1]])
SMALL = [
    ("ln_ffn1", D_MODEL), ("ln_mix", D_MODEL), ("conv_b", CONV_DIM), ("dt_bias", SSD_H), ("a_log", SSD_H), ("d_skip", SSD_H),
    ("ssd_norm", SSD_DI), ("q_lora_norm", Q_LORA), ("kv_lora_norm", KV_LORA), ("q_norm", QK), ("k_norm", QK), ("ln_ffn2", D_MODEL),
]


def _shard_shape(full, axis):
    k, n = full
    return (k // N_DEV, n) if axis == 0 else (k, n // N_DEV)


def _shard_rows(full):
    return full[0] * full[1] // N_DEV // PACK_COLS


LAYER_ROWS = sum(_shard_rows(f) for _, f, _ in BIG)
LAYER_ROWS_PAD = -(-LAYER_ROWS // 256) * 256


def _pack_shards(shards):
    parts = [(shards[name] if axis == 0 else shards[name].T).reshape(-1, PACK_COLS) for name, _, axis in BIG]
    pad = LAYER_ROWS_PAD - LAYER_ROWS
    if pad:
        parts.append(jnp.zeros((pad, PACK_COLS), parts[0].dtype))
    return jnp.concatenate(parts, axis=0)


def _unpack_shards(packed):
    out, r = {}, 0
    for name, full, axis in BIG:
        n = _shard_rows(full)
        k, c = _shard_shape(full, axis)
        blk = packed[r:r + n]
        out[name] = blk.reshape(k, c) if axis == 0 else blk.reshape(c, k).T
        r += n
    return out


def _working_shape(full, axis):
    return full if axis == 0 else full[::-1]


def _unpack_gathered(gathered):
    out, r = {}, 0
    for name, full, axis in BIG:
        n = _shard_rows(full)
        out[name] = gathered[:, r:r + n].reshape(_working_shape(full, axis))
        r += n
    return out


def _pack_full_grads(grads):
    parts = [grads[name].reshape(N_DEV, -1, PACK_COLS) for name, _, _ in BIG]
    pad = LAYER_ROWS_PAD - LAYER_ROWS
    if pad:
        parts.append(jnp.zeros((N_DEV, pad, PACK_COLS), parts[0].dtype))
    return jnp.concatenate(parts, axis=1)


SMALL_COLS = sum(n for _, n in SMALL) + CONV_K * CONV_DIM
SMALL_ROWS = -(-(DEPTH * SMALL_COLS) // (8 * PACK_COLS)) * 8


def _pack_small(vals, conv_w):
    flat = jnp.concatenate([vals[name] for name, _ in SMALL] + [conv_w.reshape(DEPTH, -1)], axis=1).reshape(-1)
    flat = jnp.concatenate([flat, jnp.zeros((SMALL_ROWS * PACK_COLS - flat.shape[0],), F32)])
    return flat.reshape(SMALL_ROWS, PACK_COLS)


def _unpack_small(packed):
    flat = packed.reshape(-1)[:DEPTH * SMALL_COLS].reshape(DEPTH, SMALL_COLS)
    out, c = {}, 0
    for name, n in SMALL:
        out[name] = flat[:, c:c + n]
        c += n
    return out, flat[:, c:].reshape(DEPTH, CONV_K, CONV_DIM)


_IN_OFFS = [sum(IN_SPLIT[:i]) for i in range(len(IN_SPLIT) + 1)]


def _arrange_w_in(w_t):
    z, xbc, dt, cq, ckv, kr, gates = [w_t[_IN_OFFS[i]:_IN_OFFS[i + 1]] for i in range(len(IN_SPLIT))]
    pad = jnp.zeros((LANE - ROPE - SSD_H, w_t.shape[1]), w_t.dtype)
    return jnp.concatenate([z, gates, xbc, cq, ckv, kr, dt, pad], axis=0)


def _restore_w_in(g):
    z, gates, xbc = g[PROJ_Z:PROJ_GATES], g[PROJ_GATES:PROJ_XBC], g[PROJ_XBC:PROJ_CQ]
    cq, ckv = g[PROJ_CQ:PROJ_CKV], g[PROJ_CKV:PROJ_LAST]
    kr, dt = g[PROJ_LAST:PROJ_LAST + ROPE], g[PROJ_LAST + ROPE:PROJ_LAST + ROPE + SSD_H]
    return jnp.concatenate([z, xbc, dt, cq, ckv, kr, gates], axis=0)


def _pad_heads(w_t):
    k = w_t.shape[1]
    return jnp.pad(w_t.reshape(MLA_H, QK, k), ((0, 0), (0, HEAD_COLS - QK), (0, 0))).reshape(MLA_H * HEAD_COLS, k)


def _unpad_heads(g):
    k = g.shape[1]
    return g.reshape(MLA_H, HEAD_COLS, k)[:, :QK].reshape(MLA_H * QK, k)


def _row(v):
    return v.reshape(1, -1)


def _head_gain(g):
    return jnp.pad(g, (0, HEAD_COLS - QK)).reshape(1, HEAD_COLS)


def _ffn_fwd(h, ln, w13_t, w2, name):
    n = _row_fwd(_f_rmsnorm, [h], [_row(ln)], [BF16], name + "_fwd")[0]
    act, gate, up = _ffn_up_call(n, w13_t)
    return _mm(act, w2, alpha=0.5, res=h), (h, n, gate, up, act)


def _ffn_bwd(dh_out, saved, ln, w13_t, w2, name):
    h, n, gate, up, act = saved
    d_gate, d_up = _ffn_down_bwd_call(dh_out, w2, gate, up)
    d_w2 = _mm(act, dh_out, ta=True, out_dtype=BF16, alpha=0.5)
    d_n = _mm(d_gate, w13_t, b_rows=(0, D_FF))
    d_n = _mm(d_up, w13_t, out_dtype=BF16, b_rows=(D_FF, D_FF), res=d_n)
    d_w13_t = jnp.concatenate([_mm(d_gate, n, ta=True, out_dtype=BF16), _mm(d_up, n, ta=True, out_dtype=BF16)], axis=0)
    (dh,), (d_ln,) = _row_bwd(_f_rmsnorm, [h], [_row(ln)], [d_n], [F32], name + "_bwd", add={0: dh_out})
    return dh, d_w13_t, d_w2, d_ln[0]


def _mixer_fwd(h, big, small, conv_w, cs, sn, rider=None):
    s = h.shape[0]
    u = _row_fwd(_f_rmsnorm, [h], [_row(small["ln_mix"])], [BF16], "ln_mix_fwd")[0]
    proj = _mm(u, big["w_in"], tb=True)
    xbc = _conv_fwd_call(proj, PROJ_XBC, conv_w, _row(small["conv_b"]))
    dt_in = proj[:, PROJ_LAST + ROPE:PROJ_LAST + ROPE + SSD_H] + small["dt_bias"][None, :]
    dt = jax.nn.softplus(dt_in)
    a = -jnp.exp(small["a_log"])[None, :]
    y_scan, states = _ssd_fwd_call(xbc, dt, a)
    dsk = _row(jnp.repeat(small["d_skip"], SSD_P))
    gn_in = [y_scan, _win(xbc, 0, SSD_DI), _win(proj, PROJ_Z, SSD_DI)]
    yn = _row_fwd(_f_gated_norm, gn_in, [dsk, _row(small["ssd_norm"])], [BF16], "gated_norm_fwd")[0]
    y_ssd = _mm(yn, big["w_ssd_out"])
    qn = _row_fwd(_f_rmsnorm, [_win(proj, PROJ_CQ, Q_LORA)], [_row(small["q_lora_norm"])], [BF16], "q_lora_norm_fwd")[0]
    kvn = _row_fwd(_f_rmsnorm, [_win(proj, PROJ_CKV, KV_LORA)], [_row(small["kv_lora_norm"])], [BF16], "kv_lora_norm_fwd")[0]
    q = _mm(qn, big["w_uq"], tb=True)
    kv = _mm(kvn, big["w_ukv"], tb=True)
    qh, kh, vh = _heads_fwd_call(q, kv, proj, cs, sn, _head_gain(small["q_norm"]), _head_gain(small["k_norm"]))
    o, lse, *carried = _attn_fwd_call(qh, kh, vh, rider)
    o_rows = jnp.transpose(o, (1, 0, 2)).reshape(s, MLA_H * VDIM)
    y_mla = _mm(o_rows, big["w_mla_out"])
    mg = _row_fwd(_f_merge, [_win(proj, PROJ_GATES, 2 * D_MODEL), y_ssd, y_mla], [], [BF16], "merge_fwd")[0]
    out = _mm(mg, big["w_o"], res=h)
    saved = (h, u, proj, xbc, dt_in, dt, a, y_scan, states, dsk, yn, y_ssd, qn, kvn, q, kv, qh, kh, vh, o, lse, o_rows, y_mla, mg)
    return out, saved, (carried[0] if carried else None)


def _mixer_bwd(dh_out, saved, big, small, conv_w, cs, sn, rider=None):
    (h, u, proj, xbc, dt_in, dt, a, y_scan, states, dsk, yn, y_ssd, qn, kvn, q, kv, qh, kh, vh, o, lse, o_rows, y_mla, mg) = saved
    s = h.shape[0]
    d_big, d_small = {}, {}
    d_mg = _mm(dh_out, big["w_o"], tb=True, out_dtype=BF16)
    d_big["w_o"] = _mm(mg, dh_out, ta=True, out_dtype=BF16)
    merge_in = [_win(proj, PROJ_GATES, 2 * D_MODEL), y_ssd, y_mla]
    (d_gates, d_y_ssd, d_y_mla), _ = _row_bwd(_f_merge, merge_in, [], [d_mg], [BF16, BF16, BF16], "merge_bwd", bwd=_b_merge)
    d_o_rows = _mm(d_y_mla, big["w_mla_out"], tb=True, out_dtype=BF16)
    d_big["w_mla_out"] = _mm(o_rows, d_y_mla, ta=True, out_dtype=BF16)
    d_o = jnp.transpose(d_o_rows.reshape(s, MLA_H, VDIM), (1, 0, 2))
    delta = _attn_delta_call(o, d_o)
    *d_heads, carried = list(_attn_bwd_call(qh, kh, vh, d_o, lse.reshape(MLA_H, 1, s), delta.reshape(MLA_H, 1, s), rider)) + ([None] if rider is None else [])
    d_q, d_kv, d_kr, d_qg, d_kg = _heads_bwd_call(
        q, kv, proj, cs, sn, _head_gain(small["q_norm"]), _head_gain(small["k_norm"]), *d_heads)
    d_small["q_norm"], d_small["k_norm"] = d_qg[0, :QK], d_kg[0, :QK]
    d_qn = _mm(d_q, big["w_uq"], out_dtype=BF16)
    d_big["w_uq"] = _mm(d_q, qn, ta=True, out_dtype=BF16)
    d_kvn = _mm(d_kv, big["w_ukv"], out_dtype=BF16)
    d_big["w_ukv"] = _mm(d_kv, kvn, ta=True, out_dtype=BF16)
    (d_cq,), (d_g,) = _row_bwd(_f_rmsnorm, [_win(proj, PROJ_CQ, Q_LORA)], [_row(small["q_lora_norm"])], [d_qn], [BF16], "q_lora_norm_bwd")
    d_small["q_lora_norm"] = d_g[0]
    (d_ckv,), (d_g,) = _row_bwd(_f_rmsnorm, [_win(proj, PROJ_CKV, KV_LORA)], [_row(small["kv_lora_norm"])], [d_kvn], [BF16], "kv_lora_norm_bwd")
    d_small["kv_lora_norm"] = d_g[0]
    d_yn = _mm(d_y_ssd, big["w_ssd_out"], tb=True, out_dtype=BF16)
    d_big["w_ssd_out"] = _mm(yn, d_y_ssd, ta=True, out_dtype=BF16)
    gn_in = [y_scan, _win(xbc, 0, SSD_DI), _win(proj, PROJ_Z, SSD_DI)]
    (d_y_scan, d_xs, d_z), (d_dsk, d_g) = _row_bwd(
        _f_gated_norm, gn_in, [dsk, _row(small["ssd_norm"])], [d_yn], [F32, F32, BF16], "gated_norm_bwd")
    d_small["ssd_norm"] = d_g[0]
    d_small["d_skip"] = jnp.sum(d_dsk.reshape(SSD_H, SSD_P), axis=1)
    d_xbc_act, d_dt, d_a = _ssd_bwd_call(xbc, dt, a, states, d_y_scan, d_xs)
    d_xbc, d_conv_w, d_conv_b = _conv_bwd_call(proj, PROJ_XBC, conv_w, _row(small["conv_b"]), d_xbc_act)
    d_small["conv_b"] = d_conv_b[0]
    d_dt_in = d_dt * jax.nn.sigmoid(dt_in)
    d_small["dt_bias"] = jnp.sum(d_dt_in, axis=0)
    d_small["a_log"] = d_a[0] * a[0]
    d_last = (d_kr + jnp.pad(d_dt_in, ((0, 0), (ROPE, LANE - ROPE - SSD_H)))).astype(BF16)
    d_proj = jnp.concatenate([d_z, d_gates, d_xbc, d_cq, d_ckv, d_last], axis=1)
    d_u = _mm(d_proj, big["w_in"], out_dtype=BF16)
    d_big["w_in"] = _mm(d_proj, u, ta=True, out_dtype=BF16)
    (dh,), (d_ln,) = _row_bwd(_f_rmsnorm, [h], [_row(small["ln_mix"])], [d_u], [F32], "ln_mix_bwd", add={0: dh_out})
    d_small["ln_mix"] = d_ln[0]
    return dh, d_big, d_small, d_conv_w, carried


def _prepare_big(b):
    return dict(b, w_in=_arrange_w_in(b["w_in"]), w_uq=_pad_heads(b["w_uq"]))


def _local_step(x, positions, target, big, small, conv_w, packed_last=None):
    inv = 1.0 / (ROPE_THETA ** (jnp.arange(0, ROPE, 2, dtype=F32) / ROPE))
    ang = positions.astype(F32)[:, None] * inv
    cos, sin = jnp.cos(ang), jnp.sin(ang)
    no_lanes = jnp.zeros((x.shape[0], LANE - ROPE), F32)
    cs = jnp.concatenate([cos, cos, no_lanes], axis=1)
    sn = jnp.concatenate([-sin, sin, no_lanes], axis=1)
    carrier = DEPTH - 2 if packed_last is not None else None
    big = [None if b is None else _prepare_big(b) for b in big]
    layer_small = [{k: v[l] for k, v in small.items()} for l in range(DEPTH)]

    h, saved = x, []
    for l in range(DEPTH):
        b, sm = big[l], layer_small[l]
        h, s1 = _ffn_fwd(h, sm["ln_ffn1"], b["ffn1_w13"], b["ffn1_w2"], "ln_ffn1")
        h, s2, gathered = _mixer_fwd(h, b, sm, conv_w[l], cs, sn, _gather_rider(packed_last) if l == carrier else None)
        if gathered is not None:
            big[l + 1] = _prepare_big(_unpack_gathered(gathered))
        h, s3 = _ffn_fwd(h, sm["ln_ffn2"], b["ffn2_w13"], b["ffn2_w2"], "ln_ffn2")
        saved.append((s1, s2, s3))
    loss, dh = _loss_and_grad(h, target)

    d_big, d_small, d_conv_w = [None] * DEPTH, [None] * DEPTH, [None] * DEPTH
    for l in reversed(range(DEPTH)):
        b, sm = big[l], layer_small[l]
        s1, s2, s3 = saved[l]
        dh, d_w13_2, d_w2_2, d_ln2 = _ffn_bwd(dh, s3, sm["ln_ffn2"], b["ffn2_w13"], b["ffn2_w2"], "ln_ffn2")
        rider = _exchange_rider(_pack_full_grads(d_big[l + 1])) if l == carrier else None
        dh, db, ds, d_conv_w[l], received = _mixer_bwd(dh, s2, b, sm, conv_w[l], cs, sn, rider)
        if received is not None:
            d_big[l + 1] = received
        dh, d_w13_1, d_w2_1, d_ln1 = _ffn_bwd(dh, s1, sm["ln_ffn1"], b["ffn1_w13"], b["ffn1_w2"], "ln_ffn1")
        db.update(ffn1_w13=d_w13_1, ffn1_w2=d_w2_1, ffn2_w13=d_w13_2, ffn2_w2=d_w2_2,
                  w_in=_restore_w_in(db["w_in"]), w_uq=_unpad_heads(db["w_uq"]))
        ds.update(ln_ffn1=d_ln1, ln_ffn2=d_ln2)
        d_big[l], d_small[l] = db, ds
    d_small = {name: jnp.stack([d_small[l][name] for l in range(DEPTH)]) for name, _ in SMALL}
    return loss, dh, d_big, d_small, jnp.stack(d_conv_w)


def _step(args):
    dev = 4 * lax.axis_index("x") + 2 * lax.axis_index("y") + lax.axis_index("c")
    x, positions, target = args["x"][0], args["positions"][0], args["loss_target"][0]

    packed = [_pack_shards({name: args[name][l].astype(BF16) for name, _, _ in BIG}) for l in range(DEPTH)]
    big = [_unpack_gathered(_all_gather(packed[l])) for l in range(DEPTH - 1)] + [None]
    cw = args["conv_w"]
    cw_cols = cw.shape[-1]
    cw_rows = -(-cw.size // (8 * PACK_COLS)) * 8
    cw_flat = jnp.concatenate([cw.reshape(-1), jnp.zeros((cw_rows * PACK_COLS - cw.size,), F32)]).reshape(cw_rows, PACK_COLS)
    cw_all = _all_gather(cw_flat).reshape(N_DEV, -1)[:, :cw.size].reshape(N_DEV, DEPTH, CONV_K, cw_cols)
    conv_w = jnp.transpose(cw_all, (1, 2, 0, 3)).reshape(DEPTH, CONV_K, CONV_DIM)
    small = {name: args[name] for name, _ in SMALL}

    loss, dx, d_big, d_small, d_conv_w = _local_step(x, positions, target, big, small, conv_w, packed_last=packed[-1])
    loss = lax.psum(loss, MESH_AXES)

    out = {"loss": loss, "grad_x": dx[None]}

    grads = {name: [] for name, _, _ in BIG}
    for l in range(DEPTH):
        received = d_big[l] if l == DEPTH - 1 else _exchange_blocks(_pack_full_grads(d_big[l]))
        summed = _sum_blocks(received)
        for name, g in _unpack_shards(summed).items():
            grads[name].append(g)
    flat = lambda t: t.reshape(-1, t.shape[-1])
    for name, _, _ in BIG:
        g = jnp.stack(grads[name])
        w = args[name]
        delta, m2, v2 = _adam(flat(w), flat(g), flat(args["m_" + name]), flat(args["v_" + name]))
        out["grad_" + name] = g
        out["delta_" + name] = delta.reshape(w.shape)
        out["new_m_" + name] = m2.reshape(w.shape)
        out["new_v_" + name] = v2.reshape(w.shape)

    total = _sum_blocks(_all_gather(_pack_small(d_small, d_conv_w)))
    g_conv_w = _unpack_small(total)[1]
    zeros_cw = jnp.zeros((DEPTH, CONV_K, CONV_DIM), F32)
    delta, m2, v2 = _adam(_pack_small(small, zeros_cw), total,
                          _pack_small({name: args["m_" + name] for name, _ in SMALL}, zeros_cw),
                          _pack_small({name: args["v_" + name] for name, _ in SMALL}, zeros_cw))
    for kind, packed in (("grad_", total), ("delta_", delta), ("new_m_", m2), ("new_v_", v2)):
        for name, val in _unpack_small(packed)[0].items():
            out[kind + name] = val
    g_cw = lax.dynamic_slice_in_dim(g_conv_w, dev * cw_cols, cw_cols, axis=2)
    delta, m2, v2 = _adam(flat(cw), flat(g_cw), flat(args["m_conv_w"]), flat(args["v_conv_w"]))
    out["grad_conv_w"] = g_cw
    out["delta_conv_w"] = delta.reshape(cw.shape)
    out["new_m_conv_w"] = m2.reshape(cw.shape)
    out["new_v_conv_w"] = v2.reshape(cw.shape)
    return out


WEIGHTS = ["ln_ffn1", "ffn1_w13", "ffn1_w2", "ln_mix", "w_in", "conv_w", "conv_b", "dt_bias", "a_log", "d_skip", "ssd_norm",
           "w_ssd_out", "q_lora_norm", "w_uq", "kv_lora_norm", "w_ukv", "q_norm", "k_norm", "w_mla_out", "w_o", "ln_ffn2",
           "ffn2_w13", "ffn2_w2"]
ARG_NAMES = (["x", "positions"] + WEIGHTS + ["loss_target"] + ["m_" + n for n in WEIGHTS] + ["v_" + n for n in WEIGHTS])


def kernel(x, positions, ln_ffn1, ffn1_w13, ffn1_w2, ln_mix, w_in, conv_w, conv_b, dt_bias, a_log, d_skip, ssd_norm, w_ssd_out, q_lora_norm, w_uq, kv_lora_norm, w_ukv, q_norm, k_norm, w_mla_out, w_o, ln_ffn2, ffn2_w13, ffn2_w2, loss_target, m_ln_ffn1, m_ffn1_w13, m_ffn1_w2, m_ln_mix, m_w_in, m_conv_w, m_conv_b, m_dt_bias, m_a_log, m_d_skip, m_ssd_norm, m_w_ssd_out, m_q_lora_norm, m_w_uq, m_kv_lora_norm, m_w_ukv, m_q_norm, m_k_norm, m_w_mla_out, m_w_o, m_ln_ffn2, m_ffn2_w13, m_ffn2_w2, v_ln_ffn1, v_ffn1_w13, v_ffn1_w2, v_ln_mix, v_w_in, v_conv_w, v_conv_b, v_dt_bias, v_a_log, v_d_skip, v_ssd_norm, v_w_ssd_out, v_q_lora_norm, v_w_uq, v_kv_lora_norm, v_w_ukv, v_q_norm, v_k_norm, v_w_mla_out, v_w_o, v_ln_ffn2, v_ffn2_w13, v_ffn2_w2):
    vals = (x, positions, ln_ffn1, ffn1_w13, ffn1_w2, ln_mix, w_in, conv_w, conv_b, dt_bias, a_log, d_skip, ssd_norm, w_ssd_out, q_lora_norm, w_uq, kv_lora_norm, w_ukv, q_norm, k_norm, w_mla_out, w_o, ln_ffn2, ffn2_w13, ffn2_w2, loss_target, m_ln_ffn1, m_ffn1_w13, m_ffn1_w2, m_ln_mix, m_w_in, m_conv_w, m_conv_b, m_dt_bias, m_a_log, m_d_skip, m_ssd_norm, m_w_ssd_out, m_q_lora_norm, m_w_uq, m_kv_lora_norm, m_w_ukv, m_q_norm, m_k_norm, m_w_mla_out, m_w_o, m_ln_ffn2, m_ffn2_w13, m_ffn2_w2, v_ln_ffn1, v_ffn1_w13, v_ffn1_w2, v_ln_mix, v_w_in, v_conv_w, v_conv_b, v_dt_bias, v_a_log, v_d_skip, v_ssd_norm, v_w_ssd_out, v_q_lora_norm, v_w_uq, v_kv_lora_norm, v_w_ukv, v_q_norm, v_k_norm, v_w_mla_out, v_w_o, v_ln_ffn2, v_ffn2_w13, v_ffn2_w2)
    out = _step(dict(zip(ARG_NAMES, vals)))
    order = ["loss", "grad_x"] + [k + n for k in ("grad_", "delta_", "new_m_", "new_v_") for n in WEIGHTS]
    return tuple(out[n] for n in order)
```

```python
import jax
import jax.numpy as jnp
from jax import lax
from jax.experimental import pallas as pl
from jax.experimental.pallas import tpu as pltpu

F32 = jnp.float32
BF16 = jnp.bfloat16

D_MODEL = 1024
D_FF = 2816
DEPTH = 2
SSD_DI = 2048
SSD_P = 64
SSD_H = 32
SSD_G = 4
SSD_HPG = 8
SSD_N = 128
SSD_L = 128
CONV_K = 4
CONV_DIM = 3072
MLA_H = 8
Q_LORA = 512
KV_LORA = 256
NOPE = 128
ROPE = 64
VDIM = 128
QK = 192
ROPE_THETA = 10000.0
EPS = 1e-6
IN_SPLIT = (SSD_DI, CONV_DIM, SSD_H, Q_LORA, KV_LORA, ROPE, 2 * D_MODEL)
D_IN = sum(IN_SPLIT)
N_DEV = 8
LANE = 128
PACK_COLS = 1024

PROJ_Z = 0
PROJ_GATES = PROJ_Z + SSD_DI
PROJ_XBC = PROJ_GATES + 2 * D_MODEL
PROJ_CQ = PROJ_XBC + CONV_DIM
PROJ_CKV = PROJ_CQ + Q_LORA
PROJ_LAST = PROJ_CKV + KV_LORA
D_IN_PAD = PROJ_LAST + LANE

ADAM_LR = 0.001
ADAM_B1 = 0.9
ADAM_B2 = 0.999
ADAM_EPS = 1e-08
ADAM_WD = 0.01
ADAM_STEP = 10

VMEM_LIMIT = 48 * 1024 * 1024
ROW_IO_BUDGET = 8 * 1024 * 1024
NEG = -1e30

MESH_AXES = ("x", "y", "c")


def _cparams(*sem):
    return pltpu.CompilerParams(dimension_semantics=sem, vmem_limit_bytes=VMEM_LIMIT)


def _pick_tile(n, target, align):
    if n <= target:
        return n
    best = None
    for t in range(align, target + 1, align):
        if n % t == 0:
            best = t
    assert best is not None, (n, target, align)
    return best


def _acc_store(ref, val, first):
    @pl.when(first)
    def _():
        ref[...] = val

    @pl.when(jnp.logical_not(first))
    def _():
        ref[...] += val


def _win(arr, start, width):
    assert start % width == 0, (start, width)
    return (arr, start, width)


def _operand(entry):
    if isinstance(entry, tuple):
        arr, start, width = entry
        return arr, width, start // width
    return entry, entry.shape[1], 0


def _row_tile(rows, bytes_per_row):
    if rows <= 16:
        return rows
    t = 1024
    while t > 16 and (t * bytes_per_row > ROW_IO_BUDGET or rows % t):
        t //= 2
    assert rows % t == 0, (rows, t)
    return t


def _rowwise_call(fn, tiled, params, outs, accs, name):
    ops = [_operand(e) for e in tiled]
    rows = ops[0][0].shape[0]
    per_row = sum(w * a.dtype.itemsize for a, w, _ in ops) + sum(c * jnp.dtype(d).itemsize for c, d in outs)
    tile = _row_tile(rows, per_row)
    n_in = len(tiled) + len(params)
    n_o = len(outs)

    def body(*refs):
        vals = [r[...] for r in refs[:n_in]]
        t_out, a_out = fn(*vals)
        for r, v in zip(refs[n_in:n_in + n_o], t_out):
            r[...] = v.astype(r.dtype)
        first = pl.program_id(0) == 0
        for r, v in zip(refs[n_in + n_o:], a_out):
            _acc_store(r, v.astype(F32), first)

    def tiled_spec(width, blk):
        return pl.BlockSpec((tile, width), lambda i: (i, blk))

    in_specs = [tiled_spec(w, blk) for _, w, blk in ops]
    in_specs += [pl.BlockSpec(p.shape, lambda i: (0, 0)) for p in params]
    out_specs = [tiled_spec(c, 0) for c, _ in outs]
    out_specs += [pl.BlockSpec(s, lambda i: (0, 0)) for s in accs]
    out_shape = [jax.ShapeDtypeStruct((rows, c), d) for c, d in outs]
    out_shape += [jax.ShapeDtypeStruct(s, F32) for s in accs]
    return pl.pallas_call(
        body, grid=(rows // tile,), in_specs=in_specs, out_specs=out_specs, out_shape=out_shape,
        compiler_params=_cparams("arbitrary"), name=name,
    )(*[a for a, _, _ in ops], *params)


def _to_f32(vals):
    return [v.astype(F32) for v in vals]


def _row_fwd(f, tiled, params, out_dtypes, name):
    ops = [_operand(e) for e in tiled]
    rows = ops[0][0].shape[0]
    shapes = jax.eval_shape(f, *[jax.ShapeDtypeStruct((rows, w), F32) for _, w, _ in ops],
                            *[jax.ShapeDtypeStruct(p.shape, F32) for p in params])
    outs = [(s.shape[1], d) for s, d in zip(shapes, out_dtypes)]
    return _rowwise_call(lambda *v: (f(*_to_f32(v)), ()), tiled, params, outs, [], name)


def _row_bwd(f, tiled, params, gs, d_dtypes, name, bwd=None, add=None):
    n_t, n_g = len(tiled), len(gs)
    adds = sorted((add or {}).items())
    n_a = len(adds)

    def fn(*vals):
        vals = _to_f32(vals)
        prim = vals[:n_t] + vals[n_t + n_g + n_a:]
        g = tuple(vals[n_t:n_t + n_g])
        if bwd is not None:
            d_t, d_p = bwd(*prim, *g)
        else:
            _, vjp = jax.vjp(f, *prim)
            cts = vjp(g)
            d_t, d_p = cts[:n_t], cts[n_t:]
        d_t = list(d_t)
        for (idx, _), extra in zip(adds, vals[n_t + n_g:n_t + n_g + n_a]):
            d_t[idx] = d_t[idx] + extra
        return tuple(d_t), tuple(d_p)

    outs = [(_operand(e)[1], d) for e, d in zip(tiled, d_dtypes)]
    accs = [p.shape for p in params]
    res = _rowwise_call(fn, list(tiled) + list(gs) + [a for _, a in adds], params, outs, accs, name)
    return res[:n_t], res[n_t:]


def _f_rmsnorm(x, g):
    return (x * lax.rsqrt(jnp.mean(x * x, axis=-1, keepdims=True) + EPS) * g,)


def _f_gated_norm(ys, xs, z, dsk, g):
    t = (ys + xs * dsk) * (z * jax.nn.sigmoid(z))
    return (t * lax.rsqrt(jnp.mean(t * t, axis=-1, keepdims=True) + EPS) * g,)


def _f_merge(gates, ys, ym):
    s = jax.nn.sigmoid(gates)
    return (s[:, :D_MODEL] * ys + s[:, D_MODEL:] * ym,)


def _b_merge(gates, ys, ym, d):
    s = jax.nn.sigmoid(gates)
    s1, s2 = s[:, :D_MODEL], s[:, D_MODEL:]
    d_gates = jnp.concatenate([d * ys * s1 * (1.0 - s1), d * ym * s2 * (1.0 - s2)], axis=1)
    return (d_gates, d * s1, d * s2), ()


def _loss_and_grad(y, target):
    def fn(yv, tv):
        d = yv - tv
        return (d * (1.0 / D_MODEL),), (jnp.sum(d * d, axis=0, keepdims=True) * (0.5 / D_MODEL),)

    dy, part = _rowwise_call(fn, [y, target], [], [(D_MODEL, F32)], [(1, D_MODEL)], "loss")
    return jnp.sum(part), dy


def _adam(w, g, m, v):
    def fn(wv, gv, mv, vv):
        m2 = ADAM_B1 * mv + (1.0 - ADAM_B1) * gv
        v2 = ADAM_B2 * vv + (1.0 - ADAM_B2) * (gv * gv)
        m_hat = m2 / (1.0 - ADAM_B1 ** ADAM_STEP)
        v_hat = v2 / (1.0 - ADAM_B2 ** ADAM_STEP)
        delta = -ADAM_LR * (m_hat / (jnp.sqrt(v_hat) + ADAM_EPS) + ADAM_WD * wv)
        return (delta, m2, v2), ()

    c = w.shape[1]
    return _rowwise_call(fn, [w, g, m, v], [], [(c, F32)] * 3, [], "adamw")


def _sum_blocks(blocks):
    _, rows, c = blocks.shape
    tile = _row_tile(rows, N_DEV * c * blocks.dtype.itemsize + c * 4)

    def body(b_ref, o_ref):
        acc = b_ref[0].astype(F32)
        for i in range(1, N_DEV):
            acc = acc + b_ref[i].astype(F32)
        o_ref[...] = acc

    return pl.pallas_call(
        body, grid=(rows // tile,), in_specs=[pl.BlockSpec((N_DEV, tile, c), lambda i: (0, i, 0))],
        out_specs=pl.BlockSpec((tile, c), lambda i: (i, 0)), out_shape=jax.ShapeDtypeStruct((rows, c), F32),
        compiler_params=_cparams("arbitrary"), name="sum_blocks",
    )(blocks)


def _mm(a, b, ta=False, tb=False, out_dtype=F32, alpha=1.0, res=None, b_rows=None):
    r_dim, p_dim = a.shape if ta else a.shape[::-1]
    b_row0, b_nrows = (0, b.shape[0]) if b_rows is None else b_rows
    r2, q_dim = (b.shape[1], b_nrows) if tb else (b_nrows, b.shape[1])
    assert r_dim == r2, (a.shape, b.shape, ta, tb)
    tp = _pick_tile(p_dim, 512, LANE)
    if tp < 512 < p_dim:
        tp = _pick_tile(p_dim, 1536, LANE)
    tq = _pick_tile(q_dim, 1536, LANE)
    tr = _pick_tile(r_dim, 1536, LANE)
    nr = r_dim // tr
    dims = (((0 if ta else 1,), (1 if tb else 0,)), ((), ()))
    has_res = res is not None

    def body(*refs):
        a_ref, b_ref = refs[:2]
        res_ref = refs[2] if has_res else None
        o_ref = refs[2 + has_res]

        def finish(val):
            if alpha != 1.0:
                val = val * alpha
            if has_res:
                val = val + res_ref[...].astype(F32)
            o_ref[...] = val.astype(o_ref.dtype)

        part = lax.dot_general(a_ref[...].astype(BF16), b_ref[...].astype(BF16), dims, preferred_element_type=F32)
        if nr == 1:
            finish(part)
        else:
            acc_ref = refs[3 + has_res]
            k = pl.program_id(2)
            _acc_store(acc_ref, part, k == 0)

            @pl.when(k == nr - 1)
            def _():
                finish(acc_ref[...])

    a_spec = pl.BlockSpec((tr, tp), lambda j, i, k: (k, i)) if ta else pl.BlockSpec((tp, tr), lambda j, i, k: (i, k))
    assert b_row0 % (tq if tb else tr) == 0
    b0 = b_row0 // (tq if tb else tr)
    b_spec = pl.BlockSpec((tq, tr), lambda j, i, k: (j + b0, k)) if tb else pl.BlockSpec((tr, tq), lambda j, i, k: (k + b0, j))
    o_spec = pl.BlockSpec((tp, tq), lambda j, i, k: (i, j))
    return pl.pallas_call(
        body, grid=(q_dim // tq, p_dim // tp, nr), in_specs=[a_spec, b_spec] + ([o_spec] if has_res else []),
        out_specs=o_spec, out_shape=jax.ShapeDtypeStruct((p_dim, q_dim), out_dtype),
        scratch_shapes=[pltpu.VMEM((tp, tq), F32)] if nr > 1 else [],
        compiler_params=_cparams("arbitrary", "arbitrary", "arbitrary"),
        name=f"mm_{'t' if ta else 'n'}{'t' if tb else 'n'}_{p_dim}x{r_dim}x{q_dim}",
    )(*([a, b] + ([res] if has_res else [])))


FFN_TP = 512
FFN_TQ = 1408


def _ffn_up_call(n, w13_t):
    s, d = n.shape
    tp = min(FFN_TP, s)
    up0 = D_FF // FFN_TQ

    def body(n_ref, wg_ref, wu_ref, act_ref, gate_ref, up_ref):
        a = n_ref[...]
        g = _nt(a, wg_ref[...])
        u = _nt(a, wu_ref[...])
        act_ref[...] = (g * jax.nn.sigmoid(g) * u).astype(BF16)
        gate_ref[...] = g.astype(BF16)
        up_ref[...] = u.astype(BF16)

    o_spec = pl.BlockSpec((tp, FFN_TQ), lambda j, i: (i, j))
    return pl.pallas_call(
        body, grid=(D_FF // FFN_TQ, s // tp),
        in_specs=[pl.BlockSpec((tp, d), lambda j, i: (i, 0)), pl.BlockSpec((FFN_TQ, d), lambda j, i: (j, 0)),
                  pl.BlockSpec((FFN_TQ, d), lambda j, i: (j + up0, 0))],
        out_specs=[o_spec] * 3, out_shape=[jax.ShapeDtypeStruct((s, D_FF), BF16)] * 3,
        compiler_params=_cparams("arbitrary", "arbitrary"), name="ffn_up_swiglu",
    )(n, w13_t, w13_t)


def _ffn_down_bwd_call(dh, w2, gate, up):
    s, d = dh.shape
    tp = min(FFN_TP, s)

    def body(dh_ref, w2_ref, gate_ref, up_ref, dg_ref, du_ref):
        d_act = 0.5 * _nt(dh_ref[...].astype(BF16), w2_ref[...])
        g, u = gate_ref[...].astype(F32), up_ref[...].astype(F32)
        sg = jax.nn.sigmoid(g)
        dg_ref[...] = (d_act * u * sg * (1.0 + g * (1.0 - sg))).astype(BF16)
        du_ref[...] = (d_act * g * sg).astype(BF16)

    o_spec = pl.BlockSpec((tp, FFN_TQ), lambda j, i: (i, j))
    return pl.pallas_call(
        body, grid=(D_FF // FFN_TQ, s // tp),
        in_specs=[pl.BlockSpec((tp, d), lambda j, i: (i, 0)), pl.BlockSpec((FFN_TQ, d), lambda j, i: (j, 0)), o_spec, o_spec],
        out_specs=[o_spec] * 2, out_shape=[jax.ShapeDtypeStruct((s, D_FF), BF16)] * 2,
        compiler_params=_cparams("arbitrary", "arbitrary"), name="ffn_down_bwd_swiglu",
    )(dh, w2, gate, up)


ATTN_SCALE = QK ** -0.5
LOG2E = 1.4426950408889634
ATTN_C = ATTN_SCALE * LOG2E


ATTN_HEADS = 2


def _attn_tile(s):
    return min(512, s)


def _causal_keep(t, keys_on_rows=False):
    row = lax.broadcasted_iota(jnp.int32, (t, t), 0)
    col = lax.broadcasted_iota(jnp.int32, (t, t), 1)
    return row <= col if keys_on_rows else col <= row


def _nt(a, b):
    return lax.dot_general(a, b, (((1,), (1,)), ((), ())), preferred_element_type=F32)


def _rider_phases(rider, src_ref, out_ref, sems, first, last):
    @pl.when(first)
    def _():
        _peer_copies(src_ref, out_ref, sems, rider["gather"], "start")

    def finish():
        @pl.when(last)
        def _():
            _peer_copies(src_ref, out_ref, sems, rider["gather"], "finish")

    return finish


def _attn_fwd_call(q, k, v, rider=None):
    nh, s, _ = q.shape
    t = _attn_tile(s)
    nb = s // t
    hp = ATTN_HEADS
    n_r = 0 if rider is None else 1

    def body(*refs):
        q_ref, k_ref, v_ref = refs[:3]
        o_ref, lse_ref = refs[3 + n_r:5 + n_r]
        qi = pl.program_id(1)
        finish = None
        if rider is not None:
            h = pl.program_id(0)
            finish = _rider_phases(rider, refs[3], refs[5 + n_r], refs[6 + n_r:],
                                   jnp.logical_and(h == 0, qi == 0), jnp.logical_and(h == nh // hp - 1, qi == nb - 1))
        qs = [q_ref[i] for i in range(hp)]

        def block(kb, carries, diagonal, width=1):
            start = pl.multiple_of(kb * t, t)
            out = []
            for i, (m_prev, l_prev, acc) in enumerate(carries):
                sc = _nt(qs[i], k_ref[i, pl.ds(start, width * t), :])
                if diagonal:
                    sc = jnp.where(_causal_keep(t), sc, NEG)
                m_new = jnp.maximum(m_prev, jnp.max(sc, axis=-1, keepdims=True))
                p = jnp.exp2(sc * ATTN_C - m_new * ATTN_C)
                alpha = jnp.exp2((m_prev - m_new) * ATTN_C)
                l_new = alpha * l_prev + jnp.sum(p, axis=-1, keepdims=True)
                pv = jnp.dot(p.astype(BF16), v_ref[i, pl.ds(start, width * t), :], preferred_element_type=F32)
                out.append((m_new, l_new, alpha * acc + pv))
            return tuple(out)

        init = tuple((jnp.full((t, 1), NEG, F32), jnp.zeros((t, 1), F32), jnp.zeros((t, VDIM), F32)) for _ in range(hp))
        carries = lax.fori_loop(0, qi // 2, lambda j, c: block(2 * j, c, False, width=2), init)
        carries = lax.cond(qi % 2 == 1, lambda c: block(qi - 1, c, False), lambda c: c, carries)
        for i, (m, l, acc) in enumerate(block(qi, carries, True)):
            o_ref[i] = (acc / l).astype(o_ref.dtype)
            lse_ref[i] = m * ATTN_SCALE + jnp.log(l)
        if finish is not None:
            finish()

    qmap = lambda h, i: (h, i, 0)
    whole = lambda h, i: (h, 0, 0)
    return pl.pallas_call(
        body, grid=(nh // hp, nb),
        in_specs=[pl.BlockSpec((hp, t, QK), qmap), pl.BlockSpec((hp, s, QK), whole), pl.BlockSpec((hp, s, VDIM), whole)] + [HBM_SPEC] * n_r,
        out_specs=[pl.BlockSpec((hp, t, VDIM), qmap), pl.BlockSpec((hp, t, 1), qmap)] + [HBM_SPEC] * n_r,
        out_shape=[jax.ShapeDtypeStruct((nh, s, VDIM), BF16), jax.ShapeDtypeStruct((nh, s, 1), F32)] + ([rider["out"]] if n_r else []),
        scratch_shapes=_comm_scratch() if n_r else [],
        compiler_params=_cparams("arbitrary", "arbitrary"), name="attn_fwd_gather" if n_r else "attn_fwd",
    )(*([q, k, v] + ([rider["src"]] if n_r else [])))


def _attn_delta_call(o, do):
    nh, s, d = o.shape

    def fn(ov, dv):
        return (jnp.sum(ov.astype(F32) * dv.astype(F32), axis=-1, keepdims=True),), ()

    return _rowwise_call(fn, [o.reshape(nh * s, d), do.reshape(nh * s, d)], [], [(1, F32)], [], "attn_delta")[0]


def _attn_bwd_call(q, k, v, do, lse_t, delta_t, rider=None):
    nh, s, _ = q.shape
    t = _attn_tile(s)
    nb = s // t
    hp = ATTN_HEADS
    n_r = 0 if rider is None else 1

    def body(*refs):
        q_ref, k_ref, v_ref, do_ref, lse_ref, delta_ref = refs[:6]
        dq_ref, dk_ref, dv_ref = refs[6 + n_r:9 + n_r]
        dk_sc, dv_sc = refs[9 + 2 * n_r:11 + 2 * n_r]
        kj = pl.program_id(1)
        finish = None
        if rider is not None:
            h = pl.program_id(0)
            finish = _rider_phases(rider, refs[6], refs[9 + n_r], refs[11 + 2 * n_r:],
                                   jnp.logical_and(h == 0, kj == 0), jnp.logical_and(h == nh // hp - 1, kj == nb - 1))

        @pl.when(kj == 0)
        def _():
            dq_ref[...] = jnp.zeros_like(dq_ref)

        dk_sc[...] = jnp.zeros_like(dk_sc)
        dv_sc[...] = jnp.zeros_like(dv_sc)
        kblks = [k_ref[i] for i in range(hp)]
        vblks = [v_ref[i] for i in range(hp)]

        def block(qb, diagonal):
            start = pl.multiple_of(qb * t, t)
            for i in range(hp):
                qblk = q_ref[i, pl.ds(start, t), :]
                doblk = do_ref[i, pl.ds(start, t), :]
                sc = _nt(kblks[i], qblk)
                if diagonal:
                    sc = jnp.where(_causal_keep(t, keys_on_rows=True), sc, NEG)
                p = jnp.exp2(sc * ATTN_C - lse_ref[i, :, pl.ds(start, t)] * LOG2E)
                dv_sc[i] += jnp.dot(p.astype(BF16), doblk, preferred_element_type=F32)
                dp = _nt(vblks[i], doblk)
                ds = (p * (dp - delta_ref[i, :, pl.ds(start, t)])).astype(BF16)
                dk_sc[i] += jnp.dot(ds, qblk, preferred_element_type=F32)
                dq_ref[i, pl.ds(start, t), :] += lax.dot_general(ds, kblks[i], (((0,), (0,)), ((), ())), preferred_element_type=F32)

        block(kj, True)

        def rest(qb, carry):
            block(qb, False)
            return carry

        lax.fori_loop(kj + 1, nb, rest, 0)
        dk_ref[...] = (dk_sc[...] * ATTN_SCALE).astype(dk_ref.dtype)
        dv_ref[...] = dv_sc[...].astype(dv_ref.dtype)

        @pl.when(kj == nb - 1)
        def _():
            dq_ref[...] = dq_ref[...] * ATTN_SCALE

        if finish is not None:
            finish()

    kmap = lambda h, j: (h, j, 0)
    whole = lambda h, j: (h, 0, 0)
    once = pl.Buffered(buffer_count=1)
    return pl.pallas_call(
        body, grid=(nh // hp, nb),
        in_specs=[pl.BlockSpec((hp, s, QK), whole, pipeline_mode=once), pl.BlockSpec((hp, t, QK), kmap), pl.BlockSpec((hp, t, VDIM), kmap),
                  pl.BlockSpec((hp, s, VDIM), whole, pipeline_mode=once), pl.BlockSpec((hp, 1, s), whole, pipeline_mode=once),
                  pl.BlockSpec((hp, 1, s), whole, pipeline_mode=once)] + [HBM_SPEC] * n_r,
        out_specs=[pl.BlockSpec((hp, s, QK), whole, pipeline_mode=once), pl.BlockSpec((hp, t, QK), kmap),
                   pl.BlockSpec((hp, t, VDIM), kmap)] + [HBM_SPEC] * n_r,
        out_shape=[jax.ShapeDtypeStruct((nh, s, QK), F32), jax.ShapeDtypeStruct((nh, s, QK), F32),
                   jax.ShapeDtypeStruct((nh, s, VDIM), F32)] + ([rider["out"]] if n_r else []),
        scratch_shapes=[pltpu.VMEM((hp, t, QK), F32), pltpu.VMEM((hp, t, VDIM), F32)] + (_comm_scratch() if n_r else []),
        compiler_params=_cparams("arbitrary", "arbitrary"), name="attn_bwd_exchange" if n_r else "attn_bwd",
    )(*([q, k, v, do, lse_t, delta_t] + ([rider["src"]] if n_r else [])))


HEAD_COLS = NOPE + VDIM
HEADS_TILE = 256


def _swap_rope_halves(t, lane):
    half = ROPE // 2
    return jnp.where(lane < half, pltpu.roll(t, LANE - half, 1), pltpu.roll(t, half, 1))


def _head_fwd(n, p, gain, cs, sn, lane):
    r = lax.rsqrt((jnp.sum(n * n, axis=-1, keepdims=True) + jnp.sum(p * p, axis=-1, keepdims=True)) * (1.0 / QK) + EPS)
    yp = p * r * gain[:, NOPE:]
    return n * r * gain[:, :NOPE], yp * cs + _swap_rope_halves(yp, lane) * sn


def _head_bwd(n, p, gain, cs, sn, lane, dzn, dzp):
    r = lax.rsqrt((jnp.sum(n * n, axis=-1, keepdims=True) + jnp.sum(p * p, axis=-1, keepdims=True)) * (1.0 / QK) + EPS)
    dyp = dzp * cs + _swap_rope_halves(dzp * sn, lane)
    gyn, gyp = dzn * gain[:, :NOPE], dyp * gain[:, NOPE:]
    dot = jnp.sum(gyn * n, axis=-1, keepdims=True) + jnp.sum(gyp * p, axis=-1, keepdims=True)
    coef = dot * (r * r * r) * (1.0 / QK)
    d_gn = jnp.sum(dzn * n * r, axis=0, keepdims=True)
    d_gp = jnp.sum(dyp * p * r, axis=0, keepdims=True)
    return gyn * r - n * coef, gyp * r - p * coef, d_gn, d_gp


def _heads_fwd_call(q, kv, proj, cs, sn, q_gain, k_gain):
    s = q.shape[0]
    t = min(HEADS_TILE, s)

    def body(q_ref, kv_ref, last_ref, cs_ref, sn_ref, qg_ref, kg_ref, qh_ref, kh_ref, vh_ref):
        lane = lax.broadcasted_iota(jnp.int32, (t, LANE), 1)
        cs_, sn_ = cs_ref[...], sn_ref[...]
        kp = jnp.where(lane < ROPE, last_ref[...], 0.0)
        for h in range(MLA_H):
            c0 = h * HEAD_COLS
            zn, zp = _head_fwd(q_ref[:, c0:c0 + NOPE], q_ref[:, c0 + NOPE:c0 + HEAD_COLS], qg_ref[...], cs_, sn_, lane)
            qh_ref[h, :, :NOPE] = zn.astype(BF16)
            qh_ref[h, :, NOPE:] = zp[:, :ROPE].astype(BF16)
            zn, zp = _head_fwd(kv_ref[:, c0:c0 + NOPE], kp, kg_ref[...], cs_, sn_, lane)
            kh_ref[h, :, :NOPE] = zn.astype(BF16)
            kh_ref[h, :, NOPE:] = zp[:, :ROPE].astype(BF16)
            vh_ref[h] = kv_ref[:, c0 + NOPE:c0 + HEAD_COLS].astype(BF16)

    rows = lambda i: (i, 0)
    whole = lambda i: (0, 0)
    heads = lambda i: (0, i, 0)
    wide = MLA_H * HEAD_COLS
    return pl.pallas_call(
        body, grid=(s // t,),
        in_specs=[pl.BlockSpec((t, wide), rows), pl.BlockSpec((t, wide), rows),
                  pl.BlockSpec((t, LANE), lambda i: (i, PROJ_LAST // LANE)),
                  pl.BlockSpec((t, LANE), rows), pl.BlockSpec((t, LANE), rows),
                  pl.BlockSpec((1, HEAD_COLS), whole), pl.BlockSpec((1, HEAD_COLS), whole)],
        out_specs=[pl.BlockSpec((MLA_H, t, QK), heads), pl.BlockSpec((MLA_H, t, QK), heads), pl.BlockSpec((MLA_H, t, VDIM), heads)],
        out_shape=[jax.ShapeDtypeStruct((MLA_H, s, QK), BF16), jax.ShapeDtypeStruct((MLA_H, s, QK), BF16),
                   jax.ShapeDtypeStruct((MLA_H, s, VDIM), BF16)],
        compiler_params=_cparams("arbitrary"), name="mla_heads_fwd",
    )(q, kv, proj, cs, sn, q_gain, k_gain)


def _heads_bwd_call(q, kv, proj, cs, sn, q_gain, k_gain, dqh, dkh, dvh):
    s = q.shape[0]
    t = min(HEADS_TILE, s)

    def body(q_ref, kv_ref, last_ref, cs_ref, sn_ref, qg_ref, kg_ref, dqh_ref, dkh_ref, dvh_ref,
             dq_ref, dkv_ref, dkr_ref, dqg_ref, dkg_ref):
        lane = lax.broadcasted_iota(jnp.int32, (t, LANE), 1)
        cs_, sn_ = cs_ref[...], sn_ref[...]
        kp = jnp.where(lane < ROPE, last_ref[...], 0.0)
        no_lanes = jnp.zeros((t, LANE - ROPE), F32)
        d_kp = jnp.zeros((t, LANE), F32)
        d_qg = [jnp.zeros((1, NOPE), F32), jnp.zeros((1, LANE), F32)]
        d_kg = [jnp.zeros((1, NOPE), F32), jnp.zeros((1, LANE), F32)]
        for h in range(MLA_H):
            c0 = h * HEAD_COLS
            dz = dqh_ref[h]
            dzp = jnp.concatenate([dz[:, NOPE:], no_lanes], axis=1)
            d_n, d_p, g_n, g_p = _head_bwd(q_ref[:, c0:c0 + NOPE], q_ref[:, c0 + NOPE:c0 + HEAD_COLS], qg_ref[...],
                                           cs_, sn_, lane, dz[:, :NOPE], dzp)
            dq_ref[:, c0:c0 + NOPE] = d_n.astype(dq_ref.dtype)
            dq_ref[:, c0 + NOPE:c0 + HEAD_COLS] = d_p.astype(dq_ref.dtype)
            d_qg = [d_qg[0] + g_n, d_qg[1] + g_p]
            dz = dkh_ref[h]
            dzp = jnp.concatenate([dz[:, NOPE:], no_lanes], axis=1)
            d_n, d_p, g_n, g_p = _head_bwd(kv_ref[:, c0:c0 + NOPE], kp, kg_ref[...], cs_, sn_, lane, dz[:, :NOPE], dzp)
            dkv_ref[:, c0:c0 + NOPE] = d_n.astype(dkv_ref.dtype)
            dkv_ref[:, c0 + NOPE:c0 + HEAD_COLS] = dvh_ref[h].astype(dkv_ref.dtype)
            d_kp = d_kp + d_p
            d_kg = [d_kg[0] + g_n, d_kg[1] + g_p]
        dkr_ref[...] = d_kp
        first = pl.program_id(0) == 0
        _acc_store(dqg_ref.at[:, pl.ds(0, NOPE)], d_qg[0], first)
        _acc_store(dqg_ref.at[:, pl.ds(NOPE, LANE)], d_qg[1], first)
        _acc_store(dkg_ref.at[:, pl.ds(0, NOPE)], d_kg[0], first)
        _acc_store(dkg_ref.at[:, pl.ds(NOPE, LANE)], d_kg[1], first)

    rows = lambda i: (i, 0)
    whole = lambda i: (0, 0)
    heads = lambda i: (0, i, 0)
    wide = MLA_H * HEAD_COLS
    return pl.pallas_call(
        body, grid=(s // t,),
        in_specs=[pl.BlockSpec((t, wide), rows), pl.BlockSpec((t, wide), rows),
                  pl.BlockSpec((t, LANE), lambda i: (i, PROJ_LAST // LANE)),
                  pl.BlockSpec((t, LANE), rows), pl.BlockSpec((t, LANE), rows),
                  pl.BlockSpec((1, HEAD_COLS), whole), pl.BlockSpec((1, HEAD_COLS), whole),
                  pl.BlockSpec((MLA_H, t, QK), heads), pl.BlockSpec((MLA_H, t, QK), heads), pl.BlockSpec((MLA_H, t, VDIM), heads)],
        out_specs=[pl.BlockSpec((t, wide), rows), pl.BlockSpec((t, wide), rows), pl.BlockSpec((t, LANE), rows),
                   pl.BlockSpec((1, HEAD_COLS), whole), pl.BlockSpec((1, HEAD_COLS), whole)],
        out_shape=[jax.ShapeDtypeStruct((s, wide), BF16), jax.ShapeDtypeStruct((s, wide), BF16), jax.ShapeDtypeStruct((s, LANE), F32),
                   jax.ShapeDtypeStruct((1, HEAD_COLS), F32), jax.ShapeDtypeStruct((1, HEAD_COLS), F32)],
        compiler_params=_cparams("arbitrary"), name="mla_heads_bwd",
    )(q, kv, proj, cs, sn, q_gain, k_gain, dqh, dkh, dvh)


CONV_TC = 512
HALO = 8


def _conv_tiles(s):
    return min(512, s)


def _conv_fwd_call(x, col0, w, b):
    s = x.shape[0]
    ts = _conv_tiles(s)
    hb = ts // HALO
    c0 = col0 // CONV_TC
    assert col0 % CONV_TC == 0

    def body(x_ref, prev_ref, w_ref, b_ref, y_ref, buf):
        si = pl.program_id(1)
        buf[0:HALO, :] = jnp.where(si > 0, prev_ref[...], 0.0)
        buf[HALO:, :] = x_ref[...]
        acc = jnp.broadcast_to(b_ref[...], (ts, CONV_TC))
        for k in range(CONV_K):
            acc = acc + w_ref[k:k + 1, :] * buf[pl.ds(HALO - (CONV_K - 1) + k, ts), :]
        y_ref[...] = acc * jax.nn.sigmoid(acc)

    return pl.pallas_call(
        body, grid=(CONV_DIM // CONV_TC, s // ts),
        in_specs=[pl.BlockSpec((ts, CONV_TC), lambda ci, si: (si, ci + c0)),
                  pl.BlockSpec((HALO, CONV_TC), lambda ci, si: (jnp.maximum(si * hb - 1, 0), ci + c0)),
                  pl.BlockSpec((CONV_K, CONV_TC), lambda ci, si: (0, ci)),
                  pl.BlockSpec((1, CONV_TC), lambda ci, si: (0, ci))],
        out_specs=pl.BlockSpec((ts, CONV_TC), lambda ci, si: (si, ci)),
        out_shape=jax.ShapeDtypeStruct((s, CONV_DIM), F32),
        scratch_shapes=[pltpu.VMEM((ts + HALO, CONV_TC), F32)],
        compiler_params=_cparams("arbitrary", "arbitrary"), name="conv_fwd",
    )(x, x, w, b)


def _conv_bwd_call(x, col0, w, b, dy):
    s = x.shape[0]
    ts = _conv_tiles(s)
    hb = ts // HALO
    ns = s // ts
    last_halo = s // HALO - 1
    c0 = col0 // CONV_TC

    def body(x_ref, prev_ref, next_ref, dy_ref, dyn_ref, w_ref, b_ref, dx_ref, dw_ref, db_ref, xbuf, dbuf):
        si = pl.program_id(1)
        xbuf[0:HALO, :] = jnp.where(si > 0, prev_ref[...], 0.0)
        xbuf[HALO:HALO + ts, :] = x_ref[...]
        xbuf[HALO + ts:, :] = next_ref[...]
        pre = jnp.broadcast_to(b_ref[...], (ts + HALO, CONV_TC))
        for k in range(CONV_K):
            pre = pre + w_ref[k:k + 1, :] * xbuf[pl.ds(HALO - (CONV_K - 1) + k, ts + HALO), :]
        sg = jax.nn.sigmoid(pre)
        dsilu = sg * (1.0 + pre * (1.0 - sg))
        dbuf[0:ts, :] = dy_ref[...] * dsilu[0:ts]
        dbuf[ts:, :] = jnp.where(si < ns - 1, dyn_ref[...] * dsilu[ts:], 0.0)
        dx = jnp.zeros((ts, CONV_TC), F32)
        for k in range(CONV_K):
            dx = dx + w_ref[k:k + 1, :] * dbuf[pl.ds(CONV_K - 1 - k, ts), :]
        dx_ref[...] = dx.astype(dx_ref.dtype)
        dpre = dbuf[0:ts, :]
        first = si == 0
        _acc_store(db_ref, jnp.sum(dpre, axis=0, keepdims=True), first)
        for k in range(CONV_K):
            dw_k = jnp.sum(dpre * xbuf[pl.ds(HALO - (CONV_K - 1) + k, ts), :], axis=0, keepdims=True)
            _acc_store(dw_ref.at[pl.ds(k, 1), :], dw_k, first)

    main = lambda ci, si: (si, ci)
    x_main = lambda ci, si: (si, ci + c0)
    x_prev = lambda ci, si: (jnp.maximum(si * hb - 1, 0), ci + c0)
    x_next = lambda ci, si: (jnp.minimum(si * hb + hb, last_halo), ci + c0)
    return pl.pallas_call(
        body, grid=(CONV_DIM // CONV_TC, ns),
        in_specs=[pl.BlockSpec((ts, CONV_TC), x_main), pl.BlockSpec((HALO, CONV_TC), x_prev), pl.BlockSpec((HALO, CONV_TC), x_next),
                  pl.BlockSpec((ts, CONV_TC), main),
                  pl.BlockSpec((HALO, CONV_TC), lambda ci, si: (jnp.minimum(si * hb + hb, last_halo), ci)),
                  pl.BlockSpec((CONV_K, CONV_TC), lambda ci, si: (0, ci)),
                  pl.BlockSpec((1, CONV_TC), lambda ci, si: (0, ci))],
        out_specs=[pl.BlockSpec((ts, CONV_TC), main),
                   pl.BlockSpec((CONV_K, CONV_TC), lambda ci, si: (0, ci)),
                   pl.BlockSpec((1, CONV_TC), lambda ci, si: (0, ci))],
        out_shape=[jax.ShapeDtypeStruct((s, CONV_DIM), BF16), jax.ShapeDtypeStruct((CONV_K, CONV_DIM), F32),
                   jax.ShapeDtypeStruct((1, CONV_DIM), F32)],
        scratch_shapes=[pltpu.VMEM((ts + 2 * HALO, CONV_TC), F32), pltpu.VMEM((ts + HALO, CONV_TC), F32)],
        compiler_params=_cparams("arbitrary", "arbitrary"), name="conv_bwd",
    )(x, x, x, dy, dy, w, b)


GW = SSD_HPG * SSD_P
B_COL = SSD_DI
C_COL = SSD_DI + SSD_G * SSD_N


def _ones_where(mask):
    return jnp.where(mask, 1.0, 0.0).astype(BF16)


def _split(v, passes):
    parts, rest = [], v
    for i in range(passes):
        part = rest.astype(BF16)
        parts.append(part)
        if i + 1 < passes:
            rest = rest - part.astype(F32)
    return parts


def _dot_sel_r(v, sel, passes=3):
    out = None
    for part in _split(v, passes):
        t = jnp.dot(part, sel, preferred_element_type=F32)
        out = t if out is None else out + t
    return out


def _dot_sel_l(sel, v, passes=3):
    out = None
    for part in _split(v, passes):
        t = jnp.dot(sel, part, preferred_element_type=F32)
        out = t if out is None else out + t
    return out


def _ssd_consts():
    r = lax.broadcasted_iota(jnp.int32, (SSD_L, SSD_L), 0)
    c = lax.broadcasted_iota(jnp.int32, (SSD_L, SSD_L), 1)
    tril = r >= c
    triu = c >= r
    shift = SSD_P.bit_length() - 1
    eh = lax.broadcasted_iota(jnp.int32, (SSD_H, SSD_DI), 0)
    ej = lax.broadcasted_iota(jnp.int32, (SSD_H, SSD_DI), 1)
    expand = _ones_where(lax.shift_right_logical(ej, shift) == eh)
    rj = lax.broadcasted_iota(jnp.int32, (SSD_DI, SSD_H), 0)
    rh = lax.broadcasted_iota(jnp.int32, (SSD_DI, SSD_H), 1)
    reduce_ = _ones_where(lax.shift_right_logical(rj, shift) == rh)
    lane = lax.broadcasted_iota(jnp.int32, (SSD_L, LANE), 1)
    return tril, triu, expand, reduce_, lane < SSD_P


def _ssd_decays(dt, dt_t, a, a_t, tril, triu, expand):
    dta = dt * a
    acum = _dot_sel_l(_ones_where(tril), dta)
    acum_t = _dot_sel_r(dt_t * a_t, _ones_where(triu))
    dta_e = _dot_sel_r(dta, expand)
    acum_e = _dot_sel_r(acum, expand)
    last_e = jnp.sum(dta_e, axis=0, keepdims=True)
    return acum, acum_t, acum_e, last_e


def _head_decay(acum, acum_t, h, tril):
    seg = acum[:, h:h + 1] - acum_t[h:h + 1, :]
    return jnp.exp(jnp.where(tril, seg, NEG))


def _ssd_fwd_call(xbc, dt, a):
    s = xbc.shape[0]
    nc = s // SSD_L
    dt_t = dt.T
    a_t = a.T

    def body(xbc_ref, dt_ref, dtt_ref, a_ref, at_ref, y_ref, st_ref, s_sc):
        ci = pl.program_id(0)

        @pl.when(ci == 0)
        def _():
            s_sc[...] = jnp.zeros_like(s_sc)

        st_ref[0] = s_sc[...]
        tril, triu, expand, _, low_half = _ssd_consts()
        acum, acum_t, acum_e, last_e = _ssd_decays(dt_ref[...], dtt_ref[...], a_ref[...], at_ref[...], tril, triu, expand)
        dt_e = _dot_sel_r(dt_ref[...], expand, passes=2)
        xdt = xbc_ref[:, :SSD_DI] * dt_e
        xdt_b = xdt.astype(BF16)
        xw_b = (xdt * jnp.exp(last_e - acum_e)).astype(BF16)
        ea_e = jnp.exp(acum_e)
        el_e = jnp.exp(last_e)
        for g in range(SSD_G):
            gs = slice(g * GW, (g + 1) * GW)
            bg = xbc_ref[:, B_COL + g * SSD_N:B_COL + (g + 1) * SSD_N]
            cg_b = xbc_ref[:, C_COL + g * SSD_N:C_COL + (g + 1) * SSD_N].astype(BF16)
            bg_b = bg.astype(BF16)
            cb = _nt(cg_b, bg_b)
            st = s_sc[:, gs]
            y_off = jnp.dot(cg_b, st.astype(BF16), preferred_element_type=F32) * ea_e[:, gs]
            for pr in range(SSD_HPG // 2):
                ls = slice(g * GW + pr * LANE, g * GW + (pr + 1) * LANE)
                xp = xdt_b[:, ls]
                yd = []
                for half in range(2):
                    h = g * SSD_HPG + pr * 2 + half
                    m = (cb * _head_decay(acum, acum_t, h, tril)).astype(BF16)
                    yd.append(jnp.dot(m, xp, preferred_element_type=F32))
                y_ref[:, ls] = jnp.where(low_half, yd[0], yd[1]) + y_off[:, pr * LANE:(pr + 1) * LANE]
            s_sc[:, gs] = st * el_e[:, gs] + jnp.dot(bg.T.astype(BF16), xw_b[:, gs], preferred_element_type=F32)

    row = lambda i: (i, 0)
    return pl.pallas_call(
        body, grid=(nc,),
        in_specs=[pl.BlockSpec((SSD_L, CONV_DIM), row), pl.BlockSpec((SSD_L, SSD_H), row),
                  pl.BlockSpec((SSD_H, SSD_L), lambda i: (0, i)), pl.BlockSpec((1, SSD_H), lambda i: (0, 0)),
                  pl.BlockSpec((SSD_H, 1), lambda i: (0, 0))],
        out_specs=[pl.BlockSpec((SSD_L, SSD_DI), row), pl.BlockSpec((1, SSD_N, SSD_DI), lambda i: (i, 0, 0))],
        out_shape=[jax.ShapeDtypeStruct((s, SSD_DI), F32), jax.ShapeDtypeStruct((nc, SSD_N, SSD_DI), F32)],
        scratch_shapes=[pltpu.VMEM((SSD_N, SSD_DI), F32)],
        compiler_params=_cparams("arbitrary"), name="ssd_fwd",
    )(xbc, dt, dt_t, a, a_t)


def _ssd_bwd_call(xbc, dt, a, states, dy, dx_extra):
    s = xbc.shape[0]
    nc = s // SSD_L
    dt_t = dt.T
    a_t = a.T

    def body(xbc_ref, dt_ref, dtt_ref, a_ref, at_ref, st_ref, dy_ref, dxe_ref,
             dxbc_ref, ddt_ref, da_ref, ds_sc, yf_sc, dxd_sc, dxw_sc):
        i = pl.program_id(0)

        @pl.when(i == 0)
        def _():
            ds_sc[...] = jnp.zeros_like(ds_sc)

        tril, triu, expand, reduce_, low_half = _ssd_consts()
        dt = dt_ref[...]
        a_row = a_ref[...]
        acum, acum_t, acum_e, last_e = _ssd_decays(dt, dtt_ref[...], a_row, at_ref[...], tril, triu, expand)
        dt_e = _dot_sel_r(dt, expand, passes=2)
        x = xbc_ref[:, :SSD_DI]
        xdt = x * dt_e
        xdt_b = xdt.astype(BF16)
        w_e = jnp.exp(last_e - acum_e)
        xw_b = (xdt * w_e).astype(BF16)
        ea_e = jnp.exp(acum_e)
        el_e = jnp.exp(last_e)
        dy = dy_ref[...]
        dy_b = dy.astype(BF16)
        s_prev = st_ref[0]
        ds_new = ds_sc[...]
        ds_new_b = ds_new.astype(BF16)
        triu_b = _ones_where(triu)
        strict_tril = jnp.logical_not(triu)
        head_ids = lax.broadcasted_iota(jnp.int32, (1, SSD_H), 1)
        d_dta_diag = jnp.zeros((SSD_L, SSD_H), F32)
        for g in range(SSD_G):
            gs = slice(g * GW, (g + 1) * GW)
            bs_ = slice(B_COL + g * SSD_N, B_COL + (g + 1) * SSD_N)
            cs_ = slice(C_COL + g * SSD_N, C_COL + (g + 1) * SSD_N)
            bg = xbc_ref[:, bs_]
            cg = xbc_ref[:, cs_]
            bg_b, cg_b = bg.astype(BF16), cg.astype(BF16)
            st_b = s_prev[:, gs].astype(BF16)
            y_off = jnp.dot(cg_b, st_b, preferred_element_type=F32) * ea_e[:, gs]
            yf_sc[:, gs] = y_off
            dz_b = (dy[:, gs] * ea_e[:, gs]).astype(BF16)
            d_c = _nt(dz_b, st_b)
            ds_prev = ds_new[:, gs] * el_e[:, gs] + jnp.dot(cg.T.astype(BF16), dz_b, preferred_element_type=F32)
            dxw_sc[:, gs] = jnp.dot(bg_b, ds_new_b[:, gs], preferred_element_type=F32)
            d_b = _nt(xw_b[:, gs], ds_new_b[:, gs])
            cb = _nt(cg_b, bg_b)
            d_g = jnp.zeros((SSD_L, SSD_L), F32)
            for pr in range(SSD_HPG // 2):
                ls = slice(g * GW + pr * LANE, g * GW + (pr + 1) * LANE)
                xp = xdt_b[:, ls]
                dyp = dy[:, ls]
                dyp_b = dy_b[:, ls]
                dxd = []
                for half in range(2):
                    h = g * SSD_HPG + pr * 2 + half
                    dec = _head_decay(acum, acum_t, h, tril)
                    m = cb * dec
                    dxd.append(jnp.dot(m.T.astype(BF16), dyp_b, preferred_element_type=F32))
                    mine = low_half if half == 0 else jnp.logical_not(low_half)
                    d_m = _nt(jnp.where(mine, dyp, 0.0).astype(BF16), xp)
                    d_g = d_g + d_m * dec
                    below = jnp.dot(triu_b, (d_m * m).astype(BF16), preferred_element_type=F32)
                    col = jnp.sum(jnp.where(strict_tril, below, 0.0), axis=1, keepdims=True)
                    d_dta_diag = d_dta_diag + col * jnp.where(head_ids == h, 1.0, 0.0)
                dxd_sc[:, ls] = jnp.where(low_half, dxd[0], dxd[1])
            d_g_b = d_g.astype(BF16)
            dxbc_ref[:, cs_] = d_c + jnp.dot(d_g_b, bg_b, preferred_element_type=F32)
            dxbc_ref[:, bs_] = d_b + jnp.dot(d_g.T.astype(BF16), cg_b, preferred_element_type=F32)
            ds_sc[:, gs] = ds_prev
        dxw = dxw_sc[...]
        dxd = dxd_sc[...]
        dw_e = xdt * dxw * w_e
        d_out = _dot_sel_r(dy * yf_sc[...], reduce_, passes=2)
        d_upd = _dot_sel_r(dw_e, reduce_, passes=2)
        d_tot_e = jnp.sum(ds_new * s_prev, axis=0, keepdims=True) * el_e
        d_tot = _dot_sel_r(jnp.broadcast_to(d_tot_e, (8, SSD_DI)), reduce_, passes=2)[0:1]
        d_dta = _dot_sel_l(triu_b, d_out) + _dot_sel_l(_ones_where(strict_tril), d_upd) + d_tot + d_dta_diag
        dxdt = dxd + dxw * w_e
        dxbc_ref[:, :SSD_DI] = dxdt * dt_e + dxe_ref[...]
        ddt_ref[...] = d_dta * a_row + _dot_sel_r(dxdt * x, reduce_, passes=2)
        _acc_store(da_ref, jnp.sum(d_dta * dt, axis=0, keepdims=True), i == 0)

    rev = lambda i: (nc - 1 - i, 0)
    return pl.pallas_call(
        body, grid=(nc,),
        in_specs=[pl.BlockSpec((SSD_L, CONV_DIM), rev), pl.BlockSpec((SSD_L, SSD_H), rev),
                  pl.BlockSpec((SSD_H, SSD_L), lambda i: (0, nc - 1 - i)), pl.BlockSpec((1, SSD_H), lambda i: (0, 0)),
                  pl.BlockSpec((SSD_H, 1), lambda i: (0, 0)),
                  pl.BlockSpec((1, SSD_N, SSD_DI), lambda i: (nc - 1 - i, 0, 0)),
                  pl.BlockSpec((SSD_L, SSD_DI), rev), pl.BlockSpec((SSD_L, SSD_DI), rev)],
        out_specs=[pl.BlockSpec((SSD_L, CONV_DIM), rev), pl.BlockSpec((SSD_L, SSD_H), rev),
                   pl.BlockSpec((1, SSD_H), lambda i: (0, 0))],
        out_shape=[jax.ShapeDtypeStruct((s, CONV_DIM), F32), jax.ShapeDtypeStruct((s, SSD_H), F32),
                   jax.ShapeDtypeStruct((1, SSD_H), F32)],
        scratch_shapes=[pltpu.VMEM((SSD_N, SSD_DI), F32), pltpu.VMEM((SSD_L, SSD_DI), F32),
                        pltpu.VMEM((SSD_L, SSD_DI), F32), pltpu.VMEM((SSD_L, SSD_DI), F32)],
        compiler_params=_cparams("arbitrary"), name="ssd_bwd",
    )(xbc, dt, dt_t, a, a_t, states, dy, dx_extra)


HBM_SPEC = pl.BlockSpec(memory_space=pltpu.HBM)
N_PEERS = N_DEV - 1


def _flip(v, f):
    return 1 - v if f else v


def _all_gather(shard):
    rows, c = shard.shape

    def body(x_ref, out_ref, send_sems, recv_sems, local_sem):
        x, y, cc = lax.axis_index("x"), lax.axis_index("y"), lax.axis_index("c")
        me, sibling = (x, y, cc), (x, y, 1 - cc)
        chips = [(1 - x, y), (x, 1 - y), (1 - x, 1 - y)]

        def slot(px, py, pc):
            return out_ref.at[4 * px + 2 * py + pc]

        def copy(k, block, to, src=None):
            return pltpu.make_async_remote_copy(
                src_ref=slot(*block) if src is None else src, dst_ref=slot(*block),
                send_sem=send_sems.at[k], recv_sem=recv_sems.at[k],
                device_id=to, device_id_type=pl.DeviceIdType.MESH)

        mine = pltpu.make_async_copy(x_ref, slot(*me), local_sem)
        mine.start()
        first = [copy(0, me, sibling, src=x_ref)]
        first += [copy(1 + j, me, (*chip, cc), src=x_ref) for j, chip in enumerate(chips)]
        for cp in first:
            cp.start()
        passed = [copy(4 + j, (*chip, cc), sibling) for j, chip in enumerate(chips)]
        for j, chip in enumerate(chips):
            copy(1 + j, (*chip, cc), me).wait_recv()
            passed[j].start()
        copy(0, sibling, me).wait_recv()
        for j, chip in enumerate(chips):
            copy(4 + j, (*chip, 1 - cc), me).wait_recv()
        for cp in first + passed:
            cp.wait_send()
        mine.wait()

    return pl.pallas_call(
        body, out_shape=jax.ShapeDtypeStruct((N_DEV, rows, c), shard.dtype),
        in_specs=[HBM_SPEC], out_specs=HBM_SPEC,
        scratch_shapes=[pltpu.SemaphoreType.DMA((N_PEERS,)), pltpu.SemaphoreType.DMA((N_PEERS,)), pltpu.SemaphoreType.DMA(())],
        name="all_gather",
    )(shard)


def _peer_copies(src_ref, out_ref, sems, gather, phase):
    send_sems, recv_sems, local_sem = sems
    x, y, cc = lax.axis_index("x"), lax.axis_index("y"), lax.axis_index("c")
    me = 4 * x + 2 * y + cc
    mine = pltpu.make_async_copy(src_ref if gather else src_ref.at[me], out_ref.at[me], local_sem)
    copies = []
    for k in range(1, N_DEV):
        px, py, pc = _flip(x, k & 4), _flip(y, k & 2), _flip(cc, k & 1)
        peer = 4 * px + 2 * py + pc
        src = src_ref if gather else src_ref.at[peer]
        copies.append((
            pltpu.make_async_remote_copy(
                src_ref=src, dst_ref=out_ref.at[me], send_sem=send_sems.at[k - 1], recv_sem=recv_sems.at[k - 1],
                device_id=(px, py, pc), device_id_type=pl.DeviceIdType.MESH),
            pltpu.make_async_remote_copy(
                src_ref=src, dst_ref=out_ref.at[peer], send_sem=send_sems.at[k - 1], recv_sem=recv_sems.at[k - 1],
                device_id=(px, py, pc), device_id_type=pl.DeviceIdType.MESH)))
    if phase == "start":
        mine.start()
        for send, _ in copies:
            send.start()
    else:
        for _, landed in copies:
            landed.wait_recv()
        for send, _ in copies:
            send.wait_send()
        mine.wait()


def _comm_scratch():
    return [pltpu.SemaphoreType.DMA((N_PEERS,)), pltpu.SemaphoreType.DMA((N_PEERS,)), pltpu.SemaphoreType.DMA(())]


def _gather_rider(shard):
    return dict(src=shard, out=jax.ShapeDtypeStruct((N_DEV,) + shard.shape, shard.dtype), gather=True)


def _exchange_rider(blocks):
    return dict(src=blocks, out=jax.ShapeDtypeStruct(blocks.shape, blocks.dtype), gather=False)


def _exchange_blocks(blocks):
    def body(g_ref, out_ref, *sems):
        _peer_copies(g_ref, out_ref, sems, False, "start")
        _peer_copies(g_ref, out_ref, sems, False, "finish")

    return pl.pallas_call(
        body, out_shape=jax.ShapeDtypeStruct(blocks.shape, blocks.dtype),
        in_specs=[HBM_SPEC], out_specs=HBM_SPEC, scratch_shapes=_comm_scratch(), name="exchange_blocks",
    )(blocks)


BIG = [
    ("ffn1_w13", (D_MODEL, 2 * D_FF), 1), ("ffn1_w2", (D_FF, D_MODEL), 0),
    ("w_ssd_out", (SSD_DI, D_MODEL), 0), ("w_uq", (Q_LORA, MLA_H * QK), 1), ("w_ukv", (KV_LORA, MLA_H * (NOPE + VDIM)), 1),
    ("w_mla_out", (MLA_H * VDIM, D_MODEL), 0), ("w_o", (D_MODEL, D_MODEL), 0),
    ("ffn2_w13", (D_MODEL, 2 * D_FF), 1), ("ffn2_w2", (D_FF, D_MODEL), 0), ("w_in", (D_MODEL, D_IN), 1),
]
assert all(_r % 16 == 0 for _r in [_f[0] * _f[1] // N_DEV // PACK_COLS for _, _f, _ in BIG[:-1]])
SMALL = [
    ("ln_ffn1", D_MODEL), ("ln_mix", D_MODEL), ("conv_b", CONV_DIM), ("dt_bias", SSD_H), ("a_log", SSD_H), ("d_skip", SSD_H),
    ("ssd_norm", SSD_DI), ("q_lora_norm", Q_LORA), ("kv_lora_norm", KV_LORA), ("q_norm", QK), ("k_norm", QK), ("ln_ffn2", D_MODEL),
]


def _shard_shape(full, axis):
    k, n = full
    return (k // N_DEV, n) if axis == 0 else (k, n // N_DEV)


def _shard_rows(full):
    return full[0] * full[1] // N_DEV // PACK_COLS


LAYER_ROWS = sum(_shard_rows(f) for _, f, _ in BIG)
LAYER_ROWS_PAD = -(-LAYER_ROWS // 256) * 256


def _pack_shards(shards):
    parts = [(shards[name] if axis == 0 else shards[name].T).reshape(-1, PACK_COLS) for name, _, axis in BIG]
    pad = LAYER_ROWS_PAD - LAYER_ROWS
    if pad:
        parts.append(jnp.zeros((pad, PACK_COLS), parts[0].dtype))
    return jnp.concatenate(parts, axis=0)


def _unpack_shards(packed):
    out, r = {}, 0
    for name, full, axis in BIG:
        n = _shard_rows(full)
        k, c = _shard_shape(full, axis)
        blk = packed[r:r + n]
        out[name] = blk.reshape(k, c) if axis == 0 else blk.reshape(c, k).T
        r += n
    return out


def _working_shape(full, axis):
    return full if axis == 0 else full[::-1]


def _unpack_gathered(gathered):
    out, r = {}, 0
    for name, full, axis in BIG:
        n = _shard_rows(full)
        out[name] = gathered[:, r:r + n].reshape(_working_shape(full, axis))
        r += n
    return out


def _pack_full_grads(grads):
    parts = [grads[name].reshape(N_DEV, -1, PACK_COLS) for name, _, _ in BIG]
    pad = LAYER_ROWS_PAD - LAYER_ROWS
    if pad:
        parts.append(jnp.zeros((N_DEV, pad, PACK_COLS), parts[0].dtype))
    return jnp.concatenate(parts, axis=1)


SMALL_COLS = sum(n for _, n in SMALL) + CONV_K * CONV_DIM
SMALL_ROWS = -(-(DEPTH * SMALL_COLS) // (8 * PACK_COLS)) * 8


def _pack_small(vals, conv_w):
    flat = jnp.concatenate([vals[name] for name, _ in SMALL] + [conv_w.reshape(DEPTH, -1)], axis=1).reshape(-1)
    flat = jnp.concatenate([flat, jnp.zeros((SMALL_ROWS * PACK_COLS - flat.shape[0],), F32)])
    return flat.reshape(SMALL_ROWS, PACK_COLS)


def _unpack_small(packed):
    flat = packed.reshape(-1)[:DEPTH * SMALL_COLS].reshape(DEPTH, SMALL_COLS)
    out, c = {}, 0
    for name, n in SMALL:
        out[name] = flat[:, c:c + n]
        c += n
    return out, flat[:, c:].reshape(DEPTH, CONV_K, CONV_DIM)


_IN_OFFS = [sum(IN_SPLIT[:i]) for i in range(len(IN_SPLIT) + 1)]


def _arrange_w_in(w_t):
    z, xbc, dt, cq, ckv, kr, gates = [w_t[_IN_OFFS[i]:_IN_OFFS[i + 1]] for i in range(len(IN_SPLIT))]
    pad = jnp.zeros((LANE - ROPE - SSD_H, w_t.shape[1]), w_t.dtype)
    return jnp.concatenate([z, gates, xbc, cq, ckv, kr, dt, pad], axis=0)


def _restore_w_in(g):
    z, gates, xbc = g[PROJ_Z:PROJ_GATES], g[PROJ_GATES:PROJ_XBC], g[PROJ_XBC:PROJ_CQ]
    cq, ckv = g[PROJ_CQ:PROJ_CKV], g[PROJ_CKV:PROJ_LAST]
    kr, dt = g[PROJ_LAST:PROJ_LAST + ROPE], g[PROJ_LAST + ROPE:PROJ_LAST + ROPE + SSD_H]
    return jnp.concatenate([z, xbc, dt, cq, ckv, kr, gates], axis=0)


def _pad_heads(w_t):
    k = w_t.shape[1]
    return jnp.pad(w_t.reshape(MLA_H, QK, k), ((0, 0), (0, HEAD_COLS - QK), (0, 0))).reshape(MLA_H * HEAD_COLS, k)


def _unpad_heads(g):
    k = g.shape[1]
    return g.reshape(MLA_H, HEAD_COLS, k)[:, :QK].reshape(MLA_H * QK, k)


def _row(v):
    return v.reshape(1, -1)


def _head_gain(g):
    return jnp.pad(g, (0, HEAD_COLS - QK)).reshape(1, HEAD_COLS)


def _ffn_fwd(h, ln, w13_t, w2, name):
    n = _row_fwd(_f_rmsnorm, [h], [_row(ln)], [BF16], name + "_fwd")[0]
    act, gate, up = _ffn_up_call(n, w13_t)
    return _mm(act, w2, alpha=0.5, res=h), (h, n, gate, up, act)


def _ffn_bwd(dh_out, saved, ln, w13_t, w2, name):
    h, n, gate, up, act = saved
    d_gate, d_up = _ffn_down_bwd_call(dh_out, w2, gate, up)
    d_w2 = _mm(act, dh_out, ta=True, out_dtype=BF16, alpha=0.5)
    d_n = _mm(d_gate, w13_t, b_rows=(0, D_FF))
    d_n = _mm(d_up, w13_t, out_dtype=BF16, b_rows=(D_FF, D_FF), res=d_n)
    d_w13_t = jnp.concatenate([_mm(d_gate, n, ta=True, out_dtype=BF16), _mm(d_up, n, ta=True, out_dtype=BF16)], axis=0)
    (dh,), (d_ln,) = _row_bwd(_f_rmsnorm, [h], [_row(ln)], [d_n], [F32], name + "_bwd", add={0: dh_out})
    return dh, d_w13_t, d_w2, d_ln[0]


def _mixer_fwd(h, big, small, conv_w, cs, sn, rider=None):
    s = h.shape[0]
    u = _row_fwd(_f_rmsnorm, [h], [_row(small["ln_mix"])], [BF16], "ln_mix_fwd")[0]
    proj = _mm(u, big["w_in"], tb=True)
    xbc = _conv_fwd_call(proj, PROJ_XBC, conv_w, _row(small["conv_b"]))
    dt_in = proj[:, PROJ_LAST + ROPE:PROJ_LAST + ROPE + SSD_H] + small["dt_bias"][None, :]
    dt = jax.nn.softplus(dt_in)
    a = -jnp.exp(small["a_log"])[None, :]
    y_scan, states = _ssd_fwd_call(xbc, dt, a)
    dsk = _row(jnp.repeat(small["d_skip"], SSD_P))
    gn_in = [y_scan, _win(xbc, 0, SSD_DI), _win(proj, PROJ_Z, SSD_DI)]
    yn = _row_fwd(_f_gated_norm, gn_in, [dsk, _row(small["ssd_norm"])], [BF16], "gated_norm_fwd")[0]
    y_ssd = _mm(yn, big["w_ssd_out"])
    qn = _row_fwd(_f_rmsnorm, [_win(proj, PROJ_CQ, Q_LORA)], [_row(small["q_lora_norm"])], [BF16], "q_lora_norm_fwd")[0]
    kvn = _row_fwd(_f_rmsnorm, [_win(proj, PROJ_CKV, KV_LORA)], [_row(small["kv_lora_norm"])], [BF16], "kv_lora_norm_fwd")[0]
    q = _mm(qn, big["w_uq"], tb=True)
    kv = _mm(kvn, big["w_ukv"], tb=True)
    qh, kh, vh = _heads_fwd_call(q, kv, proj, cs, sn, _head_gain(small["q_norm"]), _head_gain(small["k_norm"]))
    o, lse, *carried = _attn_fwd_call(qh, kh, vh, rider)
    o_rows = jnp.transpose(o, (1, 0, 2)).reshape(s, MLA_H * VDIM)
    y_mla = _mm(o_rows, big["w_mla_out"])
    mg = _row_fwd(_f_merge, [_win(proj, PROJ_GATES, 2 * D_MODEL), y_ssd, y_mla], [], [BF16], "merge_fwd")[0]
    out = _mm(mg, big["w_o"], res=h)
    saved = (h, u, proj, xbc, dt_in, dt, a, y_scan, states, dsk, yn, y_ssd, qn, kvn, q, kv, qh, kh, vh, o, lse, o_rows, y_mla, mg)
    return out, saved, (carried[0] if carried else None)


def _mixer_bwd(dh_out, saved, big, small, conv_w, cs, sn, rider=None):
    (h, u, proj, xbc, dt_in, dt, a, y_scan, states, dsk, yn, y_ssd, qn, kvn, q, kv, qh, kh, vh, o, lse, o_rows, y_mla, mg) = saved
    s = h.shape[0]
    d_big, d_small = {}, {}
    d_mg = _mm(dh_out, big["w_o"], tb=True, out_dtype=BF16)
    d_big["w_o"] = _mm(mg, dh_out, ta=True, out_dtype=BF16)
    merge_in = [_win(proj, PROJ_GATES, 2 * D_MODEL), y_ssd, y_mla]
    (d_gates, d_y_ssd, d_y_mla), _ = _row_bwd(_f_merge, merge_in, [], [d_mg], [BF16, BF16, BF16], "merge_bwd", bwd=_b_merge)
    d_o_rows = _mm(d_y_mla, big["w_mla_out"], tb=True, out_dtype=BF16)
    d_big["w_mla_out"] = _mm(o_rows, d_y_mla, ta=True, out_dtype=BF16)
    d_o = jnp.transpose(d_o_rows.reshape(s, MLA_H, VDIM), (1, 0, 2))
    delta = _attn_delta_call(o, d_o)
    *d_heads, carried = list(_attn_bwd_call(qh, kh, vh, d_o, lse.reshape(MLA_H, 1, s), delta.reshape(MLA_H, 1, s), rider)) + ([None] if rider is None else [])
    d_q, d_kv, d_kr, d_qg, d_kg = _heads_bwd_call(
        q, kv, proj, cs, sn, _head_gain(small["q_norm"]), _head_gain(small["k_norm"]), *d_heads)
    d_small["q_norm"], d_small["k_norm"] = d_qg[0, :QK], d_kg[0, :QK]
    d_qn = _mm(d_q, big["w_uq"], out_dtype=BF16)
    d_big["w_uq"] = _mm(d_q, qn, ta=True, out_dtype=BF16)
    d_kvn = _mm(d_kv, big["w_ukv"], out_dtype=BF16)
    d_big["w_ukv"] = _mm(d_kv, kvn, ta=True, out_dtype=BF16)
    (d_cq,), (d_g,) = _row_bwd(_f_rmsnorm, [_win(proj, PROJ_CQ, Q_LORA)], [_row(small["q_lora_norm"])], [d_qn], [BF16], "q_lora_norm_bwd")
    d_small["q_lora_norm"] = d_g[0]
    (d_ckv,), (d_g,) = _row_bwd(_f_rmsnorm, [_win(proj, PROJ_CKV, KV_LORA)], [_row(small["kv_lora_norm"])], [d_kvn], [BF16], "kv_lora_norm_bwd")
    d_small["kv_lora_norm"] = d_g[0]
    d_yn = _mm(d_y_ssd, big["w_ssd_out"], tb=True, out_dtype=BF16)
    d_big["w_ssd_out"] = _mm(yn, d_y_ssd, ta=True, out_dtype=BF16)
    gn_in = [y_scan, _win(xbc, 0, SSD_DI), _win(proj, PROJ_Z, SSD_DI)]
    (d_y_scan, d_xs, d_z), (d_dsk, d_g) = _row_bwd(
        _f_gated_norm, gn_in, [dsk, _row(small["ssd_norm"])], [d_yn], [F32, F32, BF16], "gated_norm_bwd")
    d_small["ssd_norm"] = d_g[0]
    d_small["d_skip"] = jnp.sum(d_dsk.reshape(SSD_H, SSD_P), axis=1)
    d_xbc_act, d_dt, d_a = _ssd_bwd_call(xbc, dt, a, states, d_y_scan, d_xs)
    d_xbc, d_conv_w, d_conv_b = _conv_bwd_call(proj, PROJ_XBC, conv_w, _row(small["conv_b"]), d_xbc_act)
    d_small["conv_b"] = d_conv_b[0]
    d_dt_in = d_dt * jax.nn.sigmoid(dt_in)
    d_small["dt_bias"] = jnp.sum(d_dt_in, axis=0)
    d_small["a_log"] = d_a[0] * a[0]
    d_last = (d_kr + jnp.pad(d_dt_in, ((0, 0), (ROPE, LANE - ROPE - SSD_H)))).astype(BF16)
    d_proj = jnp.concatenate([d_z, d_gates, d_xbc, d_cq, d_ckv, d_last], axis=1)
    d_u = _mm(d_proj, big["w_in"], out_dtype=BF16)
    d_big["w_in"] = _mm(d_proj, u, ta=True, out_dtype=BF16)
    (dh,), (d_ln,) = _row_bwd(_f_rmsnorm, [h], [_row(small["ln_mix"])], [d_u], [F32], "ln_mix_bwd", add={0: dh_out})
    d_small["ln_mix"] = d_ln[0]
    return dh, d_big, d_small, d_conv_w, carried


def _prepare_big(b):
    return dict(b, w_in=_arrange_w_in(b["w_in"]), w_uq=_pad_heads(b["w_uq"]))


def _local_step(x, positions, target, big, small, conv_w, packed_last=None):
    inv = 1.0 / (ROPE_THETA ** (jnp.arange(0, ROPE, 2, dtype=F32) / ROPE))
    ang = positions.astype(F32)[:, None] * inv
    cos, sin = jnp.cos(ang), jnp.sin(ang)
    no_lanes = jnp.zeros((x.shape[0], LANE - ROPE), F32)
    cs = jnp.concatenate([cos, cos, no_lanes], axis=1)
    sn = jnp.concatenate([-sin, sin, no_lanes], axis=1)
    carrier = DEPTH - 2 if packed_last is not None else None
    big = [None if b is None else _prepare_big(b) for b in big]
    layer_small = [{k: v[l] for k, v in small.items()} for l in range(DEPTH)]

    h, saved = x, []
    for l in range(DEPTH):
        b, sm = big[l], layer_small[l]
        h, s1 = _ffn_fwd(h, sm["ln_ffn1"], b["ffn1_w13"], b["ffn1_w2"], "ln_ffn1")
        h, s2, gathered = _mixer_fwd(h, b, sm, conv_w[l], cs, sn, _gather_rider(packed_last) if l == carrier else None)
        if gathered is not None:
            big[l + 1] = _prepare_big(_unpack_gathered(gathered))
        h, s3 = _ffn_fwd(h, sm["ln_ffn2"], b["ffn2_w13"], b["ffn2_w2"], "ln_ffn2")
        saved.append((s1, s2, s3))
    loss, dh = _loss_and_grad(h, target)

    d_big, d_small, d_conv_w = [None] * DEPTH, [None] * DEPTH, [None] * DEPTH
    for l in reversed(range(DEPTH)):
        b, sm = big[l], layer_small[l]
        s1, s2, s3 = saved[l]
        dh, d_w13_2, d_w2_2, d_ln2 = _ffn_bwd(dh, s3, sm["ln_ffn2"], b["ffn2_w13"], b["ffn2_w2"], "ln_ffn2")
        rider = _exchange_rider(_pack_full_grads(d_big[l + 1])) if l == carrier else None
        dh, db, ds, d_conv_w[l], received = _mixer_bwd(dh, s2, b, sm, conv_w[l], cs, sn, rider)
        if received is not None:
            d_big[l + 1] = received
        dh, d_w13_1, d_w2_1, d_ln1 = _ffn_bwd(dh, s1, sm["ln_ffn1"], b["ffn1_w13"], b["ffn1_w2"], "ln_ffn1")
        db.update(ffn1_w13=d_w13_1, ffn1_w2=d_w2_1, ffn2_w13=d_w13_2, ffn2_w2=d_w2_2,
                  w_in=_restore_w_in(db["w_in"]), w_uq=_unpad_heads(db["w_uq"]))
        ds.update(ln_ffn1=d_ln1, ln_ffn2=d_ln2)
        d_big[l], d_small[l] = db, ds
    d_small = {name: jnp.stack([d_small[l][name] for l in range(DEPTH)]) for name, _ in SMALL}
    return loss, dh, d_big, d_small, jnp.stack(d_conv_w)


def _step(args):
    dev = 4 * lax.axis_index("x") + 2 * lax.axis_index("y") + lax.axis_index("c")
    x, positions, target = args["x"][0], args["positions"][0], args["loss_target"][0]

    packed = [_pack_shards({name: args[name][l].astype(BF16) for name, _, _ in BIG}) for l in range(DEPTH)]
    big = [_unpack_gathered(_all_gather(packed[l])) for l in range(DEPTH - 1)] + [None]
    cw = args["conv_w"]
    cw_cols = cw.shape[-1]
    cw_rows = -(-cw.size // (8 * PACK_COLS)) * 8
    cw_flat = jnp.concatenate([cw.reshape(-1), jnp.zeros((cw_rows * PACK_COLS - cw.size,), F32)]).reshape(cw_rows, PACK_COLS)
    cw_all = _all_gather(cw_flat).reshape(N_DEV, -1)[:, :cw.size].reshape(N_DEV, DEPTH, CONV_K, cw_cols)
    conv_w = jnp.transpose(cw_all, (1, 2, 0, 3)).reshape(DEPTH, CONV_K, CONV_DIM)
    small = {name: args[name] for name, _ in SMALL}

    loss, dx, d_big, d_small, d_conv_w = _local_step(x, positions, target, big, small, conv_w, packed_last=packed[-1])
    loss = lax.psum(loss, MESH_AXES)

    out = {"loss": loss, "grad_x": dx[None]}

    grads = {name: [] for name, _, _ in BIG}
    for l in range(DEPTH):
        received = d_big[l] if l == DEPTH - 1 else _exchange_blocks(_pack_full_grads(d_big[l]))
        summed = _sum_blocks(received)
        for name, g in _unpack_shards(summed).items():
            grads[name].append(g)
    flat = lambda t: t.reshape(-1, t.shape[-1])
    for name, _, _ in BIG:
        g = jnp.stack(grads[name])
        w = args[name]
        delta, m2, v2 = _adam(flat(w), flat(g), flat(args["m_" + name]), flat(args["v_" + name]))
        out["grad_" + name] = g
        out["delta_" + name] = delta.reshape(w.shape)
        out["new_m_" + name] = m2.reshape(w.shape)
        out["new_v_" + name] = v2.reshape(w.shape)

    total = _sum_blocks(_all_gather(_pack_small(d_small, d_conv_w)))
    g_conv_w = _unpack_small(total)[1]
    zeros_cw = jnp.zeros((DEPTH, CONV_K, CONV_DIM), F32)
    delta, m2, v2 = _adam(_pack_small(small, zeros_cw), total,
                          _pack_small({name: args["m_" + name] for name, _ in SMALL}, zeros_cw),
                          _pack_small({name: args["v_" + name] for name, _ in SMALL}, zeros_cw))
    for kind, packed in (("grad_", total), ("delta_", delta), ("new_m_", m2), ("new_v_", v2)):
        for name, val in _unpack_small(packed)[0].items():
            out[kind + name] = val
    g_cw = lax.dynamic_slice_in_dim(g_conv_w, dev * cw_cols, cw_cols, axis=2)
    delta, m2, v2 = _adam(flat(cw), flat(g_cw), flat(args["m_conv_w"]), flat(args["v_conv_w"]))
    out["grad_conv_w"] = g_cw
    out["delta_conv_w"] = delta.reshape(cw.shape)
    out["new_m_conv_w"] = m2.reshape(cw.shape)
    out["new_v_conv_w"] = v2.reshape(cw.shape)
    return out


WEIGHTS = ["ln_ffn1", "ffn1_w13", "ffn1_w2", "ln_mix", "w_in", "conv_w", "conv_b", "dt_bias", "a_log", "d_skip", "ssd_norm",
           "w_ssd_out", "q_lora_norm", "w_uq", "kv_lora_norm", "w_ukv", "q_norm", "k_norm", "w_mla_out", "w_o", "ln_ffn2",
           "ffn2_w13", "ffn2_w2"]
ARG_NAMES = (["x", "positions"] + WEIGHTS + ["loss_target"] + ["m_" + n for n in WEIGHTS] + ["v_" + n for n in WEIGHTS])


def kernel(x, positions, ln_ffn1, ffn1_w13, ffn1_w2, ln_mix, w_in, conv_w, conv_b, dt_bias, a_log, d_skip, ssd_norm, w_ssd_out, q_lora_norm, w_uq, kv_lora_norm, w_ukv, q_norm, k_norm, w_mla_out, w_o, ln_ffn2, ffn2_w13, ffn2_w2, loss_target, m_ln_ffn1, m_ffn1_w13, m_ffn1_w2, m_ln_mix, m_w_in, m_conv_w, m_conv_b, m_dt_bias, m_a_log, m_d_skip, m_ssd_norm, m_w_ssd_out, m_q_lora_norm, m_w_uq, m_kv_lora_norm, m_w_ukv, m_q_norm, m_k_norm, m_w_mla_out, m_w_o, m_ln_ffn2, m_ffn2_w13, m_ffn2_w2, v_ln_ffn1, v_ffn1_w13, v_ffn1_w2, v_ln_mix, v_w_in, v_conv_w, v_conv_b, v_dt_bias, v_a_log, v_d_skip, v_ssd_norm, v_w_ssd_out, v_q_lora_norm, v_w_uq, v_kv_lora_norm, v_w_ukv, v_q_norm, v_k_norm, v_w_mla_out, v_w_o, v_ln_ffn2, v_ffn2_w13, v_ffn2_w2):
    vals = (x, positions, ln_ffn1, ffn1_w13, ffn1_w2, ln_mix, w_in, conv_w, conv_b, dt_bias, a_log, d_skip, ssd_norm, w_ssd_out, q_lora_norm, w_uq, kv_lora_norm, w_ukv, q_norm, k_norm, w_mla_out, w_o, ln_ffn2, ffn2_w13, ffn2_w2, loss_target, m_ln_ffn1, m_ffn1_w13, m_ffn1_w2, m_ln_mix, m_w_in, m_conv_w, m_conv_b, m_dt_bias, m_a_log, m_d_skip, m_ssd_norm, m_w_ssd_out, m_q_lora_norm, m_w_uq, m_kv_lora_norm, m_w_ukv, m_q_norm, m_k_norm, m_w_mla_out, m_w_o, m_ln_ffn2, m_ffn2_w13, m_ffn2_w2, v_ln_ffn1, v_ffn1_w13, v_ffn1_w2, v_ln_mix, v_w_in, v_conv_w, v_conv_b, v_dt_bias, v_a_log, v_d_skip, v_ssd_norm, v_w_ssd_out, v_q_lora_norm, v_w_uq, v_kv_lora_norm, v_w_ukv, v_q_norm, v_k_norm, v_w_mla_out, v_w_o, v_ln_ffn2, v_ffn2_w13, v_ffn2_w2)
    out = _step(dict(zip(ARG_NAMES, vals)))
    order = ["loss", "grad_x"] + [k + n for k in ("grad_", "delta_", "new_m_", "new_v_") for n in WEIGHTS]
    return tuple(out[n] for n in order)
```

```python
import jax
import jax.numpy as jnp
from jax import lax
from jax.experimental import pallas as pl
from jax.experimental.pallas import tpu as pltpu

F32 = jnp.float32
BF16 = jnp.bfloat16

D_MODEL = 1024
D_FF = 2816
DEPTH = 2
SSD_DI = 2048
SSD_P = 64
SSD_H = 32
SSD_G = 4
SSD_HPG = 8
SSD_N = 128
SSD_L = 128
CONV_K = 4
CONV_DIM = 3072
MLA_H = 8
Q_LORA = 512
KV_LORA = 256
NOPE = 128
ROPE = 64
VDIM = 128
QK = 192
ROPE_THETA = 10000.0
EPS = 1e-6
IN_SPLIT = (SSD_DI, CONV_DIM, SSD_H, Q_LORA, KV_LORA, ROPE, 2 * D_MODEL)
D_IN = sum(IN_SPLIT)
N_DEV = 8
LANE = 128
PACK_COLS = 1024

PROJ_Z = 0
PROJ_GATES = PROJ_Z + SSD_DI
PROJ_XBC = PROJ_GATES + 2 * D_MODEL
PROJ_CQ = PROJ_XBC + CONV_DIM
PROJ_CKV = PROJ_CQ + Q_LORA
PROJ_LAST = PROJ_CKV + KV_LORA
D_IN_PAD = PROJ_LAST + LANE

ADAM_LR = 0.001
ADAM_B1 = 0.9
ADAM_B2 = 0.999
ADAM_EPS = 1e-08
ADAM_WD = 0.01
ADAM_STEP = 10

VMEM_LIMIT = 48 * 1024 * 1024
ROW_IO_BUDGET = 8 * 1024 * 1024
NEG = -1e30

MESH_AXES = ("x", "y", "c")


def _cparams(*sem):
    return pltpu.CompilerParams(dimension_semantics=sem, vmem_limit_bytes=VMEM_LIMIT)


def _pick_tile(n, target, align):
    if n <= target:
        return n
    best = None
    for t in range(align, target + 1, align):
        if n % t == 0:
            best = t
    assert best is not None, (n, target, align)
    return best


def _acc_store(ref, val, first):
    @pl.when(first)
    def _():
        ref[...] = val

    @pl.when(jnp.logical_not(first))
    def _():
        ref[...] += val


def _win(arr, start, width):
    assert start % width == 0, (start, width)
    return (arr, start, width)


def _operand(entry):
    if isinstance(entry, tuple):
        arr, start, width = entry
        return arr, width, start // width
    return entry, entry.shape[1], 0


def _row_tile(rows, bytes_per_row):
    if rows <= 16:
        return rows
    t = 1024
    while t > 16 and (t * bytes_per_row > ROW_IO_BUDGET or rows % t):
        t //= 2
    assert rows % t == 0, (rows, t)
    return t


def _rowwise_call(fn, tiled, params, outs, accs, name):
    ops = [_operand(e) for e in tiled]
    rows = ops[0][0].shape[0]
    per_row = sum(w * a.dtype.itemsize for a, w, _ in ops) + sum(c * jnp.dtype(d).itemsize for c, d in outs)
    tile = _row_tile(rows, per_row)
    n_in = len(tiled) + len(params)
    n_o = len(outs)

    def body(*refs):
        vals = [r[...] for r in refs[:n_in]]
        t_out, a_out = fn(*vals)
        for r, v in zip(refs[n_in:n_in + n_o], t_out):
            r[...] = v.astype(r.dtype)
        first = pl.program_id(0) == 0
        for r, v in zip(refs[n_in + n_o:], a_out):
            _acc_store(r, v.astype(F32), first)

    def tiled_spec(width, blk):
        return pl.BlockSpec((tile, width), lambda i: (i, blk))

    in_specs = [tiled_spec(w, blk) for _, w, blk in ops]
    in_specs += [pl.BlockSpec(p.shape, lambda i: (0, 0)) for p in params]
    out_specs = [tiled_spec(c, 0) for c, _ in outs]
    out_specs += [pl.BlockSpec(s, lambda i: (0, 0)) for s in accs]
    out_shape = [jax.ShapeDtypeStruct((rows, c), d) for c, d in outs]
    out_shape += [jax.ShapeDtypeStruct(s, F32) for s in accs]
    return pl.pallas_call(
        body, grid=(rows // tile,), in_specs=in_specs, out_specs=out_specs, out_shape=out_shape,
        compiler_params=_cparams("arbitrary"), name=name,
    )(*[a for a, _, _ in ops], *params)


def _to_f32(vals):
    return [v.astype(F32) for v in vals]


def _row_fwd(f, tiled, params, out_dtypes, name):
    ops = [_operand(e) for e in tiled]
    rows = ops[0][0].shape[0]
    shapes = jax.eval_shape(f, *[jax.ShapeDtypeStruct((rows, w), F32) for _, w, _ in ops],
                            *[jax.ShapeDtypeStruct(p.shape, F32) for p in params])
    outs = [(s.shape[1], d) for s, d in zip(shapes, out_dtypes)]
    return _rowwise_call(lambda *v: (f(*_to_f32(v)), ()), tiled, params, outs, [], name)


def _row_bwd(f, tiled, params, gs, d_dtypes, name, bwd=None, add=None):
    n_t, n_g = len(tiled), len(gs)
    adds = sorted((add or {}).items())
    n_a = len(adds)

    def fn(*vals):
        vals = _to_f32(vals)
        prim = vals[:n_t] + vals[n_t + n_g + n_a:]
        g = tuple(vals[n_t:n_t + n_g])
        if bwd is not None:
            d_t, d_p = bwd(*prim, *g)
        else:
            _, vjp = jax.vjp(f, *prim)
            cts = vjp(g)
            d_t, d_p = cts[:n_t], cts[n_t:]
        d_t = list(d_t)
        for (idx, _), extra in zip(adds, vals[n_t + n_g:n_t + n_g + n_a]):
            d_t[idx] = d_t[idx] + extra
        return tuple(d_t), tuple(d_p)

    outs = [(_operand(e)[1], d) for e, d in zip(tiled, d_dtypes)]
    accs = [p.shape for p in params]
    res = _rowwise_call(fn, list(tiled) + list(gs) + [a for _, a in adds], params, outs, accs, name)
    return res[:n_t], res[n_t:]


def _f_rmsnorm(x, g):
    return (x * lax.rsqrt(jnp.mean(x * x, axis=-1, keepdims=True) + EPS) * g,)


def _f_gated_norm(ys, xs, z, dsk, g):
    t = (ys + xs * dsk) * (z * jax.nn.sigmoid(z))
    return (t * lax.rsqrt(jnp.mean(t * t, axis=-1, keepdims=True) + EPS) * g,)


def _f_merge(gates, ys, ym):
    s = jax.nn.sigmoid(gates)
    return (s[:, :D_MODEL] * ys + s[:, D_MODEL:] * ym,)


def _b_merge(gates, ys, ym, d):
    s = jax.nn.sigmoid(gates)
    s1, s2 = s[:, :D_MODEL], s[:, D_MODEL:]
    d_gates = jnp.concatenate([d * ys * s1 * (1.0 - s1), d * ym * s2 * (1.0 - s2)], axis=1)
    return (d_gates, d * s1, d * s2), ()


def _loss_and_grad(y, target):
    def fn(yv, tv):
        d = yv - tv
        return (d * (1.0 / D_MODEL),), (jnp.sum(d * d, axis=0, keepdims=True) * (0.5 / D_MODEL),)

    dy, part = _rowwise_call(fn, [y, target], [], [(D_MODEL, F32)], [(1, D_MODEL)], "loss")
    return jnp.sum(part), dy


def _adam(w, g, m, v):
    def fn(wv, gv, mv, vv):
        m2 = ADAM_B1 * mv + (1.0 - ADAM_B1) * gv
        v2 = ADAM_B2 * vv + (1.0 - ADAM_B2) * (gv * gv)
        m_hat = m2 / (1.0 - ADAM_B1 ** ADAM_STEP)
        v_hat = v2 / (1.0 - ADAM_B2 ** ADAM_STEP)
        delta = -ADAM_LR * (m_hat / (jnp.sqrt(v_hat) + ADAM_EPS) + ADAM_WD * wv)
        return (delta, m2, v2), ()

    c = w.shape[1]
    return _rowwise_call(fn, [w, g, m, v], [], [(c, F32)] * 3, [], "adamw")


def _sum_blocks(blocks):
    _, rows, c = blocks.shape
    tile = _row_tile(rows, N_DEV * c * blocks.dtype.itemsize + c * 4)

    def body(b_ref, o_ref):
        acc = b_ref[0].astype(F32)
        for i in range(1, N_DEV):
            acc = acc + b_ref[i].astype(F32)
        o_ref[...] = acc

    return pl.pallas_call(
        body, grid=(rows // tile,), in_specs=[pl.BlockSpec((N_DEV, tile, c), lambda i: (0, i, 0))],
        out_specs=pl.BlockSpec((tile, c), lambda i: (i, 0)), out_shape=jax.ShapeDtypeStruct((rows, c), F32),
        compiler_params=_cparams("arbitrary"), name="sum_blocks",
    )(blocks)


def _mm(a, b, ta=False, tb=False, out_dtype=F32, alpha=1.0, res=None, b_rows=None, norm_bwd=None):
    r_dim, p_dim = a.shape if ta else a.shape[::-1]
    b_row0, b_nrows = (0, b.shape[0]) if b_rows is None else b_rows
    r2, q_dim = (b.shape[1], b_nrows) if tb else (b_nrows, b.shape[1])
    assert r_dim == r2, (a.shape, b.shape, ta, tb)
    tp = _pick_tile(p_dim, 512, LANE)
    if tp < 512 < p_dim:
        tp = _pick_tile(p_dim, 1536, LANE)
    tq = _pick_tile(q_dim, 1536, LANE)
    tr = _pick_tile(r_dim, 1536, LANE)
    nr = r_dim // tr
    dims = (((0 if ta else 1,), (1 if tb else 0,)), ((), ()))
    has_res = res is not None
    n_nb = 0 if norm_bwd is None else 3
    assert norm_bwd is None or tq == q_dim

    def body(*refs):
        a_ref, b_ref = refs[:2]
        res_ref = refs[2] if has_res else None
        n_in = 2 + has_res + n_nb
        o_ref = refs[n_in]

        def finish(val):
            if alpha != 1.0:
                val = val * alpha
            if has_res:
                val = val + res_ref[...].astype(F32)
            if norm_bwd is not None:
                x_ref, g_ref, add_ref = refs[2 + has_res:n_in]
                x = x_ref[...]
                r = lax.rsqrt(jnp.mean(x * x, axis=-1, keepdims=True) + EPS)
                gy = val * g_ref[...]
                dot = jnp.sum(gy * x, axis=-1, keepdims=True)
                _acc_store(refs[n_in + 1], jnp.sum(val * x * r, axis=0, keepdims=True), pl.program_id(1) == 0)
                val = gy * r - x * (dot * (r * r * r) * (1.0 / q_dim)) + add_ref[...]
            o_ref[...] = val.astype(o_ref.dtype)

        part = lax.dot_general(a_ref[...].astype(BF16), b_ref[...].astype(BF16), dims, preferred_element_type=F32)
        if nr == 1:
            finish(part)
        else:
            acc_ref = refs[-1]
            k = pl.program_id(2)
            _acc_store(acc_ref, part, k == 0)

            @pl.when(k == nr - 1)
            def _():
                finish(acc_ref[...])

    a_spec = pl.BlockSpec((tr, tp), lambda j, i, k: (k, i)) if ta else pl.BlockSpec((tp, tr), lambda j, i, k: (i, k))
    assert b_row0 % (tq if tb else tr) == 0
    b0 = b_row0 // (tq if tb else tr)
    b_spec = pl.BlockSpec((tq, tr), lambda j, i, k: (j + b0, k)) if tb else pl.BlockSpec((tr, tq), lambda j, i, k: (k + b0, j))
    o_spec = pl.BlockSpec((tp, tq), lambda j, i, k: (i, j))
    row_spec = pl.BlockSpec((1, tq), lambda j, i, k: (0, 0))
    in_specs = [a_spec, b_spec] + ([o_spec] if has_res else []) + ([o_spec, row_spec, o_spec] if n_nb else [])
    out = pl.pallas_call(
        body, grid=(q_dim // tq, p_dim // tp, nr), in_specs=in_specs,
        out_specs=[o_spec] + ([row_spec] if n_nb else []),
        out_shape=[jax.ShapeDtypeStruct((p_dim, q_dim), out_dtype)] + ([jax.ShapeDtypeStruct((1, q_dim), F32)] if n_nb else []),
        scratch_shapes=[pltpu.VMEM((tp, tq), F32)] if nr > 1 else [],
        compiler_params=_cparams("arbitrary", "arbitrary", "arbitrary"),
        name=f"mm_{'t' if ta else 'n'}{'t' if tb else 'n'}_{p_dim}x{r_dim}x{q_dim}" + ("_norm_bwd" if n_nb else ""),
    )(*([a, b] + ([res] if has_res else []) + (list(norm_bwd) if n_nb else [])))
    return out if n_nb else out[0]


MERGE_TP = 256


def _merge_out_call(proj, y_ssd, y_mla, w_o, h):
    s = h.shape[0]
    tp = min(MERGE_TP, s)

    def body(g_ref, ys_ref, ym_ref, w_ref, h_ref, o_ref, mg_ref):
        mg = _f_merge(g_ref[...], ys_ref[...], ym_ref[...])[0].astype(BF16)
        mg_ref[...] = mg
        o_ref[...] = h_ref[...] + jnp.dot(mg, w_ref[...], preferred_element_type=F32)

    rows = pl.BlockSpec((tp, D_MODEL), lambda i: (i, 0))
    return pl.pallas_call(
        body, grid=(s // tp,),
        in_specs=[pl.BlockSpec((tp, 2 * D_MODEL), lambda i: (i, PROJ_GATES // (2 * D_MODEL))), rows, rows,
                  pl.BlockSpec((D_MODEL, D_MODEL), lambda i: (0, 0)), rows],
        out_specs=[rows, rows],
        out_shape=[jax.ShapeDtypeStruct((s, D_MODEL), F32), jax.ShapeDtypeStruct((s, D_MODEL), BF16)],
        compiler_params=_cparams("arbitrary"), name="merge_out",
    )(proj, y_ssd, y_mla, w_o, h)


def _merge_out_bwd_call(dh, w_o, proj, y_ssd, y_mla):
    s = dh.shape[0]
    tp = min(MERGE_TP, s)

    def body(dh_ref, w_ref, g_ref, ys_ref, ym_ref, dg_ref, dys_ref, dym_ref):
        d_mg = _nt(dh_ref[...].astype(BF16), w_ref[...])
        (d_g, d_ys, d_ym), _ = _b_merge(g_ref[...], ys_ref[...], ym_ref[...], d_mg)
        dg_ref[...] = d_g.astype(BF16)
        dys_ref[...] = d_ys.astype(BF16)
        dym_ref[...] = d_ym.astype(BF16)

    rows = pl.BlockSpec((tp, D_MODEL), lambda i: (i, 0))
    wide = pl.BlockSpec((tp, 2 * D_MODEL), lambda i: (i, 0))
    return pl.pallas_call(
        body, grid=(s // tp,),
        in_specs=[rows, pl.BlockSpec((D_MODEL, D_MODEL), lambda i: (0, 0)),
                  pl.BlockSpec((tp, 2 * D_MODEL), lambda i: (i, PROJ_GATES // (2 * D_MODEL))), rows, rows],
        out_specs=[wide, rows, rows],
        out_shape=[jax.ShapeDtypeStruct((s, 2 * D_MODEL), BF16), jax.ShapeDtypeStruct((s, D_MODEL), BF16),
                   jax.ShapeDtypeStruct((s, D_MODEL), BF16)],
        compiler_params=_cparams("arbitrary"), name="merge_out_bwd",
    )(dh, w_o, proj, y_ssd, y_mla)


FFN_TP = 512
FFN_TQ = 1408


def _ffn_up_call(n, w13_t):
    s, d = n.shape
    tp = min(FFN_TP, s)
    up0 = D_FF // FFN_TQ

    def body(n_ref, wg_ref, wu_ref, act_ref, gate_ref, up_ref):
        a = n_ref[...]
        g = _nt(a, wg_ref[...])
        u = _nt(a, wu_ref[...])
        act_ref[...] = (g * jax.nn.sigmoid(g) * u).astype(BF16)
        gate_ref[...] = g.astype(BF16)
        up_ref[...] = u.astype(BF16)

    o_spec = pl.BlockSpec((tp, FFN_TQ), lambda j, i: (i, j))
    return pl.pallas_call(
        body, grid=(D_FF // FFN_TQ, s // tp),
        in_specs=[pl.BlockSpec((tp, d), lambda j, i: (i, 0)), pl.BlockSpec((FFN_TQ, d), lambda j, i: (j, 0)),
                  pl.BlockSpec((FFN_TQ, d), lambda j, i: (j + up0, 0))],
        out_specs=[o_spec] * 3, out_shape=[jax.ShapeDtypeStruct((s, D_FF), BF16)] * 3,
        compiler_params=_cparams("arbitrary", "arbitrary"), name="ffn_up_swiglu",
    )(n, w13_t, w13_t)


def _ffn_down_bwd_call(dh, w2, gate, up):
    s, d = dh.shape
    tp = min(FFN_TP, s)

    def body(dh_ref, w2_ref, gate_ref, up_ref, dg_ref, du_ref):
        d_act = 0.5 * _nt(dh_ref[...].astype(BF16), w2_ref[...])
        g, u = gate_ref[...].astype(F32), up_ref[...].astype(F32)
        sg = jax.nn.sigmoid(g)
        dg_ref[...] = (d_act * u * sg * (1.0 + g * (1.0 - sg))).astype(BF16)
        du_ref[...] = (d_act * g * sg).astype(BF16)

    o_spec = pl.BlockSpec((tp, FFN_TQ), lambda j, i: (i, j))
    return pl.pallas_call(
        body, grid=(D_FF // FFN_TQ, s // tp),
        in_specs=[pl.BlockSpec((tp, d), lambda j, i: (i, 0)), pl.BlockSpec((FFN_TQ, d), lambda j, i: (j, 0)), o_spec, o_spec],
        out_specs=[o_spec] * 2, out_shape=[jax.ShapeDtypeStruct((s, D_FF), BF16)] * 2,
        compiler_params=_cparams("arbitrary", "arbitrary"), name="ffn_down_bwd_swiglu",
    )(dh, w2, gate, up)


ATTN_SCALE = QK ** -0.5
LOG2E = 1.4426950408889634
ATTN_C = ATTN_SCALE * LOG2E


ATTN_HEADS = 2


def _attn_tile(s):
    return min(512, s)


def _causal_keep(t, keys_on_rows=False):
    row = lax.broadcasted_iota(jnp.int32, (t, t), 0)
    col = lax.broadcasted_iota(jnp.int32, (t, t), 1)
    return row <= col if keys_on_rows else col <= row


def _nt(a, b):
    return lax.dot_general(a, b, (((1,), (1,)), ((), ())), preferred_element_type=F32)


def _rider_phases(rider, src_ref, out_ref, sems, first, last):
    @pl.when(first)
    def _():
        _peer_copies(src_ref, out_ref, sems, rider["gather"], "start")

    def finish():
        @pl.when(last)
        def _():
            _peer_copies(src_ref, out_ref, sems, rider["gather"], "finish")

    return finish


def _attn_fwd_call(q, k, v, rider=None):
    nh, s, _ = q.shape
    t = _attn_tile(s)
    nb = s // t
    hp = ATTN_HEADS
    n_r = 0 if rider is None else 1

    def body(*refs):
        q_ref, k_ref, v_ref = refs[:3]
        o_ref, lse_ref = refs[3 + n_r:5 + n_r]
        qi = pl.program_id(1)
        finish = None
        if rider is not None:
            h = pl.program_id(0)
            finish = _rider_phases(rider, refs[3], refs[5 + n_r], refs[6 + n_r:],
                                   jnp.logical_and(h == 0, qi == 0), jnp.logical_and(h == nh // hp - 1, qi == nb - 1))
        qs = [q_ref[i] for i in range(hp)]

        def block(kb, carries, diagonal, width=1):
            start = pl.multiple_of(kb * t, t)
            out = []
            for i, (m_prev, l_prev, acc) in enumerate(carries):
                sc = _nt(qs[i], k_ref[i, pl.ds(start, width * t), :])
                if diagonal:
                    sc = jnp.where(_causal_keep(t), sc, NEG)
                m_new = jnp.maximum(m_prev, jnp.max(sc, axis=-1, keepdims=True))
                p = jnp.exp2(sc * ATTN_C - m_new * ATTN_C)
                alpha = jnp.exp2((m_prev - m_new) * ATTN_C)
                l_new = alpha * l_prev + jnp.sum(p, axis=-1, keepdims=True)
                pv = jnp.dot(p.astype(BF16), v_ref[i, pl.ds(start, width * t), :], preferred_element_type=F32)
                out.append((m_new, l_new, alpha * acc + pv))
            return tuple(out)

        init = tuple((jnp.full((t, 1), NEG, F32), jnp.zeros((t, 1), F32), jnp.zeros((t, VDIM), F32)) for _ in range(hp))
        carries = lax.fori_loop(0, qi // 2, lambda j, c: block(2 * j, c, False, width=2), init)
        carries = lax.cond(qi % 2 == 1, lambda c: block(qi - 1, c, False), lambda c: c, carries)
        for i, (m, l, acc) in enumerate(block(qi, carries, True)):
            o_ref[i] = (acc / l).astype(o_ref.dtype)
            lse_ref[i] = m * ATTN_SCALE + jnp.log(l)
        if finish is not None:
            finish()

    qmap = lambda h, i: (h, i, 0)
    whole = lambda h, i: (h, 0, 0)
    return pl.pallas_call(
        body, grid=(nh // hp, nb),
        in_specs=[pl.BlockSpec((hp, t, QK), qmap), pl.BlockSpec((hp, s, QK), whole), pl.BlockSpec((hp, s, VDIM), whole)] + [HBM_SPEC] * n_r,
        out_specs=[pl.BlockSpec((hp, t, VDIM), qmap), pl.BlockSpec((hp, t, 1), qmap)] + [HBM_SPEC] * n_r,
        out_shape=[jax.ShapeDtypeStruct((nh, s, VDIM), BF16), jax.ShapeDtypeStruct((nh, s, 1), F32)] + ([rider["out"]] if n_r else []),
        scratch_shapes=_comm_scratch() if n_r else [],
        compiler_params=_cparams("arbitrary", "arbitrary"), name="attn_fwd_gather" if n_r else "attn_fwd",
    )(*([q, k, v] + ([rider["src"]] if n_r else [])))


def _attn_delta_call(o, do):
    nh, s, d = o.shape

    def fn(ov, dv):
        return (jnp.sum(ov.astype(F32) * dv.astype(F32), axis=-1, keepdims=True),), ()

    return _rowwise_call(fn, [o.reshape(nh * s, d), do.reshape(nh * s, d)], [], [(1, F32)], [], "attn_delta")[0]


def _attn_bwd_call(q, k, v, do, lse_t, delta_t, rider=None):
    nh, s, _ = q.shape
    t = _attn_tile(s)
    nb = s // t
    hp = ATTN_HEADS
    n_r = 0 if rider is None else 1

    def body(*refs):
        q_ref, k_ref, v_ref, do_ref, lse_ref, delta_ref = refs[:6]
        dq_ref, dk_ref, dv_ref = refs[6 + n_r:9 + n_r]
        dk_sc, dv_sc = refs[9 + 2 * n_r:11 + 2 * n_r]
        kj = pl.program_id(1)
        finish = None
        if rider is not None:
            h = pl.program_id(0)
            finish = _rider_phases(rider, refs[6], refs[9 + n_r], refs[11 + 2 * n_r:],
                                   jnp.logical_and(h == 0, kj == 0), jnp.logical_and(h == nh // hp - 1, kj == nb - 1))

        @pl.when(kj == 0)
        def _():
            dq_ref[...] = jnp.zeros_like(dq_ref)

        dk_sc[...] = jnp.zeros_like(dk_sc)
        dv_sc[...] = jnp.zeros_like(dv_sc)
        kblks = [k_ref[i] for i in range(hp)]
        vblks = [v_ref[i] for i in range(hp)]

        def block(qb, diagonal):
            start = pl.multiple_of(qb * t, t)
            for i in range(hp):
                qblk = q_ref[i, pl.ds(start, t), :]
                doblk = do_ref[i, pl.ds(start, t), :]
                sc = _nt(kblks[i], qblk)
                if diagonal:
                    sc = jnp.where(_causal_keep(t, keys_on_rows=True), sc, NEG)
                p = jnp.exp2(sc * ATTN_C - lse_ref[i, :, pl.ds(start, t)] * LOG2E)
                dv_sc[i] += jnp.dot(p.astype(BF16), doblk, preferred_element_type=F32)
                dp = _nt(vblks[i], doblk)
                ds = (p * (dp - delta_ref[i, :, pl.ds(start, t)])).astype(BF16)
                dk_sc[i] += jnp.dot(ds, qblk, preferred_element_type=F32)
                dq_ref[i, pl.ds(start, t), :] += lax.dot_general(ds, kblks[i], (((0,), (0,)), ((), ())), preferred_element_type=F32)

        block(kj, True)

        def rest(qb, carry):
            block(qb, False)
            return carry

        lax.fori_loop(kj + 1, nb, rest, 0)
        dk_ref[...] = (dk_sc[...] * ATTN_SCALE).astype(dk_ref.dtype)
        dv_ref[...] = dv_sc[...].astype(dv_ref.dtype)

        @pl.when(kj == nb - 1)
        def _():
            dq_ref[...] = dq_ref[...] * ATTN_SCALE

        if finish is not None:
            finish()

    kmap = lambda h, j: (h, j, 0)
    whole = lambda h, j: (h, 0, 0)
    once = pl.Buffered(buffer_count=1)
    return pl.pallas_call(
        body, grid=(nh // hp, nb),
        in_specs=[pl.BlockSpec((hp, s, QK), whole, pipeline_mode=once), pl.BlockSpec((hp, t, QK), kmap), pl.BlockSpec((hp, t, VDIM), kmap),
                  pl.BlockSpec((hp, s, VDIM), whole, pipeline_mode=once), pl.BlockSpec((hp, 1, s), whole, pipeline_mode=once),
                  pl.BlockSpec((hp, 1, s), whole, pipeline_mode=once)] + [HBM_SPEC] * n_r,
        out_specs=[pl.BlockSpec((hp, s, QK), whole, pipeline_mode=once), pl.BlockSpec((hp, t, QK), kmap),
                   pl.BlockSpec((hp, t, VDIM), kmap)] + [HBM_SPEC] * n_r,
        out_shape=[jax.ShapeDtypeStruct((nh, s, QK), F32), jax.ShapeDtypeStruct((nh, s, QK), F32),
                   jax.ShapeDtypeStruct((nh, s, VDIM), F32)] + ([rider["out"]] if n_r else []),
        scratch_shapes=[pltpu.VMEM((hp, t, QK), F32), pltpu.VMEM((hp, t, VDIM), F32)] + (_comm_scratch() if n_r else []),
        compiler_params=_cparams("arbitrary", "arbitrary"), name="attn_bwd_exchange" if n_r else "attn_bwd",
    )(*([q, k, v, do, lse_t, delta_t] + ([rider["src"]] if n_r else [])))


HEAD_COLS = NOPE + VDIM
HEADS_TILE = 256


def _swap_rope_halves(t, lane):
    half = ROPE // 2
    return jnp.where(lane < half, pltpu.roll(t, LANE - half, 1), pltpu.roll(t, half, 1))


def _head_fwd(n, p, gain, cs, sn, lane):
    r = lax.rsqrt((jnp.sum(n * n, axis=-1, keepdims=True) + jnp.sum(p * p, axis=-1, keepdims=True)) * (1.0 / QK) + EPS)
    yp = p * r * gain[:, NOPE:]
    return n * r * gain[:, :NOPE], yp * cs + _swap_rope_halves(yp, lane) * sn


def _head_bwd(n, p, gain, cs, sn, lane, dzn, dzp):
    r = lax.rsqrt((jnp.sum(n * n, axis=-1, keepdims=True) + jnp.sum(p * p, axis=-1, keepdims=True)) * (1.0 / QK) + EPS)
    dyp = dzp * cs + _swap_rope_halves(dzp * sn, lane)
    gyn, gyp = dzn * gain[:, :NOPE], dyp * gain[:, NOPE:]
    dot = jnp.sum(gyn * n, axis=-1, keepdims=True) + jnp.sum(gyp * p, axis=-1, keepdims=True)
    coef = dot * (r * r * r) * (1.0 / QK)
    d_gn = jnp.sum(dzn * n * r, axis=0, keepdims=True)
    d_gp = jnp.sum(dyp * p * r, axis=0, keepdims=True)
    return gyn * r - n * coef, gyp * r - p * coef, d_gn, d_gp


def _heads_fwd_call(q, kv, proj, cs, sn, q_gain, k_gain):
    s = q.shape[0]
    t = min(HEADS_TILE, s)

    def body(q_ref, kv_ref, last_ref, cs_ref, sn_ref, qg_ref, kg_ref, qh_ref, kh_ref, vh_ref):
        lane = lax.broadcasted_iota(jnp.int32, (t, LANE), 1)
        cs_, sn_ = cs_ref[...], sn_ref[...]
        kp = jnp.where(lane < ROPE, last_ref[...], 0.0)
        for h in range(MLA_H):
            c0 = h * HEAD_COLS
            zn, zp = _head_fwd(q_ref[:, c0:c0 + NOPE], q_ref[:, c0 + NOPE:c0 + HEAD_COLS], qg_ref[...], cs_, sn_, lane)
            qh_ref[h, :, :NOPE] = zn.astype(BF16)
            qh_ref[h, :, NOPE:] = zp[:, :ROPE].astype(BF16)
            zn, zp = _head_fwd(kv_ref[:, c0:c0 + NOPE], kp, kg_ref[...], cs_, sn_, lane)
            kh_ref[h, :, :NOPE] = zn.astype(BF16)
            kh_ref[h, :, NOPE:] = zp[:, :ROPE].astype(BF16)
            vh_ref[h] = kv_ref[:, c0 + NOPE:c0 + HEAD_COLS].astype(BF16)

    rows = lambda i: (i, 0)
    whole = lambda i: (0, 0)
    heads = lambda i: (0, i, 0)
    wide = MLA_H * HEAD_COLS
    return pl.pallas_call(
        body, grid=(s // t,),
        in_specs=[pl.BlockSpec((t, wide), rows), pl.BlockSpec((t, wide), rows),
                  pl.BlockSpec((t, LANE), lambda i: (i, PROJ_LAST // LANE)),
                  pl.BlockSpec((t, LANE), rows), pl.BlockSpec((t, LANE), rows),
                  pl.BlockSpec((1, HEAD_COLS), whole), pl.BlockSpec((1, HEAD_COLS), whole)],
        out_specs=[pl.BlockSpec((MLA_H, t, QK), heads), pl.BlockSpec((MLA_H, t, QK), heads), pl.BlockSpec((MLA_H, t, VDIM), heads)],
        out_shape=[jax.ShapeDtypeStruct((MLA_H, s, QK), BF16), jax.ShapeDtypeStruct((MLA_H, s, QK), BF16),
                   jax.ShapeDtypeStruct((MLA_H, s, VDIM), BF16)],
        compiler_params=_cparams("arbitrary"), name="mla_heads_fwd",
    )(q, kv, proj, cs, sn, q_gain, k_gain)


def _heads_bwd_call(q, kv, proj, cs, sn, q_gain, k_gain, dqh, dkh, dvh):
    s = q.shape[0]
    t = min(HEADS_TILE, s)

    def body(q_ref, kv_ref, last_ref, cs_ref, sn_ref, qg_ref, kg_ref, dqh_ref, dkh_ref, dvh_ref,
             dq_ref, dkv_ref, dkr_ref, dqg_ref, dkg_ref):
        lane = lax.broadcasted_iota(jnp.int32, (t, LANE), 1)
        cs_, sn_ = cs_ref[...], sn_ref[...]
        kp = jnp.where(lane < ROPE, last_ref[...], 0.0)
        no_lanes = jnp.zeros((t, LANE - ROPE), F32)
        d_kp = jnp.zeros((t, LANE), F32)
        d_qg = [jnp.zeros((1, NOPE), F32), jnp.zeros((1, LANE), F32)]
        d_kg = [jnp.zeros((1, NOPE), F32), jnp.zeros((1, LANE), F32)]
        for h in range(MLA_H):
            c0 = h * HEAD_COLS
            dz = dqh_ref[h]
            dzp = jnp.concatenate([dz[:, NOPE:], no_lanes], axis=1)
            d_n, d_p, g_n, g_p = _head_bwd(q_ref[:, c0:c0 + NOPE], q_ref[:, c0 + NOPE:c0 + HEAD_COLS], qg_ref[...],
                                           cs_, sn_, lane, dz[:, :NOPE], dzp)
            dq_ref[:, c0:c0 + NOPE] = d_n.astype(dq_ref.dtype)
            dq_ref[:, c0 + NOPE:c0 + HEAD_COLS] = d_p.astype(dq_ref.dtype)
            d_qg = [d_qg[0] + g_n, d_qg[1] + g_p]
            dz = dkh_ref[h]
            dzp = jnp.concatenate([dz[:, NOPE:], no_lanes], axis=1)
            d_n, d_p, g_n, g_p = _head_bwd(kv_ref[:, c0:c0 + NOPE], kp, kg_ref[...], cs_, sn_, lane, dz[:, :NOPE], dzp)
            dkv_ref[:, c0:c0 + NOPE] = d_n.astype(dkv_ref.dtype)
            dkv_ref[:, c0 + NOPE:c0 + HEAD_COLS] = dvh_ref[h].astype(dkv_ref.dtype)
            d_kp = d_kp + d_p
            d_kg = [d_kg[0] + g_n, d_kg[1] + g_p]
        dkr_ref[...] = d_kp
        first = pl.program_id(0) == 0
        _acc_store(dqg_ref.at[:, pl.ds(0, NOPE)], d_qg[0], first)
        _acc_store(dqg_ref.at[:, pl.ds(NOPE, LANE)], d_qg[1], first)
        _acc_store(dkg_ref.at[:, pl.ds(0, NOPE)], d_kg[0], first)
        _acc_store(dkg_ref.at[:, pl.ds(NOPE, LANE)], d_kg[1], first)

    rows = lambda i: (i, 0)
    whole = lambda i: (0, 0)
    heads = lambda i: (0, i, 0)
    wide = MLA_H * HEAD_COLS
    return pl.pallas_call(
        body, grid=(s // t,),
        in_specs=[pl.BlockSpec((t, wide), rows), pl.BlockSpec((t, wide), rows),
                  pl.BlockSpec((t, LANE), lambda i: (i, PROJ_LAST // LANE)),
                  pl.BlockSpec((t, LANE), rows), pl.BlockSpec((t, LANE), rows),
                  pl.BlockSpec((1, HEAD_COLS), whole), pl.BlockSpec((1, HEAD_COLS), whole),
                  pl.BlockSpec((MLA_H, t, QK), heads), pl.BlockSpec((MLA_H, t, QK), heads), pl.BlockSpec((MLA_H, t, VDIM), heads)],
        out_specs=[pl.BlockSpec((t, wide), rows), pl.BlockSpec((t, wide), rows), pl.BlockSpec((t, LANE), rows),
                   pl.BlockSpec((1, HEAD_COLS), whole), pl.BlockSpec((1, HEAD_COLS), whole)],
        out_shape=[jax.ShapeDtypeStruct((s, wide), BF16), jax.ShapeDtypeStruct((s, wide), BF16), jax.ShapeDtypeStruct((s, LANE), F32),
                   jax.ShapeDtypeStruct((1, HEAD_COLS), F32), jax.ShapeDtypeStruct((1, HEAD_COLS), F32)],
        compiler_params=_cparams("arbitrary"), name="mla_heads_bwd",
    )(q, kv, proj, cs, sn, q_gain, k_gain, dqh, dkh, dvh)


CONV_TC = 512
HALO = 8


def _conv_tiles(s):
    return min(512, s)


def _conv_fwd_call(x, col0, w, b):
    s = x.shape[0]
    ts = _conv_tiles(s)
    hb = ts // HALO
    c0 = col0 // CONV_TC
    assert col0 % CONV_TC == 0

    def body(x_ref, prev_ref, w_ref, b_ref, y_ref, buf):
        si = pl.program_id(1)
        buf[0:HALO, :] = jnp.where(si > 0, prev_ref[...], 0.0)
        buf[HALO:, :] = x_ref[...]
        acc = jnp.broadcast_to(b_ref[...], (ts, CONV_TC))
        for k in range(CONV_K):
            acc = acc + w_ref[k:k + 1, :] * buf[pl.ds(HALO - (CONV_K - 1) + k, ts), :]
        y_ref[...] = acc * jax.nn.sigmoid(acc)

    return pl.pallas_call(
        body, grid=(CONV_DIM // CONV_TC, s // ts),
        in_specs=[pl.BlockSpec((ts, CONV_TC), lambda ci, si: (si, ci + c0)),
                  pl.BlockSpec((HALO, CONV_TC), lambda ci, si: (jnp.maximum(si * hb - 1, 0), ci + c0)),
                  pl.BlockSpec((CONV_K, CONV_TC), lambda ci, si: (0, ci)),
                  pl.BlockSpec((1, CONV_TC), lambda ci, si: (0, ci))],
        out_specs=pl.BlockSpec((ts, CONV_TC), lambda ci, si: (si, ci)),
        out_shape=jax.ShapeDtypeStruct((s, CONV_DIM), F32),
        scratch_shapes=[pltpu.VMEM((ts + HALO, CONV_TC), F32)],
        compiler_params=_cparams("arbitrary", "arbitrary"), name="conv_fwd",
    )(x, x, w, b)


def _conv_bwd_call(x, col0, w, b, dy):
    s = x.shape[0]
    ts = _conv_tiles(s)
    hb = ts // HALO
    ns = s // ts
    last_halo = s // HALO - 1
    c0 = col0 // CONV_TC

    def body(x_ref, prev_ref, next_ref, dy_ref, dyn_ref, w_ref, b_ref, dx_ref, dw_ref, db_ref, xbuf, dbuf):
        si = pl.program_id(1)
        xbuf[0:HALO, :] = jnp.where(si > 0, prev_ref[...], 0.0)
        xbuf[HALO:HALO + ts, :] = x_ref[...]
        xbuf[HALO + ts:, :] = next_ref[...]
        pre = jnp.broadcast_to(b_ref[...], (ts + HALO, CONV_TC))
        for k in range(CONV_K):
            pre = pre + w_ref[k:k + 1, :] * xbuf[pl.ds(HALO - (CONV_K - 1) + k, ts + HALO), :]
        sg = jax.nn.sigmoid(pre)
        dsilu = sg * (1.0 + pre * (1.0 - sg))
        dbuf[0:ts, :] = dy_ref[...] * dsilu[0:ts]
        dbuf[ts:, :] = jnp.where(si < ns - 1, dyn_ref[...] * dsilu[ts:], 0.0)
        dx = jnp.zeros((ts, CONV_TC), F32)
        for k in range(CONV_K):
            dx = dx + w_ref[k:k + 1, :] * dbuf[pl.ds(CONV_K - 1 - k, ts), :]
        dx_ref[...] = dx.astype(dx_ref.dtype)
        dpre = dbuf[0:ts, :]
        first = si == 0
        _acc_store(db_ref, jnp.sum(dpre, axis=0, keepdims=True), first)
        for k in range(CONV_K):
            dw_k = jnp.sum(dpre * xbuf[pl.ds(HALO - (CONV_K - 1) + k, ts), :], axis=0, keepdims=True)
            _acc_store(dw_ref.at[pl.ds(k, 1), :], dw_k, first)

    main = lambda ci, si: (si, ci)
    x_main = lambda ci, si: (si, ci + c0)
    x_prev = lambda ci, si: (jnp.maximum(si * hb - 1, 0), ci + c0)
    x_next = lambda ci, si: (jnp.minimum(si * hb + hb, last_halo), ci + c0)
    return pl.pallas_call(
        body, grid=(CONV_DIM // CONV_TC, ns),
        in_specs=[pl.BlockSpec((ts, CONV_TC), x_main), pl.BlockSpec((HALO, CONV_TC), x_prev), pl.BlockSpec((HALO, CONV_TC), x_next),
                  pl.BlockSpec((ts, CONV_TC), main),
                  pl.BlockSpec((HALO, CONV_TC), lambda ci, si: (jnp.minimum(si * hb + hb, last_halo), ci)),
                  pl.BlockSpec((CONV_K, CONV_TC), lambda ci, si: (0, ci)),
                  pl.BlockSpec((1, CONV_TC), lambda ci, si: (0, ci))],
        out_specs=[pl.BlockSpec((ts, CONV_TC), main),
                   pl.BlockSpec((CONV_K, CONV_TC), lambda ci, si: (0, ci)),
                   pl.BlockSpec((1, CONV_TC), lambda ci, si: (0, ci))],
        out_shape=[jax.ShapeDtypeStruct((s, CONV_DIM), BF16), jax.ShapeDtypeStruct((CONV_K, CONV_DIM), F32),
                   jax.ShapeDtypeStruct((1, CONV_DIM), F32)],
        scratch_shapes=[pltpu.VMEM((ts + 2 * HALO, CONV_TC), F32), pltpu.VMEM((ts + HALO, CONV_TC), F32)],
        compiler_params=_cparams("arbitrary", "arbitrary"), name="conv_bwd",
    )(x, x, x, dy, dy, w, b)


GW = SSD_HPG * SSD_P
B_COL = SSD_DI
C_COL = SSD_DI + SSD_G * SSD_N


def _ones_where(mask):
    return jnp.where(mask, 1.0, 0.0).astype(BF16)


def _split(v, passes):
    parts, rest = [], v
    for i in range(passes):
        part = rest.astype(BF16)
        parts.append(part)
        if i + 1 < passes:
            rest = rest - part.astype(F32)
    return parts


def _dot_sel_r(v, sel, passes=3):
    out = None
    for part in _split(v, passes):
        t = jnp.dot(part, sel, preferred_element_type=F32)
        out = t if out is None else out + t
    return out


def _dot_sel_l(sel, v, passes=3):
    out = None
    for part in _split(v, passes):
        t = jnp.dot(sel, part, preferred_element_type=F32)
        out = t if out is None else out + t
    return out


def _ssd_consts():
    r = lax.broadcasted_iota(jnp.int32, (SSD_L, SSD_L), 0)
    c = lax.broadcasted_iota(jnp.int32, (SSD_L, SSD_L), 1)
    tril = r >= c
    triu = c >= r
    shift = SSD_P.bit_length() - 1
    eh = lax.broadcasted_iota(jnp.int32, (SSD_H, SSD_DI), 0)
    ej = lax.broadcasted_iota(jnp.int32, (SSD_H, SSD_DI), 1)
    expand = _ones_where(lax.shift_right_logical(ej, shift) == eh)
    rj = lax.broadcasted_iota(jnp.int32, (SSD_DI, SSD_H), 0)
    rh = lax.broadcasted_iota(jnp.int32, (SSD_DI, SSD_H), 1)
    reduce_ = _ones_where(lax.shift_right_logical(rj, shift) == rh)
    lane = lax.broadcasted_iota(jnp.int32, (SSD_L, LANE), 1)
    return tril, triu, expand, reduce_, lane < SSD_P


def _ssd_decays(dt, dt_t, a, a_t, tril, triu, expand):
    dta = dt * a
    acum = _dot_sel_l(_ones_where(tril), dta)
    acum_t = _dot_sel_r(dt_t * a_t, _ones_where(triu))
    dta_e = _dot_sel_r(dta, expand)
    acum_e = _dot_sel_r(acum, expand)
    last_e = jnp.sum(dta_e, axis=0, keepdims=True)
    return acum, acum_t, acum_e, last_e


def _head_decay(acum, acum_t, h, tril):
    seg = acum[:, h:h + 1] - acum_t[h:h + 1, :]
    return jnp.exp(jnp.where(tril, seg, NEG))


def _ssd_fwd_call(xbc, dt, a):
    s = xbc.shape[0]
    nc = s // SSD_L
    dt_t = dt.T
    a_t = a.T

    def body(xbc_ref, dt_ref, dtt_ref, a_ref, at_ref, y_ref, st_ref, s_sc):
        ci = pl.program_id(0)

        @pl.when(ci == 0)
        def _():
            s_sc[...] = jnp.zeros_like(s_sc)

        st_ref[0] = s_sc[...]
        tril, triu, expand, _, low_half = _ssd_consts()
        acum, acum_t, acum_e, last_e = _ssd_decays(dt_ref[...], dtt_ref[...], a_ref[...], at_ref[...], tril, triu, expand)
        dt_e = _dot_sel_r(dt_ref[...], expand, passes=2)
        xdt = xbc_ref[:, :SSD_DI] * dt_e
        xdt_b = xdt.astype(BF16)
        xw_b = (xdt * jnp.exp(last_e - acum_e)).astype(BF16)
        ea_e = jnp.exp(acum_e)
        el_e = jnp.exp(last_e)
        for g in range(SSD_G):
            gs = slice(g * GW, (g + 1) * GW)
            bg = xbc_ref[:, B_COL + g * SSD_N:B_COL + (g + 1) * SSD_N]
            cg_b = xbc_ref[:, C_COL + g * SSD_N:C_COL + (g + 1) * SSD_N].astype(BF16)
            bg_b = bg.astype(BF16)
            cb = _nt(cg_b, bg_b)
            st = s_sc[:, gs]
            y_off = jnp.dot(cg_b, st.astype(BF16), preferred_element_type=F32) * ea_e[:, gs]
            for pr in range(SSD_HPG // 2):
                ls = slice(g * GW + pr * LANE, g * GW + (pr + 1) * LANE)
                xp = xdt_b[:, ls]
                yd = []
                for half in range(2):
                    h = g * SSD_HPG + pr * 2 + half
                    m = (cb * _head_decay(acum, acum_t, h, tril)).astype(BF16)
                    yd.append(jnp.dot(m, xp, preferred_element_type=F32))
                y_ref[:, ls] = jnp.where(low_half, yd[0], yd[1]) + y_off[:, pr * LANE:(pr + 1) * LANE]
            s_sc[:, gs] = st * el_e[:, gs] + jnp.dot(bg.T.astype(BF16), xw_b[:, gs], preferred_element_type=F32)

    row = lambda i: (i, 0)
    return pl.pallas_call(
        body, grid=(nc,),
        in_specs=[pl.BlockSpec((SSD_L, CONV_DIM), row), pl.BlockSpec((SSD_L, SSD_H), row),
                  pl.BlockSpec((SSD_H, SSD_L), lambda i: (0, i)), pl.BlockSpec((1, SSD_H), lambda i: (0, 0)),
                  pl.BlockSpec((SSD_H, 1), lambda i: (0, 0))],
        out_specs=[pl.BlockSpec((SSD_L, SSD_DI), row), pl.BlockSpec((1, SSD_N, SSD_DI), lambda i: (i, 0, 0))],
        out_shape=[jax.ShapeDtypeStruct((s, SSD_DI), F32), jax.ShapeDtypeStruct((nc, SSD_N, SSD_DI), F32)],
        scratch_shapes=[pltpu.VMEM((SSD_N, SSD_DI), F32)],
        compiler_params=_cparams("arbitrary"), name="ssd_fwd",
    )(xbc, dt, dt_t, a, a_t)


def _ssd_bwd_call(xbc, dt, a, states, dy, dx_extra):
    s = xbc.shape[0]
    nc = s // SSD_L
    dt_t = dt.T
    a_t = a.T

    def body(xbc_ref, dt_ref, dtt_ref, a_ref, at_ref, st_ref, dy_ref, dxe_ref,
             dxbc_ref, ddt_ref, da_ref, ds_sc, yf_sc, dxd_sc, dxw_sc):
        i = pl.program_id(0)

        @pl.when(i == 0)
        def _():
            ds_sc[...] = jnp.zeros_like(ds_sc)

        tril, triu, expand, reduce_, low_half = _ssd_consts()
        dt = dt_ref[...]
        a_row = a_ref[...]
        acum, acum_t, acum_e, last_e = _ssd_decays(dt, dtt_ref[...], a_row, at_ref[...], tril, triu, expand)
        dt_e = _dot_sel_r(dt, expand, passes=2)
        x = xbc_ref[:, :SSD_DI]
        xdt = x * dt_e
        xdt_b = xdt.astype(BF16)
        w_e = jnp.exp(last_e - acum_e)
        xw_b = (xdt * w_e).astype(BF16)
        ea_e = jnp.exp(acum_e)
        el_e = jnp.exp(last_e)
        dy = dy_ref[...]
        dy_b = dy.astype(BF16)
        s_prev = st_ref[0]
        ds_new = ds_sc[...]
        ds_new_b = ds_new.astype(BF16)
        triu_b = _ones_where(triu)
        strict_tril = jnp.logical_not(triu)
        head_ids = lax.broadcasted_iota(jnp.int32, (1, SSD_H), 1)
        d_dta_diag = jnp.zeros((SSD_L, SSD_H), F32)
        for g in range(SSD_G):
            gs = slice(g * GW, (g + 1) * GW)
            bs_ = slice(B_COL + g * SSD_N, B_COL + (g + 1) * SSD_N)
            cs_ = slice(C_COL + g * SSD_N, C_COL + (g + 1) * SSD_N)
            bg = xbc_ref[:, bs_]
            cg = xbc_ref[:, cs_]
            bg_b, cg_b = bg.astype(BF16), cg.astype(BF16)
            st_b = s_prev[:, gs].astype(BF16)
            y_off = jnp.dot(cg_b, st_b, preferred_element_type=F32) * ea_e[:, gs]
            yf_sc[:, gs] = y_off
            dz_b = (dy[:, gs] * ea_e[:, gs]).astype(BF16)
            d_c = _nt(dz_b, st_b)
            ds_prev = ds_new[:, gs] * el_e[:, gs] + jnp.dot(cg.T.astype(BF16), dz_b, preferred_element_type=F32)
            dxw_sc[:, gs] = jnp.dot(bg_b, ds_new_b[:, gs], preferred_element_type=F32)
            d_b = _nt(xw_b[:, gs], ds_new_b[:, gs])
            cb = _nt(cg_b, bg_b)
            d_g = jnp.zeros((SSD_L, SSD_L), F32)
            for pr in range(SSD_HPG // 2):
                ls = slice(g * GW + pr * LANE, g * GW + (pr + 1) * LANE)
                xp = xdt_b[:, ls]
                dyp = dy[:, ls]
                dyp_b = dy_b[:, ls]
                dxd = []
                for half in range(2):
                    h = g * SSD_HPG + pr * 2 + half
                    dec = _head_decay(acum, acum_t, h, tril)
                    m = cb * dec
                    dxd.append(jnp.dot(m.T.astype(BF16), dyp_b, preferred_element_type=F32))
                    mine = low_half if half == 0 else jnp.logical_not(low_half)
                    d_m = _nt(jnp.where(mine, dyp, 0.0).astype(BF16), xp)
                    d_g = d_g + d_m * dec
                    below = jnp.dot(triu_b, (d_m * m).astype(BF16), preferred_element_type=F32)
                    col = jnp.sum(jnp.where(strict_tril, below, 0.0), axis=1, keepdims=True)
                    d_dta_diag = d_dta_diag + col * jnp.where(head_ids == h, 1.0, 0.0)
                dxd_sc[:, ls] = jnp.where(low_half, dxd[0], dxd[1])
            d_g_b = d_g.astype(BF16)
            dxbc_ref[:, cs_] = d_c + jnp.dot(d_g_b, bg_b, preferred_element_type=F32)
            dxbc_ref[:, bs_] = d_b + jnp.dot(d_g.T.astype(BF16), cg_b, preferred_element_type=F32)
            ds_sc[:, gs] = ds_prev
        dxw = dxw_sc[...]
        dxd = dxd_sc[...]
        dw_e = xdt * dxw * w_e
        d_out = _dot_sel_r(dy * yf_sc[...], reduce_, passes=2)
        d_upd = _dot_sel_r(dw_e, reduce_, passes=2)
        d_tot_e = jnp.sum(ds_new * s_prev, axis=0, keepdims=True) * el_e
        d_tot = _dot_sel_r(jnp.broadcast_to(d_tot_e, (8, SSD_DI)), reduce_, passes=2)[0:1]
        d_dta = _dot_sel_l(triu_b, d_out) + _dot_sel_l(_ones_where(strict_tril), d_upd) + d_tot + d_dta_diag
        dxdt = dxd + dxw * w_e
        dxbc_ref[:, :SSD_DI] = dxdt * dt_e + dxe_ref[...]
        ddt_ref[...] = d_dta * a_row + _dot_sel_r(dxdt * x, reduce_, passes=2)
        _acc_store(da_ref, jnp.sum(d_dta * dt, axis=0, keepdims=True), i == 0)

    rev = lambda i: (nc - 1 - i, 0)
    return pl.pallas_call(
        body, grid=(nc,),
        in_specs=[pl.BlockSpec((SSD_L, CONV_DIM), rev), pl.BlockSpec((SSD_L, SSD_H), rev),
                  pl.BlockSpec((SSD_H, SSD_L), lambda i: (0, nc - 1 - i)), pl.BlockSpec((1, SSD_H), lambda i: (0, 0)),
                  pl.BlockSpec((SSD_H, 1), lambda i: (0, 0)),
                  pl.BlockSpec((1, SSD_N, SSD_DI), lambda i: (nc - 1 - i, 0, 0)),
                  pl.BlockSpec((SSD_L, SSD_DI), rev), pl.BlockSpec((SSD_L, SSD_DI), rev)],
        out_specs=[pl.BlockSpec((SSD_L, CONV_DIM), rev), pl.BlockSpec((SSD_L, SSD_H), rev),
                   pl.BlockSpec((1, SSD_H), lambda i: (0, 0))],
        out_shape=[jax.ShapeDtypeStruct((s, CONV_DIM), F32), jax.ShapeDtypeStruct((s, SSD_H), F32),
                   jax.ShapeDtypeStruct((1, SSD_H), F32)],
        scratch_shapes=[pltpu.VMEM((SSD_N, SSD_DI), F32), pltpu.VMEM((SSD_L, SSD_DI), F32),
                        pltpu.VMEM((SSD_L, SSD_DI), F32), pltpu.VMEM((SSD_L, SSD_DI), F32)],
        compiler_params=_cparams("arbitrary"), name="ssd_bwd",
    )(xbc, dt, dt_t, a, a_t, states, dy, dx_extra)


HBM_SPEC = pl.BlockSpec(memory_space=pltpu.HBM)
N_PEERS = N_DEV - 1


def _flip(v, f):
    return 1 - v if f else v


def _all_gather(shard):
    rows, c = shard.shape

    def body(x_ref, out_ref, send_sems, recv_sems, local_sem):
        x, y, cc = lax.axis_index("x"), lax.axis_index("y"), lax.axis_index("c")
        me, sibling = (x, y, cc), (x, y, 1 - cc)
        chips = [(1 - x, y), (x, 1 - y), (1 - x, 1 - y)]

        def slot(px, py, pc):
            return out_ref.at[4 * px + 2 * py + pc]

        def copy(k, block, to, src=None):
            return pltpu.make_async_remote_copy(
                src_ref=slot(*block) if src is None else src, dst_ref=slot(*block),
                send_sem=send_sems.at[k], recv_sem=recv_sems.at[k],
                device_id=to, device_id_type=pl.DeviceIdType.MESH)

        mine = pltpu.make_async_copy(x_ref, slot(*me), local_sem)
        mine.start()
        first = [copy(0, me, sibling, src=x_ref)]
        first += [copy(1 + j, me, (*chip, cc), src=x_ref) for j, chip in enumerate(chips)]
        for cp in first:
            cp.start()
        passed = [copy(4 + j, (*chip, cc), sibling) for j, chip in enumerate(chips)]
        for j, chip in enumerate(chips):
            copy(1 + j, (*chip, cc), me).wait_recv()
            passed[j].start()
        copy(0, sibling, me).wait_recv()
        for j, chip in enumerate(chips):
            copy(4 + j, (*chip, 1 - cc), me).wait_recv()
        for cp in first + passed:
            cp.wait_send()
        mine.wait()

    return pl.pallas_call(
        body, out_shape=jax.ShapeDtypeStruct((N_DEV, rows, c), shard.dtype),
        in_specs=[HBM_SPEC], out_specs=HBM_SPEC,
        scratch_shapes=[pltpu.SemaphoreType.DMA((N_PEERS,)), pltpu.SemaphoreType.DMA((N_PEERS,)), pltpu.SemaphoreType.DMA(())],
        name="all_gather",
    )(shard)


def _peer_copies(src_ref, out_ref, sems, gather, phase):
    send_sems, recv_sems, local_sem = sems
    x, y, cc = lax.axis_index("x"), lax.axis_index("y"), lax.axis_index("c")
    me = 4 * x + 2 * y + cc
    mine = pltpu.make_async_copy(src_ref if gather else src_ref.at[me], out_ref.at[me], local_sem)
    copies = []
    for k in range(1, N_DEV):
        px, py, pc = _flip(x, k & 4), _flip(y, k & 2), _flip(cc, k & 1)
        peer = 4 * px + 2 * py + pc
        src = src_ref if gather else src_ref.at[peer]
        copies.append((
            pltpu.make_async_remote_copy(
                src_ref=src, dst_ref=out_ref.at[me], send_sem=send_sems.at[k - 1], recv_sem=recv_sems.at[k - 1],
                device_id=(px, py, pc), device_id_type=pl.DeviceIdType.MESH),
            pltpu.make_async_remote_copy(
                src_ref=src, dst_ref=out_ref.at[peer], send_sem=send_sems.at[k - 1], recv_sem=recv_sems.at[k - 1],
                device_id=(px, py, pc), device_id_type=pl.DeviceIdType.MESH)))
    if phase == "start":
        mine.start()
        for send, _ in copies:
            send.start()
    else:
        for _, landed in copies:
            landed.wait_recv()
        for send, _ in copies:
            send.wait_send()
        mine.wait()


def _comm_scratch():
    return [pltpu.SemaphoreType.DMA((N_PEERS,)), pltpu.SemaphoreType.DMA((N_PEERS,)), pltpu.SemaphoreType.DMA(())]


def _gather_rider(shard):
    return dict(src=shard, out=jax.ShapeDtypeStruct((N_DEV,) + shard.shape, shard.dtype), gather=True)


def _exchange_rider(blocks):
    return dict(src=blocks, out=jax.ShapeDtypeStruct(blocks.shape, blocks.dtype), gather=False)


def _exchange_blocks(blocks):
    def body(g_ref, out_ref, *sems):
        _peer_copies(g_ref, out_ref, sems, False, "start")
        _peer_copies(g_ref, out_ref, sems, False, "finish")

    return pl.pallas_call(
        body, out_shape=jax.ShapeDtypeStruct(blocks.shape, blocks.dtype),
        in_specs=[HBM_SPEC], out_specs=HBM_SPEC, scratch_shapes=_comm_scratch(), name="exchange_blocks",
    )(blocks)


BIG = [
    ("ffn1_w13", (D_MODEL, 2 * D_FF), 1), ("ffn1_w2", (D_FF, D_MODEL), 0),
    ("w_ssd_out", (SSD_DI, D_MODEL), 0), ("w_uq", (Q_LORA, MLA_H * QK), 1), ("w_ukv", (KV_LORA, MLA_H * (NOPE + VDIM)), 1),
    ("w_mla_out", (MLA_H * VDIM, D_MODEL), 0), ("w_o", (D_MODEL, D_MODEL), 0),
    ("ffn2_w13", (D_MODEL, 2 * D_FF), 1), ("ffn2_w2", (D_FF, D_MODEL), 0), ("w_in", (D_MODEL, D_IN), 1),
]
assert all(_r % 16 == 0 for _r in [_f[0] * _f[1] // N_DEV // PACK_COLS for _, _f, _ in BIG[:-1]])
SMALL = [
    ("ln_ffn1", D_MODEL), ("ln_mix", D_MODEL), ("conv_b", CONV_DIM), ("dt_bias", SSD_H), ("a_log", SSD_H), ("d_skip", SSD_H),
    ("ssd_norm", SSD_DI), ("q_lora_norm", Q_LORA), ("kv_lora_norm", KV_LORA), ("q_norm", QK), ("k_norm", QK), ("ln_ffn2", D_MODEL),
]


def _shard_shape(full, axis):
    k, n = full
    return (k // N_DEV, n) if axis == 0 else (k, n // N_DEV)


def _shard_rows(full):
    return full[0] * full[1] // N_DEV // PACK_COLS


LAYER_ROWS = sum(_shard_rows(f) for _, f, _ in BIG)
LAYER_ROWS_PAD = -(-LAYER_ROWS // 256) * 256


def _pack_shards(shards):
    parts = [(shards[name] if axis == 0 else shards[name].T).reshape(-1, PACK_COLS) for name, _, axis in BIG]
    pad = LAYER_ROWS_PAD - LAYER_ROWS
    if pad:
        parts.append(jnp.zeros((pad, PACK_COLS), parts[0].dtype))
    return jnp.concatenate(parts, axis=0)


def _unpack_shards(packed):
    out, r = {}, 0
    for name, full, axis in BIG:
        n = _shard_rows(full)
        k, c = _shard_shape(full, axis)
        blk = packed[r:r + n]
        out[name] = blk.reshape(k, c) if axis == 0 else blk.reshape(c, k).T
        r += n
    return out


def _working_shape(full, axis):
    return full if axis == 0 else full[::-1]


def _unpack_gathered(gathered):
    out, r = {}, 0
    for name, full, axis in BIG:
        n = _shard_rows(full)
        out[name] = gathered[:, r:r + n].reshape(_working_shape(full, axis))
        r += n
    return out


def _pack_full_grads(grads):
    parts = [grads[name].reshape(N_DEV, -1, PACK_COLS) for name, _, _ in BIG]
    pad = LAYER_ROWS_PAD - LAYER_ROWS
    if pad:
        parts.append(jnp.zeros((N_DEV, pad, PACK_COLS), parts[0].dtype))
    return jnp.concatenate(parts, axis=1)


SMALL_COLS = sum(n for _, n in SMALL) + CONV_K * CONV_DIM
SMALL_ROWS = -(-(DEPTH * SMALL_COLS) // (8 * PACK_COLS)) * 8


def _pack_small(vals, conv_w):
    flat = jnp.concatenate([vals[name] for name, _ in SMALL] + [conv_w.reshape(DEPTH, -1)], axis=1).reshape(-1)
    flat = jnp.concatenate([flat, jnp.zeros((SMALL_ROWS * PACK_COLS - flat.shape[0],), F32)])
    return flat.reshape(SMALL_ROWS, PACK_COLS)


def _unpack_small(packed):
    flat = packed.reshape(-1)[:DEPTH * SMALL_COLS].reshape(DEPTH, SMALL_COLS)
    out, c = {}, 0
    for name, n in SMALL:
        out[name] = flat[:, c:c + n]
        c += n
    return out, flat[:, c:].reshape(DEPTH, CONV_K, CONV_DIM)


_IN_OFFS = [sum(IN_SPLIT[:i]) for i in range(len(IN_SPLIT) + 1)]


def _arrange_w_in(w_t):
    z, xbc, dt, cq, ckv, kr, gates = [w_t[_IN_OFFS[i]:_IN_OFFS[i + 1]] for i in range(len(IN_SPLIT))]
    pad = jnp.zeros((LANE - ROPE - SSD_H, w_t.shape[1]), w_t.dtype)
    return jnp.concatenate([z, gates, xbc, cq, ckv, kr, dt, pad], axis=0)


def _restore_w_in(g):
    z, gates, xbc = g[PROJ_Z:PROJ_GATES], g[PROJ_GATES:PROJ_XBC], g[PROJ_XBC:PROJ_CQ]
    cq, ckv = g[PROJ_CQ:PROJ_CKV], g[PROJ_CKV:PROJ_LAST]
    kr, dt = g[PROJ_LAST:PROJ_LAST + ROPE], g[PROJ_LAST + ROPE:PROJ_LAST + ROPE + SSD_H]
    return jnp.concatenate([z, xbc, dt, cq, ckv, kr, gates], axis=0)


def _pad_heads(w_t):
    k = w_t.shape[1]
    return jnp.pad(w_t.reshape(MLA_H, QK, k), ((0, 0), (0, HEAD_COLS - QK), (0, 0))).reshape(MLA_H * HEAD_COLS, k)


def _unpad_heads(g):
    k = g.shape[1]
    return g.reshape(MLA_H, HEAD_COLS, k)[:, :QK].reshape(MLA_H * QK, k)


def _row(v):
    return v.reshape(1, -1)


def _head_gain(g):
    return jnp.pad(g, (0, HEAD_COLS - QK)).reshape(1, HEAD_COLS)


def _ffn_fwd(h, ln, w13_t, w2, name):
    n = _row_fwd(_f_rmsnorm, [h], [_row(ln)], [BF16], name + "_fwd")[0]
    act, gate, up = _ffn_up_call(n, w13_t)
    return _mm(act, w2, alpha=0.5, res=h), (h, n, gate, up, act)


def _ffn_bwd(dh_out, saved, ln, w13_t, w2, name):
    h, n, gate, up, act = saved
    d_gate, d_up = _ffn_down_bwd_call(dh_out, w2, gate, up)
    d_w2 = _mm(act, dh_out, ta=True, out_dtype=BF16, alpha=0.5)
    d_n = _mm(d_gate, w13_t, b_rows=(0, D_FF))
    dh, d_ln = _mm(d_up, w13_t, b_rows=(D_FF, D_FF), res=d_n, norm_bwd=(h, _row(ln), dh_out))
    d_w13_t = jnp.concatenate([_mm(d_gate, n, ta=True, out_dtype=BF16), _mm(d_up, n, ta=True, out_dtype=BF16)], axis=0)
    return dh, d_w13_t, d_w2, d_ln[0]


def _mixer_fwd(h, big, small, conv_w, cs, sn, rider=None):
    s = h.shape[0]
    u = _row_fwd(_f_rmsnorm, [h], [_row(small["ln_mix"])], [BF16], "ln_mix_fwd")[0]
    proj = _mm(u, big["w_in"], tb=True)
    xbc = _conv_fwd_call(proj, PROJ_XBC, conv_w, _row(small["conv_b"]))
    dt_in = proj[:, PROJ_LAST + ROPE:PROJ_LAST + ROPE + SSD_H] + small["dt_bias"][None, :]
    dt = jax.nn.softplus(dt_in)
    a = -jnp.exp(small["a_log"])[None, :]
    y_scan, states = _ssd_fwd_call(xbc, dt, a)
    dsk = _row(jnp.repeat(small["d_skip"], SSD_P))
    gn_in = [y_scan, _win(xbc, 0, SSD_DI), _win(proj, PROJ_Z, SSD_DI)]
    yn = _row_fwd(_f_gated_norm, gn_in, [dsk, _row(small["ssd_norm"])], [BF16], "gated_norm_fwd")[0]
    y_ssd = _mm(yn, big["w_ssd_out"])
    qn = _row_fwd(_f_rmsnorm, [_win(proj, PROJ_CQ, Q_LORA)], [_row(small["q_lora_norm"])], [BF16], "q_lora_norm_fwd")[0]
    kvn = _row_fwd(_f_rmsnorm, [_win(proj, PROJ_CKV, KV_LORA)], [_row(small["kv_lora_norm"])], [BF16], "kv_lora_norm_fwd")[0]
    q = _mm(qn, big["w_uq"], tb=True)
    kv = _mm(kvn, big["w_ukv"], tb=True)
    qh, kh, vh = _heads_fwd_call(q, kv, proj, cs, sn, _head_gain(small["q_norm"]), _head_gain(small["k_norm"]))
    o, lse, *carried = _attn_fwd_call(qh, kh, vh, rider)
    o_rows = jnp.transpose(o, (1, 0, 2)).reshape(s, MLA_H * VDIM)
    y_mla = _mm(o_rows, big["w_mla_out"])
    out, mg = _merge_out_call(proj, y_ssd, y_mla, big["w_o"], h)
    saved = (h, u, proj, xbc, dt_in, dt, a, y_scan, states, dsk, yn, y_ssd, qn, kvn, q, kv, qh, kh, vh, o, lse, o_rows, y_mla, mg)
    return out, saved, (carried[0] if carried else None)


def _mixer_bwd(dh_out, saved, big, small, conv_w, cs, sn, rider=None):
    (h, u, proj, xbc, dt_in, dt, a, y_scan, states, dsk, yn, y_ssd, qn, kvn, q, kv, qh, kh, vh, o, lse, o_rows, y_mla, mg) = saved
    s = h.shape[0]
    d_big, d_small = {}, {}
    d_gates, d_y_ssd, d_y_mla = _merge_out_bwd_call(dh_out, big["w_o"], proj, y_ssd, y_mla)
    d_big["w_o"] = _mm(mg, dh_out, ta=True, out_dtype=BF16)
    d_o_rows = _mm(d_y_mla, big["w_mla_out"], tb=True, out_dtype=BF16)
    d_big["w_mla_out"] = _mm(o_rows, d_y_mla, ta=True, out_dtype=BF16)
    d_o = jnp.transpose(d_o_rows.reshape(s, MLA_H, VDIM), (1, 0, 2))
    delta = _attn_delta_call(o, d_o)
    *d_heads, carried = list(_attn_bwd_call(qh, kh, vh, d_o, lse.reshape(MLA_H, 1, s), delta.reshape(MLA_H, 1, s), rider)) + ([None] if rider is None else [])
    d_q, d_kv, d_kr, d_qg, d_kg = _heads_bwd_call(
        q, kv, proj, cs, sn, _head_gain(small["q_norm"]), _head_gain(small["k_norm"]), *d_heads)
    d_small["q_norm"], d_small["k_norm"] = d_qg[0, :QK], d_kg[0, :QK]
    d_qn = _mm(d_q, big["w_uq"], out_dtype=BF16)
    d_big["w_uq"] = _mm(d_q, qn, ta=True, out_dtype=BF16)
    d_kvn = _mm(d_kv, big["w_ukv"], out_dtype=BF16)
    d_big["w_ukv"] = _mm(d_kv, kvn, ta=True, out_dtype=BF16)
    (d_cq,), (d_g,) = _row_bwd(_f_rmsnorm, [_win(proj, PROJ_CQ, Q_LORA)], [_row(small["q_lora_norm"])], [d_qn], [BF16], "q_lora_norm_bwd")
    d_small["q_lora_norm"] = d_g[0]
    (d_ckv,), (d_g,) = _row_bwd(_f_rmsnorm, [_win(proj, PROJ_CKV, KV_LORA)], [_row(small["kv_lora_norm"])], [d_kvn], [BF16], "kv_lora_norm_bwd")
    d_small["kv_lora_norm"] = d_g[0]
    d_yn = _mm(d_y_ssd, big["w_ssd_out"], tb=True, out_dtype=BF16)
    d_big["w_ssd_out"] = _mm(yn, d_y_ssd, ta=True, out_dtype=BF16)
    gn_in = [y_scan, _win(xbc, 0, SSD_DI), _win(proj, PROJ_Z, SSD_DI)]
    (d_y_scan, d_xs, d_z), (d_dsk, d_g) = _row_bwd(
        _f_gated_norm, gn_in, [dsk, _row(small["ssd_norm"])], [d_yn], [F32, F32, BF16], "gated_norm_bwd")
    d_small["ssd_norm"] = d_g[0]
    d_small["d_skip"] = jnp.sum(d_dsk.reshape(SSD_H, SSD_P), axis=1)
    d_xbc_act, d_dt, d_a = _ssd_bwd_call(xbc, dt, a, states, d_y_scan, d_xs)
    d_xbc, d_conv_w, d_conv_b = _conv_bwd_call(proj, PROJ_XBC, conv_w, _row(small["conv_b"]), d_xbc_act)
    d_small["conv_b"] = d_conv_b[0]
    d_dt_in = d_dt * jax.nn.sigmoid(dt_in)
    d_small["dt_bias"] = jnp.sum(d_dt_in, axis=0)
    d_small["a_log"] = d_a[0] * a[0]
    d_last = (d_kr + jnp.pad(d_dt_in, ((0, 0), (ROPE, LANE - ROPE - SSD_H)))).astype(BF16)
    d_proj = jnp.concatenate([d_z, d_gates, d_xbc, d_cq, d_ckv, d_last], axis=1)
    dh, d_ln = _mm(d_proj, big["w_in"], norm_bwd=(h, _row(small["ln_mix"]), dh_out))
    d_big["w_in"] = _mm(d_proj, u, ta=True, out_dtype=BF16)
    d_small["ln_mix"] = d_ln[0]
    return dh, d_big, d_small, d_conv_w, carried


def _prepare_big(b):
    return dict(b, w_in=_arrange_w_in(b["w_in"]), w_uq=_pad_heads(b["w_uq"]))


def _local_step(x, positions, target, big, small, conv_w, packed_last=None):
    inv = 1.0 / (ROPE_THETA ** (jnp.arange(0, ROPE, 2, dtype=F32) / ROPE))
    ang = positions.astype(F32)[:, None] * inv
    cos, sin = jnp.cos(ang), jnp.sin(ang)
    no_lanes = jnp.zeros((x.shape[0], LANE - ROPE), F32)
    cs = jnp.concatenate([cos, cos, no_lanes], axis=1)
    sn = jnp.concatenate([-sin, sin, no_lanes], axis=1)
    carrier = DEPTH - 2 if packed_last is not None else None
    big = [None if b is None else _prepare_big(b) for b in big]
    layer_small = [{k: v[l] for k, v in small.items()} for l in range(DEPTH)]

    h, saved = x, []
    for l in range(DEPTH):
        b, sm = big[l], layer_small[l]
        h, s1 = _ffn_fwd(h, sm["ln_ffn1"], b["ffn1_w13"], b["ffn1_w2"], "ln_ffn1")
        h, s2, gathered = _mixer_fwd(h, b, sm, conv_w[l], cs, sn, _gather_rider(packed_last) if l == carrier else None)
        if gathered is not None:
            big[l + 1] = _prepare_big(_unpack_gathered(gathered))
        h, s3 = _ffn_fwd(h, sm["ln_ffn2"], b["ffn2_w13"], b["ffn2_w2"], "ln_ffn2")
        saved.append((s1, s2, s3))
    loss, dh = _loss_and_grad(h, target)

    d_big, d_small, d_conv_w = [None] * DEPTH, [None] * DEPTH, [None] * DEPTH
    for l in reversed(range(DEPTH)):
        b, sm = big[l], layer_small[l]
        s1, s2, s3 = saved[l]
        dh, d_w13_2, d_w2_2, d_ln2 = _ffn_bwd(dh, s3, sm["ln_ffn2"], b["ffn2_w13"], b["ffn2_w2"], "ln_ffn2")
        rider = _exchange_rider(_pack_full_grads(d_big[l + 1])) if l == carrier else None
        dh, db, ds, d_conv_w[l], received = _mixer_bwd(dh, s2, b, sm, conv_w[l], cs, sn, rider)
        if received is not None:
            d_big[l + 1] = received
        dh, d_w13_1, d_w2_1, d_ln1 = _ffn_bwd(dh, s1, sm["ln_ffn1"], b["ffn1_w13"], b["ffn1_w2"], "ln_ffn1")
        db.update(ffn1_w13=d_w13_1, ffn1_w2=d_w2_1, ffn2_w13=d_w13_2, ffn2_w2=d_w2_2,
                  w_in=_restore_w_in(db["w_in"]), w_uq=_unpad_heads(db["w_uq"]))
        ds.update(ln_ffn1=d_ln1, ln_ffn2=d_ln2)
        d_big[l], d_small[l] = db, ds
    d_small = {name: jnp.stack([d_small[l][name] for l in range(DEPTH)]) for name, _ in SMALL}
    return loss, dh, d_big, d_small, jnp.stack(d_conv_w)


def _step(args):
    dev = 4 * lax.axis_index("x") + 2 * lax.axis_index("y") + lax.axis_index("c")
    x, positions, target = args["x"][0], args["positions"][0], args["loss_target"][0]

    packed = [_pack_shards({name: args[name][l].astype(BF16) for name, _, _ in BIG}) for l in range(DEPTH)]
    big = [_unpack_gathered(_all_gather(packed[l])) for l in range(DEPTH - 1)] + [None]
    cw = args["conv_w"]
    cw_cols = cw.shape[-1]
    cw_rows = -(-cw.size // (8 * PACK_COLS)) * 8
    cw_flat = jnp.concatenate([cw.reshape(-1), jnp.zeros((cw_rows * PACK_COLS - cw.size,), F32)]).reshape(cw_rows, PACK_COLS)
    cw_all = _all_gather(cw_flat).reshape(N_DEV, -1)[:, :cw.size].reshape(N_DEV, DEPTH, CONV_K, cw_cols)
    conv_w = jnp.transpose(cw_all, (1, 2, 0, 3)).reshape(DEPTH, CONV_K, CONV_DIM)
    small = {name: args[name] for name, _ in SMALL}

    loss, dx, d_big, d_small, d_conv_w = _local_step(x, positions, target, big, small, conv_w, packed_last=packed[-1])
    loss = lax.psum(loss, MESH_AXES)

    out = {"loss": loss, "grad_x": dx[None]}

    grads = {name: [] for name, _, _ in BIG}
    for l in range(DEPTH):
        received = d_big[l] if l == DEPTH - 1 else _exchange_blocks(_pack_full_grads(d_big[l]))
        summed = _sum_blocks(received)
        for name, g in _unpack_shards(summed).items():
            grads[name].append(g)
    flat = lambda t: t.reshape(-1, t.shape[-1])
    for name, _, _ in BIG:
        g = jnp.stack(grads[name])
        w = args[name]
        delta, m2, v2 = _adam(flat(w), flat(g), flat(args["m_" + name]), flat(args["v_" + name]))
        out["grad_" + name] = g
        out["delta_" + name] = delta.reshape(w.shape)
        out["new_m_" + name] = m2.reshape(w.shape)
        out["new_v_" + name] = v2.reshape(w.shape)

    total = _sum_blocks(_all_gather(_pack_small(d_small, d_conv_w)))
    g_conv_w = _unpack_small(total)[1]
    zeros_cw = jnp.zeros((DEPTH, CONV_K, CONV_DIM), F32)
    delta, m2, v2 = _adam(_pack_small(small, zeros_cw), total,
                          _pack_small({name: args["m_" + name] for name, _ in SMALL}, zeros_cw),
                          _pack_small({name: args["v_" + name] for name, _ in SMALL}, zeros_cw))
    for kind, packed in (("grad_", total), ("delta_", delta), ("new_m_", m2), ("new_v_", v2)):
        for name, val in _unpack_small(packed)[0].items():
            out[kind + name] = val
    g_cw = lax.dynamic_slice_in_dim(g_conv_w, dev * cw_cols, cw_cols, axis=2)
    delta, m2, v2 = _adam(flat(cw), flat(g_cw), flat(args["m_conv_w"]), flat(args["v_conv_w"]))
    out["grad_conv_w"] = g_cw
    out["delta_conv_w"] = delta.reshape(cw.shape)
    out["new_m_conv_w"] = m2.reshape(cw.shape)
    out["new_v_conv_w"] = v2.reshape(cw.shape)
    return out


WEIGHTS = ["ln_ffn1", "ffn1_w13", "ffn1_w2", "ln_mix", "w_in", "conv_w", "conv_b", "dt_bias", "a_log", "d_skip", "ssd_norm",
           "w_ssd_out", "q_lora_norm", "w_uq", "kv_lora_norm", "w_ukv", "q_norm", "k_norm", "w_mla_out", "w_o", "ln_ffn2",
           "ffn2_w13", "ffn2_w2"]
ARG_NAMES = (["x", "positions"] + WEIGHTS + ["loss_target"] + ["m_" + n for n in WEIGHTS] + ["v_" + n for n in WEIGHTS])


def kernel(x, positions, ln_ffn1, ffn1_w13, ffn1_w2, ln_mix, w_in, conv_w, conv_b, dt_bias, a_log, d_skip, ssd_norm, w_ssd_out, q_lora_norm, w_uq, kv_lora_norm, w_ukv, q_norm, k_norm, w_mla_out, w_o, ln_ffn2, ffn2_w13, ffn2_w2, loss_target, m_ln_ffn1, m_ffn1_w13, m_ffn1_w2, m_ln_mix, m_w_in, m_conv_w, m_conv_b, m_dt_bias, m_a_log, m_d_skip, m_ssd_norm, m_w_ssd_out, m_q_lora_norm, m_w_uq, m_kv_lora_norm, m_w_ukv, m_q_norm, m_k_norm, m_w_mla_out, m_w_o, m_ln_ffn2, m_ffn2_w13, m_ffn2_w2, v_ln_ffn1, v_ffn1_w13, v_ffn1_w2, v_ln_mix, v_w_in, v_conv_w, v_conv_b, v_dt_bias, v_a_log, v_d_skip, v_ssd_norm, v_w_ssd_out, v_q_lora_norm, v_w_uq, v_kv_lora_norm, v_w_ukv, v_q_norm, v_k_norm, v_w_mla_out, v_w_o, v_ln_ffn2, v_ffn2_w13, v_ffn2_w2):
    vals = (x, positions, ln_ffn1, ffn1_w13, ffn1_w2, ln_mix, w_in, conv_w, conv_b, dt_bias, a_log, d_skip, ssd_norm, w_ssd_out, q_lora_norm, w_uq, kv_lora_norm, w_ukv, q_norm, k_norm, w_mla_out, w_o, ln_ffn2, ffn2_w13, ffn2_w2, loss_target, m_ln_ffn1, m_ffn1_w13, m_ffn1_w2, m_ln_mix, m_w_in, m_conv_w, m_conv_b, m_dt_bias, m_a_log, m_d_skip, m_ssd_norm, m_w_ssd_out, m_q_lora_norm, m_w_uq, m_kv_lora_norm, m_w_ukv, m_q_norm, m_k_norm, m_w_mla_out, m_w_o, m_ln_ffn2, m_ffn2_w13, m_ffn2_w2, v_ln_ffn1, v_ffn1_w13, v_ffn1_w2, v_ln_mix, v_w_in, v_conv_w, v_conv_b, v_dt_bias, v_a_log, v_d_skip, v_ssd_norm, v_w_ssd_out, v_q_lora_norm, v_w_uq, v_kv_lora_norm, v_w_ukv, v_q_norm, v_k_norm, v_w_mla_out, v_w_o, v_ln_ffn2, v_ffn2_w13, v_ffn2_w2)
    out = _step(dict(zip(ARG_NAMES, vals)))
    order = ["loss", "grad_x"] + [k + n for k in ("grad_", "delta_", "new_m_", "new_v_") for n in WEIGHTS]
    return tuple(out[n] for n in order)
```

```python
import jax
import jax.numpy as jnp
from jax import lax
from jax.experimental import pallas as pl
from jax.experimental.pallas import tpu as pltpu

F32 = jnp.float32
BF16 = jnp.bfloat16

D_MODEL = 1024
D_FF = 2816
DEPTH = 2
SSD_DI = 2048
SSD_P = 64
SSD_H = 32
SSD_G = 4
SSD_HPG = 8
SSD_N = 128
SSD_L = 128
CONV_K = 4
CONV_DIM = 3072
MLA_H = 8
Q_LORA = 512
KV_LORA = 256
NOPE = 128
ROPE = 64
VDIM = 128
QK = 192
ROPE_THETA = 10000.0
EPS = 1e-6
IN_SPLIT = (SSD_DI, CONV_DIM, SSD_H, Q_LORA, KV_LORA, ROPE, 2 * D_MODEL)
D_IN = sum(IN_SPLIT)
N_DEV = 8
LANE = 128
PACK_COLS = 1024

PROJ_Z = 0
PROJ_GATES = PROJ_Z + SSD_DI
PROJ_XBC = PROJ_GATES + 2 * D_MODEL
PROJ_CQ = PROJ_XBC + CONV_DIM
PROJ_CKV = PROJ_CQ + Q_LORA
PROJ_LAST = PROJ_CKV + KV_LORA
D_IN_PAD = PROJ_LAST + LANE

ADAM_LR = 0.001
ADAM_B1 = 0.9
ADAM_B2 = 0.999
ADAM_EPS = 1e-08
ADAM_WD = 0.01
ADAM_STEP = 10

VMEM_LIMIT = 48 * 1024 * 1024
ROW_IO_BUDGET = 8 * 1024 * 1024
NEG = -1e30

MESH_AXES = ("x", "y", "c")


def _cparams(*sem):
    return pltpu.CompilerParams(dimension_semantics=sem, vmem_limit_bytes=VMEM_LIMIT)


def _pick_tile(n, target, align):
    if n <= target:
        return n
    best = None
    for t in range(align, target + 1, align):
        if n % t == 0:
            best = t
    assert best is not None, (n, target, align)
    return best


def _acc_store(ref, val, first):
    @pl.when(first)
    def _():
        ref[...] = val

    @pl.when(jnp.logical_not(first))
    def _():
        ref[...] += val


def _win(arr, start, width):
    assert start % width == 0, (start, width)
    return (arr, start, width)


def _operand(entry):
    if isinstance(entry, tuple):
        arr, start, width = entry
        return arr, width, start // width
    return entry, entry.shape[1], 0


def _row_tile(rows, bytes_per_row):
    if rows <= 16:
        return rows
    t = 1024
    while t > 16 and (t * bytes_per_row > ROW_IO_BUDGET or rows % t):
        t //= 2
    assert rows % t == 0, (rows, t)
    return t


def _rowwise_call(fn, tiled, params, outs, accs, name):
    ops = [_operand(e) for e in tiled]
    rows = ops[0][0].shape[0]
    per_row = sum(w * a.dtype.itemsize for a, w, _ in ops) + sum(c * jnp.dtype(d).itemsize for c, d in outs)
    tile = _row_tile(rows, per_row)
    n_in = len(tiled) + len(params)
    n_o = len(outs)

    def body(*refs):
        vals = [r[...] for r in refs[:n_in]]
        t_out, a_out = fn(*vals)
        for r, v in zip(refs[n_in:n_in + n_o], t_out):
            r[...] = v.astype(r.dtype)
        first = pl.program_id(0) == 0
        for r, v in zip(refs[n_in + n_o:], a_out):
            _acc_store(r, v.astype(F32), first)

    def tiled_spec(width, blk):
        return pl.BlockSpec((tile, width), lambda i: (i, blk))

    in_specs = [tiled_spec(w, blk) for _, w, blk in ops]
    in_specs += [pl.BlockSpec(p.shape, lambda i: (0, 0)) for p in params]
    out_specs = [tiled_spec(c, 0) for c, _ in outs]
    out_specs += [pl.BlockSpec(s, lambda i: (0, 0)) for s in accs]
    out_shape = [jax.ShapeDtypeStruct((rows, c), d) for c, d in outs]
    out_shape += [jax.ShapeDtypeStruct(s, F32) for s in accs]
    return pl.pallas_call(
        body, grid=(rows // tile,), in_specs=in_specs, out_specs=out_specs, out_shape=out_shape,
        compiler_params=_cparams("arbitrary"), name=name,
    )(*[a for a, _, _ in ops], *params)


def _to_f32(vals):
    return [v.astype(F32) for v in vals]


def _row_fwd(f, tiled, params, out_dtypes, name):
    ops = [_operand(e) for e in tiled]
    rows = ops[0][0].shape[0]
    shapes = jax.eval_shape(f, *[jax.ShapeDtypeStruct((rows, w), F32) for _, w, _ in ops],
                            *[jax.ShapeDtypeStruct(p.shape, F32) for p in params])
    outs = [(s.shape[1], d) for s, d in zip(shapes, out_dtypes)]
    return _rowwise_call(lambda *v: (f(*_to_f32(v)), ()), tiled, params, outs, [], name)


def _row_bwd(f, tiled, params, gs, d_dtypes, name, bwd=None, add=None):
    n_t, n_g = len(tiled), len(gs)
    adds = sorted((add or {}).items())
    n_a = len(adds)

    def fn(*vals):
        vals = _to_f32(vals)
        prim = vals[:n_t] + vals[n_t + n_g + n_a:]
        g = tuple(vals[n_t:n_t + n_g])
        if bwd is not None:
            d_t, d_p = bwd(*prim, *g)
        else:
            _, vjp = jax.vjp(f, *prim)
            cts = vjp(g)
            d_t, d_p = cts[:n_t], cts[n_t:]
        d_t = list(d_t)
        for (idx, _), extra in zip(adds, vals[n_t + n_g:n_t + n_g + n_a]):
            d_t[idx] = d_t[idx] + extra
        return tuple(d_t), tuple(d_p)

    outs = [(_operand(e)[1], d) for e, d in zip(tiled, d_dtypes)]
    accs = [p.shape for p in params]
    res = _rowwise_call(fn, list(tiled) + list(gs) + [a for _, a in adds], params, outs, accs, name)
    return res[:n_t], res[n_t:]


def _f_rmsnorm(x, g):
    return (x * lax.rsqrt(jnp.mean(x * x, axis=-1, keepdims=True) + EPS) * g,)


def _f_gated_norm(ys, xs, z, dsk, g):
    t = (ys + xs * dsk) * (z * jax.nn.sigmoid(z))
    return (t * lax.rsqrt(jnp.mean(t * t, axis=-1, keepdims=True) + EPS) * g,)


def _f_merge(gates, ys, ym):
    s = jax.nn.sigmoid(gates)
    return (s[:, :D_MODEL] * ys + s[:, D_MODEL:] * ym,)


def _b_merge(gates, ys, ym, d):
    s = jax.nn.sigmoid(gates)
    s1, s2 = s[:, :D_MODEL], s[:, D_MODEL:]
    d_gates = jnp.concatenate([d * ys * s1 * (1.0 - s1), d * ym * s2 * (1.0 - s2)], axis=1)
    return (d_gates, d * s1, d * s2), ()


def _loss_and_grad(y, target):
    def fn(yv, tv):
        d = yv - tv
        return (d * (1.0 / D_MODEL),), (jnp.sum(d * d, axis=0, keepdims=True) * (0.5 / D_MODEL),)

    dy, part = _rowwise_call(fn, [y, target], [], [(D_MODEL, F32)], [(1, D_MODEL)], "loss")
    return jnp.sum(part), dy


def _adam(w, g, m, v):
    def fn(wv, gv, mv, vv):
        m2 = ADAM_B1 * mv + (1.0 - ADAM_B1) * gv
        v2 = ADAM_B2 * vv + (1.0 - ADAM_B2) * (gv * gv)
        m_hat = m2 / (1.0 - ADAM_B1 ** ADAM_STEP)
        v_hat = v2 / (1.0 - ADAM_B2 ** ADAM_STEP)
        delta = -ADAM_LR * (m_hat / (jnp.sqrt(v_hat) + ADAM_EPS) + ADAM_WD * wv)
        return (delta, m2, v2), ()

    c = w.shape[1]
    return _rowwise_call(fn, [w, g, m, v], [], [(c, F32)] * 3, [], "adamw")


def _sum_blocks(blocks):
    _, rows, c = blocks.shape
    tile = _row_tile(rows, N_DEV * c * blocks.dtype.itemsize + c * 4)

    def body(b_ref, o_ref):
        acc = b_ref[0].astype(F32)
        for i in range(1, N_DEV):
            acc = acc + b_ref[i].astype(F32)
        o_ref[...] = acc

    return pl.pallas_call(
        body, grid=(rows // tile,), in_specs=[pl.BlockSpec((N_DEV, tile, c), lambda i: (0, i, 0))],
        out_specs=pl.BlockSpec((tile, c), lambda i: (i, 0)), out_shape=jax.ShapeDtypeStruct((rows, c), F32),
        compiler_params=_cparams("arbitrary"), name="sum_blocks",
    )(blocks)


def _mm(a, b, ta=False, tb=False, out_dtype=F32, alpha=1.0, res=None, b_rows=None, norm_bwd=None):
    r_dim, p_dim = a.shape if ta else a.shape[::-1]
    b_row0, b_nrows = (0, b.shape[0]) if b_rows is None else b_rows
    r2, q_dim = (b.shape[1], b_nrows) if tb else (b_nrows, b.shape[1])
    assert r_dim == r2, (a.shape, b.shape, ta, tb)
    tp = _pick_tile(p_dim, 512, LANE)
    if tp < 512 < p_dim:
        tp = _pick_tile(p_dim, 1536, LANE)
    tq = _pick_tile(q_dim, 1536, LANE)
    tr = _pick_tile(r_dim, 1536, LANE)
    nr = r_dim // tr
    dims = (((0 if ta else 1,), (1 if tb else 0,)), ((), ()))
    has_res = res is not None
    n_nb = 0 if norm_bwd is None else 3
    assert norm_bwd is None or tq == q_dim

    def body(*refs):
        a_ref, b_ref = refs[:2]
        res_ref = refs[2] if has_res else None
        n_in = 2 + has_res + n_nb
        o_ref = refs[n_in]

        def finish(val):
            if alpha != 1.0:
                val = val * alpha
            if has_res:
                val = val + res_ref[...].astype(F32)
            if norm_bwd is not None:
                x_ref, g_ref, add_ref = refs[2 + has_res:n_in]
                x = x_ref[...]
                r = lax.rsqrt(jnp.mean(x * x, axis=-1, keepdims=True) + EPS)
                gy = val * g_ref[...]
                dot = jnp.sum(gy * x, axis=-1, keepdims=True)
                _acc_store(refs[n_in + 1], jnp.sum(val * x * r, axis=0, keepdims=True), pl.program_id(1) == 0)
                val = gy * r - x * (dot * (r * r * r) * (1.0 / q_dim)) + add_ref[...]
            o_ref[...] = val.astype(o_ref.dtype)

        part = lax.dot_general(a_ref[...].astype(BF16), b_ref[...].astype(BF16), dims, preferred_element_type=F32)
        if nr == 1:
            finish(part)
        else:
            acc_ref = refs[-1]
            k = pl.program_id(2)
            _acc_store(acc_ref, part, k == 0)

            @pl.when(k == nr - 1)
            def _():
                finish(acc_ref[...])

    a_spec = pl.BlockSpec((tr, tp), lambda j, i, k: (k, i)) if ta else pl.BlockSpec((tp, tr), lambda j, i, k: (i, k))
    assert b_row0 % (tq if tb else tr) == 0
    b0 = b_row0 // (tq if tb else tr)
    b_spec = pl.BlockSpec((tq, tr), lambda j, i, k: (j + b0, k)) if tb else pl.BlockSpec((tr, tq), lambda j, i, k: (k + b0, j))
    o_spec = pl.BlockSpec((tp, tq), lambda j, i, k: (i, j))
    row_spec = pl.BlockSpec((1, tq), lambda j, i, k: (0, 0))
    in_specs = [a_spec, b_spec] + ([o_spec] if has_res else []) + ([o_spec, row_spec, o_spec] if n_nb else [])
    out = pl.pallas_call(
        body, grid=(q_dim // tq, p_dim // tp, nr), in_specs=in_specs,
        out_specs=[o_spec] + ([row_spec] if n_nb else []),
        out_shape=[jax.ShapeDtypeStruct((p_dim, q_dim), out_dtype)] + ([jax.ShapeDtypeStruct((1, q_dim), F32)] if n_nb else []),
        scratch_shapes=[pltpu.VMEM((tp, tq), F32)] if nr > 1 else [],
        compiler_params=_cparams("arbitrary", "arbitrary", "arbitrary"),
        name=f"mm_{'t' if ta else 'n'}{'t' if tb else 'n'}_{p_dim}x{r_dim}x{q_dim}" + ("_norm_bwd" if n_nb else ""),
    )(*([a, b] + ([res] if has_res else []) + (list(norm_bwd) if n_nb else [])))
    return out if n_nb else out[0]


MERGE_TP = 256


def _merge_out_call(proj, y_ssd, y_mla, w_o, h):
    s = h.shape[0]
    tp = min(MERGE_TP, s)

    def body(g_ref, ys_ref, ym_ref, w_ref, h_ref, o_ref, mg_ref):
        mg = _f_merge(g_ref[...], ys_ref[...], ym_ref[...])[0].astype(BF16)
        mg_ref[...] = mg
        o_ref[...] = h_ref[...] + jnp.dot(mg, w_ref[...], preferred_element_type=F32)

    rows = pl.BlockSpec((tp, D_MODEL), lambda i: (i, 0))
    return pl.pallas_call(
        body, grid=(s // tp,),
        in_specs=[pl.BlockSpec((tp, 2 * D_MODEL), lambda i: (i, PROJ_GATES // (2 * D_MODEL))), rows, rows,
                  pl.BlockSpec((D_MODEL, D_MODEL), lambda i: (0, 0)), rows],
        out_specs=[rows, rows],
        out_shape=[jax.ShapeDtypeStruct((s, D_MODEL), F32), jax.ShapeDtypeStruct((s, D_MODEL), BF16)],
        compiler_params=_cparams("arbitrary"), name="merge_out",
    )(proj, y_ssd, y_mla, w_o, h)


def _merge_out_bwd_call(dh, w_o, proj, y_ssd, y_mla):
    s = dh.shape[0]
    tp = min(MERGE_TP, s)

    def body(dh_ref, w_ref, g_ref, ys_ref, ym_ref, dg_ref, dys_ref, dym_ref):
        d_mg = _nt(dh_ref[...].astype(BF16), w_ref[...])
        (d_g, d_ys, d_ym), _ = _b_merge(g_ref[...], ys_ref[...], ym_ref[...], d_mg)
        dg_ref[...] = d_g.astype(BF16)
        dys_ref[...] = d_ys.astype(BF16)
        dym_ref[...] = d_ym.astype(BF16)

    rows = pl.BlockSpec((tp, D_MODEL), lambda i: (i, 0))
    wide = pl.BlockSpec((tp, 2 * D_MODEL), lambda i: (i, 0))
    return pl.pallas_call(
        body, grid=(s // tp,),
        in_specs=[rows, pl.BlockSpec((D_MODEL, D_MODEL), lambda i: (0, 0)),
                  pl.BlockSpec((tp, 2 * D_MODEL), lambda i: (i, PROJ_GATES // (2 * D_MODEL))), rows, rows],
        out_specs=[wide, rows, rows],
        out_shape=[jax.ShapeDtypeStruct((s, 2 * D_MODEL), BF16), jax.ShapeDtypeStruct((s, D_MODEL), BF16),
                   jax.ShapeDtypeStruct((s, D_MODEL), BF16)],
        compiler_params=_cparams("arbitrary"), name="merge_out_bwd",
    )(dh, w_o, proj, y_ssd, y_mla)


FFN_TP = 512
FFN_TQ = 1408


def _ffn_up_call(n, w13_t):
    s, d = n.shape
    tp = min(FFN_TP, s)
    up0 = D_FF // FFN_TQ

    def body(n_ref, wg_ref, wu_ref, act_ref, gate_ref, up_ref):
        a = n_ref[...]
        g = _nt(a, wg_ref[...])
        u = _nt(a, wu_ref[...])
        act_ref[...] = (g * jax.nn.sigmoid(g) * u).astype(BF16)
        gate_ref[...] = g.astype(BF16)
        up_ref[...] = u.astype(BF16)

    o_spec = pl.BlockSpec((tp, FFN_TQ), lambda j, i: (i, j))
    return pl.pallas_call(
        body, grid=(D_FF // FFN_TQ, s // tp),
        in_specs=[pl.BlockSpec((tp, d), lambda j, i: (i, 0)), pl.BlockSpec((FFN_TQ, d), lambda j, i: (j, 0)),
                  pl.BlockSpec((FFN_TQ, d), lambda j, i: (j + up0, 0))],
        out_specs=[o_spec] * 3, out_shape=[jax.ShapeDtypeStruct((s, D_FF), BF16)] * 3,
        compiler_params=_cparams("arbitrary", "arbitrary"), name="ffn_up_swiglu",
    )(n, w13_t, w13_t)


def _ffn_down_bwd_call(dh, w2, gate, up):
    s, d = dh.shape
    tp = min(FFN_TP, s)

    def body(dh_ref, w2_ref, gate_ref, up_ref, dg_ref, du_ref):
        d_act = 0.5 * _nt(dh_ref[...].astype(BF16), w2_ref[...])
        g, u = gate_ref[...].astype(F32), up_ref[...].astype(F32)
        sg = jax.nn.sigmoid(g)
        dg_ref[...] = (d_act * u * sg * (1.0 + g * (1.0 - sg))).astype(BF16)
        du_ref[...] = (d_act * g * sg).astype(BF16)

    o_spec = pl.BlockSpec((tp, FFN_TQ), lambda j, i: (i, j))
    return pl.pallas_call(
        body, grid=(D_FF // FFN_TQ, s // tp),
        in_specs=[pl.BlockSpec((tp, d), lambda j, i: (i, 0)), pl.BlockSpec((FFN_TQ, d), lambda j, i: (j, 0)), o_spec, o_spec],
        out_specs=[o_spec] * 2, out_shape=[jax.ShapeDtypeStruct((s, D_FF), BF16)] * 2,
        compiler_params=_cparams("arbitrary", "arbitrary"), name="ffn_down_bwd_swiglu",
    )(dh, w2, gate, up)


ATTN_SCALE = QK ** -0.5
LOG2E = 1.4426950408889634
ATTN_C = ATTN_SCALE * LOG2E


ATTN_HEADS = 2


def _attn_tile(s):
    return min(512, s)


def _causal_keep(t, keys_on_rows=False):
    row = lax.broadcasted_iota(jnp.int32, (t, t), 0)
    col = lax.broadcasted_iota(jnp.int32, (t, t), 1)
    return row <= col if keys_on_rows else col <= row


def _nt(a, b):
    return lax.dot_general(a, b, (((1,), (1,)), ((), ())), preferred_element_type=F32)


def _rider_phases(rider, src_ref, out_ref, sems, first, last):
    @pl.when(first)
    def _():
        _peer_copies(src_ref, out_ref, sems, rider["gather"], "start")

    def finish():
        @pl.when(last)
        def _():
            _peer_copies(src_ref, out_ref, sems, rider["gather"], "finish")

    return finish


def _attn_fwd_call(q, k, v, rider=None):
    nh, s, _ = q.shape
    t = _attn_tile(s)
    nb = s // t
    hp = ATTN_HEADS
    n_r = 0 if rider is None else 1

    def body(*refs):
        q_ref, k_ref, v_ref = refs[:3]
        o_ref, lse_ref = refs[3 + n_r:5 + n_r]
        qi = pl.program_id(1)
        finish = None
        if rider is not None:
            h = pl.program_id(0)
            finish = _rider_phases(rider, refs[3:4], refs[5 + n_r], refs[6 + n_r:],
                                   jnp.logical_and(h == 0, qi == 0), jnp.logical_and(h == nh // hp - 1, qi == nb - 1))
        qs = [q_ref[i] for i in range(hp)]

        def block(kb, carries, diagonal, width=1):
            start = pl.multiple_of(kb * t, t)
            out = []
            for i, (m_prev, l_prev, acc) in enumerate(carries):
                sc = _nt(qs[i], k_ref[i, pl.ds(start, width * t), :])
                if diagonal:
                    sc = jnp.where(_causal_keep(t), sc, NEG)
                m_new = jnp.maximum(m_prev, jnp.max(sc, axis=-1, keepdims=True))
                p = jnp.exp2(sc * ATTN_C - m_new * ATTN_C)
                alpha = jnp.exp2((m_prev - m_new) * ATTN_C)
                l_new = alpha * l_prev + jnp.sum(p, axis=-1, keepdims=True)
                pv = jnp.dot(p.astype(BF16), v_ref[i, pl.ds(start, width * t), :], preferred_element_type=F32)
                out.append((m_new, l_new, alpha * acc + pv))
            return tuple(out)

        init = tuple((jnp.full((t, 1), NEG, F32), jnp.zeros((t, 1), F32), jnp.zeros((t, VDIM), F32)) for _ in range(hp))
        carries = lax.fori_loop(0, qi // 2, lambda j, c: block(2 * j, c, False, width=2), init)
        carries = lax.cond(qi % 2 == 1, lambda c: block(qi - 1, c, False), lambda c: c, carries)
        for i, (m, l, acc) in enumerate(block(qi, carries, True)):
            o_ref[i] = (acc / l).astype(o_ref.dtype)
            lse_ref[i] = m * ATTN_SCALE + jnp.log(l)
        if finish is not None:
            finish()

    qmap = lambda h, i: (h, i, 0)
    whole = lambda h, i: (h, 0, 0)
    return pl.pallas_call(
        body, grid=(nh // hp, nb),
        in_specs=[pl.BlockSpec((hp, t, QK), qmap), pl.BlockSpec((hp, s, QK), whole), pl.BlockSpec((hp, s, VDIM), whole)] + [HBM_SPEC] * n_r,
        out_specs=[pl.BlockSpec((hp, t, VDIM), qmap), pl.BlockSpec((hp, t, 1), qmap)] + [HBM_SPEC] * n_r,
        out_shape=[jax.ShapeDtypeStruct((nh, s, VDIM), BF16), jax.ShapeDtypeStruct((nh, s, 1), F32)] + ([rider["out"]] if n_r else []),
        scratch_shapes=_comm_scratch() if n_r else [],
        compiler_params=_cparams("arbitrary", "arbitrary"), name="attn_fwd_gather" if n_r else "attn_fwd",
    )(*([q, k, v] + (rider["srcs"] if n_r else [])))


def _attn_delta_call(o, do):
    nh, s, d = o.shape

    def fn(ov, dv):
        return (jnp.sum(ov.astype(F32) * dv.astype(F32), axis=-1, keepdims=True),), ()

    return _rowwise_call(fn, [o.reshape(nh * s, d), do.reshape(nh * s, d)], [], [(1, F32)], [], "attn_delta")[0]


def _attn_bwd_call(q, k, v, do, lse_t, delta_t, rider=None):
    nh, s, _ = q.shape
    t = _attn_tile(s)
    nb = s // t
    hp = ATTN_HEADS
    n_src = 0 if rider is None else len(rider["srcs"])
    n_r = 0 if rider is None else 1

    def body(*refs):
        q_ref, k_ref, v_ref, do_ref, lse_ref, delta_ref = refs[:6]
        dq_ref, dk_ref, dv_ref = refs[6 + n_src:9 + n_src]
        dk_sc, dv_sc = refs[9 + n_src + n_r:11 + n_src + n_r]
        kj = pl.program_id(1)
        finish = None
        if rider is not None:
            h = pl.program_id(0)
            finish = _rider_phases(rider, refs[6:6 + n_src], refs[9 + n_src], refs[11 + n_src + n_r:],
                                   jnp.logical_and(h == 0, kj == 0), jnp.logical_and(h == nh // hp - 1, kj == nb - 1))

        @pl.when(kj == 0)
        def _():
            dq_ref[...] = jnp.zeros_like(dq_ref)

        dk_sc[...] = jnp.zeros_like(dk_sc)
        dv_sc[...] = jnp.zeros_like(dv_sc)
        kblks = [k_ref[i] for i in range(hp)]
        vblks = [v_ref[i] for i in range(hp)]

        def block(qb, diagonal):
            start = pl.multiple_of(qb * t, t)
            for i in range(hp):
                qblk = q_ref[i, pl.ds(start, t), :]
                doblk = do_ref[i, pl.ds(start, t), :]
                sc = _nt(kblks[i], qblk)
                if diagonal:
                    sc = jnp.where(_causal_keep(t, keys_on_rows=True), sc, NEG)
                p = jnp.exp2(sc * ATTN_C - lse_ref[i, :, pl.ds(start, t)] * LOG2E)
                dv_sc[i] += jnp.dot(p.astype(BF16), doblk, preferred_element_type=F32)
                dp = _nt(vblks[i], doblk)
                ds = (p * (dp - delta_ref[i, :, pl.ds(start, t)])).astype(BF16)
                dk_sc[i] += jnp.dot(ds, qblk, preferred_element_type=F32)
                dq_ref[i, pl.ds(start, t), :] += lax.dot_general(ds, kblks[i], (((0,), (0,)), ((), ())), preferred_element_type=F32)

        block(kj, True)

        def rest(qb, carry):
            block(qb, False)
            return carry

        lax.fori_loop(kj + 1, nb, rest, 0)
        dk_ref[...] = (dk_sc[...] * ATTN_SCALE).astype(dk_ref.dtype)
        dv_ref[...] = dv_sc[...].astype(dv_ref.dtype)

        @pl.when(kj == nb - 1)
        def _():
            dq_ref[...] = dq_ref[...] * ATTN_SCALE

        if finish is not None:
            finish()

    kmap = lambda h, j: (h, j, 0)
    whole = lambda h, j: (h, 0, 0)
    once = pl.Buffered(buffer_count=1)
    return pl.pallas_call(
        body, grid=(nh // hp, nb),
        in_specs=[pl.BlockSpec((hp, s, QK), whole, pipeline_mode=once), pl.BlockSpec((hp, t, QK), kmap), pl.BlockSpec((hp, t, VDIM), kmap),
                  pl.BlockSpec((hp, s, VDIM), whole, pipeline_mode=once), pl.BlockSpec((hp, 1, s), whole, pipeline_mode=once),
                  pl.BlockSpec((hp, 1, s), whole, pipeline_mode=once)] + [HBM_SPEC] * n_src,
        out_specs=[pl.BlockSpec((hp, s, QK), whole, pipeline_mode=once), pl.BlockSpec((hp, t, QK), kmap),
                   pl.BlockSpec((hp, t, VDIM), kmap)] + [HBM_SPEC] * n_r,
        out_shape=[jax.ShapeDtypeStruct((nh, s, QK), F32), jax.ShapeDtypeStruct((nh, s, QK), F32),
                   jax.ShapeDtypeStruct((nh, s, VDIM), F32)] + ([rider["out"]] if n_r else []),
        scratch_shapes=[pltpu.VMEM((hp, t, QK), F32), pltpu.VMEM((hp, t, VDIM), F32)] + (_comm_scratch() if n_r else []),
        compiler_params=_cparams("arbitrary", "arbitrary"), name="attn_bwd_exchange" if n_r else "attn_bwd",
    )(*([q, k, v, do, lse_t, delta_t] + (rider["srcs"] if n_r else [])))


HEAD_COLS = NOPE + VDIM
HEADS_TILE = 256


def _swap_rope_halves(t, lane):
    half = ROPE // 2
    return jnp.where(lane < half, pltpu.roll(t, LANE - half, 1), pltpu.roll(t, half, 1))


def _head_fwd(n, p, gain, cs, sn, lane):
    r = lax.rsqrt((jnp.sum(n * n, axis=-1, keepdims=True) + jnp.sum(p * p, axis=-1, keepdims=True)) * (1.0 / QK) + EPS)
    yp = p * r * gain[:, NOPE:]
    return n * r * gain[:, :NOPE], yp * cs + _swap_rope_halves(yp, lane) * sn


def _head_bwd(n, p, gain, cs, sn, lane, dzn, dzp):
    r = lax.rsqrt((jnp.sum(n * n, axis=-1, keepdims=True) + jnp.sum(p * p, axis=-1, keepdims=True)) * (1.0 / QK) + EPS)
    dyp = dzp * cs + _swap_rope_halves(dzp * sn, lane)
    gyn, gyp = dzn * gain[:, :NOPE], dyp * gain[:, NOPE:]
    dot = jnp.sum(gyn * n, axis=-1, keepdims=True) + jnp.sum(gyp * p, axis=-1, keepdims=True)
    coef = dot * (r * r * r) * (1.0 / QK)
    d_gn = jnp.sum(dzn * n * r, axis=0, keepdims=True)
    d_gp = jnp.sum(dyp * p * r, axis=0, keepdims=True)
    return gyn * r - n * coef, gyp * r - p * coef, d_gn, d_gp


def _heads_fwd_call(q, kv, proj, cs, sn, q_gain, k_gain):
    s = q.shape[0]
    t = min(HEADS_TILE, s)

    def body(q_ref, kv_ref, last_ref, cs_ref, sn_ref, qg_ref, kg_ref, qh_ref, kh_ref, vh_ref):
        lane = lax.broadcasted_iota(jnp.int32, (t, LANE), 1)
        cs_, sn_ = cs_ref[...], sn_ref[...]
        kp = jnp.where(lane < ROPE, last_ref[...], 0.0)
        for h in range(MLA_H):
            c0 = h * HEAD_COLS
            zn, zp = _head_fwd(q_ref[:, c0:c0 + NOPE], q_ref[:, c0 + NOPE:c0 + HEAD_COLS], qg_ref[...], cs_, sn_, lane)
            qh_ref[h, :, :NOPE] = zn.astype(BF16)
            qh_ref[h, :, NOPE:] = zp[:, :ROPE].astype(BF16)
            zn, zp = _head_fwd(kv_ref[:, c0:c0 + NOPE], kp, kg_ref[...], cs_, sn_, lane)
            kh_ref[h, :, :NOPE] = zn.astype(BF16)
            kh_ref[h, :, NOPE:] = zp[:, :ROPE].astype(BF16)
            vh_ref[h] = kv_ref[:, c0 + NOPE:c0 + HEAD_COLS].astype(BF16)

    rows = lambda i: (i, 0)
    whole = lambda i: (0, 0)
    heads = lambda i: (0, i, 0)
    wide = MLA_H * HEAD_COLS
    return pl.pallas_call(
        body, grid=(s // t,),
        in_specs=[pl.BlockSpec((t, wide), rows), pl.BlockSpec((t, wide), rows),
                  pl.BlockSpec((t, LANE), lambda i: (i, PROJ_LAST // LANE)),
                  pl.BlockSpec((t, LANE), rows), pl.BlockSpec((t, LANE), rows),
                  pl.BlockSpec((1, HEAD_COLS), whole), pl.BlockSpec((1, HEAD_COLS), whole)],
        out_specs=[pl.BlockSpec((MLA_H, t, QK), heads), pl.BlockSpec((MLA_H, t, QK), heads), pl.BlockSpec((MLA_H, t, VDIM), heads)],
        out_shape=[jax.ShapeDtypeStruct((MLA_H, s, QK), BF16), jax.ShapeDtypeStruct((MLA_H, s, QK), BF16),
                   jax.ShapeDtypeStruct((MLA_H, s, VDIM), BF16)],
        compiler_params=_cparams("arbitrary"), name="mla_heads_fwd",
    )(q, kv, proj, cs, sn, q_gain, k_gain)


def _heads_bwd_call(q, kv, proj, cs, sn, q_gain, k_gain, dqh, dkh, dvh):
    s = q.shape[0]
    t = min(HEADS_TILE, s)

    def body(q_ref, kv_ref, last_ref, cs_ref, sn_ref, qg_ref, kg_ref, dqh_ref, dkh_ref, dvh_ref,
             dq_ref, dkv_ref, dkr_ref, dqg_ref, dkg_ref):
        lane = lax.broadcasted_iota(jnp.int32, (t, LANE), 1)
        cs_, sn_ = cs_ref[...], sn_ref[...]
        kp = jnp.where(lane < ROPE, last_ref[...], 0.0)
        no_lanes = jnp.zeros((t, LANE - ROPE), F32)
        d_kp = jnp.zeros((t, LANE), F32)
        d_qg = [jnp.zeros((1, NOPE), F32), jnp.zeros((1, LANE), F32)]
        d_kg = [jnp.zeros((1, NOPE), F32), jnp.zeros((1, LANE), F32)]
        for h in range(MLA_H):
            c0 = h * HEAD_COLS
            dz = dqh_ref[h]
            dzp = jnp.concatenate([dz[:, NOPE:], no_lanes], axis=1)
            d_n, d_p, g_n, g_p = _head_bwd(q_ref[:, c0:c0 + NOPE], q_ref[:, c0 + NOPE:c0 + HEAD_COLS], qg_ref[...],
                                           cs_, sn_, lane, dz[:, :NOPE], dzp)
            dq_ref[:, c0:c0 + NOPE] = d_n.astype(dq_ref.dtype)
            dq_ref[:, c0 + NOPE:c0 + HEAD_COLS] = d_p.astype(dq_ref.dtype)
            d_qg = [d_qg[0] + g_n, d_qg[1] + g_p]
            dz = dkh_ref[h]
            dzp = jnp.concatenate([dz[:, NOPE:], no_lanes], axis=1)
            d_n, d_p, g_n, g_p = _head_bwd(kv_ref[:, c0:c0 + NOPE], kp, kg_ref[...], cs_, sn_, lane, dz[:, :NOPE], dzp)
            dkv_ref[:, c0:c0 + NOPE] = d_n.astype(dkv_ref.dtype)
            dkv_ref[:, c0 + NOPE:c0 + HEAD_COLS] = dvh_ref[h].astype(dkv_ref.dtype)
            d_kp = d_kp + d_p
            d_kg = [d_kg[0] + g_n, d_kg[1] + g_p]
        dkr_ref[...] = d_kp
        first = pl.program_id(0) == 0
        _acc_store(dqg_ref.at[:, pl.ds(0, NOPE)], d_qg[0], first)
        _acc_store(dqg_ref.at[:, pl.ds(NOPE, LANE)], d_qg[1], first)
        _acc_store(dkg_ref.at[:, pl.ds(0, NOPE)], d_kg[0], first)
        _acc_store(dkg_ref.at[:, pl.ds(NOPE, LANE)], d_kg[1], first)

    rows = lambda i: (i, 0)
    whole = lambda i: (0, 0)
    heads = lambda i: (0, i, 0)
    wide = MLA_H * HEAD_COLS
    return pl.pallas_call(
        body, grid=(s // t,),
        in_specs=[pl.BlockSpec((t, wide), rows), pl.BlockSpec((t, wide), rows),
                  pl.BlockSpec((t, LANE), lambda i: (i, PROJ_LAST // LANE)),
                  pl.BlockSpec((t, LANE), rows), pl.BlockSpec((t, LANE), rows),
                  pl.BlockSpec((1, HEAD_COLS), whole), pl.BlockSpec((1, HEAD_COLS), whole),
                  pl.BlockSpec((MLA_H, t, QK), heads), pl.BlockSpec((MLA_H, t, QK), heads), pl.BlockSpec((MLA_H, t, VDIM), heads)],
        out_specs=[pl.BlockSpec((t, wide), rows), pl.BlockSpec((t, wide), rows), pl.BlockSpec((t, LANE), rows),
                   pl.BlockSpec((1, HEAD_COLS), whole), pl.BlockSpec((1, HEAD_COLS), whole)],
        out_shape=[jax.ShapeDtypeStruct((s, wide), BF16), jax.ShapeDtypeStruct((s, wide), BF16), jax.ShapeDtypeStruct((s, LANE), F32),
                   jax.ShapeDtypeStruct((1, HEAD_COLS), F32), jax.ShapeDtypeStruct((1, HEAD_COLS), F32)],
        compiler_params=_cparams("arbitrary"), name="mla_heads_bwd",
    )(q, kv, proj, cs, sn, q_gain, k_gain, dqh, dkh, dvh)


CONV_TC = 512
HALO = 8


def _conv_tiles(s):
    return min(512, s)


def _conv_fwd_call(x, col0, w, b):
    s = x.shape[0]
    ts = _conv_tiles(s)
    hb = ts // HALO
    c0 = col0 // CONV_TC
    assert col0 % CONV_TC == 0

    def body(x_ref, prev_ref, w_ref, b_ref, y_ref, buf):
        si = pl.program_id(1)
        buf[0:HALO, :] = jnp.where(si > 0, prev_ref[...], 0.0)
        buf[HALO:, :] = x_ref[...]
        acc = jnp.broadcast_to(b_ref[...], (ts, CONV_TC))
        for k in range(CONV_K):
            acc = acc + w_ref[k:k + 1, :] * buf[pl.ds(HALO - (CONV_K - 1) + k, ts), :]
        y_ref[...] = acc * jax.nn.sigmoid(acc)

    return pl.pallas_call(
        body, grid=(CONV_DIM // CONV_TC, s // ts),
        in_specs=[pl.BlockSpec((ts, CONV_TC), lambda ci, si: (si, ci + c0)),
                  pl.BlockSpec((HALO, CONV_TC), lambda ci, si: (jnp.maximum(si * hb - 1, 0), ci + c0)),
                  pl.BlockSpec((CONV_K, CONV_TC), lambda ci, si: (0, ci)),
                  pl.BlockSpec((1, CONV_TC), lambda ci, si: (0, ci))],
        out_specs=pl.BlockSpec((ts, CONV_TC), lambda ci, si: (si, ci)),
        out_shape=jax.ShapeDtypeStruct((s, CONV_DIM), F32),
        scratch_shapes=[pltpu.VMEM((ts + HALO, CONV_TC), F32)],
        compiler_params=_cparams("arbitrary", "arbitrary"), name="conv_fwd",
    )(x, x, w, b)


def _conv_bwd_call(x, col0, w, b, dy):
    s = x.shape[0]
    ts = _conv_tiles(s)
    hb = ts // HALO
    ns = s // ts
    last_halo = s // HALO - 1
    c0 = col0 // CONV_TC

    def body(x_ref, prev_ref, next_ref, dy_ref, dyn_ref, w_ref, b_ref, dx_ref, dw_ref, db_ref, xbuf, dbuf):
        si = pl.program_id(1)
        xbuf[0:HALO, :] = jnp.where(si > 0, prev_ref[...], 0.0)
        xbuf[HALO:HALO + ts, :] = x_ref[...]
        xbuf[HALO + ts:, :] = next_ref[...]
        pre = jnp.broadcast_to(b_ref[...], (ts + HALO, CONV_TC))
        for k in range(CONV_K):
            pre = pre + w_ref[k:k + 1, :] * xbuf[pl.ds(HALO - (CONV_K - 1) + k, ts + HALO), :]
        sg = jax.nn.sigmoid(pre)
        dsilu = sg * (1.0 + pre * (1.0 - sg))
        dbuf[0:ts, :] = dy_ref[...] * dsilu[0:ts]
        dbuf[ts:, :] = jnp.where(si < ns - 1, dyn_ref[...] * dsilu[ts:], 0.0)
        dx = jnp.zeros((ts, CONV_TC), F32)
        for k in range(CONV_K):
            dx = dx + w_ref[k:k + 1, :] * dbuf[pl.ds(CONV_K - 1 - k, ts), :]
        dx_ref[...] = dx.astype(dx_ref.dtype)
        dpre = dbuf[0:ts, :]
        first = si == 0
        _acc_store(db_ref, jnp.sum(dpre, axis=0, keepdims=True), first)
        for k in range(CONV_K):
            dw_k = jnp.sum(dpre * xbuf[pl.ds(HALO - (CONV_K - 1) + k, ts), :], axis=0, keepdims=True)
            _acc_store(dw_ref.at[pl.ds(k, 1), :], dw_k, first)

    main = lambda ci, si: (si, ci)
    x_main = lambda ci, si: (si, ci + c0)
    x_prev = lambda ci, si: (jnp.maximum(si * hb - 1, 0), ci + c0)
    x_next = lambda ci, si: (jnp.minimum(si * hb + hb, last_halo), ci + c0)
    return pl.pallas_call(
        body, grid=(CONV_DIM // CONV_TC, ns),
        in_specs=[pl.BlockSpec((ts, CONV_TC), x_main), pl.BlockSpec((HALO, CONV_TC), x_prev), pl.BlockSpec((HALO, CONV_TC), x_next),
                  pl.BlockSpec((ts, CONV_TC), main),
                  pl.BlockSpec((HALO, CONV_TC), lambda ci, si: (jnp.minimum(si * hb + hb, last_halo), ci)),
                  pl.BlockSpec((CONV_K, CONV_TC), lambda ci, si: (0, ci)),
                  pl.BlockSpec((1, CONV_TC), lambda ci, si: (0, ci))],
        out_specs=[pl.BlockSpec((ts, CONV_TC), main),
                   pl.BlockSpec((CONV_K, CONV_TC), lambda ci, si: (0, ci)),
                   pl.BlockSpec((1, CONV_TC), lambda ci, si: (0, ci))],
        out_shape=[jax.ShapeDtypeStruct((s, CONV_DIM), BF16), jax.ShapeDtypeStruct((CONV_K, CONV_DIM), F32),
                   jax.ShapeDtypeStruct((1, CONV_DIM), F32)],
        scratch_shapes=[pltpu.VMEM((ts + 2 * HALO, CONV_TC), F32), pltpu.VMEM((ts + HALO, CONV_TC), F32)],
        compiler_params=_cparams("arbitrary", "arbitrary"), name="conv_bwd",
    )(x, x, x, dy, dy, w, b)


GW = SSD_HPG * SSD_P
B_COL = SSD_DI
C_COL = SSD_DI + SSD_G * SSD_N


def _ones_where(mask):
    return jnp.where(mask, 1.0, 0.0).astype(BF16)


def _split(v, passes):
    parts, rest = [], v
    for i in range(passes):
        part = rest.astype(BF16)
        parts.append(part)
        if i + 1 < passes:
            rest = rest - part.astype(F32)
    return parts


def _dot_sel_r(v, sel, passes=3):
    out = None
    for part in _split(v, passes):
        t = jnp.dot(part, sel, preferred_element_type=F32)
        out = t if out is None else out + t
    return out


def _dot_sel_l(sel, v, passes=3):
    out = None
    for part in _split(v, passes):
        t = jnp.dot(sel, part, preferred_element_type=F32)
        out = t if out is None else out + t
    return out


def _ssd_consts():
    r = lax.broadcasted_iota(jnp.int32, (SSD_L, SSD_L), 0)
    c = lax.broadcasted_iota(jnp.int32, (SSD_L, SSD_L), 1)
    tril = r >= c
    triu = c >= r
    shift = SSD_P.bit_length() - 1
    eh = lax.broadcasted_iota(jnp.int32, (SSD_H, SSD_DI), 0)
    ej = lax.broadcasted_iota(jnp.int32, (SSD_H, SSD_DI), 1)
    expand = _ones_where(lax.shift_right_logical(ej, shift) == eh)
    rj = lax.broadcasted_iota(jnp.int32, (SSD_DI, SSD_H), 0)
    rh = lax.broadcasted_iota(jnp.int32, (SSD_DI, SSD_H), 1)
    reduce_ = _ones_where(lax.shift_right_logical(rj, shift) == rh)
    lane = lax.broadcasted_iota(jnp.int32, (SSD_L, LANE), 1)
    return tril, triu, expand, reduce_, lane < SSD_P


def _ssd_decays(dt, dt_t, a, a_t, tril, triu, expand):
    dta = dt * a
    acum = _dot_sel_l(_ones_where(tril), dta)
    acum_t = _dot_sel_r(dt_t * a_t, _ones_where(triu))
    dta_e = _dot_sel_r(dta, expand)
    acum_e = _dot_sel_r(acum, expand)
    last_e = jnp.sum(dta_e, axis=0, keepdims=True)
    return acum, acum_t, acum_e, last_e


def _head_decay(acum, acum_t, h, tril):
    seg = acum[:, h:h + 1] - acum_t[h:h + 1, :]
    return jnp.exp(jnp.where(tril, seg, NEG))


def _ssd_fwd_call(xbc, dt, a):
    s = xbc.shape[0]
    nc = s // SSD_L
    dt_t = dt.T
    a_t = a.T

    def body(xbc_ref, dt_ref, dtt_ref, a_ref, at_ref, y_ref, st_ref, s_sc):
        ci = pl.program_id(0)

        @pl.when(ci == 0)
        def _():
            s_sc[...] = jnp.zeros_like(s_sc)

        st_ref[0] = s_sc[...]
        tril, triu, expand, _, low_half = _ssd_consts()
        acum, acum_t, acum_e, last_e = _ssd_decays(dt_ref[...], dtt_ref[...], a_ref[...], at_ref[...], tril, triu, expand)
        dt_e = _dot_sel_r(dt_ref[...], expand, passes=2)
        xdt = xbc_ref[:, :SSD_DI] * dt_e
        xdt_b = xdt.astype(BF16)
        xw_b = (xdt * jnp.exp(last_e - acum_e)).astype(BF16)
        ea_e = jnp.exp(acum_e)
        el_e = jnp.exp(last_e)
        for g in range(SSD_G):
            gs = slice(g * GW, (g + 1) * GW)
            bg = xbc_ref[:, B_COL + g * SSD_N:B_COL + (g + 1) * SSD_N]
            cg_b = xbc_ref[:, C_COL + g * SSD_N:C_COL + (g + 1) * SSD_N].astype(BF16)
            bg_b = bg.astype(BF16)
            cb = _nt(cg_b, bg_b)
            st = s_sc[:, gs]
            y_off = jnp.dot(cg_b, st.astype(BF16), preferred_element_type=F32) * ea_e[:, gs]
            for pr in range(SSD_HPG // 2):
                ls = slice(g * GW + pr * LANE, g * GW + (pr + 1) * LANE)
                xp = xdt_b[:, ls]
                yd = []
                for half in range(2):
                    h = g * SSD_HPG + pr * 2 + half
                    m = (cb * _head_decay(acum, acum_t, h, tril)).astype(BF16)
                    yd.append(jnp.dot(m, xp, preferred_element_type=F32))
                y_ref[:, ls] = jnp.where(low_half, yd[0], yd[1]) + y_off[:, pr * LANE:(pr + 1) * LANE]
            s_sc[:, gs] = st * el_e[:, gs] + jnp.dot(bg.T.astype(BF16), xw_b[:, gs], preferred_element_type=F32)

    row = lambda i: (i, 0)
    return pl.pallas_call(
        body, grid=(nc,),
        in_specs=[pl.BlockSpec((SSD_L, CONV_DIM), row), pl.BlockSpec((SSD_L, SSD_H), row),
                  pl.BlockSpec((SSD_H, SSD_L), lambda i: (0, i)), pl.BlockSpec((1, SSD_H), lambda i: (0, 0)),
                  pl.BlockSpec((SSD_H, 1), lambda i: (0, 0))],
        out_specs=[pl.BlockSpec((SSD_L, SSD_DI), row), pl.BlockSpec((1, SSD_N, SSD_DI), lambda i: (i, 0, 0))],
        out_shape=[jax.ShapeDtypeStruct((s, SSD_DI), F32), jax.ShapeDtypeStruct((nc, SSD_N, SSD_DI), F32)],
        scratch_shapes=[pltpu.VMEM((SSD_N, SSD_DI), F32)],
        compiler_params=_cparams("arbitrary"), name="ssd_fwd",
    )(xbc, dt, dt_t, a, a_t)


def _ssd_bwd_call(xbc, dt, a, states, dy, dx_extra):
    s = xbc.shape[0]
    nc = s // SSD_L
    dt_t = dt.T
    a_t = a.T

    def body(xbc_ref, dt_ref, dtt_ref, a_ref, at_ref, st_ref, dy_ref, dxe_ref,
             dxbc_ref, ddt_ref, da_ref, ds_sc, yf_sc, dxd_sc, dxw_sc):
        i = pl.program_id(0)

        @pl.when(i == 0)
        def _():
            ds_sc[...] = jnp.zeros_like(ds_sc)

        tril, triu, expand, reduce_, low_half = _ssd_consts()
        dt = dt_ref[...]
        a_row = a_ref[...]
        acum, acum_t, acum_e, last_e = _ssd_decays(dt, dtt_ref[...], a_row, at_ref[...], tril, triu, expand)
        dt_e = _dot_sel_r(dt, expand, passes=2)
        x = xbc_ref[:, :SSD_DI]
        xdt = x * dt_e
        xdt_b = xdt.astype(BF16)
        w_e = jnp.exp(last_e - acum_e)
        xw_b = (xdt * w_e).astype(BF16)
        ea_e = jnp.exp(acum_e)
        el_e = jnp.exp(last_e)
        dy = dy_ref[...]
        dy_b = dy.astype(BF16)
        s_prev = st_ref[0]
        ds_new = ds_sc[...]
        ds_new_b = ds_new.astype(BF16)
        triu_b = _ones_where(triu)
        strict_tril = jnp.logical_not(triu)
        head_ids = lax.broadcasted_iota(jnp.int32, (1, SSD_H), 1)
        d_dta_diag = jnp.zeros((SSD_L, SSD_H), F32)
        for g in range(SSD_G):
            gs = slice(g * GW, (g + 1) * GW)
            bs_ = slice(B_COL + g * SSD_N, B_COL + (g + 1) * SSD_N)
            cs_ = slice(C_COL + g * SSD_N, C_COL + (g + 1) * SSD_N)
            bg = xbc_ref[:, bs_]
            cg = xbc_ref[:, cs_]
            bg_b, cg_b = bg.astype(BF16), cg.astype(BF16)
            st_b = s_prev[:, gs].astype(BF16)
            y_off = jnp.dot(cg_b, st_b, preferred_element_type=F32) * ea_e[:, gs]
            yf_sc[:, gs] = y_off
            dz_b = (dy[:, gs] * ea_e[:, gs]).astype(BF16)
            d_c = _nt(dz_b, st_b)
            ds_prev = ds_new[:, gs] * el_e[:, gs] + jnp.dot(cg.T.astype(BF16), dz_b, preferred_element_type=F32)
            dxw_sc[:, gs] = jnp.dot(bg_b, ds_new_b[:, gs], preferred_element_type=F32)
            d_b = _nt(xw_b[:, gs], ds_new_b[:, gs])
            cb = _nt(cg_b, bg_b)
            d_g = jnp.zeros((SSD_L, SSD_L), F32)
            for pr in range(SSD_HPG // 2):
                ls = slice(g * GW + pr * LANE, g * GW + (pr + 1) * LANE)
                xp = xdt_b[:, ls]
                dyp = dy[:, ls]
                dyp_b = dy_b[:, ls]
                dxd = []
                for half in range(2):
                    h = g * SSD_HPG + pr * 2 + half
                    dec = _head_decay(acum, acum_t, h, tril)
                    m = cb * dec
                    dxd.append(jnp.dot(m.T.astype(BF16), dyp_b, preferred_element_type=F32))
                    mine = low_half if half == 0 else jnp.logical_not(low_half)
                    d_m = _nt(jnp.where(mine, dyp, 0.0).astype(BF16), xp)
                    d_g = d_g + d_m * dec
                    below = jnp.dot(triu_b, (d_m * m).astype(BF16), preferred_element_type=F32)
                    col = jnp.sum(jnp.where(strict_tril, below, 0.0), axis=1, keepdims=True)
                    d_dta_diag = d_dta_diag + col * jnp.where(head_ids == h, 1.0, 0.0)
                dxd_sc[:, ls] = jnp.where(low_half, dxd[0], dxd[1])
            d_g_b = d_g.astype(BF16)
            dxbc_ref[:, cs_] = d_c + jnp.dot(d_g_b, bg_b, preferred_element_type=F32)
            dxbc_ref[:, bs_] = d_b + jnp.dot(d_g.T.astype(BF16), cg_b, preferred_element_type=F32)
            ds_sc[:, gs] = ds_prev
        dxw = dxw_sc[...]
        dxd = dxd_sc[...]
        dw_e = xdt * dxw * w_e
        d_out = _dot_sel_r(dy * yf_sc[...], reduce_, passes=2)
        d_upd = _dot_sel_r(dw_e, reduce_, passes=2)
        d_tot_e = jnp.sum(ds_new * s_prev, axis=0, keepdims=True) * el_e
        d_tot = _dot_sel_r(jnp.broadcast_to(d_tot_e, (8, SSD_DI)), reduce_, passes=2)[0:1]
        d_dta = _dot_sel_l(triu_b, d_out) + _dot_sel_l(_ones_where(strict_tril), d_upd) + d_tot + d_dta_diag
        dxdt = dxd + dxw * w_e
        dxbc_ref[:, :SSD_DI] = dxdt * dt_e + dxe_ref[...]
        ddt_ref[...] = d_dta * a_row + _dot_sel_r(dxdt * x, reduce_, passes=2)
        _acc_store(da_ref, jnp.sum(d_dta * dt, axis=0, keepdims=True), i == 0)

    rev = lambda i: (nc - 1 - i, 0)
    return pl.pallas_call(
        body, grid=(nc,),
        in_specs=[pl.BlockSpec((SSD_L, CONV_DIM), rev), pl.BlockSpec((SSD_L, SSD_H), rev),
                  pl.BlockSpec((SSD_H, SSD_L), lambda i: (0, nc - 1 - i)), pl.BlockSpec((1, SSD_H), lambda i: (0, 0)),
                  pl.BlockSpec((SSD_H, 1), lambda i: (0, 0)),
                  pl.BlockSpec((1, SSD_N, SSD_DI), lambda i: (nc - 1 - i, 0, 0)),
                  pl.BlockSpec((SSD_L, SSD_DI), rev), pl.BlockSpec((SSD_L, SSD_DI), rev)],
        out_specs=[pl.BlockSpec((SSD_L, CONV_DIM), rev), pl.BlockSpec((SSD_L, SSD_H), rev),
                   pl.BlockSpec((1, SSD_H), lambda i: (0, 0))],
        out_shape=[jax.ShapeDtypeStruct((s, CONV_DIM), F32), jax.ShapeDtypeStruct((s, SSD_H), F32),
                   jax.ShapeDtypeStruct((1, SSD_H), F32)],
        scratch_shapes=[pltpu.VMEM((SSD_N, SSD_DI), F32), pltpu.VMEM((SSD_L, SSD_DI), F32),
                        pltpu.VMEM((SSD_L, SSD_DI), F32), pltpu.VMEM((SSD_L, SSD_DI), F32)],
        compiler_params=_cparams("arbitrary"), name="ssd_bwd",
    )(xbc, dt, dt_t, a, a_t, states, dy, dx_extra)


HBM_SPEC = pl.BlockSpec(memory_space=pltpu.HBM)
N_PEERS = N_DEV - 1


def _flip(v, f):
    return 1 - v if f else v


def _all_gather(shard):
    rows, c = shard.shape

    def body(x_ref, out_ref, send_sems, recv_sems, local_sem):
        x, y, cc = lax.axis_index("x"), lax.axis_index("y"), lax.axis_index("c")
        me, sibling = (x, y, cc), (x, y, 1 - cc)
        chips = [(1 - x, y), (x, 1 - y), (1 - x, 1 - y)]

        def slot(px, py, pc):
            return out_ref.at[4 * px + 2 * py + pc]

        def copy(k, block, to, src=None):
            return pltpu.make_async_remote_copy(
                src_ref=slot(*block) if src is None else src, dst_ref=slot(*block),
                send_sem=send_sems.at[k], recv_sem=recv_sems.at[k],
                device_id=to, device_id_type=pl.DeviceIdType.MESH)

        mine = pltpu.make_async_copy(x_ref, slot(*me), local_sem)
        mine.start()
        first = [copy(0, me, sibling, src=x_ref)]
        first += [copy(1 + j, me, (*chip, cc), src=x_ref) for j, chip in enumerate(chips)]
        for cp in first:
            cp.start()
        passed = [copy(4 + j, (*chip, cc), sibling) for j, chip in enumerate(chips)]
        for j, chip in enumerate(chips):
            copy(1 + j, (*chip, cc), me).wait_recv()
            passed[j].start()
        copy(0, sibling, me).wait_recv()
        for j, chip in enumerate(chips):
            copy(4 + j, (*chip, 1 - cc), me).wait_recv()
        for cp in first + passed:
            cp.wait_send()
        mine.wait()

    return pl.pallas_call(
        body, out_shape=jax.ShapeDtypeStruct((N_DEV, rows, c), shard.dtype),
        in_specs=[HBM_SPEC], out_specs=HBM_SPEC,
        scratch_shapes=[pltpu.SemaphoreType.DMA((N_PEERS,)), pltpu.SemaphoreType.DMA((N_PEERS,)), pltpu.SemaphoreType.DMA(())],
        name="all_gather",
    )(shard)


def _peer_copies(src_refs, out_ref, sems, gather, phase):
    send_sems, recv_sems, local_sem = sems
    x, y, cc = lax.axis_index("x"), lax.axis_index("y"), lax.axis_index("c")
    me = 4 * x + 2 * y + cc

    def pieces(block, slot):
        if gather:
            return [(src_refs[0], out_ref.at[slot])]
        out, r0 = [], 0
        for src in src_refs:
            out.append((src.at[block], out_ref.at[slot, pl.ds(r0, src.shape[1])]))
            r0 += src.shape[1]
        assert r0 == out_ref.shape[1], (r0, out_ref.shape)
        return out

    if phase == "start":
        for src, dst in pieces(me, me):
            pltpu.make_async_copy(src, dst, local_sem).start()
    for k in range(1, N_DEV):
        px, py, pc = _flip(x, k & 4), _flip(y, k & 2), _flip(cc, k & 1)
        peer = 4 * px + 2 * py + pc
        to_peer = dict(send_sem=send_sems.at[k - 1], recv_sem=recv_sems.at[k - 1],
                       device_id=(px, py, pc), device_id_type=pl.DeviceIdType.MESH)
        if phase == "start":
            for src, dst in pieces(peer, me):
                pltpu.make_async_remote_copy(src_ref=src, dst_ref=dst, **to_peer).start()
        else:
            whole = pltpu.make_async_remote_copy(src_ref=out_ref.at[peer], dst_ref=out_ref.at[peer], **to_peer)
            whole.wait_recv()
            whole.wait_send()
    if phase != "start":
        pltpu.make_async_copy(out_ref.at[me], out_ref.at[me], local_sem).wait()


def _comm_scratch():
    return [pltpu.SemaphoreType.DMA((N_PEERS,)), pltpu.SemaphoreType.DMA((N_PEERS,)), pltpu.SemaphoreType.DMA(())]


def _gather_rider(shard):
    return dict(srcs=[shard], out=jax.ShapeDtypeStruct((N_DEV,) + shard.shape, shard.dtype), gather=True)


def _exchange_out(parts):
    rows = sum(p.shape[1] for p in parts)
    return jax.ShapeDtypeStruct((N_DEV, rows) + parts[0].shape[2:], parts[0].dtype)


def _exchange_rider(parts):
    return dict(srcs=list(parts), out=_exchange_out(parts), gather=False)


def _exchange_blocks(parts):
    n = len(parts)

    def body(*refs):
        _peer_copies(refs[:n], refs[n], refs[n + 1:], False, "start")
        _peer_copies(refs[:n], refs[n], refs[n + 1:], False, "finish")

    return pl.pallas_call(
        body, out_shape=_exchange_out(parts),
        in_specs=[HBM_SPEC] * n, out_specs=HBM_SPEC, scratch_shapes=_comm_scratch(), name="exchange_blocks",
    )(*parts)


BIG = [
    ("ffn1_w13", (D_MODEL, 2 * D_FF), 1), ("ffn1_w2", (D_FF, D_MODEL), 0),
    ("w_ssd_out", (SSD_DI, D_MODEL), 0), ("w_uq", (Q_LORA, MLA_H * QK), 1), ("w_ukv", (KV_LORA, MLA_H * (NOPE + VDIM)), 1),
    ("w_mla_out", (MLA_H * VDIM, D_MODEL), 0), ("w_o", (D_MODEL, D_MODEL), 0),
    ("ffn2_w13", (D_MODEL, 2 * D_FF), 1), ("ffn2_w2", (D_FF, D_MODEL), 0), ("w_in", (D_MODEL, D_IN), 1),
]
assert all(_r % 16 == 0 for _r in [_f[0] * _f[1] // N_DEV // PACK_COLS for _, _f, _ in BIG[:-1]])
SMALL = [
    ("ln_ffn1", D_MODEL), ("ln_mix", D_MODEL), ("conv_b", CONV_DIM), ("dt_bias", SSD_H), ("a_log", SSD_H), ("d_skip", SSD_H),
    ("ssd_norm", SSD_DI), ("q_lora_norm", Q_LORA), ("kv_lora_norm", KV_LORA), ("q_norm", QK), ("k_norm", QK), ("ln_ffn2", D_MODEL),
]


def _shard_shape(full, axis):
    k, n = full
    return (k // N_DEV, n) if axis == 0 else (k, n // N_DEV)


def _shard_rows(full):
    return full[0] * full[1] // N_DEV // PACK_COLS


LAYER_ROWS = sum(_shard_rows(f) for _, f, _ in BIG)
LAYER_ROWS_PAD = -(-LAYER_ROWS // 256) * 256


def _pack_shards(shards):
    parts = [(shards[name] if axis == 0 else shards[name].T).reshape(-1, PACK_COLS) for name, _, axis in BIG]
    pad = LAYER_ROWS_PAD - LAYER_ROWS
    if pad:
        parts.append(jnp.zeros((pad, PACK_COLS), parts[0].dtype))
    return jnp.concatenate(parts, axis=0)


def _unpack_shards(packed):
    out, r = {}, 0
    for name, full, axis in BIG:
        n = _shard_rows(full)
        k, c = _shard_shape(full, axis)
        blk = packed[r:r + n]
        out[name] = blk.reshape(k, c) if axis == 0 else blk.reshape(c, k).T
        r += n
    return out


def _working_shape(full, axis):
    return full if axis == 0 else full[::-1]


def _unpack_gathered(gathered):
    out, r = {}, 0
    for name, full, axis in BIG:
        n = _shard_rows(full)
        out[name] = gathered[:, r:r + n].reshape(_working_shape(full, axis))
        r += n
    return out


def _pack_full_grads(grads):
    parts = [grads[name].reshape(N_DEV, -1, PACK_COLS) for name, _, _ in BIG]
    short = -parts[-1].shape[1] % 16
    parts[-1] = jnp.pad(parts[-1], ((0, 0), (0, short), (0, 0)))
    pad = LAYER_ROWS_PAD - LAYER_ROWS - short
    if pad:
        parts.append(jnp.zeros((N_DEV, pad, PACK_COLS), parts[0].dtype))
    return parts


SMALL_COLS = sum(n for _, n in SMALL) + CONV_K * CONV_DIM
SMALL_ROWS = -(-(DEPTH * SMALL_COLS) // (8 * PACK_COLS)) * 8


def _pack_small(vals, conv_w):
    flat = jnp.concatenate([vals[name] for name, _ in SMALL] + [conv_w.reshape(DEPTH, -1)], axis=1).reshape(-1)
    flat = jnp.concatenate([flat, jnp.zeros((SMALL_ROWS * PACK_COLS - flat.shape[0],), F32)])
    return flat.reshape(SMALL_ROWS, PACK_COLS)


def _unpack_small(packed):
    flat = packed.reshape(-1)[:DEPTH * SMALL_COLS].reshape(DEPTH, SMALL_COLS)
    out, c = {}, 0
    for name, n in SMALL:
        out[name] = flat[:, c:c + n]
        c += n
    return out, flat[:, c:].reshape(DEPTH, CONV_K, CONV_DIM)


_IN_OFFS = [sum(IN_SPLIT[:i]) for i in range(len(IN_SPLIT) + 1)]


def _arrange_w_in(w_t):
    z, xbc, dt, cq, ckv, kr, gates = [w_t[_IN_OFFS[i]:_IN_OFFS[i + 1]] for i in range(len(IN_SPLIT))]
    pad = jnp.zeros((LANE - ROPE - SSD_H, w_t.shape[1]), w_t.dtype)
    return jnp.concatenate([z, gates, xbc, cq, ckv, kr, dt, pad], axis=0)


def _restore_w_in(g):
    z, gates, xbc = g[PROJ_Z:PROJ_GATES], g[PROJ_GATES:PROJ_XBC], g[PROJ_XBC:PROJ_CQ]
    cq, ckv = g[PROJ_CQ:PROJ_CKV], g[PROJ_CKV:PROJ_LAST]
    kr, dt = g[PROJ_LAST:PROJ_LAST + ROPE], g[PROJ_LAST + ROPE:PROJ_LAST + ROPE + SSD_H]
    return jnp.concatenate([z, xbc, dt, cq, ckv, kr, gates], axis=0)


def _pad_heads(w_t):
    k = w_t.shape[1]
    return jnp.pad(w_t.reshape(MLA_H, QK, k), ((0, 0), (0, HEAD_COLS - QK), (0, 0))).reshape(MLA_H * HEAD_COLS, k)


def _unpad_heads(g):
    k = g.shape[1]
    return g.reshape(MLA_H, HEAD_COLS, k)[:, :QK].reshape(MLA_H * QK, k)


def _row(v):
    return v.reshape(1, -1)


def _head_gain(g):
    return jnp.pad(g, (0, HEAD_COLS - QK)).reshape(1, HEAD_COLS)


def _ffn_fwd(h, ln, w13_t, w2, name):
    n = _row_fwd(_f_rmsnorm, [h], [_row(ln)], [BF16], name + "_fwd")[0]
    act, gate, up = _ffn_up_call(n, w13_t)
    return _mm(act, w2, alpha=0.5, res=h), (h, n, gate, up, act)


def _ffn_bwd(dh_out, saved, ln, w13_t, w2, name):
    h, n, gate, up, act = saved
    d_gate, d_up = _ffn_down_bwd_call(dh_out, w2, gate, up)
    d_w2 = _mm(act, dh_out, ta=True, out_dtype=BF16, alpha=0.5)
    d_n = _mm(d_gate, w13_t, b_rows=(0, D_FF))
    dh, d_ln = _mm(d_up, w13_t, b_rows=(D_FF, D_FF), res=d_n, norm_bwd=(h, _row(ln), dh_out))
    d_w13_t = jnp.concatenate([_mm(d_gate, n, ta=True, out_dtype=BF16), _mm(d_up, n, ta=True, out_dtype=BF16)], axis=0)
    return dh, d_w13_t, d_w2, d_ln[0]


def _mixer_fwd(h, big, small, conv_w, cs, sn, rider=None):
    s = h.shape[0]
    u = _row_fwd(_f_rmsnorm, [h], [_row(small["ln_mix"])], [BF16], "ln_mix_fwd")[0]
    proj = _mm(u, big["w_in"], tb=True)
    xbc = _conv_fwd_call(proj, PROJ_XBC, conv_w, _row(small["conv_b"]))
    dt_in = proj[:, PROJ_LAST + ROPE:PROJ_LAST + ROPE + SSD_H] + small["dt_bias"][None, :]
    dt = jax.nn.softplus(dt_in)
    a = -jnp.exp(small["a_log"])[None, :]
    y_scan, states = _ssd_fwd_call(xbc, dt, a)
    dsk = _row(jnp.repeat(small["d_skip"], SSD_P))
    gn_in = [y_scan, _win(xbc, 0, SSD_DI), _win(proj, PROJ_Z, SSD_DI)]
    yn = _row_fwd(_f_gated_norm, gn_in, [dsk, _row(small["ssd_norm"])], [BF16], "gated_norm_fwd")[0]
    y_ssd = _mm(yn, big["w_ssd_out"])
    qn = _row_fwd(_f_rmsnorm, [_win(proj, PROJ_CQ, Q_LORA)], [_row(small["q_lora_norm"])], [BF16], "q_lora_norm_fwd")[0]
    kvn = _row_fwd(_f_rmsnorm, [_win(proj, PROJ_CKV, KV_LORA)], [_row(small["kv_lora_norm"])], [BF16], "kv_lora_norm_fwd")[0]
    q = _mm(qn, big["w_uq"], tb=True)
    kv = _mm(kvn, big["w_ukv"], tb=True)
    qh, kh, vh = _heads_fwd_call(q, kv, proj, cs, sn, _head_gain(small["q_norm"]), _head_gain(small["k_norm"]))
    o, lse, *carried = _attn_fwd_call(qh, kh, vh, rider)
    o_rows = jnp.transpose(o, (1, 0, 2)).reshape(s, MLA_H * VDIM)
    y_mla = _mm(o_rows, big["w_mla_out"])
    out, mg = _merge_out_call(proj, y_ssd, y_mla, big["w_o"], h)
    saved = (h, u, proj, xbc, dt_in, dt, a, y_scan, states, dsk, yn, y_ssd, qn, kvn, q, kv, qh, kh, vh, o, lse, o_rows, y_mla, mg)
    return out, saved, (carried[0] if carried else None)


def _mixer_bwd(dh_out, saved, big, small, conv_w, cs, sn, rider=None):
    (h, u, proj, xbc, dt_in, dt, a, y_scan, states, dsk, yn, y_ssd, qn, kvn, q, kv, qh, kh, vh, o, lse, o_rows, y_mla, mg) = saved
    s = h.shape[0]
    d_big, d_small = {}, {}
    d_gates, d_y_ssd, d_y_mla = _merge_out_bwd_call(dh_out, big["w_o"], proj, y_ssd, y_mla)
    d_big["w_o"] = _mm(mg, dh_out, ta=True, out_dtype=BF16)
    d_o_rows = _mm(d_y_mla, big["w_mla_out"], tb=True, out_dtype=BF16)
    d_big["w_mla_out"] = _mm(o_rows, d_y_mla, ta=True, out_dtype=BF16)
    d_o = jnp.transpose(d_o_rows.reshape(s, MLA_H, VDIM), (1, 0, 2))
    delta = _attn_delta_call(o, d_o)
    *d_heads, carried = list(_attn_bwd_call(qh, kh, vh, d_o, lse.reshape(MLA_H, 1, s), delta.reshape(MLA_H, 1, s), rider)) + ([None] if rider is None else [])
    d_q, d_kv, d_kr, d_qg, d_kg = _heads_bwd_call(
        q, kv, proj, cs, sn, _head_gain(small["q_norm"]), _head_gain(small["k_norm"]), *d_heads)
    d_small["q_norm"], d_small["k_norm"] = d_qg[0, :QK], d_kg[0, :QK]
    d_qn = _mm(d_q, big["w_uq"], out_dtype=BF16)
    d_big["w_uq"] = _mm(d_q, qn, ta=True, out_dtype=BF16)
    d_kvn = _mm(d_kv, big["w_ukv"], out_dtype=BF16)
    d_big["w_ukv"] = _mm(d_kv, kvn, ta=True, out_dtype=BF16)
    (d_cq,), (d_g,) = _row_bwd(_f_rmsnorm, [_win(proj, PROJ_CQ, Q_LORA)], [_row(small["q_lora_norm"])], [d_qn], [BF16], "q_lora_norm_bwd")
    d_small["q_lora_norm"] = d_g[0]
    (d_ckv,), (d_g,) = _row_bwd(_f_rmsnorm, [_win(proj, PROJ_CKV, KV_LORA)], [_row(small["kv_lora_norm"])], [d_kvn], [BF16], "kv_lora_norm_bwd")
    d_small["kv_lora_norm"] = d_g[0]
    d_yn = _mm(d_y_ssd, big["w_ssd_out"], tb=True, out_dtype=BF16)
    d_big["w_ssd_out"] = _mm(yn, d_y_ssd, ta=True, out_dtype=BF16)
    gn_in = [y_scan, _win(xbc, 0, SSD_DI), _win(proj, PROJ_Z, SSD_DI)]
    (d_y_scan, d_xs, d_z), (d_dsk, d_g) = _row_bwd(
        _f_gated_norm, gn_in, [dsk, _row(small["ssd_norm"])], [d_yn], [F32, F32, BF16], "gated_norm_bwd")
    d_small["ssd_norm"] = d_g[0]
    d_small["d_skip"] = jnp.sum(d_dsk.reshape(SSD_H, SSD_P), axis=1)
    d_xbc_act, d_dt, d_a = _ssd_bwd_call(xbc, dt, a, states, d_y_scan, d_xs)
    d_xbc, d_conv_w, d_conv_b = _conv_bwd_call(proj, PROJ_XBC, conv_w, _row(small["conv_b"]), d_xbc_act)
    d_small["conv_b"] = d_conv_b[0]
    d_dt_in = d_dt * jax.nn.sigmoid(dt_in)
    d_small["dt_bias"] = jnp.sum(d_dt_in, axis=0)
    d_small["a_log"] = d_a[0] * a[0]
    d_last = (d_kr + jnp.pad(d_dt_in, ((0, 0), (ROPE, LANE - ROPE - SSD_H)))).astype(BF16)
    d_proj = jnp.concatenate([d_z, d_gates, d_xbc, d_cq, d_ckv, d_last], axis=1)
    dh, d_ln = _mm(d_proj, big["w_in"], norm_bwd=(h, _row(small["ln_mix"]), dh_out))
    d_big["w_in"] = _mm(d_proj, u, ta=True, out_dtype=BF16)
    d_small["ln_mix"] = d_ln[0]
    return dh, d_big, d_small, d_conv_w, carried


def _prepare_big(b):
    return dict(b, w_in=_arrange_w_in(b["w_in"]), w_uq=_pad_heads(b["w_uq"]))


def _local_step(x, positions, target, big, small, conv_w, packed_last=None):
    inv = 1.0 / (ROPE_THETA ** (jnp.arange(0, ROPE, 2, dtype=F32) / ROPE))
    ang = positions.astype(F32)[:, None] * inv
    cos, sin = jnp.cos(ang), jnp.sin(ang)
    no_lanes = jnp.zeros((x.shape[0], LANE - ROPE), F32)
    cs = jnp.concatenate([cos, cos, no_lanes], axis=1)
    sn = jnp.concatenate([-sin, sin, no_lanes], axis=1)
    carrier = DEPTH - 2 if packed_last is not None else None
    big = [None if b is None else _prepare_big(b) for b in big]
    layer_small = [{k: v[l] for k, v in small.items()} for l in range(DEPTH)]

    h, saved = x, []
    for l in range(DEPTH):
        b, sm = big[l], layer_small[l]
        h, s1 = _ffn_fwd(h, sm["ln_ffn1"], b["ffn1_w13"], b["ffn1_w2"], "ln_ffn1")
        h, s2, gathered = _mixer_fwd(h, b, sm, conv_w[l], cs, sn, _gather_rider(packed_last) if l == carrier else None)
        if gathered is not None:
            big[l + 1] = _prepare_big(_unpack_gathered(gathered))
        h, s3 = _ffn_fwd(h, sm["ln_ffn2"], b["ffn2_w13"], b["ffn2_w2"], "ln_ffn2")
        saved.append((s1, s2, s3))
    loss, dh = _loss_and_grad(h, target)

    d_big, d_small, d_conv_w = [None] * DEPTH, [None] * DEPTH, [None] * DEPTH
    for l in reversed(range(DEPTH)):
        b, sm = big[l], layer_small[l]
        s1, s2, s3 = saved[l]
        dh, d_w13_2, d_w2_2, d_ln2 = _ffn_bwd(dh, s3, sm["ln_ffn2"], b["ffn2_w13"], b["ffn2_w2"], "ln_ffn2")
        rider = _exchange_rider(_pack_full_grads(d_big[l + 1])) if l == carrier else None
        dh, db, ds, d_conv_w[l], received = _mixer_bwd(dh, s2, b, sm, conv_w[l], cs, sn, rider)
        if received is not None:
            d_big[l + 1] = received
        dh, d_w13_1, d_w2_1, d_ln1 = _ffn_bwd(dh, s1, sm["ln_ffn1"], b["ffn1_w13"], b["ffn1_w2"], "ln_ffn1")
        db.update(ffn1_w13=d_w13_1, ffn1_w2=d_w2_1, ffn2_w13=d_w13_2, ffn2_w2=d_w2_2,
                  w_in=_restore_w_in(db["w_in"]), w_uq=_unpad_heads(db["w_uq"]))
        ds.update(ln_ffn1=d_ln1, ln_ffn2=d_ln2)
        d_big[l], d_small[l] = db, ds
    d_small = {name: jnp.stack([d_small[l][name] for l in range(DEPTH)]) for name, _ in SMALL}
    return loss, dh, d_big, d_small, jnp.stack(d_conv_w)


def _step(args):
    dev = 4 * lax.axis_index("x") + 2 * lax.axis_index("y") + lax.axis_index("c")
    x, positions, target = args["x"][0], args["positions"][0], args["loss_target"][0]

    packed = [_pack_shards({name: args[name][l].astype(BF16) for name, _, _ in BIG}) for l in range(DEPTH)]
    big = [_unpack_gathered(_all_gather(packed[l])) for l in range(DEPTH - 1)] + [None]
    cw = args["conv_w"]
    cw_cols = cw.shape[-1]
    cw_rows = -(-cw.size // (8 * PACK_COLS)) * 8
    cw_flat = jnp.concatenate([cw.reshape(-1), jnp.zeros((cw_rows * PACK_COLS - cw.size,), F32)]).reshape(cw_rows, PACK_COLS)
    cw_all = _all_gather(cw_flat).reshape(N_DEV, -1)[:, :cw.size].reshape(N_DEV, DEPTH, CONV_K, cw_cols)
    conv_w = jnp.transpose(cw_all, (1, 2, 0, 3)).reshape(DEPTH, CONV_K, CONV_DIM)
    small = {name: args[name] for name, _ in SMALL}

    loss, dx, d_big, d_small, d_conv_w = _local_step(x, positions, target, big, small, conv_w, packed_last=packed[-1])
    loss = lax.psum(loss, MESH_AXES)

    out = {"loss": loss, "grad_x": dx[None]}

    grads = {name: [] for name, _, _ in BIG}
    for l in range(DEPTH):
        received = d_big[l] if l == DEPTH - 1 else _exchange_blocks(_pack_full_grads(d_big[l]))
        summed = _sum_blocks(received)
        for name, g in _unpack_shards(summed).items():
            grads[name].append(g)
    flat = lambda t: t.reshape(-1, t.shape[-1])
    for name, _, _ in BIG:
        g = jnp.stack(grads[name])
        w = args[name]
        delta, m2, v2 = _adam(flat(w), flat(g), flat(args["m_" + name]), flat(args["v_" + name]))
        out["grad_" + name] = g
        out["delta_" + name] = delta.reshape(w.shape)
        out["new_m_" + name] = m2.reshape(w.shape)
        out["new_v_" + name] = v2.reshape(w.shape)

    total = _sum_blocks(_all_gather(_pack_small(d_small, d_conv_w)))
    g_conv_w = _unpack_small(total)[1]
    zeros_cw = jnp.zeros((DEPTH, CONV_K, CONV_DIM), F32)
    delta, m2, v2 = _adam(_pack_small(small, zeros_cw), total,
                          _pack_small({name: args["m_" + name] for name, _ in SMALL}, zeros_cw),
                          _pack_small({name: args["v_" + name] for name, _ in SMALL}, zeros_cw))
    for kind, packed in (("grad_", total), ("delta_", delta), ("new_m_", m2), ("new_v_", v2)):
        for name, val in _unpack_small(packed)[0].items():
            out[kind + name] = val
    g_cw = lax.dynamic_slice_in_dim(g_conv_w, dev * cw_cols, cw_cols, axis=2)
    delta, m2, v2 = _adam(flat(cw), flat(g_cw), flat(args["m_conv_w"]), flat(args["v_conv_w"]))
    out["grad_conv_w"] = g_cw
    out["delta_conv_w"] = delta.reshape(cw.shape)
    out["new_m_conv_w"] = m2.reshape(cw.shape)
    out["new_v_conv_w"] = v2.reshape(cw.shape)
    return out


WEIGHTS = ["ln_ffn1", "ffn1_w13", "ffn1_w2", "ln_mix", "w_in", "conv_w", "conv_b", "dt_bias", "a_log", "d_skip", "ssd_norm",
           "w_ssd_out", "q_lora_norm", "w_uq", "kv_lora_norm", "w_ukv", "q_norm", "k_norm", "w_mla_out", "w_o", "ln_ffn2",
           "ffn2_w13", "ffn2_w2"]
ARG_NAMES = (["x", "positions"] + WEIGHTS + ["loss_target"] + ["m_" + n for n in WEIGHTS] + ["v_" + n for n in WEIGHTS])


def kernel(x, positions, ln_ffn1, ffn1_w13, ffn1_w2, ln_mix, w_in, conv_w, conv_b, dt_bias, a_log, d_skip, ssd_norm, w_ssd_out, q_lora_norm, w_uq, kv_lora_norm, w_ukv, q_norm, k_norm, w_mla_out, w_o, ln_ffn2, ffn2_w13, ffn2_w2, loss_target, m_ln_ffn1, m_ffn1_w13, m_ffn1_w2, m_ln_mix, m_w_in, m_conv_w, m_conv_b, m_dt_bias, m_a_log, m_d_skip, m_ssd_norm, m_w_ssd_out, m_q_lora_norm, m_w_uq, m_kv_lora_norm, m_w_ukv, m_q_norm, m_k_norm, m_w_mla_out, m_w_o, m_ln_ffn2, m_ffn2_w13, m_ffn2_w2, v_ln_ffn1, v_ffn1_w13, v_ffn1_w2, v_ln_mix, v_w_in, v_conv_w, v_conv_b, v_dt_bias, v_a_log, v_d_skip, v_ssd_norm, v_w_ssd_out, v_q_lora_norm, v_w_uq, v_kv_lora_norm, v_w_ukv, v_q_norm, v_k_norm, v_w_mla_out, v_w_o, v_ln_ffn2, v_ffn2_w13, v_ffn2_w2):
    vals = (x, positions, ln_ffn1, ffn1_w13, ffn1_w2, ln_mix, w_in, conv_w, conv_b, dt_bias, a_log, d_skip, ssd_norm, w_ssd_out, q_lora_norm, w_uq, kv_lora_norm, w_ukv, q_norm, k_norm, w_mla_out, w_o, ln_ffn2, ffn2_w13, ffn2_w2, loss_target, m_ln_ffn1, m_ffn1_w13, m_ffn1_w2, m_ln_mix, m_w_in, m_conv_w, m_conv_b, m_dt_bias, m_a_log, m_d_skip, m_ssd_norm, m_w_ssd_out, m_q_lora_norm, m_w_uq, m_kv_lora_norm, m_w_ukv, m_q_norm, m_k_norm, m_w_mla_out, m_w_o, m_ln_ffn2, m_ffn2_w13, m_ffn2_w2, v_ln_ffn1, v_ffn1_w13, v_ffn1_w2, v_ln_mix, v_w_in, v_conv_w, v_conv_b, v_dt_bias, v_a_log, v_d_skip, v_ssd_norm, v_w_ssd_out, v_q_lora_norm, v_w_uq, v_kv_lora_norm, v_w_ukv, v_q_norm, v_k_norm, v_w_mla_out, v_w_o, v_ln_ffn2, v_ffn2_w13, v_ffn2_w2)
    out = _step(dict(zip(ARG_NAMES, vals)))
    order = ["loss", "grad_x"] + [k + n for k in ("grad_", "delta_", "new_m_", "new_v_") for n in WEIGHTS]
    return tuple(out[n] for n in order)
```

```python
import jax
import jax.numpy as jnp
from jax import lax
from jax.experimental import pallas as pl
from jax.experimental.pallas import tpu as pltpu

F32 = jnp.float32
BF16 = jnp.bfloat16

D_MODEL = 1024
D_FF = 2816
DEPTH = 2
SSD_DI = 2048
SSD_P = 64
SSD_H = 32
SSD_G = 4
SSD_HPG = 8
SSD_N = 128
SSD_L = 128
CONV_K = 4
CONV_DIM = 3072
MLA_H = 8
Q_LORA = 512
KV_LORA = 256
NOPE = 128
ROPE = 64
VDIM = 128
QK = 192
ROPE_THETA = 10000.0
EPS = 1e-6
IN_SPLIT = (SSD_DI, CONV_DIM, SSD_H, Q_LORA, KV_LORA, ROPE, 2 * D_MODEL)
D_IN = sum(IN_SPLIT)
N_DEV = 8
LANE = 128
PACK_COLS = 1024

PROJ_Z = 0
PROJ_GATES = PROJ_Z + SSD_DI
PROJ_XBC = PROJ_GATES + 2 * D_MODEL
PROJ_CQ = PROJ_XBC + CONV_DIM
PROJ_CKV = PROJ_CQ + Q_LORA
PROJ_LAST = PROJ_CKV + KV_LORA
D_IN_PAD = PROJ_LAST + LANE

ADAM_LR = 0.001
ADAM_B1 = 0.9
ADAM_B2 = 0.999
ADAM_EPS = 1e-08
ADAM_WD = 0.01
ADAM_STEP = 10

VMEM_LIMIT = 48 * 1024 * 1024
ROW_IO_BUDGET = 8 * 1024 * 1024
NEG = -1e30

MESH_AXES = ("x", "y", "c")


def _cparams(*sem):
    return pltpu.CompilerParams(dimension_semantics=sem, vmem_limit_bytes=VMEM_LIMIT)


def _pick_tile(n, target, align):
    if n <= target:
        return n
    best = None
    for t in range(align, target + 1, align):
        if n % t == 0:
            best = t
    assert best is not None, (n, target, align)
    return best


def _acc_store(ref, val, first):
    @pl.when(first)
    def _():
        ref[...] = val

    @pl.when(jnp.logical_not(first))
    def _():
        ref[...] += val


def _win(arr, start, width):
    assert start % width == 0, (start, width)
    return (arr, start, width)


def _operand(entry):
    if isinstance(entry, tuple):
        arr, start, width = entry
        return arr, width, start // width
    return entry, entry.shape[1], 0


def _row_tile(rows, bytes_per_row):
    if rows <= 16:
        return rows
    t = 1024
    while t > 16 and (t * bytes_per_row > ROW_IO_BUDGET or rows % t):
        t //= 2
    assert rows % t == 0, (rows, t)
    return t


def _rowwise_call(fn, tiled, params, outs, accs, name):
    ops = [_operand(e) for e in tiled]
    rows = ops[0][0].shape[0]
    per_row = sum(w * a.dtype.itemsize for a, w, _ in ops) + sum(c * jnp.dtype(d).itemsize for c, d in outs)
    tile = _row_tile(rows, per_row)
    n_in = len(tiled) + len(params)
    n_o = len(outs)

    def body(*refs):
        vals = [r[...] for r in refs[:n_in]]
        t_out, a_out = fn(*vals)
        for r, v in zip(refs[n_in:n_in + n_o], t_out):
            r[...] = v.astype(r.dtype)
        first = pl.program_id(0) == 0
        for r, v in zip(refs[n_in + n_o:], a_out):
            _acc_store(r, v.astype(F32), first)

    def tiled_spec(width, blk):
        return pl.BlockSpec((tile, width), lambda i: (i, blk))

    in_specs = [tiled_spec(w, blk) for _, w, blk in ops]
    in_specs += [pl.BlockSpec(p.shape, lambda i: (0, 0)) for p in params]
    out_specs = [tiled_spec(c, 0) for c, _ in outs]
    out_specs += [pl.BlockSpec(s, lambda i: (0, 0)) for s in accs]
    out_shape = [jax.ShapeDtypeStruct((rows, c), d) for c, d in outs]
    out_shape += [jax.ShapeDtypeStruct(s, F32) for s in accs]
    return pl.pallas_call(
        body, grid=(rows // tile,), in_specs=in_specs, out_specs=out_specs, out_shape=out_shape,
        compiler_params=_cparams("arbitrary"), name=name,
    )(*[a for a, _, _ in ops], *params)


def _to_f32(vals):
    return [v.astype(F32) for v in vals]


def _row_fwd(f, tiled, params, out_dtypes, name):
    ops = [_operand(e) for e in tiled]
    rows = ops[0][0].shape[0]
    shapes = jax.eval_shape(f, *[jax.ShapeDtypeStruct((rows, w), F32) for _, w, _ in ops],
                            *[jax.ShapeDtypeStruct(p.shape, F32) for p in params])
    outs = [(s.shape[1], d) for s, d in zip(shapes, out_dtypes)]
    return _rowwise_call(lambda *v: (f(*_to_f32(v)), ()), tiled, params, outs, [], name)


def _row_bwd(f, tiled, params, gs, d_dtypes, name, bwd=None, add=None):
    n_t, n_g = len(tiled), len(gs)
    adds = sorted((add or {}).items())
    n_a = len(adds)

    def fn(*vals):
        vals = _to_f32(vals)
        prim = vals[:n_t] + vals[n_t + n_g + n_a:]
        g = tuple(vals[n_t:n_t + n_g])
        if bwd is not None:
            d_t, d_p = bwd(*prim, *g)
        else:
            _, vjp = jax.vjp(f, *prim)
            cts = vjp(g)
            d_t, d_p = cts[:n_t], cts[n_t:]
        d_t = list(d_t)
        for (idx, _), extra in zip(adds, vals[n_t + n_g:n_t + n_g + n_a]):
            d_t[idx] = d_t[idx] + extra
        return tuple(d_t), tuple(d_p)

    outs = [(_operand(e)[1], d) for e, d in zip(tiled, d_dtypes)]
    accs = [p.shape for p in params]
    res = _rowwise_call(fn, list(tiled) + list(gs) + [a for _, a in adds], params, outs, accs, name)
    return res[:n_t], res[n_t:]


def _f_rmsnorm(x, g):
    return (x * lax.rsqrt(jnp.mean(x * x, axis=-1, keepdims=True) + EPS) * g,)


def _f_gated_norm(ys, xs, z, dsk, g):
    t = (ys + xs * dsk) * (z * jax.nn.sigmoid(z))
    return (t * lax.rsqrt(jnp.mean(t * t, axis=-1, keepdims=True) + EPS) * g,)


def _f_merge(gates, ys, ym):
    s = jax.nn.sigmoid(gates)
    return (s[:, :D_MODEL] * ys + s[:, D_MODEL:] * ym,)


def _b_merge(gates, ys, ym, d):
    s = jax.nn.sigmoid(gates)
    s1, s2 = s[:, :D_MODEL], s[:, D_MODEL:]
    d_gates = jnp.concatenate([d * ys * s1 * (1.0 - s1), d * ym * s2 * (1.0 - s2)], axis=1)
    return (d_gates, d * s1, d * s2), ()


def _loss_and_grad(y, target):
    def fn(yv, tv):
        d = yv - tv
        return (d * (1.0 / D_MODEL),), (jnp.sum(d * d, axis=0, keepdims=True) * (0.5 / D_MODEL),)

    dy, part = _rowwise_call(fn, [y, target], [], [(D_MODEL, F32)], [(1, D_MODEL)], "loss")
    return jnp.sum(part), dy


def _adam(w, g, m, v):
    def fn(wv, gv, mv, vv):
        m2 = ADAM_B1 * mv + (1.0 - ADAM_B1) * gv
        v2 = ADAM_B2 * vv + (1.0 - ADAM_B2) * (gv * gv)
        m_hat = m2 / (1.0 - ADAM_B1 ** ADAM_STEP)
        v_hat = v2 / (1.0 - ADAM_B2 ** ADAM_STEP)
        delta = -ADAM_LR * (m_hat / (jnp.sqrt(v_hat) + ADAM_EPS) + ADAM_WD * wv)
        return (delta, m2, v2), ()

    c = w.shape[1]
    return _rowwise_call(fn, [w, g, m, v], [], [(c, F32)] * 3, [], "adamw")


def _sum_blocks(blocks):
    _, rows, c = blocks.shape
    tile = _row_tile(rows, N_DEV * c * blocks.dtype.itemsize + c * 4)

    def body(b_ref, o_ref):
        acc = b_ref[0].astype(F32)
        for i in range(1, N_DEV):
            acc = acc + b_ref[i].astype(F32)
        o_ref[...] = acc

    return pl.pallas_call(
        body, grid=(rows // tile,), in_specs=[pl.BlockSpec((N_DEV, tile, c), lambda i: (0, i, 0))],
        out_specs=pl.BlockSpec((tile, c), lambda i: (i, 0)), out_shape=jax.ShapeDtypeStruct((rows, c), F32),
        compiler_params=_cparams("arbitrary"), name="sum_blocks",
    )(blocks)


def _mm(a, b, ta=False, tb=False, out_dtype=F32, alpha=1.0, res=None, b_rows=None, norm_bwd=None):
    r_dim, p_dim = a.shape if ta else a.shape[::-1]
    b_row0, b_nrows = (0, b.shape[0]) if b_rows is None else b_rows
    r2, q_dim = (b.shape[1], b_nrows) if tb else (b_nrows, b.shape[1])
    assert r_dim == r2, (a.shape, b.shape, ta, tb)
    tp = _pick_tile(p_dim, 512, LANE)
    if tp < 512 < p_dim:
        tp = _pick_tile(p_dim, 1536, LANE)
    tq = _pick_tile(q_dim, 1536, LANE)
    tr = _pick_tile(r_dim, 1536, LANE)
    nr = r_dim // tr
    dims = (((0 if ta else 1,), (1 if tb else 0,)), ((), ()))
    has_res = res is not None
    n_nb = 0 if norm_bwd is None else 3
    assert norm_bwd is None or tq == q_dim

    def body(*refs):
        a_ref, b_ref = refs[:2]
        res_ref = refs[2] if has_res else None
        n_in = 2 + has_res + n_nb
        o_ref = refs[n_in]

        def finish(val):
            if alpha != 1.0:
                val = val * alpha
            if has_res:
                val = val + res_ref[...].astype(F32)
            if norm_bwd is not None:
                x_ref, g_ref, add_ref = refs[2 + has_res:n_in]
                x = x_ref[...]
                r = lax.rsqrt(jnp.mean(x * x, axis=-1, keepdims=True) + EPS)
                gy = val * g_ref[...]
                dot = jnp.sum(gy * x, axis=-1, keepdims=True)
                _acc_store(refs[n_in + 1], jnp.sum(val * x * r, axis=0, keepdims=True), pl.program_id(1) == 0)
                val = gy * r - x * (dot * (r * r * r) * (1.0 / q_dim)) + add_ref[...]
            o_ref[...] = val.astype(o_ref.dtype)

        part = lax.dot_general(a_ref[...].astype(BF16), b_ref[...].astype(BF16), dims, preferred_element_type=F32)
        if nr == 1:
            finish(part)
        else:
            acc_ref = refs[-1]
            k = pl.program_id(2)
            _acc_store(acc_ref, part, k == 0)

            @pl.when(k == nr - 1)
            def _():
                finish(acc_ref[...])

    a_spec = pl.BlockSpec((tr, tp), lambda j, i, k: (k, i)) if ta else pl.BlockSpec((tp, tr), lambda j, i, k: (i, k))
    assert b_row0 % (tq if tb else tr) == 0
    b0 = b_row0 // (tq if tb else tr)
    b_spec = pl.BlockSpec((tq, tr), lambda j, i, k: (j + b0, k)) if tb else pl.BlockSpec((tr, tq), lambda j, i, k: (k + b0, j))
    o_spec = pl.BlockSpec((tp, tq), lambda j, i, k: (i, j))
    row_spec = pl.BlockSpec((1, tq), lambda j, i, k: (0, 0))
    in_specs = [a_spec, b_spec] + ([o_spec] if has_res else []) + ([o_spec, row_spec, o_spec] if n_nb else [])
    out = pl.pallas_call(
        body, grid=(q_dim // tq, p_dim // tp, nr), in_specs=in_specs,
        out_specs=[o_spec] + ([row_spec] if n_nb else []),
        out_shape=[jax.ShapeDtypeStruct((p_dim, q_dim), out_dtype)] + ([jax.ShapeDtypeStruct((1, q_dim), F32)] if n_nb else []),
        scratch_shapes=[pltpu.VMEM((tp, tq), F32)] if nr > 1 else [],
        compiler_params=_cparams("arbitrary", "arbitrary", "arbitrary"),
        name=f"mm_{'t' if ta else 'n'}{'t' if tb else 'n'}_{p_dim}x{r_dim}x{q_dim}" + ("_norm_bwd" if n_nb else ""),
    )(*([a, b] + ([res] if has_res else []) + (list(norm_bwd) if n_nb else [])))
    return out if n_nb else out[0]


MERGE_TP = 256


def _merge_out_call(proj, y_ssd, y_mla, w_o, h):
    s = h.shape[0]
    tp = min(MERGE_TP, s)

    def body(g_ref, ys_ref, ym_ref, w_ref, h_ref, o_ref, mg_ref):
        mg = _f_merge(g_ref[...], ys_ref[...], ym_ref[...])[0].astype(BF16)
        mg_ref[...] = mg
        o_ref[...] = h_ref[...] + jnp.dot(mg, w_ref[...], preferred_element_type=F32)

    rows = pl.BlockSpec((tp, D_MODEL), lambda i: (i, 0))
    return pl.pallas_call(
        body, grid=(s // tp,),
        in_specs=[pl.BlockSpec((tp, 2 * D_MODEL), lambda i: (i, PROJ_GATES // (2 * D_MODEL))), rows, rows,
                  pl.BlockSpec((D_MODEL, D_MODEL), lambda i: (0, 0)), rows],
        out_specs=[rows, rows],
        out_shape=[jax.ShapeDtypeStruct((s, D_MODEL), F32), jax.ShapeDtypeStruct((s, D_MODEL), BF16)],
        compiler_params=_cparams("arbitrary"), name="merge_out",
    )(proj, y_ssd, y_mla, w_o, h)


def _merge_out_bwd_call(dh, w_o, proj, y_ssd, y_mla):
    s = dh.shape[0]
    tp = min(MERGE_TP, s)

    def body(dh_ref, w_ref, g_ref, ys_ref, ym_ref, dg_ref, dys_ref, dym_ref):
        d_mg = _nt(dh_ref[...].astype(BF16), w_ref[...])
        (d_g, d_ys, d_ym), _ = _b_merge(g_ref[...], ys_ref[...], ym_ref[...], d_mg)
        dg_ref[...] = d_g.astype(BF16)
        dys_ref[...] = d_ys.astype(BF16)
        dym_ref[...] = d_ym.astype(BF16)

    rows = pl.BlockSpec((tp, D_MODEL), lambda i: (i, 0))
    wide = pl.BlockSpec((tp, 2 * D_MODEL), lambda i: (i, 0))
    return pl.pallas_call(
        body, grid=(s // tp,),
        in_specs=[rows, pl.BlockSpec((D_MODEL, D_MODEL), lambda i: (0, 0)),
                  pl.BlockSpec((tp, 2 * D_MODEL), lambda i: (i, PROJ_GATES // (2 * D_MODEL))), rows, rows],
        out_specs=[wide, rows, rows],
        out_shape=[jax.ShapeDtypeStruct((s, 2 * D_MODEL), BF16), jax.ShapeDtypeStruct((s, D_MODEL), BF16),
                   jax.ShapeDtypeStruct((s, D_MODEL), BF16)],
        compiler_params=_cparams("arbitrary"), name="merge_out_bwd",
    )(dh, w_o, proj, y_ssd, y_mla)


FFN_TP = 512
FFN_TQ = 1408


def _ffn_up_call(n, w13_t):
    s, d = n.shape
    tp = min(FFN_TP, s)
    up0 = D_FF // FFN_TQ

    def body(n_ref, wg_ref, wu_ref, act_ref, gate_ref, up_ref):
        a = n_ref[...]
        g = _nt(a, wg_ref[...])
        u = _nt(a, wu_ref[...])
        act_ref[...] = (g * jax.nn.sigmoid(g) * u).astype(BF16)
        gate_ref[...] = g.astype(BF16)
        up_ref[...] = u.astype(BF16)

    o_spec = pl.BlockSpec((tp, FFN_TQ), lambda j, i: (i, j))
    return pl.pallas_call(
        body, grid=(D_FF // FFN_TQ, s // tp),
        in_specs=[pl.BlockSpec((tp, d), lambda j, i: (i, 0)), pl.BlockSpec((FFN_TQ, d), lambda j, i: (j, 0)),
                  pl.BlockSpec((FFN_TQ, d), lambda j, i: (j + up0, 0))],
        out_specs=[o_spec] * 3, out_shape=[jax.ShapeDtypeStruct((s, D_FF), BF16)] * 3,
        compiler_params=_cparams("arbitrary", "arbitrary"), name="ffn_up_swiglu",
    )(n, w13_t, w13_t)


def _ffn_down_bwd_call(dh, w2, gate, up):
    s, d = dh.shape
    tp = min(FFN_TP, s)

    def body(dh_ref, w2_ref, gate_ref, up_ref, dg_ref, du_ref):
        d_act = 0.5 * _nt(dh_ref[...].astype(BF16), w2_ref[...])
        g, u = gate_ref[...].astype(F32), up_ref[...].astype(F32)
        sg = jax.nn.sigmoid(g)
        dg_ref[...] = (d_act * u * sg * (1.0 + g * (1.0 - sg))).astype(BF16)
        du_ref[...] = (d_act * g * sg).astype(BF16)

    o_spec = pl.BlockSpec((tp, FFN_TQ), lambda j, i: (i, j))
    return pl.pallas_call(
        body, grid=(D_FF // FFN_TQ, s // tp),
        in_specs=[pl.BlockSpec((tp, d), lambda j, i: (i, 0)), pl.BlockSpec((FFN_TQ, d), lambda j, i: (j, 0)), o_spec, o_spec],
        out_specs=[o_spec] * 2, out_shape=[jax.ShapeDtypeStruct((s, D_FF), BF16)] * 2,
        compiler_params=_cparams("arbitrary", "arbitrary"), name="ffn_down_bwd_swiglu",
    )(dh, w2, gate, up)


ATTN_SCALE = QK ** -0.5
LOG2E = 1.4426950408889634
ATTN_C = ATTN_SCALE * LOG2E


ATTN_HEADS = 2


def _attn_tile(s):
    return min(512, s)


def _causal_keep(t, keys_on_rows=False):
    row = lax.broadcasted_iota(jnp.int32, (t, t), 0)
    col = lax.broadcasted_iota(jnp.int32, (t, t), 1)
    return row <= col if keys_on_rows else col <= row


def _nt(a, b):
    return lax.dot_general(a, b, (((1,), (1,)), ((), ())), preferred_element_type=F32)


def _rider_phases(rider, src_ref, out_ref, sems, first, last):
    @pl.when(first)
    def _():
        _peer_copies(src_ref, out_ref, sems, rider["gather"], "start")

    def finish():
        @pl.when(last)
        def _():
            _peer_copies(src_ref, out_ref, sems, rider["gather"], "finish")

    return finish


def _attn_fwd_call(q, k, v, rider=None):
    nh, s, _ = q.shape
    t = _attn_tile(s)
    nb = s // t
    hp = ATTN_HEADS
    n_r = 0 if rider is None else 1

    def body(*refs):
        q_ref, k_ref, v_ref = refs[:3]
        o_ref, lse_ref = refs[3 + n_r:5 + n_r]
        qi = pl.program_id(1)
        finish = None
        if rider is not None:
            h = pl.program_id(0)
            finish = _rider_phases(rider, refs[3:4], refs[5 + n_r], refs[6 + n_r:],
                                   jnp.logical_and(h == 0, qi == 0), jnp.logical_and(h == nh // hp - 1, qi == nb - 1))
        qs = [q_ref[i] for i in range(hp)]

        def block(kb, carries, diagonal, width=1):
            start = pl.multiple_of(kb * t, t)
            out = []
            for i, (m_prev, l_prev, acc) in enumerate(carries):
                sc = _nt(qs[i], k_ref[i, pl.ds(start, width * t), :])
                if diagonal:
                    sc = jnp.where(_causal_keep(t), sc, NEG)
                m_new = jnp.maximum(m_prev, jnp.max(sc, axis=-1, keepdims=True))
                p = jnp.exp2(sc * ATTN_C - m_new * ATTN_C)
                alpha = jnp.exp2((m_prev - m_new) * ATTN_C)
                l_new = alpha * l_prev + jnp.sum(p, axis=-1, keepdims=True)
                pv = jnp.dot(p.astype(BF16), v_ref[i, pl.ds(start, width * t), :], preferred_element_type=F32)
                out.append((m_new, l_new, alpha * acc + pv))
            return tuple(out)

        init = tuple((jnp.full((t, 1), NEG, F32), jnp.zeros((t, 1), F32), jnp.zeros((t, VDIM), F32)) for _ in range(hp))
        carries = lax.fori_loop(0, qi // 2, lambda j, c: block(2 * j, c, False, width=2), init)
        carries = lax.cond(qi % 2 == 1, lambda c: block(qi - 1, c, False), lambda c: c, carries)
        for i, (m, l, acc) in enumerate(block(qi, carries, True)):
            o_ref[i] = (acc / l).astype(o_ref.dtype)
            lse_ref[i] = m * ATTN_SCALE + jnp.log(l)
        if finish is not None:
            finish()

    qmap = lambda h, i: (h, i, 0)
    whole = lambda h, i: (h, 0, 0)
    return pl.pallas_call(
        body, grid=(nh // hp, nb),
        in_specs=[pl.BlockSpec((hp, t, QK), qmap), pl.BlockSpec((hp, s, QK), whole), pl.BlockSpec((hp, s, VDIM), whole)] + [HBM_SPEC] * n_r,
        out_specs=[pl.BlockSpec((hp, t, VDIM), qmap), pl.BlockSpec((hp, t, 1), qmap)] + [HBM_SPEC] * n_r,
        out_shape=[jax.ShapeDtypeStruct((nh, s, VDIM), BF16), jax.ShapeDtypeStruct((nh, s, 1), F32)] + ([rider["out"]] if n_r else []),
        scratch_shapes=_comm_scratch() if n_r else [],
        compiler_params=_cparams("arbitrary", "arbitrary"), name="attn_fwd_gather" if n_r else "attn_fwd",
    )(*([q, k, v] + (rider["srcs"] if n_r else [])))


def _attn_delta_call(o, do):
    nh, s, d = o.shape

    def fn(ov, dv):
        return (jnp.sum(ov.astype(F32) * dv.astype(F32), axis=-1, keepdims=True),), ()

    return _rowwise_call(fn, [o.reshape(nh * s, d), do.reshape(nh * s, d)], [], [(1, F32)], [], "attn_delta")[0]


def _attn_bwd_call(q, k, v, do, lse_t, delta_t, rider=None):
    nh, s, _ = q.shape
    t = _attn_tile(s)
    nb = s // t
    hp = ATTN_HEADS
    n_src = 0 if rider is None else len(rider["srcs"])
    n_r = 0 if rider is None else 1

    def body(*refs):
        q_ref, k_ref, v_ref, do_ref, lse_ref, delta_ref = refs[:6]
        dq_ref, dk_ref, dv_ref = refs[6 + n_src:9 + n_src]
        dk_sc, dv_sc = refs[9 + n_src + n_r:11 + n_src + n_r]
        kj = pl.program_id(1)
        finish = None
        if rider is not None:
            h = pl.program_id(0)
            finish = _rider_phases(rider, refs[6:6 + n_src], refs[9 + n_src], refs[11 + n_src + n_r:],
                                   jnp.logical_and(h == 0, kj == 0), jnp.logical_and(h == nh // hp - 1, kj == nb - 1))

        @pl.when(kj == 0)
        def _():
            dq_ref[...] = jnp.zeros_like(dq_ref)

        dk_sc[...] = jnp.zeros_like(dk_sc)
        dv_sc[...] = jnp.zeros_like(dv_sc)
        kblks = [k_ref[i] for i in range(hp)]
        vblks = [v_ref[i] for i in range(hp)]

        def block(qb, diagonal):
            start = pl.multiple_of(qb * t, t)
            for i in range(hp):
                qblk = q_ref[i, pl.ds(start, t), :]
                doblk = do_ref[i, pl.ds(start, t), :]
                sc = _nt(kblks[i], qblk)
                if diagonal:
                    sc = jnp.where(_causal_keep(t, keys_on_rows=True), sc, NEG)
                p = jnp.exp2(sc * ATTN_C - lse_ref[i, :, pl.ds(start, t)] * LOG2E)
                dv_sc[i] += jnp.dot(p.astype(BF16), doblk, preferred_element_type=F32)
                dp = _nt(vblks[i], doblk)
                ds = (p * (dp - delta_ref[i, :, pl.ds(start, t)])).astype(BF16)
                dk_sc[i] += jnp.dot(ds, qblk, preferred_element_type=F32)
                dq_ref[i, pl.ds(start, t), :] += lax.dot_general(ds, kblks[i], (((0,), (0,)), ((), ())), preferred_element_type=F32)

        block(kj, True)

        def rest(qb, carry):
            block(qb, False)
            return carry

        lax.fori_loop(kj + 1, nb, rest, 0)
        dk_ref[...] = (dk_sc[...] * ATTN_SCALE).astype(dk_ref.dtype)
        dv_ref[...] = dv_sc[...].astype(dv_ref.dtype)

        @pl.when(kj == nb - 1)
        def _():
            dq_ref[...] = dq_ref[...] * ATTN_SCALE

        if finish is not None:
            finish()

    kmap = lambda h, j: (h, j, 0)
    whole = lambda h, j: (h, 0, 0)
    once = pl.Buffered(buffer_count=1)
    return pl.pallas_call(
        body, grid=(nh // hp, nb),
        in_specs=[pl.BlockSpec((hp, s, QK), whole, pipeline_mode=once), pl.BlockSpec((hp, t, QK), kmap), pl.BlockSpec((hp, t, VDIM), kmap),
                  pl.BlockSpec((hp, s, VDIM), whole, pipeline_mode=once), pl.BlockSpec((hp, 1, s), whole, pipeline_mode=once),
                  pl.BlockSpec((hp, 1, s), whole, pipeline_mode=once)] + [HBM_SPEC] * n_src,
        out_specs=[pl.BlockSpec((hp, s, QK), whole, pipeline_mode=once), pl.BlockSpec((hp, t, QK), kmap),
                   pl.BlockSpec((hp, t, VDIM), kmap)] + [HBM_SPEC] * n_r,
        out_shape=[jax.ShapeDtypeStruct((nh, s, QK), F32), jax.ShapeDtypeStruct((nh, s, QK), F32),
                   jax.ShapeDtypeStruct((nh, s, VDIM), F32)] + ([rider["out"]] if n_r else []),
        scratch_shapes=[pltpu.VMEM((hp, t, QK), F32), pltpu.VMEM((hp, t, VDIM), F32)] + (_comm_scratch() if n_r else []),
        compiler_params=_cparams("arbitrary", "arbitrary"), name="attn_bwd_exchange" if n_r else "attn_bwd",
    )(*([q, k, v, do, lse_t, delta_t] + (rider["srcs"] if n_r else [])))


HEAD_COLS = NOPE + VDIM
HEADS_TILE = 256


def _swap_rope_halves(t, lane):
    half = ROPE // 2
    return jnp.where(lane < half, pltpu.roll(t, LANE - half, 1), pltpu.roll(t, half, 1))


def _head_fwd(n, p, gain, cs, sn, lane):
    r = lax.rsqrt((jnp.sum(n * n, axis=-1, keepdims=True) + jnp.sum(p * p, axis=-1, keepdims=True)) * (1.0 / QK) + EPS)
    yp = p * r * gain[:, NOPE:]
    return n * r * gain[:, :NOPE], yp * cs + _swap_rope_halves(yp, lane) * sn


def _head_bwd(n, p, gain, cs, sn, lane, dzn, dzp):
    r = lax.rsqrt((jnp.sum(n * n, axis=-1, keepdims=True) + jnp.sum(p * p, axis=-1, keepdims=True)) * (1.0 / QK) + EPS)
    dyp = dzp * cs + _swap_rope_halves(dzp * sn, lane)
    gyn, gyp = dzn * gain[:, :NOPE], dyp * gain[:, NOPE:]
    dot = jnp.sum(gyn * n, axis=-1, keepdims=True) + jnp.sum(gyp * p, axis=-1, keepdims=True)
    coef = dot * (r * r * r) * (1.0 / QK)
    d_gn = jnp.sum(dzn * n * r, axis=0, keepdims=True)
    d_gp = jnp.sum(dyp * p * r, axis=0, keepdims=True)
    return gyn * r - n * coef, gyp * r - p * coef, d_gn, d_gp


def _heads_fwd_call(q, kv, proj, cs, sn, q_gain, k_gain):
    s = q.shape[0]
    t = min(HEADS_TILE, s)

    def body(q_ref, kv_ref, last_ref, cs_ref, sn_ref, qg_ref, kg_ref, qh_ref, kh_ref, vh_ref):
        lane = lax.broadcasted_iota(jnp.int32, (t, LANE), 1)
        cs_, sn_ = cs_ref[...], sn_ref[...]
        kp = jnp.where(lane < ROPE, last_ref[...], 0.0)
        for h in range(MLA_H):
            c0 = h * HEAD_COLS
            zn, zp = _head_fwd(q_ref[:, c0:c0 + NOPE], q_ref[:, c0 + NOPE:c0 + HEAD_COLS], qg_ref[...], cs_, sn_, lane)
            qh_ref[h, :, :NOPE] = zn.astype(BF16)
            qh_ref[h, :, NOPE:] = zp[:, :ROPE].astype(BF16)
            zn, zp = _head_fwd(kv_ref[:, c0:c0 + NOPE], kp, kg_ref[...], cs_, sn_, lane)
            kh_ref[h, :, :NOPE] = zn.astype(BF16)
            kh_ref[h, :, NOPE:] = zp[:, :ROPE].astype(BF16)
            vh_ref[h] = kv_ref[:, c0 + NOPE:c0 + HEAD_COLS].astype(BF16)

    rows = lambda i: (i, 0)
    whole = lambda i: (0, 0)
    heads = lambda i: (0, i, 0)
    wide = MLA_H * HEAD_COLS
    return pl.pallas_call(
        body, grid=(s // t,),
        in_specs=[pl.BlockSpec((t, wide), rows), pl.BlockSpec((t, wide), rows),
                  pl.BlockSpec((t, LANE), lambda i: (i, PROJ_LAST // LANE)),
                  pl.BlockSpec((t, LANE), rows), pl.BlockSpec((t, LANE), rows),
                  pl.BlockSpec((1, HEAD_COLS), whole), pl.BlockSpec((1, HEAD_COLS), whole)],
        out_specs=[pl.BlockSpec((MLA_H, t, QK), heads), pl.BlockSpec((MLA_H, t, QK), heads), pl.BlockSpec((MLA_H, t, VDIM), heads)],
        out_shape=[jax.ShapeDtypeStruct((MLA_H, s, QK), BF16), jax.ShapeDtypeStruct((MLA_H, s, QK), BF16),
                   jax.ShapeDtypeStruct((MLA_H, s, VDIM), BF16)],
        compiler_params=_cparams("arbitrary"), name="mla_heads_fwd",
    )(q, kv, proj, cs, sn, q_gain, k_gain)


def _heads_bwd_call(q, kv, proj, cs, sn, q_gain, k_gain, dqh, dkh, dvh):
    s = q.shape[0]
    t = min(HEADS_TILE, s)

    def body(q_ref, kv_ref, last_ref, cs_ref, sn_ref, qg_ref, kg_ref, dqh_ref, dkh_ref, dvh_ref,
             dq_ref, dkv_ref, dkr_ref, dqg_ref, dkg_ref):
        lane = lax.broadcasted_iota(jnp.int32, (t, LANE), 1)
        cs_, sn_ = cs_ref[...], sn_ref[...]
        kp = jnp.where(lane < ROPE, last_ref[...], 0.0)
        no_lanes = jnp.zeros((t, LANE - ROPE), F32)
        d_kp = jnp.zeros((t, LANE), F32)
        d_qg = [jnp.zeros((1, NOPE), F32), jnp.zeros((1, LANE), F32)]
        d_kg = [jnp.zeros((1, NOPE), F32), jnp.zeros((1, LANE), F32)]
        for h in range(MLA_H):
            c0 = h * HEAD_COLS
            dz = dqh_ref[h]
            dzp = jnp.concatenate([dz[:, NOPE:], no_lanes], axis=1)
            d_n, d_p, g_n, g_p = _head_bwd(q_ref[:, c0:c0 + NOPE], q_ref[:, c0 + NOPE:c0 + HEAD_COLS], qg_ref[...],
                                           cs_, sn_, lane, dz[:, :NOPE], dzp)
            dq_ref[:, c0:c0 + NOPE] = d_n.astype(dq_ref.dtype)
            dq_ref[:, c0 + NOPE:c0 + HEAD_COLS] = d_p.astype(dq_ref.dtype)
            d_qg = [d_qg[0] + g_n, d_qg[1] + g_p]
            dz = dkh_ref[h]
            dzp = jnp.concatenate([dz[:, NOPE:], no_lanes], axis=1)
            d_n, d_p, g_n, g_p = _head_bwd(kv_ref[:, c0:c0 + NOPE], kp, kg_ref[...], cs_, sn_, lane, dz[:, :NOPE], dzp)
            dkv_ref[:, c0:c0 + NOPE] = d_n.astype(dkv_ref.dtype)
            dkv_ref[:, c0 + NOPE:c0 + HEAD_COLS] = dvh_ref[h].astype(dkv_ref.dtype)
            d_kp = d_kp + d_p
            d_kg = [d_kg[0] + g_n, d_kg[1] + g_p]
        dkr_ref[...] = d_kp
        first = pl.program_id(0) == 0
        _acc_store(dqg_ref.at[:, pl.ds(0, NOPE)], d_qg[0], first)
        _acc_store(dqg_ref.at[:, pl.ds(NOPE, LANE)], d_qg[1], first)
        _acc_store(dkg_ref.at[:, pl.ds(0, NOPE)], d_kg[0], first)
        _acc_store(dkg_ref.at[:, pl.ds(NOPE, LANE)], d_kg[1], first)

    rows = lambda i: (i, 0)
    whole = lambda i: (0, 0)
    heads = lambda i: (0, i, 0)
    wide = MLA_H * HEAD_COLS
    return pl.pallas_call(
        body, grid=(s // t,),
        in_specs=[pl.BlockSpec((t, wide), rows), pl.BlockSpec((t, wide), rows),
                  pl.BlockSpec((t, LANE), lambda i: (i, PROJ_LAST // LANE)),
                  pl.BlockSpec((t, LANE), rows), pl.BlockSpec((t, LANE), rows),
                  pl.BlockSpec((1, HEAD_COLS), whole), pl.BlockSpec((1, HEAD_COLS), whole),
                  pl.BlockSpec((MLA_H, t, QK), heads), pl.BlockSpec((MLA_H, t, QK), heads), pl.BlockSpec((MLA_H, t, VDIM), heads)],
        out_specs=[pl.BlockSpec((t, wide), rows), pl.BlockSpec((t, wide), rows), pl.BlockSpec((t, LANE), rows),
                   pl.BlockSpec((1, HEAD_COLS), whole), pl.BlockSpec((1, HEAD_COLS), whole)],
        out_shape=[jax.ShapeDtypeStruct((s, wide), BF16), jax.ShapeDtypeStruct((s, wide), BF16), jax.ShapeDtypeStruct((s, LANE), F32),
                   jax.ShapeDtypeStruct((1, HEAD_COLS), F32), jax.ShapeDtypeStruct((1, HEAD_COLS), F32)],
        compiler_params=_cparams("arbitrary"), name="mla_heads_bwd",
    )(q, kv, proj, cs, sn, q_gain, k_gain, dqh, dkh, dvh)


CONV_TC = 512
HALO = 8


def _conv_tiles(s):
    return min(512, s)


def _conv_fwd_call(x, col0, w, b):
    s = x.shape[0]
    ts = _conv_tiles(s)
    hb = ts // HALO
    c0 = col0 // CONV_TC
    assert col0 % CONV_TC == 0

    def body(x_ref, prev_ref, w_ref, b_ref, y_ref, buf):
        si = pl.program_id(1)
        buf[0:HALO, :] = jnp.where(si > 0, prev_ref[...], 0.0)
        buf[HALO:, :] = x_ref[...]
        acc = jnp.broadcast_to(b_ref[...], (ts, CONV_TC))
        for k in range(CONV_K):
            acc = acc + w_ref[k:k + 1, :] * buf[pl.ds(HALO - (CONV_K - 1) + k, ts), :]
        y_ref[...] = acc * jax.nn.sigmoid(acc)

    return pl.pallas_call(
        body, grid=(CONV_DIM // CONV_TC, s // ts),
        in_specs=[pl.BlockSpec((ts, CONV_TC), lambda ci, si: (si, ci + c0)),
                  pl.BlockSpec((HALO, CONV_TC), lambda ci, si: (jnp.maximum(si * hb - 1, 0), ci + c0)),
                  pl.BlockSpec((CONV_K, CONV_TC), lambda ci, si: (0, ci)),
                  pl.BlockSpec((1, CONV_TC), lambda ci, si: (0, ci))],
        out_specs=pl.BlockSpec((ts, CONV_TC), lambda ci, si: (si, ci)),
        out_shape=jax.ShapeDtypeStruct((s, CONV_DIM), F32),
        scratch_shapes=[pltpu.VMEM((ts + HALO, CONV_TC), F32)],
        compiler_params=_cparams("arbitrary", "arbitrary"), name="conv_fwd",
    )(x, x, w, b)


def _conv_bwd_call(x, col0, w, b, dy):
    s = x.shape[0]
    ts = _conv_tiles(s)
    hb = ts // HALO
    ns = s // ts
    last_halo = s // HALO - 1
    c0 = col0 // CONV_TC

    def body(x_ref, prev_ref, next_ref, dy_ref, dyn_ref, w_ref, b_ref, dx_ref, dw_ref, db_ref, xbuf, dbuf):
        si = pl.program_id(1)
        xbuf[0:HALO, :] = jnp.where(si > 0, prev_ref[...], 0.0)
        xbuf[HALO:HALO + ts, :] = x_ref[...]
        xbuf[HALO + ts:, :] = next_ref[...]
        pre = jnp.broadcast_to(b_ref[...], (ts + HALO, CONV_TC))
        for k in range(CONV_K):
            pre = pre + w_ref[k:k + 1, :] * xbuf[pl.ds(HALO - (CONV_K - 1) + k, ts + HALO), :]
        sg = jax.nn.sigmoid(pre)
        dsilu = sg * (1.0 + pre * (1.0 - sg))
        dbuf[0:ts, :] = dy_ref[...] * dsilu[0:ts]
        dbuf[ts:, :] = jnp.where(si < ns - 1, dyn_ref[...] * dsilu[ts:], 0.0)
        dx = jnp.zeros((ts, CONV_TC), F32)
        for k in range(CONV_K):
            dx = dx + w_ref[k:k + 1, :] * dbuf[pl.ds(CONV_K - 1 - k, ts), :]
        dx_ref[...] = dx.astype(dx_ref.dtype)
        dpre = dbuf[0:ts, :]
        first = si == 0
        _acc_store(db_ref, jnp.sum(dpre, axis=0, keepdims=True), first)
        for k in range(CONV_K):
            dw_k = jnp.sum(dpre * xbuf[pl.ds(HALO - (CONV_K - 1) + k, ts), :], axis=0, keepdims=True)
            _acc_store(dw_ref.at[pl.ds(k, 1), :], dw_k, first)

    main = lambda ci, si: (si, ci)
    x_main = lambda ci, si: (si, ci + c0)
    x_prev = lambda ci, si: (jnp.maximum(si * hb - 1, 0), ci + c0)
    x_next = lambda ci, si: (jnp.minimum(si * hb + hb, last_halo), ci + c0)
    return pl.pallas_call(
        body, grid=(CONV_DIM // CONV_TC, ns),
        in_specs=[pl.BlockSpec((ts, CONV_TC), x_main), pl.BlockSpec((HALO, CONV_TC), x_prev), pl.BlockSpec((HALO, CONV_TC), x_next),
                  pl.BlockSpec((ts, CONV_TC), main),
                  pl.BlockSpec((HALO, CONV_TC), lambda ci, si: (jnp.minimum(si * hb + hb, last_halo), ci)),
                  pl.BlockSpec((CONV_K, CONV_TC), lambda ci, si: (0, ci)),
                  pl.BlockSpec((1, CONV_TC), lambda ci, si: (0, ci))],
        out_specs=[pl.BlockSpec((ts, CONV_TC), main),
                   pl.BlockSpec((CONV_K, CONV_TC), lambda ci, si: (0, ci)),
                   pl.BlockSpec((1, CONV_TC), lambda ci, si: (0, ci))],
        out_shape=[jax.ShapeDtypeStruct((s, CONV_DIM), BF16), jax.ShapeDtypeStruct((CONV_K, CONV_DIM), F32),
                   jax.ShapeDtypeStruct((1, CONV_DIM), F32)],
        scratch_shapes=[pltpu.VMEM((ts + 2 * HALO, CONV_TC), F32), pltpu.VMEM((ts + HALO, CONV_TC), F32)],
        compiler_params=_cparams("arbitrary", "arbitrary"), name="conv_bwd",
    )(x, x, x, dy, dy, w, b)


GW = SSD_HPG * SSD_P
B_COL = SSD_DI
C_COL = SSD_DI + SSD_G * SSD_N


def _ones_where(mask):
    return jnp.where(mask, 1.0, 0.0).astype(BF16)


def _split(v, passes):
    parts, rest = [], v
    for i in range(passes):
        part = rest.astype(BF16)
        parts.append(part)
        if i + 1 < passes:
            rest = rest - part.astype(F32)
    return parts


def _dot_sel_r(v, sel, passes=3):
    out = None
    for part in _split(v, passes):
        t = jnp.dot(part, sel, preferred_element_type=F32)
        out = t if out is None else out + t
    return out


def _dot_sel_l(sel, v, passes=3):
    out = None
    for part in _split(v, passes):
        t = jnp.dot(sel, part, preferred_element_type=F32)
        out = t if out is None else out + t
    return out


def _ssd_consts():
    r = lax.broadcasted_iota(jnp.int32, (SSD_L, SSD_L), 0)
    c = lax.broadcasted_iota(jnp.int32, (SSD_L, SSD_L), 1)
    tril = r >= c
    triu = c >= r
    shift = SSD_P.bit_length() - 1
    eh = lax.broadcasted_iota(jnp.int32, (SSD_H, SSD_DI), 0)
    ej = lax.broadcasted_iota(jnp.int32, (SSD_H, SSD_DI), 1)
    expand = _ones_where(lax.shift_right_logical(ej, shift) == eh)
    rj = lax.broadcasted_iota(jnp.int32, (SSD_DI, SSD_H), 0)
    rh = lax.broadcasted_iota(jnp.int32, (SSD_DI, SSD_H), 1)
    reduce_ = _ones_where(lax.shift_right_logical(rj, shift) == rh)
    lane = lax.broadcasted_iota(jnp.int32, (SSD_L, LANE), 1)
    return tril, triu, expand, reduce_, lane < SSD_P


def _ssd_decays(dt, dt_t, a, a_t, tril, triu, expand):
    dta = dt * a
    acum = _dot_sel_l(_ones_where(tril), dta)
    acum_t = _dot_sel_r(dt_t * a_t, _ones_where(triu))
    dta_e = _dot_sel_r(dta, expand)
    acum_e = _dot_sel_r(acum, expand)
    last_e = jnp.sum(dta_e, axis=0, keepdims=True)
    return acum, acum_t, acum_e, last_e


def _head_decay(acum, acum_t, h, tril):
    seg = acum[:, h:h + 1] - acum_t[h:h + 1, :]
    return jnp.exp(jnp.where(tril, seg, NEG))


def _ssd_fwd_call(xbc, dt, a):
    s = xbc.shape[0]
    nc = s // SSD_L
    dt_t = dt.T
    a_t = a.T

    def body(xbc_ref, dt_ref, dtt_ref, a_ref, at_ref, y_ref, st_ref, s_sc):
        ci = pl.program_id(0)

        @pl.when(ci == 0)
        def _():
            s_sc[...] = jnp.zeros_like(s_sc)

        st_ref[0] = s_sc[...]
        tril, triu, expand, _, low_half = _ssd_consts()
        acum, acum_t, acum_e, last_e = _ssd_decays(dt_ref[...], dtt_ref[...], a_ref[...], at_ref[...], tril, triu, expand)
        dt_e = _dot_sel_r(dt_ref[...], expand, passes=2)
        xdt = xbc_ref[:, :SSD_DI] * dt_e
        xdt_b = xdt.astype(BF16)
        xw_b = (xdt * jnp.exp(last_e - acum_e)).astype(BF16)
        ea_e = jnp.exp(acum_e)
        el_e = jnp.exp(last_e)
        for g in range(SSD_G):
            gs = slice(g * GW, (g + 1) * GW)
            bg = xbc_ref[:, B_COL + g * SSD_N:B_COL + (g + 1) * SSD_N]
            cg_b = xbc_ref[:, C_COL + g * SSD_N:C_COL + (g + 1) * SSD_N].astype(BF16)
            bg_b = bg.astype(BF16)
            cb = _nt(cg_b, bg_b)
            st = s_sc[:, gs]
            y_off = jnp.dot(cg_b, st.astype(BF16), preferred_element_type=F32) * ea_e[:, gs]
            for pr in range(SSD_HPG // 2):
                ls = slice(g * GW + pr * LANE, g * GW + (pr + 1) * LANE)
                xp = xdt_b[:, ls]
                yd = []
                for half in range(2):
                    h = g * SSD_HPG + pr * 2 + half
                    m = (cb * _head_decay(acum, acum_t, h, tril)).astype(BF16)
                    yd.append(jnp.dot(m, xp, preferred_element_type=F32))
                y_ref[:, ls] = jnp.where(low_half, yd[0], yd[1]) + y_off[:, pr * LANE:(pr + 1) * LANE]
            s_sc[:, gs] = st * el_e[:, gs] + jnp.dot(bg.T.astype(BF16), xw_b[:, gs], preferred_element_type=F32)

    row = lambda i: (i, 0)
    return pl.pallas_call(
        body, grid=(nc,),
        in_specs=[pl.BlockSpec((SSD_L, CONV_DIM), row), pl.BlockSpec((SSD_L, SSD_H), row),
                  pl.BlockSpec((SSD_H, SSD_L), lambda i: (0, i)), pl.BlockSpec((1, SSD_H), lambda i: (0, 0)),
                  pl.BlockSpec((SSD_H, 1), lambda i: (0, 0))],
        out_specs=[pl.BlockSpec((SSD_L, SSD_DI), row), pl.BlockSpec((1, SSD_N, SSD_DI), lambda i: (i, 0, 0))],
        out_shape=[jax.ShapeDtypeStruct((s, SSD_DI), F32), jax.ShapeDtypeStruct((nc, SSD_N, SSD_DI), F32)],
        scratch_shapes=[pltpu.VMEM((SSD_N, SSD_DI), F32)],
        compiler_params=_cparams("arbitrary"), name="ssd_fwd",
    )(xbc, dt, dt_t, a, a_t)


def _ssd_bwd_call(xbc, dt, a, states, dy, dx_extra):
    s = xbc.shape[0]
    nc = s // SSD_L
    dt_t = dt.T
    a_t = a.T

    def body(xbc_ref, dt_ref, dtt_ref, a_ref, at_ref, st_ref, dy_ref, dxe_ref,
             dxbc_ref, ddt_ref, da_ref, ds_sc, yf_sc, dxd_sc, dxw_sc):
        i = pl.program_id(0)

        @pl.when(i == 0)
        def _():
            ds_sc[...] = jnp.zeros_like(ds_sc)

        tril, triu, expand, reduce_, low_half = _ssd_consts()
        dt = dt_ref[...]
        a_row = a_ref[...]
        acum, acum_t, acum_e, last_e = _ssd_decays(dt, dtt_ref[...], a_row, at_ref[...], tril, triu, expand)
        dt_e = _dot_sel_r(dt, expand, passes=2)
        x = xbc_ref[:, :SSD_DI]
        xdt = x * dt_e
        xdt_b = xdt.astype(BF16)
        w_e = jnp.exp(last_e - acum_e)
        xw_b = (xdt * w_e).astype(BF16)
        ea_e = jnp.exp(acum_e)
        el_e = jnp.exp(last_e)
        dy = dy_ref[...]
        dy_b = dy.astype(BF16)
        s_prev = st_ref[0]
        ds_new = ds_sc[...]
        ds_new_b = ds_new.astype(BF16)
        triu_b = _ones_where(triu)
        strict_tril = jnp.logical_not(triu)
        head_ids = lax.broadcasted_iota(jnp.int32, (1, SSD_H), 1)
        d_dta_diag = jnp.zeros((SSD_L, SSD_H), F32)
        for g in range(SSD_G):
            gs = slice(g * GW, (g + 1) * GW)
            bs_ = slice(B_COL + g * SSD_N, B_COL + (g + 1) * SSD_N)
            cs_ = slice(C_COL + g * SSD_N, C_COL + (g + 1) * SSD_N)
            bg = xbc_ref[:, bs_]
            cg = xbc_ref[:, cs_]
            bg_b, cg_b = bg.astype(BF16), cg.astype(BF16)
            st_b = s_prev[:, gs].astype(BF16)
            y_off = jnp.dot(cg_b, st_b, preferred_element_type=F32) * ea_e[:, gs]
            yf_sc[:, gs] = y_off
            dz_b = (dy[:, gs] * ea_e[:, gs]).astype(BF16)
            d_c = _nt(dz_b, st_b)
            ds_prev = ds_new[:, gs] * el_e[:, gs] + jnp.dot(cg.T.astype(BF16), dz_b, preferred_element_type=F32)
            dxw_sc[:, gs] = jnp.dot(bg_b, ds_new_b[:, gs], preferred_element_type=F32)
            d_b = _nt(xw_b[:, gs], ds_new_b[:, gs])
            cb = _nt(cg_b, bg_b)
            d_g = jnp.zeros((SSD_L, SSD_L), F32)
            for pr in range(SSD_HPG // 2):
                ls = slice(g * GW + pr * LANE, g * GW + (pr + 1) * LANE)
                xp = xdt_b[:, ls]
                dyp = dy[:, ls]
                dyp_b = dy_b[:, ls]
                dxd = []
                for half in range(2):
                    h = g * SSD_HPG + pr * 2 + half
                    dec = _head_decay(acum, acum_t, h, tril)
                    m = cb * dec
                    dxd.append(jnp.dot(m.T.astype(BF16), dyp_b, preferred_element_type=F32))
                    mine = low_half if half == 0 else jnp.logical_not(low_half)
                    d_m = _nt(jnp.where(mine, dyp, 0.0).astype(BF16), xp)
                    d_g = d_g + d_m * dec
                    below = jnp.dot(triu_b, (d_m * m).astype(BF16), preferred_element_type=F32)
                    col = jnp.sum(jnp.where(strict_tril, below, 0.0), axis=1, keepdims=True)
                    d_dta_diag = d_dta_diag + col * jnp.where(head_ids == h, 1.0, 0.0)
                dxd_sc[:, ls] = jnp.where(low_half, dxd[0], dxd[1])
            d_g_b = d_g.astype(BF16)
            dxbc_ref[:, cs_] = d_c + jnp.dot(d_g_b, bg_b, preferred_element_type=F32)
            dxbc_ref[:, bs_] = d_b + jnp.dot(d_g.T.astype(BF16), cg_b, preferred_element_type=F32)
            ds_sc[:, gs] = ds_prev
        dxw = dxw_sc[...]
        dxd = dxd_sc[...]
        dw_e = xdt * dxw * w_e
        d_out = _dot_sel_r(dy * yf_sc[...], reduce_, passes=2)
        d_upd = _dot_sel_r(dw_e, reduce_, passes=2)
        d_tot_e = jnp.sum(ds_new * s_prev, axis=0, keepdims=True) * el_e
        d_tot = _dot_sel_r(jnp.broadcast_to(d_tot_e, (8, SSD_DI)), reduce_, passes=2)[0:1]
        d_dta = _dot_sel_l(triu_b, d_out) + _dot_sel_l(_ones_where(strict_tril), d_upd) + d_tot + d_dta_diag
        dxdt = dxd + dxw * w_e
        dxbc_ref[:, :SSD_DI] = dxdt * dt_e + dxe_ref[...]
        ddt_ref[...] = d_dta * a_row + _dot_sel_r(dxdt * x, reduce_, passes=2)
        _acc_store(da_ref, jnp.sum(d_dta * dt, axis=0, keepdims=True), i == 0)

    rev = lambda i: (nc - 1 - i, 0)
    return pl.pallas_call(
        body, grid=(nc,),
        in_specs=[pl.BlockSpec((SSD_L, CONV_DIM), rev), pl.BlockSpec((SSD_L, SSD_H), rev),
                  pl.BlockSpec((SSD_H, SSD_L), lambda i: (0, nc - 1 - i)), pl.BlockSpec((1, SSD_H), lambda i: (0, 0)),
                  pl.BlockSpec((SSD_H, 1), lambda i: (0, 0)),
                  pl.BlockSpec((1, SSD_N, SSD_DI), lambda i: (nc - 1 - i, 0, 0)),
                  pl.BlockSpec((SSD_L, SSD_DI), rev), pl.BlockSpec((SSD_L, SSD_DI), rev)],
        out_specs=[pl.BlockSpec((SSD_L, CONV_DIM), rev), pl.BlockSpec((SSD_L, SSD_H), rev),
                   pl.BlockSpec((1, SSD_H), lambda i: (0, 0))],
        out_shape=[jax.ShapeDtypeStruct((s, CONV_DIM), F32), jax.ShapeDtypeStruct((s, SSD_H), F32),
                   jax.ShapeDtypeStruct((1, SSD_H), F32)],
        scratch_shapes=[pltpu.VMEM((SSD_N, SSD_DI), F32), pltpu.VMEM((SSD_L, SSD_DI), F32),
                        pltpu.VMEM((SSD_L, SSD_DI), F32), pltpu.VMEM((SSD_L, SSD_DI), F32)],
        compiler_params=_cparams("arbitrary"), name="ssd_bwd",
    )(xbc, dt, dt_t, a, a_t, states, dy, dx_extra)


HBM_SPEC = pl.BlockSpec(memory_space=pltpu.HBM)
N_PEERS = N_DEV - 1


def _flip(v, f):
    return 1 - v if f else v


def _all_gather(shard):
    rows, c = shard.shape

    def body(x_ref, out_ref, send_sems, recv_sems, local_sem):
        x, y, cc = lax.axis_index("x"), lax.axis_index("y"), lax.axis_index("c")
        me, sibling = (x, y, cc), (x, y, 1 - cc)
        chips = [(1 - x, y), (x, 1 - y), (1 - x, 1 - y)]

        def slot(px, py, pc):
            return out_ref.at[4 * px + 2 * py + pc]

        def copy(k, block, to, src=None):
            return pltpu.make_async_remote_copy(
                src_ref=slot(*block) if src is None else src, dst_ref=slot(*block),
                send_sem=send_sems.at[k], recv_sem=recv_sems.at[k],
                device_id=to, device_id_type=pl.DeviceIdType.MESH)

        mine = pltpu.make_async_copy(x_ref, slot(*me), local_sem)
        mine.start()
        first = [copy(0, me, sibling, src=x_ref)]
        first += [copy(1 + j, me, (*chip, cc), src=x_ref) for j, chip in enumerate(chips)]
        for cp in first:
            cp.start()
        passed = [copy(4 + j, (*chip, cc), sibling) for j, chip in enumerate(chips)]
        for j, chip in enumerate(chips):
            copy(1 + j, (*chip, cc), me).wait_recv()
            passed[j].start()
        copy(0, sibling, me).wait_recv()
        for j, chip in enumerate(chips):
            copy(4 + j, (*chip, 1 - cc), me).wait_recv()
        for cp in first + passed:
            cp.wait_send()
        mine.wait()

    return pl.pallas_call(
        body, out_shape=jax.ShapeDtypeStruct((N_DEV, rows, c), shard.dtype),
        in_specs=[HBM_SPEC], out_specs=HBM_SPEC,
        scratch_shapes=[pltpu.SemaphoreType.DMA((N_PEERS,)), pltpu.SemaphoreType.DMA((N_PEERS,)), pltpu.SemaphoreType.DMA(())],
        name="all_gather",
    )(shard)


def _peer_copies(src_refs, out_ref, sems, gather, phase):
    send_sems, recv_sems, local_sem = sems
    x, y, cc = lax.axis_index("x"), lax.axis_index("y"), lax.axis_index("c")
    me = 4 * x + 2 * y + cc

    def pieces(block, slot):
        if gather:
            return [(src_refs[0], out_ref.at[slot])]
        out, r0 = [], 0
        for src in src_refs:
            out.append((src.at[block], out_ref.at[slot, pl.ds(r0, src.shape[1])]))
            r0 += src.shape[1]
        assert r0 == out_ref.shape[1], (r0, out_ref.shape)
        return out

    if phase == "start":
        for src, dst in pieces(me, me):
            pltpu.make_async_copy(src, dst, local_sem).start()
    for k in range(1, N_DEV):
        px, py, pc = _flip(x, k & 4), _flip(y, k & 2), _flip(cc, k & 1)
        peer = 4 * px + 2 * py + pc
        to_peer = dict(send_sem=send_sems.at[k - 1], recv_sem=recv_sems.at[k - 1],
                       device_id=(px, py, pc), device_id_type=pl.DeviceIdType.MESH)
        if phase == "start":
            for src, dst in pieces(peer, me):
                pltpu.make_async_remote_copy(src_ref=src, dst_ref=dst, **to_peer).start()
        else:
            whole = pltpu.make_async_remote_copy(src_ref=out_ref.at[peer], dst_ref=out_ref.at[peer], **to_peer)
            whole.wait_recv()
            whole.wait_send()
    if phase != "start":
        pltpu.make_async_copy(out_ref.at[me], out_ref.at[me], local_sem).wait()


def _comm_scratch():
    return [pltpu.SemaphoreType.DMA((N_PEERS,)), pltpu.SemaphoreType.DMA((N_PEERS,)), pltpu.SemaphoreType.DMA(())]


def _gather_rider(shard):
    return dict(srcs=[shard], out=jax.ShapeDtypeStruct((N_DEV,) + shard.shape, shard.dtype), gather=True)


def _exchange_out(parts):
    rows = sum(p.shape[1] for p in parts)
    return jax.ShapeDtypeStruct((N_DEV, rows) + parts[0].shape[2:], parts[0].dtype)


def _exchange_rider(parts):
    return dict(srcs=list(parts), out=_exchange_out(parts), gather=False)


def _exchange_blocks(parts):
    n = len(parts)

    def body(*refs):
        _peer_copies(refs[:n], refs[n], refs[n + 1:], False, "start")
        _peer_copies(refs[:n], refs[n], refs[n + 1:], False, "finish")

    return pl.pallas_call(
        body, out_shape=_exchange_out(parts),
        in_specs=[HBM_SPEC] * n, out_specs=HBM_SPEC, scratch_shapes=_comm_scratch(), name="exchange_blocks",
    )(*parts)


BIG = [
    ("ffn1_w13", (D_MODEL, 2 * D_FF), 1), ("ffn1_w2", (D_FF, D_MODEL), 0),
    ("w_ssd_out", (SSD_DI, D_MODEL), 0), ("w_uq", (Q_LORA, MLA_H * QK), 1), ("w_ukv", (KV_LORA, MLA_H * (NOPE + VDIM)), 1),
    ("w_mla_out", (MLA_H * VDIM, D_MODEL), 0), ("w_o", (D_MODEL, D_MODEL), 0),
    ("ffn2_w13", (D_MODEL, 2 * D_FF), 1), ("ffn2_w2", (D_FF, D_MODEL), 0), ("w_in", (D_MODEL, D_IN), 1),
]
assert all(_r % 16 == 0 for _r in [_f[0] * _f[1] // N_DEV // PACK_COLS for _, _f, _ in BIG[:-1]])
SMALL = [
    ("ln_ffn1", D_MODEL), ("ln_mix", D_MODEL), ("conv_b", CONV_DIM), ("dt_bias", SSD_H), ("a_log", SSD_H), ("d_skip", SSD_H),
    ("ssd_norm", SSD_DI), ("q_lora_norm", Q_LORA), ("kv_lora_norm", KV_LORA), ("q_norm", QK), ("k_norm", QK), ("ln_ffn2", D_MODEL),
]


def _shard_shape(full, axis):
    k, n = full
    return (k // N_DEV, n) if axis == 0 else (k, n // N_DEV)


def _shard_rows(full):
    return full[0] * full[1] // N_DEV // PACK_COLS


LAYER_ROWS = sum(_shard_rows(f) for _, f, _ in BIG)
LAYER_ROWS_PAD = -(-LAYER_ROWS // 256) * 256


def _pack_shards(shards):
    parts = [(shards[name] if axis == 0 else shards[name].T).reshape(-1, PACK_COLS) for name, _, axis in BIG]
    pad = LAYER_ROWS_PAD - LAYER_ROWS
    if pad:
        parts.append(jnp.zeros((pad, PACK_COLS), parts[0].dtype))
    return jnp.concatenate(parts, axis=0)


BIG_BY_NAME = {name: (full, axis) for name, full, axis in BIG}
BIG_NAMES = [name for name, _, _ in BIG]
EARLY = ["ffn2_w13", "ffn2_w2", "w_o", "w_mla_out"]
LATE = [name for name in BIG_NAMES if name not in EARLY]
SUM_ROWS = 128


def _part_rows(name):
    return -(-_shard_rows(BIG_BY_NAME[name][0]) // 16) * 16


def _grad_parts(grads, names):
    parts = []
    for name in names:
        part = grads[name].reshape(N_DEV, -1, PACK_COLS)
        parts.append(jnp.pad(part, ((0, 0), (0, _part_rows(name) - part.shape[1]), (0, 0))))
    return parts


def _pad_parts(parts):
    pad = -sum(p.shape[1] for p in parts) % SUM_ROWS
    return parts + ([jnp.zeros((N_DEV, pad, PACK_COLS), parts[0].dtype)] if pad else [])


def _unpack_parts(summed, names, r=0):
    out = {}
    for name in names:
        full, axis = BIG_BY_NAME[name]
        k, c = _shard_shape(full, axis)
        blk = summed[r:r + _shard_rows(full)]
        out[name] = blk.reshape(k, c) if axis == 0 else blk.reshape(c, k).T
        r += _part_rows(name)
    return out, r


def _working_shape(full, axis):
    return full if axis == 0 else full[::-1]


def _unpack_gathered(gathered):
    out, r = {}, 0
    for name, full, axis in BIG:
        n = _shard_rows(full)
        out[name] = gathered[:, r:r + n].reshape(_working_shape(full, axis))
        r += n
    return out


SMALL_COLS = sum(n for _, n in SMALL) + CONV_K * CONV_DIM
SMALL_ROWS = -(-(DEPTH * SMALL_COLS) // (8 * PACK_COLS)) * 8


def _pack_small(vals, conv_w):
    flat = jnp.concatenate([vals[name] for name, _ in SMALL] + [conv_w.reshape(DEPTH, -1)], axis=1).reshape(-1)
    flat = jnp.concatenate([flat, jnp.zeros((SMALL_ROWS * PACK_COLS - flat.shape[0],), F32)])
    return flat.reshape(SMALL_ROWS, PACK_COLS)


def _unpack_small(packed):
    flat = packed.reshape(-1)[:DEPTH * SMALL_COLS].reshape(DEPTH, SMALL_COLS)
    out, c = {}, 0
    for name, n in SMALL:
        out[name] = flat[:, c:c + n]
        c += n
    return out, flat[:, c:].reshape(DEPTH, CONV_K, CONV_DIM)


_IN_OFFS = [sum(IN_SPLIT[:i]) for i in range(len(IN_SPLIT) + 1)]


def _arrange_w_in(w_t):
    z, xbc, dt, cq, ckv, kr, gates = [w_t[_IN_OFFS[i]:_IN_OFFS[i + 1]] for i in range(len(IN_SPLIT))]
    pad = jnp.zeros((LANE - ROPE - SSD_H, w_t.shape[1]), w_t.dtype)
    return jnp.concatenate([z, gates, xbc, cq, ckv, kr, dt, pad], axis=0)


def _restore_w_in(g):
    z, gates, xbc = g[PROJ_Z:PROJ_GATES], g[PROJ_GATES:PROJ_XBC], g[PROJ_XBC:PROJ_CQ]
    cq, ckv = g[PROJ_CQ:PROJ_CKV], g[PROJ_CKV:PROJ_LAST]
    kr, dt = g[PROJ_LAST:PROJ_LAST + ROPE], g[PROJ_LAST + ROPE:PROJ_LAST + ROPE + SSD_H]
    return jnp.concatenate([z, xbc, dt, cq, ckv, kr, gates], axis=0)


def _pad_heads(w_t):
    k = w_t.shape[1]
    return jnp.pad(w_t.reshape(MLA_H, QK, k), ((0, 0), (0, HEAD_COLS - QK), (0, 0))).reshape(MLA_H * HEAD_COLS, k)


def _unpad_heads(g):
    k = g.shape[1]
    return g.reshape(MLA_H, HEAD_COLS, k)[:, :QK].reshape(MLA_H * QK, k)


def _row(v):
    return v.reshape(1, -1)


def _head_gain(g):
    return jnp.pad(g, (0, HEAD_COLS - QK)).reshape(1, HEAD_COLS)


def _ffn_fwd(h, ln, w13_t, w2, name):
    n = _row_fwd(_f_rmsnorm, [h], [_row(ln)], [BF16], name + "_fwd")[0]
    act, gate, up = _ffn_up_call(n, w13_t)
    return _mm(act, w2, alpha=0.5, res=h), (h, n, gate, up, act)


def _ffn_bwd(dh_out, saved, ln, w13_t, w2, name):
    h, n, gate, up, act = saved
    d_gate, d_up = _ffn_down_bwd_call(dh_out, w2, gate, up)
    d_w2 = _mm(act, dh_out, ta=True, out_dtype=BF16, alpha=0.5)
    d_n = _mm(d_gate, w13_t, b_rows=(0, D_FF))
    dh, d_ln = _mm(d_up, w13_t, b_rows=(D_FF, D_FF), res=d_n, norm_bwd=(h, _row(ln), dh_out))
    d_w13_t = jnp.concatenate([_mm(d_gate, n, ta=True, out_dtype=BF16), _mm(d_up, n, ta=True, out_dtype=BF16)], axis=0)
    return dh, d_w13_t, d_w2, d_ln[0]


def _mixer_fwd(h, big, small, conv_w, cs, sn, rider=None):
    s = h.shape[0]
    u = _row_fwd(_f_rmsnorm, [h], [_row(small["ln_mix"])], [BF16], "ln_mix_fwd")[0]
    proj = _mm(u, big["w_in"], tb=True)
    xbc = _conv_fwd_call(proj, PROJ_XBC, conv_w, _row(small["conv_b"]))
    dt_in = proj[:, PROJ_LAST + ROPE:PROJ_LAST + ROPE + SSD_H] + small["dt_bias"][None, :]
    dt = jax.nn.softplus(dt_in)
    a = -jnp.exp(small["a_log"])[None, :]
    y_scan, states = _ssd_fwd_call(xbc, dt, a)
    dsk = _row(jnp.repeat(small["d_skip"], SSD_P))
    gn_in = [y_scan, _win(xbc, 0, SSD_DI), _win(proj, PROJ_Z, SSD_DI)]
    yn = _row_fwd(_f_gated_norm, gn_in, [dsk, _row(small["ssd_norm"])], [BF16], "gated_norm_fwd")[0]
    y_ssd = _mm(yn, big["w_ssd_out"])
    qn = _row_fwd(_f_rmsnorm, [_win(proj, PROJ_CQ, Q_LORA)], [_row(small["q_lora_norm"])], [BF16], "q_lora_norm_fwd")[0]
    kvn = _row_fwd(_f_rmsnorm, [_win(proj, PROJ_CKV, KV_LORA)], [_row(small["kv_lora_norm"])], [BF16], "kv_lora_norm_fwd")[0]
    q = _mm(qn, big["w_uq"], tb=True)
    kv = _mm(kvn, big["w_ukv"], tb=True)
    qh, kh, vh = _heads_fwd_call(q, kv, proj, cs, sn, _head_gain(small["q_norm"]), _head_gain(small["k_norm"]))
    o, lse, *carried = _attn_fwd_call(qh, kh, vh, rider)
    o_rows = jnp.transpose(o, (1, 0, 2)).reshape(s, MLA_H * VDIM)
    y_mla = _mm(o_rows, big["w_mla_out"])
    out, mg = _merge_out_call(proj, y_ssd, y_mla, big["w_o"], h)
    saved = (h, u, proj, xbc, dt_in, dt, a, y_scan, states, dsk, yn, y_ssd, qn, kvn, q, kv, qh, kh, vh, o, lse, o_rows, y_mla, mg)
    return out, saved, (carried[0] if carried else None)


def _mixer_bwd(dh_out, saved, big, small, conv_w, cs, sn, carry_parts=None):
    (h, u, proj, xbc, dt_in, dt, a, y_scan, states, dsk, yn, y_ssd, qn, kvn, q, kv, qh, kh, vh, o, lse, o_rows, y_mla, mg) = saved
    s = h.shape[0]
    d_big, d_small = {}, {}
    d_gates, d_y_ssd, d_y_mla = _merge_out_bwd_call(dh_out, big["w_o"], proj, y_ssd, y_mla)
    d_big["w_o"] = _mm(mg, dh_out, ta=True, out_dtype=BF16)
    d_o_rows = _mm(d_y_mla, big["w_mla_out"], tb=True, out_dtype=BF16)
    d_big["w_mla_out"] = _mm(o_rows, d_y_mla, ta=True, out_dtype=BF16)
    d_o = jnp.transpose(d_o_rows.reshape(s, MLA_H, VDIM), (1, 0, 2))
    delta = _attn_delta_call(o, d_o)
    rider = None
    if carry_parts is not None:
        rider = _exchange_rider(_pad_parts(carry_parts + _grad_parts(d_big, EARLY[2:])))
    *d_heads, carried = list(_attn_bwd_call(qh, kh, vh, d_o, lse.reshape(MLA_H, 1, s), delta.reshape(MLA_H, 1, s), rider)) + ([None] if rider is None else [])
    d_q, d_kv, d_kr, d_qg, d_kg = _heads_bwd_call(
        q, kv, proj, cs, sn, _head_gain(small["q_norm"]), _head_gain(small["k_norm"]), *d_heads)
    d_small["q_norm"], d_small["k_norm"] = d_qg[0, :QK], d_kg[0, :QK]
    d_qn = _mm(d_q, big["w_uq"], out_dtype=BF16)
    d_big["w_uq"] = _mm(d_q, qn, ta=True, out_dtype=BF16)
    d_kvn = _mm(d_kv, big["w_ukv"], out_dtype=BF16)
    d_big["w_ukv"] = _mm(d_kv, kvn, ta=True, out_dtype=BF16)
    (d_cq,), (d_g,) = _row_bwd(_f_rmsnorm, [_win(proj, PROJ_CQ, Q_LORA)], [_row(small["q_lora_norm"])], [d_qn], [BF16], "q_lora_norm_bwd")
    d_small["q_lora_norm"] = d_g[0]
    (d_ckv,), (d_g,) = _row_bwd(_f_rmsnorm, [_win(proj, PROJ_CKV, KV_LORA)], [_row(small["kv_lora_norm"])], [d_kvn], [BF16], "kv_lora_norm_bwd")
    d_small["kv_lora_norm"] = d_g[0]
    d_yn = _mm(d_y_ssd, big["w_ssd_out"], tb=True, out_dtype=BF16)
    d_big["w_ssd_out"] = _mm(yn, d_y_ssd, ta=True, out_dtype=BF16)
    gn_in = [y_scan, _win(xbc, 0, SSD_DI), _win(proj, PROJ_Z, SSD_DI)]
    (d_y_scan, d_xs, d_z), (d_dsk, d_g) = _row_bwd(
        _f_gated_norm, gn_in, [dsk, _row(small["ssd_norm"])], [d_yn], [F32, F32, BF16], "gated_norm_bwd")
    d_small["ssd_norm"] = d_g[0]
    d_small["d_skip"] = jnp.sum(d_dsk.reshape(SSD_H, SSD_P), axis=1)
    d_xbc_act, d_dt, d_a = _ssd_bwd_call(xbc, dt, a, states, d_y_scan, d_xs)
    d_xbc, d_conv_w, d_conv_b = _conv_bwd_call(proj, PROJ_XBC, conv_w, _row(small["conv_b"]), d_xbc_act)
    d_small["conv_b"] = d_conv_b[0]
    d_dt_in = d_dt * jax.nn.sigmoid(dt_in)
    d_small["dt_bias"] = jnp.sum(d_dt_in, axis=0)
    d_small["a_log"] = d_a[0] * a[0]
    d_last = (d_kr + jnp.pad(d_dt_in, ((0, 0), (ROPE, LANE - ROPE - SSD_H)))).astype(BF16)
    d_proj = jnp.concatenate([d_z, d_gates, d_xbc, d_cq, d_ckv, d_last], axis=1)
    dh, d_ln = _mm(d_proj, big["w_in"], norm_bwd=(h, _row(small["ln_mix"]), dh_out))
    d_big["w_in"] = _mm(d_proj, u, ta=True, out_dtype=BF16)
    d_small["ln_mix"] = d_ln[0]
    return dh, d_big, d_small, d_conv_w, carried


def _prepare_big(b):
    return dict(b, w_in=_arrange_w_in(b["w_in"]), w_uq=_pad_heads(b["w_uq"]))


def _local_step(x, positions, target, big, small, conv_w, packed_last=None):
    inv = 1.0 / (ROPE_THETA ** (jnp.arange(0, ROPE, 2, dtype=F32) / ROPE))
    ang = positions.astype(F32)[:, None] * inv
    cos, sin = jnp.cos(ang), jnp.sin(ang)
    no_lanes = jnp.zeros((x.shape[0], LANE - ROPE), F32)
    cs = jnp.concatenate([cos, cos, no_lanes], axis=1)
    sn = jnp.concatenate([-sin, sin, no_lanes], axis=1)
    carrier = DEPTH - 2 if packed_last is not None else None
    big = [None if b is None else _prepare_big(b) for b in big]
    layer_small = [{k: v[l] for k, v in small.items()} for l in range(DEPTH)]

    h, saved = x, []
    for l in range(DEPTH):
        b, sm = big[l], layer_small[l]
        h, s1 = _ffn_fwd(h, sm["ln_ffn1"], b["ffn1_w13"], b["ffn1_w2"], "ln_ffn1")
        h, s2, gathered = _mixer_fwd(h, b, sm, conv_w[l], cs, sn, _gather_rider(packed_last) if l == carrier else None)
        if gathered is not None:
            big[l + 1] = _prepare_big(_unpack_gathered(gathered))
        h, s3 = _ffn_fwd(h, sm["ln_ffn2"], b["ffn2_w13"], b["ffn2_w2"], "ln_ffn2")
        saved.append((s1, s2, s3))
    loss, dh = _loss_and_grad(h, target)

    d_big, d_small, d_conv_w = [None] * DEPTH, [None] * DEPTH, [None] * DEPTH
    for l in reversed(range(DEPTH)):
        b, sm = big[l], layer_small[l]
        s1, s2, s3 = saved[l]
        dh, d_w13_2, d_w2_2, d_ln2 = _ffn_bwd(dh, s3, sm["ln_ffn2"], b["ffn2_w13"], b["ffn2_w2"], "ln_ffn2")
        carry_parts = None
        if l == carrier:
            carry_parts = _grad_parts(d_big[l + 1], BIG_NAMES) + _grad_parts({"ffn2_w13": d_w13_2, "ffn2_w2": d_w2_2}, EARLY[:2])
        dh, db, ds, d_conv_w[l], received = _mixer_bwd(dh, s2, b, sm, conv_w[l], cs, sn, carry_parts)
        if received is not None:
            d_big[l + 1] = received
        dh, d_w13_1, d_w2_1, d_ln1 = _ffn_bwd(dh, s1, sm["ln_ffn1"], b["ffn1_w13"], b["ffn1_w2"], "ln_ffn1")
        db.update(ffn1_w13=d_w13_1, ffn1_w2=d_w2_1, ffn2_w13=d_w13_2, ffn2_w2=d_w2_2,
                  w_in=_restore_w_in(db["w_in"]), w_uq=_unpad_heads(db["w_uq"]))
        ds.update(ln_ffn1=d_ln1, ln_ffn2=d_ln2)
        d_big[l], d_small[l] = db, ds
    d_small = {name: jnp.stack([d_small[l][name] for l in range(DEPTH)]) for name, _ in SMALL}
    return loss, dh, d_big, d_small, jnp.stack(d_conv_w)


def _step(args):
    dev = 4 * lax.axis_index("x") + 2 * lax.axis_index("y") + lax.axis_index("c")
    x, positions, target = args["x"][0], args["positions"][0], args["loss_target"][0]

    packed = [_pack_shards({name: args[name][l].astype(BF16) for name, _, _ in BIG}) for l in range(DEPTH)]
    big = [_unpack_gathered(_all_gather(packed[l])) for l in range(DEPTH - 1)] + [None]
    cw = args["conv_w"]
    cw_cols = cw.shape[-1]
    cw_rows = -(-cw.size // (8 * PACK_COLS)) * 8
    cw_flat = jnp.concatenate([cw.reshape(-1), jnp.zeros((cw_rows * PACK_COLS - cw.size,), F32)]).reshape(cw_rows, PACK_COLS)
    cw_all = _all_gather(cw_flat).reshape(N_DEV, -1)[:, :cw.size].reshape(N_DEV, DEPTH, CONV_K, cw_cols)
    conv_w = jnp.transpose(cw_all, (1, 2, 0, 3)).reshape(DEPTH, CONV_K, CONV_DIM)
    small = {name: args[name] for name, _ in SMALL}

    loss, dx, d_big, d_small, d_conv_w = _local_step(x, positions, target, big, small, conv_w, packed_last=packed[-1])
    loss = lax.psum(loss, MESH_AXES)

    out = {"loss": loss, "grad_x": dx[None]}

    assert DEPTH == 2
    grads = {name: [None] * DEPTH for name in BIG_NAMES}
    summed = _sum_blocks(d_big[1])
    own, r = _unpack_parts(summed, BIG_NAMES)
    early, _ = _unpack_parts(summed, EARLY, r)
    late, _ = _unpack_parts(_sum_blocks(_exchange_blocks(_pad_parts(_grad_parts(d_big[0], LATE)))), LATE)
    for name in BIG_NAMES:
        grads[name] = [early[name] if name in EARLY else late[name], own[name]]
    flat = lambda t: t.reshape(-1, t.shape[-1])
    for name, _, _ in BIG:
        g = jnp.stack(grads[name])
        w = args[name]
        delta, m2, v2 = _adam(flat(w), flat(g), flat(args["m_" + name]), flat(args["v_" + name]))
        out["grad_" + name] = g
        out["delta_" + name] = delta.reshape(w.shape)
        out["new_m_" + name] = m2.reshape(w.shape)
        out["new_v_" + name] = v2.reshape(w.shape)

    total = _sum_blocks(_all_gather(_pack_small(d_small, d_conv_w)))
    g_conv_w = _unpack_small(total)[1]
    zeros_cw = jnp.zeros((DEPTH, CONV_K, CONV_DIM), F32)
    delta, m2, v2 = _adam(_pack_small(small, zeros_cw), total,
                          _pack_small({name: args["m_" + name] for name, _ in SMALL}, zeros_cw),
                          _pack_small({name: args["v_" + name] for name, _ in SMALL}, zeros_cw))
    for kind, packed in (("grad_", total), ("delta_", delta), ("new_m_", m2), ("new_v_", v2)):
        for name, val in _unpack_small(packed)[0].items():
            out[kind + name] = val
    g_cw = lax.dynamic_slice_in_dim(g_conv_w, dev * cw_cols, cw_cols, axis=2)
    delta, m2, v2 = _adam(flat(cw), flat(g_cw), flat(args["m_conv_w"]), flat(args["v_conv_w"]))
    out["grad_conv_w"] = g_cw
    out["delta_conv_w"] = delta.reshape(cw.shape)
    out["new_m_conv_w"] = m2.reshape(cw.shape)
    out["new_v_conv_w"] = v2.reshape(cw.shape)
    return out


WEIGHTS = ["ln_ffn1", "ffn1_w13", "ffn1_w2", "ln_mix", "w_in", "conv_w", "conv_b", "dt_bias", "a_log", "d_skip", "ssd_norm",
           "w_ssd_out", "q_lora_norm", "w_uq", "kv_lora_norm", "w_ukv", "q_norm", "k_norm", "w_mla_out", "w_o", "ln_ffn2",
           "ffn2_w13", "ffn2_w2"]
ARG_NAMES = (["x", "positions"] + WEIGHTS + ["loss_target"] + ["m_" + n for n in WEIGHTS] + ["v_" + n for n in WEIGHTS])


def kernel(x, positions, ln_ffn1, ffn1_w13, ffn1_w2, ln_mix, w_in, conv_w, conv_b, dt_bias, a_log, d_skip, ssd_norm, w_ssd_out, q_lora_norm, w_uq, kv_lora_norm, w_ukv, q_norm, k_norm, w_mla_out, w_o, ln_ffn2, ffn2_w13, ffn2_w2, loss_target, m_ln_ffn1, m_ffn1_w13, m_ffn1_w2, m_ln_mix, m_w_in, m_conv_w, m_conv_b, m_dt_bias, m_a_log, m_d_skip, m_ssd_norm, m_w_ssd_out, m_q_lora_norm, m_w_uq, m_kv_lora_norm, m_w_ukv, m_q_norm, m_k_norm, m_w_mla_out, m_w_o, m_ln_ffn2, m_ffn2_w13, m_ffn2_w2, v_ln_ffn1, v_ffn1_w13, v_ffn1_w2, v_ln_mix, v_w_in, v_conv_w, v_conv_b, v_dt_bias, v_a_log, v_d_skip, v_ssd_norm, v_w_ssd_out, v_q_lora_norm, v_w_uq, v_kv_lora_norm, v_w_ukv, v_q_norm, v_k_norm, v_w_mla_out, v_w_o, v_ln_ffn2, v_ffn2_w13, v_ffn2_w2):
    vals = (x, positions, ln_ffn1, ffn1_w13, ffn1_w2, ln_mix, w_in, conv_w, conv_b, dt_bias, a_log, d_skip, ssd_norm, w_ssd_out, q_lora_norm, w_uq, kv_lora_norm, w_ukv, q_norm, k_norm, w_mla_out, w_o, ln_ffn2, ffn2_w13, ffn2_w2, loss_target, m_ln_ffn1, m_ffn1_w13, m_ffn1_w2, m_ln_mix, m_w_in, m_conv_w, m_conv_b, m_dt_bias, m_a_log, m_d_skip, m_ssd_norm, m_w_ssd_out, m_q_lora_norm, m_w_uq, m_kv_lora_norm, m_w_ukv, m_q_norm, m_k_norm, m_w_mla_out, m_w_o, m_ln_ffn2, m_ffn2_w13, m_ffn2_w2, v_ln_ffn1, v_ffn1_w13, v_ffn1_w2, v_ln_mix, v_w_in, v_conv_w, v_conv_b, v_dt_bias, v_a_log, v_d_skip, v_ssd_norm, v_w_ssd_out, v_q_lora_norm, v_w_uq, v_kv_lora_norm, v_w_ukv, v_q_norm, v_k_norm, v_w_mla_out, v_w_o, v_ln_ffn2, v_ffn2_w13, v_ffn2_w2)
    out = _step(dict(zip(ARG_NAMES, vals)))
    order = ["loss", "grad_x"] + [k + n for k in ("grad_", "delta_", "new_m_", "new_v_") for n in WEIGHTS]
    return tuple(out[n] for n in order)
```

```python
import jax
import jax.numpy as jnp
from jax import lax
from jax.experimental import pallas as pl
from jax.experimental.pallas import tpu as pltpu

F32 = jnp.float32
BF16 = jnp.bfloat16

D_MODEL = 1024
D_FF = 2816
DEPTH = 2
SSD_DI = 2048
SSD_P = 64
SSD_H = 32
SSD_G = 4
SSD_HPG = 8
SSD_N = 128
SSD_L = 128
CONV_K = 4
CONV_DIM = 3072
MLA_H = 8
Q_LORA = 512
KV_LORA = 256
NOPE = 128
ROPE = 64
VDIM = 128
QK = 192
ROPE_THETA = 10000.0
EPS = 1e-6
IN_SPLIT = (SSD_DI, CONV_DIM, SSD_H, Q_LORA, KV_LORA, ROPE, 2 * D_MODEL)
D_IN = sum(IN_SPLIT)
N_DEV = 8
LANE = 128
PACK_COLS = 1024

PROJ_Z = 0
PROJ_GATES = PROJ_Z + SSD_DI
PROJ_XBC = PROJ_GATES + 2 * D_MODEL
PROJ_CQ = PROJ_XBC + CONV_DIM
PROJ_CKV = PROJ_CQ + Q_LORA
PROJ_LAST = PROJ_CKV + KV_LORA
D_IN_PAD = PROJ_LAST + LANE

ADAM_LR = 0.001
ADAM_B1 = 0.9
ADAM_B2 = 0.999
ADAM_EPS = 1e-08
ADAM_WD = 0.01
ADAM_STEP = 10

VMEM_LIMIT = 48 * 1024 * 1024
ROW_IO_BUDGET = 8 * 1024 * 1024
NEG = -1e30

MESH_AXES = ("x", "y", "c")


def _cparams(*sem):
    return pltpu.CompilerParams(dimension_semantics=sem, vmem_limit_bytes=VMEM_LIMIT)


def _pick_tile(n, target, align):
    if n <= target:
        return n
    best = None
    for t in range(align, target + 1, align):
        if n % t == 0:
            best = t
    assert best is not None, (n, target, align)
    return best


def _acc_store(ref, val, first):
    @pl.when(first)
    def _():
        ref[...] = val

    @pl.when(jnp.logical_not(first))
    def _():
        ref[...] += val


def _win(arr, start, width):
    assert start % width == 0, (start, width)
    return (arr, start, width)


def _operand(entry):
    if isinstance(entry, tuple):
        arr, start, width = entry
        return arr, width, start // width
    return entry, entry.shape[1], 0


def _row_tile(rows, bytes_per_row):
    if rows <= 16:
        return rows
    t = 1024
    while t > 16 and (t * bytes_per_row > ROW_IO_BUDGET or rows % t):
        t //= 2
    assert rows % t == 0, (rows, t)
    return t


def _rowwise_call(fn, tiled, params, outs, accs, name):
    ops = [_operand(e) for e in tiled]
    rows = ops[0][0].shape[0]
    per_row = sum(w * a.dtype.itemsize for a, w, _ in ops) + sum(c * jnp.dtype(d).itemsize for c, d in outs)
    tile = _row_tile(rows, per_row)
    n_in = len(tiled) + len(params)
    n_o = len(outs)

    def body(*refs):
        vals = [r[...] for r in refs[:n_in]]
        t_out, a_out = fn(*vals)
        for r, v in zip(refs[n_in:n_in + n_o], t_out):
            r[...] = v.astype(r.dtype)
        first = pl.program_id(0) == 0
        for r, v in zip(refs[n_in + n_o:], a_out):
            _acc_store(r, v.astype(F32), first)

    def tiled_spec(width, blk):
        return pl.BlockSpec((tile, width), lambda i: (i, blk))

    in_specs = [tiled_spec(w, blk) for _, w, blk in ops]
    in_specs += [pl.BlockSpec(p.shape, lambda i: (0, 0)) for p in params]
    out_specs = [tiled_spec(c, 0) for c, _ in outs]
    out_specs += [pl.BlockSpec(s, lambda i: (0, 0)) for s in accs]
    out_shape = [jax.ShapeDtypeStruct((rows, c), d) for c, d in outs]
    out_shape += [jax.ShapeDtypeStruct(s, F32) for s in accs]
    return pl.pallas_call(
        body, grid=(rows // tile,), in_specs=in_specs, out_specs=out_specs, out_shape=out_shape,
        compiler_params=_cparams("arbitrary"), name=name,
    )(*[a for a, _, _ in ops], *params)


def _to_f32(vals):
    return [v.astype(F32) for v in vals]


def _row_fwd(f, tiled, params, out_dtypes, name):
    ops = [_operand(e) for e in tiled]
    rows = ops[0][0].shape[0]
    shapes = jax.eval_shape(f, *[jax.ShapeDtypeStruct((rows, w), F32) for _, w, _ in ops],
                            *[jax.ShapeDtypeStruct(p.shape, F32) for p in params])
    outs = [(s.shape[1], d) for s, d in zip(shapes, out_dtypes)]
    return _rowwise_call(lambda *v: (f(*_to_f32(v)), ()), tiled, params, outs, [], name)


def _row_bwd(f, tiled, params, gs, d_dtypes, name, bwd=None, add=None):
    n_t, n_g = len(tiled), len(gs)
    adds = sorted((add or {}).items())
    n_a = len(adds)

    def fn(*vals):
        vals = _to_f32(vals)
        prim = vals[:n_t] + vals[n_t + n_g + n_a:]
        g = tuple(vals[n_t:n_t + n_g])
        if bwd is not None:
            d_t, d_p = bwd(*prim, *g)
        else:
            _, vjp = jax.vjp(f, *prim)
            cts = vjp(g)
            d_t, d_p = cts[:n_t], cts[n_t:]
        d_t = list(d_t)
        for (idx, _), extra in zip(adds, vals[n_t + n_g:n_t + n_g + n_a]):
            d_t[idx] = d_t[idx] + extra
        return tuple(d_t), tuple(d_p)

    outs = [(_operand(e)[1], d) for e, d in zip(tiled, d_dtypes)]
    accs = [p.shape for p in params]
    res = _rowwise_call(fn, list(tiled) + list(gs) + [a for _, a in adds], params, outs, accs, name)
    return res[:n_t], res[n_t:]


def _f_rmsnorm(x, g):
    return (x * lax.rsqrt(jnp.mean(x * x, axis=-1, keepdims=True) + EPS) * g,)


def _f_gated_norm(ys, xs, z, dsk, g):
    t = (ys + xs * dsk) * (z * jax.nn.sigmoid(z))
    return (t * lax.rsqrt(jnp.mean(t * t, axis=-1, keepdims=True) + EPS) * g,)


def _f_merge(gates, ys, ym):
    s = jax.nn.sigmoid(gates)
    return (s[:, :D_MODEL] * ys + s[:, D_MODEL:] * ym,)


def _b_merge(gates, ys, ym, d):
    s = jax.nn.sigmoid(gates)
    s1, s2 = s[:, :D_MODEL], s[:, D_MODEL:]
    d_gates = jnp.concatenate([d * ys * s1 * (1.0 - s1), d * ym * s2 * (1.0 - s2)], axis=1)
    return (d_gates, d * s1, d * s2), ()


def _loss_and_grad(y, target):
    def fn(yv, tv):
        d = yv - tv
        return (d * (1.0 / D_MODEL),), (jnp.sum(d * d, axis=0, keepdims=True) * (0.5 / D_MODEL),)

    dy, part = _rowwise_call(fn, [y, target], [], [(D_MODEL, F32)], [(1, D_MODEL)], "loss")
    return jnp.sum(part), dy


def _adam(w, g, m, v):
    def fn(wv, gv, mv, vv):
        m2 = ADAM_B1 * mv + (1.0 - ADAM_B1) * gv
        v2 = ADAM_B2 * vv + (1.0 - ADAM_B2) * (gv * gv)
        m_hat = m2 / (1.0 - ADAM_B1 ** ADAM_STEP)
        v_hat = v2 / (1.0 - ADAM_B2 ** ADAM_STEP)
        delta = -ADAM_LR * (m_hat / (jnp.sqrt(v_hat) + ADAM_EPS) + ADAM_WD * wv)
        return (delta, m2, v2), ()

    c = w.shape[1]
    return _rowwise_call(fn, [w, g, m, v], [], [(c, F32)] * 3, [], "adamw")


def _sum_blocks(blocks):
    _, rows, c = blocks.shape
    tile = _row_tile(rows, N_DEV * c * blocks.dtype.itemsize + c * 4)

    def body(b_ref, o_ref):
        acc = b_ref[0].astype(F32)
        for i in range(1, N_DEV):
            acc = acc + b_ref[i].astype(F32)
        o_ref[...] = acc

    return pl.pallas_call(
        body, grid=(rows // tile,), in_specs=[pl.BlockSpec((N_DEV, tile, c), lambda i: (0, i, 0))],
        out_specs=pl.BlockSpec((tile, c), lambda i: (i, 0)), out_shape=jax.ShapeDtypeStruct((rows, c), F32),
        compiler_params=_cparams("arbitrary"), name="sum_blocks",
    )(blocks)


def _mm(a, b, ta=False, tb=False, out_dtype=F32, alpha=1.0, res=None, b_rows=None, norm_bwd=None):
    r_dim, p_dim = a.shape if ta else a.shape[::-1]
    b_row0, b_nrows = (0, b.shape[0]) if b_rows is None else b_rows
    r2, q_dim = (b.shape[1], b_nrows) if tb else (b_nrows, b.shape[1])
    assert r_dim == r2, (a.shape, b.shape, ta, tb)
    tp = _pick_tile(p_dim, 512, LANE)
    if tp < 512 < p_dim:
        tp = _pick_tile(p_dim, 1536, LANE)
    tq = _pick_tile(q_dim, 1536, LANE)
    tr = _pick_tile(r_dim, 1536, LANE)
    nr = r_dim // tr
    dims = (((0 if ta else 1,), (1 if tb else 0,)), ((), ()))
    has_res = res is not None
    n_nb = 0 if norm_bwd is None else 3
    assert norm_bwd is None or tq == q_dim

    def body(*refs):
        a_ref, b_ref = refs[:2]
        res_ref = refs[2] if has_res else None
        n_in = 2 + has_res + n_nb
        o_ref = refs[n_in]

        def finish(val):
            if alpha != 1.0:
                val = val * alpha
            if has_res:
                val = val + res_ref[...].astype(F32)
            if norm_bwd is not None:
                x_ref, g_ref, add_ref = refs[2 + has_res:n_in]
                x = x_ref[...]
                r = lax.rsqrt(jnp.mean(x * x, axis=-1, keepdims=True) + EPS)
                gy = val * g_ref[...]
                dot = jnp.sum(gy * x, axis=-1, keepdims=True)
                _acc_store(refs[n_in + 1], jnp.sum(val * x * r, axis=0, keepdims=True), pl.program_id(1) == 0)
                val = gy * r - x * (dot * (r * r * r) * (1.0 / q_dim)) + add_ref[...]
            o_ref[...] = val.astype(o_ref.dtype)

        part = lax.dot_general(a_ref[...].astype(BF16), b_ref[...].astype(BF16), dims, preferred_element_type=F32)
        if nr == 1:
            finish(part)
        else:
            acc_ref = refs[-1]
            k = pl.program_id(2)
            _acc_store(acc_ref, part, k == 0)

            @pl.when(k == nr - 1)
            def _():
                finish(acc_ref[...])

    a_spec = pl.BlockSpec((tr, tp), lambda j, i, k: (k, i)) if ta else pl.BlockSpec((tp, tr), lambda j, i, k: (i, k))
    assert b_row0 % (tq if tb else tr) == 0
    b0 = b_row0 // (tq if tb else tr)
    b_spec = pl.BlockSpec((tq, tr), lambda j, i, k: (j + b0, k)) if tb else pl.BlockSpec((tr, tq), lambda j, i, k: (k + b0, j))
    o_spec = pl.BlockSpec((tp, tq), lambda j, i, k: (i, j))
    row_spec = pl.BlockSpec((1, tq), lambda j, i, k: (0, 0))
    in_specs = [a_spec, b_spec] + ([o_spec] if has_res else []) + ([o_spec, row_spec, o_spec] if n_nb else [])
    out = pl.pallas_call(
        body, grid=(q_dim // tq, p_dim // tp, nr), in_specs=in_specs,
        out_specs=[o_spec] + ([row_spec] if n_nb else []),
        out_shape=[jax.ShapeDtypeStruct((p_dim, q_dim), out_dtype)] + ([jax.ShapeDtypeStruct((1, q_dim), F32)] if n_nb else []),
        scratch_shapes=[pltpu.VMEM((tp, tq), F32)] if nr > 1 else [],
        compiler_params=_cparams("arbitrary", "arbitrary", "arbitrary"),
        name=f"mm_{'t' if ta else 'n'}{'t' if tb else 'n'}_{p_dim}x{r_dim}x{q_dim}" + ("_norm_bwd" if n_nb else ""),
    )(*([a, b] + ([res] if has_res else []) + (list(norm_bwd) if n_nb else [])))
    return out if n_nb else out[0]


MERGE_TP = 256


def _merge_out_call(proj, y_ssd, y_mla, w_o, h):
    s = h.shape[0]
    tp = min(MERGE_TP, s)

    def body(g_ref, ys_ref, ym_ref, w_ref, h_ref, o_ref, mg_ref):
        mg = _f_merge(g_ref[...], ys_ref[...], ym_ref[...])[0].astype(BF16)
        mg_ref[...] = mg
        o_ref[...] = h_ref[...] + jnp.dot(mg, w_ref[...], preferred_element_type=F32)

    rows = pl.BlockSpec((tp, D_MODEL), lambda i: (i, 0))
    return pl.pallas_call(
        body, grid=(s // tp,),
        in_specs=[pl.BlockSpec((tp, 2 * D_MODEL), lambda i: (i, PROJ_GATES // (2 * D_MODEL))), rows, rows,
                  pl.BlockSpec((D_MODEL, D_MODEL), lambda i: (0, 0)), rows],
        out_specs=[rows, rows],
        out_shape=[jax.ShapeDtypeStruct((s, D_MODEL), F32), jax.ShapeDtypeStruct((s, D_MODEL), BF16)],
        compiler_params=_cparams("arbitrary"), name="merge_out",
    )(proj, y_ssd, y_mla, w_o, h)


def _merge_out_bwd_call(dh, w_o, proj, y_ssd, y_mla):
    s = dh.shape[0]
    tp = min(MERGE_TP, s)

    def body(dh_ref, w_ref, g_ref, ys_ref, ym_ref, dg_ref, dys_ref, dym_ref):
        d_mg = _nt(dh_ref[...].astype(BF16), w_ref[...])
        (d_g, d_ys, d_ym), _ = _b_merge(g_ref[...], ys_ref[...], ym_ref[...], d_mg)
        dg_ref[...] = d_g.astype(BF16)
        dys_ref[...] = d_ys.astype(BF16)
        dym_ref[...] = d_ym.astype(BF16)

    rows = pl.BlockSpec((tp, D_MODEL), lambda i: (i, 0))
    wide = pl.BlockSpec((tp, 2 * D_MODEL), lambda i: (i, 0))
    return pl.pallas_call(
        body, grid=(s // tp,),
        in_specs=[rows, pl.BlockSpec((D_MODEL, D_MODEL), lambda i: (0, 0)),
                  pl.BlockSpec((tp, 2 * D_MODEL), lambda i: (i, PROJ_GATES // (2 * D_MODEL))), rows, rows],
        out_specs=[wide, rows, rows],
        out_shape=[jax.ShapeDtypeStruct((s, 2 * D_MODEL), BF16), jax.ShapeDtypeStruct((s, D_MODEL), BF16),
                   jax.ShapeDtypeStruct((s, D_MODEL), BF16)],
        compiler_params=_cparams("arbitrary"), name="merge_out_bwd",
    )(dh, w_o, proj, y_ssd, y_mla)


def _attn_out_bwd_call(d_y, w_out, o_rows):
    s = d_y.shape[0]
    tp = min(MERGE_TP, s)
    wide = MLA_H * VDIM

    def body(dy_ref, w_ref, o_ref, do_ref, delta_ref):
        d_o = _nt(dy_ref[...], w_ref[...]).astype(BF16)
        do_ref[...] = d_o
        col = lax.broadcasted_iota(jnp.int32, (wide, MLA_H), 0)
        head = lax.broadcasted_iota(jnp.int32, (wide, MLA_H), 1)
        per_head = _ones_where(lax.shift_right_logical(col, VDIM.bit_length() - 1) == head)
        delta_ref[...] = _dot_sel_r(d_o.astype(F32) * o_ref[...].astype(F32), per_head)

    rows = lambda c: pl.BlockSpec((tp, c), lambda i: (i, 0))
    return pl.pallas_call(
        body, grid=(s // tp,),
        in_specs=[rows(D_MODEL), pl.BlockSpec((wide, D_MODEL), lambda i: (0, 0)), rows(wide)],
        out_specs=[rows(wide), rows(MLA_H)],
        out_shape=[jax.ShapeDtypeStruct((s, wide), BF16), jax.ShapeDtypeStruct((s, MLA_H), F32)],
        compiler_params=_cparams("arbitrary"), name="attn_out_bwd",
    )(d_y, w_out, o_rows)


FFN_TP = 512
FFN_TQ = 1408


def _ffn_up_call(n, w13_t):
    s, d = n.shape
    tp = min(FFN_TP, s)
    up0 = D_FF // FFN_TQ

    def body(n_ref, wg_ref, wu_ref, act_ref, gate_ref, up_ref):
        a = n_ref[...]
        g = _nt(a, wg_ref[...])
        u = _nt(a, wu_ref[...])
        act_ref[...] = (g * jax.nn.sigmoid(g) * u).astype(BF16)
        gate_ref[...] = g.astype(BF16)
        up_ref[...] = u.astype(BF16)

    o_spec = pl.BlockSpec((tp, FFN_TQ), lambda j, i: (i, j))
    return pl.pallas_call(
        body, grid=(D_FF // FFN_TQ, s // tp),
        in_specs=[pl.BlockSpec((tp, d), lambda j, i: (i, 0)), pl.BlockSpec((FFN_TQ, d), lambda j, i: (j, 0)),
                  pl.BlockSpec((FFN_TQ, d), lambda j, i: (j + up0, 0))],
        out_specs=[o_spec] * 3, out_shape=[jax.ShapeDtypeStruct((s, D_FF), BF16)] * 3,
        compiler_params=_cparams("arbitrary", "arbitrary"), name="ffn_up_swiglu",
    )(n, w13_t, w13_t)


def _ffn_down_bwd_call(dh, w2, gate, up):
    s, d = dh.shape
    tp = min(FFN_TP, s)

    def body(dh_ref, w2_ref, gate_ref, up_ref, dg_ref, du_ref):
        d_act = 0.5 * _nt(dh_ref[...].astype(BF16), w2_ref[...])
        g, u = gate_ref[...].astype(F32), up_ref[...].astype(F32)
        sg = jax.nn.sigmoid(g)
        dg_ref[...] = (d_act * u * sg * (1.0 + g * (1.0 - sg))).astype(BF16)
        du_ref[...] = (d_act * g * sg).astype(BF16)

    o_spec = pl.BlockSpec((tp, FFN_TQ), lambda j, i: (i, j))
    return pl.pallas_call(
        body, grid=(D_FF // FFN_TQ, s // tp),
        in_specs=[pl.BlockSpec((tp, d), lambda j, i: (i, 0)), pl.BlockSpec((FFN_TQ, d), lambda j, i: (j, 0)), o_spec, o_spec],
        out_specs=[o_spec] * 2, out_shape=[jax.ShapeDtypeStruct((s, D_FF), BF16)] * 2,
        compiler_params=_cparams("arbitrary", "arbitrary"), name="ffn_down_bwd_swiglu",
    )(dh, w2, gate, up)


ATTN_SCALE = QK ** -0.5
LOG2E = 1.4426950408889634
ATTN_C = ATTN_SCALE * LOG2E


ATTN_HEADS = 2


def _attn_tile(s):
    return min(512, s)


def _causal_keep(t, keys_on_rows=False):
    row = lax.broadcasted_iota(jnp.int32, (t, t), 0)
    col = lax.broadcasted_iota(jnp.int32, (t, t), 1)
    return row <= col if keys_on_rows else col <= row


def _nt(a, b):
    return lax.dot_general(a, b, (((1,), (1,)), ((), ())), preferred_element_type=F32)


def _rider_phases(rider, src_ref, out_ref, sems, first, last):
    @pl.when(first)
    def _():
        _peer_copies(src_ref, out_ref, sems, rider["gather"], "start")

    def finish():
        @pl.when(last)
        def _():
            _peer_copies(src_ref, out_ref, sems, rider["gather"], "finish")

    return finish


def _attn_fwd_call(q, k, v, rider=None):
    nh, s, _ = q.shape
    t = _attn_tile(s)
    nb = s // t
    hp = ATTN_HEADS
    n_r = 0 if rider is None else 1

    def body(*refs):
        q_ref, k_ref, v_ref = refs[:3]
        o_ref, lse_ref = refs[3 + n_r:5 + n_r]
        qi = pl.program_id(1)
        finish = None
        if rider is not None:
            h = pl.program_id(0)
            finish = _rider_phases(rider, refs[3:4], refs[5 + n_r], refs[6 + n_r:],
                                   jnp.logical_and(h == 0, qi == 0), jnp.logical_and(h == nh // hp - 1, qi == nb - 1))
        qs = [q_ref[i] for i in range(hp)]

        def block(kb, carries, diagonal, width=1):
            start = pl.multiple_of(kb * t, t)
            out = []
            for i, (m_prev, l_prev, acc) in enumerate(carries):
                sc = _nt(qs[i], k_ref[i, pl.ds(start, width * t), :])
                if diagonal:
                    sc = jnp.where(_causal_keep(t), sc, NEG)
                m_new = jnp.maximum(m_prev, jnp.max(sc, axis=-1, keepdims=True))
                p = jnp.exp2(sc * ATTN_C - m_new * ATTN_C)
                alpha = jnp.exp2((m_prev - m_new) * ATTN_C)
                l_new = alpha * l_prev + jnp.sum(p, axis=-1, keepdims=True)
                pv = jnp.dot(p.astype(BF16), v_ref[i, pl.ds(start, width * t), :], preferred_element_type=F32)
                out.append((m_new, l_new, alpha * acc + pv))
            return tuple(out)

        init = tuple((jnp.full((t, 1), NEG, F32), jnp.zeros((t, 1), F32), jnp.zeros((t, VDIM), F32)) for _ in range(hp))
        carries = lax.fori_loop(0, qi // 2, lambda j, c: block(2 * j, c, False, width=2), init)
        carries = lax.cond(qi % 2 == 1, lambda c: block(qi - 1, c, False), lambda c: c, carries)
        for i, (m, l, acc) in enumerate(block(qi, carries, True)):
            o_ref[i] = (acc / l).astype(o_ref.dtype)
            lse_ref[i] = m * ATTN_SCALE + jnp.log(l)
        if finish is not None:
            finish()

    qmap = lambda h, i: (h, i, 0)
    whole = lambda h, i: (h, 0, 0)
    return pl.pallas_call(
        body, grid=(nh // hp, nb),
        in_specs=[pl.BlockSpec((hp, t, QK), qmap), pl.BlockSpec((hp, s, QK), whole), pl.BlockSpec((hp, s, VDIM), whole)] + [HBM_SPEC] * n_r,
        out_specs=[pl.BlockSpec((hp, t, VDIM), qmap), pl.BlockSpec((hp, t, 1), qmap)] + [HBM_SPEC] * n_r,
        out_shape=[jax.ShapeDtypeStruct((nh, s, VDIM), BF16), jax.ShapeDtypeStruct((nh, s, 1), F32)] + ([rider["out"]] if n_r else []),
        scratch_shapes=_comm_scratch() if n_r else [],
        compiler_params=_cparams("arbitrary", "arbitrary"), name="attn_fwd_gather" if n_r else "attn_fwd",
    )(*([q, k, v] + (rider["srcs"] if n_r else [])))


def _attn_bwd_call(q, k, v, do, lse_t, delta_t, rider=None):
    nh, s, _ = q.shape
    t = _attn_tile(s)
    nb = s // t
    hp = ATTN_HEADS
    n_src = 0 if rider is None else len(rider["srcs"])
    n_r = 0 if rider is None else 1

    def body(*refs):
        q_ref, k_ref, v_ref, do_ref, lse_ref, delta_ref = refs[:6]
        dq_ref, dk_ref, dv_ref = refs[6 + n_src:9 + n_src]
        dk_sc, dv_sc = refs[9 + n_src + n_r:11 + n_src + n_r]
        kj = pl.program_id(1)
        finish = None
        if rider is not None:
            h = pl.program_id(0)
            finish = _rider_phases(rider, refs[6:6 + n_src], refs[9 + n_src], refs[11 + n_src + n_r:],
                                   jnp.logical_and(h == 0, kj == 0), jnp.logical_and(h == nh // hp - 1, kj == nb - 1))

        @pl.when(kj == 0)
        def _():
            dq_ref[...] = jnp.zeros_like(dq_ref)

        dk_sc[...] = jnp.zeros_like(dk_sc)
        dv_sc[...] = jnp.zeros_like(dv_sc)
        kblks = [k_ref[i] for i in range(hp)]
        vblks = [v_ref[i] for i in range(hp)]

        def block(qb, diagonal):
            start = pl.multiple_of(qb * t, t)
            for i in range(hp):
                qblk = q_ref[i, pl.ds(start, t), :]
                doblk = do_ref[i, pl.ds(start, t), :]
                sc = _nt(kblks[i], qblk)
                if diagonal:
                    sc = jnp.where(_causal_keep(t, keys_on_rows=True), sc, NEG)
                p = jnp.exp2(sc * ATTN_C - lse_ref[i, :, pl.ds(start, t)] * LOG2E)
                dv_sc[i] += jnp.dot(p.astype(BF16), doblk, preferred_element_type=F32)
                dp = _nt(vblks[i], doblk)
                ds = (p * (dp - delta_ref[i, :, pl.ds(start, t)])).astype(BF16)
                dk_sc[i] += jnp.dot(ds, qblk, preferred_element_type=F32)
                dq_ref[i, pl.ds(start, t), :] += lax.dot_general(ds, kblks[i], (((0,), (0,)), ((), ())), preferred_element_type=F32)

        block(kj, True)

        def rest(qb, carry):
            block(qb, False)
            return carry

        lax.fori_loop(kj + 1, nb, rest, 0)
        dk_ref[...] = (dk_sc[...] * ATTN_SCALE).astype(dk_ref.dtype)
        dv_ref[...] = dv_sc[...].astype(dv_ref.dtype)

        @pl.when(kj == nb - 1)
        def _():
            dq_ref[...] = dq_ref[...] * ATTN_SCALE

        if finish is not None:
            finish()

    kmap = lambda h, j: (h, j, 0)
    whole = lambda h, j: (h, 0, 0)
    once = pl.Buffered(buffer_count=1)
    return pl.pallas_call(
        body, grid=(nh // hp, nb),
        in_specs=[pl.BlockSpec((hp, s, QK), whole, pipeline_mode=once), pl.BlockSpec((hp, t, QK), kmap), pl.BlockSpec((hp, t, VDIM), kmap),
                  pl.BlockSpec((hp, s, VDIM), whole, pipeline_mode=once), pl.BlockSpec((hp, 1, s), whole, pipeline_mode=once),
                  pl.BlockSpec((hp, 1, s), whole, pipeline_mode=once)] + [HBM_SPEC] * n_src,
        out_specs=[pl.BlockSpec((hp, s, QK), whole, pipeline_mode=once), pl.BlockSpec((hp, t, QK), kmap),
                   pl.BlockSpec((hp, t, VDIM), kmap)] + [HBM_SPEC] * n_r,
        out_shape=[jax.ShapeDtypeStruct((nh, s, QK), F32), jax.ShapeDtypeStruct((nh, s, QK), F32),
                   jax.ShapeDtypeStruct((nh, s, VDIM), F32)] + ([rider["out"]] if n_r else []),
        scratch_shapes=[pltpu.VMEM((hp, t, QK), F32), pltpu.VMEM((hp, t, VDIM), F32)] + (_comm_scratch() if n_r else []),
        compiler_params=_cparams("arbitrary", "arbitrary"), name="attn_bwd_exchange" if n_r else "attn_bwd",
    )(*([q, k, v, do, lse_t, delta_t] + (rider["srcs"] if n_r else [])))


HEAD_COLS = NOPE + VDIM
HEADS_TILE = 256


def _swap_rope_halves(t, lane):
    half = ROPE // 2
    return jnp.where(lane < half, pltpu.roll(t, LANE - half, 1), pltpu.roll(t, half, 1))


def _head_fwd(n, p, gain, cs, sn, lane):
    r = lax.rsqrt((jnp.sum(n * n, axis=-1, keepdims=True) + jnp.sum(p * p, axis=-1, keepdims=True)) * (1.0 / QK) + EPS)
    yp = p * r * gain[:, NOPE:]
    return n * r * gain[:, :NOPE], yp * cs + _swap_rope_halves(yp, lane) * sn


def _head_bwd(n, p, gain, cs, sn, lane, dzn, dzp):
    r = lax.rsqrt((jnp.sum(n * n, axis=-1, keepdims=True) + jnp.sum(p * p, axis=-1, keepdims=True)) * (1.0 / QK) + EPS)
    dyp = dzp * cs + _swap_rope_halves(dzp * sn, lane)
    gyn, gyp = dzn * gain[:, :NOPE], dyp * gain[:, NOPE:]
    dot = jnp.sum(gyn * n, axis=-1, keepdims=True) + jnp.sum(gyp * p, axis=-1, keepdims=True)
    coef = dot * (r * r * r) * (1.0 / QK)
    d_gn = jnp.sum(dzn * n * r, axis=0, keepdims=True)
    d_gp = jnp.sum(dyp * p * r, axis=0, keepdims=True)
    return gyn * r - n * coef, gyp * r - p * coef, d_gn, d_gp


def _heads_fwd_call(q, kv, proj, cs, sn, q_gain, k_gain):
    s = q.shape[0]
    t = min(HEADS_TILE, s)

    def body(q_ref, kv_ref, last_ref, cs_ref, sn_ref, qg_ref, kg_ref, qh_ref, kh_ref, vh_ref):
        lane = lax.broadcasted_iota(jnp.int32, (t, LANE), 1)
        cs_, sn_ = cs_ref[...], sn_ref[...]
        kp = jnp.where(lane < ROPE, last_ref[...], 0.0)
        for h in range(MLA_H):
            c0 = h * HEAD_COLS
            zn, zp = _head_fwd(q_ref[:, c0:c0 + NOPE], q_ref[:, c0 + NOPE:c0 + HEAD_COLS], qg_ref[...], cs_, sn_, lane)
            qh_ref[h, :, :NOPE] = zn.astype(BF16)
            qh_ref[h, :, NOPE:] = zp[:, :ROPE].astype(BF16)
            zn, zp = _head_fwd(kv_ref[:, c0:c0 + NOPE], kp, kg_ref[...], cs_, sn_, lane)
            kh_ref[h, :, :NOPE] = zn.astype(BF16)
            kh_ref[h, :, NOPE:] = zp[:, :ROPE].astype(BF16)
            vh_ref[h] = kv_ref[:, c0 + NOPE:c0 + HEAD_COLS].astype(BF16)

    rows = lambda i: (i, 0)
    whole = lambda i: (0, 0)
    heads = lambda i: (0, i, 0)
    wide = MLA_H * HEAD_COLS
    return pl.pallas_call(
        body, grid=(s // t,),
        in_specs=[pl.BlockSpec((t, wide), rows), pl.BlockSpec((t, wide), rows),
                  pl.BlockSpec((t, LANE), lambda i: (i, PROJ_LAST // LANE)),
                  pl.BlockSpec((t, LANE), rows), pl.BlockSpec((t, LANE), rows),
                  pl.BlockSpec((1, HEAD_COLS), whole), pl.BlockSpec((1, HEAD_COLS), whole)],
        out_specs=[pl.BlockSpec((MLA_H, t, QK), heads), pl.BlockSpec((MLA_H, t, QK), heads), pl.BlockSpec((MLA_H, t, VDIM), heads)],
        out_shape=[jax.ShapeDtypeStruct((MLA_H, s, QK), BF16), jax.ShapeDtypeStruct((MLA_H, s, QK), BF16),
                   jax.ShapeDtypeStruct((MLA_H, s, VDIM), BF16)],
        compiler_params=_cparams("arbitrary"), name="mla_heads_fwd",
    )(q, kv, proj, cs, sn, q_gain, k_gain)


def _heads_bwd_call(q, kv, proj, cs, sn, q_gain, k_gain, dqh, dkh, dvh):
    s = q.shape[0]
    t = min(HEADS_TILE, s)

    def body(q_ref, kv_ref, last_ref, cs_ref, sn_ref, qg_ref, kg_ref, dqh_ref, dkh_ref, dvh_ref,
             dq_ref, dkv_ref, dkr_ref, dqg_ref, dkg_ref):
        lane = lax.broadcasted_iota(jnp.int32, (t, LANE), 1)
        cs_, sn_ = cs_ref[...], sn_ref[...]
        kp = jnp.where(lane < ROPE, last_ref[...], 0.0)
        no_lanes = jnp.zeros((t, LANE - ROPE), F32)
        d_kp = jnp.zeros((t, LANE), F32)
        d_qg = [jnp.zeros((1, NOPE), F32), jnp.zeros((1, LANE), F32)]
        d_kg = [jnp.zeros((1, NOPE), F32), jnp.zeros((1, LANE), F32)]
        for h in range(MLA_H):
            c0 = h * HEAD_COLS
            dz = dqh_ref[h]
            dzp = jnp.concatenate([dz[:, NOPE:], no_lanes], axis=1)
            d_n, d_p, g_n, g_p = _head_bwd(q_ref[:, c0:c0 + NOPE], q_ref[:, c0 + NOPE:c0 + HEAD_COLS], qg_ref[...],
                                           cs_, sn_, lane, dz[:, :NOPE], dzp)
            dq_ref[:, c0:c0 + NOPE] = d_n.astype(dq_ref.dtype)
            dq_ref[:, c0 + NOPE:c0 + HEAD_COLS] = d_p.astype(dq_ref.dtype)
            d_qg = [d_qg[0] + g_n, d_qg[1] + g_p]
            dz = dkh_ref[h]
            dzp = jnp.concatenate([dz[:, NOPE:], no_lanes], axis=1)
            d_n, d_p, g_n, g_p = _head_bwd(kv_ref[:, c0:c0 + NOPE], kp, kg_ref[...], cs_, sn_, lane, dz[:, :NOPE], dzp)
            dkv_ref[:, c0:c0 + NOPE] = d_n.astype(dkv_ref.dtype)
            dkv_ref[:, c0 + NOPE:c0 + HEAD_COLS] = dvh_ref[h].astype(dkv_ref.dtype)
            d_kp = d_kp + d_p
            d_kg = [d_kg[0] + g_n, d_kg[1] + g_p]
        dkr_ref[...] = d_kp
        first = pl.program_id(0) == 0
        _acc_store(dqg_ref.at[:, pl.ds(0, NOPE)], d_qg[0], first)
        _acc_store(dqg_ref.at[:, pl.ds(NOPE, LANE)], d_qg[1], first)
        _acc_store(dkg_ref.at[:, pl.ds(0, NOPE)], d_kg[0], first)
        _acc_store(dkg_ref.at[:, pl.ds(NOPE, LANE)], d_kg[1], first)

    rows = lambda i: (i, 0)
    whole = lambda i: (0, 0)
    heads = lambda i: (0, i, 0)
    wide = MLA_H * HEAD_COLS
    return pl.pallas_call(
        body, grid=(s // t,),
        in_specs=[pl.BlockSpec((t, wide), rows), pl.BlockSpec((t, wide), rows),
                  pl.BlockSpec((t, LANE), lambda i: (i, PROJ_LAST // LANE)),
                  pl.BlockSpec((t, LANE), rows), pl.BlockSpec((t, LANE), rows),
                  pl.BlockSpec((1, HEAD_COLS), whole), pl.BlockSpec((1, HEAD_COLS), whole),
                  pl.BlockSpec((MLA_H, t, QK), heads), pl.BlockSpec((MLA_H, t, QK), heads), pl.BlockSpec((MLA_H, t, VDIM), heads)],
        out_specs=[pl.BlockSpec((t, wide), rows), pl.BlockSpec((t, wide), rows), pl.BlockSpec((t, LANE), rows),
                   pl.BlockSpec((1, HEAD_COLS), whole), pl.BlockSpec((1, HEAD_COLS), whole)],
        out_shape=[jax.ShapeDtypeStruct((s, wide), BF16), jax.ShapeDtypeStruct((s, wide), BF16), jax.ShapeDtypeStruct((s, LANE), F32),
                   jax.ShapeDtypeStruct((1, HEAD_COLS), F32), jax.ShapeDtypeStruct((1, HEAD_COLS), F32)],
        compiler_params=_cparams("arbitrary"), name="mla_heads_bwd",
    )(q, kv, proj, cs, sn, q_gain, k_gain, dqh, dkh, dvh)


CONV_TC = 512
HALO = 8


def _conv_tiles(s):
    return min(512, s)


def _conv_fwd_call(x, col0, w, b):
    s = x.shape[0]
    ts = _conv_tiles(s)
    hb = ts // HALO
    c0 = col0 // CONV_TC
    assert col0 % CONV_TC == 0

    def body(x_ref, prev_ref, w_ref, b_ref, y_ref, buf):
        si = pl.program_id(1)
        buf[0:HALO, :] = jnp.where(si > 0, prev_ref[...], 0.0)
        buf[HALO:, :] = x_ref[...]
        acc = jnp.broadcast_to(b_ref[...], (ts, CONV_TC))
        for k in range(CONV_K):
            acc = acc + w_ref[k:k + 1, :] * buf[pl.ds(HALO - (CONV_K - 1) + k, ts), :]
        y_ref[...] = acc * jax.nn.sigmoid(acc)

    return pl.pallas_call(
        body, grid=(CONV_DIM // CONV_TC, s // ts),
        in_specs=[pl.BlockSpec((ts, CONV_TC), lambda ci, si: (si, ci + c0)),
                  pl.BlockSpec((HALO, CONV_TC), lambda ci, si: (jnp.maximum(si * hb - 1, 0), ci + c0)),
                  pl.BlockSpec((CONV_K, CONV_TC), lambda ci, si: (0, ci)),
                  pl.BlockSpec((1, CONV_TC), lambda ci, si: (0, ci))],
        out_specs=pl.BlockSpec((ts, CONV_TC), lambda ci, si: (si, ci)),
        out_shape=jax.ShapeDtypeStruct((s, CONV_DIM), F32),
        scratch_shapes=[pltpu.VMEM((ts + HALO, CONV_TC), F32)],
        compiler_params=_cparams("arbitrary", "arbitrary"), name="conv_fwd",
    )(x, x, w, b)


def _conv_bwd_call(x, col0, w, b, dy):
    s = x.shape[0]
    ts = _conv_tiles(s)
    hb = ts // HALO
    ns = s // ts
    last_halo = s // HALO - 1
    c0 = col0 // CONV_TC

    def body(x_ref, prev_ref, next_ref, dy_ref, dyn_ref, w_ref, b_ref, dx_ref, dw_ref, db_ref, xbuf, dbuf):
        si = pl.program_id(1)
        xbuf[0:HALO, :] = jnp.where(si > 0, prev_ref[...], 0.0)
        xbuf[HALO:HALO + ts, :] = x_ref[...]
        xbuf[HALO + ts:, :] = next_ref[...]
        pre = jnp.broadcast_to(b_ref[...], (ts + HALO, CONV_TC))
        for k in range(CONV_K):
            pre = pre + w_ref[k:k + 1, :] * xbuf[pl.ds(HALO - (CONV_K - 1) + k, ts + HALO), :]
        sg = jax.nn.sigmoid(pre)
        dsilu = sg * (1.0 + pre * (1.0 - sg))
        dbuf[0:ts, :] = dy_ref[...] * dsilu[0:ts]
        dbuf[ts:, :] = jnp.where(si < ns - 1, dyn_ref[...] * dsilu[ts:], 0.0)
        dx = jnp.zeros((ts, CONV_TC), F32)
        for k in range(CONV_K):
            dx = dx + w_ref[k:k + 1, :] * dbuf[pl.ds(CONV_K - 1 - k, ts), :]
        dx_ref[...] = dx.astype(dx_ref.dtype)
        dpre = dbuf[0:ts, :]
        first = si == 0
        _acc_store(db_ref, jnp.sum(dpre, axis=0, keepdims=True), first)
        for k in range(CONV_K):
            dw_k = jnp.sum(dpre * xbuf[pl.ds(HALO - (CONV_K - 1) + k, ts), :], axis=0, keepdims=True)
            _acc_store(dw_ref.at[pl.ds(k, 1), :], dw_k, first)

    main = lambda ci, si: (si, ci)
    x_main = lambda ci, si: (si, ci + c0)
    x_prev = lambda ci, si: (jnp.maximum(si * hb - 1, 0), ci + c0)
    x_next = lambda ci, si: (jnp.minimum(si * hb + hb, last_halo), ci + c0)
    return pl.pallas_call(
        body, grid=(CONV_DIM // CONV_TC, ns),
        in_specs=[pl.BlockSpec((ts, CONV_TC), x_main), pl.BlockSpec((HALO, CONV_TC), x_prev), pl.BlockSpec((HALO, CONV_TC), x_next),
                  pl.BlockSpec((ts, CONV_TC), main),
                  pl.BlockSpec((HALO, CONV_TC), lambda ci, si: (jnp.minimum(si * hb + hb, last_halo), ci)),
                  pl.BlockSpec((CONV_K, CONV_TC), lambda ci, si: (0, ci)),
                  pl.BlockSpec((1, CONV_TC), lambda ci, si: (0, ci))],
        out_specs=[pl.BlockSpec((ts, CONV_TC), main),
                   pl.BlockSpec((CONV_K, CONV_TC), lambda ci, si: (0, ci)),
                   pl.BlockSpec((1, CONV_TC), lambda ci, si: (0, ci))],
        out_shape=[jax.ShapeDtypeStruct((s, CONV_DIM), BF16), jax.ShapeDtypeStruct((CONV_K, CONV_DIM), F32),
                   jax.ShapeDtypeStruct((1, CONV_DIM), F32)],
        scratch_shapes=[pltpu.VMEM((ts + 2 * HALO, CONV_TC), F32), pltpu.VMEM((ts + HALO, CONV_TC), F32)],
        compiler_params=_cparams("arbitrary", "arbitrary"), name="conv_bwd",
    )(x, x, x, dy, dy, w, b)


GW = SSD_HPG * SSD_P
B_COL = SSD_DI
C_COL = SSD_DI + SSD_G * SSD_N


def _ones_where(mask):
    return jnp.where(mask, 1.0, 0.0).astype(BF16)


def _split(v, passes):
    parts, rest = [], v
    for i in range(passes):
        part = rest.astype(BF16)
        parts.append(part)
        if i + 1 < passes:
            rest = rest - part.astype(F32)
    return parts


def _dot_sel_r(v, sel, passes=3):
    out = None
    for part in _split(v, passes):
        t = jnp.dot(part, sel, preferred_element_type=F32)
        out = t if out is None else out + t
    return out


def _dot_sel_l(sel, v, passes=3):
    out = None
    for part in _split(v, passes):
        t = jnp.dot(sel, part, preferred_element_type=F32)
        out = t if out is None else out + t
    return out


def _ssd_consts():
    r = lax.broadcasted_iota(jnp.int32, (SSD_L, SSD_L), 0)
    c = lax.broadcasted_iota(jnp.int32, (SSD_L, SSD_L), 1)
    tril = r >= c
    triu = c >= r
    shift = SSD_P.bit_length() - 1
    eh = lax.broadcasted_iota(jnp.int32, (SSD_H, SSD_DI), 0)
    ej = lax.broadcasted_iota(jnp.int32, (SSD_H, SSD_DI), 1)
    expand = _ones_where(lax.shift_right_logical(ej, shift) == eh)
    rj = lax.broadcasted_iota(jnp.int32, (SSD_DI, SSD_H), 0)
    rh = lax.broadcasted_iota(jnp.int32, (SSD_DI, SSD_H), 1)
    reduce_ = _ones_where(lax.shift_right_logical(rj, shift) == rh)
    lane = lax.broadcasted_iota(jnp.int32, (SSD_L, LANE), 1)
    return tril, triu, expand, reduce_, lane < SSD_P


def _ssd_decays(dt, dt_t, a, a_t, tril, triu, expand):
    dta = dt * a
    acum = _dot_sel_l(_ones_where(tril), dta)
    acum_t = _dot_sel_r(dt_t * a_t, _ones_where(triu))
    dta_e = _dot_sel_r(dta, expand)
    acum_e = _dot_sel_r(acum, expand)
    last_e = jnp.sum(dta_e, axis=0, keepdims=True)
    return acum, acum_t, acum_e, last_e


def _head_decay(acum, acum_t, h, tril):
    seg = acum[:, h:h + 1] - acum_t[h:h + 1, :]
    return jnp.exp(jnp.where(tril, seg, NEG))


def _ssd_fwd_call(xbc, dt, a):
    s = xbc.shape[0]
    nc = s // SSD_L
    dt_t = dt.T
    a_t = a.T

    def body(xbc_ref, dt_ref, dtt_ref, a_ref, at_ref, y_ref, st_ref, s_sc):
        ci = pl.program_id(0)

        @pl.when(ci == 0)
        def _():
            s_sc[...] = jnp.zeros_like(s_sc)

        st_ref[0] = s_sc[...]
        tril, triu, expand, _, low_half = _ssd_consts()
        acum, acum_t, acum_e, last_e = _ssd_decays(dt_ref[...], dtt_ref[...], a_ref[...], at_ref[...], tril, triu, expand)
        dt_e = _dot_sel_r(dt_ref[...], expand, passes=2)
        xdt = xbc_ref[:, :SSD_DI] * dt_e
        xdt_b = xdt.astype(BF16)
        xw_b = (xdt * jnp.exp(last_e - acum_e)).astype(BF16)
        ea_e = jnp.exp(acum_e)
        el_e = jnp.exp(last_e)
        for g in range(SSD_G):
            gs = slice(g * GW, (g + 1) * GW)
            bg = xbc_ref[:, B_COL + g * SSD_N:B_COL + (g + 1) * SSD_N]
            cg_b = xbc_ref[:, C_COL + g * SSD_N:C_COL + (g + 1) * SSD_N].astype(BF16)
            bg_b = bg.astype(BF16)
            cb = _nt(cg_b, bg_b)
            st = s_sc[:, gs]
            y_off = jnp.dot(cg_b, st.astype(BF16), preferred_element_type=F32) * ea_e[:, gs]
            for pr in range(SSD_HPG // 2):
                ls = slice(g * GW + pr * LANE, g * GW + (pr + 1) * LANE)
                xp = xdt_b[:, ls]
                yd = []
                for half in range(2):
                    h = g * SSD_HPG + pr * 2 + half
                    m = (cb * _head_decay(acum, acum_t, h, tril)).astype(BF16)
                    yd.append(jnp.dot(m, xp, preferred_element_type=F32))
                y_ref[:, ls] = jnp.where(low_half, yd[0], yd[1]) + y_off[:, pr * LANE:(pr + 1) * LANE]
            s_sc[:, gs] = st * el_e[:, gs] + jnp.dot(bg.T.astype(BF16), xw_b[:, gs], preferred_element_type=F32)

    row = lambda i: (i, 0)
    return pl.pallas_call(
        body, grid=(nc,),
        in_specs=[pl.BlockSpec((SSD_L, CONV_DIM), row), pl.BlockSpec((SSD_L, SSD_H), row),
                  pl.BlockSpec((SSD_H, SSD_L), lambda i: (0, i)), pl.BlockSpec((1, SSD_H), lambda i: (0, 0)),
                  pl.BlockSpec((SSD_H, 1), lambda i: (0, 0))],
        out_specs=[pl.BlockSpec((SSD_L, SSD_DI), row), pl.BlockSpec((1, SSD_N, SSD_DI), lambda i: (i, 0, 0))],
        out_shape=[jax.ShapeDtypeStruct((s, SSD_DI), F32), jax.ShapeDtypeStruct((nc, SSD_N, SSD_DI), F32)],
        scratch_shapes=[pltpu.VMEM((SSD_N, SSD_DI), F32)],
        compiler_params=_cparams("arbitrary"), name="ssd_fwd",
    )(xbc, dt, dt_t, a, a_t)


def _ssd_bwd_call(xbc, dt, a, states, dy, dx_extra):
    s = xbc.shape[0]
    nc = s // SSD_L
    dt_t = dt.T
    a_t = a.T

    def body(xbc_ref, dt_ref, dtt_ref, a_ref, at_ref, st_ref, dy_ref, dxe_ref,
             dxbc_ref, ddt_ref, da_ref, ds_sc, yf_sc, dxd_sc, dxw_sc):
        i = pl.program_id(0)

        @pl.when(i == 0)
        def _():
            ds_sc[...] = jnp.zeros_like(ds_sc)

        tril, triu, expand, reduce_, low_half = _ssd_consts()
        dt = dt_ref[...]
        a_row = a_ref[...]
        acum, acum_t, acum_e, last_e = _ssd_decays(dt, dtt_ref[...], a_row, at_ref[...], tril, triu, expand)
        dt_e = _dot_sel_r(dt, expand, passes=2)
        x = xbc_ref[:, :SSD_DI]
        xdt = x * dt_e
        xdt_b = xdt.astype(BF16)
        w_e = jnp.exp(last_e - acum_e)
        xw_b = (xdt * w_e).astype(BF16)
        ea_e = jnp.exp(acum_e)
        el_e = jnp.exp(last_e)
        dy = dy_ref[...]
        dy_b = dy.astype(BF16)
        s_prev = st_ref[0]
        ds_new = ds_sc[...]
        ds_new_b = ds_new.astype(BF16)
        triu_b = _ones_where(triu)
        strict_tril = jnp.logical_not(triu)
        head_ids = lax.broadcasted_iota(jnp.int32, (1, SSD_H), 1)
        d_dta_diag = jnp.zeros((SSD_L, SSD_H), F32)
        for g in range(SSD_G):
            gs = slice(g * GW, (g + 1) * GW)
            bs_ = slice(B_COL + g * SSD_N, B_COL + (g + 1) * SSD_N)
            cs_ = slice(C_COL + g * SSD_N, C_COL + (g + 1) * SSD_N)
            bg = xbc_ref[:, bs_]
            cg = xbc_ref[:, cs_]
            bg_b, cg_b = bg.astype(BF16), cg.astype(BF16)
            st_b = s_prev[:, gs].astype(BF16)
            y_off = jnp.dot(cg_b, st_b, preferred_element_type=F32) * ea_e[:, gs]
            yf_sc[:, gs] = y_off
            dz_b = (dy[:, gs] * ea_e[:, gs]).astype(BF16)
            d_c = _nt(dz_b, st_b)
            ds_prev = ds_new[:, gs] * el_e[:, gs] + jnp.dot(cg.T.astype(BF16), dz_b, preferred_element_type=F32)
            dxw_sc[:, gs] = jnp.dot(bg_b, ds_new_b[:, gs], preferred_element_type=F32)
            d_b = _nt(xw_b[:, gs], ds_new_b[:, gs])
            cb = _nt(cg_b, bg_b)
            d_g = jnp.zeros((SSD_L, SSD_L), F32)
            for pr in range(SSD_HPG // 2):
                ls = slice(g * GW + pr * LANE, g * GW + (pr + 1) * LANE)
                xp = xdt_b[:, ls]
                dyp = dy[:, ls]
                dyp_b = dy_b[:, ls]
                dxd = []
                for half in range(2):
                    h = g * SSD_HPG + pr * 2 + half
                    dec = _head_decay(acum, acum_t, h, tril)
                    m = cb * dec
                    dxd.append(jnp.dot(m.T.astype(BF16), dyp_b, preferred_element_type=F32))
                    mine = low_half if half == 0 else jnp.logical_not(low_half)
                    d_m = _nt(jnp.where(mine, dyp, 0.0).astype(BF16), xp)
                    d_g = d_g + d_m * dec
                    below = jnp.dot(triu_b, (d_m * m).astype(BF16), preferred_element_type=F32)
                    col = jnp.sum(jnp.where(strict_tril, below, 0.0), axis=1, keepdims=True)
                    d_dta_diag = d_dta_diag + col * jnp.where(head_ids == h, 1.0, 0.0)
                dxd_sc[:, ls] = jnp.where(low_half, dxd[0], dxd[1])
            d_g_b = d_g.astype(BF16)
            dxbc_ref[:, cs_] = d_c + jnp.dot(d_g_b, bg_b, preferred_element_type=F32)
            dxbc_ref[:, bs_] = d_b + jnp.dot(d_g.T.astype(BF16), cg_b, preferred_element_type=F32)
            ds_sc[:, gs] = ds_prev
        dxw = dxw_sc[...]
        dxd = dxd_sc[...]
        dw_e = xdt * dxw * w_e
        d_tot_e = jnp.sum(ds_new * s_prev, axis=0, keepdims=True) * el_e
        d_state_e = (_dot_sel_l(triu_b, dy * yf_sc[...], passes=2)
                     + _dot_sel_l(_ones_where(strict_tril), dw_e, passes=2) + d_tot_e)
        dxdt = dxd + dxw * w_e
        dxbc_ref[:, :SSD_DI] = dxdt * dt_e + dxe_ref[...]
        a_e = _dot_sel_r(jnp.broadcast_to(a_row, (8, SSD_H)), expand)[0:1]
        ddt_ref[...] = _dot_sel_r(d_state_e * a_e + dxdt * x, reduce_, passes=2) + d_dta_diag * a_row
        d_a_e = jnp.sum(d_state_e * dt_e, axis=0, keepdims=True)
        d_a = _dot_sel_r(jnp.broadcast_to(d_a_e, (8, SSD_DI)), reduce_)[0:1] + jnp.sum(d_dta_diag * dt, axis=0, keepdims=True)
        _acc_store(da_ref, d_a, i == 0)

    rev = lambda i: (nc - 1 - i, 0)
    return pl.pallas_call(
        body, grid=(nc,),
        in_specs=[pl.BlockSpec((SSD_L, CONV_DIM), rev), pl.BlockSpec((SSD_L, SSD_H), rev),
                  pl.BlockSpec((SSD_H, SSD_L), lambda i: (0, nc - 1 - i)), pl.BlockSpec((1, SSD_H), lambda i: (0, 0)),
                  pl.BlockSpec((SSD_H, 1), lambda i: (0, 0)),
                  pl.BlockSpec((1, SSD_N, SSD_DI), lambda i: (nc - 1 - i, 0, 0)),
                  pl.BlockSpec((SSD_L, SSD_DI), rev), pl.BlockSpec((SSD_L, SSD_DI), rev)],
        out_specs=[pl.BlockSpec((SSD_L, CONV_DIM), rev), pl.BlockSpec((SSD_L, SSD_H), rev),
                   pl.BlockSpec((1, SSD_H), lambda i: (0, 0))],
        out_shape=[jax.ShapeDtypeStruct((s, CONV_DIM), F32), jax.ShapeDtypeStruct((s, SSD_H), F32),
                   jax.ShapeDtypeStruct((1, SSD_H), F32)],
        scratch_shapes=[pltpu.VMEM((SSD_N, SSD_DI), F32), pltpu.VMEM((SSD_L, SSD_DI), F32),
                        pltpu.VMEM((SSD_L, SSD_DI), F32), pltpu.VMEM((SSD_L, SSD_DI), F32)],
        compiler_params=_cparams("arbitrary"), name="ssd_bwd",
    )(xbc, dt, dt_t, a, a_t, states, dy, dx_extra)


HBM_SPEC = pl.BlockSpec(memory_space=pltpu.HBM)
N_PEERS = N_DEV - 1


def _flip(v, f):
    return 1 - v if f else v


def _all_gather(shard):
    rows, c = shard.shape

    def body(x_ref, out_ref, send_sems, recv_sems, local_sem):
        x, y, cc = lax.axis_index("x"), lax.axis_index("y"), lax.axis_index("c")
        me, sibling = (x, y, cc), (x, y, 1 - cc)
        chips = [(1 - x, y), (x, 1 - y), (1 - x, 1 - y)]

        def slot(px, py, pc):
            return out_ref.at[4 * px + 2 * py + pc]

        def copy(k, block, to, src=None):
            return pltpu.make_async_remote_copy(
                src_ref=slot(*block) if src is None else src, dst_ref=slot(*block),
                send_sem=send_sems.at[k], recv_sem=recv_sems.at[k],
                device_id=to, device_id_type=pl.DeviceIdType.MESH)

        mine = pltpu.make_async_copy(x_ref, slot(*me), local_sem)
        mine.start()
        first = [copy(0, me, sibling, src=x_ref)]
        first += [copy(1 + j, me, (*chip, cc), src=x_ref) for j, chip in enumerate(chips)]
        for cp in first:
            cp.start()
        passed = [copy(4 + j, (*chip, cc), sibling) for j, chip in enumerate(chips)]
        for j, chip in enumerate(chips):
            copy(1 + j, (*chip, cc), me).wait_recv()
            passed[j].start()
        copy(0, sibling, me).wait_recv()
        for j, chip in enumerate(chips):
            copy(4 + j, (*chip, 1 - cc), me).wait_recv()
        for cp in first + passed:
            cp.wait_send()
        mine.wait()

    return pl.pallas_call(
        body, out_shape=jax.ShapeDtypeStruct((N_DEV, rows, c), shard.dtype),
        in_specs=[HBM_SPEC], out_specs=HBM_SPEC,
        scratch_shapes=[pltpu.SemaphoreType.DMA((N_PEERS,)), pltpu.SemaphoreType.DMA((N_PEERS,)), pltpu.SemaphoreType.DMA(())],
        name="all_gather",
    )(shard)


def _peer_copies(src_refs, out_ref, sems, gather, phase):
    send_sems, recv_sems, local_sem = sems
    x, y, cc = lax.axis_index("x"), lax.axis_index("y"), lax.axis_index("c")
    me = 4 * x + 2 * y + cc

    def pieces(block, slot):
        if gather:
            return [(src_refs[0], out_ref.at[slot])]
        out, r0 = [], 0
        for src in src_refs:
            out.append((src.at[block], out_ref.at[slot, pl.ds(r0, src.shape[1])]))
            r0 += src.shape[1]
        assert r0 == out_ref.shape[1], (r0, out_ref.shape)
        return out

    if phase == "start":
        for src, dst in pieces(me, me):
            pltpu.make_async_copy(src, dst, local_sem).start()
    for k in range(1, N_DEV):
        px, py, pc = _flip(x, k & 4), _flip(y, k & 2), _flip(cc, k & 1)
        peer = 4 * px + 2 * py + pc
        to_peer = dict(send_sem=send_sems.at[k - 1], recv_sem=recv_sems.at[k - 1],
                       device_id=(px, py, pc), device_id_type=pl.DeviceIdType.MESH)
        if phase == "start":
            for src, dst in pieces(peer, me):
                pltpu.make_async_remote_copy(src_ref=src, dst_ref=dst, **to_peer).start()
        else:
            whole = pltpu.make_async_remote_copy(src_ref=out_ref.at[peer], dst_ref=out_ref.at[peer], **to_peer)
            whole.wait_recv()
            whole.wait_send()
    if phase != "start":
        pltpu.make_async_copy(out_ref.at[me], out_ref.at[me], local_sem).wait()


def _comm_scratch():
    return [pltpu.SemaphoreType.DMA((N_PEERS,)), pltpu.SemaphoreType.DMA((N_PEERS,)), pltpu.SemaphoreType.DMA(())]


def _gather_rider(shard):
    return dict(srcs=[shard], out=jax.ShapeDtypeStruct((N_DEV,) + shard.shape, shard.dtype), gather=True)


def _exchange_out(parts):
    rows = sum(p.shape[1] for p in parts)
    return jax.ShapeDtypeStruct((N_DEV, rows) + parts[0].shape[2:], parts[0].dtype)


def _exchange_rider(parts):
    return dict(srcs=list(parts), out=_exchange_out(parts), gather=False)


def _exchange_blocks(parts):
    n = len(parts)

    def body(*refs):
        _peer_copies(refs[:n], refs[n], refs[n + 1:], False, "start")
        _peer_copies(refs[:n], refs[n], refs[n + 1:], False, "finish")

    return pl.pallas_call(
        body, out_shape=_exchange_out(parts),
        in_specs=[HBM_SPEC] * n, out_specs=HBM_SPEC, scratch_shapes=_comm_scratch(), name="exchange_blocks",
    )(*parts)


BIG = [
    ("ffn1_w13", (D_MODEL, 2 * D_FF), 1), ("ffn1_w2", (D_FF, D_MODEL), 0),
    ("w_ssd_out", (SSD_DI, D_MODEL), 0), ("w_uq", (Q_LORA, MLA_H * QK), 1), ("w_ukv", (KV_LORA, MLA_H * (NOPE + VDIM)), 1),
    ("w_mla_out", (MLA_H * VDIM, D_MODEL), 0), ("w_o", (D_MODEL, D_MODEL), 0),
    ("ffn2_w13", (D_MODEL, 2 * D_FF), 1), ("ffn2_w2", (D_FF, D_MODEL), 0), ("w_in", (D_MODEL, D_IN), 1),
]
assert all(_r % 16 == 0 for _r in [_f[0] * _f[1] // N_DEV // PACK_COLS for _, _f, _ in BIG[:-1]])
SMALL = [
    ("ln_ffn1", D_MODEL), ("ln_mix", D_MODEL), ("conv_b", CONV_DIM), ("dt_bias", SSD_H), ("a_log", SSD_H), ("d_skip", SSD_H),
    ("ssd_norm", SSD_DI), ("q_lora_norm", Q_LORA), ("kv_lora_norm", KV_LORA), ("q_norm", QK), ("k_norm", QK), ("ln_ffn2", D_MODEL),
]


def _shard_shape(full, axis):
    k, n = full
    return (k // N_DEV, n) if axis == 0 else (k, n // N_DEV)


def _shard_rows(full):
    return full[0] * full[1] // N_DEV // PACK_COLS


LAYER_ROWS = sum(_shard_rows(f) for _, f, _ in BIG)
LAYER_ROWS_PAD = -(-LAYER_ROWS // 256) * 256


def _pack_shards(shards):
    parts = [(shards[name] if axis == 0 else shards[name].T).reshape(-1, PACK_COLS) for name, _, axis in BIG]
    pad = LAYER_ROWS_PAD - LAYER_ROWS
    if pad:
        parts.append(jnp.zeros((pad, PACK_COLS), parts[0].dtype))
    return jnp.concatenate(parts, axis=0)


BIG_BY_NAME = {name: (full, axis) for name, full, axis in BIG}
BIG_NAMES = [name for name, _, _ in BIG]
EARLY = ["ffn2_w13", "ffn2_w2", "w_o", "w_mla_out"]
LATE = [name for name in BIG_NAMES if name not in EARLY]
SUM_ROWS = 128


def _part_rows(name):
    return -(-_shard_rows(BIG_BY_NAME[name][0]) // 16) * 16


def _grad_parts(grads, names):
    parts = []
    for name in names:
        part = grads[name].reshape(N_DEV, -1, PACK_COLS)
        parts.append(jnp.pad(part, ((0, 0), (0, _part_rows(name) - part.shape[1]), (0, 0))))
    return parts


def _pad_parts(parts):
    pad = -sum(p.shape[1] for p in parts) % SUM_ROWS
    return parts + ([jnp.zeros((N_DEV, pad, PACK_COLS), parts[0].dtype)] if pad else [])


def _unpack_parts(summed, names, r=0):
    out = {}
    for name in names:
        full, axis = BIG_BY_NAME[name]
        k, c = _shard_shape(full, axis)
        blk = summed[r:r + _shard_rows(full)]
        out[name] = blk.reshape(k, c) if axis == 0 else blk.reshape(c, k).T
        r += _part_rows(name)
    return out, r


def _working_shape(full, axis):
    return full if axis == 0 else full[::-1]


def _unpack_gathered(gathered):
    out, r = {}, 0
    for name, full, axis in BIG:
        n = _shard_rows(full)
        out[name] = gathered[:, r:r + n].reshape(_working_shape(full, axis))
        r += n
    return out


SMALL_COLS = sum(n for _, n in SMALL) + CONV_K * CONV_DIM
SMALL_ROWS = -(-(DEPTH * SMALL_COLS) // (8 * PACK_COLS)) * 8


def _pack_small(vals, conv_w):
    flat = jnp.concatenate([vals[name] for name, _ in SMALL] + [conv_w.reshape(DEPTH, -1)], axis=1).reshape(-1)
    flat = jnp.concatenate([flat, jnp.zeros((SMALL_ROWS * PACK_COLS - flat.shape[0],), F32)])
    return flat.reshape(SMALL_ROWS, PACK_COLS)


def _unpack_small(packed):
    flat = packed.reshape(-1)[:DEPTH * SMALL_COLS].reshape(DEPTH, SMALL_COLS)
    out, c = {}, 0
    for name, n in SMALL:
        out[name] = flat[:, c:c + n]
        c += n
    return out, flat[:, c:].reshape(DEPTH, CONV_K, CONV_DIM)


_IN_OFFS = [sum(IN_SPLIT[:i]) for i in range(len(IN_SPLIT) + 1)]


def _arrange_w_in(w_t):
    z, xbc, dt, cq, ckv, kr, gates = [w_t[_IN_OFFS[i]:_IN_OFFS[i + 1]] for i in range(len(IN_SPLIT))]
    pad = jnp.zeros((LANE - ROPE - SSD_H, w_t.shape[1]), w_t.dtype)
    return jnp.concatenate([z, gates, xbc, cq, ckv, kr, dt, pad], axis=0)


def _restore_w_in(g):
    z, gates, xbc = g[PROJ_Z:PROJ_GATES], g[PROJ_GATES:PROJ_XBC], g[PROJ_XBC:PROJ_CQ]
    cq, ckv = g[PROJ_CQ:PROJ_CKV], g[PROJ_CKV:PROJ_LAST]
    kr, dt = g[PROJ_LAST:PROJ_LAST + ROPE], g[PROJ_LAST + ROPE:PROJ_LAST + ROPE + SSD_H]
    return jnp.concatenate([z, xbc, dt, cq, ckv, kr, gates], axis=0)


def _pad_heads(w_t):
    k = w_t.shape[1]
    return jnp.pad(w_t.reshape(MLA_H, QK, k), ((0, 0), (0, HEAD_COLS - QK), (0, 0))).reshape(MLA_H * HEAD_COLS, k)


def _unpad_heads(g):
    k = g.shape[1]
    return g.reshape(MLA_H, HEAD_COLS, k)[:, :QK].reshape(MLA_H * QK, k)


def _row(v):
    return v.reshape(1, -1)


def _head_gain(g):
    return jnp.pad(g, (0, HEAD_COLS - QK)).reshape(1, HEAD_COLS)


def _ffn_fwd(h, ln, w13_t, w2, name):
    n = _row_fwd(_f_rmsnorm, [h], [_row(ln)], [BF16], name + "_fwd")[0]
    act, gate, up = _ffn_up_call(n, w13_t)
    return _mm(act, w2, alpha=0.5, res=h), (h, n, gate, up, act)


def _ffn_bwd(dh_out, saved, ln, w13_t, w2, name):
    h, n, gate, up, act = saved
    d_gate, d_up = _ffn_down_bwd_call(dh_out, w2, gate, up)
    d_w2 = _mm(act, dh_out, ta=True, out_dtype=BF16, alpha=0.5)
    d_n = _mm(d_gate, w13_t, b_rows=(0, D_FF))
    dh, d_ln = _mm(d_up, w13_t, b_rows=(D_FF, D_FF), res=d_n, norm_bwd=(h, _row(ln), dh_out))
    d_w13_t = jnp.concatenate([_mm(d_gate, n, ta=True, out_dtype=BF16), _mm(d_up, n, ta=True, out_dtype=BF16)], axis=0)
    return dh, d_w13_t, d_w2, d_ln[0]


def _mixer_fwd(h, big, small, conv_w, cs, sn, rider=None):
    s = h.shape[0]
    u = _row_fwd(_f_rmsnorm, [h], [_row(small["ln_mix"])], [BF16], "ln_mix_fwd")[0]
    proj = _mm(u, big["w_in"], tb=True)
    xbc = _conv_fwd_call(proj, PROJ_XBC, conv_w, _row(small["conv_b"]))
    dt_in = proj[:, PROJ_LAST + ROPE:PROJ_LAST + ROPE + SSD_H] + small["dt_bias"][None, :]
    dt = jax.nn.softplus(dt_in)
    a = -jnp.exp(small["a_log"])[None, :]
    y_scan, states = _ssd_fwd_call(xbc, dt, a)
    dsk = _row(jnp.repeat(small["d_skip"], SSD_P))
    gn_in = [y_scan, _win(xbc, 0, SSD_DI), _win(proj, PROJ_Z, SSD_DI)]
    yn = _row_fwd(_f_gated_norm, gn_in, [dsk, _row(small["ssd_norm"])], [BF16], "gated_norm_fwd")[0]
    y_ssd = _mm(yn, big["w_ssd_out"])
    qn = _row_fwd(_f_rmsnorm, [_win(proj, PROJ_CQ, Q_LORA)], [_row(small["q_lora_norm"])], [BF16], "q_lora_norm_fwd")[0]
    kvn = _row_fwd(_f_rmsnorm, [_win(proj, PROJ_CKV, KV_LORA)], [_row(small["kv_lora_norm"])], [BF16], "kv_lora_norm_fwd")[0]
    q = _mm(qn, big["w_uq"], tb=True)
    kv = _mm(kvn, big["w_ukv"], tb=True)
    qh, kh, vh = _heads_fwd_call(q, kv, proj, cs, sn, _head_gain(small["q_norm"]), _head_gain(small["k_norm"]))
    o, lse, *carried = _attn_fwd_call(qh, kh, vh, rider)
    o_rows = jnp.transpose(o, (1, 0, 2)).reshape(s, MLA_H * VDIM)
    y_mla = _mm(o_rows, big["w_mla_out"])
    out, mg = _merge_out_call(proj, y_ssd, y_mla, big["w_o"], h)
    saved = (h, u, proj, xbc, dt_in, dt, a, y_scan, states, dsk, yn, y_ssd, qn, kvn, q, kv, qh, kh, vh, o, lse, o_rows, y_mla, mg)
    return out, saved, (carried[0] if carried else None)


def _mixer_bwd(dh_out, saved, big, small, conv_w, cs, sn, carry_parts=None):
    (h, u, proj, xbc, dt_in, dt, a, y_scan, states, dsk, yn, y_ssd, qn, kvn, q, kv, qh, kh, vh, o, lse, o_rows, y_mla, mg) = saved
    s = h.shape[0]
    d_big, d_small = {}, {}
    d_gates, d_y_ssd, d_y_mla = _merge_out_bwd_call(dh_out, big["w_o"], proj, y_ssd, y_mla)
    d_big["w_o"] = _mm(mg, dh_out, ta=True, out_dtype=BF16)
    d_o_rows, delta = _attn_out_bwd_call(d_y_mla, big["w_mla_out"], o_rows)
    d_big["w_mla_out"] = _mm(o_rows, d_y_mla, ta=True, out_dtype=BF16)
    d_o = jnp.transpose(d_o_rows.reshape(s, MLA_H, VDIM), (1, 0, 2))
    rider = None
    if carry_parts is not None:
        rider = _exchange_rider(_pad_parts(carry_parts + _grad_parts(d_big, EARLY[2:])))
    *d_heads, carried = list(_attn_bwd_call(qh, kh, vh, d_o, lse.reshape(MLA_H, 1, s), delta.T.reshape(MLA_H, 1, s), rider)) + ([None] if rider is None else [])
    d_q, d_kv, d_kr, d_qg, d_kg = _heads_bwd_call(
        q, kv, proj, cs, sn, _head_gain(small["q_norm"]), _head_gain(small["k_norm"]), *d_heads)
    d_small["q_norm"], d_small["k_norm"] = d_qg[0, :QK], d_kg[0, :QK]
    d_qn = _mm(d_q, big["w_uq"], out_dtype=BF16)
    d_big["w_uq"] = _mm(d_q, qn, ta=True, out_dtype=BF16)
    d_kvn = _mm(d_kv, big["w_ukv"], out_dtype=BF16)
    d_big["w_ukv"] = _mm(d_kv, kvn, ta=True, out_dtype=BF16)
    (d_cq,), (d_g,) = _row_bwd(_f_rmsnorm, [_win(proj, PROJ_CQ, Q_LORA)], [_row(small["q_lora_norm"])], [d_qn], [BF16], "q_lora_norm_bwd")
    d_small["q_lora_norm"] = d_g[0]
    (d_ckv,), (d_g,) = _row_bwd(_f_rmsnorm, [_win(proj, PROJ_CKV, KV_LORA)], [_row(small["kv_lora_norm"])], [d_kvn], [BF16], "kv_lora_norm_bwd")
    d_small["kv_lora_norm"] = d_g[0]
    d_yn = _mm(d_y_ssd, big["w_ssd_out"], tb=True, out_dtype=BF16)
    d_big["w_ssd_out"] = _mm(yn, d_y_ssd, ta=True, out_dtype=BF16)
    gn_in = [y_scan, _win(xbc, 0, SSD_DI), _win(proj, PROJ_Z, SSD_DI)]
    (d_y_scan, d_xs, d_z), (d_dsk, d_g) = _row_bwd(
        _f_gated_norm, gn_in, [dsk, _row(small["ssd_norm"])], [d_yn], [F32, F32, BF16], "gated_norm_bwd")
    d_small["ssd_norm"] = d_g[0]
    d_small["d_skip"] = jnp.sum(d_dsk.reshape(SSD_H, SSD_P), axis=1)
    d_xbc_act, d_dt, d_a = _ssd_bwd_call(xbc, dt, a, states, d_y_scan, d_xs)
    d_xbc, d_conv_w, d_conv_b = _conv_bwd_call(proj, PROJ_XBC, conv_w, _row(small["conv_b"]), d_xbc_act)
    d_small["conv_b"] = d_conv_b[0]
    d_dt_in = d_dt * jax.nn.sigmoid(dt_in)
    d_small["dt_bias"] = jnp.sum(d_dt_in, axis=0)
    d_small["a_log"] = d_a[0] * a[0]
    d_last = (d_kr + jnp.pad(d_dt_in, ((0, 0), (ROPE, LANE - ROPE - SSD_H)))).astype(BF16)
    d_proj = jnp.concatenate([d_z, d_gates, d_xbc, d_cq, d_ckv, d_last], axis=1)
    dh, d_ln = _mm(d_proj, big["w_in"], norm_bwd=(h, _row(small["ln_mix"]), dh_out))
    d_big["w_in"] = _mm(d_proj, u, ta=True, out_dtype=BF16)
    d_small["ln_mix"] = d_ln[0]
    return dh, d_big, d_small, d_conv_w, carried


def _prepare_big(b):
    return dict(b, w_in=_arrange_w_in(b["w_in"]), w_uq=_pad_heads(b["w_uq"]))


def _local_step(x, positions, target, big, small, conv_w, packed_last=None):
    inv = 1.0 / (ROPE_THETA ** (jnp.arange(0, ROPE, 2, dtype=F32) / ROPE))
    ang = positions.astype(F32)[:, None] * inv
    cos, sin = jnp.cos(ang), jnp.sin(ang)
    no_lanes = jnp.zeros((x.shape[0], LANE - ROPE), F32)
    cs = jnp.concatenate([cos, cos, no_lanes], axis=1)
    sn = jnp.concatenate([-sin, sin, no_lanes], axis=1)
    carrier = DEPTH - 2 if packed_last is not None else None
    big = [None if b is None else _prepare_big(b) for b in big]
    layer_small = [{k: v[l] for k, v in small.items()} for l in range(DEPTH)]

    h, saved = x, []
    for l in range(DEPTH):
        b, sm = big[l], layer_small[l]
        h, s1 = _ffn_fwd(h, sm["ln_ffn1"], b["ffn1_w13"], b["ffn1_w2"], "ln_ffn1")
        h, s2, gathered = _mixer_fwd(h, b, sm, conv_w[l], cs, sn, _gather_rider(packed_last) if l == carrier else None)
        if gathered is not None:
            big[l + 1] = _prepare_big(_unpack_gathered(gathered))
        h, s3 = _ffn_fwd(h, sm["ln_ffn2"], b["ffn2_w13"], b["ffn2_w2"], "ln_ffn2")
        saved.append((s1, s2, s3))
    loss, dh = _loss_and_grad(h, target)

    d_big, d_small, d_conv_w = [None] * DEPTH, [None] * DEPTH, [None] * DEPTH
    for l in reversed(range(DEPTH)):
        b, sm = big[l], layer_small[l]
        s1, s2, s3 = saved[l]
        dh, d_w13_2, d_w2_2, d_ln2 = _ffn_bwd(dh, s3, sm["ln_ffn2"], b["ffn2_w13"], b["ffn2_w2"], "ln_ffn2")
        carry_parts = None
        if l == carrier:
            carry_parts = _grad_parts(d_big[l + 1], BIG_NAMES) + _grad_parts({"ffn2_w13": d_w13_2, "ffn2_w2": d_w2_2}, EARLY[:2])
        dh, db, ds, d_conv_w[l], received = _mixer_bwd(dh, s2, b, sm, conv_w[l], cs, sn, carry_parts)
        if received is not None:
            d_big[l + 1] = received
        dh, d_w13_1, d_w2_1, d_ln1 = _ffn_bwd(dh, s1, sm["ln_ffn1"], b["ffn1_w13"], b["ffn1_w2"], "ln_ffn1")
        db.update(ffn1_w13=d_w13_1, ffn1_w2=d_w2_1, ffn2_w13=d_w13_2, ffn2_w2=d_w2_2,
                  w_in=_restore_w_in(db["w_in"]), w_uq=_unpad_heads(db["w_uq"]))
        ds.update(ln_ffn1=d_ln1, ln_ffn2=d_ln2)
        d_big[l], d_small[l] = db, ds
    d_small = {name: jnp.stack([d_small[l][name] for l in range(DEPTH)]) for name, _ in SMALL}
    return loss, dh, d_big, d_small, jnp.stack(d_conv_w)


def _step(args):
    dev = 4 * lax.axis_index("x") + 2 * lax.axis_index("y") + lax.axis_index("c")
    x, positions, target = args["x"][0], args["positions"][0], args["loss_target"][0]

    packed = [_pack_shards({name: args[name][l].astype(BF16) for name, _, _ in BIG}) for l in range(DEPTH)]
    big = [_unpack_gathered(_all_gather(packed[l])) for l in range(DEPTH - 1)] + [None]
    cw = args["conv_w"]
    cw_cols = cw.shape[-1]
    cw_rows = -(-cw.size // (8 * PACK_COLS)) * 8
    cw_flat = jnp.concatenate([cw.reshape(-1), jnp.zeros((cw_rows * PACK_COLS - cw.size,), F32)]).reshape(cw_rows, PACK_COLS)
    cw_all = _all_gather(cw_flat).reshape(N_DEV, -1)[:, :cw.size].reshape(N_DEV, DEPTH, CONV_K, cw_cols)
    conv_w = jnp.transpose(cw_all, (1, 2, 0, 3)).reshape(DEPTH, CONV_K, CONV_DIM)
    small = {name: args[name] for name, _ in SMALL}

    loss, dx, d_big, d_small, d_conv_w = _local_step(x, positions, target, big, small, conv_w, packed_last=packed[-1])
    loss = lax.psum(loss, MESH_AXES)

    out = {"loss": loss, "grad_x": dx[None]}

    assert DEPTH == 2
    grads = {name: [None] * DEPTH for name in BIG_NAMES}
    summed = _sum_blocks(d_big[1])
    own, r = _unpack_parts(summed, BIG_NAMES)
    early, _ = _unpack_parts(summed, EARLY, r)
    late, _ = _unpack_parts(_sum_blocks(_exchange_blocks(_pad_parts(_grad_parts(d_big[0], LATE)))), LATE)
    for name in BIG_NAMES:
        grads[name] = [early[name] if name in EARLY else late[name], own[name]]
    flat = lambda t: t.reshape(-1, t.shape[-1])
    for name, _, _ in BIG:
        g = jnp.stack(grads[name])
        w = args[name]
        delta, m2, v2 = _adam(flat(w), flat(g), flat(args["m_" + name]), flat(args["v_" + name]))
        out["grad_" + name] = g
        out["delta_" + name] = delta.reshape(w.shape)
        out["new_m_" + name] = m2.reshape(w.shape)
        out["new_v_" + name] = v2.reshape(w.shape)

    total = _sum_blocks(_all_gather(_pack_small(d_small, d_conv_w)))
    g_conv_w = _unpack_small(total)[1]
    zeros_cw = jnp.zeros((DEPTH, CONV_K, CONV_DIM), F32)
    delta, m2, v2 = _adam(_pack_small(small, zeros_cw), total,
                          _pack_small({name: args["m_" + name] for name, _ in SMALL}, zeros_cw),
                          _pack_small({name: args["v_" + name] for name, _ in SMALL}, zeros_cw))
    for kind, packed in (("grad_", total), ("delta_", delta), ("new_m_", m2), ("new_v_", v2)):
        for name, val in _unpack_small(packed)[0].items():
            out[kind + name] = val
    g_cw = lax.dynamic_slice_in_dim(g_conv_w, dev * cw_cols, cw_cols, axis=2)
    delta, m2, v2 = _adam(flat(cw), flat(g_cw), flat(args["m_conv_w"]), flat(args["v_conv_w"]))
    out["grad_conv_w"] = g_cw
    out["delta_conv_w"] = delta.reshape(cw.shape)
    out["new_m_conv_w"] = m2.reshape(cw.shape)
    out["new_v_conv_w"] = v2.reshape(cw.shape)
    return out


WEIGHTS = ["ln_ffn1", "ffn1_w13", "ffn1_w2", "ln_mix", "w_in", "conv_w", "conv_b", "dt_bias", "a_log", "d_skip", "ssd_norm",
           "w_ssd_out", "q_lora_norm", "w_uq", "kv_lora_norm", "w_ukv", "q_norm", "k_norm", "w_mla_out", "w_o", "ln_ffn2",
           "ffn2_w13", "ffn2_w2"]
ARG_NAMES = (["x", "positions"] + WEIGHTS + ["loss_target"] + ["m_" + n for n in WEIGHTS] + ["v_" + n for n in WEIGHTS])


def kernel(x, positions, ln_ffn1, ffn1_w13, ffn1_w2, ln_mix, w_in, conv_w, conv_b, dt_bias, a_log, d_skip, ssd_norm, w_ssd_out, q_lora_norm, w_uq, kv_lora_norm, w_ukv, q_norm, k_norm, w_mla_out, w_o, ln_ffn2, ffn2_w13, ffn2_w2, loss_target, m_ln_ffn1, m_ffn1_w13, m_ffn1_w2, m_ln_mix, m_w_in, m_conv_w, m_conv_b, m_dt_bias, m_a_log, m_d_skip, m_ssd_norm, m_w_ssd_out, m_q_lora_norm, m_w_uq, m_kv_lora_norm, m_w_ukv, m_q_norm, m_k_norm, m_w_mla_out, m_w_o, m_ln_ffn2, m_ffn2_w13, m_ffn2_w2, v_ln_ffn1, v_ffn1_w13, v_ffn1_w2, v_ln_mix, v_w_in, v_conv_w, v_conv_b, v_dt_bias, v_a_log, v_d_skip, v_ssd_norm, v_w_ssd_out, v_q_lora_norm, v_w_uq, v_kv_lora_norm, v_w_ukv, v_q_norm, v_k_norm, v_w_mla_out, v_w_o, v_ln_ffn2, v_ffn2_w13, v_ffn2_w2):
    vals = (x, positions, ln_ffn1, ffn1_w13, ffn1_w2, ln_mix, w_in, conv_w, conv_b, dt_bias, a_log, d_skip, ssd_norm, w_ssd_out, q_lora_norm, w_uq, kv_lora_norm, w_ukv, q_norm, k_norm, w_mla_out, w_o, ln_ffn2, ffn2_w13, ffn2_w2, loss_target, m_ln_ffn1, m_ffn1_w13, m_ffn1_w2, m_ln_mix, m_w_in, m_conv_w, m_conv_b, m_dt_bias, m_a_log, m_d_skip, m_ssd_norm, m_w_ssd_out, m_q_lora_norm, m_w_uq, m_kv_lora_norm, m_w_ukv, m_q_norm, m_k_norm, m_w_mla_out, m_w_o, m_ln_ffn2, m_ffn2_w13, m_ffn2_w2, v_ln_ffn1, v_ffn1_w13, v_ffn1_w2, v_ln_mix, v_w_in, v_conv_w, v_conv_b, v_dt_bias, v_a_log, v_d_skip, v_ssd_norm, v_w_ssd_out, v_q_lora_norm, v_w_uq, v_kv_lora_norm, v_w_ukv, v_q_norm, v_k_norm, v_w_mla_out, v_w_o, v_ln_ffn2, v_ffn2_w13, v_ffn2_w2)
    out = _step(dict(zip(ARG_NAMES, vals)))
    order = ["loss", "grad_x"] + [k + n for k in ("grad_", "delta_", "new_m_", "new_v_") for n in WEIGHTS]
    return tuple(out[n] for n in order)
```

```python
import jax
import jax.numpy as jnp
from jax import lax
from jax.experimental import pallas as pl
from jax.experimental.pallas import tpu as pltpu

F32 = jnp.float32
BF16 = jnp.bfloat16

D_MODEL = 1024
D_FF = 2816
DEPTH = 2
SSD_DI = 2048
SSD_P = 64
SSD_H = 32
SSD_G = 4
SSD_HPG = 8
SSD_N = 128
SSD_L = 128
CONV_K = 4
CONV_DIM = 3072
MLA_H = 8
Q_LORA = 512
KV_LORA = 256
NOPE = 128
ROPE = 64
VDIM = 128
QK = 192
ROPE_THETA = 10000.0
EPS = 1e-6
IN_SPLIT = (SSD_DI, CONV_DIM, SSD_H, Q_LORA, KV_LORA, ROPE, 2 * D_MODEL)
D_IN = sum(IN_SPLIT)
N_DEV = 8
LANE = 128
PACK_COLS = 1024

PROJ_Z = 0
PROJ_GATES = PROJ_Z + SSD_DI
PROJ_XBC = PROJ_GATES + 2 * D_MODEL
PROJ_CQ = PROJ_XBC + CONV_DIM
PROJ_CKV = PROJ_CQ + Q_LORA
PROJ_LAST = PROJ_CKV + KV_LORA
D_IN_PAD = PROJ_LAST + LANE

ADAM_LR = 0.001
ADAM_B1 = 0.9
ADAM_B2 = 0.999
ADAM_EPS = 1e-08
ADAM_WD = 0.01
ADAM_STEP = 10

VMEM_LIMIT = 48 * 1024 * 1024
ROW_IO_BUDGET = 8 * 1024 * 1024
NEG = -1e30

MESH_AXES = ("x", "y", "c")


def _cparams(*sem):
    return pltpu.CompilerParams(dimension_semantics=sem, vmem_limit_bytes=VMEM_LIMIT)


def _pick_tile(n, target, align):
    if n <= target:
        return n
    best = None
    for t in range(align, target + 1, align):
        if n % t == 0:
            best = t
    assert best is not None, (n, target, align)
    return best


def _acc_store(ref, val, first):
    @pl.when(first)
    def _():
        ref[...] = val

    @pl.when(jnp.logical_not(first))
    def _():
        ref[...] += val


def _win(arr, start, width):
    assert start % width == 0, (start, width)
    return (arr, start, width)


def _operand(entry):
    if isinstance(entry, tuple):
        arr, start, width = entry
        return arr, width, start // width
    return entry, entry.shape[1], 0


def _row_tile(rows, bytes_per_row):
    if rows <= 16:
        return rows
    t = 1024
    while t > 16 and (t * bytes_per_row > ROW_IO_BUDGET or rows % t):
        t //= 2
    assert rows % t == 0, (rows, t)
    return t


def _rowwise_call(fn, tiled, params, outs, accs, name):
    ops = [_operand(e) for e in tiled]
    rows = ops[0][0].shape[0]
    per_row = sum(w * a.dtype.itemsize for a, w, _ in ops) + sum(c * jnp.dtype(d).itemsize for c, d in outs)
    tile = _row_tile(rows, per_row)
    n_in = len(tiled) + len(params)
    n_o = len(outs)

    def body(*refs):
        vals = [r[...] for r in refs[:n_in]]
        t_out, a_out = fn(*vals)
        for r, v in zip(refs[n_in:n_in + n_o], t_out):
            r[...] = v.astype(r.dtype)
        first = pl.program_id(0) == 0
        for r, v in zip(refs[n_in + n_o:], a_out):
            _acc_store(r, v.astype(F32), first)

    def tiled_spec(width, blk):
        return pl.BlockSpec((tile, width), lambda i: (i, blk))

    in_specs = [tiled_spec(w, blk) for _, w, blk in ops]
    in_specs += [pl.BlockSpec(p.shape, lambda i: (0, 0)) for p in params]
    out_specs = [tiled_spec(c, 0) for c, _ in outs]
    out_specs += [pl.BlockSpec(s, lambda i: (0, 0)) for s in accs]
    out_shape = [jax.ShapeDtypeStruct((rows, c), d) for c, d in outs]
    out_shape += [jax.ShapeDtypeStruct(s, F32) for s in accs]
    return pl.pallas_call(
        body, grid=(rows // tile,), in_specs=in_specs, out_specs=out_specs, out_shape=out_shape,
        compiler_params=_cparams("arbitrary"), name=name,
    )(*[a for a, _, _ in ops], *params)


def _to_f32(vals):
    return [v.astype(F32) for v in vals]


def _row_fwd(f, tiled, params, out_dtypes, name):
    ops = [_operand(e) for e in tiled]
    rows = ops[0][0].shape[0]
    shapes = jax.eval_shape(f, *[jax.ShapeDtypeStruct((rows, w), F32) for _, w, _ in ops],
                            *[jax.ShapeDtypeStruct(p.shape, F32) for p in params])
    outs = [(s.shape[1], d) for s, d in zip(shapes, out_dtypes)]
    return _rowwise_call(lambda *v: (f(*_to_f32(v)), ()), tiled, params, outs, [], name)


def _row_bwd(f, tiled, params, gs, d_dtypes, name, bwd=None, add=None):
    n_t, n_g = len(tiled), len(gs)
    adds = sorted((add or {}).items())
    n_a = len(adds)

    def fn(*vals):
        vals = _to_f32(vals)
        prim = vals[:n_t] + vals[n_t + n_g + n_a:]
        g = tuple(vals[n_t:n_t + n_g])
        if bwd is not None:
            d_t, d_p = bwd(*prim, *g)
        else:
            _, vjp = jax.vjp(f, *prim)
            cts = vjp(g)
            d_t, d_p = cts[:n_t], cts[n_t:]
        d_t = list(d_t)
        for (idx, _), extra in zip(adds, vals[n_t + n_g:n_t + n_g + n_a]):
            d_t[idx] = d_t[idx] + extra
        return tuple(d_t), tuple(d_p)

    outs = [(_operand(e)[1], d) for e, d in zip(tiled, d_dtypes)]
    accs = [p.shape for p in params]
    res = _rowwise_call(fn, list(tiled) + list(gs) + [a for _, a in adds], params, outs, accs, name)
    return res[:n_t], res[n_t:]


def _f_rmsnorm(x, g):
    return (x * lax.rsqrt(jnp.mean(x * x, axis=-1, keepdims=True) + EPS) * g,)


def _f_gated_norm(ys, xs, z, dsk, g):
    t = (ys + xs * dsk) * (z * jax.nn.sigmoid(z))
    return (t * lax.rsqrt(jnp.mean(t * t, axis=-1, keepdims=True) + EPS) * g,)


def _f_merge(gates, ys, ym):
    s = jax.nn.sigmoid(gates)
    return (s[:, :D_MODEL] * ys + s[:, D_MODEL:] * ym,)


def _b_merge(gates, ys, ym, d):
    s = jax.nn.sigmoid(gates)
    s1, s2 = s[:, :D_MODEL], s[:, D_MODEL:]
    d_gates = jnp.concatenate([d * ys * s1 * (1.0 - s1), d * ym * s2 * (1.0 - s2)], axis=1)
    return (d_gates, d * s1, d * s2), ()


def _loss_and_grad(y, target):
    def fn(yv, tv):
        d = yv - tv
        return (d * (1.0 / D_MODEL),), (jnp.sum(d * d, axis=0, keepdims=True) * (0.5 / D_MODEL),)

    dy, part = _rowwise_call(fn, [y, target], [], [(D_MODEL, F32)], [(1, D_MODEL)], "loss")
    return jnp.sum(part), dy


def _adam(w, g, m, v):
    def fn(wv, gv, mv, vv):
        m2 = ADAM_B1 * mv + (1.0 - ADAM_B1) * gv
        v2 = ADAM_B2 * vv + (1.0 - ADAM_B2) * (gv * gv)
        m_hat = m2 / (1.0 - ADAM_B1 ** ADAM_STEP)
        v_hat = v2 / (1.0 - ADAM_B2 ** ADAM_STEP)
        delta = -ADAM_LR * (m_hat / (jnp.sqrt(v_hat) + ADAM_EPS) + ADAM_WD * wv)
        return (delta, m2, v2), ()

    c = w.shape[1]
    return _rowwise_call(fn, [w, g, m, v], [], [(c, F32)] * 3, [], "adamw")


def _sum_blocks(blocks):
    _, rows, c = blocks.shape
    tile = _row_tile(rows, N_DEV * c * blocks.dtype.itemsize + c * 4)

    def body(b_ref, o_ref):
        acc = b_ref[0].astype(F32)
        for i in range(1, N_DEV):
            acc = acc + b_ref[i].astype(F32)
        o_ref[...] = acc

    return pl.pallas_call(
        body, grid=(rows // tile,), in_specs=[pl.BlockSpec((N_DEV, tile, c), lambda i: (0, i, 0))],
        out_specs=pl.BlockSpec((tile, c), lambda i: (i, 0)), out_shape=jax.ShapeDtypeStruct((rows, c), F32),
        compiler_params=_cparams("arbitrary"), name="sum_blocks",
    )(blocks)


def _mm(a, b, ta=False, tb=False, out_dtype=F32, alpha=1.0, res=None, b_rows=None, norm_bwd=None):
    r_dim, p_dim = a.shape if ta else a.shape[::-1]
    b_row0, b_nrows = (0, b.shape[0]) if b_rows is None else b_rows
    r2, q_dim = (b.shape[1], b_nrows) if tb else (b_nrows, b.shape[1])
    assert r_dim == r2, (a.shape, b.shape, ta, tb)
    tp = _pick_tile(p_dim, 512, LANE)
    if tp < 512 < p_dim:
        tp = _pick_tile(p_dim, 1536, LANE)
    tq = _pick_tile(q_dim, 1536, LANE)
    tr = _pick_tile(r_dim, 1536, LANE)
    nr = r_dim // tr
    dims = (((0 if ta else 1,), (1 if tb else 0,)), ((), ()))
    has_res = res is not None
    n_nb = 0 if norm_bwd is None else 3
    assert norm_bwd is None or tq == q_dim

    def body(*refs):
        a_ref, b_ref = refs[:2]
        res_ref = refs[2] if has_res else None
        n_in = 2 + has_res + n_nb
        o_ref = refs[n_in]

        def finish(val):
            if alpha != 1.0:
                val = val * alpha
            if has_res:
                val = val + res_ref[...].astype(F32)
            if norm_bwd is not None:
                x_ref, g_ref, add_ref = refs[2 + has_res:n_in]
                x = x_ref[...]
                r = lax.rsqrt(jnp.mean(x * x, axis=-1, keepdims=True) + EPS)
                gy = val * g_ref[...]
                dot = jnp.sum(gy * x, axis=-1, keepdims=True)
                _acc_store(refs[n_in + 1], jnp.sum(val * x * r, axis=0, keepdims=True), pl.program_id(1) == 0)
                val = gy * r - x * (dot * (r * r * r) * (1.0 / q_dim)) + add_ref[...]
            o_ref[...] = val.astype(o_ref.dtype)

        part = lax.dot_general(a_ref[...].astype(BF16), b_ref[...].astype(BF16), dims, preferred_element_type=F32)
        if nr == 1:
            finish(part)
        else:
            acc_ref = refs[-1]
            k = pl.program_id(2)
            _acc_store(acc_ref, part, k == 0)

            @pl.when(k == nr - 1)
            def _():
                finish(acc_ref[...])

    a_spec = pl.BlockSpec((tr, tp), lambda j, i, k: (k, i)) if ta else pl.BlockSpec((tp, tr), lambda j, i, k: (i, k))
    assert b_row0 % (tq if tb else tr) == 0
    b0 = b_row0 // (tq if tb else tr)
    b_spec = pl.BlockSpec((tq, tr), lambda j, i, k: (j + b0, k)) if tb else pl.BlockSpec((tr, tq), lambda j, i, k: (k + b0, j))
    o_spec = pl.BlockSpec((tp, tq), lambda j, i, k: (i, j))
    row_spec = pl.BlockSpec((1, tq), lambda j, i, k: (0, 0))
    in_specs = [a_spec, b_spec] + ([o_spec] if has_res else []) + ([o_spec, row_spec, o_spec] if n_nb else [])
    out = pl.pallas_call(
        body, grid=(q_dim // tq, p_dim // tp, nr), in_specs=in_specs,
        out_specs=[o_spec] + ([row_spec] if n_nb else []),
        out_shape=[jax.ShapeDtypeStruct((p_dim, q_dim), out_dtype)] + ([jax.ShapeDtypeStruct((1, q_dim), F32)] if n_nb else []),
        scratch_shapes=[pltpu.VMEM((tp, tq), F32)] if nr > 1 else [],
        compiler_params=_cparams("arbitrary", "arbitrary", "arbitrary"),
        name=f"mm_{'t' if ta else 'n'}{'t' if tb else 'n'}_{p_dim}x{r_dim}x{q_dim}" + ("_norm_bwd" if n_nb else ""),
    )(*([a, b] + ([res] if has_res else []) + (list(norm_bwd) if n_nb else [])))
    return out if n_nb else out[0]


MERGE_TP = 256


def _merge_out_call(proj, y_ssd, y_mla, w_o, h):
    s = h.shape[0]
    tp = min(MERGE_TP, s)

    def body(g_ref, ys_ref, ym_ref, w_ref, h_ref, o_ref, mg_ref):
        mg = _f_merge(g_ref[...], ys_ref[...], ym_ref[...])[0].astype(BF16)
        mg_ref[...] = mg
        o_ref[...] = h_ref[...] + jnp.dot(mg, w_ref[...], preferred_element_type=F32)

    rows = pl.BlockSpec((tp, D_MODEL), lambda i: (i, 0))
    return pl.pallas_call(
        body, grid=(s // tp,),
        in_specs=[pl.BlockSpec((tp, 2 * D_MODEL), lambda i: (i, PROJ_GATES // (2 * D_MODEL))), rows, rows,
                  pl.BlockSpec((D_MODEL, D_MODEL), lambda i: (0, 0)), rows],
        out_specs=[rows, rows],
        out_shape=[jax.ShapeDtypeStruct((s, D_MODEL), F32), jax.ShapeDtypeStruct((s, D_MODEL), BF16)],
        compiler_params=_cparams("arbitrary"), name="merge_out",
    )(proj, y_ssd, y_mla, w_o, h)


def _merge_out_bwd_call(dh, w_o, proj, y_ssd, y_mla):
    s = dh.shape[0]
    tp = min(MERGE_TP, s)

    def body(dh_ref, w_ref, g_ref, ys_ref, ym_ref, dg_ref, dys_ref, dym_ref):
        d_mg = _nt(dh_ref[...].astype(BF16), w_ref[...])
        (d_g, d_ys, d_ym), _ = _b_merge(g_ref[...], ys_ref[...], ym_ref[...], d_mg)
        dg_ref[...] = d_g.astype(BF16)
        dys_ref[...] = d_ys.astype(BF16)
        dym_ref[...] = d_ym.astype(BF16)

    rows = pl.BlockSpec((tp, D_MODEL), lambda i: (i, 0))
    wide = pl.BlockSpec((tp, 2 * D_MODEL), lambda i: (i, 0))
    return pl.pallas_call(
        body, grid=(s // tp,),
        in_specs=[rows, pl.BlockSpec((D_MODEL, D_MODEL), lambda i: (0, 0)),
                  pl.BlockSpec((tp, 2 * D_MODEL), lambda i: (i, PROJ_GATES // (2 * D_MODEL))), rows, rows],
        out_specs=[wide, rows, rows],
        out_shape=[jax.ShapeDtypeStruct((s, 2 * D_MODEL), BF16), jax.ShapeDtypeStruct((s, D_MODEL), BF16),
                   jax.ShapeDtypeStruct((s, D_MODEL), BF16)],
        compiler_params=_cparams("arbitrary"), name="merge_out_bwd",
    )(dh, w_o, proj, y_ssd, y_mla)


def _attn_out_bwd_call(d_y, w_out, o_rows):
    s = d_y.shape[0]
    tp = min(MERGE_TP, s)
    wide = MLA_H * VDIM

    def body(dy_ref, w_ref, o_ref, do_ref, delta_ref):
        d_o = _nt(dy_ref[...], w_ref[...]).astype(BF16)
        do_ref[...] = d_o
        col = lax.broadcasted_iota(jnp.int32, (wide, MLA_H), 0)
        head = lax.broadcasted_iota(jnp.int32, (wide, MLA_H), 1)
        per_head = _ones_where(lax.shift_right_logical(col, VDIM.bit_length() - 1) == head)
        delta_ref[...] = _dot_sel_r(d_o.astype(F32) * o_ref[...].astype(F32), per_head)

    rows = lambda c: pl.BlockSpec((tp, c), lambda i: (i, 0))
    return pl.pallas_call(
        body, grid=(s // tp,),
        in_specs=[rows(D_MODEL), pl.BlockSpec((wide, D_MODEL), lambda i: (0, 0)), rows(wide)],
        out_specs=[rows(wide), rows(MLA_H)],
        out_shape=[jax.ShapeDtypeStruct((s, wide), BF16), jax.ShapeDtypeStruct((s, MLA_H), F32)],
        compiler_params=_cparams("arbitrary"), name="attn_out_bwd",
    )(d_y, w_out, o_rows)


FFN_TP = 512
FFN_TQ = 1408


def _ffn_up_call(n, w13_t):
    s, d = n.shape
    tp = min(FFN_TP, s)
    up0 = D_FF // FFN_TQ

    def body(n_ref, wg_ref, wu_ref, act_ref, gate_ref, up_ref):
        a = n_ref[...]
        g = _nt(a, wg_ref[...])
        u = _nt(a, wu_ref[...])
        act_ref[...] = (g * jax.nn.sigmoid(g) * u).astype(BF16)
        gate_ref[...] = g.astype(BF16)
        up_ref[...] = u.astype(BF16)

    o_spec = pl.BlockSpec((tp, FFN_TQ), lambda j, i: (i, j))
    return pl.pallas_call(
        body, grid=(D_FF // FFN_TQ, s // tp),
        in_specs=[pl.BlockSpec((tp, d), lambda j, i: (i, 0)), pl.BlockSpec((FFN_TQ, d), lambda j, i: (j, 0)),
                  pl.BlockSpec((FFN_TQ, d), lambda j, i: (j + up0, 0))],
        out_specs=[o_spec] * 3, out_shape=[jax.ShapeDtypeStruct((s, D_FF), BF16)] * 3,
        compiler_params=_cparams("arbitrary", "arbitrary"), name="ffn_up_swiglu",
    )(n, w13_t, w13_t)


def _ffn_down_bwd_call(dh, w2, gate, up):
    s, d = dh.shape
    tp = min(FFN_TP, s)

    def body(dh_ref, w2_ref, gate_ref, up_ref, dg_ref, du_ref):
        d_act = 0.5 * _nt(dh_ref[...].astype(BF16), w2_ref[...])
        g, u = gate_ref[...].astype(F32), up_ref[...].astype(F32)
        sg = jax.nn.sigmoid(g)
        dg_ref[...] = (d_act * u * sg * (1.0 + g * (1.0 - sg))).astype(BF16)
        du_ref[...] = (d_act * g * sg).astype(BF16)

    o_spec = pl.BlockSpec((tp, FFN_TQ), lambda j, i: (i, j))
    return pl.pallas_call(
        body, grid=(D_FF // FFN_TQ, s // tp),
        in_specs=[pl.BlockSpec((tp, d), lambda j, i: (i, 0)), pl.BlockSpec((FFN_TQ, d), lambda j, i: (j, 0)), o_spec, o_spec],
        out_specs=[o_spec] * 2, out_shape=[jax.ShapeDtypeStruct((s, D_FF), BF16)] * 2,
        compiler_params=_cparams("arbitrary", "arbitrary"), name="ffn_down_bwd_swiglu",
    )(dh, w2, gate, up)


ATTN_SCALE = QK ** -0.5
LOG2E = 1.4426950408889634
ATTN_C = ATTN_SCALE * LOG2E


ATTN_HEADS = 2
ATTN_HEADS_FWD = 4


def _attn_tile(s):
    return min(512, s)


def _causal_keep(t, keys_on_rows=False):
    row = lax.broadcasted_iota(jnp.int32, (t, t), 0)
    col = lax.broadcasted_iota(jnp.int32, (t, t), 1)
    return row <= col if keys_on_rows else col <= row


def _nt(a, b):
    return lax.dot_general(a, b, (((1,), (1,)), ((), ())), preferred_element_type=F32)


def _rider_phases(rider, src_ref, out_ref, sems, first, last):
    @pl.when(first)
    def _():
        _peer_copies(src_ref, out_ref, sems, rider["gather"], "start")

    def finish():
        @pl.when(last)
        def _():
            _peer_copies(src_ref, out_ref, sems, rider["gather"], "finish")

    return finish


def _attn_fwd_call(q, k, v, rider=None):
    nh, s, _ = q.shape
    t = _attn_tile(s)
    nb = s // t
    hp = ATTN_HEADS_FWD
    n_r = 0 if rider is None else 1

    def body(*refs):
        q_ref, k_ref, v_ref = refs[:3]
        o_ref, lse_ref = refs[3 + n_r:5 + n_r]
        qi = pl.program_id(1)
        finish = None
        if rider is not None:
            h = pl.program_id(0)
            finish = _rider_phases(rider, refs[3:4], refs[5 + n_r], refs[6 + n_r:],
                                   jnp.logical_and(h == 0, qi == 0), jnp.logical_and(h == nh // hp - 1, qi == nb - 1))
        qs = [q_ref[i] for i in range(hp)]

        def block(kb, carries, diagonal, width=1):
            start = pl.multiple_of(kb * t, t)
            out = []
            for i, (m_prev, l_prev, acc) in enumerate(carries):
                sc = _nt(qs[i], k_ref[i, pl.ds(start, width * t), :])
                if diagonal:
                    sc = jnp.where(_causal_keep(t), sc, NEG)
                m_new = jnp.maximum(m_prev, jnp.max(sc, axis=-1, keepdims=True))
                p = jnp.exp2(sc * ATTN_C - m_new * ATTN_C)
                alpha = jnp.exp2((m_prev - m_new) * ATTN_C)
                l_new = alpha * l_prev + jnp.sum(p, axis=-1, keepdims=True)
                pv = jnp.dot(p.astype(BF16), v_ref[i, pl.ds(start, width * t), :], preferred_element_type=F32)
                out.append((m_new, l_new, alpha * acc + pv))
            return tuple(out)

        init = tuple((jnp.full((t, 1), NEG, F32), jnp.zeros((t, 1), F32), jnp.zeros((t, VDIM), F32)) for _ in range(hp))
        carries = lax.fori_loop(0, qi // 2, lambda j, c: block(2 * j, c, False, width=2), init)
        carries = lax.cond(qi % 2 == 1, lambda c: block(qi - 1, c, False), lambda c: c, carries)
        for i, (m, l, acc) in enumerate(block(qi, carries, True)):
            o_ref[i] = (acc / l).astype(o_ref.dtype)
            lse_ref[i] = m * ATTN_SCALE + jnp.log(l)
        if finish is not None:
            finish()

    qmap = lambda h, i: (h, i, 0)
    whole = lambda h, i: (h, 0, 0)
    return pl.pallas_call(
        body, grid=(nh // hp, nb),
        in_specs=[pl.BlockSpec((hp, t, QK), qmap), pl.BlockSpec((hp, s, QK), whole, pipeline_mode=pl.Buffered(buffer_count=1)),
                  pl.BlockSpec((hp, s, VDIM), whole, pipeline_mode=pl.Buffered(buffer_count=1))] + [HBM_SPEC] * n_r,
        out_specs=[pl.BlockSpec((hp, t, VDIM), qmap), pl.BlockSpec((hp, t, 1), qmap)] + [HBM_SPEC] * n_r,
        out_shape=[jax.ShapeDtypeStruct((nh, s, VDIM), BF16), jax.ShapeDtypeStruct((nh, s, 1), F32)] + ([rider["out"]] if n_r else []),
        scratch_shapes=_comm_scratch() if n_r else [],
        compiler_params=_cparams("arbitrary", "arbitrary"), name="attn_fwd_gather" if n_r else "attn_fwd",
    )(*([q, k, v] + (rider["srcs"] if n_r else [])))


def _attn_bwd_call(q, k, v, do, lse_t, delta_t, rider=None):
    nh, s, _ = q.shape
    t = _attn_tile(s)
    nb = s // t
    hp = ATTN_HEADS
    n_src = 0 if rider is None else len(rider["srcs"])
    n_r = 0 if rider is None else 1

    def body(*refs):
        q_ref, k_ref, v_ref, do_ref, lse_ref, delta_ref = refs[:6]
        dq_ref, dk_ref, dv_ref = refs[6 + n_src:9 + n_src]
        dk_sc, dv_sc = refs[9 + n_src + n_r:11 + n_src + n_r]
        kj = pl.program_id(1)
        finish = None
        if rider is not None:
            h = pl.program_id(0)
            finish = _rider_phases(rider, refs[6:6 + n_src], refs[9 + n_src], refs[11 + n_src + n_r:],
                                   jnp.logical_and(h == 0, kj == 0), jnp.logical_and(h == nh // hp - 1, kj == nb - 1))

        @pl.when(kj == 0)
        def _():
            dq_ref[...] = jnp.zeros_like(dq_ref)

        dk_sc[...] = jnp.zeros_like(dk_sc)
        dv_sc[...] = jnp.zeros_like(dv_sc)
        kblks = [k_ref[i] for i in range(hp)]
        vblks = [v_ref[i] for i in range(hp)]

        def block(qb, diagonal):
            start = pl.multiple_of(qb * t, t)
            for i in range(hp):
                qblk = q_ref[i, pl.ds(start, t), :]
                doblk = do_ref[i, pl.ds(start, t), :]
                sc = _nt(kblks[i], qblk)
                if diagonal:
                    sc = jnp.where(_causal_keep(t, keys_on_rows=True), sc, NEG)
                p = jnp.exp2(sc * ATTN_C - lse_ref[i, :, pl.ds(start, t)] * LOG2E)
                dv_sc[i] += jnp.dot(p.astype(BF16), doblk, preferred_element_type=F32)
                dp = _nt(vblks[i], doblk)
                ds = (p * (dp - delta_ref[i, :, pl.ds(start, t)])).astype(BF16)
                dk_sc[i] += jnp.dot(ds, qblk, preferred_element_type=F32)
                dq_ref[i, pl.ds(start, t), :] += lax.dot_general(ds, kblks[i], (((0,), (0,)), ((), ())), preferred_element_type=F32)

        block(kj, True)

        def rest(qb, carry):
            block(qb, False)
            return carry

        lax.fori_loop(kj + 1, nb, rest, 0)
        dk_ref[...] = (dk_sc[...] * ATTN_SCALE).astype(dk_ref.dtype)
        dv_ref[...] = dv_sc[...].astype(dv_ref.dtype)

        @pl.when(kj == nb - 1)
        def _():
            dq_ref[...] = dq_ref[...] * ATTN_SCALE

        if finish is not None:
            finish()

    kmap = lambda h, j: (h, j, 0)
    whole = lambda h, j: (h, 0, 0)
    once = pl.Buffered(buffer_count=1)
    return pl.pallas_call(
        body, grid=(nh // hp, nb),
        in_specs=[pl.BlockSpec((hp, s, QK), whole, pipeline_mode=once), pl.BlockSpec((hp, t, QK), kmap), pl.BlockSpec((hp, t, VDIM), kmap),
                  pl.BlockSpec((hp, s, VDIM), whole, pipeline_mode=once), pl.BlockSpec((hp, 1, s), whole, pipeline_mode=once),
                  pl.BlockSpec((hp, 1, s), whole, pipeline_mode=once)] + [HBM_SPEC] * n_src,
        out_specs=[pl.BlockSpec((hp, s, QK), whole, pipeline_mode=once), pl.BlockSpec((hp, t, QK), kmap),
                   pl.BlockSpec((hp, t, VDIM), kmap)] + [HBM_SPEC] * n_r,
        out_shape=[jax.ShapeDtypeStruct((nh, s, QK), F32), jax.ShapeDtypeStruct((nh, s, QK), F32),
                   jax.ShapeDtypeStruct((nh, s, VDIM), F32)] + ([rider["out"]] if n_r else []),
        scratch_shapes=[pltpu.VMEM((hp, t, QK), F32), pltpu.VMEM((hp, t, VDIM), F32)] + (_comm_scratch() if n_r else []),
        compiler_params=_cparams("arbitrary", "arbitrary"), name="attn_bwd_exchange" if n_r else "attn_bwd",
    )(*([q, k, v, do, lse_t, delta_t] + (rider["srcs"] if n_r else [])))


HEAD_COLS = NOPE + VDIM
HEADS_TILE = 256


def _swap_rope_halves(t, lane):
    half = ROPE // 2
    return jnp.where(lane < half, pltpu.roll(t, LANE - half, 1), pltpu.roll(t, half, 1))


def _head_fwd(n, p, gain, cs, sn, lane):
    r = lax.rsqrt((jnp.sum(n * n, axis=-1, keepdims=True) + jnp.sum(p * p, axis=-1, keepdims=True)) * (1.0 / QK) + EPS)
    yp = p * r * gain[:, NOPE:]
    return n * r * gain[:, :NOPE], yp * cs + _swap_rope_halves(yp, lane) * sn


def _head_bwd(n, p, gain, cs, sn, lane, dzn, dzp):
    r = lax.rsqrt((jnp.sum(n * n, axis=-1, keepdims=True) + jnp.sum(p * p, axis=-1, keepdims=True)) * (1.0 / QK) + EPS)
    dyp = dzp * cs + _swap_rope_halves(dzp * sn, lane)
    gyn, gyp = dzn * gain[:, :NOPE], dyp * gain[:, NOPE:]
    dot = jnp.sum(gyn * n, axis=-1, keepdims=True) + jnp.sum(gyp * p, axis=-1, keepdims=True)
    coef = dot * (r * r * r) * (1.0 / QK)
    d_gn = jnp.sum(dzn * n * r, axis=0, keepdims=True)
    d_gp = jnp.sum(dyp * p * r, axis=0, keepdims=True)
    return gyn * r - n * coef, gyp * r - p * coef, d_gn, d_gp


def _heads_fwd_call(q, kv, proj, cs, sn, q_gain, k_gain):
    s = q.shape[0]
    t = min(HEADS_TILE, s)

    def body(q_ref, kv_ref, last_ref, cs_ref, sn_ref, qg_ref, kg_ref, qh_ref, kh_ref, vh_ref):
        lane = lax.broadcasted_iota(jnp.int32, (t, LANE), 1)
        cs_, sn_ = cs_ref[...], sn_ref[...]
        kp = jnp.where(lane < ROPE, last_ref[...], 0.0)
        for h in range(MLA_H):
            c0 = h * HEAD_COLS
            zn, zp = _head_fwd(q_ref[:, c0:c0 + NOPE], q_ref[:, c0 + NOPE:c0 + HEAD_COLS], qg_ref[...], cs_, sn_, lane)
            qh_ref[h, :, :NOPE] = zn.astype(BF16)
            qh_ref[h, :, NOPE:] = zp[:, :ROPE].astype(BF16)
            zn, zp = _head_fwd(kv_ref[:, c0:c0 + NOPE], kp, kg_ref[...], cs_, sn_, lane)
            kh_ref[h, :, :NOPE] = zn.astype(BF16)
            kh_ref[h, :, NOPE:] = zp[:, :ROPE].astype(BF16)
            vh_ref[h] = kv_ref[:, c0 + NOPE:c0 + HEAD_COLS].astype(BF16)

    rows = lambda i: (i, 0)
    whole = lambda i: (0, 0)
    heads = lambda i: (0, i, 0)
    wide = MLA_H * HEAD_COLS
    return pl.pallas_call(
        body, grid=(s // t,),
        in_specs=[pl.BlockSpec((t, wide), rows), pl.BlockSpec((t, wide), rows),
                  pl.BlockSpec((t, LANE), lambda i: (i, PROJ_LAST // LANE)),
                  pl.BlockSpec((t, LANE), rows), pl.BlockSpec((t, LANE), rows),
                  pl.BlockSpec((1, HEAD_COLS), whole), pl.BlockSpec((1, HEAD_COLS), whole)],
        out_specs=[pl.BlockSpec((MLA_H, t, QK), heads), pl.BlockSpec((MLA_H, t, QK), heads), pl.BlockSpec((MLA_H, t, VDIM), heads)],
        out_shape=[jax.ShapeDtypeStruct((MLA_H, s, QK), BF16), jax.ShapeDtypeStruct((MLA_H, s, QK), BF16),
                   jax.ShapeDtypeStruct((MLA_H, s, VDIM), BF16)],
        compiler_params=_cparams("arbitrary"), name="mla_heads_fwd",
    )(q, kv, proj, cs, sn, q_gain, k_gain)


def _heads_bwd_call(q, kv, proj, cs, sn, q_gain, k_gain, dqh, dkh, dvh):
    s = q.shape[0]
    t = min(HEADS_TILE, s)

    def body(q_ref, kv_ref, last_ref, cs_ref, sn_ref, qg_ref, kg_ref, dqh_ref, dkh_ref, dvh_ref,
             dq_ref, dkv_ref, dkr_ref, dqg_ref, dkg_ref):
        lane = lax.broadcasted_iota(jnp.int32, (t, LANE), 1)
        cs_, sn_ = cs_ref[...], sn_ref[...]
        kp = jnp.where(lane < ROPE, last_ref[...], 0.0)
        no_lanes = jnp.zeros((t, LANE - ROPE), F32)
        d_kp = jnp.zeros((t, LANE), F32)
        d_qg = [jnp.zeros((1, NOPE), F32), jnp.zeros((1, LANE), F32)]
        d_kg = [jnp.zeros((1, NOPE), F32), jnp.zeros((1, LANE), F32)]
        for h in range(MLA_H):
            c0 = h * HEAD_COLS
            dz = dqh_ref[h]
            dzp = jnp.concatenate([dz[:, NOPE:], no_lanes], axis=1)
            d_n, d_p, g_n, g_p = _head_bwd(q_ref[:, c0:c0 + NOPE], q_ref[:, c0 + NOPE:c0 + HEAD_COLS], qg_ref[...],
                                           cs_, sn_, lane, dz[:, :NOPE], dzp)
            dq_ref[:, c0:c0 + NOPE] = d_n.astype(dq_ref.dtype)
            dq_ref[:, c0 + NOPE:c0 + HEAD_COLS] = d_p.astype(dq_ref.dtype)
            d_qg = [d_qg[0] + g_n, d_qg[1] + g_p]
            dz = dkh_ref[h]
            dzp = jnp.concatenate([dz[:, NOPE:], no_lanes], axis=1)
            d_n, d_p, g_n, g_p = _head_bwd(kv_ref[:, c0:c0 + NOPE], kp, kg_ref[...], cs_, sn_, lane, dz[:, :NOPE], dzp)
            dkv_ref[:, c0:c0 + NOPE] = d_n.astype(dkv_ref.dtype)
            dkv_ref[:, c0 + NOPE:c0 + HEAD_COLS] = dvh_ref[h].astype(dkv_ref.dtype)
            d_kp = d_kp + d_p
            d_kg = [d_kg[0] + g_n, d_kg[1] + g_p]
        dkr_ref[...] = d_kp
        first = pl.program_id(0) == 0
        _acc_store(dqg_ref.at[:, pl.ds(0, NOPE)], d_qg[0], first)
        _acc_store(dqg_ref.at[:, pl.ds(NOPE, LANE)], d_qg[1], first)
        _acc_store(dkg_ref.at[:, pl.ds(0, NOPE)], d_kg[0], first)
        _acc_store(dkg_ref.at[:, pl.ds(NOPE, LANE)], d_kg[1], first)

    rows = lambda i: (i, 0)
    whole = lambda i: (0, 0)
    heads = lambda i: (0, i, 0)
    wide = MLA_H * HEAD_COLS
    return pl.pallas_call(
        body, grid=(s // t,),
        in_specs=[pl.BlockSpec((t, wide), rows), pl.BlockSpec((t, wide), rows),
                  pl.BlockSpec((t, LANE), lambda i: (i, PROJ_LAST // LANE)),
                  pl.BlockSpec((t, LANE), rows), pl.BlockSpec((t, LANE), rows),
                  pl.BlockSpec((1, HEAD_COLS), whole), pl.BlockSpec((1, HEAD_COLS), whole),
                  pl.BlockSpec((MLA_H, t, QK), heads), pl.BlockSpec((MLA_H, t, QK), heads), pl.BlockSpec((MLA_H, t, VDIM), heads)],
        out_specs=[pl.BlockSpec((t, wide), rows), pl.BlockSpec((t, wide), rows), pl.BlockSpec((t, LANE), rows),
                   pl.BlockSpec((1, HEAD_COLS), whole), pl.BlockSpec((1, HEAD_COLS), whole)],
        out_shape=[jax.ShapeDtypeStruct((s, wide), BF16), jax.ShapeDtypeStruct((s, wide), BF16), jax.ShapeDtypeStruct((s, LANE), F32),
                   jax.ShapeDtypeStruct((1, HEAD_COLS), F32), jax.ShapeDtypeStruct((1, HEAD_COLS), F32)],
        compiler_params=_cparams("arbitrary"), name="mla_heads_bwd",
    )(q, kv, proj, cs, sn, q_gain, k_gain, dqh, dkh, dvh)


CONV_TC = 512
HALO = 8


def _conv_tiles(s):
    return min(512, s)


def _conv_fwd_call(x, col0, w, b):
    s = x.shape[0]
    ts = _conv_tiles(s)
    hb = ts // HALO
    c0 = col0 // CONV_TC
    assert col0 % CONV_TC == 0

    def body(x_ref, prev_ref, w_ref, b_ref, y_ref, buf):
        si = pl.program_id(1)
        buf[0:HALO, :] = jnp.where(si > 0, prev_ref[...], 0.0)
        buf[HALO:, :] = x_ref[...]
        acc = jnp.broadcast_to(b_ref[...], (ts, CONV_TC))
        for k in range(CONV_K):
            acc = acc + w_ref[k:k + 1, :] * buf[pl.ds(HALO - (CONV_K - 1) + k, ts), :]
        y_ref[...] = acc * jax.nn.sigmoid(acc)

    return pl.pallas_call(
        body, grid=(CONV_DIM // CONV_TC, s // ts),
        in_specs=[pl.BlockSpec((ts, CONV_TC), lambda ci, si: (si, ci + c0)),
                  pl.BlockSpec((HALO, CONV_TC), lambda ci, si: (jnp.maximum(si * hb - 1, 0), ci + c0)),
                  pl.BlockSpec((CONV_K, CONV_TC), lambda ci, si: (0, ci)),
                  pl.BlockSpec((1, CONV_TC), lambda ci, si: (0, ci))],
        out_specs=pl.BlockSpec((ts, CONV_TC), lambda ci, si: (si, ci)),
        out_shape=jax.ShapeDtypeStruct((s, CONV_DIM), F32),
        scratch_shapes=[pltpu.VMEM((ts + HALO, CONV_TC), F32)],
        compiler_params=_cparams("arbitrary", "arbitrary"), name="conv_fwd",
    )(x, x, w, b)


def _conv_bwd_call(x, col0, w, b, dy):
    s = x.shape[0]
    ts = _conv_tiles(s)
    hb = ts // HALO
    ns = s // ts
    last_halo = s // HALO - 1
    c0 = col0 // CONV_TC

    def body(x_ref, prev_ref, next_ref, dy_ref, dyn_ref, w_ref, b_ref, dx_ref, dw_ref, db_ref, xbuf, dbuf):
        si = pl.program_id(1)
        xbuf[0:HALO, :] = jnp.where(si > 0, prev_ref[...], 0.0)
        xbuf[HALO:HALO + ts, :] = x_ref[...]
        xbuf[HALO + ts:, :] = next_ref[...]
        pre = jnp.broadcast_to(b_ref[...], (ts + HALO, CONV_TC))
        for k in range(CONV_K):
            pre = pre + w_ref[k:k + 1, :] * xbuf[pl.ds(HALO - (CONV_K - 1) + k, ts + HALO), :]
        sg = jax.nn.sigmoid(pre)
        dsilu = sg * (1.0 + pre * (1.0 - sg))
        dbuf[0:ts, :] = dy_ref[...] * dsilu[0:ts]
        dbuf[ts:, :] = jnp.where(si < ns - 1, dyn_ref[...] * dsilu[ts:], 0.0)
        dx = jnp.zeros((ts, CONV_TC), F32)
        for k in range(CONV_K):
            dx = dx + w_ref[k:k + 1, :] * dbuf[pl.ds(CONV_K - 1 - k, ts), :]
        dx_ref[...] = dx.astype(dx_ref.dtype)
        dpre = dbuf[0:ts, :]
        first = si == 0
        _acc_store(db_ref, jnp.sum(dpre, axis=0, keepdims=True), first)
        for k in range(CONV_K):
            dw_k = jnp.sum(dpre * xbuf[pl.ds(HALO - (CONV_K - 1) + k, ts), :], axis=0, keepdims=True)
            _acc_store(dw_ref.at[pl.ds(k, 1), :], dw_k, first)

    main = lambda ci, si: (si, ci)
    x_main = lambda ci, si: (si, ci + c0)
    x_prev = lambda ci, si: (jnp.maximum(si * hb - 1, 0), ci + c0)
    x_next = lambda ci, si: (jnp.minimum(si * hb + hb, last_halo), ci + c0)
    return pl.pallas_call(
        body, grid=(CONV_DIM // CONV_TC, ns),
        in_specs=[pl.BlockSpec((ts, CONV_TC), x_main), pl.BlockSpec((HALO, CONV_TC), x_prev), pl.BlockSpec((HALO, CONV_TC), x_next),
                  pl.BlockSpec((ts, CONV_TC), main),
                  pl.BlockSpec((HALO, CONV_TC), lambda ci, si: (jnp.minimum(si * hb + hb, last_halo), ci)),
                  pl.BlockSpec((CONV_K, CONV_TC), lambda ci, si: (0, ci)),
                  pl.BlockSpec((1, CONV_TC), lambda ci, si: (0, ci))],
        out_specs=[pl.BlockSpec((ts, CONV_TC), main),
                   pl.BlockSpec((CONV_K, CONV_TC), lambda ci, si: (0, ci)),
                   pl.BlockSpec((1, CONV_TC), lambda ci, si: (0, ci))],
        out_shape=[jax.ShapeDtypeStruct((s, CONV_DIM), BF16), jax.ShapeDtypeStruct((CONV_K, CONV_DIM), F32),
                   jax.ShapeDtypeStruct((1, CONV_DIM), F32)],
        scratch_shapes=[pltpu.VMEM((ts + 2 * HALO, CONV_TC), F32), pltpu.VMEM((ts + HALO, CONV_TC), F32)],
        compiler_params=_cparams("arbitrary", "arbitrary"), name="conv_bwd",
    )(x, x, x, dy, dy, w, b)


GW = SSD_HPG * SSD_P
B_COL = SSD_DI
C_COL = SSD_DI + SSD_G * SSD_N


def _ones_where(mask):
    return jnp.where(mask, 1.0, 0.0).astype(BF16)


def _split(v, passes):
    parts, rest = [], v
    for i in range(passes):
        part = rest.astype(BF16)
        parts.append(part)
        if i + 1 < passes:
            rest = rest - part.astype(F32)
    return parts


def _dot_sel_r(v, sel, passes=3):
    out = None
    for part in _split(v, passes):
        t = jnp.dot(part, sel, preferred_element_type=F32)
        out = t if out is None else out + t
    return out


def _dot_sel_l(sel, v, passes=3):
    out = None
    for part in _split(v, passes):
        t = jnp.dot(sel, part, preferred_element_type=F32)
        out = t if out is None else out + t
    return out


def _ssd_consts():
    r = lax.broadcasted_iota(jnp.int32, (SSD_L, SSD_L), 0)
    c = lax.broadcasted_iota(jnp.int32, (SSD_L, SSD_L), 1)
    tril = r >= c
    triu = c >= r
    shift = SSD_P.bit_length() - 1
    eh = lax.broadcasted_iota(jnp.int32, (SSD_H, SSD_DI), 0)
    ej = lax.broadcasted_iota(jnp.int32, (SSD_H, SSD_DI), 1)
    expand = _ones_where(lax.shift_right_logical(ej, shift) == eh)
    rj = lax.broadcasted_iota(jnp.int32, (SSD_DI, SSD_H), 0)
    rh = lax.broadcasted_iota(jnp.int32, (SSD_DI, SSD_H), 1)
    reduce_ = _ones_where(lax.shift_right_logical(rj, shift) == rh)
    lane = lax.broadcasted_iota(jnp.int32, (SSD_L, LANE), 1)
    return tril, triu, expand, reduce_, lane < SSD_P


def _ssd_decays(dt, dt_t, a, a_t, tril, triu, expand):
    dta = dt * a
    acum = _dot_sel_l(_ones_where(tril), dta)
    acum_t = _dot_sel_r(dt_t * a_t, _ones_where(triu))
    dta_e = _dot_sel_r(dta, expand)
    acum_e = _dot_sel_r(acum, expand)
    last_e = jnp.sum(dta_e, axis=0, keepdims=True)
    return acum, acum_t, acum_e, last_e


def _head_decay(acum, acum_t, h, tril):
    seg = acum[:, h:h + 1] - acum_t[h:h + 1, :]
    return jnp.exp(jnp.where(tril, seg, NEG))


def _ssd_fwd_call(xbc, dt, a):
    s = xbc.shape[0]
    nc = s // SSD_L
    dt_t = dt.T
    a_t = a.T

    def body(xbc_ref, dt_ref, dtt_ref, a_ref, at_ref, y_ref, st_ref, s_sc):
        ci = pl.program_id(0)

        @pl.when(ci == 0)
        def _():
            s_sc[...] = jnp.zeros_like(s_sc)

        st_ref[0] = s_sc[...]
        tril, triu, expand, _, low_half = _ssd_consts()
        acum, acum_t, acum_e, last_e = _ssd_decays(dt_ref[...], dtt_ref[...], a_ref[...], at_ref[...], tril, triu, expand)
        dt_e = _dot_sel_r(dt_ref[...], expand, passes=2)
        xdt = xbc_ref[:, :SSD_DI] * dt_e
        xdt_b = xdt.astype(BF16)
        xw_b = (xdt * jnp.exp(last_e - acum_e)).astype(BF16)
        ea_e = jnp.exp(acum_e)
        el_e = jnp.exp(last_e)
        for g in range(SSD_G):
            gs = slice(g * GW, (g + 1) * GW)
            bg = xbc_ref[:, B_COL + g * SSD_N:B_COL + (g + 1) * SSD_N]
            cg_b = xbc_ref[:, C_COL + g * SSD_N:C_COL + (g + 1) * SSD_N].astype(BF16)
            bg_b = bg.astype(BF16)
            cb = _nt(cg_b, bg_b)
            st = s_sc[:, gs]
            y_off = jnp.dot(cg_b, st.astype(BF16), preferred_element_type=F32) * ea_e[:, gs]
            for pr in range(SSD_HPG // 2):
                ls = slice(g * GW + pr * LANE, g * GW + (pr + 1) * LANE)
                xp = xdt_b[:, ls]
                yd = []
                for half in range(2):
                    h = g * SSD_HPG + pr * 2 + half
                    m = (cb * _head_decay(acum, acum_t, h, tril)).astype(BF16)
                    yd.append(jnp.dot(m, xp, preferred_element_type=F32))
                y_ref[:, ls] = jnp.where(low_half, yd[0], yd[1]) + y_off[:, pr * LANE:(pr + 1) * LANE]
            s_sc[:, gs] = st * el_e[:, gs] + jnp.dot(bg.T.astype(BF16), xw_b[:, gs], preferred_element_type=F32)

    row = lambda i: (i, 0)
    return pl.pallas_call(
        body, grid=(nc,),
        in_specs=[pl.BlockSpec((SSD_L, CONV_DIM), row), pl.BlockSpec((SSD_L, SSD_H), row),
                  pl.BlockSpec((SSD_H, SSD_L), lambda i: (0, i)), pl.BlockSpec((1, SSD_H), lambda i: (0, 0)),
                  pl.BlockSpec((SSD_H, 1), lambda i: (0, 0))],
        out_specs=[pl.BlockSpec((SSD_L, SSD_DI), row), pl.BlockSpec((1, SSD_N, SSD_DI), lambda i: (i, 0, 0))],
        out_shape=[jax.ShapeDtypeStruct((s, SSD_DI), F32), jax.ShapeDtypeStruct((nc, SSD_N, SSD_DI), F32)],
        scratch_shapes=[pltpu.VMEM((SSD_N, SSD_DI), F32)],
        compiler_params=_cparams("arbitrary"), name="ssd_fwd",
    )(xbc, dt, dt_t, a, a_t)


def _ssd_bwd_call(xbc, dt, a, states, dy, dx_extra):
    s = xbc.shape[0]
    nc = s // SSD_L
    dt_t = dt.T
    a_t = a.T

    def body(xbc_ref, dt_ref, dtt_ref, a_ref, at_ref, st_ref, dy_ref, dxe_ref,
             dxbc_ref, ddt_ref, da_ref, ds_sc, yf_sc, dxd_sc, dxw_sc):
        i = pl.program_id(0)

        @pl.when(i == 0)
        def _():
            ds_sc[...] = jnp.zeros_like(ds_sc)

        tril, triu, expand, reduce_, low_half = _ssd_consts()
        dt = dt_ref[...]
        a_row = a_ref[...]
        acum, acum_t, acum_e, last_e = _ssd_decays(dt, dtt_ref[...], a_row, at_ref[...], tril, triu, expand)
        dt_e = _dot_sel_r(dt, expand, passes=2)
        x = xbc_ref[:, :SSD_DI]
        xdt = x * dt_e
        xdt_b = xdt.astype(BF16)
        w_e = jnp.exp(last_e - acum_e)
        xw_b = (xdt * w_e).astype(BF16)
        ea_e = jnp.exp(acum_e)
        el_e = jnp.exp(last_e)
        dy = dy_ref[...]
        dy_b = dy.astype(BF16)
        s_prev = st_ref[0]
        ds_new = ds_sc[...]
        ds_new_b = ds_new.astype(BF16)
        triu_b = _ones_where(triu)
        strict_tril = jnp.logical_not(triu)
        head_ids = lax.broadcasted_iota(jnp.int32, (1, SSD_H), 1)
        d_dta_diag = jnp.zeros((SSD_L, SSD_H), F32)
        for g in range(SSD_G):
            gs = slice(g * GW, (g + 1) * GW)
            bs_ = slice(B_COL + g * SSD_N, B_COL + (g + 1) * SSD_N)
            cs_ = slice(C_COL + g * SSD_N, C_COL + (g + 1) * SSD_N)
            bg = xbc_ref[:, bs_]
            cg = xbc_ref[:, cs_]
            bg_b, cg_b = bg.astype(BF16), cg.astype(BF16)
            st_b = s_prev[:, gs].astype(BF16)
            y_off = jnp.dot(cg_b, st_b, preferred_element_type=F32) * ea_e[:, gs]
            yf_sc[:, gs] = y_off
            dz_b = (dy[:, gs] * ea_e[:, gs]).astype(BF16)
            d_c = _nt(dz_b, st_b)
            ds_prev = ds_new[:, gs] * el_e[:, gs] + jnp.dot(cg.T.astype(BF16), dz_b, preferred_element_type=F32)
            dxw_sc[:, gs] = jnp.dot(bg_b, ds_new_b[:, gs], preferred_element_type=F32)
            d_b = _nt(xw_b[:, gs], ds_new_b[:, gs])
            cb = _nt(cg_b, bg_b)
            d_g = jnp.zeros((SSD_L, SSD_L), F32)
            for pr in range(SSD_HPG // 2):
                ls = slice(g * GW + pr * LANE, g * GW + (pr + 1) * LANE)
                xp = xdt_b[:, ls]
                dyp = dy[:, ls]
                dyp_b = dy_b[:, ls]
                dxd = []
                for half in range(2):
                    h = g * SSD_HPG + pr * 2 + half
                    dec = _head_decay(acum, acum_t, h, tril)
                    m = cb * dec
                    dxd.append(jnp.dot(m.T.astype(BF16), dyp_b, preferred_element_type=F32))
                    mine = low_half if half == 0 else jnp.logical_not(low_half)
                    d_m = _nt(jnp.where(mine, dyp, 0.0).astype(BF16), xp)
                    d_g = d_g + d_m * dec
                    below = jnp.dot(triu_b, (d_m * m).astype(BF16), preferred_element_type=F32)
                    col = jnp.sum(jnp.where(strict_tril, below, 0.0), axis=1, keepdims=True)
                    d_dta_diag = d_dta_diag + col * jnp.where(head_ids == h, 1.0, 0.0)
                dxd_sc[:, ls] = jnp.where(low_half, dxd[0], dxd[1])
            d_g_b = d_g.astype(BF16)
            dxbc_ref[:, cs_] = d_c + jnp.dot(d_g_b, bg_b, preferred_element_type=F32)
            dxbc_ref[:, bs_] = d_b + jnp.dot(d_g.T.astype(BF16), cg_b, preferred_element_type=F32)
            ds_sc[:, gs] = ds_prev
        dxw = dxw_sc[...]
        dxd = dxd_sc[...]
        dw_e = xdt * dxw * w_e
        d_tot_e = jnp.sum(ds_new * s_prev, axis=0, keepdims=True) * el_e
        d_state_e = (_dot_sel_l(triu_b, dy * yf_sc[...], passes=2)
                     + _dot_sel_l(_ones_where(strict_tril), dw_e, passes=2) + d_tot_e)
        dxdt = dxd + dxw * w_e
        dxbc_ref[:, :SSD_DI] = dxdt * dt_e + dxe_ref[...]
        a_e = _dot_sel_r(jnp.broadcast_to(a_row, (8, SSD_H)), expand)[0:1]
        ddt_ref[...] = _dot_sel_r(d_state_e * a_e + dxdt * x, reduce_, passes=2) + d_dta_diag * a_row
        d_a_e = jnp.sum(d_state_e * dt_e, axis=0, keepdims=True)
        d_a = _dot_sel_r(jnp.broadcast_to(d_a_e, (8, SSD_DI)), reduce_)[0:1] + jnp.sum(d_dta_diag * dt, axis=0, keepdims=True)
        _acc_store(da_ref, d_a, i == 0)

    rev = lambda i: (nc - 1 - i, 0)
    return pl.pallas_call(
        body, grid=(nc,),
        in_specs=[pl.BlockSpec((SSD_L, CONV_DIM), rev), pl.BlockSpec((SSD_L, SSD_H), rev),
                  pl.BlockSpec((SSD_H, SSD_L), lambda i: (0, nc - 1 - i)), pl.BlockSpec((1, SSD_H), lambda i: (0, 0)),
                  pl.BlockSpec((SSD_H, 1), lambda i: (0, 0)),
                  pl.BlockSpec((1, SSD_N, SSD_DI), lambda i: (nc - 1 - i, 0, 0)),
                  pl.BlockSpec((SSD_L, SSD_DI), rev), pl.BlockSpec((SSD_L, SSD_DI), rev)],
        out_specs=[pl.BlockSpec((SSD_L, CONV_DIM), rev), pl.BlockSpec((SSD_L, SSD_H), rev),
                   pl.BlockSpec((1, SSD_H), lambda i: (0, 0))],
        out_shape=[jax.ShapeDtypeStruct((s, CONV_DIM), F32), jax.ShapeDtypeStruct((s, SSD_H), F32),
                   jax.ShapeDtypeStruct((1, SSD_H), F32)],
        scratch_shapes=[pltpu.VMEM((SSD_N, SSD_DI), F32), pltpu.VMEM((SSD_L, SSD_DI), F32),
                        pltpu.VMEM((SSD_L, SSD_DI), F32), pltpu.VMEM((SSD_L, SSD_DI), F32)],
        compiler_params=_cparams("arbitrary"), name="ssd_bwd",
    )(xbc, dt, dt_t, a, a_t, states, dy, dx_extra)


HBM_SPEC = pl.BlockSpec(memory_space=pltpu.HBM)
N_PEERS = N_DEV - 1


def _flip(v, f):
    return 1 - v if f else v


def _all_gather(shard):
    rows, c = shard.shape

    def body(x_ref, out_ref, send_sems, recv_sems, local_sem):
        x, y, cc = lax.axis_index("x"), lax.axis_index("y"), lax.axis_index("c")
        me, sibling = (x, y, cc), (x, y, 1 - cc)
        chips = [(1 - x, y), (x, 1 - y), (1 - x, 1 - y)]

        def slot(px, py, pc):
            return out_ref.at[4 * px + 2 * py + pc]

        def copy(k, block, to, src=None):
            return pltpu.make_async_remote_copy(
                src_ref=slot(*block) if src is None else src, dst_ref=slot(*block),
                send_sem=send_sems.at[k], recv_sem=recv_sems.at[k],
                device_id=to, device_id_type=pl.DeviceIdType.MESH)

        mine = pltpu.make_async_copy(x_ref, slot(*me), local_sem)
        mine.start()
        first = [copy(0, me, sibling, src=x_ref)]
        first += [copy(1 + j, me, (*chip, cc), src=x_ref) for j, chip in enumerate(chips)]
        for cp in first:
            cp.start()
        passed = [copy(4 + j, (*chip, cc), sibling) for j, chip in enumerate(chips)]
        for j, chip in enumerate(chips):
            copy(1 + j, (*chip, cc), me).wait_recv()
            passed[j].start()
        copy(0, sibling, me).wait_recv()
        for j, chip in enumerate(chips):
            copy(4 + j, (*chip, 1 - cc), me).wait_recv()
        for cp in first + passed:
            cp.wait_send()
        mine.wait()

    return pl.pallas_call(
        body, out_shape=jax.ShapeDtypeStruct((N_DEV, rows, c), shard.dtype),
        in_specs=[HBM_SPEC], out_specs=HBM_SPEC,
        scratch_shapes=[pltpu.SemaphoreType.DMA((N_PEERS,)), pltpu.SemaphoreType.DMA((N_PEERS,)), pltpu.SemaphoreType.DMA(())],
        name="all_gather",
    )(shard)


def _peer_copies(src_refs, out_ref, sems, gather, phase):
    send_sems, recv_sems, local_sem = sems
    x, y, cc = lax.axis_index("x"), lax.axis_index("y"), lax.axis_index("c")
    me = 4 * x + 2 * y + cc

    def pieces(block, slot):
        if gather:
            return [(src_refs[0], out_ref.at[slot])]
        out, r0 = [], 0
        for src in src_refs:
            out.append((src.at[block], out_ref.at[slot, pl.ds(r0, src.shape[1])]))
            r0 += src.shape[1]
        assert r0 == out_ref.shape[1], (r0, out_ref.shape)
        return out

    if phase == "start":
        for src, dst in pieces(me, me):
            pltpu.make_async_copy(src, dst, local_sem).start()
    for k in range(1, N_DEV):
        px, py, pc = _flip(x, k & 4), _flip(y, k & 2), _flip(cc, k & 1)
        peer = 4 * px + 2 * py + pc
        to_peer = dict(send_sem=send_sems.at[k - 1], recv_sem=recv_sems.at[k - 1],
                       device_id=(px, py, pc), device_id_type=pl.DeviceIdType.MESH)
        if phase == "start":
            for src, dst in pieces(peer, me):
                pltpu.make_async_remote_copy(src_ref=src, dst_ref=dst, **to_peer).start()
        else:
            whole = pltpu.make_async_remote_copy(src_ref=out_ref.at[peer], dst_ref=out_ref.at[peer], **to_peer)
            whole.wait_recv()
            whole.wait_send()
    if phase != "start":
        pltpu.make_async_copy(out_ref.at[me], out_ref.at[me], local_sem).wait()


def _comm_scratch():
    return [pltpu.SemaphoreType.DMA((N_PEERS,)), pltpu.SemaphoreType.DMA((N_PEERS,)), pltpu.SemaphoreType.DMA(())]


def _gather_rider(shard):
    return dict(srcs=[shard], out=jax.ShapeDtypeStruct((N_DEV,) + shard.shape, shard.dtype), gather=True)


def _exchange_out(parts):
    rows = sum(p.shape[1] for p in parts)
    return jax.ShapeDtypeStruct((N_DEV, rows) + parts[0].shape[2:], parts[0].dtype)


def _exchange_rider(parts):
    return dict(srcs=list(parts), out=_exchange_out(parts), gather=False)


def _exchange_blocks(parts):
    n = len(parts)

    def body(*refs):
        _peer_copies(refs[:n], refs[n], refs[n + 1:], False, "start")
        _peer_copies(refs[:n], refs[n], refs[n + 1:], False, "finish")

    return pl.pallas_call(
        body, out_shape=_exchange_out(parts),
        in_specs=[HBM_SPEC] * n, out_specs=HBM_SPEC, scratch_shapes=_comm_scratch(), name="exchange_blocks",
    )(*parts)


BIG = [
    ("ffn1_w13", (D_MODEL, 2 * D_FF), 1), ("ffn1_w2", (D_FF, D_MODEL), 0),
    ("w_ssd_out", (SSD_DI, D_MODEL), 0), ("w_uq", (Q_LORA, MLA_H * QK), 1), ("w_ukv", (KV_LORA, MLA_H * (NOPE + VDIM)), 1),
    ("w_mla_out", (MLA_H * VDIM, D_MODEL), 0), ("w_o", (D_MODEL, D_MODEL), 0),
    ("ffn2_w13", (D_MODEL, 2 * D_FF), 1), ("ffn2_w2", (D_FF, D_MODEL), 0), ("w_in", (D_MODEL, D_IN), 1),
]
assert all(_r % 16 == 0 for _r in [_f[0] * _f[1] // N_DEV // PACK_COLS for _, _f, _ in BIG[:-1]])
SMALL = [
    ("ln_ffn1", D_MODEL), ("ln_mix", D_MODEL), ("conv_b", CONV_DIM), ("dt_bias", SSD_H), ("a_log", SSD_H), ("d_skip", SSD_H),
    ("ssd_norm", SSD_DI), ("q_lora_norm", Q_LORA), ("kv_lora_norm", KV_LORA), ("q_norm", QK), ("k_norm", QK), ("ln_ffn2", D_MODEL),
]


def _shard_shape(full, axis):
    k, n = full
    return (k // N_DEV, n) if axis == 0 else (k, n // N_DEV)


def _shard_rows(full):
    return full[0] * full[1] // N_DEV // PACK_COLS


LAYER_ROWS = sum(_shard_rows(f) for _, f, _ in BIG)
LAYER_ROWS_PAD = -(-LAYER_ROWS // 256) * 256


def _pack_shards(shards):
    parts = [(shards[name] if axis == 0 else shards[name].T).reshape(-1, PACK_COLS) for name, _, axis in BIG]
    pad = LAYER_ROWS_PAD - LAYER_ROWS
    if pad:
        parts.append(jnp.zeros((pad, PACK_COLS), parts[0].dtype))
    return jnp.concatenate(parts, axis=0)


BIG_BY_NAME = {name: (full, axis) for name, full, axis in BIG}
BIG_NAMES = [name for name, _, _ in BIG]
EARLY = ["ffn2_w13", "ffn2_w2", "w_o", "w_mla_out", "w_ssd_out"]
LATE = [name for name in BIG_NAMES if name not in EARLY]
SUM_ROWS = 128


def _part_rows(name):
    return -(-_shard_rows(BIG_BY_NAME[name][0]) // 16) * 16


def _grad_parts(grads, names):
    parts = []
    for name in names:
        part = grads[name].reshape(N_DEV, -1, PACK_COLS)
        parts.append(jnp.pad(part, ((0, 0), (0, _part_rows(name) - part.shape[1]), (0, 0))))
    return parts


def _pad_parts(parts):
    pad = -sum(p.shape[1] for p in parts) % SUM_ROWS
    return parts + ([jnp.zeros((N_DEV, pad, PACK_COLS), parts[0].dtype)] if pad else [])


def _unpack_parts(summed, names, r=0):
    out = {}
    for name in names:
        full, axis = BIG_BY_NAME[name]
        k, c = _shard_shape(full, axis)
        blk = summed[r:r + _shard_rows(full)]
        out[name] = blk.reshape(k, c) if axis == 0 else blk.reshape(c, k).T
        r += _part_rows(name)
    return out, r


def _working_shape(full, axis):
    return full if axis == 0 else full[::-1]


def _unpack_gathered(gathered):
    out, r = {}, 0
    for name, full, axis in BIG:
        n = _shard_rows(full)
        out[name] = gathered[:, r:r + n].reshape(_working_shape(full, axis))
        r += n
    return out


SMALL_COLS = sum(n for _, n in SMALL) + CONV_K * CONV_DIM
SMALL_ROWS = -(-(DEPTH * SMALL_COLS) // (8 * PACK_COLS)) * 8


def _pack_small(vals, conv_w):
    flat = jnp.concatenate([vals[name] for name, _ in SMALL] + [conv_w.reshape(DEPTH, -1)], axis=1).reshape(-1)
    flat = jnp.concatenate([flat, jnp.zeros((SMALL_ROWS * PACK_COLS - flat.shape[0],), F32)])
    return flat.reshape(SMALL_ROWS, PACK_COLS)


def _unpack_small(packed):
    flat = packed.reshape(-1)[:DEPTH * SMALL_COLS].reshape(DEPTH, SMALL_COLS)
    out, c = {}, 0
    for name, n in SMALL:
        out[name] = flat[:, c:c + n]
        c += n
    return out, flat[:, c:].reshape(DEPTH, CONV_K, CONV_DIM)


_IN_OFFS = [sum(IN_SPLIT[:i]) for i in range(len(IN_SPLIT) + 1)]


def _arrange_w_in(w_t):
    z, xbc, dt, cq, ckv, kr, gates = [w_t[_IN_OFFS[i]:_IN_OFFS[i + 1]] for i in range(len(IN_SPLIT))]
    pad = jnp.zeros((LANE - ROPE - SSD_H, w_t.shape[1]), w_t.dtype)
    return jnp.concatenate([z, gates, xbc, cq, ckv, kr, dt, pad], axis=0)


def _restore_w_in(g):
    z, gates, xbc = g[PROJ_Z:PROJ_GATES], g[PROJ_GATES:PROJ_XBC], g[PROJ_XBC:PROJ_CQ]
    cq, ckv = g[PROJ_CQ:PROJ_CKV], g[PROJ_CKV:PROJ_LAST]
    kr, dt = g[PROJ_LAST:PROJ_LAST + ROPE], g[PROJ_LAST + ROPE:PROJ_LAST + ROPE + SSD_H]
    return jnp.concatenate([z, xbc, dt, cq, ckv, kr, gates], axis=0)


def _pad_heads(w_t):
    k = w_t.shape[1]
    return jnp.pad(w_t.reshape(MLA_H, QK, k), ((0, 0), (0, HEAD_COLS - QK), (0, 0))).reshape(MLA_H * HEAD_COLS, k)


def _unpad_heads(g):
    k = g.shape[1]
    return g.reshape(MLA_H, HEAD_COLS, k)[:, :QK].reshape(MLA_H * QK, k)


def _row(v):
    return v.reshape(1, -1)


def _head_gain(g):
    return jnp.pad(g, (0, HEAD_COLS - QK)).reshape(1, HEAD_COLS)


def _ffn_fwd(h, ln, w13_t, w2, name):
    n = _row_fwd(_f_rmsnorm, [h], [_row(ln)], [BF16], name + "_fwd")[0]
    act, gate, up = _ffn_up_call(n, w13_t)
    return _mm(act, w2, alpha=0.5, res=h), (h, n, gate, up, act)


def _ffn_bwd(dh_out, saved, ln, w13_t, w2, name):
    h, n, gate, up, act = saved
    d_gate, d_up = _ffn_down_bwd_call(dh_out, w2, gate, up)
    d_w2 = _mm(act, dh_out, ta=True, out_dtype=BF16, alpha=0.5)
    d_n = _mm(d_gate, w13_t, b_rows=(0, D_FF))
    dh, d_ln = _mm(d_up, w13_t, b_rows=(D_FF, D_FF), res=d_n, norm_bwd=(h, _row(ln), dh_out))
    d_w13_t = jnp.concatenate([_mm(d_gate, n, ta=True, out_dtype=BF16), _mm(d_up, n, ta=True, out_dtype=BF16)], axis=0)
    return dh, d_w13_t, d_w2, d_ln[0]


def _mixer_fwd(h, big, small, conv_w, cs, sn, rider=None):
    s = h.shape[0]
    u = _row_fwd(_f_rmsnorm, [h], [_row(small["ln_mix"])], [BF16], "ln_mix_fwd")[0]
    proj = _mm(u, big["w_in"], tb=True)
    xbc = _conv_fwd_call(proj, PROJ_XBC, conv_w, _row(small["conv_b"]))
    dt_in = proj[:, PROJ_LAST + ROPE:PROJ_LAST + ROPE + SSD_H] + small["dt_bias"][None, :]
    dt = jax.nn.softplus(dt_in)
    a = -jnp.exp(small["a_log"])[None, :]
    y_scan, states = _ssd_fwd_call(xbc, dt, a)
    dsk = _row(jnp.repeat(small["d_skip"], SSD_P))
    gn_in = [y_scan, _win(xbc, 0, SSD_DI), _win(proj, PROJ_Z, SSD_DI)]
    yn = _row_fwd(_f_gated_norm, gn_in, [dsk, _row(small["ssd_norm"])], [BF16], "gated_norm_fwd")[0]
    y_ssd = _mm(yn, big["w_ssd_out"])
    qn = _row_fwd(_f_rmsnorm, [_win(proj, PROJ_CQ, Q_LORA)], [_row(small["q_lora_norm"])], [BF16], "q_lora_norm_fwd")[0]
    kvn = _row_fwd(_f_rmsnorm, [_win(proj, PROJ_CKV, KV_LORA)], [_row(small["kv_lora_norm"])], [BF16], "kv_lora_norm_fwd")[0]
    q = _mm(qn, big["w_uq"], tb=True)
    kv = _mm(kvn, big["w_ukv"], tb=True)
    qh, kh, vh = _heads_fwd_call(q, kv, proj, cs, sn, _head_gain(small["q_norm"]), _head_gain(small["k_norm"]))
    o, lse, *carried = _attn_fwd_call(qh, kh, vh, rider)
    o_rows = jnp.transpose(o, (1, 0, 2)).reshape(s, MLA_H * VDIM)
    y_mla = _mm(o_rows, big["w_mla_out"])
    out, mg = _merge_out_call(proj, y_ssd, y_mla, big["w_o"], h)
    saved = (h, u, proj, xbc, dt_in, dt, a, y_scan, states, dsk, yn, y_ssd, qn, kvn, q, kv, qh, kh, vh, o, lse, o_rows, y_mla, mg)
    return out, saved, (carried[0] if carried else None)


def _mixer_bwd(dh_out, saved, big, small, conv_w, cs, sn, carry_parts=None):
    (h, u, proj, xbc, dt_in, dt, a, y_scan, states, dsk, yn, y_ssd, qn, kvn, q, kv, qh, kh, vh, o, lse, o_rows, y_mla, mg) = saved
    s = h.shape[0]
    d_big, d_small = {}, {}
    d_gates, d_y_ssd, d_y_mla = _merge_out_bwd_call(dh_out, big["w_o"], proj, y_ssd, y_mla)
    d_big["w_o"] = _mm(mg, dh_out, ta=True, out_dtype=BF16)
    d_o_rows, delta = _attn_out_bwd_call(d_y_mla, big["w_mla_out"], o_rows)
    d_big["w_mla_out"] = _mm(o_rows, d_y_mla, ta=True, out_dtype=BF16)
    d_o = jnp.transpose(d_o_rows.reshape(s, MLA_H, VDIM), (1, 0, 2))
    d_yn = _mm(d_y_ssd, big["w_ssd_out"], tb=True, out_dtype=BF16)
    d_big["w_ssd_out"] = _mm(yn, d_y_ssd, ta=True, out_dtype=BF16)
    rider = None
    if carry_parts is not None:
        rider = _exchange_rider(_pad_parts(carry_parts + _grad_parts(d_big, EARLY[2:])))
    *d_heads, carried = list(_attn_bwd_call(qh, kh, vh, d_o, lse.reshape(MLA_H, 1, s), delta.T.reshape(MLA_H, 1, s), rider)) + ([None] if rider is None else [])
    d_q, d_kv, d_kr, d_qg, d_kg = _heads_bwd_call(
        q, kv, proj, cs, sn, _head_gain(small["q_norm"]), _head_gain(small["k_norm"]), *d_heads)
    d_small["q_norm"], d_small["k_norm"] = d_qg[0, :QK], d_kg[0, :QK]
    d_qn = _mm(d_q, big["w_uq"], out_dtype=BF16)
    d_big["w_uq"] = _mm(d_q, qn, ta=True, out_dtype=BF16)
    d_kvn = _mm(d_kv, big["w_ukv"], out_dtype=BF16)
    d_big["w_ukv"] = _mm(d_kv, kvn, ta=True, out_dtype=BF16)
    (d_cq,), (d_g,) = _row_bwd(_f_rmsnorm, [_win(proj, PROJ_CQ, Q_LORA)], [_row(small["q_lora_norm"])], [d_qn], [BF16], "q_lora_norm_bwd")
    d_small["q_lora_norm"] = d_g[0]
    (d_ckv,), (d_g,) = _row_bwd(_f_rmsnorm, [_win(proj, PROJ_CKV, KV_LORA)], [_row(small["kv_lora_norm"])], [d_kvn], [BF16], "kv_lora_norm_bwd")
    d_small["kv_lora_norm"] = d_g[0]
    gn_in = [y_scan, _win(xbc, 0, SSD_DI), _win(proj, PROJ_Z, SSD_DI)]
    (d_y_scan, d_xs, d_z), (d_dsk, d_g) = _row_bwd(
        _f_gated_norm, gn_in, [dsk, _row(small["ssd_norm"])], [d_yn], [F32, F32, BF16], "gated_norm_bwd")
    d_small["ssd_norm"] = d_g[0]
    d_small["d_skip"] = jnp.sum(d_dsk.reshape(SSD_H, SSD_P), axis=1)
    d_xbc_act, d_dt, d_a = _ssd_bwd_call(xbc, dt, a, states, d_y_scan, d_xs)
    d_xbc, d_conv_w, d_conv_b = _conv_bwd_call(proj, PROJ_XBC, conv_w, _row(small["conv_b"]), d_xbc_act)
    d_small["conv_b"] = d_conv_b[0]
    d_dt_in = d_dt * jax.nn.sigmoid(dt_in)
    d_small["dt_bias"] = jnp.sum(d_dt_in, axis=0)
    d_small["a_log"] = d_a[0] * a[0]
    d_last = (d_kr + jnp.pad(d_dt_in, ((0, 0), (ROPE, LANE - ROPE - SSD_H)))).astype(BF16)
    d_proj = jnp.concatenate([d_z, d_gates, d_xbc, d_cq, d_ckv, d_last], axis=1)
    dh, d_ln = _mm(d_proj, big["w_in"], norm_bwd=(h, _row(small["ln_mix"]), dh_out))
    d_big["w_in"] = _mm(d_proj, u, ta=True, out_dtype=BF16)
    d_small["ln_mix"] = d_ln[0]
    return dh, d_big, d_small, d_conv_w, carried


def _prepare_big(b):
    return dict(b, w_in=_arrange_w_in(b["w_in"]), w_uq=_pad_heads(b["w_uq"]))


def _local_step(x, positions, target, big, small, conv_w, packed_last=None):
    inv = 1.0 / (ROPE_THETA ** (jnp.arange(0, ROPE, 2, dtype=F32) / ROPE))
    ang = positions.astype(F32)[:, None] * inv
    cos, sin = jnp.cos(ang), jnp.sin(ang)
    no_lanes = jnp.zeros((x.shape[0], LANE - ROPE), F32)
    cs = jnp.concatenate([cos, cos, no_lanes], axis=1)
    sn = jnp.concatenate([-sin, sin, no_lanes], axis=1)
    carrier = DEPTH - 2 if packed_last is not None else None
    big = [None if b is None else _prepare_big(b) for b in big]
    layer_small = [{k: v[l] for k, v in small.items()} for l in range(DEPTH)]

    h, saved = x, []
    for l in range(DEPTH):
        b, sm = big[l], layer_small[l]
        h, s1 = _ffn_fwd(h, sm["ln_ffn1"], b["ffn1_w13"], b["ffn1_w2"], "ln_ffn1")
        h, s2, gathered = _mixer_fwd(h, b, sm, conv_w[l], cs, sn, _gather_rider(packed_last) if l == carrier else None)
        if gathered is not None:
            big[l + 1] = _prepare_big(_unpack_gathered(gathered))
        h, s3 = _ffn_fwd(h, sm["ln_ffn2"], b["ffn2_w13"], b["ffn2_w2"], "ln_ffn2")
        saved.append((s1, s2, s3))
    loss, dh = _loss_and_grad(h, target)

    d_big, d_small, d_conv_w = [None] * DEPTH, [None] * DEPTH, [None] * DEPTH
    for l in reversed(range(DEPTH)):
        b, sm = big[l], layer_small[l]
        s1, s2, s3 = saved[l]
        dh, d_w13_2, d_w2_2, d_ln2 = _ffn_bwd(dh, s3, sm["ln_ffn2"], b["ffn2_w13"], b["ffn2_w2"], "ln_ffn2")
        carry_parts = None
        if l == carrier:
            carry_parts = _grad_parts(d_big[l + 1], BIG_NAMES) + _grad_parts({"ffn2_w13": d_w13_2, "ffn2_w2": d_w2_2}, EARLY[:2])
        dh, db, ds, d_conv_w[l], received = _mixer_bwd(dh, s2, b, sm, conv_w[l], cs, sn, carry_parts)
        if received is not None:
            d_big[l + 1] = received
        dh, d_w13_1, d_w2_1, d_ln1 = _ffn_bwd(dh, s1, sm["ln_ffn1"], b["ffn1_w13"], b["ffn1_w2"], "ln_ffn1")
        db.update(ffn1_w13=d_w13_1, ffn1_w2=d_w2_1, ffn2_w13=d_w13_2, ffn2_w2=d_w2_2,
                  w_in=_restore_w_in(db["w_in"]), w_uq=_unpad_heads(db["w_uq"]))
        ds.update(ln_ffn1=d_ln1, ln_ffn2=d_ln2)
        d_big[l], d_small[l] = db, ds
    d_small = {name: jnp.stack([d_small[l][name] for l in range(DEPTH)]) for name, _ in SMALL}
    return loss, dh, d_big, d_small, jnp.stack(d_conv_w)


def _step(args):
    dev = 4 * lax.axis_index("x") + 2 * lax.axis_index("y") + lax.axis_index("c")
    x, positions, target = args["x"][0], args["positions"][0], args["loss_target"][0]

    packed = [_pack_shards({name: args[name][l].astype(BF16) for name, _, _ in BIG}) for l in range(DEPTH)]
    big = [_unpack_gathered(_all_gather(packed[l])) for l in range(DEPTH - 1)] + [None]
    cw = args["conv_w"]
    cw_cols = cw.shape[-1]
    cw_rows = -(-cw.size // (8 * PACK_COLS)) * 8
    cw_flat = jnp.concatenate([cw.reshape(-1), jnp.zeros((cw_rows * PACK_COLS - cw.size,), F32)]).reshape(cw_rows, PACK_COLS)
    cw_all = _all_gather(cw_flat).reshape(N_DEV, -1)[:, :cw.size].reshape(N_DEV, DEPTH, CONV_K, cw_cols)
    conv_w = jnp.transpose(cw_all, (1, 2, 0, 3)).reshape(DEPTH, CONV_K, CONV_DIM)
    small = {name: args[name] for name, _ in SMALL}

    loss, dx, d_big, d_small, d_conv_w = _local_step(x, positions, target, big, small, conv_w, packed_last=packed[-1])
    loss = lax.psum(loss, MESH_AXES)

    out = {"loss": loss, "grad_x": dx[None]}

    assert DEPTH == 2
    grads = {name: [None] * DEPTH for name in BIG_NAMES}
    summed = _sum_blocks(d_big[1])
    own, r = _unpack_parts(summed, BIG_NAMES)
    early, _ = _unpack_parts(summed, EARLY, r)
    late, _ = _unpack_parts(_sum_blocks(_exchange_blocks(_pad_parts(_grad_parts(d_big[0], LATE)))), LATE)
    for name in BIG_NAMES:
        grads[name] = [early[name] if name in EARLY else late[name], own[name]]
    flat = lambda t: t.reshape(-1, t.shape[-1])
    for name, _, _ in BIG:
        g = jnp.stack(grads[name])
        w = args[name]
        delta, m2, v2 = _adam(flat(w), flat(g), flat(args["m_" + name]), flat(args["v_" + name]))
        out["grad_" + name] = g
        out["delta_" + name] = delta.reshape(w.shape)
        out["new_m_" + name] = m2.reshape(w.shape)
        out["new_v_" + name] = v2.reshape(w.shape)

    total = _sum_blocks(_all_gather(_pack_small(d_small, d_conv_w)))
    g_conv_w = _unpack_small(total)[1]
    zeros_cw = jnp.zeros((DEPTH, CONV_K, CONV_DIM), F32)
    delta, m2, v2 = _adam(_pack_small(small, zeros_cw), total,
                          _pack_small({name: args["m_" + name] for name, _ in SMALL}, zeros_cw),
                          _pack_small({name: args["v_" + name] for name, _ in SMALL}, zeros_cw))
    for kind, packed in (("grad_", total), ("delta_", delta), ("new_m_", m2), ("new_v_", v2)):
        for name, val in _unpack_small(packed)[0].items():
            out[kind + name] = val
    g_cw = lax.dynamic_slice_in_dim(g_conv_w, dev * cw_cols, cw_cols, axis=2)
    delta, m2, v2 = _adam(flat(cw), flat(g_cw), flat(args["m_conv_w"]), flat(args["v_conv_w"]))
    out["grad_conv_w"] = g_cw
    out["delta_conv_w"] = delta.reshape(cw.shape)
    out["new_m_conv_w"] = m2.reshape(cw.shape)
    out["new_v_conv_w"] = v2.reshape(cw.shape)
    return out


WEIGHTS = ["ln_ffn1", "ffn1_w13", "ffn1_w2", "ln_mix", "w_in", "conv_w", "conv_b", "dt_bias", "a_log", "d_skip", "ssd_norm",
           "w_ssd_out", "q_lora_norm", "w_uq", "kv_lora_norm", "w_ukv", "q_norm", "k_norm", "w_mla_out", "w_o", "ln_ffn2",
           "ffn2_w13", "ffn2_w2"]
ARG_NAMES = (["x", "positions"] + WEIGHTS + ["loss_target"] + ["m_" + n for n in WEIGHTS] + ["v_" + n for n in WEIGHTS])


def kernel(x, positions, ln_ffn1, ffn1_w13, ffn1_w2, ln_mix, w_in, conv_w, conv_b, dt_bias, a_log, d_skip, ssd_norm, w_ssd_out, q_lora_norm, w_uq, kv_lora_norm, w_ukv, q_norm, k_norm, w_mla_out, w_o, ln_ffn2, ffn2_w13, ffn2_w2, loss_target, m_ln_ffn1, m_ffn1_w13, m_ffn1_w2, m_ln_mix, m_w_in, m_conv_w, m_conv_b, m_dt_bias, m_a_log, m_d_skip, m_ssd_norm, m_w_ssd_out, m_q_lora_norm, m_w_uq, m_kv_lora_norm, m_w_ukv, m_q_norm, m_k_norm, m_w_mla_out, m_w_o, m_ln_ffn2, m_ffn2_w13, m_ffn2_w2, v_ln_ffn1, v_ffn1_w13, v_ffn1_w2, v_ln_mix, v_w_in, v_conv_w, v_conv_b, v_dt_bias, v_a_log, v_d_skip, v_ssd_norm, v_w_ssd_out, v_q_lora_norm, v_w_uq, v_kv_lora_norm, v_w_ukv, v_q_norm, v_k_norm, v_w_mla_out, v_w_o, v_ln_ffn2, v_ffn2_w13, v_ffn2_w2):
    vals = (x, positions, ln_ffn1, ffn1_w13, ffn1_w2, ln_mix, w_in, conv_w, conv_b, dt_bias, a_log, d_skip, ssd_norm, w_ssd_out, q_lora_norm, w_uq, kv_lora_norm, w_ukv, q_norm, k_norm, w_mla_out, w_o, ln_ffn2, ffn2_w13, ffn2_w2, loss_target, m_ln_ffn1, m_ffn1_w13, m_ffn1_w2, m_ln_mix, m_w_in, m_conv_w, m_conv_b, m_dt_bias, m_a_log, m_d_skip, m_ssd_norm, m_w_ssd_out, m_q_lora_norm, m_w_uq, m_kv_lora_norm, m_w_ukv, m_q_norm, m_k_norm, m_w_mla_out, m_w_o, m_ln_ffn2, m_ffn2_w13, m_ffn2_w2, v_ln_ffn1, v_ffn1_w13, v_ffn1_w2, v_ln_mix, v_w_in, v_conv_w, v_conv_b, v_dt_bias, v_a_log, v_d_skip, v_ssd_norm, v_w_ssd_out, v_q_lora_norm, v_w_uq, v_kv_lora_norm, v_w_ukv, v_q_norm, v_k_norm, v_w_mla_out, v_w_o, v_ln_ffn2, v_ffn2_w13, v_ffn2_w2)
    out = _step(dict(zip(ARG_NAMES, vals)))
    order = ["loss", "grad_x"] + [k + n for k in ("grad_", "delta_", "new_m_", "new_v_") for n in WEIGHTS]
    return tuple(out[n] for n in order)
```

```python
import jax
import jax.numpy as jnp
from jax import lax
from jax.experimental import pallas as pl
from jax.experimental.pallas import tpu as pltpu

F32 = jnp.float32
BF16 = jnp.bfloat16

D_MODEL = 1024
D_FF = 2816
DEPTH = 2
SSD_DI = 2048
SSD_P = 64
SSD_H = 32
SSD_G = 4
SSD_HPG = 8
SSD_N = 128
SSD_L = 128
CONV_K = 4
CONV_DIM = 3072
MLA_H = 8
Q_LORA = 512
KV_LORA = 256
NOPE = 128
ROPE = 64
VDIM = 128
QK = 192
ROPE_THETA = 10000.0
EPS = 1e-6
IN_SPLIT = (SSD_DI, CONV_DIM, SSD_H, Q_LORA, KV_LORA, ROPE, 2 * D_MODEL)
D_IN = sum(IN_SPLIT)
N_DEV = 8
LANE = 128
PACK_COLS = 1024

PROJ_Z = 0
PROJ_GATES = PROJ_Z + SSD_DI
PROJ_XBC = PROJ_GATES + 2 * D_MODEL
PROJ_CQ = PROJ_XBC + CONV_DIM
PROJ_CKV = PROJ_CQ + Q_LORA
PROJ_LAST = PROJ_CKV + KV_LORA
D_IN_PAD = PROJ_LAST + LANE

ADAM_LR = 0.001
ADAM_B1 = 0.9
ADAM_B2 = 0.999
ADAM_EPS = 1e-08
ADAM_WD = 0.01
ADAM_STEP = 10

VMEM_LIMIT = 48 * 1024 * 1024
ROW_IO_BUDGET = 8 * 1024 * 1024
NEG = -1e30

MESH_AXES = ("x", "y", "c")


def _cparams(*sem):
    return pltpu.CompilerParams(dimension_semantics=sem, vmem_limit_bytes=VMEM_LIMIT)


def _pick_tile(n, target, align):
    if n <= target:
        return n
    best = None
    for t in range(align, target + 1, align):
        if n % t == 0:
            best = t
    assert best is not None, (n, target, align)
    return best


def _acc_store(ref, val, first):
    @pl.when(first)
    def _():
        ref[...] = val

    @pl.when(jnp.logical_not(first))
    def _():
        ref[...] += val


def _win(arr, start, width):
    assert start % width == 0, (start, width)
    return (arr, start, width)


def _operand(entry):
    if isinstance(entry, tuple):
        arr, start, width = entry
        return arr, width, start // width
    return entry, entry.shape[1], 0


def _row_tile(rows, bytes_per_row):
    if rows <= 16:
        return rows
    t = 1024
    while t > 16 and (t * bytes_per_row > ROW_IO_BUDGET or rows % t):
        t //= 2
    assert rows % t == 0, (rows, t)
    return t


def _rowwise_call(fn, tiled, params, outs, accs, name):
    ops = [_operand(e) for e in tiled]
    rows = ops[0][0].shape[0]
    per_row = sum(w * a.dtype.itemsize for a, w, _ in ops) + sum(c * jnp.dtype(d).itemsize for c, d in outs)
    tile = _row_tile(rows, per_row)
    n_in = len(tiled) + len(params)
    n_o = len(outs)

    def body(*refs):
        vals = [r[...] for r in refs[:n_in]]
        t_out, a_out = fn(*vals)
        for r, v in zip(refs[n_in:n_in + n_o], t_out):
            r[...] = v.astype(r.dtype)
        first = pl.program_id(0) == 0
        for r, v in zip(refs[n_in + n_o:], a_out):
            _acc_store(r, v.astype(F32), first)

    def tiled_spec(width, blk):
        return pl.BlockSpec((tile, width), lambda i: (i, blk))

    in_specs = [tiled_spec(w, blk) for _, w, blk in ops]
    in_specs += [pl.BlockSpec(p.shape, lambda i: (0, 0)) for p in params]
    out_specs = [tiled_spec(c, 0) for c, _ in outs]
    out_specs += [pl.BlockSpec(s, lambda i: (0, 0)) for s in accs]
    out_shape = [jax.ShapeDtypeStruct((rows, c), d) for c, d in outs]
    out_shape += [jax.ShapeDtypeStruct(s, F32) for s in accs]
    return pl.pallas_call(
        body, grid=(rows // tile,), in_specs=in_specs, out_specs=out_specs, out_shape=out_shape,
        compiler_params=_cparams("arbitrary"), name=name,
    )(*[a for a, _, _ in ops], *params)


def _to_f32(vals):
    return [v.astype(F32) for v in vals]


def _row_fwd(f, tiled, params, out_dtypes, name):
    ops = [_operand(e) for e in tiled]
    rows = ops[0][0].shape[0]
    shapes = jax.eval_shape(f, *[jax.ShapeDtypeStruct((rows, w), F32) for _, w, _ in ops],
                            *[jax.ShapeDtypeStruct(p.shape, F32) for p in params])
    outs = [(s.shape[1], d) for s, d in zip(shapes, out_dtypes)]
    return _rowwise_call(lambda *v: (f(*_to_f32(v)), ()), tiled, params, outs, [], name)


def _row_bwd(f, tiled, params, gs, d_dtypes, name, bwd=None, add=None):
    n_t, n_g = len(tiled), len(gs)
    adds = sorted((add or {}).items())
    n_a = len(adds)

    def fn(*vals):
        vals = _to_f32(vals)
        prim = vals[:n_t] + vals[n_t + n_g + n_a:]
        g = tuple(vals[n_t:n_t + n_g])
        if bwd is not None:
            d_t, d_p = bwd(*prim, *g)
        else:
            _, vjp = jax.vjp(f, *prim)
            cts = vjp(g)
            d_t, d_p = cts[:n_t], cts[n_t:]
        d_t = list(d_t)
        for (idx, _), extra in zip(adds, vals[n_t + n_g:n_t + n_g + n_a]):
            d_t[idx] = d_t[idx] + extra
        return tuple(d_t), tuple(d_p)

    outs = [(_operand(e)[1], d) for e, d in zip(tiled, d_dtypes)]
    accs = [p.shape for p in params]
    res = _rowwise_call(fn, list(tiled) + list(gs) + [a for _, a in adds], params, outs, accs, name)
    return res[:n_t], res[n_t:]


def _f_rmsnorm(x, g):
    return (x * lax.rsqrt(jnp.mean(x * x, axis=-1, keepdims=True) + EPS) * g,)


def _f_gated_norm(ys, xs, z, dsk, g):
    t = (ys + xs * dsk) * (z * jax.nn.sigmoid(z))
    return (t * lax.rsqrt(jnp.mean(t * t, axis=-1, keepdims=True) + EPS) * g,)


def _f_merge(gates, ys, ym):
    s = jax.nn.sigmoid(gates)
    return (s[:, :D_MODEL] * ys + s[:, D_MODEL:] * ym,)


def _b_merge(gates, ys, ym, d):
    s = jax.nn.sigmoid(gates)
    s1, s2 = s[:, :D_MODEL], s[:, D_MODEL:]
    d_gates = jnp.concatenate([d * ys * s1 * (1.0 - s1), d * ym * s2 * (1.0 - s2)], axis=1)
    return (d_gates, d * s1, d * s2), ()


def _loss_and_grad(y, target):
    def fn(yv, tv):
        d = yv - tv
        return (d * (1.0 / D_MODEL),), (jnp.sum(d * d, axis=0, keepdims=True) * (0.5 / D_MODEL),)

    dy, part = _rowwise_call(fn, [y, target], [], [(D_MODEL, F32)], [(1, D_MODEL)], "loss")
    return jnp.sum(part), dy


def _adam(w, g, m, v):
    def fn(wv, gv, mv, vv):
        m2 = ADAM_B1 * mv + (1.0 - ADAM_B1) * gv
        v2 = ADAM_B2 * vv + (1.0 - ADAM_B2) * (gv * gv)
        m_hat = m2 / (1.0 - ADAM_B1 ** ADAM_STEP)
        v_hat = v2 / (1.0 - ADAM_B2 ** ADAM_STEP)
        delta = -ADAM_LR * (m_hat / (jnp.sqrt(v_hat) + ADAM_EPS) + ADAM_WD * wv)
        return (delta, m2, v2), ()

    c = w.shape[1]
    return _rowwise_call(fn, [w, g, m, v], [], [(c, F32)] * 3, [], "adamw")


def _sum_blocks(blocks):
    _, rows, c = blocks.shape
    tile = _row_tile(rows, N_DEV * c * blocks.dtype.itemsize + c * 4)

    def body(b_ref, o_ref):
        acc = b_ref[0].astype(F32)
        for i in range(1, N_DEV):
            acc = acc + b_ref[i].astype(F32)
        o_ref[...] = acc

    return pl.pallas_call(
        body, grid=(rows // tile,), in_specs=[pl.BlockSpec((N_DEV, tile, c), lambda i: (0, i, 0))],
        out_specs=pl.BlockSpec((tile, c), lambda i: (i, 0)), out_shape=jax.ShapeDtypeStruct((rows, c), F32),
        compiler_params=_cparams("arbitrary"), name="sum_blocks",
    )(blocks)


def _mm(a, b, ta=False, tb=False, out_dtype=F32, alpha=1.0, res=None, b_rows=None, norm_bwd=None):
    r_dim, p_dim = a.shape if ta else a.shape[::-1]
    b_row0, b_nrows = (0, b.shape[0]) if b_rows is None else b_rows
    r2, q_dim = (b.shape[1], b_nrows) if tb else (b_nrows, b.shape[1])
    assert r_dim == r2, (a.shape, b.shape, ta, tb)
    tp = _pick_tile(p_dim, 512, LANE)
    if tp < 512 < p_dim:
        tp = _pick_tile(p_dim, 1536, LANE)
    tq = _pick_tile(q_dim, 1536, LANE)
    tr = _pick_tile(r_dim, 1536, LANE)
    nr = r_dim // tr
    dims = (((0 if ta else 1,), (1 if tb else 0,)), ((), ()))
    has_res = res is not None
    n_nb = 0 if norm_bwd is None else 3
    assert norm_bwd is None or tq == q_dim

    def body(*refs):
        a_ref, b_ref = refs[:2]
        res_ref = refs[2] if has_res else None
        n_in = 2 + has_res + n_nb
        o_ref = refs[n_in]

        def finish(val):
            if alpha != 1.0:
                val = val * alpha
            if has_res:
                val = val + res_ref[...].astype(F32)
            if norm_bwd is not None:
                x_ref, g_ref, add_ref = refs[2 + has_res:n_in]
                x = x_ref[...]
                r = lax.rsqrt(jnp.mean(x * x, axis=-1, keepdims=True) + EPS)
                gy = val * g_ref[...]
                dot = jnp.sum(gy * x, axis=-1, keepdims=True)
                _acc_store(refs[n_in + 1], jnp.sum(val * x * r, axis=0, keepdims=True), pl.program_id(1) == 0)
                val = gy * r - x * (dot * (r * r * r) * (1.0 / q_dim)) + add_ref[...]
            o_ref[...] = val.astype(o_ref.dtype)

        part = lax.dot_general(a_ref[...].astype(BF16), b_ref[...].astype(BF16), dims, preferred_element_type=F32)
        if nr == 1:
            finish(part)
        else:
            acc_ref = refs[-1]
            k = pl.program_id(2)
            _acc_store(acc_ref, part, k == 0)

            @pl.when(k == nr - 1)
            def _():
                finish(acc_ref[...])

    a_spec = pl.BlockSpec((tr, tp), lambda j, i, k: (k, i)) if ta else pl.BlockSpec((tp, tr), lambda j, i, k: (i, k))
    assert b_row0 % (tq if tb else tr) == 0
    b0 = b_row0 // (tq if tb else tr)
    b_spec = pl.BlockSpec((tq, tr), lambda j, i, k: (j + b0, k)) if tb else pl.BlockSpec((tr, tq), lambda j, i, k: (k + b0, j))
    o_spec = pl.BlockSpec((tp, tq), lambda j, i, k: (i, j))
    row_spec = pl.BlockSpec((1, tq), lambda j, i, k: (0, 0))
    in_specs = [a_spec, b_spec] + ([o_spec] if has_res else []) + ([o_spec, row_spec, o_spec] if n_nb else [])
    out = pl.pallas_call(
        body, grid=(q_dim // tq, p_dim // tp, nr), in_specs=in_specs,
        out_specs=[o_spec] + ([row_spec] if n_nb else []),
        out_shape=[jax.ShapeDtypeStruct((p_dim, q_dim), out_dtype)] + ([jax.ShapeDtypeStruct((1, q_dim), F32)] if n_nb else []),
        scratch_shapes=[pltpu.VMEM((tp, tq), F32)] if nr > 1 else [],
        compiler_params=_cparams("arbitrary", "arbitrary", "arbitrary"),
        name=f"mm_{'t' if ta else 'n'}{'t' if tb else 'n'}_{p_dim}x{r_dim}x{q_dim}" + ("_norm_bwd" if n_nb else ""),
    )(*([a, b] + ([res] if has_res else []) + (list(norm_bwd) if n_nb else [])))
    return out if n_nb else out[0]


MERGE_TP = 512


def _merge_out_call(proj, y_ssd, y_mla, w_o, h):
    s = h.shape[0]
    tp = min(MERGE_TP, s)

    def body(g_ref, ys_ref, ym_ref, w_ref, h_ref, o_ref, mg_ref):
        mg = _f_merge(g_ref[...], ys_ref[...], ym_ref[...])[0].astype(BF16)
        mg_ref[...] = mg
        o_ref[...] = h_ref[...] + jnp.dot(mg, w_ref[...], preferred_element_type=F32)

    rows = pl.BlockSpec((tp, D_MODEL), lambda i: (i, 0))
    return pl.pallas_call(
        body, grid=(s // tp,),
        in_specs=[pl.BlockSpec((tp, 2 * D_MODEL), lambda i: (i, PROJ_GATES // (2 * D_MODEL))), rows, rows,
                  pl.BlockSpec((D_MODEL, D_MODEL), lambda i: (0, 0)), rows],
        out_specs=[rows, rows],
        out_shape=[jax.ShapeDtypeStruct((s, D_MODEL), F32), jax.ShapeDtypeStruct((s, D_MODEL), BF16)],
        compiler_params=_cparams("arbitrary"), name="merge_out",
    )(proj, y_ssd, y_mla, w_o, h)


def _merge_out_bwd_call(dh, w_o, proj, y_ssd, y_mla):
    s = dh.shape[0]
    tp = min(MERGE_TP, s)

    def body(dh_ref, w_ref, g_ref, ys_ref, ym_ref, dg_ref, dys_ref, dym_ref):
        d_mg = _nt(dh_ref[...].astype(BF16), w_ref[...])
        (d_g, d_ys, d_ym), _ = _b_merge(g_ref[...], ys_ref[...], ym_ref[...], d_mg)
        dg_ref[...] = d_g.astype(BF16)
        dys_ref[...] = d_ys.astype(BF16)
        dym_ref[...] = d_ym.astype(BF16)

    rows = pl.BlockSpec((tp, D_MODEL), lambda i: (i, 0))
    wide = pl.BlockSpec((tp, 2 * D_MODEL), lambda i: (i, 0))
    return pl.pallas_call(
        body, grid=(s // tp,),
        in_specs=[rows, pl.BlockSpec((D_MODEL, D_MODEL), lambda i: (0, 0)),
                  pl.BlockSpec((tp, 2 * D_MODEL), lambda i: (i, PROJ_GATES // (2 * D_MODEL))), rows, rows],
        out_specs=[wide, rows, rows],
        out_shape=[jax.ShapeDtypeStruct((s, 2 * D_MODEL), BF16), jax.ShapeDtypeStruct((s, D_MODEL), BF16),
                   jax.ShapeDtypeStruct((s, D_MODEL), BF16)],
        compiler_params=_cparams("arbitrary"), name="merge_out_bwd",
    )(dh, w_o, proj, y_ssd, y_mla)


def _attn_out_bwd_call(d_y, w_out, o_rows):
    s = d_y.shape[0]
    tp = min(MERGE_TP, s)
    wide = MLA_H * VDIM

    def body(dy_ref, w_ref, o_ref, do_ref, delta_ref):
        d_o = _nt(dy_ref[...], w_ref[...]).astype(BF16)
        do_ref[...] = d_o
        col = lax.broadcasted_iota(jnp.int32, (wide, MLA_H), 0)
        head = lax.broadcasted_iota(jnp.int32, (wide, MLA_H), 1)
        per_head = _ones_where(lax.shift_right_logical(col, VDIM.bit_length() - 1) == head)
        delta_ref[...] = _dot_sel_r(d_o.astype(F32) * o_ref[...].astype(F32), per_head)

    rows = lambda c: pl.BlockSpec((tp, c), lambda i: (i, 0))
    return pl.pallas_call(
        body, grid=(s // tp,),
        in_specs=[rows(D_MODEL), pl.BlockSpec((wide, D_MODEL), lambda i: (0, 0)), rows(wide)],
        out_specs=[rows(wide), rows(MLA_H)],
        out_shape=[jax.ShapeDtypeStruct((s, wide), BF16), jax.ShapeDtypeStruct((s, MLA_H), F32)],
        compiler_params=_cparams("arbitrary"), name="attn_out_bwd",
    )(d_y, w_out, o_rows)


FFN_TP = 512
FFN_TQ = 1408


def _ffn_up_call(n, w13_t):
    s, d = n.shape
    tp = min(FFN_TP, s)
    up0 = D_FF // FFN_TQ

    def body(n_ref, wg_ref, wu_ref, act_ref, gate_ref, up_ref):
        a = n_ref[...]
        g = _nt(a, wg_ref[...])
        u = _nt(a, wu_ref[...])
        act_ref[...] = (g * jax.nn.sigmoid(g) * u).astype(BF16)
        gate_ref[...] = g.astype(BF16)
        up_ref[...] = u.astype(BF16)

    o_spec = pl.BlockSpec((tp, FFN_TQ), lambda j, i: (i, j))
    return pl.pallas_call(
        body, grid=(D_FF // FFN_TQ, s // tp),
        in_specs=[pl.BlockSpec((tp, d), lambda j, i: (i, 0)), pl.BlockSpec((FFN_TQ, d), lambda j, i: (j, 0)),
                  pl.BlockSpec((FFN_TQ, d), lambda j, i: (j + up0, 0))],
        out_specs=[o_spec] * 3, out_shape=[jax.ShapeDtypeStruct((s, D_FF), BF16)] * 3,
        compiler_params=_cparams("arbitrary", "arbitrary"), name="ffn_up_swiglu",
    )(n, w13_t, w13_t)


def _ffn_down_bwd_call(dh, w2, gate, up):
    s, d = dh.shape
    tp = min(FFN_TP, s)

    def body(dh_ref, w2_ref, gate_ref, up_ref, dg_ref, du_ref):
        d_act = 0.5 * _nt(dh_ref[...].astype(BF16), w2_ref[...])
        g, u = gate_ref[...].astype(F32), up_ref[...].astype(F32)
        sg = jax.nn.sigmoid(g)
        dg_ref[...] = (d_act * u * sg * (1.0 + g * (1.0 - sg))).astype(BF16)
        du_ref[...] = (d_act * g * sg).astype(BF16)

    o_spec = pl.BlockSpec((tp, FFN_TQ), lambda j, i: (i, j))
    return pl.pallas_call(
        body, grid=(D_FF // FFN_TQ, s // tp),
        in_specs=[pl.BlockSpec((tp, d), lambda j, i: (i, 0)), pl.BlockSpec((FFN_TQ, d), lambda j, i: (j, 0)), o_spec, o_spec],
        out_specs=[o_spec] * 2, out_shape=[jax.ShapeDtypeStruct((s, D_FF), BF16)] * 2,
        compiler_params=_cparams("arbitrary", "arbitrary"), name="ffn_down_bwd_swiglu",
    )(dh, w2, gate, up)


ATTN_SCALE = QK ** -0.5
LOG2E = 1.4426950408889634
ATTN_C = ATTN_SCALE * LOG2E


ATTN_HEADS = 2
ATTN_HEADS_FWD = 4


def _attn_tile(s):
    return min(512, s)


def _causal_keep(t, keys_on_rows=False):
    row = lax.broadcasted_iota(jnp.int32, (t, t), 0)
    col = lax.broadcasted_iota(jnp.int32, (t, t), 1)
    return row <= col if keys_on_rows else col <= row


def _nt(a, b):
    return lax.dot_general(a, b, (((1,), (1,)), ((), ())), preferred_element_type=F32)


def _rider_phases(rider, src_ref, out_ref, sems, first, last):
    @pl.when(first)
    def _():
        _peer_copies(src_ref, out_ref, sems, rider["gather"], "start")

    def finish():
        @pl.when(last)
        def _():
            _peer_copies(src_ref, out_ref, sems, rider["gather"], "finish")

    return finish


def _attn_fwd_call(q, k, v, rider=None):
    nh, s, _ = q.shape
    t = _attn_tile(s)
    nb = s // t
    hp = ATTN_HEADS_FWD
    n_r = 0 if rider is None else 1

    def body(*refs):
        q_ref, k_ref, v_ref = refs[:3]
        o_ref, lse_ref = refs[3 + n_r:5 + n_r]
        qi = pl.program_id(1)
        finish = None
        if rider is not None:
            h = pl.program_id(0)
            finish = _rider_phases(rider, refs[3:4], refs[5 + n_r], refs[6 + n_r:],
                                   jnp.logical_and(h == 0, qi == 0), jnp.logical_and(h == nh // hp - 1, qi == nb - 1))
        qs = [q_ref[i] for i in range(hp)]

        def block(kb, carries, diagonal, width=1):
            start = pl.multiple_of(kb * t, t)
            out = []
            for i, (m_prev, l_prev, acc) in enumerate(carries):
                sc = _nt(qs[i], k_ref[i, pl.ds(start, width * t), :])
                if diagonal:
                    sc = jnp.where(_causal_keep(t), sc, NEG)
                m_new = jnp.maximum(m_prev, jnp.max(sc, axis=-1, keepdims=True))
                p = jnp.exp2(sc * ATTN_C - m_new * ATTN_C)
                alpha = jnp.exp2((m_prev - m_new) * ATTN_C)
                l_new = alpha * l_prev + jnp.sum(p, axis=-1, keepdims=True)
                pv = jnp.dot(p.astype(BF16), v_ref[i, pl.ds(start, width * t), :], preferred_element_type=F32)
                out.append((m_new, l_new, alpha * acc + pv))
            return tuple(out)

        init = tuple((jnp.full((t, 1), NEG, F32), jnp.zeros((t, 1), F32), jnp.zeros((t, VDIM), F32)) for _ in range(hp))
        carries = lax.fori_loop(0, qi // 2, lambda j, c: block(2 * j, c, False, width=2), init)
        carries = lax.cond(qi % 2 == 1, lambda c: block(qi - 1, c, False), lambda c: c, carries)
        for i, (m, l, acc) in enumerate(block(qi, carries, True)):
            o_ref[i] = (acc / l).astype(o_ref.dtype)
            lse_ref[i] = m * ATTN_SCALE + jnp.log(l)
        if finish is not None:
            finish()

    qmap = lambda h, i: (h, i, 0)
    whole = lambda h, i: (h, 0, 0)
    return pl.pallas_call(
        body, grid=(nh // hp, nb),
        in_specs=[pl.BlockSpec((hp, t, QK), qmap), pl.BlockSpec((hp, s, QK), whole, pipeline_mode=pl.Buffered(buffer_count=1)),
                  pl.BlockSpec((hp, s, VDIM), whole, pipeline_mode=pl.Buffered(buffer_count=1))] + [HBM_SPEC] * n_r,
        out_specs=[pl.BlockSpec((hp, t, VDIM), qmap), pl.BlockSpec((hp, t, 1), qmap)] + [HBM_SPEC] * n_r,
        out_shape=[jax.ShapeDtypeStruct((nh, s, VDIM), BF16), jax.ShapeDtypeStruct((nh, s, 1), F32)] + ([rider["out"]] if n_r else []),
        scratch_shapes=_comm_scratch() if n_r else [],
        compiler_params=_cparams("arbitrary", "arbitrary"), name="attn_fwd_gather" if n_r else "attn_fwd",
    )(*([q, k, v] + (rider["srcs"] if n_r else [])))


def _attn_bwd_call(q, k, v, do, lse_t, delta_t, rider=None):
    nh, s, _ = q.shape
    t = _attn_tile(s)
    nb = s // t
    hp = ATTN_HEADS
    n_src = 0 if rider is None else len(rider["srcs"])
    n_r = 0 if rider is None else 1

    def body(*refs):
        q_ref, k_ref, v_ref, do_ref, lse_ref, delta_ref = refs[:6]
        dq_ref, dk_ref, dv_ref = refs[6 + n_src:9 + n_src]
        dk_sc, dv_sc = refs[9 + n_src + n_r:11 + n_src + n_r]
        kj = pl.program_id(1)
        finish = None
        if rider is not None:
            h = pl.program_id(0)
            finish = _rider_phases(rider, refs[6:6 + n_src], refs[9 + n_src], refs[11 + n_src + n_r:],
                                   jnp.logical_and(h == 0, kj == 0), jnp.logical_and(h == nh // hp - 1, kj == nb - 1))

        @pl.when(kj == 0)
        def _():
            dq_ref[...] = jnp.zeros_like(dq_ref)

        dk_sc[...] = jnp.zeros_like(dk_sc)
        dv_sc[...] = jnp.zeros_like(dv_sc)
        kblks = [k_ref[i] for i in range(hp)]
        vblks = [v_ref[i] for i in range(hp)]

        def block(qb, diagonal):
            start = pl.multiple_of(qb * t, t)
            for i in range(hp):
                qblk = q_ref[i, pl.ds(start, t), :]
                doblk = do_ref[i, pl.ds(start, t), :]
                sc = _nt(kblks[i], qblk)
                if diagonal:
                    sc = jnp.where(_causal_keep(t, keys_on_rows=True), sc, NEG)
                p = jnp.exp2(sc * ATTN_C - lse_ref[i, :, pl.ds(start, t)] * LOG2E)
                dv_sc[i] += jnp.dot(p.astype(BF16), doblk, preferred_element_type=F32)
                dp = _nt(vblks[i], doblk)
                ds = (p * (dp - delta_ref[i, :, pl.ds(start, t)])).astype(BF16)
                dk_sc[i] += jnp.dot(ds, qblk, preferred_element_type=F32)
                dq_ref[i, pl.ds(start, t), :] += lax.dot_general(ds, kblks[i], (((0,), (0,)), ((), ())), preferred_element_type=F32)

        block(kj, True)

        def rest(qb, carry):
            block(qb, False)
            return carry

        lax.fori_loop(kj + 1, nb, rest, 0)
        dk_ref[...] = (dk_sc[...] * ATTN_SCALE).astype(dk_ref.dtype)
        dv_ref[...] = dv_sc[...].astype(dv_ref.dtype)

        @pl.when(kj == nb - 1)
        def _():
            dq_ref[...] = dq_ref[...] * ATTN_SCALE

        if finish is not None:
            finish()

    kmap = lambda h, j: (h, j, 0)
    whole = lambda h, j: (h, 0, 0)
    once = pl.Buffered(buffer_count=1)
    return pl.pallas_call(
        body, grid=(nh // hp, nb),
        in_specs=[pl.BlockSpec((hp, s, QK), whole, pipeline_mode=once), pl.BlockSpec((hp, t, QK), kmap), pl.BlockSpec((hp, t, VDIM), kmap),
                  pl.BlockSpec((hp, s, VDIM), whole, pipeline_mode=once), pl.BlockSpec((hp, 1, s), whole, pipeline_mode=once),
                  pl.BlockSpec((hp, 1, s), whole, pipeline_mode=once)] + [HBM_SPEC] * n_src,
        out_specs=[pl.BlockSpec((hp, s, QK), whole, pipeline_mode=once), pl.BlockSpec((hp, t, QK), kmap),
                   pl.BlockSpec((hp, t, VDIM), kmap)] + [HBM_SPEC] * n_r,
        out_shape=[jax.ShapeDtypeStruct((nh, s, QK), F32), jax.ShapeDtypeStruct((nh, s, QK), F32),
                   jax.ShapeDtypeStruct((nh, s, VDIM), F32)] + ([rider["out"]] if n_r else []),
        scratch_shapes=[pltpu.VMEM((hp, t, QK), F32), pltpu.VMEM((hp, t, VDIM), F32)] + (_comm_scratch() if n_r else []),
        compiler_params=_cparams("arbitrary", "arbitrary"), name="attn_bwd_exchange" if n_r else "attn_bwd",
    )(*([q, k, v, do, lse_t, delta_t] + (rider["srcs"] if n_r else [])))


HEAD_COLS = NOPE + VDIM
HEADS_TILE = 256


def _swap_rope_halves(t, lane):
    half = ROPE // 2
    return jnp.where(lane < half, pltpu.roll(t, LANE - half, 1), pltpu.roll(t, half, 1))


def _head_fwd(n, p, gain, cs, sn, lane):
    r = lax.rsqrt((jnp.sum(n * n, axis=-1, keepdims=True) + jnp.sum(p * p, axis=-1, keepdims=True)) * (1.0 / QK) + EPS)
    yp = p * r * gain[:, NOPE:]
    return n * r * gain[:, :NOPE], yp * cs + _swap_rope_halves(yp, lane) * sn


def _head_bwd(n, p, gain, cs, sn, lane, dzn, dzp):
    r = lax.rsqrt((jnp.sum(n * n, axis=-1, keepdims=True) + jnp.sum(p * p, axis=-1, keepdims=True)) * (1.0 / QK) + EPS)
    dyp = dzp * cs + _swap_rope_halves(dzp * sn, lane)
    gyn, gyp = dzn * gain[:, :NOPE], dyp * gain[:, NOPE:]
    dot = jnp.sum(gyn * n, axis=-1, keepdims=True) + jnp.sum(gyp * p, axis=-1, keepdims=True)
    coef = dot * (r * r * r) * (1.0 / QK)
    d_gn = jnp.sum(dzn * n * r, axis=0, keepdims=True)
    d_gp = jnp.sum(dyp * p * r, axis=0, keepdims=True)
    return gyn * r - n * coef, gyp * r - p * coef, d_gn, d_gp


def _heads_fwd_call(q, kv, proj, cs, sn, q_gain, k_gain):
    s = q.shape[0]
    t = min(HEADS_TILE, s)

    def body(q_ref, kv_ref, last_ref, cs_ref, sn_ref, qg_ref, kg_ref, qh_ref, kh_ref, vh_ref):
        lane = lax.broadcasted_iota(jnp.int32, (t, LANE), 1)
        cs_, sn_ = cs_ref[...], sn_ref[...]
        kp = jnp.where(lane < ROPE, last_ref[...], 0.0)
        for h in range(MLA_H):
            c0 = h * HEAD_COLS
            zn, zp = _head_fwd(q_ref[:, c0:c0 + NOPE], q_ref[:, c0 + NOPE:c0 + HEAD_COLS], qg_ref[...], cs_, sn_, lane)
            qh_ref[h, :, :NOPE] = zn.astype(BF16)
            qh_ref[h, :, NOPE:] = zp[:, :ROPE].astype(BF16)
            zn, zp = _head_fwd(kv_ref[:, c0:c0 + NOPE], kp, kg_ref[...], cs_, sn_, lane)
            kh_ref[h, :, :NOPE] = zn.astype(BF16)
            kh_ref[h, :, NOPE:] = zp[:, :ROPE].astype(BF16)
            vh_ref[h] = kv_ref[:, c0 + NOPE:c0 + HEAD_COLS].astype(BF16)

    rows = lambda i: (i, 0)
    whole = lambda i: (0, 0)
    heads = lambda i: (0, i, 0)
    wide = MLA_H * HEAD_COLS
    return pl.pallas_call(
        body, grid=(s // t,),
        in_specs=[pl.BlockSpec((t, wide), rows), pl.BlockSpec((t, wide), rows),
                  pl.BlockSpec((t, LANE), lambda i: (i, PROJ_LAST // LANE)),
                  pl.BlockSpec((t, LANE), rows), pl.BlockSpec((t, LANE), rows),
                  pl.BlockSpec((1, HEAD_COLS), whole), pl.BlockSpec((1, HEAD_COLS), whole)],
        out_specs=[pl.BlockSpec((MLA_H, t, QK), heads), pl.BlockSpec((MLA_H, t, QK), heads), pl.BlockSpec((MLA_H, t, VDIM), heads)],
        out_shape=[jax.ShapeDtypeStruct((MLA_H, s, QK), BF16), jax.ShapeDtypeStruct((MLA_H, s, QK), BF16),
                   jax.ShapeDtypeStruct((MLA_H, s, VDIM), BF16)],
        compiler_params=_cparams("arbitrary"), name="mla_heads_fwd",
    )(q, kv, proj, cs, sn, q_gain, k_gain)


def _heads_bwd_call(q, kv, proj, cs, sn, q_gain, k_gain, dqh, dkh, dvh):
    s = q.shape[0]
    t = min(HEADS_TILE, s)

    def body(q_ref, kv_ref, last_ref, cs_ref, sn_ref, qg_ref, kg_ref, dqh_ref, dkh_ref, dvh_ref,
             dq_ref, dkv_ref, dkr_ref, dqg_ref, dkg_ref):
        lane = lax.broadcasted_iota(jnp.int32, (t, LANE), 1)
        cs_, sn_ = cs_ref[...], sn_ref[...]
        kp = jnp.where(lane < ROPE, last_ref[...], 0.0)
        no_lanes = jnp.zeros((t, LANE - ROPE), F32)
        d_kp = jnp.zeros((t, LANE), F32)
        d_qg = [jnp.zeros((1, NOPE), F32), jnp.zeros((1, LANE), F32)]
        d_kg = [jnp.zeros((1, NOPE), F32), jnp.zeros((1, LANE), F32)]
        for h in range(MLA_H):
            c0 = h * HEAD_COLS
            dz = dqh_ref[h]
            dzp = jnp.concatenate([dz[:, NOPE:], no_lanes], axis=1)
            d_n, d_p, g_n, g_p = _head_bwd(q_ref[:, c0:c0 + NOPE], q_ref[:, c0 + NOPE:c0 + HEAD_COLS], qg_ref[...],
                                           cs_, sn_, lane, dz[:, :NOPE], dzp)
            dq_ref[:, c0:c0 + NOPE] = d_n.astype(dq_ref.dtype)
            dq_ref[:, c0 + NOPE:c0 + HEAD_COLS] = d_p.astype(dq_ref.dtype)
            d_qg = [d_qg[0] + g_n, d_qg[1] + g_p]
            dz = dkh_ref[h]
            dzp = jnp.concatenate([dz[:, NOPE:], no_lanes], axis=1)
            d_n, d_p, g_n, g_p = _head_bwd(kv_ref[:, c0:c0 + NOPE], kp, kg_ref[...], cs_, sn_, lane, dz[:, :NOPE], dzp)
            dkv_ref[:, c0:c0 + NOPE] = d_n.astype(dkv_ref.dtype)
            dkv_ref[:, c0 + NOPE:c0 + HEAD_COLS] = dvh_ref[h].astype(dkv_ref.dtype)
            d_kp = d_kp + d_p
            d_kg = [d_kg[0] + g_n, d_kg[1] + g_p]
        dkr_ref[...] = d_kp
        first = pl.program_id(0) == 0
        _acc_store(dqg_ref.at[:, pl.ds(0, NOPE)], d_qg[0], first)
        _acc_store(dqg_ref.at[:, pl.ds(NOPE, LANE)], d_qg[1], first)
        _acc_store(dkg_ref.at[:, pl.ds(0, NOPE)], d_kg[0], first)
        _acc_store(dkg_ref.at[:, pl.ds(NOPE, LANE)], d_kg[1], first)

    rows = lambda i: (i, 0)
    whole = lambda i: (0, 0)
    heads = lambda i: (0, i, 0)
    wide = MLA_H * HEAD_COLS
    return pl.pallas_call(
        body, grid=(s // t,),
        in_specs=[pl.BlockSpec((t, wide), rows), pl.BlockSpec((t, wide), rows),
                  pl.BlockSpec((t, LANE), lambda i: (i, PROJ_LAST // LANE)),
                  pl.BlockSpec((t, LANE), rows), pl.BlockSpec((t, LANE), rows),
                  pl.BlockSpec((1, HEAD_COLS), whole), pl.BlockSpec((1, HEAD_COLS), whole),
                  pl.BlockSpec((MLA_H, t, QK), heads), pl.BlockSpec((MLA_H, t, QK), heads), pl.BlockSpec((MLA_H, t, VDIM), heads)],
        out_specs=[pl.BlockSpec((t, wide), rows), pl.BlockSpec((t, wide), rows), pl.BlockSpec((t, LANE), rows),
                   pl.BlockSpec((1, HEAD_COLS), whole), pl.BlockSpec((1, HEAD_COLS), whole)],
        out_shape=[jax.ShapeDtypeStruct((s, wide), BF16), jax.ShapeDtypeStruct((s, wide), BF16), jax.ShapeDtypeStruct((s, LANE), F32),
                   jax.ShapeDtypeStruct((1, HEAD_COLS), F32), jax.ShapeDtypeStruct((1, HEAD_COLS), F32)],
        compiler_params=_cparams("arbitrary"), name="mla_heads_bwd",
    )(q, kv, proj, cs, sn, q_gain, k_gain, dqh, dkh, dvh)


CONV_TC = 512
HALO = 8


def _conv_tiles(s):
    return min(512, s)


def _conv_fwd_call(x, col0, w, b):
    s = x.shape[0]
    ts = _conv_tiles(s)
    hb = ts // HALO
    c0 = col0 // CONV_TC
    assert col0 % CONV_TC == 0

    def body(x_ref, prev_ref, w_ref, b_ref, y_ref, buf):
        si = pl.program_id(1)
        buf[0:HALO, :] = jnp.where(si > 0, prev_ref[...], 0.0)
        buf[HALO:, :] = x_ref[...]
        acc = jnp.broadcast_to(b_ref[...], (ts, CONV_TC))
        for k in range(CONV_K):
            acc = acc + w_ref[k:k + 1, :] * buf[pl.ds(HALO - (CONV_K - 1) + k, ts), :]
        y_ref[...] = acc * jax.nn.sigmoid(acc)

    return pl.pallas_call(
        body, grid=(CONV_DIM // CONV_TC, s // ts),
        in_specs=[pl.BlockSpec((ts, CONV_TC), lambda ci, si: (si, ci + c0)),
                  pl.BlockSpec((HALO, CONV_TC), lambda ci, si: (jnp.maximum(si * hb - 1, 0), ci + c0)),
                  pl.BlockSpec((CONV_K, CONV_TC), lambda ci, si: (0, ci)),
                  pl.BlockSpec((1, CONV_TC), lambda ci, si: (0, ci))],
        out_specs=pl.BlockSpec((ts, CONV_TC), lambda ci, si: (si, ci)),
        out_shape=jax.ShapeDtypeStruct((s, CONV_DIM), F32),
        scratch_shapes=[pltpu.VMEM((ts + HALO, CONV_TC), F32)],
        compiler_params=_cparams("arbitrary", "arbitrary"), name="conv_fwd",
    )(x, x, w, b)


def _conv_bwd_call(x, col0, w, b, dy):
    s = x.shape[0]
    ts = _conv_tiles(s)
    hb = ts // HALO
    ns = s // ts
    last_halo = s // HALO - 1
    c0 = col0 // CONV_TC

    def body(x_ref, prev_ref, next_ref, dy_ref, dyn_ref, w_ref, b_ref, dx_ref, dw_ref, db_ref, xbuf, dbuf):
        si = pl.program_id(1)
        xbuf[0:HALO, :] = jnp.where(si > 0, prev_ref[...], 0.0)
        xbuf[HALO:HALO + ts, :] = x_ref[...]
        xbuf[HALO + ts:, :] = next_ref[...]
        pre = jnp.broadcast_to(b_ref[...], (ts + HALO, CONV_TC))
        for k in range(CONV_K):
            pre = pre + w_ref[k:k + 1, :] * xbuf[pl.ds(HALO - (CONV_K - 1) + k, ts + HALO), :]
        sg = jax.nn.sigmoid(pre)
        dsilu = sg * (1.0 + pre * (1.0 - sg))
        dbuf[0:ts, :] = dy_ref[...] * dsilu[0:ts]
        dbuf[ts:, :] = jnp.where(si < ns - 1, dyn_ref[...] * dsilu[ts:], 0.0)
        dx = jnp.zeros((ts, CONV_TC), F32)
        for k in range(CONV_K):
            dx = dx + w_ref[k:k + 1, :] * dbuf[pl.ds(CONV_K - 1 - k, ts), :]
        dx_ref[...] = dx.astype(dx_ref.dtype)
        dpre = dbuf[0:ts, :]
        first = si == 0
        _acc_store(db_ref, jnp.sum(dpre, axis=0, keepdims=True), first)
        for k in range(CONV_K):
            dw_k = jnp.sum(dpre * xbuf[pl.ds(HALO - (CONV_K - 1) + k, ts), :], axis=0, keepdims=True)
            _acc_store(dw_ref.at[pl.ds(k, 1), :], dw_k, first)

    main = lambda ci, si: (si, ci)
    x_main = lambda ci, si: (si, ci + c0)
    x_prev = lambda ci, si: (jnp.maximum(si * hb - 1, 0), ci + c0)
    x_next = lambda ci, si: (jnp.minimum(si * hb + hb, last_halo), ci + c0)
    return pl.pallas_call(
        body, grid=(CONV_DIM // CONV_TC, ns),
        in_specs=[pl.BlockSpec((ts, CONV_TC), x_main), pl.BlockSpec((HALO, CONV_TC), x_prev), pl.BlockSpec((HALO, CONV_TC), x_next),
                  pl.BlockSpec((ts, CONV_TC), main),
                  pl.BlockSpec((HALO, CONV_TC), lambda ci, si: (jnp.minimum(si * hb + hb, last_halo), ci)),
                  pl.BlockSpec((CONV_K, CONV_TC), lambda ci, si: (0, ci)),
                  pl.BlockSpec((1, CONV_TC), lambda ci, si: (0, ci))],
        out_specs=[pl.BlockSpec((ts, CONV_TC), main),
                   pl.BlockSpec((CONV_K, CONV_TC), lambda ci, si: (0, ci)),
                   pl.BlockSpec((1, CONV_TC), lambda ci, si: (0, ci))],
        out_shape=[jax.ShapeDtypeStruct((s, CONV_DIM), BF16), jax.ShapeDtypeStruct((CONV_K, CONV_DIM), F32),
                   jax.ShapeDtypeStruct((1, CONV_DIM), F32)],
        scratch_shapes=[pltpu.VMEM((ts + 2 * HALO, CONV_TC), F32), pltpu.VMEM((ts + HALO, CONV_TC), F32)],
        compiler_params=_cparams("arbitrary", "arbitrary"), name="conv_bwd",
    )(x, x, x, dy, dy, w, b)


GW = SSD_HPG * SSD_P
B_COL = SSD_DI
C_COL = SSD_DI + SSD_G * SSD_N


def _ones_where(mask):
    return jnp.where(mask, 1.0, 0.0).astype(BF16)


def _split(v, passes):
    parts, rest = [], v
    for i in range(passes):
        part = rest.astype(BF16)
        parts.append(part)
        if i + 1 < passes:
            rest = rest - part.astype(F32)
    return parts


def _dot_sel_r(v, sel, passes=3):
    out = None
    for part in _split(v, passes):
        t = jnp.dot(part, sel, preferred_element_type=F32)
        out = t if out is None else out + t
    return out


def _dot_sel_l(sel, v, passes=3):
    out = None
    for part in _split(v, passes):
        t = jnp.dot(sel, part, preferred_element_type=F32)
        out = t if out is None else out + t
    return out


def _ssd_consts():
    r = lax.broadcasted_iota(jnp.int32, (SSD_L, SSD_L), 0)
    c = lax.broadcasted_iota(jnp.int32, (SSD_L, SSD_L), 1)
    tril = r >= c
    triu = c >= r
    shift = SSD_P.bit_length() - 1
    eh = lax.broadcasted_iota(jnp.int32, (SSD_H, SSD_DI), 0)
    ej = lax.broadcasted_iota(jnp.int32, (SSD_H, SSD_DI), 1)
    expand = _ones_where(lax.shift_right_logical(ej, shift) == eh)
    rj = lax.broadcasted_iota(jnp.int32, (SSD_DI, SSD_H), 0)
    rh = lax.broadcasted_iota(jnp.int32, (SSD_DI, SSD_H), 1)
    reduce_ = _ones_where(lax.shift_right_logical(rj, shift) == rh)
    lane = lax.broadcasted_iota(jnp.int32, (SSD_L, LANE), 1)
    return tril, triu, expand, reduce_, lane < SSD_P


def _ssd_decays(dt, dt_t, a, a_t, tril, triu, expand):
    dta = dt * a
    acum = _dot_sel_l(_ones_where(tril), dta)
    acum_t = _dot_sel_r(dt_t * a_t, _ones_where(triu))
    dta_e = _dot_sel_r(dta, expand)
    acum_e = _dot_sel_r(acum, expand)
    last_e = jnp.sum(dta_e, axis=0, keepdims=True)
    return acum, acum_t, acum_e, last_e


def _head_decay(acum, acum_t, h, tril):
    seg = acum[:, h:h + 1] - acum_t[h:h + 1, :]
    return jnp.exp(jnp.where(tril, seg, NEG))


def _ssd_fwd_call(xbc, dt, a):
    s = xbc.shape[0]
    nc = s // SSD_L
    dt_t = dt.T
    a_t = a.T

    def body(xbc_ref, dt_ref, dtt_ref, a_ref, at_ref, y_ref, st_ref, s_sc):
        ci = pl.program_id(0)

        @pl.when(ci == 0)
        def _():
            s_sc[...] = jnp.zeros_like(s_sc)

        st_ref[0] = s_sc[...]
        tril, triu, expand, _, low_half = _ssd_consts()
        acum, acum_t, acum_e, last_e = _ssd_decays(dt_ref[...], dtt_ref[...], a_ref[...], at_ref[...], tril, triu, expand)
        dt_e = _dot_sel_r(dt_ref[...], expand, passes=2)
        xdt = xbc_ref[:, :SSD_DI] * dt_e
        xdt_b = xdt.astype(BF16)
        xw_b = (xdt * jnp.exp(last_e - acum_e)).astype(BF16)
        ea_e = jnp.exp(acum_e)
        el_e = jnp.exp(last_e)
        for g in range(SSD_G):
            gs = slice(g * GW, (g + 1) * GW)
            bg = xbc_ref[:, B_COL + g * SSD_N:B_COL + (g + 1) * SSD_N]
            cg_b = xbc_ref[:, C_COL + g * SSD_N:C_COL + (g + 1) * SSD_N].astype(BF16)
            bg_b = bg.astype(BF16)
            cb = _nt(cg_b, bg_b)
            st = s_sc[:, gs]
            y_off = jnp.dot(cg_b, st.astype(BF16), preferred_element_type=F32) * ea_e[:, gs]
            for pr in range(SSD_HPG // 2):
                ls = slice(g * GW + pr * LANE, g * GW + (pr + 1) * LANE)
                xp = xdt_b[:, ls]
                yd = []
                for half in range(2):
                    h = g * SSD_HPG + pr * 2 + half
                    m = (cb * _head_decay(acum, acum_t, h, tril)).astype(BF16)
                    yd.append(jnp.dot(m, xp, preferred_element_type=F32))
                y_ref[:, ls] = jnp.where(low_half, yd[0], yd[1]) + y_off[:, pr * LANE:(pr + 1) * LANE]
            s_sc[:, gs] = st * el_e[:, gs] + jnp.dot(bg.T.astype(BF16), xw_b[:, gs], preferred_element_type=F32)

    row = lambda i: (i, 0)
    return pl.pallas_call(
        body, grid=(nc,),
        in_specs=[pl.BlockSpec((SSD_L, CONV_DIM), row), pl.BlockSpec((SSD_L, SSD_H), row),
                  pl.BlockSpec((SSD_H, SSD_L), lambda i: (0, i)), pl.BlockSpec((1, SSD_H), lambda i: (0, 0)),
                  pl.BlockSpec((SSD_H, 1), lambda i: (0, 0))],
        out_specs=[pl.BlockSpec((SSD_L, SSD_DI), row), pl.BlockSpec((1, SSD_N, SSD_DI), lambda i: (i, 0, 0))],
        out_shape=[jax.ShapeDtypeStruct((s, SSD_DI), F32), jax.ShapeDtypeStruct((nc, SSD_N, SSD_DI), F32)],
        scratch_shapes=[pltpu.VMEM((SSD_N, SSD_DI), F32)],
        compiler_params=_cparams("arbitrary"), name="ssd_fwd",
    )(xbc, dt, dt_t, a, a_t)


def _ssd_bwd_call(xbc, dt, a, states, dy, dx_extra):
    s = xbc.shape[0]
    nc = s // SSD_L
    dt_t = dt.T
    a_t = a.T

    def body(xbc_ref, dt_ref, dtt_ref, a_ref, at_ref, st_ref, dy_ref, dxe_ref,
             dxbc_ref, ddt_ref, da_ref, ds_sc, yf_sc, dxd_sc, dxw_sc):
        i = pl.program_id(0)

        @pl.when(i == 0)
        def _():
            ds_sc[...] = jnp.zeros_like(ds_sc)

        tril, triu, expand, reduce_, low_half = _ssd_consts()
        dt = dt_ref[...]
        a_row = a_ref[...]
        acum, acum_t, acum_e, last_e = _ssd_decays(dt, dtt_ref[...], a_row, at_ref[...], tril, triu, expand)
        dt_e = _dot_sel_r(dt, expand, passes=2)
        x = xbc_ref[:, :SSD_DI]
        xdt = x * dt_e
        xdt_b = xdt.astype(BF16)
        w_e = jnp.exp(last_e - acum_e)
        xw_b = (xdt * w_e).astype(BF16)
        ea_e = jnp.exp(acum_e)
        el_e = jnp.exp(last_e)
        dy = dy_ref[...]
        dy_b = dy.astype(BF16)
        s_prev = st_ref[0]
        ds_new = ds_sc[...]
        ds_new_b = ds_new.astype(BF16)
        triu_b = _ones_where(triu)
        strict_tril = jnp.logical_not(triu)
        head_ids = lax.broadcasted_iota(jnp.int32, (1, SSD_H), 1)
        d_dta_diag = jnp.zeros((SSD_L, SSD_H), F32)
        for g in range(SSD_G):
            gs = slice(g * GW, (g + 1) * GW)
            bs_ = slice(B_COL + g * SSD_N, B_COL + (g + 1) * SSD_N)
            cs_ = slice(C_COL + g * SSD_N, C_COL + (g + 1) * SSD_N)
            bg = xbc_ref[:, bs_]
            cg = xbc_ref[:, cs_]
            bg_b, cg_b = bg.astype(BF16), cg.astype(BF16)
            st_b = s_prev[:, gs].astype(BF16)
            y_off = jnp.dot(cg_b, st_b, preferred_element_type=F32) * ea_e[:, gs]
            yf_sc[:, gs] = y_off
            dz_b = (dy[:, gs] * ea_e[:, gs]).astype(BF16)
            d_c = _nt(dz_b, st_b)
            ds_prev = ds_new[:, gs] * el_e[:, gs] + jnp.dot(cg.T.astype(BF16), dz_b, preferred_element_type=F32)
            dxw_sc[:, gs] = jnp.dot(bg_b, ds_new_b[:, gs], preferred_element_type=F32)
            d_b = _nt(xw_b[:, gs], ds_new_b[:, gs])
            cb = _nt(cg_b, bg_b)
            d_g = jnp.zeros((SSD_L, SSD_L), F32)
            for pr in range(SSD_HPG // 2):
                ls = slice(g * GW + pr * LANE, g * GW + (pr + 1) * LANE)
                xp = xdt_b[:, ls]
                dyp = dy[:, ls]
                dyp_b = dy_b[:, ls]
                dxd = []
                for half in range(2):
                    h = g * SSD_HPG + pr * 2 + half
                    dec = _head_decay(acum, acum_t, h, tril)
                    m = cb * dec
                    dxd.append(jnp.dot(m.T.astype(BF16), dyp_b, preferred_element_type=F32))
                    mine = low_half if half == 0 else jnp.logical_not(low_half)
                    d_m = _nt(jnp.where(mine, dyp, 0.0).astype(BF16), xp)
                    d_g = d_g + d_m * dec
                    below = jnp.dot(triu_b, (d_m * m).astype(BF16), preferred_element_type=F32)
                    col = jnp.sum(jnp.where(strict_tril, below, 0.0), axis=1, keepdims=True)
                    d_dta_diag = d_dta_diag + col * jnp.where(head_ids == h, 1.0, 0.0)
                dxd_sc[:, ls] = jnp.where(low_half, dxd[0], dxd[1])
            d_g_b = d_g.astype(BF16)
            dxbc_ref[:, cs_] = d_c + jnp.dot(d_g_b, bg_b, preferred_element_type=F32)
            dxbc_ref[:, bs_] = d_b + jnp.dot(d_g.T.astype(BF16), cg_b, preferred_element_type=F32)
            ds_sc[:, gs] = ds_prev
        dxw = dxw_sc[...]
        dxd = dxd_sc[...]
        dw_e = xdt * dxw * w_e
        d_tot_e = jnp.sum(ds_new * s_prev, axis=0, keepdims=True) * el_e
        d_state_e = (_dot_sel_l(triu_b, dy * yf_sc[...], passes=2)
                     + _dot_sel_l(_ones_where(strict_tril), dw_e, passes=2) + d_tot_e)
        dxdt = dxd + dxw * w_e
        dxbc_ref[:, :SSD_DI] = dxdt * dt_e + dxe_ref[...]
        a_e = _dot_sel_r(jnp.broadcast_to(a_row, (8, SSD_H)), expand)[0:1]
        ddt_ref[...] = _dot_sel_r(d_state_e * a_e + dxdt * x, reduce_, passes=2) + d_dta_diag * a_row
        d_a_e = jnp.sum(d_state_e * dt_e, axis=0, keepdims=True)
        d_a = _dot_sel_r(jnp.broadcast_to(d_a_e, (8, SSD_DI)), reduce_)[0:1] + jnp.sum(d_dta_diag * dt, axis=0, keepdims=True)
        _acc_store(da_ref, d_a, i == 0)

    rev = lambda i: (nc - 1 - i, 0)
    return pl.pallas_call(
        body, grid=(nc,),
        in_specs=[pl.BlockSpec((SSD_L, CONV_DIM), rev), pl.BlockSpec((SSD_L, SSD_H), rev),
                  pl.BlockSpec((SSD_H, SSD_L), lambda i: (0, nc - 1 - i)), pl.BlockSpec((1, SSD_H), lambda i: (0, 0)),
                  pl.BlockSpec((SSD_H, 1), lambda i: (0, 0)),
                  pl.BlockSpec((1, SSD_N, SSD_DI), lambda i: (nc - 1 - i, 0, 0)),
                  pl.BlockSpec((SSD_L, SSD_DI), rev), pl.BlockSpec((SSD_L, SSD_DI), rev)],
        out_specs=[pl.BlockSpec((SSD_L, CONV_DIM), rev), pl.BlockSpec((SSD_L, SSD_H), rev),
                   pl.BlockSpec((1, SSD_H), lambda i: (0, 0))],
        out_shape=[jax.ShapeDtypeStruct((s, CONV_DIM), F32), jax.ShapeDtypeStruct((s, SSD_H), F32),
                   jax.ShapeDtypeStruct((1, SSD_H), F32)],
        scratch_shapes=[pltpu.VMEM((SSD_N, SSD_DI), F32), pltpu.VMEM((SSD_L, SSD_DI), F32),
                        pltpu.VMEM((SSD_L, SSD_DI), F32), pltpu.VMEM((SSD_L, SSD_DI), F32)],
        compiler_params=_cparams("arbitrary"), name="ssd_bwd",
    )(xbc, dt, dt_t, a, a_t, states, dy, dx_extra)


HBM_SPEC = pl.BlockSpec(memory_space=pltpu.HBM)
N_PEERS = N_DEV - 1


def _flip(v, f):
    return 1 - v if f else v


def _all_gather(shard):
    rows, c = shard.shape

    def body(x_ref, out_ref, send_sems, recv_sems, local_sem):
        x, y, cc = lax.axis_index("x"), lax.axis_index("y"), lax.axis_index("c")
        me, sibling = (x, y, cc), (x, y, 1 - cc)
        chips = [(1 - x, y), (x, 1 - y), (1 - x, 1 - y)]

        def slot(px, py, pc):
            return out_ref.at[4 * px + 2 * py + pc]

        def copy(k, block, to, src=None):
            return pltpu.make_async_remote_copy(
                src_ref=slot(*block) if src is None else src, dst_ref=slot(*block),
                send_sem=send_sems.at[k], recv_sem=recv_sems.at[k],
                device_id=to, device_id_type=pl.DeviceIdType.MESH)

        mine = pltpu.make_async_copy(x_ref, slot(*me), local_sem)
        mine.start()
        first = [copy(0, me, sibling, src=x_ref)]
        first += [copy(1 + j, me, (*chip, cc), src=x_ref) for j, chip in enumerate(chips)]
        for cp in first:
            cp.start()
        passed = [copy(4 + j, (*chip, cc), sibling) for j, chip in enumerate(chips)]
        for j, chip in enumerate(chips):
            copy(1 + j, (*chip, cc), me).wait_recv()
            passed[j].start()
        copy(0, sibling, me).wait_recv()
        for j, chip in enumerate(chips):
            copy(4 + j, (*chip, 1 - cc), me).wait_recv()
        for cp in first + passed:
            cp.wait_send()
        mine.wait()

    return pl.pallas_call(
        body, out_shape=jax.ShapeDtypeStruct((N_DEV, rows, c), shard.dtype),
        in_specs=[HBM_SPEC], out_specs=HBM_SPEC,
        scratch_shapes=[pltpu.SemaphoreType.DMA((N_PEERS,)), pltpu.SemaphoreType.DMA((N_PEERS,)), pltpu.SemaphoreType.DMA(())],
        name="all_gather",
    )(shard)


def _peer_copies(src_refs, out_ref, sems, gather, phase):
    send_sems, recv_sems, local_sem = sems
    x, y, cc = lax.axis_index("x"), lax.axis_index("y"), lax.axis_index("c")
    me = 4 * x + 2 * y + cc

    def pieces(block, slot):
        if gather:
            return [(src_refs[0], out_ref.at[slot])]
        out, r0 = [], 0
        for src in src_refs:
            out.append((src.at[block], out_ref.at[slot, pl.ds(r0, src.shape[1])]))
            r0 += src.shape[1]
        assert r0 == out_ref.shape[1], (r0, out_ref.shape)
        return out

    if phase == "start":
        for src, dst in pieces(me, me):
            pltpu.make_async_copy(src, dst, local_sem).start()
    for k in range(1, N_DEV):
        px, py, pc = _flip(x, k & 4), _flip(y, k & 2), _flip(cc, k & 1)
        peer = 4 * px + 2 * py + pc
        to_peer = dict(send_sem=send_sems.at[k - 1], recv_sem=recv_sems.at[k - 1],
                       device_id=(px, py, pc), device_id_type=pl.DeviceIdType.MESH)
        if phase == "start":
            for src, dst in pieces(peer, me):
                pltpu.make_async_remote_copy(src_ref=src, dst_ref=dst, **to_peer).start()
        else:
            whole = pltpu.make_async_remote_copy(src_ref=out_ref.at[peer], dst_ref=out_ref.at[peer], **to_peer)
            whole.wait_recv()
            whole.wait_send()
    if phase != "start":
        pltpu.make_async_copy(out_ref.at[me], out_ref.at[me], local_sem).wait()


def _comm_scratch():
    return [pltpu.SemaphoreType.DMA((N_PEERS,)), pltpu.SemaphoreType.DMA((N_PEERS,)), pltpu.SemaphoreType.DMA(())]


def _gather_rider(shard):
    return dict(srcs=[shard], out=jax.ShapeDtypeStruct((N_DEV,) + shard.shape, shard.dtype), gather=True)


def _exchange_out(parts):
    rows = sum(p.shape[1] for p in parts)
    return jax.ShapeDtypeStruct((N_DEV, rows) + parts[0].shape[2:], parts[0].dtype)


def _exchange_rider(parts):
    return dict(srcs=list(parts), out=_exchange_out(parts), gather=False)


def _exchange_blocks(parts):
    n = len(parts)

    def body(*refs):
        _peer_copies(refs[:n], refs[n], refs[n + 1:], False, "start")
        _peer_copies(refs[:n], refs[n], refs[n + 1:], False, "finish")

    return pl.pallas_call(
        body, out_shape=_exchange_out(parts),
        in_specs=[HBM_SPEC] * n, out_specs=HBM_SPEC, scratch_shapes=_comm_scratch(), name="exchange_blocks",
    )(*parts)


BIG = [
    ("ffn1_w13", (D_MODEL, 2 * D_FF), 1), ("ffn1_w2", (D_FF, D_MODEL), 0),
    ("w_ssd_out", (SSD_DI, D_MODEL), 0), ("w_uq", (Q_LORA, MLA_H * QK), 1), ("w_ukv", (KV_LORA, MLA_H * (NOPE + VDIM)), 1),
    ("w_mla_out", (MLA_H * VDIM, D_MODEL), 0), ("w_o", (D_MODEL, D_MODEL), 0),
    ("ffn2_w13", (D_MODEL, 2 * D_FF), 1), ("ffn2_w2", (D_FF, D_MODEL), 0), ("w_in", (D_MODEL, D_IN), 1),
]
assert all(_r % 16 == 0 for _r in [_f[0] * _f[1] // N_DEV // PACK_COLS for _, _f, _ in BIG[:-1]])
SMALL = [
    ("ln_ffn1", D_MODEL), ("ln_mix", D_MODEL), ("conv_b", CONV_DIM), ("dt_bias", SSD_H), ("a_log", SSD_H), ("d_skip", SSD_H),
    ("ssd_norm", SSD_DI), ("q_lora_norm", Q_LORA), ("kv_lora_norm", KV_LORA), ("q_norm", QK), ("k_norm", QK), ("ln_ffn2", D_MODEL),
]


def _shard_shape(full, axis):
    k, n = full
    return (k // N_DEV, n) if axis == 0 else (k, n // N_DEV)


def _shard_rows(full):
    return full[0] * full[1] // N_DEV // PACK_COLS


LAYER_ROWS = sum(_shard_rows(f) for _, f, _ in BIG)
LAYER_ROWS_PAD = -(-LAYER_ROWS // 256) * 256


def _pack_shards(shards):
    parts = [(shards[name] if axis == 0 else shards[name].T).reshape(-1, PACK_COLS) for name, _, axis in BIG]
    pad = LAYER_ROWS_PAD - LAYER_ROWS
    if pad:
        parts.append(jnp.zeros((pad, PACK_COLS), parts[0].dtype))
    return jnp.concatenate(parts, axis=0)


BIG_BY_NAME = {name: (full, axis) for name, full, axis in BIG}
BIG_NAMES = [name for name, _, _ in BIG]
EARLY = ["ffn2_w13", "ffn2_w2", "w_o", "w_mla_out", "w_ssd_out"]
LATE = [name for name in BIG_NAMES if name not in EARLY]
SUM_ROWS = 128


def _part_rows(name):
    return -(-_shard_rows(BIG_BY_NAME[name][0]) // 16) * 16


def _grad_parts(grads, names):
    parts = []
    for name in names:
        part = grads[name].reshape(N_DEV, -1, PACK_COLS)
        parts.append(jnp.pad(part, ((0, 0), (0, _part_rows(name) - part.shape[1]), (0, 0))))
    return parts


def _pad_parts(parts):
    pad = -sum(p.shape[1] for p in parts) % SUM_ROWS
    return parts + ([jnp.zeros((N_DEV, pad, PACK_COLS), parts[0].dtype)] if pad else [])


def _unpack_parts(summed, names, r=0):
    out = {}
    for name in names:
        full, axis = BIG_BY_NAME[name]
        k, c = _shard_shape(full, axis)
        blk = summed[r:r + _shard_rows(full)]
        out[name] = blk.reshape(k, c) if axis == 0 else blk.reshape(c, k).T
        r += _part_rows(name)
    return out, r


def _working_shape(full, axis):
    return full if axis == 0 else full[::-1]


def _unpack_gathered(gathered):
    out, r = {}, 0
    for name, full, axis in BIG:
        n = _shard_rows(full)
        out[name] = gathered[:, r:r + n].reshape(_working_shape(full, axis))
        r += n
    return out


SMALL_COLS = sum(n for _, n in SMALL) + CONV_K * CONV_DIM
SMALL_ROWS = -(-(DEPTH * SMALL_COLS) // (8 * PACK_COLS)) * 8


def _pack_small(vals, conv_w):
    flat = jnp.concatenate([vals[name] for name, _ in SMALL] + [conv_w.reshape(DEPTH, -1)], axis=1).reshape(-1)
    flat = jnp.concatenate([flat, jnp.zeros((SMALL_ROWS * PACK_COLS - flat.shape[0],), F32)])
    return flat.reshape(SMALL_ROWS, PACK_COLS)


def _unpack_small(packed):
    flat = packed.reshape(-1)[:DEPTH * SMALL_COLS].reshape(DEPTH, SMALL_COLS)
    out, c = {}, 0
    for name, n in SMALL:
        out[name] = flat[:, c:c + n]
        c += n
    return out, flat[:, c:].reshape(DEPTH, CONV_K, CONV_DIM)


_IN_OFFS = [sum(IN_SPLIT[:i]) for i in range(len(IN_SPLIT) + 1)]


def _arrange_w_in(w_t):
    z, xbc, dt, cq, ckv, kr, gates = [w_t[_IN_OFFS[i]:_IN_OFFS[i + 1]] for i in range(len(IN_SPLIT))]
    pad = jnp.zeros((LANE - ROPE - SSD_H, w_t.shape[1]), w_t.dtype)
    return jnp.concatenate([z, gates, xbc, cq, ckv, kr, dt, pad], axis=0)


def _restore_w_in(g):
    z, gates, xbc = g[PROJ_Z:PROJ_GATES], g[PROJ_GATES:PROJ_XBC], g[PROJ_XBC:PROJ_CQ]
    cq, ckv = g[PROJ_CQ:PROJ_CKV], g[PROJ_CKV:PROJ_LAST]
    kr, dt = g[PROJ_LAST:PROJ_LAST + ROPE], g[PROJ_LAST + ROPE:PROJ_LAST + ROPE + SSD_H]
    return jnp.concatenate([z, xbc, dt, cq, ckv, kr, gates], axis=0)


def _pad_heads(w_t):
    k = w_t.shape[1]
    return jnp.pad(w_t.reshape(MLA_H, QK, k), ((0, 0), (0, HEAD_COLS - QK), (0, 0))).reshape(MLA_H * HEAD_COLS, k)


def _unpad_heads(g):
    k = g.shape[1]
    return g.reshape(MLA_H, HEAD_COLS, k)[:, :QK].reshape(MLA_H * QK, k)


def _row(v):
    return v.reshape(1, -1)


def _head_gain(g):
    return jnp.pad(g, (0, HEAD_COLS - QK)).reshape(1, HEAD_COLS)


def _ffn_fwd(h, ln, w13_t, w2, name):
    n = _row_fwd(_f_rmsnorm, [h], [_row(ln)], [BF16], name + "_fwd")[0]
    act, gate, up = _ffn_up_call(n, w13_t)
    return _mm(act, w2, alpha=0.5, res=h), (h, n, gate, up, act)


def _ffn_bwd(dh_out, saved, ln, w13_t, w2, name):
    h, n, gate, up, act = saved
    d_gate, d_up = _ffn_down_bwd_call(dh_out, w2, gate, up)
    d_w2 = _mm(act, dh_out, ta=True, out_dtype=BF16, alpha=0.5)
    d_n = _mm(d_gate, w13_t, b_rows=(0, D_FF))
    dh, d_ln = _mm(d_up, w13_t, b_rows=(D_FF, D_FF), res=d_n, norm_bwd=(h, _row(ln), dh_out))
    d_w13_t = jnp.concatenate([_mm(d_gate, n, ta=True, out_dtype=BF16), _mm(d_up, n, ta=True, out_dtype=BF16)], axis=0)
    return dh, d_w13_t, d_w2, d_ln[0]


def _mixer_fwd(h, big, small, conv_w, cs, sn, rider=None):
    s = h.shape[0]
    u = _row_fwd(_f_rmsnorm, [h], [_row(small["ln_mix"])], [BF16], "ln_mix_fwd")[0]
    proj = _mm(u, big["w_in"], tb=True)
    xbc = _conv_fwd_call(proj, PROJ_XBC, conv_w, _row(small["conv_b"]))
    dt_in = proj[:, PROJ_LAST + ROPE:PROJ_LAST + ROPE + SSD_H] + small["dt_bias"][None, :]
    dt = jax.nn.softplus(dt_in)
    a = -jnp.exp(small["a_log"])[None, :]
    y_scan, states = _ssd_fwd_call(xbc, dt, a)
    dsk = _row(jnp.repeat(small["d_skip"], SSD_P))
    gn_in = [y_scan, _win(xbc, 0, SSD_DI), _win(proj, PROJ_Z, SSD_DI)]
    yn = _row_fwd(_f_gated_norm, gn_in, [dsk, _row(small["ssd_norm"])], [BF16], "gated_norm_fwd")[0]
    y_ssd = _mm(yn, big["w_ssd_out"])
    qn = _row_fwd(_f_rmsnorm, [_win(proj, PROJ_CQ, Q_LORA)], [_row(small["q_lora_norm"])], [BF16], "q_lora_norm_fwd")[0]
    kvn = _row_fwd(_f_rmsnorm, [_win(proj, PROJ_CKV, KV_LORA)], [_row(small["kv_lora_norm"])], [BF16], "kv_lora_norm_fwd")[0]
    q = _mm(qn, big["w_uq"], tb=True)
    kv = _mm(kvn, big["w_ukv"], tb=True)
    qh, kh, vh = _heads_fwd_call(q, kv, proj, cs, sn, _head_gain(small["q_norm"]), _head_gain(small["k_norm"]))
    o, lse, *carried = _attn_fwd_call(qh, kh, vh, rider)
    o_rows = jnp.transpose(o, (1, 0, 2)).reshape(s, MLA_H * VDIM)
    y_mla = _mm(o_rows, big["w_mla_out"])
    out, mg = _merge_out_call(proj, y_ssd, y_mla, big["w_o"], h)
    saved = (h, u, proj, xbc, dt_in, dt, a, y_scan, states, dsk, yn, y_ssd, qn, kvn, q, kv, qh, kh, vh, o, lse, o_rows, y_mla, mg)
    return out, saved, (carried[0] if carried else None)


def _mixer_bwd(dh_out, saved, big, small, conv_w, cs, sn, carry_parts=None):
    (h, u, proj, xbc, dt_in, dt, a, y_scan, states, dsk, yn, y_ssd, qn, kvn, q, kv, qh, kh, vh, o, lse, o_rows, y_mla, mg) = saved
    s = h.shape[0]
    d_big, d_small = {}, {}
    d_gates, d_y_ssd, d_y_mla = _merge_out_bwd_call(dh_out, big["w_o"], proj, y_ssd, y_mla)
    d_big["w_o"] = _mm(mg, dh_out, ta=True, out_dtype=BF16)
    d_o_rows, delta = _attn_out_bwd_call(d_y_mla, big["w_mla_out"], o_rows)
    d_big["w_mla_out"] = _mm(o_rows, d_y_mla, ta=True, out_dtype=BF16)
    d_o = jnp.transpose(d_o_rows.reshape(s, MLA_H, VDIM), (1, 0, 2))
    d_yn = _mm(d_y_ssd, big["w_ssd_out"], tb=True, out_dtype=BF16)
    d_big["w_ssd_out"] = _mm(yn, d_y_ssd, ta=True, out_dtype=BF16)
    rider = None
    if carry_parts is not None:
        rider = _exchange_rider(_pad_parts(carry_parts + _grad_parts(d_big, EARLY[2:])))
    *d_heads, carried = list(_attn_bwd_call(qh, kh, vh, d_o, lse.reshape(MLA_H, 1, s), delta.T.reshape(MLA_H, 1, s), rider)) + ([None] if rider is None else [])
    d_q, d_kv, d_kr, d_qg, d_kg = _heads_bwd_call(
        q, kv, proj, cs, sn, _head_gain(small["q_norm"]), _head_gain(small["k_norm"]), *d_heads)
    d_small["q_norm"], d_small["k_norm"] = d_qg[0, :QK], d_kg[0, :QK]
    d_qn = _mm(d_q, big["w_uq"], out_dtype=BF16)
    d_big["w_uq"] = _mm(d_q, qn, ta=True, out_dtype=BF16)
    d_kvn = _mm(d_kv, big["w_ukv"], out_dtype=BF16)
    d_big["w_ukv"] = _mm(d_kv, kvn, ta=True, out_dtype=BF16)
    (d_cq,), (d_g,) = _row_bwd(_f_rmsnorm, [_win(proj, PROJ_CQ, Q_LORA)], [_row(small["q_lora_norm"])], [d_qn], [BF16], "q_lora_norm_bwd")
    d_small["q_lora_norm"] = d_g[0]
    (d_ckv,), (d_g,) = _row_bwd(_f_rmsnorm, [_win(proj, PROJ_CKV, KV_LORA)], [_row(small["kv_lora_norm"])], [d_kvn], [BF16], "kv_lora_norm_bwd")
    d_small["kv_lora_norm"] = d_g[0]
    gn_in = [y_scan, _win(xbc, 0, SSD_DI), _win(proj, PROJ_Z, SSD_DI)]
    (d_y_scan, d_xs, d_z), (d_dsk, d_g) = _row_bwd(
        _f_gated_norm, gn_in, [dsk, _row(small["ssd_norm"])], [d_yn], [F32, F32, BF16], "gated_norm_bwd")
    d_small["ssd_norm"] = d_g[0]
    d_small["d_skip"] = jnp.sum(d_dsk.reshape(SSD_H, SSD_P), axis=1)
    d_xbc_act, d_dt, d_a = _ssd_bwd_call(xbc, dt, a, states, d_y_scan, d_xs)
    d_xbc, d_conv_w, d_conv_b = _conv_bwd_call(proj, PROJ_XBC, conv_w, _row(small["conv_b"]), d_xbc_act)
    d_small["conv_b"] = d_conv_b[0]
    d_dt_in = d_dt * jax.nn.sigmoid(dt_in)
    d_small["dt_bias"] = jnp.sum(d_dt_in, axis=0)
    d_small["a_log"] = d_a[0] * a[0]
    d_last = (d_kr + jnp.pad(d_dt_in, ((0, 0), (ROPE, LANE - ROPE - SSD_H)))).astype(BF16)
    d_proj = jnp.concatenate([d_z, d_gates, d_xbc, d_cq, d_ckv, d_last], axis=1)
    dh, d_ln = _mm(d_proj, big["w_in"], norm_bwd=(h, _row(small["ln_mix"]), dh_out))
    d_big["w_in"] = _mm(d_proj, u, ta=True, out_dtype=BF16)
    d_small["ln_mix"] = d_ln[0]
    return dh, d_big, d_small, d_conv_w, carried


def _prepare_big(b):
    return dict(b, w_in=_arrange_w_in(b["w_in"]), w_uq=_pad_heads(b["w_uq"]))


def _local_step(x, positions, target, big, small, conv_w, packed_last=None):
    inv = 1.0 / (ROPE_THETA ** (jnp.arange(0, ROPE, 2, dtype=F32) / ROPE))
    ang = positions.astype(F32)[:, None] * inv
    cos, sin = jnp.cos(ang), jnp.sin(ang)
    no_lanes = jnp.zeros((x.shape[0], LANE - ROPE), F32)
    cs = jnp.concatenate([cos, cos, no_lanes], axis=1)
    sn = jnp.concatenate([-sin, sin, no_lanes], axis=1)
    carrier = DEPTH - 2 if packed_last is not None else None
    big = [None if b is None else _prepare_big(b) for b in big]
    layer_small = [{k: v[l] for k, v in small.items()} for l in range(DEPTH)]

    h, saved = x, []
    for l in range(DEPTH):
        b, sm = big[l], layer_small[l]
        h, s1 = _ffn_fwd(h, sm["ln_ffn1"], b["ffn1_w13"], b["ffn1_w2"], "ln_ffn1")
        h, s2, gathered = _mixer_fwd(h, b, sm, conv_w[l], cs, sn, _gather_rider(packed_last) if l == carrier else None)
        if gathered is not None:
            big[l + 1] = _prepare_big(_unpack_gathered(gathered))
        h, s3 = _ffn_fwd(h, sm["ln_ffn2"], b["ffn2_w13"], b["ffn2_w2"], "ln_ffn2")
        saved.append((s1, s2, s3))
    loss, dh = _loss_and_grad(h, target)

    d_big, d_small, d_conv_w = [None] * DEPTH, [None] * DEPTH, [None] * DEPTH
    for l in reversed(range(DEPTH)):
        b, sm = big[l], layer_small[l]
        s1, s2, s3 = saved[l]
        dh, d_w13_2, d_w2_2, d_ln2 = _ffn_bwd(dh, s3, sm["ln_ffn2"], b["ffn2_w13"], b["ffn2_w2"], "ln_ffn2")
        carry_parts = None
        if l == carrier:
            carry_parts = _grad_parts(d_big[l + 1], BIG_NAMES) + _grad_parts({"ffn2_w13": d_w13_2, "ffn2_w2": d_w2_2}, EARLY[:2])
        dh, db, ds, d_conv_w[l], received = _mixer_bwd(dh, s2, b, sm, conv_w[l], cs, sn, carry_parts)
        if received is not None:
            d_big[l + 1] = received
        dh, d_w13_1, d_w2_1, d_ln1 = _ffn_bwd(dh, s1, sm["ln_ffn1"], b["ffn1_w13"], b["ffn1_w2"], "ln_ffn1")
        db.update(ffn1_w13=d_w13_1, ffn1_w2=d_w2_1, ffn2_w13=d_w13_2, ffn2_w2=d_w2_2,
                  w_in=_restore_w_in(db["w_in"]), w_uq=_unpad_heads(db["w_uq"]))
        ds.update(ln_ffn1=d_ln1, ln_ffn2=d_ln2)
        d_big[l], d_small[l] = db, ds
    d_small = {name: jnp.stack([d_small[l][name] for l in range(DEPTH)]) for name, _ in SMALL}
    return loss, dh, d_big, d_small, jnp.stack(d_conv_w)


def _step(args):
    dev = 4 * lax.axis_index("x") + 2 * lax.axis_index("y") + lax.axis_index("c")
    x, positions, target = args["x"][0], args["positions"][0], args["loss_target"][0]

    packed = [_pack_shards({name: args[name][l].astype(BF16) for name, _, _ in BIG}) for l in range(DEPTH)]
    big = [_unpack_gathered(_all_gather(packed[l])) for l in range(DEPTH - 1)] + [None]
    cw = args["conv_w"]
    cw_cols = cw.shape[-1]
    cw_rows = -(-cw.size // (8 * PACK_COLS)) * 8
    cw_flat = jnp.concatenate([cw.reshape(-1), jnp.zeros((cw_rows * PACK_COLS - cw.size,), F32)]).reshape(cw_rows, PACK_COLS)
    cw_all = _all_gather(cw_flat).reshape(N_DEV, -1)[:, :cw.size].reshape(N_DEV, DEPTH, CONV_K, cw_cols)
    conv_w = jnp.transpose(cw_all, (1, 2, 0, 3)).reshape(DEPTH, CONV_K, CONV_DIM)
    small = {name: args[name] for name, _ in SMALL}

    loss, dx, d_big, d_small, d_conv_w = _local_step(x, positions, target, big, small, conv_w, packed_last=packed[-1])
    loss = lax.psum(loss, MESH_AXES)

    out = {"loss": loss, "grad_x": dx[None]}

    assert DEPTH == 2
    grads = {name: [None] * DEPTH for name in BIG_NAMES}
    summed = _sum_blocks(d_big[1])
    own, r = _unpack_parts(summed, BIG_NAMES)
    early, _ = _unpack_parts(summed, EARLY, r)
    late, _ = _unpack_parts(_sum_blocks(_exchange_blocks(_pad_parts(_grad_parts(d_big[0], LATE)))), LATE)
    for name in BIG_NAMES:
        grads[name] = [early[name] if name in EARLY else late[name], own[name]]
    flat = lambda t: t.reshape(-1, t.shape[-1])
    for name, _, _ in BIG:
        g = jnp.stack(grads[name])
        w = args[name]
        delta, m2, v2 = _adam(flat(w), flat(g), flat(args["m_" + name]), flat(args["v_" + name]))
        out["grad_" + name] = g
        out["delta_" + name] = delta.reshape(w.shape)
        out["new_m_" + name] = m2.reshape(w.shape)
        out["new_v_" + name] = v2.reshape(w.shape)

    total = _sum_blocks(_all_gather(_pack_small(d_small, d_conv_w)))
    g_conv_w = _unpack_small(total)[1]
    zeros_cw = jnp.zeros((DEPTH, CONV_K, CONV_DIM), F32)
    delta, m2, v2 = _adam(_pack_small(small, zeros_cw), total,
                          _pack_small({name: args["m_" + name] for name, _ in SMALL}, zeros_cw),
                          _pack_small({name: args["v_" + name] for name, _ in SMALL}, zeros_cw))
    for kind, packed in (("grad_", total), ("delta_", delta), ("new_m_", m2), ("new_v_", v2)):
        for name, val in _unpack_small(packed)[0].items():
            out[kind + name] = val
    g_cw = lax.dynamic_slice_in_dim(g_conv_w, dev * cw_cols, cw_cols, axis=2)
    delta, m2, v2 = _adam(flat(cw), flat(g_cw), flat(args["m_conv_w"]), flat(args["v_conv_w"]))
    out["grad_conv_w"] = g_cw
    out["delta_conv_w"] = delta.reshape(cw.shape)
    out["new_m_conv_w"] = m2.reshape(cw.shape)
    out["new_v_conv_w"] = v2.reshape(cw.shape)
    return out


WEIGHTS = ["ln_ffn1", "ffn1_w13", "ffn1_w2", "ln_mix", "w_in", "conv_w", "conv_b", "dt_bias", "a_log", "d_skip", "ssd_norm",
           "w_ssd_out", "q_lora_norm", "w_uq", "kv_lora_norm", "w_ukv", "q_norm", "k_norm", "w_mla_out", "w_o", "ln_ffn2",
           "ffn2_w13", "ffn2_w2"]
ARG_NAMES = (["x", "positions"] + WEIGHTS + ["loss_target"] + ["m_" + n for n in WEIGHTS] + ["v_" + n for n in WEIGHTS])


def kernel(x, positions, ln_ffn1, ffn1_w13, ffn1_w2, ln_mix, w_in, conv_w, conv_b, dt_bias, a_log, d_skip, ssd_norm, w_ssd_out, q_lora_norm, w_uq, kv_lora_norm, w_ukv, q_norm, k_norm, w_mla_out, w_o, ln_ffn2, ffn2_w13, ffn2_w2, loss_target, m_ln_ffn1, m_ffn1_w13, m_ffn1_w2, m_ln_mix, m_w_in, m_conv_w, m_conv_b, m_dt_bias, m_a_log, m_d_skip, m_ssd_norm, m_w_ssd_out, m_q_lora_norm, m_w_uq, m_kv_lora_norm, m_w_ukv, m_q_norm, m_k_norm, m_w_mla_out, m_w_o, m_ln_ffn2, m_ffn2_w13, m_ffn2_w2, v_ln_ffn1, v_ffn1_w13, v_ffn1_w2, v_ln_mix, v_w_in, v_conv_w, v_conv_b, v_dt_bias, v_a_log, v_d_skip, v_ssd_norm, v_w_ssd_out, v_q_lora_norm, v_w_uq, v_kv_lora_norm, v_w_ukv, v_q_norm, v_k_norm, v_w_mla_out, v_w_o, v_ln_ffn2, v_ffn2_w13, v_ffn2_w2):
    vals = (x, positions, ln_ffn1, ffn1_w13, ffn1_w2, ln_mix, w_in, conv_w, conv_b, dt_bias, a_log, d_skip, ssd_norm, w_ssd_out, q_lora_norm, w_uq, kv_lora_norm, w_ukv, q_norm, k_norm, w_mla_out, w_o, ln_ffn2, ffn2_w13, ffn2_w2, loss_target, m_ln_ffn1, m_ffn1_w13, m_ffn1_w2, m_ln_mix, m_w_in, m_conv_w, m_conv_b, m_dt_bias, m_a_log, m_d_skip, m_ssd_norm, m_w_ssd_out, m_q_lora_norm, m_w_uq, m_kv_lora_norm, m_w_ukv, m_q_norm, m_k_norm, m_w_mla_out, m_w_o, m_ln_ffn2, m_ffn2_w13, m_ffn2_w2, v_ln_ffn1, v_ffn1_w13, v_ffn1_w2, v_ln_mix, v_w_in, v_conv_w, v_conv_b, v_dt_bias, v_a_log, v_d_skip, v_ssd_norm, v_w_ssd_out, v_q_lora_norm, v_w_uq, v_kv_lora_norm, v_w_ukv, v_q_norm, v_k_norm, v_w_mla_out, v_w_o, v_ln_ffn2, v_ffn2_w13, v_ffn2_w2)
    out = _step(dict(zip(ARG_NAMES, vals)))
    order = ["loss", "grad_x"] + [k + n for k in ("grad_", "delta_", "new_m_", "new_v_") for n in WEIGHTS]
    return tuple(out[n] for n in order)
```

```python
import jax
import jax.numpy as jnp
from jax import lax
from jax.experimental import pallas as pl
from jax.experimental.pallas import tpu as pltpu

F32 = jnp.float32
BF16 = jnp.bfloat16

D_MODEL = 1024
D_FF = 2816
DEPTH = 2
SSD_DI = 2048
SSD_P = 64
SSD_H = 32
SSD_G = 4
SSD_HPG = 8
SSD_N = 128
SSD_L = 128
CONV_K = 4
CONV_DIM = 3072
MLA_H = 8
Q_LORA = 512
KV_LORA = 256
NOPE = 128
ROPE = 64
VDIM = 128
QK = 192
ROPE_THETA = 10000.0
EPS = 1e-6
IN_SPLIT = (SSD_DI, CONV_DIM, SSD_H, Q_LORA, KV_LORA, ROPE, 2 * D_MODEL)
D_IN = sum(IN_SPLIT)
N_DEV = 8
LANE = 128
PACK_COLS = 1024

PROJ_Z = 0
PROJ_GATES = PROJ_Z + SSD_DI
PROJ_XBC = PROJ_GATES + 2 * D_MODEL
PROJ_CQ = PROJ_XBC + CONV_DIM
PROJ_CKV = PROJ_CQ + Q_LORA
PROJ_LAST = PROJ_CKV + KV_LORA
D_IN_PAD = PROJ_LAST + LANE

ADAM_LR = 0.001
ADAM_B1 = 0.9
ADAM_B2 = 0.999
ADAM_EPS = 1e-08
ADAM_WD = 0.01
ADAM_STEP = 10

VMEM_LIMIT = 48 * 1024 * 1024
ROW_IO_BUDGET = 8 * 1024 * 1024
NEG = -1e30

MESH_AXES = ("x", "y", "c")


def _cparams(*sem):
    return pltpu.CompilerParams(dimension_semantics=sem, vmem_limit_bytes=VMEM_LIMIT)


def _pick_tile(n, target, align):
    if n <= target:
        return n
    best = None
    for t in range(align, target + 1, align):
        if n % t == 0:
            best = t
    assert best is not None, (n, target, align)
    return best


def _acc_store(ref, val, first):
    @pl.when(first)
    def _():
        ref[...] = val

    @pl.when(jnp.logical_not(first))
    def _():
        ref[...] += val


def _win(arr, start, width):
    assert start % width == 0, (start, width)
    return (arr, start, width)


def _operand(entry):
    if isinstance(entry, tuple):
        arr, start, width = entry
        return arr, width, start // width
    return entry, entry.shape[1], 0


def _row_tile(rows, bytes_per_row):
    if rows <= 16:
        return rows
    t = 1024
    while t > 16 and (t * bytes_per_row > ROW_IO_BUDGET or rows % t):
        t //= 2
    assert rows % t == 0, (rows, t)
    return t


def _rowwise_call(fn, tiled, params, outs, accs, name):
    ops = [_operand(e) for e in tiled]
    rows = ops[0][0].shape[0]
    per_row = sum(w * a.dtype.itemsize for a, w, _ in ops) + sum(c * jnp.dtype(d).itemsize for c, d in outs)
    tile = _row_tile(rows, per_row)
    n_in = len(tiled) + len(params)
    n_o = len(outs)

    def body(*refs):
        vals = [r[...] for r in refs[:n_in]]
        t_out, a_out = fn(*vals)
        for r, v in zip(refs[n_in:n_in + n_o], t_out):
            r[...] = v.astype(r.dtype)
        first = pl.program_id(0) == 0
        for r, v in zip(refs[n_in + n_o:], a_out):
            _acc_store(r, v.astype(F32), first)

    def tiled_spec(width, blk):
        return pl.BlockSpec((tile, width), lambda i: (i, blk))

    in_specs = [tiled_spec(w, blk) for _, w, blk in ops]
    in_specs += [pl.BlockSpec(p.shape, lambda i: (0, 0)) for p in params]
    out_specs = [tiled_spec(c, 0) for c, _ in outs]
    out_specs += [pl.BlockSpec(s, lambda i: (0, 0)) for s in accs]
    out_shape = [jax.ShapeDtypeStruct((rows, c), d) for c, d in outs]
    out_shape += [jax.ShapeDtypeStruct(s, F32) for s in accs]
    return pl.pallas_call(
        body, grid=(rows // tile,), in_specs=in_specs, out_specs=out_specs, out_shape=out_shape,
        compiler_params=_cparams("arbitrary"), name=name,
    )(*[a for a, _, _ in ops], *params)


def _to_f32(vals):
    return [v.astype(F32) for v in vals]


def _row_fwd(f, tiled, params, out_dtypes, name):
    ops = [_operand(e) for e in tiled]
    rows = ops[0][0].shape[0]
    shapes = jax.eval_shape(f, *[jax.ShapeDtypeStruct((rows, w), F32) for _, w, _ in ops],
                            *[jax.ShapeDtypeStruct(p.shape, F32) for p in params])
    outs = [(s.shape[1], d) for s, d in zip(shapes, out_dtypes)]
    return _rowwise_call(lambda *v: (f(*_to_f32(v)), ()), tiled, params, outs, [], name)


def _row_bwd(f, tiled, params, gs, d_dtypes, name, bwd=None, add=None):
    n_t, n_g = len(tiled), len(gs)
    adds = sorted((add or {}).items())
    n_a = len(adds)

    def fn(*vals):
        vals = _to_f32(vals)
        prim = vals[:n_t] + vals[n_t + n_g + n_a:]
        g = tuple(vals[n_t:n_t + n_g])
        if bwd is not None:
            d_t, d_p = bwd(*prim, *g)
        else:
            _, vjp = jax.vjp(f, *prim)
            cts = vjp(g)
            d_t, d_p = cts[:n_t], cts[n_t:]
        d_t = list(d_t)
        for (idx, _), extra in zip(adds, vals[n_t + n_g:n_t + n_g + n_a]):
            d_t[idx] = d_t[idx] + extra
        return tuple(d_t), tuple(d_p)

    outs = [(_operand(e)[1], d) for e, d in zip(tiled, d_dtypes)]
    accs = [p.shape for p in params]
    res = _rowwise_call(fn, list(tiled) + list(gs) + [a for _, a in adds], params, outs, accs, name)
    return res[:n_t], res[n_t:]


def _f_rmsnorm(x, g):
    return (x * lax.rsqrt(jnp.mean(x * x, axis=-1, keepdims=True) + EPS) * g,)


def _f_gated_norm(ys, xs, z, dsk, g):
    t = (ys + xs * dsk) * (z * jax.nn.sigmoid(z))
    return (t * lax.rsqrt(jnp.mean(t * t, axis=-1, keepdims=True) + EPS) * g,)


def _f_merge(gates, ys, ym):
    s = jax.nn.sigmoid(gates)
    return (s[:, :D_MODEL] * ys + s[:, D_MODEL:] * ym,)


def _b_merge(gates, ys, ym, d):
    s = jax.nn.sigmoid(gates)
    s1, s2 = s[:, :D_MODEL], s[:, D_MODEL:]
    d_gates = jnp.concatenate([d * ys * s1 * (1.0 - s1), d * ym * s2 * (1.0 - s2)], axis=1)
    return (d_gates, d * s1, d * s2), ()


def _loss_and_grad(y, target):
    def fn(yv, tv):
        d = yv - tv
        return (d * (1.0 / D_MODEL),), (jnp.sum(d * d, axis=0, keepdims=True) * (0.5 / D_MODEL),)

    dy, part = _rowwise_call(fn, [y, target], [], [(D_MODEL, F32)], [(1, D_MODEL)], "loss")
    return jnp.sum(part), dy


def _adam(w, g, m, v):
    def fn(wv, gv, mv, vv):
        m2 = ADAM_B1 * mv + (1.0 - ADAM_B1) * gv
        v2 = ADAM_B2 * vv + (1.0 - ADAM_B2) * (gv * gv)
        m_hat = m2 / (1.0 - ADAM_B1 ** ADAM_STEP)
        v_hat = v2 / (1.0 - ADAM_B2 ** ADAM_STEP)
        delta = -ADAM_LR * (m_hat / (jnp.sqrt(v_hat) + ADAM_EPS) + ADAM_WD * wv)
        return (delta, m2, v2), ()

    c = w.shape[1]
    return _rowwise_call(fn, [w, g, m, v], [], [(c, F32)] * 3, [], "adamw")


def _sum_blocks(blocks):
    _, rows, c = blocks.shape
    tile = _row_tile(rows, N_DEV * c * blocks.dtype.itemsize + c * 4)

    def body(b_ref, o_ref):
        acc = b_ref[0].astype(F32)
        for i in range(1, N_DEV):
            acc = acc + b_ref[i].astype(F32)
        o_ref[...] = acc

    return pl.pallas_call(
        body, grid=(rows // tile,), in_specs=[pl.BlockSpec((N_DEV, tile, c), lambda i: (0, i, 0))],
        out_specs=pl.BlockSpec((tile, c), lambda i: (i, 0)), out_shape=jax.ShapeDtypeStruct((rows, c), F32),
        compiler_params=_cparams("arbitrary"), name="sum_blocks",
    )(blocks)


MM_TP = 1024


def _mm(a, b, ta=False, tb=False, out_dtype=F32, alpha=1.0, res=None, b_rows=None, norm_bwd=None):
    r_dim, p_dim = a.shape if ta else a.shape[::-1]
    b_row0, b_nrows = (0, b.shape[0]) if b_rows is None else b_rows
    r2, q_dim = (b.shape[1], b_nrows) if tb else (b_nrows, b.shape[1])
    assert r_dim == r2, (a.shape, b.shape, ta, tb)
    tp = _pick_tile(p_dim, 512 if (res is not None or norm_bwd is not None) else MM_TP, LANE)
    if tp < 512 < p_dim:
        tp = _pick_tile(p_dim, 1536, LANE)
    tq = _pick_tile(q_dim, 1536, LANE)
    tr = _pick_tile(r_dim, 1536, LANE)
    nr = r_dim // tr
    dims = (((0 if ta else 1,), (1 if tb else 0,)), ((), ()))
    has_res = res is not None
    n_nb = 0 if norm_bwd is None else 3
    assert norm_bwd is None or tq == q_dim

    def body(*refs):
        a_ref, b_ref = refs[:2]
        res_ref = refs[2] if has_res else None
        n_in = 2 + has_res + n_nb
        o_ref = refs[n_in]

        def finish(val):
            if alpha != 1.0:
                val = val * alpha
            if has_res:
                val = val + res_ref[...].astype(F32)
            if norm_bwd is not None:
                x_ref, g_ref, add_ref = refs[2 + has_res:n_in]
                x = x_ref[...]
                r = lax.rsqrt(jnp.mean(x * x, axis=-1, keepdims=True) + EPS)
                gy = val * g_ref[...]
                dot = jnp.sum(gy * x, axis=-1, keepdims=True)
                _acc_store(refs[n_in + 1], jnp.sum(val * x * r, axis=0, keepdims=True), pl.program_id(1) == 0)
                val = gy * r - x * (dot * (r * r * r) * (1.0 / q_dim)) + add_ref[...]
            o_ref[...] = val.astype(o_ref.dtype)

        part = lax.dot_general(a_ref[...].astype(BF16), b_ref[...].astype(BF16), dims, preferred_element_type=F32)
        if nr == 1:
            finish(part)
        else:
            acc_ref = refs[-1]
            k = pl.program_id(2)
            _acc_store(acc_ref, part, k == 0)

            @pl.when(k == nr - 1)
            def _():
                finish(acc_ref[...])

    a_spec = pl.BlockSpec((tr, tp), lambda j, i, k: (k, i)) if ta else pl.BlockSpec((tp, tr), lambda j, i, k: (i, k))
    assert b_row0 % (tq if tb else tr) == 0
    b0 = b_row0 // (tq if tb else tr)
    b_spec = pl.BlockSpec((tq, tr), lambda j, i, k: (j + b0, k)) if tb else pl.BlockSpec((tr, tq), lambda j, i, k: (k + b0, j))
    o_spec = pl.BlockSpec((tp, tq), lambda j, i, k: (i, j))
    row_spec = pl.BlockSpec((1, tq), lambda j, i, k: (0, 0))
    in_specs = [a_spec, b_spec] + ([o_spec] if has_res else []) + ([o_spec, row_spec, o_spec] if n_nb else [])
    out = pl.pallas_call(
        body, grid=(q_dim // tq, p_dim // tp, nr), in_specs=in_specs,
        out_specs=[o_spec] + ([row_spec] if n_nb else []),
        out_shape=[jax.ShapeDtypeStruct((p_dim, q_dim), out_dtype)] + ([jax.ShapeDtypeStruct((1, q_dim), F32)] if n_nb else []),
        scratch_shapes=[pltpu.VMEM((tp, tq), F32)] if nr > 1 else [],
        compiler_params=_cparams("arbitrary", "arbitrary", "arbitrary"),
        name=f"mm_{'t' if ta else 'n'}{'t' if tb else 'n'}_{p_dim}x{r_dim}x{q_dim}" + ("_norm_bwd" if n_nb else ""),
    )(*([a, b] + ([res] if has_res else []) + (list(norm_bwd) if n_nb else [])))
    return out if n_nb else out[0]


MERGE_TP = 512


def _merge_out_call(proj, y_ssd, y_mla, w_o, h):
    s = h.shape[0]
    tp = min(MERGE_TP, s)

    def body(g_ref, ys_ref, ym_ref, w_ref, h_ref, o_ref, mg_ref):
        mg = _f_merge(g_ref[...], ys_ref[...], ym_ref[...])[0].astype(BF16)
        mg_ref[...] = mg
        o_ref[...] = h_ref[...] + jnp.dot(mg, w_ref[...], preferred_element_type=F32)

    rows = pl.BlockSpec((tp, D_MODEL), lambda i: (i, 0))
    return pl.pallas_call(
        body, grid=(s // tp,),
        in_specs=[pl.BlockSpec((tp, 2 * D_MODEL), lambda i: (i, PROJ_GATES // (2 * D_MODEL))), rows, rows,
                  pl.BlockSpec((D_MODEL, D_MODEL), lambda i: (0, 0)), rows],
        out_specs=[rows, rows],
        out_shape=[jax.ShapeDtypeStruct((s, D_MODEL), F32), jax.ShapeDtypeStruct((s, D_MODEL), BF16)],
        compiler_params=_cparams("arbitrary"), name="merge_out",
    )(proj, y_ssd, y_mla, w_o, h)


def _merge_out_bwd_call(dh, w_o, proj, y_ssd, y_mla):
    s = dh.shape[0]
    tp = min(MERGE_TP, s)

    def body(dh_ref, w_ref, g_ref, ys_ref, ym_ref, dg_ref, dys_ref, dym_ref):
        d_mg = _nt(dh_ref[...].astype(BF16), w_ref[...])
        (d_g, d_ys, d_ym), _ = _b_merge(g_ref[...], ys_ref[...], ym_ref[...], d_mg)
        dg_ref[...] = d_g.astype(BF16)
        dys_ref[...] = d_ys.astype(BF16)
        dym_ref[...] = d_ym.astype(BF16)

    rows = pl.BlockSpec((tp, D_MODEL), lambda i: (i, 0))
    wide = pl.BlockSpec((tp, 2 * D_MODEL), lambda i: (i, 0))
    return pl.pallas_call(
        body, grid=(s // tp,),
        in_specs=[rows, pl.BlockSpec((D_MODEL, D_MODEL), lambda i: (0, 0)),
                  pl.BlockSpec((tp, 2 * D_MODEL), lambda i: (i, PROJ_GATES // (2 * D_MODEL))), rows, rows],
        out_specs=[wide, rows, rows],
        out_shape=[jax.ShapeDtypeStruct((s, 2 * D_MODEL), BF16), jax.ShapeDtypeStruct((s, D_MODEL), BF16),
                   jax.ShapeDtypeStruct((s, D_MODEL), BF16)],
        compiler_params=_cparams("arbitrary"), name="merge_out_bwd",
    )(dh, w_o, proj, y_ssd, y_mla)


def _attn_out_bwd_call(d_y, w_out, o_rows):
    s = d_y.shape[0]
    tp = min(MERGE_TP, s)
    wide = MLA_H * VDIM

    def body(dy_ref, w_ref, o_ref, do_ref, delta_ref):
        d_o = _nt(dy_ref[...], w_ref[...]).astype(BF16)
        do_ref[...] = d_o
        col = lax.broadcasted_iota(jnp.int32, (wide, MLA_H), 0)
        head = lax.broadcasted_iota(jnp.int32, (wide, MLA_H), 1)
        per_head = _ones_where(lax.shift_right_logical(col, VDIM.bit_length() - 1) == head)
        delta_ref[...] = _dot_sel_r(d_o.astype(F32) * o_ref[...].astype(F32), per_head)

    rows = lambda c: pl.BlockSpec((tp, c), lambda i: (i, 0))
    return pl.pallas_call(
        body, grid=(s // tp,),
        in_specs=[rows(D_MODEL), pl.BlockSpec((wide, D_MODEL), lambda i: (0, 0)), rows(wide)],
        out_specs=[rows(wide), rows(MLA_H)],
        out_shape=[jax.ShapeDtypeStruct((s, wide), BF16), jax.ShapeDtypeStruct((s, MLA_H), F32)],
        compiler_params=_cparams("arbitrary"), name="attn_out_bwd",
    )(d_y, w_out, o_rows)


FFN_TP = 512
FFN_TQ = 1408


def _ffn_up_call(n, w13_t):
    s, d = n.shape
    tp = min(FFN_TP, s)
    up0 = D_FF // FFN_TQ

    def body(n_ref, wg_ref, wu_ref, act_ref, gate_ref, up_ref):
        a = n_ref[...]
        g = _nt(a, wg_ref[...])
        u = _nt(a, wu_ref[...])
        act_ref[...] = (g * jax.nn.sigmoid(g) * u).astype(BF16)
        gate_ref[...] = g.astype(BF16)
        up_ref[...] = u.astype(BF16)

    o_spec = pl.BlockSpec((tp, FFN_TQ), lambda j, i: (i, j))
    return pl.pallas_call(
        body, grid=(D_FF // FFN_TQ, s // tp),
        in_specs=[pl.BlockSpec((tp, d), lambda j, i: (i, 0)), pl.BlockSpec((FFN_TQ, d), lambda j, i: (j, 0)),
                  pl.BlockSpec((FFN_TQ, d), lambda j, i: (j + up0, 0))],
        out_specs=[o_spec] * 3, out_shape=[jax.ShapeDtypeStruct((s, D_FF), BF16)] * 3,
        compiler_params=_cparams("arbitrary", "arbitrary"), name="ffn_up_swiglu",
    )(n, w13_t, w13_t)


def _ffn_down_bwd_call(dh, w2, gate, up):
    s, d = dh.shape
    tp = min(FFN_TP, s)

    def body(dh_ref, w2_ref, gate_ref, up_ref, dg_ref, du_ref):
        d_act = 0.5 * _nt(dh_ref[...].astype(BF16), w2_ref[...])
        g, u = gate_ref[...].astype(F32), up_ref[...].astype(F32)
        sg = jax.nn.sigmoid(g)
        dg_ref[...] = (d_act * u * sg * (1.0 + g * (1.0 - sg))).astype(BF16)
        du_ref[...] = (d_act * g * sg).astype(BF16)

    o_spec = pl.BlockSpec((tp, FFN_TQ), lambda j, i: (i, j))
    return pl.pallas_call(
        body, grid=(D_FF // FFN_TQ, s // tp),
        in_specs=[pl.BlockSpec((tp, d), lambda j, i: (i, 0)), pl.BlockSpec((FFN_TQ, d), lambda j, i: (j, 0)), o_spec, o_spec],
        out_specs=[o_spec] * 2, out_shape=[jax.ShapeDtypeStruct((s, D_FF), BF16)] * 2,
        compiler_params=_cparams("arbitrary", "arbitrary"), name="ffn_down_bwd_swiglu",
    )(dh, w2, gate, up)


ATTN_SCALE = QK ** -0.5
LOG2E = 1.4426950408889634
ATTN_C = ATTN_SCALE * LOG2E


ATTN_HEADS = 2
ATTN_HEADS_FWD = 4


def _attn_tile(s):
    return min(512, s)


def _causal_keep(t, keys_on_rows=False):
    row = lax.broadcasted_iota(jnp.int32, (t, t), 0)
    col = lax.broadcasted_iota(jnp.int32, (t, t), 1)
    return row <= col if keys_on_rows else col <= row


def _nt(a, b):
    return lax.dot_general(a, b, (((1,), (1,)), ((), ())), preferred_element_type=F32)


def _rider_phases(rider, src_ref, out_ref, sems, first, last):
    @pl.when(first)
    def _():
        _peer_copies(src_ref, out_ref, sems, rider["gather"], "start")

    def finish():
        @pl.when(last)
        def _():
            _peer_copies(src_ref, out_ref, sems, rider["gather"], "finish")

    return finish


def _attn_fwd_call(q, k, v, rider=None):
    nh, s, _ = q.shape
    t = _attn_tile(s)
    nb = s // t
    hp = ATTN_HEADS_FWD
    n_r = 0 if rider is None else 1

    def body(*refs):
        q_ref, k_ref, v_ref = refs[:3]
        o_ref, lse_ref = refs[3 + n_r:5 + n_r]
        qi = pl.program_id(1)
        finish = None
        if rider is not None:
            h = pl.program_id(0)
            finish = _rider_phases(rider, refs[3:4], refs[5 + n_r], refs[6 + n_r:],
                                   jnp.logical_and(h == 0, qi == 0), jnp.logical_and(h == nh // hp - 1, qi == nb - 1))
        qs = [q_ref[i] for i in range(hp)]

        def block(kb, carries, diagonal, width=1):
            start = pl.multiple_of(kb * t, t)
            out = []
            for i, (m_prev, l_prev, acc) in enumerate(carries):
                sc = _nt(qs[i], k_ref[i, pl.ds(start, width * t), :])
                if diagonal:
                    sc = jnp.where(_causal_keep(t), sc, NEG)
                m_new = jnp.maximum(m_prev, jnp.max(sc, axis=-1, keepdims=True))
                p = jnp.exp2(sc * ATTN_C - m_new * ATTN_C)
                alpha = jnp.exp2((m_prev - m_new) * ATTN_C)
                l_new = alpha * l_prev + jnp.sum(p, axis=-1, keepdims=True)
                pv = jnp.dot(p.astype(BF16), v_ref[i, pl.ds(start, width * t), :], preferred_element_type=F32)
                out.append((m_new, l_new, alpha * acc + pv))
            return tuple(out)

        init = tuple((jnp.full((t, 1), NEG, F32), jnp.zeros((t, 1), F32), jnp.zeros((t, VDIM), F32)) for _ in range(hp))
        carries = lax.fori_loop(0, qi // 2, lambda j, c: block(2 * j, c, False, width=2), init)
        carries = lax.cond(qi % 2 == 1, lambda c: block(qi - 1, c, False), lambda c: c, carries)
        for i, (m, l, acc) in enumerate(block(qi, carries, True)):
            o_ref[i] = (acc / l).astype(o_ref.dtype)
            lse_ref[i] = m * ATTN_SCALE + jnp.log(l)
        if finish is not None:
            finish()

    qmap = lambda h, i: (h, i, 0)
    whole = lambda h, i: (h, 0, 0)
    return pl.pallas_call(
        body, grid=(nh // hp, nb),
        in_specs=[pl.BlockSpec((hp, t, QK), qmap), pl.BlockSpec((hp, s, QK), whole, pipeline_mode=pl.Buffered(buffer_count=1)),
                  pl.BlockSpec((hp, s, VDIM), whole, pipeline_mode=pl.Buffered(buffer_count=1))] + [HBM_SPEC] * n_r,
        out_specs=[pl.BlockSpec((hp, t, VDIM), qmap), pl.BlockSpec((hp, t, 1), qmap)] + [HBM_SPEC] * n_r,
        out_shape=[jax.ShapeDtypeStruct((nh, s, VDIM), BF16), jax.ShapeDtypeStruct((nh, s, 1), F32)] + ([rider["out"]] if n_r else []),
        scratch_shapes=_comm_scratch() if n_r else [],
        compiler_params=_cparams("arbitrary", "arbitrary"), name="attn_fwd_gather" if n_r else "attn_fwd",
    )(*([q, k, v] + (rider["srcs"] if n_r else [])))


def _attn_bwd_call(q, k, v, do, lse_t, delta_t, rider=None):
    nh, s, _ = q.shape
    t = _attn_tile(s)
    nb = s // t
    hp = ATTN_HEADS
    n_src = 0 if rider is None else len(rider["srcs"])
    n_r = 0 if rider is None else 1

    def body(*refs):
        q_ref, k_ref, v_ref, do_ref, lse_ref, delta_ref = refs[:6]
        dq_ref, dk_ref, dv_ref = refs[6 + n_src:9 + n_src]
        dk_sc, dv_sc = refs[9 + n_src + n_r:11 + n_src + n_r]
        kj = pl.program_id(1)
        finish = None
        if rider is not None:
            h = pl.program_id(0)
            finish = _rider_phases(rider, refs[6:6 + n_src], refs[9 + n_src], refs[11 + n_src + n_r:],
                                   jnp.logical_and(h == 0, kj == 0), jnp.logical_and(h == nh // hp - 1, kj == nb - 1))

        @pl.when(kj == 0)
        def _():
            dq_ref[...] = jnp.zeros_like(dq_ref)

        dk_sc[...] = jnp.zeros_like(dk_sc)
        dv_sc[...] = jnp.zeros_like(dv_sc)
        kblks = [k_ref[i] for i in range(hp)]
        vblks = [v_ref[i] for i in range(hp)]

        def block(qb, diagonal):
            start = pl.multiple_of(qb * t, t)
            for i in range(hp):
                qblk = q_ref[i, pl.ds(start, t), :]
                doblk = do_ref[i, pl.ds(start, t), :]
                sc = _nt(kblks[i], qblk)
                if diagonal:
                    sc = jnp.where(_causal_keep(t, keys_on_rows=True), sc, NEG)
                p = jnp.exp2(sc * ATTN_C - lse_ref[i, :, pl.ds(start, t)] * LOG2E)
                dv_sc[i] += jnp.dot(p.astype(BF16), doblk, preferred_element_type=F32)
                dp = _nt(vblks[i], doblk)
                ds = (p * (dp - delta_ref[i, :, pl.ds(start, t)])).astype(BF16)
                dk_sc[i] += jnp.dot(ds, qblk, preferred_element_type=F32)
                dq_ref[i, pl.ds(start, t), :] += lax.dot_general(ds, kblks[i], (((0,), (0,)), ((), ())), preferred_element_type=F32)

        block(kj, True)

        def rest(qb, carry):
            block(qb, False)
            return carry

        lax.fori_loop(kj + 1, nb, rest, 0)
        dk_ref[...] = (dk_sc[...] * ATTN_SCALE).astype(dk_ref.dtype)
        dv_ref[...] = dv_sc[...].astype(dv_ref.dtype)

        @pl.when(kj == nb - 1)
        def _():
            dq_ref[...] = dq_ref[...] * ATTN_SCALE

        if finish is not None:
            finish()

    kmap = lambda h, j: (h, j, 0)
    whole = lambda h, j: (h, 0, 0)
    once = pl.Buffered(buffer_count=1)
    return pl.pallas_call(
        body, grid=(nh // hp, nb),
        in_specs=[pl.BlockSpec((hp, s, QK), whole, pipeline_mode=once), pl.BlockSpec((hp, t, QK), kmap), pl.BlockSpec((hp, t, VDIM), kmap),
                  pl.BlockSpec((hp, s, VDIM), whole, pipeline_mode=once), pl.BlockSpec((hp, 1, s), whole, pipeline_mode=once),
                  pl.BlockSpec((hp, 1, s), whole, pipeline_mode=once)] + [HBM_SPEC] * n_src,
        out_specs=[pl.BlockSpec((hp, s, QK), whole, pipeline_mode=once), pl.BlockSpec((hp, t, QK), kmap),
                   pl.BlockSpec((hp, t, VDIM), kmap)] + [HBM_SPEC] * n_r,
        out_shape=[jax.ShapeDtypeStruct((nh, s, QK), F32), jax.ShapeDtypeStruct((nh, s, QK), F32),
                   jax.ShapeDtypeStruct((nh, s, VDIM), F32)] + ([rider["out"]] if n_r else []),
        scratch_shapes=[pltpu.VMEM((hp, t, QK), F32), pltpu.VMEM((hp, t, VDIM), F32)] + (_comm_scratch() if n_r else []),
        compiler_params=_cparams("arbitrary", "arbitrary"), name="attn_bwd_exchange" if n_r else "attn_bwd",
    )(*([q, k, v, do, lse_t, delta_t] + (rider["srcs"] if n_r else [])))


HEAD_COLS = NOPE + VDIM
HEADS_TILE = 256


def _swap_rope_halves(t, lane):
    half = ROPE // 2
    return jnp.where(lane < half, pltpu.roll(t, LANE - half, 1), pltpu.roll(t, half, 1))


def _head_fwd(n, p, gain, cs, sn, lane):
    r = lax.rsqrt((jnp.sum(n * n, axis=-1, keepdims=True) + jnp.sum(p * p, axis=-1, keepdims=True)) * (1.0 / QK) + EPS)
    yp = p * r * gain[:, NOPE:]
    return n * r * gain[:, :NOPE], yp * cs + _swap_rope_halves(yp, lane) * sn


def _head_bwd(n, p, gain, cs, sn, lane, dzn, dzp):
    r = lax.rsqrt((jnp.sum(n * n, axis=-1, keepdims=True) + jnp.sum(p * p, axis=-1, keepdims=True)) * (1.0 / QK) + EPS)
    dyp = dzp * cs + _swap_rope_halves(dzp * sn, lane)
    gyn, gyp = dzn * gain[:, :NOPE], dyp * gain[:, NOPE:]
    dot = jnp.sum(gyn * n, axis=-1, keepdims=True) + jnp.sum(gyp * p, axis=-1, keepdims=True)
    coef = dot * (r * r * r) * (1.0 / QK)
    d_gn = jnp.sum(dzn * n * r, axis=0, keepdims=True)
    d_gp = jnp.sum(dyp * p * r, axis=0, keepdims=True)
    return gyn * r - n * coef, gyp * r - p * coef, d_gn, d_gp


def _heads_fwd_call(q, kv, proj, cs, sn, q_gain, k_gain):
    s = q.shape[0]
    t = min(HEADS_TILE, s)

    def body(q_ref, kv_ref, last_ref, cs_ref, sn_ref, qg_ref, kg_ref, qh_ref, kh_ref, vh_ref):
        lane = lax.broadcasted_iota(jnp.int32, (t, LANE), 1)
        cs_, sn_ = cs_ref[...], sn_ref[...]
        kp = jnp.where(lane < ROPE, last_ref[...], 0.0)
        for h in range(MLA_H):
            c0 = h * HEAD_COLS
            zn, zp = _head_fwd(q_ref[:, c0:c0 + NOPE], q_ref[:, c0 + NOPE:c0 + HEAD_COLS], qg_ref[...], cs_, sn_, lane)
            qh_ref[h, :, :NOPE] = zn.astype(BF16)
            qh_ref[h, :, NOPE:] = zp[:, :ROPE].astype(BF16)
            zn, zp = _head_fwd(kv_ref[:, c0:c0 + NOPE], kp, kg_ref[...], cs_, sn_, lane)
            kh_ref[h, :, :NOPE] = zn.astype(BF16)
            kh_ref[h, :, NOPE:] = zp[:, :ROPE].astype(BF16)
            vh_ref[h] = kv_ref[:, c0 + NOPE:c0 + HEAD_COLS].astype(BF16)

    rows = lambda i: (i, 0)
    whole = lambda i: (0, 0)
    heads = lambda i: (0, i, 0)
    wide = MLA_H * HEAD_COLS
    return pl.pallas_call(
        body, grid=(s // t,),
        in_specs=[pl.BlockSpec((t, wide), rows), pl.BlockSpec((t, wide), rows),
                  pl.BlockSpec((t, LANE), lambda i: (i, PROJ_LAST // LANE)),
                  pl.BlockSpec((t, LANE), rows), pl.BlockSpec((t, LANE), rows),
                  pl.BlockSpec((1, HEAD_COLS), whole), pl.BlockSpec((1, HEAD_COLS), whole)],
        out_specs=[pl.BlockSpec((MLA_H, t, QK), heads), pl.BlockSpec((MLA_H, t, QK), heads), pl.BlockSpec((MLA_H, t, VDIM), heads)],
        out_shape=[jax.ShapeDtypeStruct((MLA_H, s, QK), BF16), jax.ShapeDtypeStruct((MLA_H, s, QK), BF16),
                   jax.ShapeDtypeStruct((MLA_H, s, VDIM), BF16)],
        compiler_params=_cparams("arbitrary"), name="mla_heads_fwd",
    )(q, kv, proj, cs, sn, q_gain, k_gain)


def _heads_bwd_call(q, kv, proj, cs, sn, q_gain, k_gain, dqh, dkh, dvh):
    s = q.shape[0]
    t = min(HEADS_TILE, s)

    def body(q_ref, kv_ref, last_ref, cs_ref, sn_ref, qg_ref, kg_ref, dqh_ref, dkh_ref, dvh_ref,
             dq_ref, dkv_ref, dkr_ref, dqg_ref, dkg_ref):
        lane = lax.broadcasted_iota(jnp.int32, (t, LANE), 1)
        cs_, sn_ = cs_ref[...], sn_ref[...]
        kp = jnp.where(lane < ROPE, last_ref[...], 0.0)
        no_lanes = jnp.zeros((t, LANE - ROPE), F32)
        d_kp = jnp.zeros((t, LANE), F32)
        d_qg = [jnp.zeros((1, NOPE), F32), jnp.zeros((1, LANE), F32)]
        d_kg = [jnp.zeros((1, NOPE), F32), jnp.zeros((1, LANE), F32)]
        for h in range(MLA_H):
            c0 = h * HEAD_COLS
            dz = dqh_ref[h]
            dzp = jnp.concatenate([dz[:, NOPE:], no_lanes], axis=1)
            d_n, d_p, g_n, g_p = _head_bwd(q_ref[:, c0:c0 + NOPE], q_ref[:, c0 + NOPE:c0 + HEAD_COLS], qg_ref[...],
                                           cs_, sn_, lane, dz[:, :NOPE], dzp)
            dq_ref[:, c0:c0 + NOPE] = d_n.astype(dq_ref.dtype)
            dq_ref[:, c0 + NOPE:c0 + HEAD_COLS] = d_p.astype(dq_ref.dtype)
            d_qg = [d_qg[0] + g_n, d_qg[1] + g_p]
            dz = dkh_ref[h]
            dzp = jnp.concatenate([dz[:, NOPE:], no_lanes], axis=1)
            d_n, d_p, g_n, g_p = _head_bwd(kv_ref[:, c0:c0 + NOPE], kp, kg_ref[...], cs_, sn_, lane, dz[:, :NOPE], dzp)
            dkv_ref[:, c0:c0 + NOPE] = d_n.astype(dkv_ref.dtype)
            dkv_ref[:, c0 + NOPE:c0 + HEAD_COLS] = dvh_ref[h].astype(dkv_ref.dtype)
            d_kp = d_kp + d_p
            d_kg = [d_kg[0] + g_n, d_kg[1] + g_p]
        dkr_ref[...] = d_kp
        first = pl.program_id(0) == 0
        _acc_store(dqg_ref.at[:, pl.ds(0, NOPE)], d_qg[0], first)
        _acc_store(dqg_ref.at[:, pl.ds(NOPE, LANE)], d_qg[1], first)
        _acc_store(dkg_ref.at[:, pl.ds(0, NOPE)], d_kg[0], first)
        _acc_store(dkg_ref.at[:, pl.ds(NOPE, LANE)], d_kg[1], first)

    rows = lambda i: (i, 0)
    whole = lambda i: (0, 0)
    heads = lambda i: (0, i, 0)
    wide = MLA_H * HEAD_COLS
    return pl.pallas_call(
        body, grid=(s // t,),
        in_specs=[pl.BlockSpec((t, wide), rows), pl.BlockSpec((t, wide), rows),
                  pl.BlockSpec((t, LANE), lambda i: (i, PROJ_LAST // LANE)),
                  pl.BlockSpec((t, LANE), rows), pl.BlockSpec((t, LANE), rows),
                  pl.BlockSpec((1, HEAD_COLS), whole), pl.BlockSpec((1, HEAD_COLS), whole),
                  pl.BlockSpec((MLA_H, t, QK), heads), pl.BlockSpec((MLA_H, t, QK), heads), pl.BlockSpec((MLA_H, t, VDIM), heads)],
        out_specs=[pl.BlockSpec((t, wide), rows), pl.BlockSpec((t, wide), rows), pl.BlockSpec((t, LANE), rows),
                   pl.BlockSpec((1, HEAD_COLS), whole), pl.BlockSpec((1, HEAD_COLS), whole)],
        out_shape=[jax.ShapeDtypeStruct((s, wide), BF16), jax.ShapeDtypeStruct((s, wide), BF16), jax.ShapeDtypeStruct((s, LANE), F32),
                   jax.ShapeDtypeStruct((1, HEAD_COLS), F32), jax.ShapeDtypeStruct((1, HEAD_COLS), F32)],
        compiler_params=_cparams("arbitrary"), name="mla_heads_bwd",
    )(q, kv, proj, cs, sn, q_gain, k_gain, dqh, dkh, dvh)


CONV_TC = 512
HALO = 8


def _conv_tiles(s):
    return min(512, s)


def _conv_fwd_call(x, col0, w, b):
    s = x.shape[0]
    ts = _conv_tiles(s)
    hb = ts // HALO
    c0 = col0 // CONV_TC
    assert col0 % CONV_TC == 0

    def body(x_ref, prev_ref, w_ref, b_ref, y_ref, buf):
        si = pl.program_id(1)
        buf[0:HALO, :] = jnp.where(si > 0, prev_ref[...], 0.0)
        buf[HALO:, :] = x_ref[...]
        acc = jnp.broadcast_to(b_ref[...], (ts, CONV_TC))
        for k in range(CONV_K):
            acc = acc + w_ref[k:k + 1, :] * buf[pl.ds(HALO - (CONV_K - 1) + k, ts), :]
        y_ref[...] = acc * jax.nn.sigmoid(acc)

    return pl.pallas_call(
        body, grid=(CONV_DIM // CONV_TC, s // ts),
        in_specs=[pl.BlockSpec((ts, CONV_TC), lambda ci, si: (si, ci + c0)),
                  pl.BlockSpec((HALO, CONV_TC), lambda ci, si: (jnp.maximum(si * hb - 1, 0), ci + c0)),
                  pl.BlockSpec((CONV_K, CONV_TC), lambda ci, si: (0, ci)),
                  pl.BlockSpec((1, CONV_TC), lambda ci, si: (0, ci))],
        out_specs=pl.BlockSpec((ts, CONV_TC), lambda ci, si: (si, ci)),
        out_shape=jax.ShapeDtypeStruct((s, CONV_DIM), F32),
        scratch_shapes=[pltpu.VMEM((ts + HALO, CONV_TC), F32)],
        compiler_params=_cparams("arbitrary", "arbitrary"), name="conv_fwd",
    )(x, x, w, b)


def _conv_bwd_call(x, col0, w, b, dy):
    s = x.shape[0]
    ts = _conv_tiles(s)
    hb = ts // HALO
    ns = s // ts
    last_halo = s // HALO - 1
    c0 = col0 // CONV_TC

    def body(x_ref, prev_ref, next_ref, dy_ref, dyn_ref, w_ref, b_ref, dx_ref, dw_ref, db_ref, xbuf, dbuf):
        si = pl.program_id(1)
        xbuf[0:HALO, :] = jnp.where(si > 0, prev_ref[...], 0.0)
        xbuf[HALO:HALO + ts, :] = x_ref[...]
        xbuf[HALO + ts:, :] = next_ref[...]
        pre = jnp.broadcast_to(b_ref[...], (ts + HALO, CONV_TC))
        for k in range(CONV_K):
            pre = pre + w_ref[k:k + 1, :] * xbuf[pl.ds(HALO - (CONV_K - 1) + k, ts + HALO), :]
        sg = jax.nn.sigmoid(pre)
        dsilu = sg * (1.0 + pre * (1.0 - sg))
        dbuf[0:ts, :] = dy_ref[...] * dsilu[0:ts]
        dbuf[ts:, :] = jnp.where(si < ns - 1, dyn_ref[...] * dsilu[ts:], 0.0)
        dx = jnp.zeros((ts, CONV_TC), F32)
        for k in range(CONV_K):
            dx = dx + w_ref[k:k + 1, :] * dbuf[pl.ds(CONV_K - 1 - k, ts), :]
        dx_ref[...] = dx.astype(dx_ref.dtype)
        dpre = dbuf[0:ts, :]
        first = si == 0
        _acc_store(db_ref, jnp.sum(dpre, axis=0, keepdims=True), first)
        for k in range(CONV_K):
            dw_k = jnp.sum(dpre * xbuf[pl.ds(HALO - (CONV_K - 1) + k, ts), :], axis=0, keepdims=True)
            _acc_store(dw_ref.at[pl.ds(k, 1), :], dw_k, first)

    main = lambda ci, si: (si, ci)
    x_main = lambda ci, si: (si, ci + c0)
    x_prev = lambda ci, si: (jnp.maximum(si * hb - 1, 0), ci + c0)
    x_next = lambda ci, si: (jnp.minimum(si * hb + hb, last_halo), ci + c0)
    return pl.pallas_call(
        body, grid=(CONV_DIM // CONV_TC, ns),
        in_specs=[pl.BlockSpec((ts, CONV_TC), x_main), pl.BlockSpec((HALO, CONV_TC), x_prev), pl.BlockSpec((HALO, CONV_TC), x_next),
                  pl.BlockSpec((ts, CONV_TC), main),
                  pl.BlockSpec((HALO, CONV_TC), lambda ci, si: (jnp.minimum(si * hb + hb, last_halo), ci)),
                  pl.BlockSpec((CONV_K, CONV_TC), lambda ci, si: (0, ci)),
                  pl.BlockSpec((1, CONV_TC), lambda ci, si: (0, ci))],
        out_specs=[pl.BlockSpec((ts, CONV_TC), main),
                   pl.BlockSpec((CONV_K, CONV_TC), lambda ci, si: (0, ci)),
                   pl.BlockSpec((1, CONV_TC), lambda ci, si: (0, ci))],
        out_shape=[jax.ShapeDtypeStruct((s, CONV_DIM), BF16), jax.ShapeDtypeStruct((CONV_K, CONV_DIM), F32),
                   jax.ShapeDtypeStruct((1, CONV_DIM), F32)],
        scratch_shapes=[pltpu.VMEM((ts + 2 * HALO, CONV_TC), F32), pltpu.VMEM((ts + HALO, CONV_TC), F32)],
        compiler_params=_cparams("arbitrary", "arbitrary"), name="conv_bwd",
    )(x, x, x, dy, dy, w, b)


GW = SSD_HPG * SSD_P
B_COL = SSD_DI
C_COL = SSD_DI + SSD_G * SSD_N


def _ones_where(mask):
    return jnp.where(mask, 1.0, 0.0).astype(BF16)


def _split(v, passes):
    parts, rest = [], v
    for i in range(passes):
        part = rest.astype(BF16)
        parts.append(part)
        if i + 1 < passes:
            rest = rest - part.astype(F32)
    return parts


def _dot_sel_r(v, sel, passes=3):
    out = None
    for part in _split(v, passes):
        t = jnp.dot(part, sel, preferred_element_type=F32)
        out = t if out is None else out + t
    return out


def _dot_sel_l(sel, v, passes=3):
    out = None
    for part in _split(v, passes):
        t = jnp.dot(sel, part, preferred_element_type=F32)
        out = t if out is None else out + t
    return out


def _ssd_consts():
    r = lax.broadcasted_iota(jnp.int32, (SSD_L, SSD_L), 0)
    c = lax.broadcasted_iota(jnp.int32, (SSD_L, SSD_L), 1)
    tril = r >= c
    triu = c >= r
    shift = SSD_P.bit_length() - 1
    eh = lax.broadcasted_iota(jnp.int32, (SSD_H, SSD_DI), 0)
    ej = lax.broadcasted_iota(jnp.int32, (SSD_H, SSD_DI), 1)
    expand = _ones_where(lax.shift_right_logical(ej, shift) == eh)
    rj = lax.broadcasted_iota(jnp.int32, (SSD_DI, SSD_H), 0)
    rh = lax.broadcasted_iota(jnp.int32, (SSD_DI, SSD_H), 1)
    reduce_ = _ones_where(lax.shift_right_logical(rj, shift) == rh)
    lane = lax.broadcasted_iota(jnp.int32, (SSD_L, LANE), 1)
    return tril, triu, expand, reduce_, lane < SSD_P


def _ssd_decays(dt, dt_t, a, a_t, tril, triu, expand):
    dta = dt * a
    acum = _dot_sel_l(_ones_where(tril), dta)
    acum_t = _dot_sel_r(dt_t * a_t, _ones_where(triu))
    dta_e = _dot_sel_r(dta, expand)
    acum_e = _dot_sel_r(acum, expand)
    last_e = jnp.sum(dta_e, axis=0, keepdims=True)
    return acum, acum_t, acum_e, last_e


def _head_decay(acum, acum_t, h, tril):
    seg = acum[:, h:h + 1] - acum_t[h:h + 1, :]
    return jnp.exp(jnp.where(tril, seg, NEG))


def _ssd_fwd_call(xbc, dt, a):
    s = xbc.shape[0]
    nc = s // SSD_L
    dt_t = dt.T
    a_t = a.T

    def body(xbc_ref, dt_ref, dtt_ref, a_ref, at_ref, y_ref, st_ref, s_sc):
        ci = pl.program_id(0)

        @pl.when(ci == 0)
        def _():
            s_sc[...] = jnp.zeros_like(s_sc)

        st_ref[0] = s_sc[...]
        tril, triu, expand, _, low_half = _ssd_consts()
        acum, acum_t, acum_e, last_e = _ssd_decays(dt_ref[...], dtt_ref[...], a_ref[...], at_ref[...], tril, triu, expand)
        dt_e = _dot_sel_r(dt_ref[...], expand, passes=2)
        xdt = xbc_ref[:, :SSD_DI] * dt_e
        xdt_b = xdt.astype(BF16)
        xw_b = (xdt * jnp.exp(last_e - acum_e)).astype(BF16)
        ea_e = jnp.exp(acum_e)
        el_e = jnp.exp(last_e)
        for g in range(SSD_G):
            gs = slice(g * GW, (g + 1) * GW)
            bg = xbc_ref[:, B_COL + g * SSD_N:B_COL + (g + 1) * SSD_N]
            cg_b = xbc_ref[:, C_COL + g * SSD_N:C_COL + (g + 1) * SSD_N].astype(BF16)
            bg_b = bg.astype(BF16)
            cb = _nt(cg_b, bg_b)
            st = s_sc[:, gs]
            y_off = jnp.dot(cg_b, st.astype(BF16), preferred_element_type=F32) * ea_e[:, gs]
            for pr in range(SSD_HPG // 2):
                ls = slice(g * GW + pr * LANE, g * GW + (pr + 1) * LANE)
                xp = xdt_b[:, ls]
                yd = []
                for half in range(2):
                    h = g * SSD_HPG + pr * 2 + half
                    m = (cb * _head_decay(acum, acum_t, h, tril)).astype(BF16)
                    yd.append(jnp.dot(m, xp, preferred_element_type=F32))
                y_ref[:, ls] = jnp.where(low_half, yd[0], yd[1]) + y_off[:, pr * LANE:(pr + 1) * LANE]
            s_sc[:, gs] = st * el_e[:, gs] + jnp.dot(bg.T.astype(BF16), xw_b[:, gs], preferred_element_type=F32)

    row = lambda i: (i, 0)
    return pl.pallas_call(
        body, grid=(nc,),
        in_specs=[pl.BlockSpec((SSD_L, CONV_DIM), row), pl.BlockSpec((SSD_L, SSD_H), row),
                  pl.BlockSpec((SSD_H, SSD_L), lambda i: (0, i)), pl.BlockSpec((1, SSD_H), lambda i: (0, 0)),
                  pl.BlockSpec((SSD_H, 1), lambda i: (0, 0))],
        out_specs=[pl.BlockSpec((SSD_L, SSD_DI), row), pl.BlockSpec((1, SSD_N, SSD_DI), lambda i: (i, 0, 0))],
        out_shape=[jax.ShapeDtypeStruct((s, SSD_DI), F32), jax.ShapeDtypeStruct((nc, SSD_N, SSD_DI), F32)],
        scratch_shapes=[pltpu.VMEM((SSD_N, SSD_DI), F32)],
        compiler_params=_cparams("arbitrary"), name="ssd_fwd",
    )(xbc, dt, dt_t, a, a_t)


def _ssd_bwd_call(xbc, dt, a, states, dy, dx_extra):
    s = xbc.shape[0]
    nc = s // SSD_L
    dt_t = dt.T
    a_t = a.T

    def body(xbc_ref, dt_ref, dtt_ref, a_ref, at_ref, st_ref, dy_ref, dxe_ref,
             dxbc_ref, ddt_ref, da_ref, ds_sc, yf_sc, dxd_sc, dxw_sc):
        i = pl.program_id(0)

        @pl.when(i == 0)
        def _():
            ds_sc[...] = jnp.zeros_like(ds_sc)

        tril, triu, expand, reduce_, low_half = _ssd_consts()
        dt = dt_ref[...]
        a_row = a_ref[...]
        acum, acum_t, acum_e, last_e = _ssd_decays(dt, dtt_ref[...], a_row, at_ref[...], tril, triu, expand)
        dt_e = _dot_sel_r(dt, expand, passes=2)
        x = xbc_ref[:, :SSD_DI]
        xdt = x * dt_e
        xdt_b = xdt.astype(BF16)
        w_e = jnp.exp(last_e - acum_e)
        xw_b = (xdt * w_e).astype(BF16)
        ea_e = jnp.exp(acum_e)
        el_e = jnp.exp(last_e)
        dy = dy_ref[...]
        dy_b = dy.astype(BF16)
        s_prev = st_ref[0]
        ds_new = ds_sc[...]
        ds_new_b = ds_new.astype(BF16)
        triu_b = _ones_where(triu)
        strict_tril = jnp.logical_not(triu)
        head_ids = lax.broadcasted_iota(jnp.int32, (1, SSD_H), 1)
        d_dta_diag = jnp.zeros((SSD_L, SSD_H), F32)
        for g in range(SSD_G):
            gs = slice(g * GW, (g + 1) * GW)
            bs_ = slice(B_COL + g * SSD_N, B_COL + (g + 1) * SSD_N)
            cs_ = slice(C_COL + g * SSD_N, C_COL + (g + 1) * SSD_N)
            bg = xbc_ref[:, bs_]
            cg = xbc_ref[:, cs_]
            bg_b, cg_b = bg.astype(BF16), cg.astype(BF16)
            st_b = s_prev[:, gs].astype(BF16)
            y_off = jnp.dot(cg_b, st_b, preferred_element_type=F32) * ea_e[:, gs]
            yf_sc[:, gs] = y_off
            dz_b = (dy[:, gs] * ea_e[:, gs]).astype(BF16)
            d_c = _nt(dz_b, st_b)
            ds_prev = ds_new[:, gs] * el_e[:, gs] + jnp.dot(cg.T.astype(BF16), dz_b, preferred_element_type=F32)
            dxw_sc[:, gs] = jnp.dot(bg_b, ds_new_b[:, gs], preferred_element_type=F32)
            d_b = _nt(xw_b[:, gs], ds_new_b[:, gs])
            cb = _nt(cg_b, bg_b)
            d_g = jnp.zeros((SSD_L, SSD_L), F32)
            for pr in range(SSD_HPG // 2):
                ls = slice(g * GW + pr * LANE, g * GW + (pr + 1) * LANE)
                xp = xdt_b[:, ls]
                dyp = dy[:, ls]
                dyp_b = dy_b[:, ls]
                dxd = []
                for half in range(2):
                    h = g * SSD_HPG + pr * 2 + half
                    dec = _head_decay(acum, acum_t, h, tril)
                    m = cb * dec
                    dxd.append(jnp.dot(m.T.astype(BF16), dyp_b, preferred_element_type=F32))
                    mine = low_half if half == 0 else jnp.logical_not(low_half)
                    d_m = _nt(jnp.where(mine, dyp, 0.0).astype(BF16), xp)
                    d_g = d_g + d_m * dec
                    below = jnp.dot(triu_b, (d_m * m).astype(BF16), preferred_element_type=F32)
                    col = jnp.sum(jnp.where(strict_tril, below, 0.0), axis=1, keepdims=True)
                    d_dta_diag = d_dta_diag + col * jnp.where(head_ids == h, 1.0, 0.0)
                dxd_sc[:, ls] = jnp.where(low_half, dxd[0], dxd[1])
            d_g_b = d_g.astype(BF16)
            dxbc_ref[:, cs_] = d_c + jnp.dot(d_g_b, bg_b, preferred_element_type=F32)
            dxbc_ref[:, bs_] = d_b + jnp.dot(d_g.T.astype(BF16), cg_b, preferred_element_type=F32)
            ds_sc[:, gs] = ds_prev
        dxw = dxw_sc[...]
        dxd = dxd_sc[...]
        dw_e = xdt * dxw * w_e
        d_tot_e = jnp.sum(ds_new * s_prev, axis=0, keepdims=True) * el_e
        d_state_e = (_dot_sel_l(triu_b, dy * yf_sc[...], passes=2)
                     + _dot_sel_l(_ones_where(strict_tril), dw_e, passes=2) + d_tot_e)
        dxdt = dxd + dxw * w_e
        dxbc_ref[:, :SSD_DI] = dxdt * dt_e + dxe_ref[...]
        a_e = _dot_sel_r(jnp.broadcast_to(a_row, (8, SSD_H)), expand)[0:1]
        ddt_ref[...] = _dot_sel_r(d_state_e * a_e + dxdt * x, reduce_, passes=2) + d_dta_diag * a_row
        d_a_e = jnp.sum(d_state_e * dt_e, axis=0, keepdims=True)
        d_a = _dot_sel_r(jnp.broadcast_to(d_a_e, (8, SSD_DI)), reduce_)[0:1] + jnp.sum(d_dta_diag * dt, axis=0, keepdims=True)
        _acc_store(da_ref, d_a, i == 0)

    rev = lambda i: (nc - 1 - i, 0)
    return pl.pallas_call(
        body, grid=(nc,),
        in_specs=[pl.BlockSpec((SSD_L, CONV_DIM), rev), pl.BlockSpec((SSD_L, SSD_H), rev),
                  pl.BlockSpec((SSD_H, SSD_L), lambda i: (0, nc - 1 - i)), pl.BlockSpec((1, SSD_H), lambda i: (0, 0)),
                  pl.BlockSpec((SSD_H, 1), lambda i: (0, 0)),
                  pl.BlockSpec((1, SSD_N, SSD_DI), lambda i: (nc - 1 - i, 0, 0)),
                  pl.BlockSpec((SSD_L, SSD_DI), rev), pl.BlockSpec((SSD_L, SSD_DI), rev)],
        out_specs=[pl.BlockSpec((SSD_L, CONV_DIM), rev), pl.BlockSpec((SSD_L, SSD_H), rev),
                   pl.BlockSpec((1, SSD_H), lambda i: (0, 0))],
        out_shape=[jax.ShapeDtypeStruct((s, CONV_DIM), F32), jax.ShapeDtypeStruct((s, SSD_H), F32),
                   jax.ShapeDtypeStruct((1, SSD_H), F32)],
        scratch_shapes=[pltpu.VMEM((SSD_N, SSD_DI), F32), pltpu.VMEM((SSD_L, SSD_DI), F32),
                        pltpu.VMEM((SSD_L, SSD_DI), F32), pltpu.VMEM((SSD_L, SSD_DI), F32)],
        compiler_params=_cparams("arbitrary"), name="ssd_bwd",
    )(xbc, dt, dt_t, a, a_t, states, dy, dx_extra)


HBM_SPEC = pl.BlockSpec(memory_space=pltpu.HBM)
N_PEERS = N_DEV - 1


def _flip(v, f):
    return 1 - v if f else v


def _all_gather(shard):
    rows, c = shard.shape

    def body(x_ref, out_ref, send_sems, recv_sems, local_sem):
        x, y, cc = lax.axis_index("x"), lax.axis_index("y"), lax.axis_index("c")
        me, sibling = (x, y, cc), (x, y, 1 - cc)
        chips = [(1 - x, y), (x, 1 - y), (1 - x, 1 - y)]

        def slot(px, py, pc):
            return out_ref.at[4 * px + 2 * py + pc]

        def copy(k, block, to, src=None):
            return pltpu.make_async_remote_copy(
                src_ref=slot(*block) if src is None else src, dst_ref=slot(*block),
                send_sem=send_sems.at[k], recv_sem=recv_sems.at[k],
                device_id=to, device_id_type=pl.DeviceIdType.MESH)

        mine = pltpu.make_async_copy(x_ref, slot(*me), local_sem)
        mine.start()
        first = [copy(0, me, sibling, src=x_ref)]
        first += [copy(1 + j, me, (*chip, cc), src=x_ref) for j, chip in enumerate(chips)]
        for cp in first:
            cp.start()
        passed = [copy(4 + j, (*chip, cc), sibling) for j, chip in enumerate(chips)]
        for j, chip in enumerate(chips):
            copy(1 + j, (*chip, cc), me).wait_recv()
            passed[j].start()
        copy(0, sibling, me).wait_recv()
        for j, chip in enumerate(chips):
            copy(4 + j, (*chip, 1 - cc), me).wait_recv()
        for cp in first + passed:
            cp.wait_send()
        mine.wait()

    return pl.pallas_call(
        body, out_shape=jax.ShapeDtypeStruct((N_DEV, rows, c), shard.dtype),
        in_specs=[HBM_SPEC], out_specs=HBM_SPEC,
        scratch_shapes=[pltpu.SemaphoreType.DMA((N_PEERS,)), pltpu.SemaphoreType.DMA((N_PEERS,)), pltpu.SemaphoreType.DMA(())],
        name="all_gather",
    )(shard)


def _peer_copies(src_refs, out_ref, sems, gather, phase):
    send_sems, recv_sems, local_sem = sems
    x, y, cc = lax.axis_index("x"), lax.axis_index("y"), lax.axis_index("c")
    me = 4 * x + 2 * y + cc

    def pieces(block, slot):
        if gather:
            return [(src_refs[0], out_ref.at[slot])]
        out, r0 = [], 0
        for src in src_refs:
            out.append((src.at[block], out_ref.at[slot, pl.ds(r0, src.shape[1])]))
            r0 += src.shape[1]
        assert r0 == out_ref.shape[1], (r0, out_ref.shape)
        return out

    if phase == "start":
        for src, dst in pieces(me, me):
            pltpu.make_async_copy(src, dst, local_sem).start()
    for k in range(1, N_DEV):
        px, py, pc = _flip(x, k & 4), _flip(y, k & 2), _flip(cc, k & 1)
        peer = 4 * px + 2 * py + pc
        to_peer = dict(send_sem=send_sems.at[k - 1], recv_sem=recv_sems.at[k - 1],
                       device_id=(px, py, pc), device_id_type=pl.DeviceIdType.MESH)
        if phase == "start":
            for src, dst in pieces(peer, me):
                pltpu.make_async_remote_copy(src_ref=src, dst_ref=dst, **to_peer).start()
        else:
            whole = pltpu.make_async_remote_copy(src_ref=out_ref.at[peer], dst_ref=out_ref.at[peer], **to_peer)
            whole.wait_recv()
            whole.wait_send()
    if phase != "start":
        pltpu.make_async_copy(out_ref.at[me], out_ref.at[me], local_sem).wait()


def _comm_scratch():
    return [pltpu.SemaphoreType.DMA((N_PEERS,)), pltpu.SemaphoreType.DMA((N_PEERS,)), pltpu.SemaphoreType.DMA(())]


def _gather_rider(shard):
    return dict(srcs=[shard], out=jax.ShapeDtypeStruct((N_DEV,) + shard.shape, shard.dtype), gather=True)


def _exchange_out(parts):
    rows = sum(p.shape[1] for p in parts)
    return jax.ShapeDtypeStruct((N_DEV, rows) + parts[0].shape[2:], parts[0].dtype)


def _exchange_rider(parts):
    return dict(srcs=list(parts), out=_exchange_out(parts), gather=False)


def _exchange_blocks(parts):
    n = len(parts)

    def body(*refs):
        _peer_copies(refs[:n], refs[n], refs[n + 1:], False, "start")
        _peer_copies(refs[:n], refs[n], refs[n + 1:], False, "finish")

    return pl.pallas_call(
        body, out_shape=_exchange_out(parts),
        in_specs=[HBM_SPEC] * n, out_specs=HBM_SPEC, scratch_shapes=_comm_scratch(), name="exchange_blocks",
    )(*parts)


BIG = [
    ("ffn1_w13", (D_MODEL, 2 * D_FF), 1), ("ffn1_w2", (D_FF, D_MODEL), 0),
    ("w_ssd_out", (SSD_DI, D_MODEL), 0), ("w_uq", (Q_LORA, MLA_H * QK), 1), ("w_ukv", (KV_LORA, MLA_H * (NOPE + VDIM)), 1),
    ("w_mla_out", (MLA_H * VDIM, D_MODEL), 0), ("w_o", (D_MODEL, D_MODEL), 0),
    ("ffn2_w13", (D_MODEL, 2 * D_FF), 1), ("ffn2_w2", (D_FF, D_MODEL), 0), ("w_in", (D_MODEL, D_IN), 1),
]
assert all(_r % 16 == 0 for _r in [_f[0] * _f[1] // N_DEV // PACK_COLS for _, _f, _ in BIG[:-1]])
SMALL = [
    ("ln_ffn1", D_MODEL), ("ln_mix", D_MODEL), ("conv_b", CONV_DIM), ("dt_bias", SSD_H), ("a_log", SSD_H), ("d_skip", SSD_H),
    ("ssd_norm", SSD_DI), ("q_lora_norm", Q_LORA), ("kv_lora_norm", KV_LORA), ("q_norm", QK), ("k_norm", QK), ("ln_ffn2", D_MODEL),
]


def _shard_shape(full, axis):
    k, n = full
    return (k // N_DEV, n) if axis == 0 else (k, n // N_DEV)


def _shard_rows(full):
    return full[0] * full[1] // N_DEV // PACK_COLS


LAYER_ROWS = sum(_shard_rows(f) for _, f, _ in BIG)
LAYER_ROWS_PAD = -(-LAYER_ROWS // 256) * 256


def _pack_shards(shards):
    parts = [(shards[name] if axis == 0 else shards[name].T).reshape(-1, PACK_COLS) for name, _, axis in BIG]
    pad = LAYER_ROWS_PAD - LAYER_ROWS
    if pad:
        parts.append(jnp.zeros((pad, PACK_COLS), parts[0].dtype))
    return jnp.concatenate(parts, axis=0)


BIG_BY_NAME = {name: (full, axis) for name, full, axis in BIG}
BIG_NAMES = [name for name, _, _ in BIG]
EARLY = ["ffn2_w13", "ffn2_w2", "w_o", "w_mla_out", "w_ssd_out"]
LATE = [name for name in BIG_NAMES if name not in EARLY]
SUM_ROWS = 128


def _part_rows(name):
    return -(-_shard_rows(BIG_BY_NAME[name][0]) // 16) * 16


def _grad_parts(grads, names):
    parts = []
    for name in names:
        part = grads[name].reshape(N_DEV, -1, PACK_COLS)
        parts.append(jnp.pad(part, ((0, 0), (0, _part_rows(name) - part.shape[1]), (0, 0))))
    return parts


def _pad_parts(parts):
    pad = -sum(p.shape[1] for p in parts) % SUM_ROWS
    return parts + ([jnp.zeros((N_DEV, pad, PACK_COLS), parts[0].dtype)] if pad else [])


def _unpack_parts(summed, names, r=0):
    out = {}
    for name in names:
        full, axis = BIG_BY_NAME[name]
        k, c = _shard_shape(full, axis)
        blk = summed[r:r + _shard_rows(full)]
        out[name] = blk.reshape(k, c) if axis == 0 else blk.reshape(c, k).T
        r += _part_rows(name)
    return out, r


def _working_shape(full, axis):
    return full if axis == 0 else full[::-1]


def _unpack_gathered(gathered):
    out, r = {}, 0
    for name, full, axis in BIG:
        n = _shard_rows(full)
        out[name] = gathered[:, r:r + n].reshape(_working_shape(full, axis))
        r += n
    return out


SMALL_COLS = sum(n for _, n in SMALL) + CONV_K * CONV_DIM
SMALL_ROWS = -(-(DEPTH * SMALL_COLS) // (8 * PACK_COLS)) * 8


def _pack_small(vals, conv_w):
    flat = jnp.concatenate([vals[name] for name, _ in SMALL] + [conv_w.reshape(DEPTH, -1)], axis=1).reshape(-1)
    flat = jnp.concatenate([flat, jnp.zeros((SMALL_ROWS * PACK_COLS - flat.shape[0],), F32)])
    return flat.reshape(SMALL_ROWS, PACK_COLS)


def _unpack_small(packed):
    flat = packed.reshape(-1)[:DEPTH * SMALL_COLS].reshape(DEPTH, SMALL_COLS)
    out, c = {}, 0
    for name, n in SMALL:
        out[name] = flat[:, c:c + n]
        c += n
    return out, flat[:, c:].reshape(DEPTH, CONV_K, CONV_DIM)


_IN_OFFS = [sum(IN_SPLIT[:i]) for i in range(len(IN_SPLIT) + 1)]


def _arrange_w_in(w_t):
    z, xbc, dt, cq, ckv, kr, gates = [w_t[_IN_OFFS[i]:_IN_OFFS[i + 1]] for i in range(len(IN_SPLIT))]
    pad = jnp.zeros((LANE - ROPE - SSD_H, w_t.shape[1]), w_t.dtype)
    return jnp.concatenate([z, gates, xbc, cq, ckv, kr, dt, pad], axis=0)


def _restore_w_in(g):
    z, gates, xbc = g[PROJ_Z:PROJ_GATES], g[PROJ_GATES:PROJ_XBC], g[PROJ_XBC:PROJ_CQ]
    cq, ckv = g[PROJ_CQ:PROJ_CKV], g[PROJ_CKV:PROJ_LAST]
    kr, dt = g[PROJ_LAST:PROJ_LAST + ROPE], g[PROJ_LAST + ROPE:PROJ_LAST + ROPE + SSD_H]
    return jnp.concatenate([z, xbc, dt, cq, ckv, kr, gates], axis=0)


def _pad_heads(w_t):
    k = w_t.shape[1]
    return jnp.pad(w_t.reshape(MLA_H, QK, k), ((0, 0), (0, HEAD_COLS - QK), (0, 0))).reshape(MLA_H * HEAD_COLS, k)


def _unpad_heads(g):
    k = g.shape[1]
    return g.reshape(MLA_H, HEAD_COLS, k)[:, :QK].reshape(MLA_H * QK, k)


def _row(v):
    return v.reshape(1, -1)


def _head_gain(g):
    return jnp.pad(g, (0, HEAD_COLS - QK)).reshape(1, HEAD_COLS)


def _ffn_fwd(h, ln, w13_t, w2, name):
    n = _row_fwd(_f_rmsnorm, [h], [_row(ln)], [BF16], name + "_fwd")[0]
    act, gate, up = _ffn_up_call(n, w13_t)
    return _mm(act, w2, alpha=0.5, res=h), (h, n, gate, up, act)


def _ffn_bwd(dh_out, saved, ln, w13_t, w2, name):
    h, n, gate, up, act = saved
    d_gate, d_up = _ffn_down_bwd_call(dh_out, w2, gate, up)
    d_w2 = _mm(act, dh_out, ta=True, out_dtype=BF16, alpha=0.5)
    d_n = _mm(d_gate, w13_t, b_rows=(0, D_FF))
    dh, d_ln = _mm(d_up, w13_t, b_rows=(D_FF, D_FF), res=d_n, norm_bwd=(h, _row(ln), dh_out))
    d_w13_t = jnp.concatenate([_mm(d_gate, n, ta=True, out_dtype=BF16), _mm(d_up, n, ta=True, out_dtype=BF16)], axis=0)
    return dh, d_w13_t, d_w2, d_ln[0]


def _mixer_fwd(h, big, small, conv_w, cs, sn, rider=None):
    s = h.shape[0]
    u = _row_fwd(_f_rmsnorm, [h], [_row(small["ln_mix"])], [BF16], "ln_mix_fwd")[0]
    proj = _mm(u, big["w_in"], tb=True)
    xbc = _conv_fwd_call(proj, PROJ_XBC, conv_w, _row(small["conv_b"]))
    dt_in = proj[:, PROJ_LAST + ROPE:PROJ_LAST + ROPE + SSD_H] + small["dt_bias"][None, :]
    dt = jax.nn.softplus(dt_in)
    a = -jnp.exp(small["a_log"])[None, :]
    y_scan, states = _ssd_fwd_call(xbc, dt, a)
    dsk = _row(jnp.repeat(small["d_skip"], SSD_P))
    gn_in = [y_scan, _win(xbc, 0, SSD_DI), _win(proj, PROJ_Z, SSD_DI)]
    yn = _row_fwd(_f_gated_norm, gn_in, [dsk, _row(small["ssd_norm"])], [BF16], "gated_norm_fwd")[0]
    y_ssd = _mm(yn, big["w_ssd_out"])
    qn = _row_fwd(_f_rmsnorm, [_win(proj, PROJ_CQ, Q_LORA)], [_row(small["q_lora_norm"])], [BF16], "q_lora_norm_fwd")[0]
    kvn = _row_fwd(_f_rmsnorm, [_win(proj, PROJ_CKV, KV_LORA)], [_row(small["kv_lora_norm"])], [BF16], "kv_lora_norm_fwd")[0]
    q = _mm(qn, big["w_uq"], tb=True)
    kv = _mm(kvn, big["w_ukv"], tb=True)
    qh, kh, vh = _heads_fwd_call(q, kv, proj, cs, sn, _head_gain(small["q_norm"]), _head_gain(small["k_norm"]))
    o, lse, *carried = _attn_fwd_call(qh, kh, vh, rider)
    o_rows = jnp.transpose(o, (1, 0, 2)).reshape(s, MLA_H * VDIM)
    y_mla = _mm(o_rows, big["w_mla_out"])
    out, mg = _merge_out_call(proj, y_ssd, y_mla, big["w_o"], h)
    saved = (h, u, proj, xbc, dt_in, dt, a, y_scan, states, dsk, yn, y_ssd, qn, kvn, q, kv, qh, kh, vh, o, lse, o_rows, y_mla, mg)
    return out, saved, (carried[0] if carried else None)


def _mixer_bwd(dh_out, saved, big, small, conv_w, cs, sn, carry_parts=None):
    (h, u, proj, xbc, dt_in, dt, a, y_scan, states, dsk, yn, y_ssd, qn, kvn, q, kv, qh, kh, vh, o, lse, o_rows, y_mla, mg) = saved
    s = h.shape[0]
    d_big, d_small = {}, {}
    d_gates, d_y_ssd, d_y_mla = _merge_out_bwd_call(dh_out, big["w_o"], proj, y_ssd, y_mla)
    d_big["w_o"] = _mm(mg, dh_out, ta=True, out_dtype=BF16)
    d_o_rows, delta = _attn_out_bwd_call(d_y_mla, big["w_mla_out"], o_rows)
    d_big["w_mla_out"] = _mm(o_rows, d_y_mla, ta=True, out_dtype=BF16)
    d_o = jnp.transpose(d_o_rows.reshape(s, MLA_H, VDIM), (1, 0, 2))
    d_yn = _mm(d_y_ssd, big["w_ssd_out"], tb=True, out_dtype=BF16)
    d_big["w_ssd_out"] = _mm(yn, d_y_ssd, ta=True, out_dtype=BF16)
    rider = None
    if carry_parts is not None:
        rider = _exchange_rider(_pad_parts(carry_parts + _grad_parts(d_big, EARLY[2:])))
    *d_heads, carried = list(_attn_bwd_call(qh, kh, vh, d_o, lse.reshape(MLA_H, 1, s), delta.T.reshape(MLA_H, 1, s), rider)) + ([None] if rider is None else [])
    d_q, d_kv, d_kr, d_qg, d_kg = _heads_bwd_call(
        q, kv, proj, cs, sn, _head_gain(small["q_norm"]), _head_gain(small["k_norm"]), *d_heads)
    d_small["q_norm"], d_small["k_norm"] = d_qg[0, :QK], d_kg[0, :QK]
    d_qn = _mm(d_q, big["w_uq"], out_dtype=BF16)
    d_big["w_uq"] = _mm(d_q, qn, ta=True, out_dtype=BF16)
    d_kvn = _mm(d_kv, big["w_ukv"], out_dtype=BF16)
    d_big["w_ukv"] = _mm(d_kv, kvn, ta=True, out_dtype=BF16)
    (d_cq,), (d_g,) = _row_bwd(_f_rmsnorm, [_win(proj, PROJ_CQ, Q_LORA)], [_row(small["q_lora_norm"])], [d_qn], [BF16], "q_lora_norm_bwd")
    d_small["q_lora_norm"] = d_g[0]
    (d_ckv,), (d_g,) = _row_bwd(_f_rmsnorm, [_win(proj, PROJ_CKV, KV_LORA)], [_row(small["kv_lora_norm"])], [d_kvn], [BF16], "kv_lora_norm_bwd")
    d_small["kv_lora_norm"] = d_g[0]
    gn_in = [y_scan, _win(xbc, 0, SSD_DI), _win(proj, PROJ_Z, SSD_DI)]
    (d_y_scan, d_xs, d_z), (d_dsk, d_g) = _row_bwd(
        _f_gated_norm, gn_in, [dsk, _row(small["ssd_norm"])], [d_yn], [F32, F32, BF16], "gated_norm_bwd")
    d_small["ssd_norm"] = d_g[0]
    d_small["d_skip"] = jnp.sum(d_dsk.reshape(SSD_H, SSD_P), axis=1)
    d_xbc_act, d_dt, d_a = _ssd_bwd_call(xbc, dt, a, states, d_y_scan, d_xs)
    d_xbc, d_conv_w, d_conv_b = _conv_bwd_call(proj, PROJ_XBC, conv_w, _row(small["conv_b"]), d_xbc_act)
    d_small["conv_b"] = d_conv_b[0]
    d_dt_in = d_dt * jax.nn.sigmoid(dt_in)
    d_small["dt_bias"] = jnp.sum(d_dt_in, axis=0)
    d_small["a_log"] = d_a[0] * a[0]
    d_last = (d_kr + jnp.pad(d_dt_in, ((0, 0), (ROPE, LANE - ROPE - SSD_H)))).astype(BF16)
    d_proj = jnp.concatenate([d_z, d_gates, d_xbc, d_cq, d_ckv, d_last], axis=1)
    dh, d_ln = _mm(d_proj, big["w_in"], norm_bwd=(h, _row(small["ln_mix"]), dh_out))
    d_big["w_in"] = _mm(d_proj, u, ta=True, out_dtype=BF16)
    d_small["ln_mix"] = d_ln[0]
    return dh, d_big, d_small, d_conv_w, carried


def _prepare_big(b):
    return dict(b, w_in=_arrange_w_in(b["w_in"]), w_uq=_pad_heads(b["w_uq"]))


def _local_step(x, positions, target, big, small, conv_w, packed_last=None):
    inv = 1.0 / (ROPE_THETA ** (jnp.arange(0, ROPE, 2, dtype=F32) / ROPE))
    ang = positions.astype(F32)[:, None] * inv
    cos, sin = jnp.cos(ang), jnp.sin(ang)
    no_lanes = jnp.zeros((x.shape[0], LANE - ROPE), F32)
    cs = jnp.concatenate([cos, cos, no_lanes], axis=1)
    sn = jnp.concatenate([-sin, sin, no_lanes], axis=1)
    carrier = DEPTH - 2 if packed_last is not None else None
    big = [None if b is None else _prepare_big(b) for b in big]
    layer_small = [{k: v[l] for k, v in small.items()} for l in range(DEPTH)]

    h, saved = x, []
    for l in range(DEPTH):
        b, sm = big[l], layer_small[l]
        h, s1 = _ffn_fwd(h, sm["ln_ffn1"], b["ffn1_w13"], b["ffn1_w2"], "ln_ffn1")
        h, s2, gathered = _mixer_fwd(h, b, sm, conv_w[l], cs, sn, _gather_rider(packed_last) if l == carrier else None)
        if gathered is not None:
            big[l + 1] = _prepare_big(_unpack_gathered(gathered))
        h, s3 = _ffn_fwd(h, sm["ln_ffn2"], b["ffn2_w13"], b["ffn2_w2"], "ln_ffn2")
        saved.append((s1, s2, s3))
    loss, dh = _loss_and_grad(h, target)

    d_big, d_small, d_conv_w = [None] * DEPTH, [None] * DEPTH, [None] * DEPTH
    for l in reversed(range(DEPTH)):
        b, sm = big[l], layer_small[l]
        s1, s2, s3 = saved[l]
        dh, d_w13_2, d_w2_2, d_ln2 = _ffn_bwd(dh, s3, sm["ln_ffn2"], b["ffn2_w13"], b["ffn2_w2"], "ln_ffn2")
        carry_parts = None
        if l == carrier:
            carry_parts = _grad_parts(d_big[l + 1], BIG_NAMES) + _grad_parts({"ffn2_w13": d_w13_2, "ffn2_w2": d_w2_2}, EARLY[:2])
        dh, db, ds, d_conv_w[l], received = _mixer_bwd(dh, s2, b, sm, conv_w[l], cs, sn, carry_parts)
        if received is not None:
            d_big[l + 1] = received
        dh, d_w13_1, d_w2_1, d_ln1 = _ffn_bwd(dh, s1, sm["ln_ffn1"], b["ffn1_w13"], b["ffn1_w2"], "ln_ffn1")
        db.update(ffn1_w13=d_w13_1, ffn1_w2=d_w2_1, ffn2_w13=d_w13_2, ffn2_w2=d_w2_2,
                  w_in=_restore_w_in(db["w_in"]), w_uq=_unpad_heads(db["w_uq"]))
        ds.update(ln_ffn1=d_ln1, ln_ffn2=d_ln2)
        d_big[l], d_small[l] = db, ds
    d_small = {name: jnp.stack([d_small[l][name] for l in range(DEPTH)]) for name, _ in SMALL}
    return loss, dh, d_big, d_small, jnp.stack(d_conv_w)


def _step(args):
    dev = 4 * lax.axis_index("x") + 2 * lax.axis_index("y") + lax.axis_index("c")
    x, positions, target = args["x"][0], args["positions"][0], args["loss_target"][0]

    packed = [_pack_shards({name: args[name][l].astype(BF16) for name, _, _ in BIG}) for l in range(DEPTH)]
    big = [_unpack_gathered(_all_gather(packed[l])) for l in range(DEPTH - 1)] + [None]
    cw = args["conv_w"]
    cw_cols = cw.shape[-1]
    cw_rows = -(-cw.size // (8 * PACK_COLS)) * 8
    cw_flat = jnp.concatenate([cw.reshape(-1), jnp.zeros((cw_rows * PACK_COLS - cw.size,), F32)]).reshape(cw_rows, PACK_COLS)
    cw_all = _all_gather(cw_flat).reshape(N_DEV, -1)[:, :cw.size].reshape(N_DEV, DEPTH, CONV_K, cw_cols)
    conv_w = jnp.transpose(cw_all, (1, 2, 0, 3)).reshape(DEPTH, CONV_K, CONV_DIM)
    small = {name: args[name] for name, _ in SMALL}

    loss, dx, d_big, d_small, d_conv_w = _local_step(x, positions, target, big, small, conv_w, packed_last=packed[-1])
    loss = lax.psum(loss, MESH_AXES)

    out = {"loss": loss, "grad_x": dx[None]}

    assert DEPTH == 2
    grads = {name: [None] * DEPTH for name in BIG_NAMES}
    summed = _sum_blocks(d_big[1])
    own, r = _unpack_parts(summed, BIG_NAMES)
    early, _ = _unpack_parts(summed, EARLY, r)
    late, _ = _unpack_parts(_sum_blocks(_exchange_blocks(_pad_parts(_grad_parts(d_big[0], LATE)))), LATE)
    for name in BIG_NAMES:
        grads[name] = [early[name] if name in EARLY else late[name], own[name]]
    flat = lambda t: t.reshape(-1, t.shape[-1])
    for name, _, _ in BIG:
        g = jnp.stack(grads[name])
        w = args[name]
        delta, m2, v2 = _adam(flat(w), flat(g), flat(args["m_" + name]), flat(args["v_" + name]))
        out["grad_" + name] = g
        out["delta_" + name] = delta.reshape(w.shape)
        out["new_m_" + name] = m2.reshape(w.shape)
        out["new_v_" + name] = v2.reshape(w.shape)

    total = _sum_blocks(_all_gather(_pack_small(d_small, d_conv_w)))
    g_conv_w = _unpack_small(total)[1]
    zeros_cw = jnp.zeros((DEPTH, CONV_K, CONV_DIM), F32)
    delta, m2, v2 = _adam(_pack_small(small, zeros_cw), total,
                          _pack_small({name: args["m_" + name] for name, _ in SMALL}, zeros_cw),
                          _pack_small({name: args["v_" + name] for name, _ in SMALL}, zeros_cw))
    for kind, packed in (("grad_", total), ("delta_", delta), ("new_m_", m2), ("new_v_", v2)):
        for name, val in _unpack_small(packed)[0].items():
            out[kind + name] = val
    g_cw = lax.dynamic_slice_in_dim(g_conv_w, dev * cw_cols, cw_cols, axis=2)
    delta, m2, v2 = _adam(flat(cw), flat(g_cw), flat(args["m_conv_w"]), flat(args["v_conv_w"]))
    out["grad_conv_w"] = g_cw
    out["delta_conv_w"] = delta.reshape(cw.shape)
    out["new_m_conv_w"] = m2.reshape(cw.shape)
    out["new_v_conv_w"] = v2.reshape(cw.shape)
    return out


WEIGHTS = ["ln_ffn1", "ffn1_w13", "ffn1_w2", "ln_mix", "w_in", "conv_w", "conv_b", "dt_bias", "a_log", "d_skip", "ssd_norm",
           "w_ssd_out", "q_lora_norm", "w_uq", "kv_lora_norm", "w_ukv", "q_norm", "k_norm", "w_mla_out", "w_o", "ln_ffn2",
           "ffn2_w13", "ffn2_w2"]
ARG_NAMES = (["x", "positions"] + WEIGHTS + ["loss_target"] + ["m_" + n for n in WEIGHTS] + ["v_" + n for n in WEIGHTS])


def kernel(x, positions, ln_ffn1, ffn1_w13, ffn1_w2, ln_mix, w_in, conv_w, conv_b, dt_bias, a_log, d_skip, ssd_norm, w_ssd_out, q_lora_norm, w_uq, kv_lora_norm, w_ukv, q_norm, k_norm, w_mla_out, w_o, ln_ffn2, ffn2_w13, ffn2_w2, loss_target, m_ln_ffn1, m_ffn1_w13, m_ffn1_w2, m_ln_mix, m_w_in, m_conv_w, m_conv_b, m_dt_bias, m_a_log, m_d_skip, m_ssd_norm, m_w_ssd_out, m_q_lora_norm, m_w_uq, m_kv_lora_norm, m_w_ukv, m_q_norm, m_k_norm, m_w_mla_out, m_w_o, m_ln_ffn2, m_ffn2_w13, m_ffn2_w2, v_ln_ffn1, v_ffn1_w13, v_ffn1_w2, v_ln_mix, v_w_in, v_conv_w, v_conv_b, v_dt_bias, v_a_log, v_d_skip, v_ssd_norm, v_w_ssd_out, v_q_lora_norm, v_w_uq, v_kv_lora_norm, v_w_ukv, v_q_norm, v_k_norm, v_w_mla_out, v_w_o, v_ln_ffn2, v_ffn2_w13, v_ffn2_w2):
    vals = (x, positions, ln_ffn1, ffn1_w13, ffn1_w2, ln_mix, w_in, conv_w, conv_b, dt_bias, a_log, d_skip, ssd_norm, w_ssd_out, q_lora_norm, w_uq, kv_lora_norm, w_ukv, q_norm, k_norm, w_mla_out, w_o, ln_ffn2, ffn2_w13, ffn2_w2, loss_target, m_ln_ffn1, m_ffn1_w13, m_ffn1_w2, m_ln_mix, m_w_in, m_conv_w, m_conv_b, m_dt_bias, m_a_log, m_d_skip, m_ssd_norm, m_w_ssd_out, m_q_lora_norm, m_w_uq, m_kv_lora_norm, m_w_ukv, m_q_norm, m_k_norm, m_w_mla_out, m_w_o, m_ln_ffn2, m_ffn2_w13, m_ffn2_w2, v_ln_ffn1, v_ffn1_w13, v_ffn1_w2, v_ln_mix, v_w_in, v_conv_w, v_conv_b, v_dt_bias, v_a_log, v_d_skip, v_ssd_norm, v_w_ssd_out, v_q_lora_norm, v_w_uq, v_kv_lora_norm, v_w_ukv, v_q_norm, v_k_norm, v_w_mla_out, v_w_o, v_ln_ffn2, v_ffn2_w13, v_ffn2_w2)
    out = _step(dict(zip(ARG_NAMES, vals)))
    order = ["loss", "grad_x"] + [k + n for k in ("grad_", "delta_", "new_m_", "new_v_") for n in WEIGHTS]
    return tuple(out[n] for n in order)
```
